```python
import jax, jax.numpy as jnp
from jax import lax
import numpy as np

D_MODEL = 1024
BATCH = 8
SEQ = 2048
DEPTH = 1
DEC_BATCH = 128
DEC_SEQ = 8
PAST_LEN = 8192
PAGE_SIZE = 128

SWA_HEADS = 8
SWA_KV_HEADS = 2
SWA_HEAD_DIM = 64
SWA_GROUP = SWA_HEADS // SWA_KV_HEADS
WINDOW = 128
SWA_BLOCK = 128
ATTN_SCALE = SWA_HEAD_DIM ** -0.5
GLA_HEADS = 4
GLA_DK = 64
GLA_DV = 128
GLA_GATE_RANK = 16
GLA_GATE_NORM = 16.0
GLA_CHUNK = 16
GLA_SCALE = GLA_DK ** -0.5
A_WIDTH = SWA_HEADS * SWA_HEAD_DIM
KV_WIDTH = SWA_KV_HEADS * SWA_HEAD_DIM
GLA_K_WIDTH = GLA_HEADS * GLA_DK
B_WIDTH = GLA_HEADS * GLA_DV
IN_SPLITS = (A_WIDTH, KV_WIDTH, KV_WIDTH, GLA_K_WIDTH, GLA_K_WIDTH, B_WIDTH, GLA_GATE_RANK, B_WIDTH, D_MODEL, D_MODEL)
IN_WIDTH = sum(IN_SPLITS)
N_EXPERTS = 32
TOP_K = 4
D_FF = D_MODEL
SWIGLU_ALPHA = 1.702
SWIGLU_LIMIT = 7.0
MOE_BLOCK = 128
NORM_EPS = 1e-5
QK_EPS = 1e-6
NEG_INF = -1e30

kernel_name = "hybrid_swa_sink_gla_moe_step"


def rmsnorm(x, g, eps):
    xf = x.astype(jnp.float32)
    y = xf * lax.rsqrt(jnp.mean(xf * xf, axis=-1, keepdims=True) + eps)
    return (y * g.astype(jnp.float32)).astype(x.dtype)


def mixer_inputs(x, norm1, w_in, q_norm, k_norm, w_gk, b_gk):
    lead = x.shape[:-1]
    u = rmsnorm(x, norm1, NORM_EPS) @ w_in
    offsets = np.cumsum(IN_SPLITS)[:-1].tolist()
    qa, ka, va, qb, kb, vb, g_low, r_b, g_a, g_b = jnp.split(u, offsets, axis=-1)
    qa = rmsnorm(qa.reshape(*lead, SWA_HEADS, SWA_HEAD_DIM), q_norm, QK_EPS)
    ka = rmsnorm(ka.reshape(*lead, SWA_KV_HEADS, SWA_HEAD_DIM), k_norm, QK_EPS)
    va = va.reshape(*lead, SWA_KV_HEADS, SWA_HEAD_DIM)
    qb = qb.reshape(*lead, GLA_HEADS, GLA_DK) * GLA_SCALE
    kb = kb.reshape(*lead, GLA_HEADS, GLA_DK)
    vb = vb.reshape(*lead, GLA_HEADS, GLA_DV)
    log_a = (jax.nn.log_sigmoid((g_low @ w_gk + b_gk).astype(jnp.float32)) / GLA_GATE_NORM).reshape(*lead, GLA_HEADS, GLA_DK)
    return qa, ka, va, qb, kb, vb, log_a, r_b, g_a, g_b


def attend_with_sinks(s, v, sinks, eq):
    sink = sinks.astype(jnp.float32).reshape(SWA_KV_HEADS, SWA_GROUP, 1, 1)
    sink = jnp.broadcast_to(sink, s.shape[:-1] + (1,))
    p = jax.nn.softmax(jnp.concatenate([s, sink], axis=-1), axis=-1)[..., :-1]
    return jnp.einsum(eq, p.astype(v.dtype), v)


def swa_prompt(q, k, v, sinks):
    bsz, L = q.shape[:2]
    nb = L // SWA_BLOCK
    qb = q.reshape(bsz, nb, SWA_BLOCK, SWA_KV_HEADS, SWA_GROUP, SWA_HEAD_DIM)

    def band(t):
        tb = t.reshape(bsz, nb, SWA_BLOCK, SWA_KV_HEADS, SWA_HEAD_DIM)
        prev = jnp.pad(tb, ((0, 0), (1, 0), (0, 0), (0, 0), (0, 0)))[:, :-1]
        return jnp.concatenate([prev, tb], axis=2)

    kb, vb = band(k), band(v)
    s = jnp.einsum('bnqkgd,bnckd->bnkgqc', qb, kb, preferred_element_type=jnp.float32) * ATTN_SCALE
    qi = jnp.arange(SWA_BLOCK)[:, None]
    ci = jnp.arange(2 * SWA_BLOCK)[None, :]
    dist = SWA_BLOCK + qi - ci
    band_mask = (dist >= 0) & (dist < WINDOW)
    key_pos = jnp.arange(nb)[:, None, None] * SWA_BLOCK + ci[None] - SWA_BLOCK
    mask = band_mask[None] & (key_pos >= 0)
    s = jnp.where(mask[None, :, None, None], s, NEG_INF)
    o = attend_with_sinks(s, vb, sinks, 'bnkgqc,bnckd->bnqkgd')
    return o.reshape(bsz, L, A_WIDTH)


def swa_sample(q, k, v, cache_k, cache_v, sinks):
    bsz, L = q.shape[:2]
    w = cache_k.shape[1]
    keys = jnp.concatenate([cache_k.astype(k.dtype), k], axis=1)
    vals = jnp.concatenate([cache_v.astype(v.dtype), v], axis=1)
    qg = q.reshape(bsz, L, SWA_KV_HEADS, SWA_GROUP, SWA_HEAD_DIM)
    s = jnp.einsum('bqkgd,bckd->bkgqc', qg, keys, preferred_element_type=jnp.float32) * ATTN_SCALE
    dist = w + jnp.arange(L)[:, None] - jnp.arange(w + L)[None, :]
    mask = (dist >= 0) & (dist < WINDOW)
    s = jnp.where(mask, s, NEG_INF)
    o = attend_with_sinks(s, vals, sinks, 'bkgqc,bckd->bqkgd').reshape(bsz, L, A_WIDTH)
    return o, keys[:, -w:], vals[:, -w:]


def gla_chunked(q, k, v, log_a, s0):
    out_dtype = v.dtype
    bsz, L, nh, dk = q.shape
    dv = v.shape[-1]
    c = GLA_CHUNK if L % GLA_CHUNK == 0 else L
    n = L // c

    def chunks(t):
        t = t.astype(jnp.float32)
        return jnp.moveaxis(t.reshape(bsz, n, c, *t.shape[2:]), 1, 0)

    causal = jnp.tril(jnp.ones((c, c), dtype=bool))[None, :, :, None, None]

    def step(s, inp):
        qc, kc, vc, gc = inp
        b = jnp.cumsum(gc, axis=1)
        o_inter = jnp.einsum('bthk,bhkv->bthv', qc * jnp.exp(b), s)
        rel = jnp.where(causal, b[:, :, None] - b[:, None, :], -jnp.inf)
        a = jnp.einsum('bthk,btshk,bshk->bhts', qc, jnp.exp(rel), kc)
        o_intra = jnp.einsum('bhts,bshv->bthv', a, vc)
        b_last = b[:, -1]
        s_new = jnp.exp(b_last)[..., None] * s + jnp.einsum('bshk,bshv->bhkv', kc * jnp.exp(b_last[:, None] - b), vc)
        return s_new, o_inter + o_intra

    s_final, o = lax.scan(step, s0.astype(jnp.float32), (chunks(q), chunks(k), chunks(v), chunks(log_a)))
    o = jnp.moveaxis(o, 0, 1).reshape(bsz, L, nh, dv)
    return o.astype(out_dtype), s_final


def expert_mlp(x, w1, b1, w2, b2):
    u = x @ w1 + b1
    x_glu = jnp.minimum(u[..., ::2], SWIGLU_LIMIT)
    x_lin = jnp.clip(u[..., 1::2], -SWIGLU_LIMIT, SWIGLU_LIMIT)
    return (x_glu * jax.nn.sigmoid(SWIGLU_ALPHA * x_glu) * (x_lin + 1.0)) @ w2 + b2


def moe(h, w_router, b_router, w1, b1, w2, b2):
    lead = h.shape[:-1]
    x2 = h.reshape(-1, D_MODEL)
    T = x2.shape[0]
    logits = jnp.dot(x2, w_router, preferred_element_type=jnp.float32) + b_router.astype(jnp.float32)
    top_v, top_e = lax.top_k(logits, TOP_K)
    gate = jax.nn.softmax(top_v, axis=-1)
    flat_e = top_e.reshape(-1).astype(jnp.int32)
    flat_t = jnp.repeat(jnp.arange(T, dtype=jnp.int32), TOP_K)
    flat_g = gate.reshape(-1)
    order = jnp.argsort(flat_e)
    se = flat_e[order]
    counts = jnp.bincount(flat_e, length=N_EXPERTS).astype(jnp.int32)
    padded = (counts + MOE_BLOCK - 1) // MOE_BLOCK * MOE_BLOCK
    pad_end = jnp.cumsum(padded)
    pad_start = pad_end - padded
    start = jnp.cumsum(counts) - counts
    dest = pad_start[se] + jnp.arange(T * TOP_K, dtype=jnp.int32) - start[se]
    n_rows = -(-(T * TOP_K + N_EXPERTS * (MOE_BLOCK - 1)) // MOE_BLOCK) * MOE_BLOCK
    n_blocks = n_rows // MOE_BLOCK
    row_tok = jnp.full((n_rows,), T, jnp.int32).at[dest].set(flat_t[order])
    row_g = jnp.zeros((n_rows,), jnp.float32).at[dest].set(flat_g[order])
    blk_e = jnp.minimum(jnp.searchsorted(pad_end, jnp.arange(n_blocks, dtype=jnp.int32) * MOE_BLOCK, side='right'), N_EXPERTS - 1)
    x_pad = jnp.concatenate([x2, jnp.zeros((1, D_MODEL), x2.dtype)], axis=0)
    xb = x_pad[row_tok].reshape(n_blocks, MOE_BLOCK, D_MODEL)

    def expert_block(args):
        xe, e = args
        return expert_mlp(xe, w1[e], b1[e], w2[e], b2[e])

    yb = lax.map(expert_block, (xb, blk_e)).reshape(n_rows, D_MODEL)
    y = jax.ops.segment_sum(yb.astype(jnp.float32) * row_g[:, None], row_tok, num_segments=T + 1)[:T]
    return y.astype(h.dtype).reshape(*lead, D_MODEL)


def block_output(x, oa, ob, r_b, g_a, g_b, gla_norm, w_a, w_b, w_o, norm2, w_router, b_router, w1, b1, w2, b2):
    ob = rmsnorm(ob, gla_norm, NORM_EPS).reshape(*x.shape[:-1], B_WIDTH) * jax.nn.silu(r_b)
    merged = jax.nn.sigmoid(g_a) * (oa @ w_a) + jax.nn.sigmoid(g_b) * (ob @ w_b)
    h = x + merged @ w_o
    return h + moe(rmsnorm(h, norm2, NORM_EPS), w_router, b_router, w1, b1, w2, b2)


def decoder_layer(x_p, x_s, cache_k, cache_v, s_gla, norm1, w_in, q_norm, k_norm, sinks, w_gk, b_gk,
                  gla_norm, w_a, w_b, w_o, norm2, w_router, b_router, w1, b1, w2, b2):
    mix_w = (norm1, w_in, q_norm, k_norm, w_gk, b_gk)
    out_w = (gla_norm, w_a, w_b, w_o, norm2, w_router, b_router, w1, b1, w2, b2)
    qa, ka, va, qb, kb, vb, la, rb, ga, gb = mixer_inputs(x_p, *mix_w)
    oa = swa_prompt(qa, ka, va, sinks)
    s0 = jnp.zeros((x_p.shape[0], GLA_HEADS, GLA_DK, GLA_DV), jnp.float32)
    ob, s_p = gla_chunked(qb, kb, vb, la, s0)
    y_p = block_output(x_p, oa, ob, rb, ga, gb, *out_w)
    k_p, v_p = ka[:, -WINDOW:], va[:, -WINDOW:]
    qa, ka, va, qb, kb, vb, la, rb, ga, gb = mixer_inputs(x_s, *mix_w)
    oa, k_s, v_s = swa_sample(qa, ka, va, cache_k, cache_v, sinks)
    ob, s_s = gla_chunked(qb, kb, vb, la, s_gla)
    y_s = block_output(x_s, oa, ob, rb, ga, gb, *out_w)
    return y_p, y_s, (k_p, v_p, s_p, k_s, v_s, s_s)


def setup_inputs(seed: int = 0) -> dict:
    key = jax.random.key(seed)
    ks = jax.random.split(key, 32)

    def nrm(k, shape, scale):
        return jax.random.normal(k, shape, jnp.float32) * scale

    win = min(WINDOW, PAST_LEN)
    return {
        "x_prompt": nrm(ks[0], (BATCH, SEQ, D_MODEL), 1.0),
        "x_sample": nrm(ks[1], (DEC_BATCH, DEC_SEQ, D_MODEL), 1.0),
        "cache_swa_k": nrm(ks[2], (DEPTH, DEC_BATCH, win, SWA_KV_HEADS, SWA_HEAD_DIM), 1.0),
        "cache_swa_v": nrm(ks[3], (DEPTH, DEC_BATCH, win, SWA_KV_HEADS, SWA_HEAD_DIM), 1.0),
        "state_gla": nrm(ks[4], (DEPTH, DEC_BATCH, GLA_HEADS, GLA_DK, GLA_DV), 1.0),
        "norm1": 1.0 + nrm(ks[5], (DEPTH, D_MODEL), 0.01),
        "w_in": nrm(ks[6], (DEPTH, D_MODEL, IN_WIDTH), D_MODEL ** -0.5),
        "q_norm": 1.0 + nrm(ks[7], (DEPTH, SWA_HEAD_DIM), 0.01),
        "k_norm": 1.0 + nrm(ks[8], (DEPTH, SWA_HEAD_DIM), 0.01),
        "sinks": nrm(ks[9], (DEPTH, SWA_HEADS), 0.5),
        "w_gk": nrm(ks[10], (DEPTH, GLA_GATE_RANK, GLA_K_WIDTH), GLA_GATE_RANK ** -0.5),
        "b_gk": nrm(ks[11], (DEPTH, GLA_K_WIDTH), 0.01),
        "gla_norm": 1.0 + nrm(ks[12], (DEPTH, GLA_DV), 0.01),
        "w_a": nrm(ks[13], (DEPTH, A_WIDTH, D_MODEL), A_WIDTH ** -0.5),
        "w_b": nrm(ks[14], (DEPTH, B_WIDTH, D_MODEL), B_WIDTH ** -0.5),
        "w_o": nrm(ks[15], (DEPTH, D_MODEL, D_MODEL), D_MODEL ** -0.5),
        "norm2": 1.0 + nrm(ks[16], (DEPTH, D_MODEL), 0.01),
        "w_router": nrm(ks[17], (DEPTH, D_MODEL, N_EXPERTS), D_MODEL ** -0.5),
        "b_router": nrm(ks[18], (DEPTH, N_EXPERTS), 0.01),
        "w1": nrm(ks[19], (DEPTH, N_EXPERTS, D_MODEL, 2 * D_FF), D_MODEL ** -0.5),
        "b1": nrm(ks[20], (DEPTH, N_EXPERTS, 2 * D_FF), 0.01),
        "w2": nrm(ks[21], (DEPTH, N_EXPERTS, D_FF, D_MODEL), D_FF ** -0.5),
        "b2": nrm(ks[22], (DEPTH, N_EXPERTS, D_MODEL), 0.01),
    }


def reference(x_prompt, x_sample, cache_swa_k, cache_swa_v, state_gla, norm1, w_in, q_norm, k_norm, sinks,
              w_gk, b_gk, gla_norm, w_a, w_b, w_o, norm2, w_router, b_router, w1, b1, w2, b2):
    y_p, y_s = x_prompt, x_sample
    states = []
    for l in range(DEPTH):
        y_p, y_s, st = decoder_layer(
            y_p, y_s, cache_swa_k[l], cache_swa_v[l], state_gla[l],
            norm1[l], w_in[l], q_norm[l], k_norm[l], sinks[l], w_gk[l], b_gk[l],
            gla_norm[l], w_a[l], w_b[l], w_o[l], norm2[l], w_router[l], b_router[l],
            w1[l], b1[l], w2[l], b2[l])
        states.append(st)
    swa_k_prompt = jnp.stack([st[0] for st in states])
    swa_v_prompt = jnp.stack([st[1] for st in states])
    gla_state_prompt = jnp.stack([st[2] for st in states])
    swa_k_sample = jnp.stack([st[3] for st in states])
    swa_v_sample = jnp.stack([st[4] for st in states])
    gla_state_sample = jnp.stack([st[5] for st in states])
    return (y_p, y_s, swa_k_prompt, swa_v_prompt, gla_state_prompt, swa_k_sample, swa_v_sample, gla_state_sample)
```

```python
import functools

import jax
import jax.numpy as jnp
import numpy as np
from jax import lax
from jax.experimental import pallas as pl
from jax.experimental.pallas import tpu as pltpu

F32 = jnp.float32
BF16 = jnp.bfloat16
I32 = jnp.int32
U32 = jnp.uint32

D_MODEL = 1024
SWA_HEADS = 8
SWA_KV_HEADS = 2
SWA_GROUP = SWA_HEADS // SWA_KV_HEADS
SWA_HEAD_DIM = 64
WINDOW = 128
ATTN_SCALE = SWA_HEAD_DIM ** -0.5
GLA_HEADS = 4
GLA_DK = 64
GLA_DV = 128
GLA_GATE_RANK = 16
GLA_GATE_NORM = 16.0
GLA_SCALE = GLA_DK ** -0.5
A_WIDTH = SWA_HEADS * SWA_HEAD_DIM
KV_WIDTH = SWA_KV_HEADS * SWA_HEAD_DIM
GLA_K_WIDTH = GLA_HEADS * GLA_DK
B_WIDTH = GLA_HEADS * GLA_DV
N_EXPERTS = 32
TOP_K = 4
D_FF = D_MODEL
SWIGLU_ALPHA = 1.702
SWIGLU_LIMIT = 7.0
NORM_EPS = 1e-5
QK_EPS = 1e-6
NEG_INF = -1e30

LANE = 128
GL_PAD = LANE
VMEM_LIMIT = 56 * 1024 * 1024

_SEGS = (("qa", A_WIDTH), ("ka", KV_WIDTH), ("va", KV_WIDTH), ("qb", GLA_K_WIDTH), ("kb", GLA_K_WIDTH),
         ("vb", B_WIDTH), ("rb", B_WIDTH), ("ga", D_MODEL), ("gb", D_MODEL), ("gl", GL_PAD))
_OFF = {}
_o = 0
for _n, _w in _SEGS:
    _OFF[_n] = (_o, _w)
    _o += _w
IN_PACKED = _o

TOK_TILE = 512
GLA_CHUNK = 64
GLA_SUB = 16
MOE_BM = 256
DISP_TILE = 256
COMB_TILE = 128


def _cparams(sem):
    return pltpu.CompilerParams(dimension_semantics=sem, vmem_limit_bytes=VMEM_LIMIT)


def _split_bf16(v):
    hi = v.astype(BF16)
    lo = (v - hi.astype(F32)).astype(BF16)
    return hi, lo


def _dot(a, b):
    return jnp.dot(a, b, preferred_element_type=F32)


def _dot_nt(a, b):
    return lax.dot_general(a, b, (((1,), (1,)), ((), ())), preferred_element_type=F32)


def _dot_tn(a, b):
    return lax.dot_general(a, b, (((0,), (0,)), ((), ())), preferred_element_type=F32)


def _inproj_kernel(x_ref, n1_ref, w_ref, gq_ref, gk_ref, qn_ref, kn_ref, wgk_ref, bgk_ref,
                   qa_ref, ka_ref, va_ref, qb_ref, kb_ref, vb_ref, la_ref, rb_ref, ga_ref, gb_ref):
    x = x_ref[...]
    ms = jnp.mean(x * x, axis=-1, keepdims=True)
    xn = (x * lax.rsqrt(ms + NORM_EPS) * n1_ref[...]).astype(BF16)

    def proj(name):
        off, width = _OFF[name]
        return _dot(xn, w_ref[:, off:off + width])

    def head_norm(v, ones_ref, gain_ref):
        hi, lo = _split_bf16(v * v)
        ss = _dot(hi, ones_ref[...]) + _dot(lo, ones_ref[...])
        return v * lax.rsqrt(ss * (1.0 / SWA_HEAD_DIM) + QK_EPS) * gain_ref[...]

    qa_ref[...] = (head_norm(proj("qa"), gq_ref, qn_ref) * ATTN_SCALE).astype(BF16)
    ka_ref[...] = head_norm(proj("ka"), gk_ref, kn_ref)
    va_ref[...] = proj("va")
    qb_ref[...] = (proj("qb") * GLA_SCALE).astype(BF16)
    kb_ref[...] = proj("kb").astype(BF16)
    vb_ref[...] = proj("vb").astype(BF16)
    z = _dot(proj("gl").astype(BF16), wgk_ref[...]) + bgk_ref[...]
    la_ref[...] = (jnp.minimum(z, 0.0) - jnp.log1p(jnp.exp(-jnp.abs(z)))) * (1.0 / GLA_GATE_NORM)
    rb = proj("rb")
    rb_ref[...] = (rb * jax.nn.sigmoid(rb)).astype(BF16)
    ga_ref[...] = jax.nn.sigmoid(proj("ga")).astype(BF16)
    gb_ref[...] = jax.nn.sigmoid(proj("gb")).astype(BF16)


def _inproj(x2, n1, w_packed, gq, gk, qn, kn, wgk, bgk):
    T = x2.shape[0]
    tm = min(TOK_TILE, T)
    assert T % tm == 0

    def tok(width):
        return pl.BlockSpec((tm, width), lambda i: (i, 0))

    def const(shape):
        return pl.BlockSpec(shape, lambda i: (0, 0))

    outs = (("qa", A_WIDTH, BF16), ("ka", KV_WIDTH, F32), ("va", KV_WIDTH, F32), ("qb", GLA_K_WIDTH, BF16),
            ("kb", GLA_K_WIDTH, BF16), ("vb", B_WIDTH, BF16), ("la", GLA_K_WIDTH, F32), ("rb", B_WIDTH, BF16),
            ("ga", D_MODEL, BF16), ("gb", D_MODEL, BF16))
    return pl.pallas_call(
        _inproj_kernel,
        grid=(T // tm,),
        in_specs=[tok(D_MODEL), const((1, D_MODEL)), const((D_MODEL, IN_PACKED)), const((A_WIDTH, A_WIDTH)),
                  const((KV_WIDTH, KV_WIDTH)), const((1, A_WIDTH)), const((1, KV_WIDTH)),
                  const((GL_PAD, GLA_K_WIDTH)), const((1, GLA_K_WIDTH))],
        out_specs=[tok(w) for _, w, _ in outs],
        out_shape=[jax.ShapeDtypeStruct((T, w), dt) for _, w, dt in outs],
        compiler_params=_cparams(("parallel",)),
        name="inproj",
    )(x2, n1, w_packed, gq, gk, qn, kn, wgk, bgk)


def _swa_kernel(sink_ref, q_ref, kc_ref, vc_ref, kp_ref, vp_ref, o_ref, *, first_block_has_past):
    n = pl.program_id(1)
    q = q_ref[0]
    lq = q.shape[0]
    k = jnp.concatenate([kp_ref[0], kc_ref[0]], axis=0).astype(BF16)
    v = jnp.concatenate([vp_ref[0], vc_ref[0]], axis=0).astype(BF16)
    qi = lax.broadcasted_iota(I32, (lq, WINDOW + lq), 0)
    ci = lax.broadcasted_iota(I32, (lq, WINDOW + lq), 1)
    dist = WINDOW + qi - ci
    mask = (dist >= 0) & (dist < WINDOW)
    if not first_block_has_past:
        mask = mask & ((ci >= WINDOW) | (n > 0))
    outs = []
    for kh in range(SWA_KV_HEADS):
        kk = k[:, kh * SWA_HEAD_DIM:(kh + 1) * SWA_HEAD_DIM]
        vv = v[:, kh * SWA_HEAD_DIM:(kh + 1) * SWA_HEAD_DIM]
        for g in range(SWA_GROUP):
            h = kh * SWA_GROUP + g
            s = _dot_nt(q[:, h * SWA_HEAD_DIM:(h + 1) * SWA_HEAD_DIM], kk)
            s = jnp.where(mask, s, NEG_INF)
            sink = sink_ref[h]
            m = jnp.maximum(jnp.max(s, axis=-1, keepdims=True), sink)
            p = jnp.exp(s - m)
            denom = jnp.sum(p, axis=-1, keepdims=True) + jnp.exp(sink - m)
            outs.append(_dot(p.astype(BF16), vv) / denom)
    o_ref[0] = jnp.concatenate(outs, axis=-1).astype(BF16)


def _swa(sinks, qa, ka, va, k_past, v_past, *, has_past):
    B, L, _ = qa.shape
    lq = min(L, WINDOW)
    nb = L // lq
    if has_past:
        assert nb == 1
        past_spec = pl.BlockSpec((1, WINDOW, KV_WIDTH), lambda b, n, s: (b, 0, 0))
        kp, vp = k_past, v_past
    else:
        assert lq == WINDOW
        past_spec = pl.BlockSpec((1, WINDOW, KV_WIDTH), lambda b, n, s: (b, jnp.maximum(n - 1, 0), 0))
        kp, vp = ka, va
    cur = lambda w: pl.BlockSpec((1, lq, w), lambda b, n, s: (b, n, 0))
    return pl.pallas_call(
        functools.partial(_swa_kernel, first_block_has_past=has_past),
        grid_spec=pltpu.PrefetchScalarGridSpec(
            num_scalar_prefetch=1,
            grid=(B, nb),
            in_specs=[cur(A_WIDTH), cur(KV_WIDTH), cur(KV_WIDTH), past_spec, past_spec],
            out_specs=cur(A_WIDTH),
        ),
        out_shape=jax.ShapeDtypeStruct((B, L, A_WIDTH), BF16),
        compiler_params=_cparams(("parallel", "parallel")),
        name="swa",
    )(sinks, qa, ka, va, kp, vp)


def _gla_kernel(q_ref, k_ref, v_ref, la_ref, rb_ref, gn_ref, s0_ref, o_ref, sout_ref, s_scr, *, chunk, sub, length):
    C, SUB = chunk, sub
    S = C // SUB
    W = GLA_K_WIDTH
    s_scr[...] = s0_ref[0]
    row = lax.broadcasted_iota(I32, (C, C), 0)
    col = lax.broadcasted_iota(I32, (C, C), 1)
    tri = (row >= col).astype(BF16)
    diag_mask = (row >= col) & ((row // SUB) == (col // SUB))
    eye = lax.broadcasted_iota(I32, (GLA_DK, GLA_DK), 0) == lax.broadcasted_iota(I32, (GLA_DK, GLA_DK), 1)
    krow = lax.broadcasted_iota(I32, (C, W), 0)

    def rows_of(x, r):
        return jnp.broadcast_to(x[r:r + 1, :], (SUB, W))

    def body(c, carry):
        r0 = pl.multiple_of(c * C, C)
        q = q_ref[0, pl.ds(r0, C), :].astype(F32)
        k = k_ref[0, pl.ds(r0, C), :].astype(F32)
        v = v_ref[0, pl.ds(r0, C), :]
        g = la_ref[0, pl.ds(r0, C), :]
        g_hi, g_lo = _split_bf16(g)
        b = _dot(tri, g_hi) + _dot(tri, g_lo)
        b_last = b[C - 1:C, :]
        mid = jnp.concatenate([rows_of(b, i * SUB + SUB // 2 - 1) for i in range(S)], axis=0)
        beta = jnp.concatenate([jnp.zeros((SUB, W), F32)] + [rows_of(b, i * SUB - 1) for i in range(1, S)], axis=0)
        q_inter = (q * jnp.exp(b)).astype(BF16)
        k_state = (k * jnp.exp(b_last - b)).astype(BF16)
        q_diag = (q * jnp.exp(b - mid)).astype(BF16)
        k_diag = (k * jnp.exp(mid - b)).astype(BF16)
        q_off = (q * jnp.exp(b - beta)).astype(BF16)
        k_off = [None]
        for i in range(1, S):
            e = jnp.exp(jnp.minimum(b[i * SUB - 1:i * SUB, :] - b, 0.0))
            k_off.append(jnp.where(krow < i * SUB, k * e, 0.0).astype(BF16))
        outs = []
        for h in range(GLA_HEADS):
            ks = slice(h * GLA_DK, (h + 1) * GLA_DK)
            vs = slice(h * GLA_DV, (h + 1) * GLA_DV)
            a_diag = _dot_nt(q_diag[:, ks], k_diag[:, ks])
            if S > 1:
                blocks = [jnp.zeros((SUB, C), F32)]
                for i in range(1, S):
                    blocks.append(_dot_nt(q_off[i * SUB:(i + 1) * SUB, ks], k_off[i][:, ks]))
                a = jnp.where(diag_mask, a_diag, jnp.concatenate(blocks, axis=0))
            else:
                a = jnp.where(diag_mask, a_diag, 0.0)
            s_h = s_scr[h]
            v_h = v[:, vs]
            o_h = _dot(a.astype(BF16), v_h) + _dot(q_inter[:, ks], s_h.astype(BF16))
            dec = jnp.exp(b_last[:, ks])
            dec_col = jnp.sum(jnp.where(eye, jnp.broadcast_to(dec, (GLA_DK, GLA_DK)), 0.0), axis=-1, keepdims=True)
            s_scr[h] = dec_col * s_h + _dot_tn(k_state[:, ks], v_h)
            ms = jnp.mean(o_h * o_h, axis=-1, keepdims=True)
            outs.append(o_h * lax.rsqrt(ms + NORM_EPS) * gn_ref[...])
        o = jnp.concatenate(outs, axis=-1) * rb_ref[0, pl.ds(r0, C), :].astype(F32)
        o_ref[0, pl.ds(r0, C), :] = o.astype(BF16)
        return carry

    lax.fori_loop(0, length // C, body, 0)
    sout_ref[0] = s_scr[...]


def _gla(qb, kb, vb, la, rb, gn, s0):
    B, L, _ = qb.shape
    chunk = GLA_CHUNK if L % GLA_CHUNK == 0 else L
    sub = GLA_SUB if chunk % GLA_SUB == 0 else chunk
    seq = lambda w: pl.BlockSpec((1, L, w), lambda b: (b, 0, 0))
    st = pl.BlockSpec((1, GLA_HEADS, GLA_DK, GLA_DV), lambda b: (b, 0, 0, 0))
    return pl.pallas_call(
        functools.partial(_gla_kernel, chunk=chunk, sub=sub, length=L),
        grid=(B,),
        in_specs=[seq(GLA_K_WIDTH), seq(GLA_K_WIDTH), seq(B_WIDTH), seq(GLA_K_WIDTH), seq(B_WIDTH),
                  pl.BlockSpec((1, GLA_DV), lambda b: (0, 0)), st],
        out_specs=[seq(B_WIDTH), st],
        out_shape=[jax.ShapeDtypeStruct((B, L, B_WIDTH), BF16),
                   jax.ShapeDtypeStruct((B, GLA_HEADS, GLA_DK, GLA_DV), F32)],
        scratch_shapes=[pltpu.VMEM((GLA_HEADS, GLA_DK, GLA_DV), F32)],
        compiler_params=_cparams(("parallel",)),
        name="gla",
    )(qb, kb, vb, la, rb, gn, s0)


def _outproj_kernel(x_ref, oa_ref, ob_ref, ga_ref, gb_ref, wa_ref, wb_ref, wo_ref, n2_ref, wr_ref, br_ref, upper_ref,
                    h_ref, hp_ref, e_ref, g_ref, rank_ref, cnt_ref, carry):
    i = pl.program_id(0)

    @pl.when(i == 0)
    def _():
        carry[...] = jnp.zeros_like(carry)

    tm = x_ref.shape[0]
    merged = (ga_ref[...].astype(F32) * _dot(oa_ref[...], wa_ref[...])
              + gb_ref[...].astype(F32) * _dot(ob_ref[...], wb_ref[...]))
    h = x_ref[...] + _dot(merged.astype(BF16), wo_ref[...])
    h_ref[...] = h
    ms = jnp.mean(h * h, axis=-1, keepdims=True)
    hn = h * lax.rsqrt(ms + NORM_EPS) * n2_ref[...]
    half = D_MODEL // 2
    lo_bits = pltpu.bitcast(hn[:, :half].astype(BF16).astype(F32), U32) >> 16
    hi_bits = pltpu.bitcast(hn[:, half:].astype(BF16).astype(F32), U32) & jnp.uint32(0xFFFF0000)
    hp_ref[...] = lo_bits | hi_bits
    hn_hi, hn_lo = _split_bf16(hn)
    wr_hi, wr_lo = _split_bf16(wr_ref[...])
    logits = _dot(hn_hi, wr_hi) + _dot(hn_lo, wr_hi) + _dot(hn_hi, wr_lo) + br_ref[...]
    lt = logits.T[:N_EXPERTS, :]
    eid = lax.broadcasted_iota(I32, (N_EXPERTS, tm), 0)
    vals, hots = [], []
    for k in range(TOP_K):
        m = jnp.max(lt, axis=0, keepdims=True)
        idx = jnp.min(jnp.where(lt == m, eid, N_EXPERTS), axis=0, keepdims=True)
        hot = eid == idx
        lt = jnp.where(hot, -jnp.inf, lt)
        vals.append(m)
        hots.append(hot)
        e_ref[k:k + 1, :] = idx
    ex = [jnp.exp(vk - vals[0]) for vk in vals]
    den = ex[0] + ex[1] + ex[2] + ex[3]
    for k in range(TOP_K):
        g_ref[k:k + 1, :] = ex[k] / den
    multi = (hots[0] | hots[1] | hots[2] | hots[3])
    before = _dot(multi.astype(BF16), upper_ref[...]) + carry[...]
    for k in range(TOP_K):
        rank_ref[k:k + 1, :] = jnp.sum(jnp.where(hots[k], before, 0.0), axis=0, keepdims=True).astype(I32)
    carry[...] = carry[...] + jnp.sum(multi.astype(F32), axis=1, keepdims=True)
    cnt_ref[...] = jnp.broadcast_to(carry[...], cnt_ref.shape).astype(I32)


def _outproj(x2, oa, ob, ga, gb, wa, wb, wo, n2, wr, br, upper):
    T = x2.shape[0]
    tm = min(TOK_TILE, T)
    assert T % tm == 0 and upper.shape == (tm, tm)
    tok = lambda w: pl.BlockSpec((tm, w), lambda i: (i, 0))
    const = lambda shape: pl.BlockSpec(shape, lambda i: (0, 0))
    kt = pl.BlockSpec((TOP_K, tm), lambda i: (0, i))
    return pl.pallas_call(
        _outproj_kernel,
        grid=(T // tm,),
        in_specs=[tok(D_MODEL), tok(A_WIDTH), tok(B_WIDTH), tok(D_MODEL), tok(D_MODEL),
                  const((A_WIDTH, D_MODEL)), const((B_WIDTH, D_MODEL)), const((D_MODEL, D_MODEL)),
                  const((1, D_MODEL)), const((D_MODEL, LANE)), const((1, LANE)), const((tm, tm))],
        out_specs=[tok(D_MODEL), tok(D_MODEL // 2), kt, kt, kt, const((N_EXPERTS, LANE))],
        out_shape=[jax.ShapeDtypeStruct((T, D_MODEL), F32), jax.ShapeDtypeStruct((T, D_MODEL // 2), U32),
                   jax.ShapeDtypeStruct((TOP_K, T), I32), jax.ShapeDtypeStruct((TOP_K, T), F32),
                   jax.ShapeDtypeStruct((TOP_K, T), I32), jax.ShapeDtypeStruct((N_EXPERTS, LANE), I32)],
        scratch_shapes=[pltpu.VMEM((N_EXPERTS, 1), F32)],
        compiler_params=_cparams(("arbitrary",)),
        name="outproj",
    )(x2, oa, ob, ga, gb, wa, wb, wo, n2, wr, br, upper)


def _dispatch_kernel(pos_ref, hp_ref, xs_in_ref, xs_ref, sem):
    del xs_in_ref
    tm = hp_ref.shape[0]

    def row_copy(t, k):
        return pltpu.make_async_copy(hp_ref.at[pl.ds(t, 1), :], xs_ref.at[pl.ds(pos_ref[k, t], 1), :], sem)

    def start(t, c):
        for k in range(TOP_K):
            row_copy(t, k).start()
        return c

    def wait(t, c):
        for k in range(TOP_K):
            row_copy(t, k).wait()
        return c

    lax.fori_loop(0, tm, start, 0)
    lax.fori_loop(0, tm, wait, 0)


def _dispatch(pos, hp, xs_zero):
    T = hp.shape[0]
    tm = DISP_TILE
    assert T % tm == 0
    return pl.pallas_call(
        _dispatch_kernel,
        grid=(T // tm,),
        in_specs=[pl.BlockSpec((TOP_K, tm), lambda i: (0, i), memory_space=pltpu.SMEM),
                  pl.BlockSpec((tm, D_MODEL // 2), lambda i: (i, 0)),
                  pl.BlockSpec(memory_space=pl.ANY)],
        out_specs=pl.BlockSpec(memory_space=pl.ANY),
        out_shape=jax.ShapeDtypeStruct(xs_zero.shape, U32),
        scratch_shapes=[pltpu.SemaphoreType.DMA],
        input_output_aliases={2: 0},
        compiler_params=_cparams(("arbitrary",)),
        name="dispatch",
    )(pos, hp, xs_zero)


def _experts_kernel(be_ref, nb_ref, xs_ref, w1g_ref, w1l_ref, b1g_ref, b1l_ref, w2_ref, b2_ref, y_ref):
    i = pl.program_id(0)

    @pl.when(i >= nb_ref[0])
    def _():
        y_ref[...] = jnp.zeros_like(y_ref)

    @pl.when(i < nb_ref[0])
    def _():
        half = D_MODEL // 2
        w = xs_ref[...]
        x_lo = pltpu.bitcast(w << 16, F32).astype(BF16)
        x_hi = pltpu.bitcast(w & jnp.uint32(0xFFFF0000), F32).astype(BF16)

        def up(w_ref, b_ref):
            return _dot(x_lo, w_ref[0, :half, :]) + _dot(x_hi, w_ref[0, half:, :]) + b_ref[0]

        x_glu = jnp.minimum(up(w1g_ref, b1g_ref), SWIGLU_LIMIT)
        x_lin = jnp.clip(up(w1l_ref, b1l_ref), -SWIGLU_LIMIT, SWIGLU_LIMIT)
        act = x_glu * jax.nn.sigmoid(SWIGLU_ALPHA * x_glu) * (x_lin + 1.0)
        y_ref[...] = _dot(act.astype(BF16), w2_ref[0]) + b2_ref[0]


def _experts(blk_e, n_used, xs, w1g, w1l, b1g, b1l, w2, b2):
    n_rows = xs.shape[0]
    nblk = n_rows // MOE_BM
    wspec = lambda r, c: pl.BlockSpec((1, r, c), lambda i, be, nb: (be[i], 0, 0))
    return pl.pallas_call(
        _experts_kernel,
        grid_spec=pltpu.PrefetchScalarGridSpec(
            num_scalar_prefetch=2,
            grid=(nblk,),
            in_specs=[pl.BlockSpec((MOE_BM, D_MODEL // 2), lambda i, be, nb: (i, 0)),
                      wspec(D_MODEL, D_FF), wspec(D_MODEL, D_FF), wspec(1, D_FF), wspec(1, D_FF),
                      wspec(D_FF, D_MODEL), wspec(1, D_MODEL)],
            out_specs=pl.BlockSpec((MOE_BM, D_MODEL), lambda i, be, nb: (i, 0)),
        ),
        out_shape=jax.ShapeDtypeStruct((n_rows, D_MODEL), F32),
        compiler_params=_cparams(("arbitrary",)),
        name="experts",
    )(blk_e, n_used, xs, w1g, w1l, b1g, b1l, w2, b2)


def _combine_kernel(pos_ref, h_ref, g_ref, yb_ref, y_ref, buf, sem):
    tm = h_ref.shape[0]

    def row_copy(t, k):
        return pltpu.make_async_copy(yb_ref.at[pl.ds(pos_ref[k, t], 1), :], buf.at[k, pl.ds(t, 1), :], sem)

    def start(t, c):
        for k in range(TOP_K):
            row_copy(t, k).start()
        return c

    def wait(t, c):
        for k in range(TOP_K):
            row_copy(t, k).wait()
        return c

    lax.fori_loop(0, tm, start, 0)
    lax.fori_loop(0, tm, wait, 0)
    g = g_ref[...]
    y = h_ref[...]
    for k in range(TOP_K):
        y = y + g[:, k:k + 1] * buf[k]
    y_ref[...] = y


def _combine(pos, h, gate_t, yb):
    T = h.shape[0]
    tm = COMB_TILE
    assert T % tm == 0
    return pl.pallas_call(
        _combine_kernel,
        grid=(T // tm,),
        in_specs=[pl.BlockSpec((TOP_K, tm), lambda i: (0, i), memory_space=pltpu.SMEM),
                  pl.BlockSpec((tm, D_MODEL), lambda i: (i, 0)),
                  pl.BlockSpec((tm, TOP_K), lambda i: (i, 0)),
                  pl.BlockSpec(memory_space=pl.ANY)],
        out_specs=pl.BlockSpec((tm, D_MODEL), lambda i: (i, 0)),
        out_shape=jax.ShapeDtypeStruct((T, D_MODEL), F32),
        scratch_shapes=[pltpu.VMEM((TOP_K, tm, D_MODEL), F32), pltpu.SemaphoreType.DMA],
        compiler_params=_cparams(("arbitrary",)),
        name="combine",
    )(pos, h, gate_t, yb)


def _block_ones(width, block):
    idx = np.arange(width) // block
    return jnp.asarray(idx[:, None] == idx[None, :], BF16)


def _layer(x_p, x_s, cache_k, cache_v, s_gla, norm1, w_in, q_norm, k_norm, sinks, w_gk, b_gk,
           gla_norm, w_a, w_b, w_o, norm2, w_router, b_router, w1, b1, w2, b2):
    Bp, Lp, _ = x_p.shape
    Bs, Ls, _ = x_s.shape
    Tp, Ts = Bp * Lp, Bs * Ls

    offs = np.cumsum((0, A_WIDTH, KV_WIDTH, KV_WIDTH, GLA_K_WIDTH, GLA_K_WIDTH, B_WIDTH, GLA_GATE_RANK, B_WIDTH,
                      D_MODEL, D_MODEL))
    src = dict(zip(("qa", "ka", "va", "qb", "kb", "vb", "gl", "rb", "ga", "gb"), zip(offs[:-1], offs[1:])))
    cols = []
    for name, width in _SEGS:
        a, b = src[name]
        piece = w_in[:, a:b]
        if b - a < width:
            piece = jnp.pad(piece, ((0, 0), (0, width - (b - a))))
        cols.append(piece)
    w_packed = jnp.concatenate(cols, axis=1).astype(BF16)
    wgk = jnp.pad(w_gk, ((0, GL_PAD - GLA_GATE_RANK), (0, 0))).astype(BF16)
    n1 = norm1.reshape(1, D_MODEL)
    gq, gk = _block_ones(A_WIDTH, SWA_HEAD_DIM), _block_ones(KV_WIDTH, SWA_HEAD_DIM)
    qn = jnp.tile(q_norm, SWA_HEADS).reshape(1, A_WIDTH)
    kn = jnp.tile(k_norm, SWA_KV_HEADS).reshape(1, KV_WIDTH)
    bgk = b_gk.reshape(1, GLA_K_WIDTH)
    gn = gla_norm.reshape(1, GLA_DV)
    wa, wb, wo = w_a.astype(BF16), w_b.astype(BF16), w_o.astype(BF16)
    n2 = norm2.reshape(1, D_MODEL)
    wr = jnp.pad(w_router, ((0, 0), (0, LANE - N_EXPERTS)))
    br = jnp.pad(b_router, (0, LANE - N_EXPERTS)).reshape(1, LANE)
    w1g = w1[:, :, 0::2].astype(BF16)
    w1l = w1[:, :, 1::2].astype(BF16)
    b1g = b1[:, 0::2].reshape(N_EXPERTS, 1, D_FF)
    b1l = b1[:, 1::2].reshape(N_EXPERTS, 1, D_FF)
    w2b = w2.astype(BF16)
    b2r = b2.reshape(N_EXPERTS, 1, D_MODEL)

    def mix(x, k_past, v_past, s0):
        B, L, _ = x.shape
        T = B * L
        x2 = x.reshape(T, D_MODEL)
        qa, ka, va, qb, kb, vb, la, rb, ga, gb = _inproj(x2, n1, w_packed, gq, gk, qn, kn, wgk, bgk)
        r3 = lambda t: t.reshape(B, L, t.shape[-1])
        oa = _swa(sinks, r3(qa), r3(ka), r3(va), k_past, v_past, has_past=k_past is not None)
        ob, s_out = _gla(r3(qb), r3(kb), r3(vb), r3(la), r3(rb), gn, s0)
        tm = min(TOK_TILE, T)
        upper = jnp.asarray(np.arange(tm)[:, None] < np.arange(tm)[None, :], BF16)
        h, hp, top_e, gate, rank, cnt = _outproj(x2, oa.reshape(T, A_WIDTH), ob.reshape(T, B_WIDTH), ga, gb,
                                                 wa, wb, wo, n2, wr, br, upper)
        return r3(ka), r3(va), s_out, h, hp, top_e, gate, rank, cnt[:, 0]

    win = cache_k.shape[1]
    assert win == WINDOW
    ck = cache_k.reshape(Bs, win, KV_WIDTH)
    cv = cache_v.reshape(Bs, win, KV_WIDTH)
    ka_p, va_p, s_p, h_p, hp_p, e_p, g_p, rank_p, cnt_p = mix(
        x_p, None, None, jnp.zeros((Bp, GLA_HEADS, GLA_DK, GLA_DV), F32))
    ka_s, va_s, s_s, h_s, hp_s, e_s, g_s, rank_s, cnt_s = mix(x_s, ck, cv, s_gla)

    T = Tp + Ts
    n_rows = -(-(T * TOP_K + N_EXPERTS * (MOE_BM - 1)) // MOE_BM) * MOE_BM
    counts = cnt_p + cnt_s
    padded = (counts + MOE_BM - 1) // MOE_BM * MOE_BM
    pad_end = jnp.cumsum(padded)
    start = pad_end - padded
    pos = jnp.concatenate([start[e_p] + rank_p, start[e_s] + cnt_p[e_s] + rank_s], axis=1)
    blk_e = jnp.minimum(jnp.searchsorted(pad_end, jnp.arange(n_rows // MOE_BM, dtype=I32) * MOE_BM, side="right"),
                        N_EXPERTS - 1).astype(I32)
    n_used = (pad_end[-1:] // MOE_BM).astype(I32)

    hp = jnp.concatenate([hp_p, hp_s], axis=0)
    h = jnp.concatenate([h_p, h_s], axis=0)
    gate_t = jnp.concatenate([g_p, g_s], axis=1).T
    xs = _dispatch(pos, hp, jnp.zeros((n_rows, D_MODEL // 2), U32))
    yb = _experts(blk_e, n_used, xs, w1g, w1l, b1g, b1l, w2b, b2r)
    y = _combine(pos, h, gate_t, yb)
    y_p = y[:Tp].reshape(Bp, Lp, D_MODEL)
    y_s = y[Tp:].reshape(Bs, Ls, D_MODEL)

    kv = lambda t: t.reshape(t.shape[0], t.shape[1], SWA_KV_HEADS, SWA_HEAD_DIM)
    k_p, v_p = kv(ka_p[:, -WINDOW:]), kv(va_p[:, -WINDOW:])
    k_s = jnp.concatenate([cache_k, kv(ka_s)], axis=1)[:, -win:]
    v_s = jnp.concatenate([cache_v, kv(va_s)], axis=1)[:, -win:]
    return y_p, y_s, (k_p, v_p, s_p, k_s, v_s, s_s)


def kernel(x_prompt, x_sample, cache_swa_k, cache_swa_v, state_gla, norm1, w_in, q_norm, k_norm, sinks, w_gk, b_gk,
           gla_norm, w_a, w_b, w_o, norm2, w_router, b_router, w1, b1, w2, b2):
    depth = norm1.shape[0]
    y_p, y_s = x_prompt, x_sample
    states = []
    for l in range(depth):
        y_p, y_s, st = _layer(y_p, y_s, cache_swa_k[l], cache_swa_v[l], state_gla[l], norm1[l], w_in[l], q_norm[l],
                              k_norm[l], sinks[l], w_gk[l], b_gk[l], gla_norm[l], w_a[l], w_b[l], w_o[l], norm2[l],
                              w_router[l], b_router[l], w1[l], b1[l], w2[l], b2[l])
        states.append(st)
    return (y_p, y_s) + tuple(jnp.stack([st[j] for st in states]) for j in range(6))
```

```python
import functools

import jax
import jax.numpy as jnp
import numpy as np
from jax import lax
from jax.experimental import pallas as pl
from jax.experimental.pallas import tpu as pltpu

F32 = jnp.float32
BF16 = jnp.bfloat16
I32 = jnp.int32
U32 = jnp.uint32

D_MODEL = 1024
SWA_HEADS = 8
SWA_KV_HEADS = 2
SWA_GROUP = SWA_HEADS // SWA_KV_HEADS
SWA_HEAD_DIM = 64
WINDOW = 128
ATTN_SCALE = SWA_HEAD_DIM ** -0.5
GLA_HEADS = 4
GLA_DK = 64
GLA_DV = 128
GLA_GATE_RANK = 16
GLA_GATE_NORM = 16.0
GLA_SCALE = GLA_DK ** -0.5
A_WIDTH = SWA_HEADS * SWA_HEAD_DIM
KV_WIDTH = SWA_KV_HEADS * SWA_HEAD_DIM
GLA_K_WIDTH = GLA_HEADS * GLA_DK
B_WIDTH = GLA_HEADS * GLA_DV
N_EXPERTS = 32
TOP_K = 4
D_FF = D_MODEL
SWIGLU_ALPHA = 1.702
SWIGLU_LIMIT = 7.0
NORM_EPS = 1e-5
QK_EPS = 1e-6
NEG_INF = -1e30

LANE = 128
GL_PAD = LANE
VMEM_LIMIT = 56 * 1024 * 1024

_SEGS = (("qa", A_WIDTH), ("ka", KV_WIDTH), ("va", KV_WIDTH), ("qb", GLA_K_WIDTH), ("kb", GLA_K_WIDTH),
         ("vb", B_WIDTH), ("rb", B_WIDTH), ("ga", D_MODEL), ("gb", D_MODEL), ("gl", GL_PAD))
_OFF = {}
_o = 0
for _n, _w in _SEGS:
    _OFF[_n] = (_o, _w)
    _o += _w
IN_PACKED = _o

TOK_TILE = 512
GLA_CHUNK = 64
GLA_SUB = 16
MOE_BM = 256
DISP_TILE = 256
COMB_TILE = 128
SAMPLE_BATCH = 8


def _cparams(sem):
    return pltpu.CompilerParams(dimension_semantics=sem, vmem_limit_bytes=VMEM_LIMIT)


def _split_bf16(v):
    hi = v.astype(BF16)
    lo = (v - hi.astype(F32)).astype(BF16)
    return hi, lo


def _dot(a, b):
    return jnp.dot(a, b, preferred_element_type=F32)


def _dot_nt(a, b):
    return lax.dot_general(a, b, (((1,), (1,)), ((), ())), preferred_element_type=F32)


def _dot_tn(a, b):
    return lax.dot_general(a, b, (((0,), (0,)), ((), ())), preferred_element_type=F32)


def _inproj_kernel(x_ref, n1_ref, w_ref, gq_ref, gk_ref, qn_ref, kn_ref, wgk_ref, bgk_ref,
                   qa_ref, ka_ref, va_ref, qb_ref, kb_ref, vb_ref, la_ref, rb_ref, ga_ref, gb_ref):
    x = x_ref[...]
    ms = jnp.mean(x * x, axis=-1, keepdims=True)
    xn = (x * lax.rsqrt(ms + NORM_EPS) * n1_ref[...]).astype(BF16)

    def proj(name):
        off, width = _OFF[name]
        return _dot(xn, w_ref[:, off:off + width])

    def head_norm(v, ones_ref, gain_ref):
        hi, lo = _split_bf16(v * v)
        ss = _dot(hi, ones_ref[...]) + _dot(lo, ones_ref[...])
        return v * lax.rsqrt(ss * (1.0 / SWA_HEAD_DIM) + QK_EPS) * gain_ref[...]

    qa_ref[...] = (head_norm(proj("qa"), gq_ref, qn_ref) * ATTN_SCALE).astype(BF16)
    ka_ref[...] = head_norm(proj("ka"), gk_ref, kn_ref)
    va_ref[...] = proj("va")
    qb_ref[...] = (proj("qb") * GLA_SCALE).astype(BF16)
    kb_ref[...] = proj("kb").astype(BF16)
    vb_ref[...] = proj("vb").astype(BF16)
    z = _dot(proj("gl").astype(BF16), wgk_ref[...]) + bgk_ref[...]
    la_ref[...] = (jnp.minimum(z, 0.0) - jnp.log1p(jnp.exp(-jnp.abs(z)))) * (1.0 / GLA_GATE_NORM)
    rb = proj("rb")
    rb_ref[...] = (rb * jax.nn.sigmoid(rb)).astype(BF16)
    ga_ref[...] = jax.nn.sigmoid(proj("ga")).astype(BF16)
    gb_ref[...] = jax.nn.sigmoid(proj("gb")).astype(BF16)


def _inproj(x2, n1, w_packed, gq, gk, qn, kn, wgk, bgk):
    T = x2.shape[0]
    tm = min(TOK_TILE, T)
    assert T % tm == 0

    def tok(width):
        return pl.BlockSpec((tm, width), lambda i: (i, 0))

    def const(shape):
        return pl.BlockSpec(shape, lambda i: (0, 0))

    outs = (("qa", A_WIDTH, BF16), ("ka", KV_WIDTH, F32), ("va", KV_WIDTH, F32), ("qb", GLA_K_WIDTH, BF16),
            ("kb", GLA_K_WIDTH, BF16), ("vb", B_WIDTH, BF16), ("la", GLA_K_WIDTH, F32), ("rb", B_WIDTH, BF16),
            ("ga", D_MODEL, BF16), ("gb", D_MODEL, BF16))
    return pl.pallas_call(
        _inproj_kernel,
        grid=(T // tm,),
        in_specs=[tok(D_MODEL), const((1, D_MODEL)), const((D_MODEL, IN_PACKED)), const((A_WIDTH, A_WIDTH)),
                  const((KV_WIDTH, KV_WIDTH)), const((1, A_WIDTH)), const((1, KV_WIDTH)),
                  const((GL_PAD, GLA_K_WIDTH)), const((1, GLA_K_WIDTH))],
        out_specs=[tok(w) for _, w, _ in outs],
        out_shape=[jax.ShapeDtypeStruct((T, w), dt) for _, w, dt in outs],
        compiler_params=_cparams(("parallel",)),
        name="inproj",
    )(x2, n1, w_packed, gq, gk, qn, kn, wgk, bgk)


def _swa_kernel(sink_ref, q_ref, kc_ref, vc_ref, kp_ref, vp_ref, o_ref, *, first_block_has_past):
    n = pl.program_id(1)
    nb, lq = q_ref.shape[0], q_ref.shape[1]
    assert lq & (lq - 1) == 0
    rows, keys = SWA_GROUP * lq, WINDOW + lq
    qi = lax.broadcasted_iota(I32, (rows, keys), 0) & (lq - 1)
    ci = lax.broadcasted_iota(I32, (rows, keys), 1)
    dist = WINDOW + qi - ci
    mask = (dist >= 0) & (dist < WINDOW)
    if not first_block_has_past:
        mask = mask & ((ci >= WINDOW) | (n > 0))
    for j in range(nb):
        q = q_ref[j].astype(F32)
        k = jnp.concatenate([kp_ref[j], kc_ref[j]], axis=0).astype(BF16)
        v = jnp.concatenate([vp_ref[j], vc_ref[j]], axis=0).astype(BF16)
        outs = []
        for kh in range(SWA_KV_HEADS):
            heads = range(kh * SWA_GROUP, (kh + 1) * SWA_GROUP)
            cols = slice(kh * SWA_HEAD_DIM, (kh + 1) * SWA_HEAD_DIM)
            qg = jnp.concatenate([q[:, h * SWA_HEAD_DIM:(h + 1) * SWA_HEAD_DIM] for h in heads], axis=0)
            sink = jnp.concatenate([jnp.full((lq, 1), sink_ref[h], F32) for h in heads], axis=0)
            s = jnp.where(mask, _dot_nt(qg.astype(BF16), k[:, cols]), NEG_INF)
            m = jnp.maximum(jnp.max(s, axis=-1, keepdims=True), sink)
            p = jnp.exp(s - m)
            denom = jnp.sum(p, axis=-1, keepdims=True) + jnp.exp(sink - m)
            o = _dot(p.astype(BF16), v[:, cols]) / denom
            outs += [o[g * lq:(g + 1) * lq] for g in range(SWA_GROUP)]
        o_ref[j] = jnp.concatenate(outs, axis=-1).astype(BF16)


def _swa(sinks, qa, ka, va, k_past, v_past, *, has_past):
    B, L, _ = qa.shape
    lq = min(L, WINDOW)
    nl = L // lq
    if has_past:
        assert nl == 1
        nb = min(B, SAMPLE_BATCH)
        past_spec = pl.BlockSpec((nb, WINDOW, KV_WIDTH), lambda b, n, s: (b, 0, 0))
        kp, vp = k_past, v_past
    else:
        assert lq == WINDOW
        nb = 1
        past_spec = pl.BlockSpec((nb, WINDOW, KV_WIDTH), lambda b, n, s: (b, jnp.maximum(n - 1, 0), 0))
        kp, vp = ka, va
    assert B % nb == 0
    cur = lambda w: pl.BlockSpec((nb, lq, w), lambda b, n, s: (b, n, 0))
    return pl.pallas_call(
        functools.partial(_swa_kernel, first_block_has_past=has_past),
        grid_spec=pltpu.PrefetchScalarGridSpec(
            num_scalar_prefetch=1,
            grid=(B // nb, nl),
            in_specs=[cur(A_WIDTH), cur(KV_WIDTH), cur(KV_WIDTH), past_spec, past_spec],
            out_specs=cur(A_WIDTH),
        ),
        out_shape=jax.ShapeDtypeStruct((B, L, A_WIDTH), BF16),
        compiler_params=_cparams(("parallel", "parallel")),
        name="swa",
    )(sinks, qa, ka, va, kp, vp)


def _gla_kernel(q_ref, k_ref, v_ref, la_ref, rb_ref, gn_ref, s0_ref, o_ref, sout_ref, s_scr, *, chunk, sub, length):
    C, SUB = chunk, sub
    S = C // SUB
    W = GLA_K_WIDTH
    nb = q_ref.shape[0]
    s_scr[...] = s0_ref[...]
    row = lax.broadcasted_iota(I32, (C, C), 0)
    col = lax.broadcasted_iota(I32, (C, C), 1)
    tri = (row >= col).astype(BF16)
    diag_mask = (row >= col) & ((row // SUB) == (col // SUB))
    eye = lax.broadcasted_iota(I32, (GLA_DK, GLA_DK), 0) == lax.broadcasted_iota(I32, (GLA_DK, GLA_DK), 1)
    krow = lax.broadcasted_iota(I32, (C, W), 0)

    def rows_of(x, r):
        return jnp.broadcast_to(x[r:r + 1, :], (SUB, W))

    def body(j, c):
        r0 = c * C if isinstance(c, int) else pl.multiple_of(c * C, C)
        q = q_ref[j, pl.ds(r0, C), :].astype(F32)
        k = k_ref[j, pl.ds(r0, C), :].astype(F32)
        v = v_ref[j, pl.ds(r0, C), :]
        g = la_ref[j, pl.ds(r0, C), :]
        g_hi, g_lo = _split_bf16(g)
        b = _dot(tri, g_hi) + _dot(tri, g_lo)
        b_last = b[C - 1:C, :]
        mid = jnp.concatenate([rows_of(b, i * SUB + SUB // 2 - 1) for i in range(S)], axis=0)
        beta = jnp.concatenate([jnp.zeros((SUB, W), F32)] + [rows_of(b, i * SUB - 1) for i in range(1, S)], axis=0)
        q_inter = (q * jnp.exp(b)).astype(BF16)
        k_state = (k * jnp.exp(b_last - b)).astype(BF16)
        q_diag = (q * jnp.exp(b - mid)).astype(BF16)
        k_diag = (k * jnp.exp(mid - b)).astype(BF16)
        q_off = (q * jnp.exp(b - beta)).astype(BF16)
        k_off = [None]
        for i in range(1, S):
            e = jnp.exp(jnp.minimum(b[i * SUB - 1:i * SUB, :] - b, 0.0))
            k_off.append(jnp.where(krow < i * SUB, k * e, 0.0).astype(BF16))
        outs = []
        for h in range(GLA_HEADS):
            ks = slice(h * GLA_DK, (h + 1) * GLA_DK)
            vs = slice(h * GLA_DV, (h + 1) * GLA_DV)
            a_diag = _dot_nt(q_diag[:, ks], k_diag[:, ks])
            if S > 1:
                blocks = [jnp.zeros((SUB, C), F32)]
                for i in range(1, S):
                    blocks.append(_dot_nt(q_off[i * SUB:(i + 1) * SUB, ks], k_off[i][:, ks]))
                a = jnp.where(diag_mask, a_diag, jnp.concatenate(blocks, axis=0))
            else:
                a = jnp.where(diag_mask, a_diag, 0.0)
            s_h = s_scr[j, h]
            v_h = v[:, vs]
            o_h = _dot(a.astype(BF16), v_h) + _dot(q_inter[:, ks], s_h.astype(BF16))
            dec = jnp.exp(b_last[:, ks])
            dec_col = jnp.sum(jnp.where(eye, jnp.broadcast_to(dec, (GLA_DK, GLA_DK)), 0.0), axis=-1, keepdims=True)
            s_scr[j, h] = dec_col * s_h + _dot_tn(k_state[:, ks], v_h)
            ms = jnp.mean(o_h * o_h, axis=-1, keepdims=True)
            outs.append(o_h * lax.rsqrt(ms + NORM_EPS) * gn_ref[...])
        o = jnp.concatenate(outs, axis=-1) * rb_ref[j, pl.ds(r0, C), :].astype(F32)
        o_ref[j, pl.ds(r0, C), :] = o.astype(BF16)

    for j in range(nb):
        if length == C:
            body(j, 0)
        else:
            lax.fori_loop(0, length // C, lambda c, carry, j=j: (body(j, c), carry)[1], 0)
    sout_ref[...] = s_scr[...]


def _gla(qb, kb, vb, la, rb, gn, s0):
    B, L, _ = qb.shape
    chunk = GLA_CHUNK if L % GLA_CHUNK == 0 else L
    sub = GLA_SUB if chunk % GLA_SUB == 0 else chunk
    nb = min(B, SAMPLE_BATCH) if L == chunk else 1
    assert B % nb == 0
    seq = lambda w: pl.BlockSpec((nb, L, w), lambda b: (b, 0, 0))
    st = pl.BlockSpec((nb, GLA_HEADS, GLA_DK, GLA_DV), lambda b: (b, 0, 0, 0))
    return pl.pallas_call(
        functools.partial(_gla_kernel, chunk=chunk, sub=sub, length=L),
        grid=(B // nb,),
        in_specs=[seq(GLA_K_WIDTH), seq(GLA_K_WIDTH), seq(B_WIDTH), seq(GLA_K_WIDTH), seq(B_WIDTH),
                  pl.BlockSpec((1, GLA_DV), lambda b: (0, 0)), st],
        out_specs=[seq(B_WIDTH), st],
        out_shape=[jax.ShapeDtypeStruct((B, L, B_WIDTH), BF16),
                   jax.ShapeDtypeStruct((B, GLA_HEADS, GLA_DK, GLA_DV), F32)],
        scratch_shapes=[pltpu.VMEM((nb, GLA_HEADS, GLA_DK, GLA_DV), F32)],
        compiler_params=_cparams(("parallel",)),
        name="gla",
    )(qb, kb, vb, la, rb, gn, s0)


def _outproj_kernel(x_ref, oa_ref, ob_ref, ga_ref, gb_ref, wa_ref, wb_ref, wo_ref, n2_ref, wr_ref, br_ref, upper_ref,
                    h_ref, hp_ref, e_ref, g_ref, rank_ref, cnt_ref, carry):
    i = pl.program_id(0)

    @pl.when(i == 0)
    def _():
        carry[...] = jnp.zeros_like(carry)

    tm = x_ref.shape[0]
    merged = (ga_ref[...].astype(F32) * _dot(oa_ref[...], wa_ref[...])
              + gb_ref[...].astype(F32) * _dot(ob_ref[...], wb_ref[...]))
    h = x_ref[...] + _dot(merged.astype(BF16), wo_ref[...])
    h_ref[...] = h
    ms = jnp.mean(h * h, axis=-1, keepdims=True)
    hn = h * lax.rsqrt(ms + NORM_EPS) * n2_ref[...]
    half = D_MODEL // 2
    lo_bits = pltpu.bitcast(hn[:, :half].astype(BF16).astype(F32), U32) >> 16
    hi_bits = pltpu.bitcast(hn[:, half:].astype(BF16).astype(F32), U32) & jnp.uint32(0xFFFF0000)
    hp_ref[...] = lo_bits | hi_bits
    hn_hi, hn_lo = _split_bf16(hn)
    wr_hi, wr_lo = _split_bf16(wr_ref[...])
    logits = _dot(hn_hi, wr_hi) + _dot(hn_lo, wr_hi) + _dot(hn_hi, wr_lo) + br_ref[...]
    lt = logits.T[:N_EXPERTS, :]
    eid = lax.broadcasted_iota(I32, (N_EXPERTS, tm), 0)
    vals, hots = [], []
    for k in range(TOP_K):
        m = jnp.max(lt, axis=0, keepdims=True)
        idx = jnp.min(jnp.where(lt == m, eid, N_EXPERTS), axis=0, keepdims=True)
        hot = eid == idx
        lt = jnp.where(hot, -jnp.inf, lt)
        vals.append(m)
        hots.append(hot)
        e_ref[k:k + 1, :] = idx
    ex = [jnp.exp(vk - vals[0]) for vk in vals]
    den = ex[0] + ex[1] + ex[2] + ex[3]
    for k in range(TOP_K):
        g_ref[k:k + 1, :] = ex[k] / den
    multi = (hots[0] | hots[1] | hots[2] | hots[3])
    before = _dot(multi.astype(BF16), upper_ref[...]) + carry[...]
    for k in range(TOP_K):
        rank_ref[k:k + 1, :] = jnp.sum(jnp.where(hots[k], before, 0.0), axis=0, keepdims=True).astype(I32)
    carry[...] = carry[...] + jnp.sum(multi.astype(F32), axis=1, keepdims=True)
    cnt_ref[...] = jnp.broadcast_to(carry[...], cnt_ref.shape).astype(I32)


def _outproj(x2, oa, ob, ga, gb, wa, wb, wo, n2, wr, br, upper):
    T = x2.shape[0]
    tm = min(TOK_TILE, T)
    assert T % tm == 0 and upper.shape == (tm, tm)
    tok = lambda w: pl.BlockSpec((tm, w), lambda i: (i, 0))
    const = lambda shape: pl.BlockSpec(shape, lambda i: (0, 0))
    kt = pl.BlockSpec((TOP_K, tm), lambda i: (0, i))
    return pl.pallas_call(
        _outproj_kernel,
        grid=(T // tm,),
        in_specs=[tok(D_MODEL), tok(A_WIDTH), tok(B_WIDTH), tok(D_MODEL), tok(D_MODEL),
                  const((A_WIDTH, D_MODEL)), const((B_WIDTH, D_MODEL)), const((D_MODEL, D_MODEL)),
                  const((1, D_MODEL)), const((D_MODEL, LANE)), const((1, LANE)), const((tm, tm))],
        out_specs=[tok(D_MODEL), tok(D_MODEL // 2), kt, kt, kt, const((N_EXPERTS, LANE))],
        out_shape=[jax.ShapeDtypeStruct((T, D_MODEL), F32), jax.ShapeDtypeStruct((T, D_MODEL // 2), U32),
                   jax.ShapeDtypeStruct((TOP_K, T), I32), jax.ShapeDtypeStruct((TOP_K, T), F32),
                   jax.ShapeDtypeStruct((TOP_K, T), I32), jax.ShapeDtypeStruct((N_EXPERTS, LANE), I32)],
        scratch_shapes=[pltpu.VMEM((N_EXPERTS, 1), F32)],
        compiler_params=_cparams(("arbitrary",)),
        name="outproj",
    )(x2, oa, ob, ga, gb, wa, wb, wo, n2, wr, br, upper)


def _dispatch_kernel(fill_ref, pos_ref, hp_p_ref, hp_s_ref, xs_ref, sem, zbuf, zsem, *, steps_p):
    i = pl.program_id(0)
    tm = hp_p_ref.shape[0]

    @pl.when(i == 0)
    def _():
        zbuf[...] = jnp.zeros_like(zbuf)

        def fill_copy(j):
            return pltpu.make_async_copy(zbuf, xs_ref.at[pl.ds(pl.multiple_of(fill_ref[j], MOE_BM), MOE_BM), :], zsem)

        for j in range(fill_ref.shape[0]):
            pl.when(fill_ref[j] >= 0)(lambda j=j: fill_copy(j).start())
        for j in range(fill_ref.shape[0]):
            pl.when(fill_ref[j] >= 0)(lambda j=j: fill_copy(j).wait())

    def scatter(hp_ref):
        def row_copy(t, k):
            return pltpu.make_async_copy(hp_ref.at[pl.ds(t, 1), :], xs_ref.at[pl.ds(pos_ref[k, t], 1), :], sem)

        def start(t, c):
            for k in range(TOP_K):
                row_copy(t, k).start()
            return c

        def wait(t, c):
            for k in range(TOP_K):
                row_copy(t, k).wait()
            return c

        lax.fori_loop(0, tm, start, 0)
        lax.fori_loop(0, tm, wait, 0)

    pl.when(i < steps_p)(lambda: scatter(hp_p_ref))
    pl.when(i >= steps_p)(lambda: scatter(hp_s_ref))


def _dispatch(fill, pos, hp_p, hp_s, n_rows):
    tm = DISP_TILE
    assert hp_p.shape[0] % tm == 0 and hp_s.shape[0] % tm == 0
    steps_p, steps_s = hp_p.shape[0] // tm, hp_s.shape[0] // tm
    return pl.pallas_call(
        functools.partial(_dispatch_kernel, steps_p=steps_p),
        grid_spec=pltpu.PrefetchScalarGridSpec(
            num_scalar_prefetch=1,
            grid=(steps_p + steps_s,),
            in_specs=[pl.BlockSpec((TOP_K, tm), lambda i, fl: (0, i), memory_space=pltpu.SMEM),
                      pl.BlockSpec((tm, D_MODEL // 2), lambda i, fl: (jnp.minimum(i, steps_p - 1), 0)),
                      pl.BlockSpec((tm, D_MODEL // 2), lambda i, fl: (jnp.maximum(i - steps_p, 0), 0))],
            out_specs=pl.BlockSpec(memory_space=pl.ANY),
            scratch_shapes=[pltpu.SemaphoreType.DMA, pltpu.VMEM((MOE_BM, D_MODEL // 2), U32),
                            pltpu.SemaphoreType.DMA],
        ),
        out_shape=jax.ShapeDtypeStruct((n_rows, D_MODEL // 2), U32),
        compiler_params=_cparams(("arbitrary",)),
        name="dispatch",
    )(fill, pos, hp_p, hp_s)


def _experts_kernel(be_ref, nb_ref, xs_ref, w1_ref, b1g_ref, b1l_ref, w2_ref, b2_ref, sel_ref, y_ref,
                    w1g_scr, w1l_scr, w2_scr):
    i = pl.program_id(0)
    live = i < nb_ref[0]

    @pl.when(jnp.logical_not(live))
    def _():
        y_ref[...] = jnp.zeros_like(y_ref)

    @pl.when(live & ((i == 0) | (be_ref[i] != be_ref[jnp.maximum(i - 1, 0)])))
    def _():
        for c in range(D_FF // LANE):
            cols = _dot(w1_ref[0, :, c * 2 * LANE:(c + 1) * 2 * LANE].astype(BF16), sel_ref[...])
            w1g_scr[:, c * LANE:(c + 1) * LANE] = cols[:, :LANE].astype(BF16)
            w1l_scr[:, c * LANE:(c + 1) * LANE] = cols[:, LANE:].astype(BF16)
        w2_scr[...] = w2_ref[0].astype(BF16)

    @pl.when(live)
    def _():
        half = D_MODEL // 2
        w = xs_ref[...]
        x_lo = pltpu.bitcast(w << 16, F32).astype(BF16)
        x_hi = pltpu.bitcast(w & jnp.uint32(0xFFFF0000), F32).astype(BF16)

        def up(w_scr, b_ref):
            return _dot(x_lo, w_scr[:half, :]) + _dot(x_hi, w_scr[half:, :]) + b_ref[0]

        x_glu = jnp.minimum(up(w1g_scr, b1g_ref), SWIGLU_LIMIT)
        x_lin = jnp.clip(up(w1l_scr, b1l_ref), -SWIGLU_LIMIT, SWIGLU_LIMIT)
        act = x_glu * jax.nn.sigmoid(SWIGLU_ALPHA * x_glu) * (x_lin + 1.0)
        y_ref[...] = _dot(act.astype(BF16), w2_scr[...]) + b2_ref[0]


def _experts(blk_e, n_used, xs, w1, b1g, b1l, w2, b2, sel):
    n_rows = xs.shape[0]
    nblk = n_rows // MOE_BM
    wspec = lambda r, c: pl.BlockSpec((1, r, c), lambda i, be, nb: (be[i], 0, 0))
    return pl.pallas_call(
        _experts_kernel,
        grid_spec=pltpu.PrefetchScalarGridSpec(
            num_scalar_prefetch=2,
            grid=(nblk,),
            in_specs=[pl.BlockSpec((MOE_BM, D_MODEL // 2), lambda i, be, nb: (i, 0)),
                      wspec(D_MODEL, 2 * D_FF), wspec(1, D_FF), wspec(1, D_FF),
                      wspec(D_FF, D_MODEL), wspec(1, D_MODEL),
                      pl.BlockSpec((2 * LANE, 2 * LANE), lambda i, be, nb: (0, 0))],
            out_specs=pl.BlockSpec((MOE_BM, D_MODEL), lambda i, be, nb: (i, 0)),
            scratch_shapes=[pltpu.VMEM((D_MODEL, D_FF), BF16), pltpu.VMEM((D_MODEL, D_FF), BF16),
                            pltpu.VMEM((D_FF, D_MODEL), BF16)],
        ),
        out_shape=jax.ShapeDtypeStruct((n_rows, D_MODEL), F32),
        compiler_params=_cparams(("arbitrary",)),
        name="experts",
    )(blk_e, n_used, xs, w1, b1g, b1l, w2, b2, sel)


def _combine_kernel(pos_ref, h_ref, g_ref, yb_ref, y_ref, buf, sem):
    tm = h_ref.shape[0]

    def row_copy(t, k):
        return pltpu.make_async_copy(yb_ref.at[pl.ds(pos_ref[k, t], 1), :], buf.at[k, pl.ds(t, 1), :], sem)

    def start(t, c):
        for k in range(TOP_K):
            row_copy(t, k).start()
        return c

    def wait(t, c):
        for k in range(TOP_K):
            row_copy(t, k).wait()
        return c

    lax.fori_loop(0, tm, start, 0)
    lax.fori_loop(0, tm, wait, 0)
    g = g_ref[...]
    y = h_ref[...]
    for k in range(TOP_K):
        y = y + g[:, k:k + 1] * buf[k]
    y_ref[...] = y


def _combine(pos, h, gate_t, yb):
    T = h.shape[0]
    tm = COMB_TILE
    assert T % tm == 0
    return pl.pallas_call(
        _combine_kernel,
        grid=(T // tm,),
        in_specs=[pl.BlockSpec((TOP_K, tm), lambda i: (0, i), memory_space=pltpu.SMEM),
                  pl.BlockSpec((tm, D_MODEL), lambda i: (i, 0)),
                  pl.BlockSpec((tm, TOP_K), lambda i: (i, 0)),
                  pl.BlockSpec(memory_space=pl.ANY)],
        out_specs=pl.BlockSpec((tm, D_MODEL), lambda i: (i, 0)),
        out_shape=jax.ShapeDtypeStruct((T, D_MODEL), F32),
        scratch_shapes=[pltpu.VMEM((TOP_K, tm, D_MODEL), F32), pltpu.SemaphoreType.DMA],
        compiler_params=_cparams(("arbitrary",)),
        name="combine",
    )(pos, h, gate_t, yb)


def _block_ones(width, block):
    idx = np.arange(width) // block
    return jnp.asarray(idx[:, None] == idx[None, :], BF16)


def _layer(x_p, x_s, cache_k, cache_v, s_gla, norm1, w_in, q_norm, k_norm, sinks, w_gk, b_gk,
           gla_norm, w_a, w_b, w_o, norm2, w_router, b_router, w1, b1, w2, b2):
    Bp, Lp, _ = x_p.shape
    Bs, Ls, _ = x_s.shape
    Tp, Ts = Bp * Lp, Bs * Ls

    offs = np.cumsum((0, A_WIDTH, KV_WIDTH, KV_WIDTH, GLA_K_WIDTH, GLA_K_WIDTH, B_WIDTH, GLA_GATE_RANK, B_WIDTH,
                      D_MODEL, D_MODEL))
    src = dict(zip(("qa", "ka", "va", "qb", "kb", "vb", "gl", "rb", "ga", "gb"), zip(offs[:-1], offs[1:])))
    cols = []
    for name, width in _SEGS:
        a, b = src[name]
        piece = w_in[:, a:b]
        if b - a < width:
            piece = jnp.pad(piece, ((0, 0), (0, width - (b - a))))
        cols.append(piece)
    w_packed = jnp.concatenate(cols, axis=1).astype(BF16)
    wgk = jnp.pad(w_gk, ((0, GL_PAD - GLA_GATE_RANK), (0, 0))).astype(BF16)
    n1 = norm1.reshape(1, D_MODEL)
    gq, gk = _block_ones(A_WIDTH, SWA_HEAD_DIM), _block_ones(KV_WIDTH, SWA_HEAD_DIM)
    qn = jnp.tile(q_norm, SWA_HEADS).reshape(1, A_WIDTH)
    kn = jnp.tile(k_norm, SWA_KV_HEADS).reshape(1, KV_WIDTH)
    bgk = b_gk.reshape(1, GLA_K_WIDTH)
    gn = gla_norm.reshape(1, GLA_DV)
    wa, wb, wo = w_a.astype(BF16), w_b.astype(BF16), w_o.astype(BF16)
    n2 = norm2.reshape(1, D_MODEL)
    wr = jnp.pad(w_router, ((0, 0), (0, LANE - N_EXPERTS)))
    br = jnp.pad(b_router, (0, LANE - N_EXPERTS)).reshape(1, LANE)
    b1g = b1[:, 0::2].reshape(N_EXPERTS, 1, D_FF)
    b1l = b1[:, 1::2].reshape(N_EXPERTS, 1, D_FF)
    b2r = b2.reshape(N_EXPERTS, 1, D_MODEL)
    sel_np = np.zeros((2 * LANE, 2 * LANE), np.float32)
    sel_np[2 * np.arange(LANE), np.arange(LANE)] = 1.0
    sel_np[2 * np.arange(LANE) + 1, LANE + np.arange(LANE)] = 1.0
    sel = jnp.asarray(sel_np, BF16)

    def mix(x, k_past, v_past, s0):
        B, L, _ = x.shape
        T = B * L
        x2 = x.reshape(T, D_MODEL)
        qa, ka, va, qb, kb, vb, la, rb, ga, gb = _inproj(x2, n1, w_packed, gq, gk, qn, kn, wgk, bgk)
        r3 = lambda t: t.reshape(B, L, t.shape[-1])
        oa = _swa(sinks, r3(qa), r3(ka), r3(va), k_past, v_past, has_past=k_past is not None)
        ob, s_out = _gla(r3(qb), r3(kb), r3(vb), r3(la), r3(rb), gn, s0)
        tm = min(TOK_TILE, T)
        upper = jnp.asarray(np.arange(tm)[:, None] < np.arange(tm)[None, :], BF16)
        h, hp, top_e, gate, rank, cnt = _outproj(x2, oa.reshape(T, A_WIDTH), ob.reshape(T, B_WIDTH), ga, gb,
                                                 wa, wb, wo, n2, wr, br, upper)
        return r3(ka), r3(va), s_out, h, hp, top_e, gate, rank, cnt[:, 0]

    win = cache_k.shape[1]
    assert win == WINDOW
    ck = cache_k.reshape(Bs, win, KV_WIDTH)
    cv = cache_v.reshape(Bs, win, KV_WIDTH)
    ka_p, va_p, s_p, h_p, hp_p, e_p, g_p, rank_p, cnt_p = mix(
        x_p, None, None, jnp.zeros((Bp, GLA_HEADS, GLA_DK, GLA_DV), F32))
    ka_s, va_s, s_s, h_s, hp_s, e_s, g_s, rank_s, cnt_s = mix(x_s, ck, cv, s_gla)

    T = Tp + Ts
    n_rows = -(-(T * TOP_K + N_EXPERTS * (MOE_BM - 1)) // MOE_BM) * MOE_BM
    counts = cnt_p + cnt_s
    padded = (counts + MOE_BM - 1) // MOE_BM * MOE_BM
    pad_end = jnp.cumsum(padded)
    start = pad_end - padded

    def lookup(table, idx):
        hot = idx[None] == jnp.arange(N_EXPERTS, dtype=I32)[:, None, None]
        return jnp.sum(jnp.where(hot, table[:, None, None], 0), axis=0)

    pos_p = lookup(start, e_p) + rank_p
    pos_s = lookup(start + cnt_p, e_s) + rank_s
    blk_row = jnp.arange(n_rows // MOE_BM, dtype=I32) * MOE_BM
    blk_e = jnp.minimum(jnp.sum(pad_end[None, :] <= blk_row[:, None], axis=1), N_EXPERTS - 1).astype(I32)
    n_used = (pad_end[-1:] // MOE_BM).astype(I32)
    tails = jnp.where(padded > 0, pad_end - MOE_BM, -1)
    spare = pad_end[-1] + blk_row[:N_EXPERTS]
    fill = jnp.concatenate([tails, jnp.where(spare < n_rows, spare, -1)]).astype(I32)

    xs = _dispatch(fill, jnp.concatenate([pos_p, pos_s], axis=1), hp_p, hp_s, n_rows)
    yb = _experts(blk_e, n_used, xs, w1, b1g, b1l, w2, b2r, sel)
    y_p = _combine(pos_p, h_p, g_p.T, yb).reshape(Bp, Lp, D_MODEL)
    y_s = _combine(pos_s, h_s, g_s.T, yb).reshape(Bs, Ls, D_MODEL)

    kv = lambda t: t.reshape(t.shape[0], t.shape[1], SWA_KV_HEADS, SWA_HEAD_DIM)
    k_p, v_p = kv(ka_p[:, -WINDOW:]), kv(va_p[:, -WINDOW:])
    k_s = jnp.concatenate([cache_k, kv(ka_s)], axis=1)[:, -win:]
    v_s = jnp.concatenate([cache_v, kv(va_s)], axis=1)[:, -win:]
    return y_p, y_s, (k_p, v_p, s_p, k_s, v_s, s_s)


def kernel(x_prompt, x_sample, cache_swa_k, cache_swa_v, state_gla, norm1, w_in, q_norm, k_norm, sinks, w_gk, b_gk,
           gla_norm, w_a, w_b, w_o, norm2, w_router, b_router, w1, b1, w2, b2):
    depth = norm1.shape[0]
    y_p, y_s = x_prompt, x_sample
    states = []
    for l in range(depth):
        y_p, y_s, st = _layer(y_p, y_s, cache_swa_k[l], cache_swa_v[l], state_gla[l], norm1[l], w_in[l], q_norm[l],
                              k_norm[l], sinks[l], w_gk[l], b_gk[l], gla_norm[l], w_a[l], w_b[l], w_o[l], norm2[l],
                              w_router[l], b_router[l], w1[l], b1[l], w2[l], b2[l])
        states.append(st)
    return (y_p, y_s) + tuple(jnp.stack([st[j] for st in states]) for j in range(6))
```

```python
import functools

import jax
import jax.numpy as jnp
import numpy as np
from jax import lax
from jax.experimental import pallas as pl
from jax.experimental.pallas import tpu as pltpu

F32 = jnp.float32
BF16 = jnp.bfloat16
I32 = jnp.int32
U32 = jnp.uint32

D_MODEL = 1024
SWA_HEADS = 8
SWA_KV_HEADS = 2
SWA_GROUP = SWA_HEADS // SWA_KV_HEADS
SWA_HEAD_DIM = 64
WINDOW = 128
ATTN_SCALE = SWA_HEAD_DIM ** -0.5
GLA_HEADS = 4
GLA_DK = 64
GLA_DV = 128
GLA_GATE_RANK = 16
GLA_GATE_NORM = 16.0
GLA_SCALE = GLA_DK ** -0.5
A_WIDTH = SWA_HEADS * SWA_HEAD_DIM
KV_WIDTH = SWA_KV_HEADS * SWA_HEAD_DIM
GLA_K_WIDTH = GLA_HEADS * GLA_DK
B_WIDTH = GLA_HEADS * GLA_DV
N_EXPERTS = 32
TOP_K = 4
D_FF = D_MODEL
SWIGLU_ALPHA = 1.702
SWIGLU_LIMIT = 7.0
NORM_EPS = 1e-5
QK_EPS = 1e-6
NEG_INF = -1e30

LANE = 128
GL_PAD = LANE
VMEM_LIMIT = 56 * 1024 * 1024

_SEGS = (("qa", A_WIDTH), ("ka", KV_WIDTH), ("va", KV_WIDTH), ("qb", GLA_K_WIDTH), ("kb", GLA_K_WIDTH),
         ("vb", B_WIDTH), ("rb", B_WIDTH), ("ga", D_MODEL), ("gb", D_MODEL), ("gl", GL_PAD))
_OFF = {}
_o = 0
for _n, _w in _SEGS:
    _OFF[_n] = (_o, _w)
    _o += _w
IN_PACKED = _o

TOK_TILE = 512
GLA_CHUNK = 64
GLA_SUB = 16
MOE_BM = 256
DISP_TILE = 256
COMB_TILE = 128
SAMPLE_BATCH = 8
ROW_UNROLL = 8


def _cparams(sem):
    return pltpu.CompilerParams(dimension_semantics=sem, vmem_limit_bytes=VMEM_LIMIT)


def _split_bf16(v):
    hi = v.astype(BF16)
    lo = (v - hi.astype(F32)).astype(BF16)
    return hi, lo


def _dot(a, b):
    return jnp.dot(a, b, preferred_element_type=F32)


def _dot_nt(a, b):
    return lax.dot_general(a, b, (((1,), (1,)), ((), ())), preferred_element_type=F32)


def _dot_tn(a, b):
    return lax.dot_general(a, b, (((0,), (0,)), ((), ())), preferred_element_type=F32)


def _inproj_kernel(x_ref, n1_ref, w_ref, gq_ref, gk_ref, qn_ref, kn_ref, wgk_ref, bgk_ref,
                   qa_ref, ka_ref, va_ref, qb_ref, kb_ref, vb_ref, la_ref, rb_ref, ga_ref, gb_ref):
    x = x_ref[...]
    ms = jnp.mean(x * x, axis=-1, keepdims=True)
    xn = (x * lax.rsqrt(ms + NORM_EPS) * n1_ref[...]).astype(BF16)

    def proj(name):
        off, width = _OFF[name]
        return _dot(xn, w_ref[:, off:off + width])

    def head_norm(v, ones_ref, gain_ref):
        hi, lo = _split_bf16(v * v)
        ss = _dot(hi, ones_ref[...]) + _dot(lo, ones_ref[...])
        return v * lax.rsqrt(ss * (1.0 / SWA_HEAD_DIM) + QK_EPS) * gain_ref[...]

    qa_ref[...] = (head_norm(proj("qa"), gq_ref, qn_ref) * ATTN_SCALE).astype(BF16)
    ka_ref[...] = head_norm(proj("ka"), gk_ref, kn_ref)
    va_ref[...] = proj("va")
    qb_ref[...] = (proj("qb") * GLA_SCALE).astype(BF16)
    kb_ref[...] = proj("kb").astype(BF16)
    vb_ref[...] = proj("vb").astype(BF16)
    z = _dot(proj("gl").astype(BF16), wgk_ref[...]) + bgk_ref[...]
    la_ref[...] = (jnp.minimum(z, 0.0) - jnp.log1p(jnp.exp(-jnp.abs(z)))) * (1.0 / GLA_GATE_NORM)
    rb = proj("rb")
    rb_ref[...] = (rb * jax.nn.sigmoid(rb)).astype(BF16)
    ga_ref[...] = jax.nn.sigmoid(proj("ga")).astype(BF16)
    gb_ref[...] = jax.nn.sigmoid(proj("gb")).astype(BF16)


def _inproj(x2, n1, w_packed, gq, gk, qn, kn, wgk, bgk):
    T = x2.shape[0]
    tm = min(TOK_TILE, T)
    assert T % tm == 0

    def tok(width):
        return pl.BlockSpec((tm, width), lambda i: (i, 0))

    def const(shape):
        return pl.BlockSpec(shape, lambda i: (0, 0))

    outs = (("qa", A_WIDTH, BF16), ("ka", KV_WIDTH, F32), ("va", KV_WIDTH, F32), ("qb", GLA_K_WIDTH, BF16),
            ("kb", GLA_K_WIDTH, BF16), ("vb", B_WIDTH, BF16), ("la", GLA_K_WIDTH, F32), ("rb", B_WIDTH, BF16),
            ("ga", D_MODEL, BF16), ("gb", D_MODEL, BF16))
    return pl.pallas_call(
        _inproj_kernel,
        grid=(T // tm,),
        in_specs=[tok(D_MODEL), const((1, D_MODEL)), const((D_MODEL, IN_PACKED)), const((A_WIDTH, A_WIDTH)),
                  const((KV_WIDTH, KV_WIDTH)), const((1, A_WIDTH)), const((1, KV_WIDTH)),
                  const((GL_PAD, GLA_K_WIDTH)), const((1, GLA_K_WIDTH))],
        out_specs=[tok(w) for _, w, _ in outs],
        out_shape=[jax.ShapeDtypeStruct((T, w), dt) for _, w, dt in outs],
        compiler_params=_cparams(("parallel",)),
        name="inproj",
    )(x2, n1, w_packed, gq, gk, qn, kn, wgk, bgk)


def _swa_kernel(sink_ref, q_ref, kc_ref, vc_ref, kp_ref, vp_ref, o_ref, *, first_block_has_past):
    n = pl.program_id(1)
    nb, lq = q_ref.shape[0], q_ref.shape[1]
    assert lq & (lq - 1) == 0
    stack = SWA_GROUP if lq < WINDOW else 1
    rows, keys = stack * lq, WINDOW + lq
    qi = lax.broadcasted_iota(I32, (rows, keys), 0) & (lq - 1)
    ci = lax.broadcasted_iota(I32, (rows, keys), 1)
    dist = WINDOW + qi - ci
    mask = (dist >= 0) & (dist < WINDOW)
    if not first_block_has_past:
        mask = mask & ((ci >= WINDOW) | (n > 0))
    for j in range(nb):
        q = q_ref[j]
        if stack > 1:
            q = q.astype(F32)
        k = jnp.concatenate([kp_ref[j], kc_ref[j]], axis=0).astype(BF16)
        v = jnp.concatenate([vp_ref[j], vc_ref[j]], axis=0).astype(BF16)
        outs = []
        for h0 in range(0, SWA_HEADS, stack):
            heads = range(h0, h0 + stack)
            kh = h0 // SWA_GROUP
            cols = slice(kh * SWA_HEAD_DIM, (kh + 1) * SWA_HEAD_DIM)
            qg = jnp.concatenate([q[:, h * SWA_HEAD_DIM:(h + 1) * SWA_HEAD_DIM] for h in heads], axis=0)
            sink = jnp.concatenate([jnp.full((lq, 1), sink_ref[h], F32) for h in heads], axis=0)
            s = jnp.where(mask, _dot_nt(qg.astype(BF16), k[:, cols]), NEG_INF)
            m = jnp.maximum(jnp.max(s, axis=-1, keepdims=True), sink)
            p = jnp.exp(s - m)
            denom = jnp.sum(p, axis=-1, keepdims=True) + jnp.exp(sink - m)
            o = _dot(p.astype(BF16), v[:, cols]) / denom
            outs += [o[g * lq:(g + 1) * lq] for g in range(stack)]
        o_ref[j] = jnp.concatenate(outs, axis=-1).astype(BF16)


def _swa(sinks, qa, ka, va, k_past, v_past, *, has_past):
    B, L, _ = qa.shape
    lq = min(L, WINDOW)
    nl = L // lq
    if has_past:
        assert nl == 1
        nb = min(B, SAMPLE_BATCH)
        past_spec = pl.BlockSpec((nb, WINDOW, KV_WIDTH), lambda b, n, s: (b, 0, 0))
        kp, vp = k_past, v_past
    else:
        assert lq == WINDOW
        nb = 1
        past_spec = pl.BlockSpec((nb, WINDOW, KV_WIDTH), lambda b, n, s: (b, jnp.maximum(n - 1, 0), 0))
        kp, vp = ka, va
    assert B % nb == 0
    cur = lambda w: pl.BlockSpec((nb, lq, w), lambda b, n, s: (b, n, 0))
    return pl.pallas_call(
        functools.partial(_swa_kernel, first_block_has_past=has_past),
        grid_spec=pltpu.PrefetchScalarGridSpec(
            num_scalar_prefetch=1,
            grid=(B // nb, nl),
            in_specs=[cur(A_WIDTH), cur(KV_WIDTH), cur(KV_WIDTH), past_spec, past_spec],
            out_specs=cur(A_WIDTH),
        ),
        out_shape=jax.ShapeDtypeStruct((B, L, A_WIDTH), BF16),
        compiler_params=_cparams(("parallel", "parallel")),
        name="swa",
    )(sinks, qa, ka, va, kp, vp)


def _gla_kernel(q_ref, k_ref, v_ref, la_ref, rb_ref, gn_ref, s0_ref, o_ref, sout_ref, s_scr, *, chunk, sub, length):
    C, SUB = chunk, sub
    S = C // SUB
    W = GLA_K_WIDTH
    nb = q_ref.shape[0]
    s_scr[...] = s0_ref[...]
    row = lax.broadcasted_iota(I32, (C, C), 0)
    col = lax.broadcasted_iota(I32, (C, C), 1)
    tri = (row >= col).astype(BF16)
    diag_mask = (row >= col) & ((row // SUB) == (col // SUB))
    eye = lax.broadcasted_iota(I32, (GLA_DK, GLA_DK), 0) == lax.broadcasted_iota(I32, (GLA_DK, GLA_DK), 1)
    krow = lax.broadcasted_iota(I32, (C, W), 0)

    def rows_of(x, r):
        return jnp.broadcast_to(x[r:r + 1, :], (SUB, W))

    def body(j, c):
        r0 = c * C if isinstance(c, int) else pl.multiple_of(c * C, C)
        q = q_ref[j, pl.ds(r0, C), :].astype(F32)
        k = k_ref[j, pl.ds(r0, C), :].astype(F32)
        v = v_ref[j, pl.ds(r0, C), :]
        g = la_ref[j, pl.ds(r0, C), :]
        g_hi, g_lo = _split_bf16(g)
        b = _dot(tri, g_hi) + _dot(tri, g_lo)
        b_last = b[C - 1:C, :]
        mid = jnp.concatenate([rows_of(b, i * SUB + SUB // 2 - 1) for i in range(S)], axis=0)
        beta = jnp.concatenate([jnp.zeros((SUB, W), F32)] + [rows_of(b, i * SUB - 1) for i in range(1, S)], axis=0)
        q_inter = (q * jnp.exp(b)).astype(BF16)
        k_state = (k * jnp.exp(b_last - b)).astype(BF16)
        q_diag = (q * jnp.exp(b - mid)).astype(BF16)
        k_diag = (k * jnp.exp(mid - b)).astype(BF16)
        q_off = (q * jnp.exp(b - beta)).astype(BF16)
        k_off = [None]
        for i in range(1, S):
            e = jnp.exp(jnp.minimum(b[i * SUB - 1:i * SUB, :] - b, 0.0))
            k_off.append(jnp.where(krow < i * SUB, k * e, 0.0).astype(BF16))
        outs = []
        for h in range(GLA_HEADS):
            ks = slice(h * GLA_DK, (h + 1) * GLA_DK)
            vs = slice(h * GLA_DV, (h + 1) * GLA_DV)
            a_diag = _dot_nt(q_diag[:, ks], k_diag[:, ks])
            if S > 1:
                blocks = [jnp.zeros((SUB, C), F32)]
                for i in range(1, S):
                    blocks.append(_dot_nt(q_off[i * SUB:(i + 1) * SUB, ks], k_off[i][:, ks]))
                a = jnp.where(diag_mask, a_diag, jnp.concatenate(blocks, axis=0))
            else:
                a = jnp.where(diag_mask, a_diag, 0.0)
            s_h = s_scr[j, h]
            v_h = v[:, vs]
            o_h = _dot(a.astype(BF16), v_h) + _dot(q_inter[:, ks], s_h.astype(BF16))
            dec = jnp.exp(b_last[:, ks])
            dec_col = jnp.sum(jnp.where(eye, jnp.broadcast_to(dec, (GLA_DK, GLA_DK)), 0.0), axis=-1, keepdims=True)
            s_scr[j, h] = dec_col * s_h + _dot_tn(k_state[:, ks], v_h)
            ms = jnp.mean(o_h * o_h, axis=-1, keepdims=True)
            outs.append(o_h * lax.rsqrt(ms + NORM_EPS) * gn_ref[...])
        o = jnp.concatenate(outs, axis=-1) * rb_ref[j, pl.ds(r0, C), :].astype(F32)
        o_ref[j, pl.ds(r0, C), :] = o.astype(BF16)

    for j in range(nb):
        if length == C:
            body(j, 0)
        else:
            lax.fori_loop(0, length // C, lambda c, carry, j=j: (body(j, c), carry)[1], 0)
    sout_ref[...] = s_scr[...]


def _gla(qb, kb, vb, la, rb, gn, s0):
    B, L, _ = qb.shape
    chunk = GLA_CHUNK if L % GLA_CHUNK == 0 else L
    sub = GLA_SUB if chunk % GLA_SUB == 0 else chunk
    nb = min(B, SAMPLE_BATCH) if L == chunk else 1
    assert B % nb == 0
    seq = lambda w: pl.BlockSpec((nb, L, w), lambda b: (b, 0, 0))
    st = pl.BlockSpec((nb, GLA_HEADS, GLA_DK, GLA_DV), lambda b: (b, 0, 0, 0))
    return pl.pallas_call(
        functools.partial(_gla_kernel, chunk=chunk, sub=sub, length=L),
        grid=(B // nb,),
        in_specs=[seq(GLA_K_WIDTH), seq(GLA_K_WIDTH), seq(B_WIDTH), seq(GLA_K_WIDTH), seq(B_WIDTH),
                  pl.BlockSpec((1, GLA_DV), lambda b: (0, 0)), st],
        out_specs=[seq(B_WIDTH), st],
        out_shape=[jax.ShapeDtypeStruct((B, L, B_WIDTH), BF16),
                   jax.ShapeDtypeStruct((B, GLA_HEADS, GLA_DK, GLA_DV), F32)],
        scratch_shapes=[pltpu.VMEM((nb, GLA_HEADS, GLA_DK, GLA_DV), F32)],
        compiler_params=_cparams(("parallel",)),
        name="gla",
    )(qb, kb, vb, la, rb, gn, s0)


def _outproj_kernel(x_ref, oa_ref, ob_ref, ga_ref, gb_ref, wa_ref, wb_ref, wo_ref, n2_ref, wr_ref, br_ref, upper_ref,
                    h_ref, hp_ref, e_ref, g_ref, rank_ref, cnt_ref, carry):
    i = pl.program_id(0)

    @pl.when(i == 0)
    def _():
        carry[...] = jnp.zeros_like(carry)

    tm = x_ref.shape[0]
    merged = (ga_ref[...].astype(F32) * _dot(oa_ref[...], wa_ref[...])
              + gb_ref[...].astype(F32) * _dot(ob_ref[...], wb_ref[...]))
    h = x_ref[...] + _dot(merged.astype(BF16), wo_ref[...])
    h_ref[...] = h
    ms = jnp.mean(h * h, axis=-1, keepdims=True)
    hn = h * lax.rsqrt(ms + NORM_EPS) * n2_ref[...]
    half = D_MODEL // 2
    lo_bits = pltpu.bitcast(hn[:, :half].astype(BF16).astype(F32), U32) >> 16
    hi_bits = pltpu.bitcast(hn[:, half:].astype(BF16).astype(F32), U32) & jnp.uint32(0xFFFF0000)
    hp_ref[...] = lo_bits | hi_bits
    hn_hi, hn_lo = _split_bf16(hn)
    wr_hi, wr_lo = _split_bf16(wr_ref[...])
    logits = _dot(hn_hi, wr_hi) + _dot(hn_lo, wr_hi) + _dot(hn_hi, wr_lo) + br_ref[...]
    lt = logits.T[:N_EXPERTS, :]
    eid = lax.broadcasted_iota(I32, (N_EXPERTS, tm), 0)
    vals, hots = [], []
    for k in range(TOP_K):
        m = jnp.max(lt, axis=0, keepdims=True)
        idx = jnp.min(jnp.where(lt == m, eid, N_EXPERTS), axis=0, keepdims=True)
        hot = eid == idx
        lt = jnp.where(hot, -jnp.inf, lt)
        vals.append(m)
        hots.append(hot)
        e_ref[k:k + 1, :] = idx
    ex = [jnp.exp(vk - vals[0]) for vk in vals]
    den = ex[0] + ex[1] + ex[2] + ex[3]
    for k in range(TOP_K):
        g_ref[k:k + 1, :] = ex[k] / den
    multi = (hots[0] | hots[1] | hots[2] | hots[3])
    before = _dot(multi.astype(BF16), upper_ref[...]) + carry[...]
    for k in range(TOP_K):
        rank_ref[k:k + 1, :] = jnp.sum(jnp.where(hots[k], before, 0.0), axis=0, keepdims=True).astype(I32)
    carry[...] = carry[...] + jnp.sum(multi.astype(F32), axis=1, keepdims=True)
    cnt_ref[...] = jnp.broadcast_to(carry[...], cnt_ref.shape).astype(I32)


def _outproj(x2, oa, ob, ga, gb, wa, wb, wo, n2, wr, br, upper):
    T = x2.shape[0]
    tm = min(TOK_TILE, T)
    assert T % tm == 0 and upper.shape == (tm, tm)
    tok = lambda w: pl.BlockSpec((tm, w), lambda i: (i, 0))
    const = lambda shape: pl.BlockSpec(shape, lambda i: (0, 0))
    kt = pl.BlockSpec((TOP_K, tm), lambda i: (0, i))
    return pl.pallas_call(
        _outproj_kernel,
        grid=(T // tm,),
        in_specs=[tok(D_MODEL), tok(A_WIDTH), tok(B_WIDTH), tok(D_MODEL), tok(D_MODEL),
                  const((A_WIDTH, D_MODEL)), const((B_WIDTH, D_MODEL)), const((D_MODEL, D_MODEL)),
                  const((1, D_MODEL)), const((D_MODEL, LANE)), const((1, LANE)), const((tm, tm))],
        out_specs=[tok(D_MODEL), tok(D_MODEL // 2), kt, kt, kt, const((N_EXPERTS, LANE))],
        out_shape=[jax.ShapeDtypeStruct((T, D_MODEL), F32), jax.ShapeDtypeStruct((T, D_MODEL // 2), U32),
                   jax.ShapeDtypeStruct((TOP_K, T), I32), jax.ShapeDtypeStruct((TOP_K, T), F32),
                   jax.ShapeDtypeStruct((TOP_K, T), I32), jax.ShapeDtypeStruct((N_EXPERTS, LANE), I32)],
        scratch_shapes=[pltpu.VMEM((N_EXPERTS, 1), F32)],
        compiler_params=_cparams(("arbitrary",)),
        name="outproj",
    )(x2, oa, ob, ga, gb, wa, wb, wo, n2, wr, br, upper)


def _start_row_copies(row_copy, n_tokens):
    def issue(g, c):
        for u in range(ROW_UNROLL):
            for k in range(TOP_K):
                row_copy(g * ROW_UNROLL + u, k).start(priority=k % 2)
        return c

    lax.fori_loop(0, n_tokens // ROW_UNROLL, issue, 0)


def _dispatch_kernel(fill_ref, pos_ref, hp_p_ref, hp_s_ref, xs_ref, sem, zbuf, zsem, *, steps_p):
    i = pl.program_id(0)
    tm = hp_p_ref.shape[0]

    @pl.when(i == 0)
    def _():
        zbuf[...] = jnp.zeros_like(zbuf)

        def fill_copy(j):
            return pltpu.make_async_copy(zbuf, xs_ref.at[pl.ds(pl.multiple_of(fill_ref[j], MOE_BM), MOE_BM), :], zsem)

        for j in range(fill_ref.shape[0]):
            pl.when(fill_ref[j] >= 0)(lambda j=j: fill_copy(j).start())
        for j in range(fill_ref.shape[0]):
            pl.when(fill_ref[j] >= 0)(lambda j=j: fill_copy(j).wait())

    def scatter(hp_ref):
        def row_copy(t, k):
            return pltpu.make_async_copy(hp_ref.at[pl.ds(t, 1), :], xs_ref.at[pl.ds(pos_ref[k, t], 1), :], sem)

        _start_row_copies(row_copy, tm)
        for k in range(TOP_K):
            pltpu.make_async_copy(hp_ref, xs_ref.at[pl.ds(0, tm), :], sem).wait()

    pl.when(i < steps_p)(lambda: scatter(hp_p_ref))
    pl.when(i >= steps_p)(lambda: scatter(hp_s_ref))


def _dispatch(fill, pos, hp_p, hp_s, n_rows):
    tm = DISP_TILE
    assert hp_p.shape[0] % tm == 0 and hp_s.shape[0] % tm == 0
    steps_p, steps_s = hp_p.shape[0] // tm, hp_s.shape[0] // tm
    return pl.pallas_call(
        functools.partial(_dispatch_kernel, steps_p=steps_p),
        grid_spec=pltpu.PrefetchScalarGridSpec(
            num_scalar_prefetch=1,
            grid=(steps_p + steps_s,),
            in_specs=[pl.BlockSpec((TOP_K, tm), lambda i, fl: (0, i), memory_space=pltpu.SMEM),
                      pl.BlockSpec((tm, D_MODEL // 2), lambda i, fl: (jnp.minimum(i, steps_p - 1), 0)),
                      pl.BlockSpec((tm, D_MODEL // 2), lambda i, fl: (jnp.maximum(i - steps_p, 0), 0))],
            out_specs=pl.BlockSpec(memory_space=pl.ANY),
            scratch_shapes=[pltpu.SemaphoreType.DMA, pltpu.VMEM((MOE_BM, D_MODEL // 2), U32),
                            pltpu.SemaphoreType.DMA],
        ),
        out_shape=jax.ShapeDtypeStruct((n_rows, D_MODEL // 2), U32),
        compiler_params=_cparams(("arbitrary",)),
        name="dispatch",
    )(fill, pos, hp_p, hp_s)


def _experts_kernel(be_ref, nb_ref, xs_ref, w1_ref, b1g_ref, b1l_ref, w2_ref, b2_ref, sel_ref, y_ref,
                    w1g_scr, w1l_scr, w2_scr):
    i = pl.program_id(0)
    live = i < nb_ref[0]

    @pl.when(jnp.logical_not(live))
    def _():
        y_ref[...] = jnp.zeros_like(y_ref)

    @pl.when(live & ((i == 0) | (be_ref[i] != be_ref[jnp.maximum(i - 1, 0)])))
    def _():
        for c in range(D_FF // LANE):
            cols = _dot(w1_ref[0, :, c * 2 * LANE:(c + 1) * 2 * LANE].astype(BF16), sel_ref[...])
            w1g_scr[:, c * LANE:(c + 1) * LANE] = cols[:, :LANE].astype(BF16)
            w1l_scr[:, c * LANE:(c + 1) * LANE] = cols[:, LANE:].astype(BF16)
        w2_scr[...] = w2_ref[0].astype(BF16)

    @pl.when(live)
    def _():
        half = D_MODEL // 2
        w = xs_ref[...]
        x_lo = pltpu.bitcast(w << 16, F32).astype(BF16)
        x_hi = pltpu.bitcast(w & jnp.uint32(0xFFFF0000), F32).astype(BF16)

        def up(w_scr, b_ref):
            return _dot(x_lo, w_scr[:half, :]) + _dot(x_hi, w_scr[half:, :]) + b_ref[0]

        x_glu = jnp.minimum(up(w1g_scr, b1g_ref), SWIGLU_LIMIT)
        x_lin = jnp.clip(up(w1l_scr, b1l_ref), -SWIGLU_LIMIT, SWIGLU_LIMIT)
        act = x_glu * jax.nn.sigmoid(SWIGLU_ALPHA * x_glu) * (x_lin + 1.0)
        y_ref[...] = _dot(act.astype(BF16), w2_scr[...]) + b2_ref[0]


def _experts(blk_e, n_used, xs, w1, b1g, b1l, w2, b2, sel):
    n_rows = xs.shape[0]
    nblk = n_rows // MOE_BM
    wspec = lambda r, c: pl.BlockSpec((1, r, c), lambda i, be, nb: (be[i], 0, 0))
    return pl.pallas_call(
        _experts_kernel,
        grid_spec=pltpu.PrefetchScalarGridSpec(
            num_scalar_prefetch=2,
            grid=(nblk,),
            in_specs=[pl.BlockSpec((MOE_BM, D_MODEL // 2), lambda i, be, nb: (i, 0)),
                      wspec(D_MODEL, 2 * D_FF), wspec(1, D_FF), wspec(1, D_FF),
                      wspec(D_FF, D_MODEL), wspec(1, D_MODEL),
                      pl.BlockSpec((2 * LANE, 2 * LANE), lambda i, be, nb: (0, 0))],
            out_specs=pl.BlockSpec((MOE_BM, D_MODEL), lambda i, be, nb: (i, 0)),
            scratch_shapes=[pltpu.VMEM((D_MODEL, D_FF), BF16), pltpu.VMEM((D_MODEL, D_FF), BF16),
                            pltpu.VMEM((D_FF, D_MODEL), BF16)],
        ),
        out_shape=jax.ShapeDtypeStruct((n_rows, D_MODEL), F32),
        compiler_params=_cparams(("arbitrary",)),
        name="experts",
    )(blk_e, n_used, xs, w1, b1g, b1l, w2, b2, sel)


def _combine_kernel(pos_ref, h_ref, g_ref, yb_ref, y_ref, buf, sem):
    tm = h_ref.shape[0]

    def row_copy(t, k):
        return pltpu.make_async_copy(yb_ref.at[pl.ds(pos_ref[k, t], 1), :], buf.at[k, pl.ds(t, 1), :], sem)

    _start_row_copies(row_copy, tm)
    for k in range(TOP_K):
        pltpu.make_async_copy(yb_ref.at[pl.ds(0, tm), :], buf.at[k], sem).wait()
    g = g_ref[...]
    y = h_ref[...]
    for k in range(TOP_K):
        y = y + g[:, k:k + 1] * buf[k]
    y_ref[...] = y


def _combine(pos, h, gate_t, yb):
    T = h.shape[0]
    tm = COMB_TILE
    assert T % tm == 0
    return pl.pallas_call(
        _combine_kernel,
        grid=(T // tm,),
        in_specs=[pl.BlockSpec((TOP_K, tm), lambda i: (0, i), memory_space=pltpu.SMEM),
                  pl.BlockSpec((tm, D_MODEL), lambda i: (i, 0)),
                  pl.BlockSpec((tm, TOP_K), lambda i: (i, 0)),
                  pl.BlockSpec(memory_space=pl.ANY)],
        out_specs=pl.BlockSpec((tm, D_MODEL), lambda i: (i, 0)),
        out_shape=jax.ShapeDtypeStruct((T, D_MODEL), F32),
        scratch_shapes=[pltpu.VMEM((TOP_K, tm, D_MODEL), F32), pltpu.SemaphoreType.DMA],
        compiler_params=_cparams(("arbitrary",)),
        name="combine",
    )(pos, h, gate_t, yb)


def _block_ones(width, block):
    idx = np.arange(width) // block
    return jnp.asarray(idx[:, None] == idx[None, :], BF16)


def _layer(x_p, x_s, cache_k, cache_v, s_gla, norm1, w_in, q_norm, k_norm, sinks, w_gk, b_gk,
           gla_norm, w_a, w_b, w_o, norm2, w_router, b_router, w1, b1, w2, b2):
    Bp, Lp, _ = x_p.shape
    Bs, Ls, _ = x_s.shape
    Tp, Ts = Bp * Lp, Bs * Ls

    offs = np.cumsum((0, A_WIDTH, KV_WIDTH, KV_WIDTH, GLA_K_WIDTH, GLA_K_WIDTH, B_WIDTH, GLA_GATE_RANK, B_WIDTH,
                      D_MODEL, D_MODEL))
    src = dict(zip(("qa", "ka", "va", "qb", "kb", "vb", "gl", "rb", "ga", "gb"), zip(offs[:-1], offs[1:])))
    cols = []
    for name, width in _SEGS:
        a, b = src[name]
        piece = w_in[:, a:b]
        if b - a < width:
            piece = jnp.pad(piece, ((0, 0), (0, width - (b - a))))
        cols.append(piece)
    w_packed = jnp.concatenate(cols, axis=1).astype(BF16)
    wgk = jnp.pad(w_gk, ((0, GL_PAD - GLA_GATE_RANK), (0, 0))).astype(BF16)
    n1 = norm1.reshape(1, D_MODEL)
    gq, gk = _block_ones(A_WIDTH, SWA_HEAD_DIM), _block_ones(KV_WIDTH, SWA_HEAD_DIM)
    qn = jnp.tile(q_norm, SWA_HEADS).reshape(1, A_WIDTH)
    kn = jnp.tile(k_norm, SWA_KV_HEADS).reshape(1, KV_WIDTH)
    bgk = b_gk.reshape(1, GLA_K_WIDTH)
    gn = gla_norm.reshape(1, GLA_DV)
    wa, wb, wo = w_a.astype(BF16), w_b.astype(BF16), w_o.astype(BF16)
    n2 = norm2.reshape(1, D_MODEL)
    wr = jnp.pad(w_router, ((0, 0), (0, LANE - N_EXPERTS)))
    br = jnp.pad(b_router, (0, LANE - N_EXPERTS)).reshape(1, LANE)
    b1g = b1[:, 0::2].reshape(N_EXPERTS, 1, D_FF)
    b1l = b1[:, 1::2].reshape(N_EXPERTS, 1, D_FF)
    b2r = b2.reshape(N_EXPERTS, 1, D_MODEL)
    sel_np = np.zeros((2 * LANE, 2 * LANE), np.float32)
    sel_np[2 * np.arange(LANE), np.arange(LANE)] = 1.0
    sel_np[2 * np.arange(LANE) + 1, LANE + np.arange(LANE)] = 1.0
    sel = jnp.asarray(sel_np, BF16)

    def mix(x, k_past, v_past, s0):
        B, L, _ = x.shape
        T = B * L
        x2 = x.reshape(T, D_MODEL)
        qa, ka, va, qb, kb, vb, la, rb, ga, gb = _inproj(x2, n1, w_packed, gq, gk, qn, kn, wgk, bgk)
        r3 = lambda t: t.reshape(B, L, t.shape[-1])
        oa = _swa(sinks, r3(qa), r3(ka), r3(va), k_past, v_past, has_past=k_past is not None)
        ob, s_out = _gla(r3(qb), r3(kb), r3(vb), r3(la), r3(rb), gn, s0)
        tm = min(TOK_TILE, T)
        upper = jnp.asarray(np.arange(tm)[:, None] < np.arange(tm)[None, :], BF16)
        h, hp, top_e, gate, rank, cnt = _outproj(x2, oa.reshape(T, A_WIDTH), ob.reshape(T, B_WIDTH), ga, gb,
                                                 wa, wb, wo, n2, wr, br, upper)
        return r3(ka), r3(va), s_out, h, hp, top_e, gate, rank, cnt[:, 0]

    win = cache_k.shape[1]
    assert win == WINDOW
    ck = cache_k.reshape(Bs, win, KV_WIDTH)
    cv = cache_v.reshape(Bs, win, KV_WIDTH)
    ka_p, va_p, s_p, h_p, hp_p, e_p, g_p, rank_p, cnt_p = mix(
        x_p, None, None, jnp.zeros((Bp, GLA_HEADS, GLA_DK, GLA_DV), F32))
    ka_s, va_s, s_s, h_s, hp_s, e_s, g_s, rank_s, cnt_s = mix(x_s, ck, cv, s_gla)

    T = Tp + Ts
    n_rows = -(-(T * TOP_K + N_EXPERTS * (MOE_BM - 1)) // MOE_BM) * MOE_BM
    counts = cnt_p + cnt_s
    padded = (counts + MOE_BM - 1) // MOE_BM * MOE_BM
    pad_end = jnp.cumsum(padded)
    start = pad_end - padded

    def lookup(table, idx):
        hot = idx[None] == jnp.arange(N_EXPERTS, dtype=I32)[:, None, None]
        return jnp.sum(jnp.where(hot, table[:, None, None], 0), axis=0)

    pos_p = lookup(start, e_p) + rank_p
    pos_s = lookup(start + cnt_p, e_s) + rank_s
    blk_row = jnp.arange(n_rows // MOE_BM, dtype=I32) * MOE_BM
    blk_e = jnp.minimum(jnp.sum(pad_end[None, :] <= blk_row[:, None], axis=1), N_EXPERTS - 1).astype(I32)
    n_used = (pad_end[-1:] // MOE_BM).astype(I32)
    tails = jnp.where(padded > 0, pad_end - MOE_BM, -1)
    spare = pad_end[-1] + blk_row[:N_EXPERTS]
    fill = jnp.concatenate([tails, jnp.where(spare < n_rows, spare, -1)]).astype(I32)

    xs = _dispatch(fill, jnp.concatenate([pos_p, pos_s], axis=1), hp_p, hp_s, n_rows)
    yb = _experts(blk_e, n_used, xs, w1, b1g, b1l, w2, b2r, sel)
    y_p = _combine(pos_p, h_p, g_p.T, yb).reshape(Bp, Lp, D_MODEL)
    y_s = _combine(pos_s, h_s, g_s.T, yb).reshape(Bs, Ls, D_MODEL)

    kv = lambda t: t.reshape(t.shape[0], t.shape[1], SWA_KV_HEADS, SWA_HEAD_DIM)
    k_p, v_p = kv(ka_p[:, -WINDOW:]), kv(va_p[:, -WINDOW:])
    k_s = jnp.concatenate([cache_k, kv(ka_s)], axis=1)[:, -win:]
    v_s = jnp.concatenate([cache_v, kv(va_s)], axis=1)[:, -win:]
    return y_p, y_s, (k_p, v_p, s_p, k_s, v_s, s_s)


def kernel(x_prompt, x_sample, cache_swa_k, cache_swa_v, state_gla, norm1, w_in, q_norm, k_norm, sinks, w_gk, b_gk,
           gla_norm, w_a, w_b, w_o, norm2, w_router, b_router, w1, b1, w2, b2):
    depth = norm1.shape[0]
    y_p, y_s = x_prompt, x_sample
    states = []
    for l in range(depth):
        y_p, y_s, st = _layer(y_p, y_s, cache_swa_k[l], cache_swa_v[l], state_gla[l], norm1[l], w_in[l], q_norm[l],
                              k_norm[l], sinks[l], w_gk[l], b_gk[l], gla_norm[l], w_a[l], w_b[l], w_o[l], norm2[l],
                              w_router[l], b_router[l], w1[l], b1[l], w2[l], b2[l])
        states.append(st)
    return (y_p, y_s) + tuple(jnp.stack([st[j] for st in states]) for j in range(6))
```

```python
import functools

import jax
import jax.numpy as jnp
import numpy as np
from jax import lax
from jax.experimental import pallas as pl
from jax.experimental.pallas import tpu as pltpu

F32 = jnp.float32
BF16 = jnp.bfloat16
I32 = jnp.int32
U32 = jnp.uint32

D_MODEL = 1024
SWA_HEADS = 8
SWA_KV_HEADS = 2
SWA_GROUP = SWA_HEADS // SWA_KV_HEADS
SWA_HEAD_DIM = 64
WINDOW = 128
ATTN_SCALE = SWA_HEAD_DIM ** -0.5
GLA_HEADS = 4
GLA_DK = 64
GLA_DV = 128
GLA_GATE_RANK = 16
GLA_GATE_NORM = 16.0
GLA_SCALE = GLA_DK ** -0.5
A_WIDTH = SWA_HEADS * SWA_HEAD_DIM
KV_WIDTH = SWA_KV_HEADS * SWA_HEAD_DIM
GLA_K_WIDTH = GLA_HEADS * GLA_DK
B_WIDTH = GLA_HEADS * GLA_DV
N_EXPERTS = 32
TOP_K = 4
D_FF = D_MODEL
SWIGLU_ALPHA = 1.702
SWIGLU_LIMIT = 7.0
NORM_EPS = 1e-5
QK_EPS = 1e-6
NEG_INF = -1e30

LANE = 128
GL_PAD = LANE
VMEM_LIMIT = 56 * 1024 * 1024

_SEGS = (("qa", A_WIDTH), ("ka", KV_WIDTH), ("va", KV_WIDTH), ("qb", GLA_K_WIDTH), ("kb", GLA_K_WIDTH),
         ("vb", B_WIDTH), ("rb", B_WIDTH), ("ga", D_MODEL), ("gb", D_MODEL), ("gl", GL_PAD))
_OFF = {}
_o = 0
for _n, _w in _SEGS:
    _OFF[_n] = (_o, _w)
    _o += _w
IN_PACKED = _o

TOK_TILE = 512
GLA_CHUNK = 64
GLA_SUB = 16
MOE_BM = 256
DISP_TILE = 256
COMB_TILE = 128
SAMPLE_BATCH = 8
ROW_UNROLL = 8
GLA_LONG_BATCH = 2


def _cparams(sem):
    return pltpu.CompilerParams(dimension_semantics=sem, vmem_limit_bytes=VMEM_LIMIT)


def _split_bf16(v):
    hi = v.astype(BF16)
    lo = (v - hi.astype(F32)).astype(BF16)
    return hi, lo


def _dot(a, b):
    return jnp.dot(a, b, preferred_element_type=F32)


def _dot_nt(a, b):
    return lax.dot_general(a, b, (((1,), (1,)), ((), ())), preferred_element_type=F32)


def _dot_tn(a, b):
    return lax.dot_general(a, b, (((0,), (0,)), ((), ())), preferred_element_type=F32)


def _inproj_kernel(x_ref, n1_ref, w_ref, gq_ref, gk_ref, qn_ref, kn_ref, wgk_ref, bgk_ref,
                   qa_ref, ka_ref, va_ref, qb_ref, kb_ref, vb_ref, la_ref, rb_ref, ga_ref, gb_ref):
    x = x_ref[...]
    ms = jnp.mean(x * x, axis=-1, keepdims=True)
    xn = (x * lax.rsqrt(ms + NORM_EPS) * n1_ref[...]).astype(BF16)

    def proj(name):
        off, width = _OFF[name]
        return _dot(xn, w_ref[:, off:off + width])

    def head_norm(v, ones_ref, gain_ref):
        hi, lo = _split_bf16(v * v)
        ss = _dot(hi, ones_ref[...]) + _dot(lo, ones_ref[...])
        return v * lax.rsqrt(ss * (1.0 / SWA_HEAD_DIM) + QK_EPS) * gain_ref[...]

    qa_ref[...] = (head_norm(proj("qa"), gq_ref, qn_ref) * ATTN_SCALE).astype(BF16)
    ka_ref[...] = head_norm(proj("ka"), gk_ref, kn_ref)
    va_ref[...] = proj("va")
    qb_ref[...] = (proj("qb") * GLA_SCALE).astype(BF16)
    kb_ref[...] = proj("kb").astype(BF16)
    vb_ref[...] = proj("vb").astype(BF16)
    z = _dot(proj("gl").astype(BF16), wgk_ref[...]) + bgk_ref[...]
    la_ref[...] = (jnp.minimum(z, 0.0) - jnp.log1p(jnp.exp(-jnp.abs(z)))) * (1.0 / GLA_GATE_NORM)
    rb = proj("rb")
    rb_ref[...] = (rb * jax.nn.sigmoid(rb)).astype(BF16)
    ga_ref[...] = jax.nn.sigmoid(proj("ga")).astype(BF16)
    gb_ref[...] = jax.nn.sigmoid(proj("gb")).astype(BF16)


def _inproj(x2, n1, w_packed, gq, gk, qn, kn, wgk, bgk):
    T = x2.shape[0]
    tm = min(TOK_TILE, T)
    assert T % tm == 0

    def tok(width):
        return pl.BlockSpec((tm, width), lambda i: (i, 0))

    def const(shape):
        return pl.BlockSpec(shape, lambda i: (0, 0))

    outs = (("qa", A_WIDTH, BF16), ("ka", KV_WIDTH, F32), ("va", KV_WIDTH, F32), ("qb", GLA_K_WIDTH, BF16),
            ("kb", GLA_K_WIDTH, BF16), ("vb", B_WIDTH, BF16), ("la", GLA_K_WIDTH, F32), ("rb", B_WIDTH, BF16),
            ("ga", D_MODEL, BF16), ("gb", D_MODEL, BF16))
    return pl.pallas_call(
        _inproj_kernel,
        grid=(T // tm,),
        in_specs=[tok(D_MODEL), const((1, D_MODEL)), const((D_MODEL, IN_PACKED)), const((A_WIDTH, A_WIDTH)),
                  const((KV_WIDTH, KV_WIDTH)), const((1, A_WIDTH)), const((1, KV_WIDTH)),
                  const((GL_PAD, GLA_K_WIDTH)), const((1, GLA_K_WIDTH))],
        out_specs=[tok(w) for _, w, _ in outs],
        out_shape=[jax.ShapeDtypeStruct((T, w), dt) for _, w, dt in outs],
        compiler_params=_cparams(("parallel",)),
        name="inproj",
    )(x2, n1, w_packed, gq, gk, qn, kn, wgk, bgk)


def _swa_kernel(sink_ref, q_ref, kc_ref, vc_ref, kp_ref, vp_ref, o_ref, *, first_block_has_past):
    n = pl.program_id(1)
    nb, lq = q_ref.shape[0], q_ref.shape[1]
    assert lq & (lq - 1) == 0
    stack = SWA_GROUP if lq < WINDOW else 1
    rows, keys = stack * lq, WINDOW + lq
    qi = lax.broadcasted_iota(I32, (rows, keys), 0) & (lq - 1)
    ci = lax.broadcasted_iota(I32, (rows, keys), 1)
    dist = WINDOW + qi - ci
    mask = (dist >= 0) & (dist < WINDOW)
    if not first_block_has_past:
        mask = mask & ((ci >= WINDOW) | (n > 0))
    cols = lambda h0: slice(h0 // SWA_GROUP * SWA_HEAD_DIM, (h0 // SWA_GROUP + 1) * SWA_HEAD_DIM)
    qs, ks, vs = [], [], []
    for j in range(nb):
        q = q_ref[j]
        qs.append(q.astype(F32) if stack > 1 else q)
        ks.append(jnp.concatenate([kp_ref[j], kc_ref[j]], axis=0).astype(BF16))
        vs.append(jnp.concatenate([vp_ref[j], vc_ref[j]], axis=0).astype(BF16))
    groups = [(j, h0) for j in range(nb) for h0 in range(0, SWA_HEADS, stack)]
    wave = len(groups) if stack > 1 else 1
    outs = {}
    for w0 in range(0, len(groups), wave):
        scores, sinks = [], []
        for j, h0 in groups[w0:w0 + wave]:
            heads = range(h0, h0 + stack)
            qg = jnp.concatenate([qs[j][:, h * SWA_HEAD_DIM:(h + 1) * SWA_HEAD_DIM] for h in heads], axis=0)
            sinks.append(jnp.concatenate([jnp.full((lq, 1), sink_ref[h], F32) for h in heads], axis=0))
            scores.append(_dot_nt(qg.astype(BF16), ks[j][:, cols(h0)]))
        probs, denoms = [], []
        for s, sink in zip(scores, sinks):
            s = jnp.where(mask, s, NEG_INF)
            m = jnp.maximum(jnp.max(s, axis=-1, keepdims=True), sink)
            p = jnp.exp(s - m)
            denoms.append(jnp.sum(p, axis=-1, keepdims=True) + jnp.exp(sink - m))
            probs.append(p.astype(BF16))
        for (j, h0), p, denom in zip(groups[w0:w0 + wave], probs, denoms):
            o = _dot(p, vs[j][:, cols(h0)]) / denom
            outs.setdefault(j, []).extend(o[g * lq:(g + 1) * lq] for g in range(stack))
    for j in range(nb):
        o_ref[j] = jnp.concatenate(outs[j], axis=-1).astype(BF16)


def _swa(sinks, qa, ka, va, k_past, v_past, *, has_past):
    B, L, _ = qa.shape
    lq = min(L, WINDOW)
    nl = L // lq
    if has_past:
        assert nl == 1
        nb = min(B, SAMPLE_BATCH)
        past_spec = pl.BlockSpec((nb, WINDOW, KV_WIDTH), lambda b, n, s: (b, 0, 0))
        kp, vp = k_past, v_past
    else:
        assert lq == WINDOW
        nb = 1
        past_spec = pl.BlockSpec((nb, WINDOW, KV_WIDTH), lambda b, n, s: (b, jnp.maximum(n - 1, 0), 0))
        kp, vp = ka, va
    assert B % nb == 0
    cur = lambda w: pl.BlockSpec((nb, lq, w), lambda b, n, s: (b, n, 0))
    return pl.pallas_call(
        functools.partial(_swa_kernel, first_block_has_past=has_past),
        grid_spec=pltpu.PrefetchScalarGridSpec(
            num_scalar_prefetch=1,
            grid=(B // nb, nl),
            in_specs=[cur(A_WIDTH), cur(KV_WIDTH), cur(KV_WIDTH), past_spec, past_spec],
            out_specs=cur(A_WIDTH),
        ),
        out_shape=jax.ShapeDtypeStruct((B, L, A_WIDTH), BF16),
        compiler_params=_cparams(("parallel", "parallel")),
        name="swa",
    )(sinks, qa, ka, va, kp, vp)


def _gla_kernel(q_ref, k_ref, v_ref, la_ref, rb_ref, gn_ref, s0_ref, o_ref, sout_ref, s_scr, *, chunk, sub, length):
    C, SUB = chunk, sub
    S = C // SUB
    W = GLA_K_WIDTH
    nb = q_ref.shape[0]
    s_scr[...] = s0_ref[...]
    row = lax.broadcasted_iota(I32, (C, C), 0)
    col = lax.broadcasted_iota(I32, (C, C), 1)
    tri = (row >= col).astype(BF16)
    diag_mask = (row >= col) & ((row // SUB) == (col // SUB))
    eye = lax.broadcasted_iota(I32, (GLA_DK, GLA_DK), 0) == lax.broadcasted_iota(I32, (GLA_DK, GLA_DK), 1)
    krow = lax.broadcasted_iota(I32, (C, W), 0)

    def rows_of(x, r):
        return jnp.broadcast_to(x[r:r + 1, :], (SUB, W))

    ksl = lambda h: slice(h * GLA_DK, (h + 1) * GLA_DK)
    vsl = lambda h: slice(h * GLA_DV, (h + 1) * GLA_DV)

    def decays(j, r0):
        g_hi, g_lo = _split_bf16(la_ref[j, pl.ds(r0, C), :])
        return _dot(tri, g_hi) + _dot(tri, g_lo)

    def factors(j, r0, b):
        q = q_ref[j, pl.ds(r0, C), :].astype(F32)
        k = k_ref[j, pl.ds(r0, C), :].astype(F32)
        b_last = b[C - 1:C, :]
        mid = jnp.concatenate([rows_of(b, i * SUB + SUB // 2 - 1) for i in range(S)], axis=0)
        beta = jnp.concatenate([jnp.zeros((SUB, W), F32)] + [rows_of(b, i * SUB - 1) for i in range(1, S)], axis=0)
        k_off = [None]
        for i in range(1, S):
            e = jnp.exp(jnp.minimum(b[i * SUB - 1:i * SUB, :] - b, 0.0))
            k_off.append(jnp.where(krow < i * SUB, k * e, 0.0).astype(BF16))
        return dict(q_inter=(q * jnp.exp(b)).astype(BF16), k_state=(k * jnp.exp(b_last - b)).astype(BF16),
                    q_diag=(q * jnp.exp(b - mid)).astype(BF16), k_diag=(k * jnp.exp(mid - b)).astype(BF16),
                    q_off=(q * jnp.exp(b - beta)).astype(BF16), k_off=k_off, dec=jnp.exp(b_last))

    def intra(f, h):
        a_diag = _dot_nt(f["q_diag"][:, ksl(h)], f["k_diag"][:, ksl(h)])
        if S == 1:
            return jnp.where(diag_mask, a_diag, 0.0).astype(BF16)
        blocks = [jnp.zeros((SUB, C), F32)]
        for i in range(1, S):
            blocks.append(_dot_nt(f["q_off"][i * SUB:(i + 1) * SUB, ksl(h)], f["k_off"][i][:, ksl(h)]))
        return jnp.where(diag_mask, a_diag, jnp.concatenate(blocks, axis=0)).astype(BF16)

    def read_out(j, r0, f, a):
        v = v_ref[j, pl.ds(r0, C), :]
        res = []
        for h in range(GLA_HEADS):
            o_h = _dot(a[h], v[:, vsl(h)]) + _dot(f["q_inter"][:, ksl(h)], s_scr[j, h].astype(BF16))
            res.append((o_h, _dot_tn(f["k_state"][:, ksl(h)], v[:, vsl(h)])))
        return res

    def finish(j, r0, f, res):
        outs = []
        for h, (o_h, s_inc) in enumerate(res):
            dec = jnp.broadcast_to(f["dec"][:, ksl(h)], (GLA_DK, GLA_DK))
            dec_col = jnp.sum(jnp.where(eye, dec, 0.0), axis=-1, keepdims=True)
            s_scr[j, h] = dec_col * s_scr[j, h] + s_inc
            ms = jnp.mean(o_h * o_h, axis=-1, keepdims=True)
            outs.append(o_h * lax.rsqrt(ms + NORM_EPS) * gn_ref[...])
        o = jnp.concatenate(outs, axis=-1) * rb_ref[j, pl.ds(r0, C), :].astype(F32)
        o_ref[j, pl.ds(r0, C), :] = o.astype(BF16)

    def step(c):
        r0 = c * C if isinstance(c, int) else pl.multiple_of(c * C, C)
        bs = [decays(j, r0) for j in range(nb)]
        fs = [factors(j, r0, b) for j, b in enumerate(bs)]
        attn = [[intra(f, h) for h in range(GLA_HEADS)] for f in fs]
        res = [read_out(j, r0, fs[j], attn[j]) for j in range(nb)]
        for j in range(nb):
            finish(j, r0, fs[j], res[j])

    if length == C:
        step(0)
    else:
        lax.fori_loop(0, length // C, lambda c, carry: (step(c), carry)[1], 0)
    sout_ref[...] = s_scr[...]


def _gla(qb, kb, vb, la, rb, gn, s0):
    B, L, _ = qb.shape
    chunk = GLA_CHUNK if L % GLA_CHUNK == 0 else L
    sub = GLA_SUB if chunk % GLA_SUB == 0 else chunk
    nb = min(B, SAMPLE_BATCH) if L == chunk else min(B, GLA_LONG_BATCH)
    assert B % nb == 0
    seq = lambda w: pl.BlockSpec((nb, L, w), lambda b: (b, 0, 0))
    st = pl.BlockSpec((nb, GLA_HEADS, GLA_DK, GLA_DV), lambda b: (b, 0, 0, 0))
    return pl.pallas_call(
        functools.partial(_gla_kernel, chunk=chunk, sub=sub, length=L),
        grid=(B // nb,),
        in_specs=[seq(GLA_K_WIDTH), seq(GLA_K_WIDTH), seq(B_WIDTH), seq(GLA_K_WIDTH), seq(B_WIDTH),
                  pl.BlockSpec((1, GLA_DV), lambda b: (0, 0)), st],
        out_specs=[seq(B_WIDTH), st],
        out_shape=[jax.ShapeDtypeStruct((B, L, B_WIDTH), BF16),
                   jax.ShapeDtypeStruct((B, GLA_HEADS, GLA_DK, GLA_DV), F32)],
        scratch_shapes=[pltpu.VMEM((nb, GLA_HEADS, GLA_DK, GLA_DV), F32)],
        compiler_params=_cparams(("parallel",)),
        name="gla",
    )(qb, kb, vb, la, rb, gn, s0)


def _outproj_kernel(x_ref, oa_ref, ob_ref, ga_ref, gb_ref, wa_ref, wb_ref, wo_ref, n2_ref, wr_ref, br_ref, upper_ref,
                    h_ref, hp_ref, e_ref, g_ref, rank_ref, cnt_ref, carry):
    i = pl.program_id(0)

    @pl.when(i == 0)
    def _():
        carry[...] = jnp.zeros_like(carry)

    tm = x_ref.shape[0]
    merged = (ga_ref[...].astype(F32) * _dot(oa_ref[...], wa_ref[...])
              + gb_ref[...].astype(F32) * _dot(ob_ref[...], wb_ref[...]))
    h = x_ref[...] + _dot(merged.astype(BF16), wo_ref[...])
    h_ref[...] = h
    ms = jnp.mean(h * h, axis=-1, keepdims=True)
    hn = h * lax.rsqrt(ms + NORM_EPS) * n2_ref[...]
    half = D_MODEL // 2
    lo_bits = pltpu.bitcast(hn[:, :half].astype(BF16).astype(F32), U32) >> 16
    hi_bits = pltpu.bitcast(hn[:, half:].astype(BF16).astype(F32), U32) & jnp.uint32(0xFFFF0000)
    hp_ref[...] = lo_bits | hi_bits
    hn_hi, hn_lo = _split_bf16(hn)
    wr_hi, wr_lo = _split_bf16(wr_ref[...])
    logits = _dot(hn_hi, wr_hi) + _dot(hn_lo, wr_hi) + _dot(hn_hi, wr_lo) + br_ref[...]
    lt = logits.T[:N_EXPERTS, :]
    eid = lax.broadcasted_iota(I32, (N_EXPERTS, tm), 0)
    vals, hots = [], []
    for k in range(TOP_K):
        m = jnp.max(lt, axis=0, keepdims=True)
        idx = jnp.min(jnp.where(lt == m, eid, N_EXPERTS), axis=0, keepdims=True)
        hot = eid == idx
        lt = jnp.where(hot, -jnp.inf, lt)
        vals.append(m)
        hots.append(hot)
        e_ref[k:k + 1, :] = idx
    ex = [jnp.exp(vk - vals[0]) for vk in vals]
    den = ex[0] + ex[1] + ex[2] + ex[3]
    for k in range(TOP_K):
        g_ref[k:k + 1, :] = ex[k] / den
    multi = (hots[0] | hots[1] | hots[2] | hots[3])
    before = _dot(multi.astype(BF16), upper_ref[...]) + carry[...]
    for k in range(TOP_K):
        rank_ref[k:k + 1, :] = jnp.sum(jnp.where(hots[k], before, 0.0), axis=0, keepdims=True).astype(I32)
    carry[...] = carry[...] + jnp.sum(multi.astype(F32), axis=1, keepdims=True)
    cnt_ref[...] = jnp.broadcast_to(carry[...], cnt_ref.shape).astype(I32)


def _outproj(x2, oa, ob, ga, gb, wa, wb, wo, n2, wr, br, upper):
    T = x2.shape[0]
    tm = min(TOK_TILE, T)
    assert T % tm == 0 and upper.shape == (tm, tm)
    tok = lambda w: pl.BlockSpec((tm, w), lambda i: (i, 0))
    const = lambda shape: pl.BlockSpec(shape, lambda i: (0, 0))
    kt = pl.BlockSpec((TOP_K, tm), lambda i: (0, i))
    return pl.pallas_call(
        _outproj_kernel,
        grid=(T // tm,),
        in_specs=[tok(D_MODEL), tok(A_WIDTH), tok(B_WIDTH), tok(D_MODEL), tok(D_MODEL),
                  const((A_WIDTH, D_MODEL)), const((B_WIDTH, D_MODEL)), const((D_MODEL, D_MODEL)),
                  const((1, D_MODEL)), const((D_MODEL, LANE)), const((1, LANE)), const((tm, tm))],
        out_specs=[tok(D_MODEL), tok(D_MODEL // 2), kt, kt, kt, const((N_EXPERTS, LANE))],
        out_shape=[jax.ShapeDtypeStruct((T, D_MODEL), F32), jax.ShapeDtypeStruct((T, D_MODEL // 2), U32),
                   jax.ShapeDtypeStruct((TOP_K, T), I32), jax.ShapeDtypeStruct((TOP_K, T), F32),
                   jax.ShapeDtypeStruct((TOP_K, T), I32), jax.ShapeDtypeStruct((N_EXPERTS, LANE), I32)],
        scratch_shapes=[pltpu.VMEM((N_EXPERTS, 1), F32)],
        compiler_params=_cparams(("arbitrary",)),
        name="outproj",
    )(x2, oa, ob, ga, gb, wa, wb, wo, n2, wr, br, upper)


def _start_row_copies(row_copy, n_tokens):
    def issue(g, c):
        for u in range(ROW_UNROLL):
            for k in range(TOP_K):
                row_copy(g * ROW_UNROLL + u, k).start(priority=k % 2)
        return c

    lax.fori_loop(0, n_tokens // ROW_UNROLL, issue, 0)


def _dispatch_kernel(fill_ref, pos_ref, hp_p_ref, hp_s_ref, xs_ref, sem, zbuf, zsem, *, steps_p):
    i = pl.program_id(0)
    tm = hp_p_ref.shape[0]

    @pl.when(i == 0)
    def _():
        zbuf[...] = jnp.zeros_like(zbuf)

        def fill_copy(j):
            return pltpu.make_async_copy(zbuf, xs_ref.at[pl.ds(pl.multiple_of(fill_ref[j], MOE_BM), MOE_BM), :], zsem)

        for j in range(fill_ref.shape[0]):
            pl.when(fill_ref[j] >= 0)(lambda j=j: fill_copy(j).start())
        for j in range(fill_ref.shape[0]):
            pl.when(fill_ref[j] >= 0)(lambda j=j: fill_copy(j).wait())

    def scatter(hp_ref):
        def row_copy(t, k):
            return pltpu.make_async_copy(hp_ref.at[pl.ds(t, 1), :], xs_ref.at[pl.ds(pos_ref[k, t], 1), :], sem)

        _start_row_copies(row_copy, tm)
        for k in range(TOP_K):
            pltpu.make_async_copy(hp_ref, xs_ref.at[pl.ds(0, tm), :], sem).wait()

    pl.when(i < steps_p)(lambda: scatter(hp_p_ref))
    pl.when(i >= steps_p)(lambda: scatter(hp_s_ref))


def _dispatch(fill, pos, hp_p, hp_s, n_rows):
    tm = DISP_TILE
    assert hp_p.shape[0] % tm == 0 and hp_s.shape[0] % tm == 0
    steps_p, steps_s = hp_p.shape[0] // tm, hp_s.shape[0] // tm
    return pl.pallas_call(
        functools.partial(_dispatch_kernel, steps_p=steps_p),
        grid_spec=pltpu.PrefetchScalarGridSpec(
            num_scalar_prefetch=1,
            grid=(steps_p + steps_s,),
            in_specs=[pl.BlockSpec((TOP_K, tm), lambda i, fl: (0, i), memory_space=pltpu.SMEM),
                      pl.BlockSpec((tm, D_MODEL // 2), lambda i, fl: (jnp.minimum(i, steps_p - 1), 0)),
                      pl.BlockSpec((tm, D_MODEL // 2), lambda i, fl: (jnp.maximum(i - steps_p, 0), 0))],
            out_specs=pl.BlockSpec(memory_space=pl.ANY),
            scratch_shapes=[pltpu.SemaphoreType.DMA, pltpu.VMEM((MOE_BM, D_MODEL // 2), U32),
                            pltpu.SemaphoreType.DMA],
        ),
        out_shape=jax.ShapeDtypeStruct((n_rows, D_MODEL // 2), U32),
        compiler_params=_cparams(("arbitrary",)),
        name="dispatch",
    )(fill, pos, hp_p, hp_s)


def _experts_kernel(be_ref, nb_ref, xs_ref, w1_ref, b1g_ref, b1l_ref, w2_ref, b2_ref, sel_ref, y_ref,
                    w1g_scr, w1l_scr, w2_scr):
    i = pl.program_id(0)
    live = i < nb_ref[0]

    @pl.when(jnp.logical_not(live))
    def _():
        y_ref[...] = jnp.zeros_like(y_ref)

    @pl.when(live & ((i == 0) | (be_ref[i] != be_ref[jnp.maximum(i - 1, 0)])))
    def _():
        for c in range(D_FF // LANE):
            cols = _dot(w1_ref[0, :, c * 2 * LANE:(c + 1) * 2 * LANE].astype(BF16), sel_ref[...])
            w1g_scr[:, c * LANE:(c + 1) * LANE] = cols[:, :LANE].astype(BF16)
            w1l_scr[:, c * LANE:(c + 1) * LANE] = cols[:, LANE:].astype(BF16)
        w2_scr[...] = w2_ref[0].astype(BF16)

    @pl.when(live)
    def _():
        half = D_MODEL // 2
        w = xs_ref[...]
        x_lo = pltpu.bitcast(w << 16, F32).astype(BF16)
        x_hi = pltpu.bitcast(w & jnp.uint32(0xFFFF0000), F32).astype(BF16)

        def up(w_scr, b_ref):
            return _dot(x_lo, w_scr[:half, :]) + _dot(x_hi, w_scr[half:, :]) + b_ref[0]

        x_glu = jnp.minimum(up(w1g_scr, b1g_ref), SWIGLU_LIMIT)
        x_lin = jnp.clip(up(w1l_scr, b1l_ref), -SWIGLU_LIMIT, SWIGLU_LIMIT)
        act = x_glu * jax.nn.sigmoid(SWIGLU_ALPHA * x_glu) * (x_lin + 1.0)
        y_ref[...] = _dot(act.astype(BF16), w2_scr[...]) + b2_ref[0]


def _experts(blk_e, n_used, xs, w1, b1g, b1l, w2, b2, sel):
    n_rows = xs.shape[0]
    nblk = n_rows // MOE_BM
    wspec = lambda r, c: pl.BlockSpec((1, r, c), lambda i, be, nb: (be[i], 0, 0))
    return pl.pallas_call(
        _experts_kernel,
        grid_spec=pltpu.PrefetchScalarGridSpec(
            num_scalar_prefetch=2,
            grid=(nblk,),
            in_specs=[pl.BlockSpec((MOE_BM, D_MODEL // 2), lambda i, be, nb: (i, 0)),
                      wspec(D_MODEL, 2 * D_FF), wspec(1, D_FF), wspec(1, D_FF),
                      wspec(D_FF, D_MODEL), wspec(1, D_MODEL),
                      pl.BlockSpec((2 * LANE, 2 * LANE), lambda i, be, nb: (0, 0))],
            out_specs=pl.BlockSpec((MOE_BM, D_MODEL), lambda i, be, nb: (i, 0)),
            scratch_shapes=[pltpu.VMEM((D_MODEL, D_FF), BF16), pltpu.VMEM((D_MODEL, D_FF), BF16),
                            pltpu.VMEM((D_FF, D_MODEL), BF16)],
        ),
        out_shape=jax.ShapeDtypeStruct((n_rows, D_MODEL), F32),
        compiler_params=_cparams(("arbitrary",)),
        name="experts",
    )(blk_e, n_used, xs, w1, b1g, b1l, w2, b2, sel)


def _combine_kernel(pos_ref, h_ref, g_ref, yb_ref, y_ref, buf, sem):
    tm = h_ref.shape[0]

    def row_copy(t, k):
        return pltpu.make_async_copy(yb_ref.at[pl.ds(pos_ref[k, t], 1), :], buf.at[k, pl.ds(t, 1), :], sem)

    _start_row_copies(row_copy, tm)
    for k in range(TOP_K):
        pltpu.make_async_copy(yb_ref.at[pl.ds(0, tm), :], buf.at[k], sem).wait()
    g = g_ref[...]
    y = h_ref[...]
    for k in range(TOP_K):
        y = y + g[:, k:k + 1] * buf[k]
    y_ref[...] = y


def _combine(pos, h, gate_t, yb):
    T = h.shape[0]
    tm = COMB_TILE
    assert T % tm == 0
    return pl.pallas_call(
        _combine_kernel,
        grid=(T // tm,),
        in_specs=[pl.BlockSpec((TOP_K, tm), lambda i: (0, i), memory_space=pltpu.SMEM),
                  pl.BlockSpec((tm, D_MODEL), lambda i: (i, 0)),
                  pl.BlockSpec((tm, TOP_K), lambda i: (i, 0)),
                  pl.BlockSpec(memory_space=pl.ANY)],
        out_specs=pl.BlockSpec((tm, D_MODEL), lambda i: (i, 0)),
        out_shape=jax.ShapeDtypeStruct((T, D_MODEL), F32),
        scratch_shapes=[pltpu.VMEM((TOP_K, tm, D_MODEL), F32), pltpu.SemaphoreType.DMA],
        compiler_params=_cparams(("arbitrary",)),
        name="combine",
    )(pos, h, gate_t, yb)


def _block_ones(width, block):
    idx = np.arange(width) // block
    return jnp.asarray(idx[:, None] == idx[None, :], BF16)


def _layer(x_p, x_s, cache_k, cache_v, s_gla, norm1, w_in, q_norm, k_norm, sinks, w_gk, b_gk,
           gla_norm, w_a, w_b, w_o, norm2, w_router, b_router, w1, b1, w2, b2):
    Bp, Lp, _ = x_p.shape
    Bs, Ls, _ = x_s.shape
    Tp, Ts = Bp * Lp, Bs * Ls

    offs = np.cumsum((0, A_WIDTH, KV_WIDTH, KV_WIDTH, GLA_K_WIDTH, GLA_K_WIDTH, B_WIDTH, GLA_GATE_RANK, B_WIDTH,
                      D_MODEL, D_MODEL))
    src = dict(zip(("qa", "ka", "va", "qb", "kb", "vb", "gl", "rb", "ga", "gb"), zip(offs[:-1], offs[1:])))
    cols = []
    for name, width in _SEGS:
        a, b = src[name]
        piece = w_in[:, a:b]
        if b - a < width:
            piece = jnp.pad(piece, ((0, 0), (0, width - (b - a))))
        cols.append(piece)
    w_packed = jnp.concatenate(cols, axis=1).astype(BF16)
    wgk = jnp.pad(w_gk, ((0, GL_PAD - GLA_GATE_RANK), (0, 0))).astype(BF16)
    n1 = norm1.reshape(1, D_MODEL)
    gq, gk = _block_ones(A_WIDTH, SWA_HEAD_DIM), _block_ones(KV_WIDTH, SWA_HEAD_DIM)
    qn = jnp.tile(q_norm, SWA_HEADS).reshape(1, A_WIDTH)
    kn = jnp.tile(k_norm, SWA_KV_HEADS).reshape(1, KV_WIDTH)
    bgk = b_gk.reshape(1, GLA_K_WIDTH)
    gn = gla_norm.reshape(1, GLA_DV)
    wa, wb, wo = w_a.astype(BF16), w_b.astype(BF16), w_o.astype(BF16)
    n2 = norm2.reshape(1, D_MODEL)
    wr = jnp.pad(w_router, ((0, 0), (0, LANE - N_EXPERTS)))
    br = jnp.pad(b_router, (0, LANE - N_EXPERTS)).reshape(1, LANE)
    b1g = b1[:, 0::2].reshape(N_EXPERTS, 1, D_FF)
    b1l = b1[:, 1::2].reshape(N_EXPERTS, 1, D_FF)
    b2r = b2.reshape(N_EXPERTS, 1, D_MODEL)
    sel_np = np.zeros((2 * LANE, 2 * LANE), np.float32)
    sel_np[2 * np.arange(LANE), np.arange(LANE)] = 1.0
    sel_np[2 * np.arange(LANE) + 1, LANE + np.arange(LANE)] = 1.0
    sel = jnp.asarray(sel_np, BF16)

    def mix(x, k_past, v_past, s0):
        B, L, _ = x.shape
        T = B * L
        x2 = x.reshape(T, D_MODEL)
        qa, ka, va, qb, kb, vb, la, rb, ga, gb = _inproj(x2, n1, w_packed, gq, gk, qn, kn, wgk, bgk)
        r3 = lambda t: t.reshape(B, L, t.shape[-1])
        oa = _swa(sinks, r3(qa), r3(ka), r3(va), k_past, v_past, has_past=k_past is not None)
        ob, s_out = _gla(r3(qb), r3(kb), r3(vb), r3(la), r3(rb), gn, s0)
        tm = min(TOK_TILE, T)
        upper = jnp.asarray(np.arange(tm)[:, None] < np.arange(tm)[None, :], BF16)
        h, hp, top_e, gate, rank, cnt = _outproj(x2, oa.reshape(T, A_WIDTH), ob.reshape(T, B_WIDTH), ga, gb,
                                                 wa, wb, wo, n2, wr, br, upper)
        return r3(ka), r3(va), s_out, h, hp, top_e, gate, rank, cnt[:, 0]

    win = cache_k.shape[1]
    assert win == WINDOW
    ck = cache_k.reshape(Bs, win, KV_WIDTH)
    cv = cache_v.reshape(Bs, win, KV_WIDTH)
    ka_p, va_p, s_p, h_p, hp_p, e_p, g_p, rank_p, cnt_p = mix(
        x_p, None, None, jnp.zeros((Bp, GLA_HEADS, GLA_DK, GLA_DV), F32))
    ka_s, va_s, s_s, h_s, hp_s, e_s, g_s, rank_s, cnt_s = mix(x_s, ck, cv, s_gla)

    T = Tp + Ts
    n_rows = -(-(T * TOP_K + N_EXPERTS * (MOE_BM - 1)) // MOE_BM) * MOE_BM
    counts = cnt_p + cnt_s
    padded = (counts + MOE_BM - 1) // MOE_BM * MOE_BM
    pad_end = jnp.cumsum(padded)
    start = pad_end - padded

    def lookup(table, idx):
        hot = idx[None] == jnp.arange(N_EXPERTS, dtype=I32)[:, None, None]
        return jnp.sum(jnp.where(hot, table[:, None, None], 0), axis=0)

    pos_p = lookup(start, e_p) + rank_p
    pos_s = lookup(start + cnt_p, e_s) + rank_s
    blk_row = jnp.arange(n_rows // MOE_BM, dtype=I32) * MOE_BM
    blk_e = jnp.minimum(jnp.sum(pad_end[None, :] <= blk_row[:, None], axis=1), N_EXPERTS - 1).astype(I32)
    n_used = (pad_end[-1:] // MOE_BM).astype(I32)
    tails = jnp.where(padded > 0, pad_end - MOE_BM, -1)
    spare = pad_end[-1] + blk_row[:N_EXPERTS]
    fill = jnp.concatenate([tails, jnp.where(spare < n_rows, spare, -1)]).astype(I32)

    xs = _dispatch(fill, jnp.concatenate([pos_p, pos_s], axis=1), hp_p, hp_s, n_rows)
    yb = _experts(blk_e, n_used, xs, w1, b1g, b1l, w2, b2r, sel)
    y_p = _combine(pos_p, h_p, g_p.T, yb).reshape(Bp, Lp, D_MODEL)
    y_s = _combine(pos_s, h_s, g_s.T, yb).reshape(Bs, Ls, D_MODEL)

    kv = lambda t: t.reshape(t.shape[0], t.shape[1], SWA_KV_HEADS, SWA_HEAD_DIM)
    k_p, v_p = kv(ka_p[:, -WINDOW:]), kv(va_p[:, -WINDOW:])
    k_s = jnp.concatenate([cache_k, kv(ka_s)], axis=1)[:, -win:]
    v_s = jnp.concatenate([cache_v, kv(va_s)], axis=1)[:, -win:]
    return y_p, y_s, (k_p, v_p, s_p, k_s, v_s, s_s)


def kernel(x_prompt, x_sample, cache_swa_k, cache_swa_v, state_gla, norm1, w_in, q_norm, k_norm, sinks, w_gk, b_gk,
           gla_norm, w_a, w_b, w_o, norm2, w_router, b_router, w1, b1, w2, b2):
    depth = norm1.shape[0]
    y_p, y_s = x_prompt, x_sample
    states = []
    for l in range(depth):
        y_p, y_s, st = _layer(y_p, y_s, cache_swa_k[l], cache_swa_v[l], state_gla[l], norm1[l], w_in[l], q_norm[l],
                              k_norm[l], sinks[l], w_gk[l], b_gk[l], gla_norm[l], w_a[l], w_b[l], w_o[l], norm2[l],
                              w_router[l], b_router[l], w1[l], b1[l], w2[l], b2[l])
        states.append(st)
    return (y_p, y_s) + tuple(jnp.stack([st[j] for st in states]) for j in range(6))
```

```python
import functools

import jax
import jax.numpy as jnp
import numpy as np
from jax import lax
from jax.experimental import pallas as pl
from jax.experimental.pallas import tpu as pltpu

F32 = jnp.float32
BF16 = jnp.bfloat16
I32 = jnp.int32
U32 = jnp.uint32

D_MODEL = 1024
SWA_HEADS = 8
SWA_KV_HEADS = 2
SWA_GROUP = SWA_HEADS // SWA_KV_HEADS
SWA_HEAD_DIM = 64
WINDOW = 128
ATTN_SCALE = SWA_HEAD_DIM ** -0.5
GLA_HEADS = 4
GLA_DK = 64
GLA_DV = 128
GLA_GATE_RANK = 16
GLA_GATE_NORM = 16.0
GLA_SCALE = GLA_DK ** -0.5
A_WIDTH = SWA_HEADS * SWA_HEAD_DIM
KV_WIDTH = SWA_KV_HEADS * SWA_HEAD_DIM
GLA_K_WIDTH = GLA_HEADS * GLA_DK
B_WIDTH = GLA_HEADS * GLA_DV
N_EXPERTS = 32
TOP_K = 4
D_FF = D_MODEL
SWIGLU_ALPHA = 1.702
SWIGLU_LIMIT = 7.0
NORM_EPS = 1e-5
QK_EPS = 1e-6
NEG_INF = -1e30

LANE = 128
GL_PAD = LANE
VMEM_LIMIT = 56 * 1024 * 1024

_SEGS = (("qa", A_WIDTH), ("ka", KV_WIDTH), ("va", KV_WIDTH), ("qb", GLA_K_WIDTH), ("kb", GLA_K_WIDTH),
         ("vb", B_WIDTH), ("rb", B_WIDTH), ("ga", D_MODEL), ("gb", D_MODEL), ("gl", GL_PAD))
_OFF = {}
_o = 0
for _n, _w in _SEGS:
    _OFF[_n] = (_o, _w)
    _o += _w
IN_PACKED = _o

TOK_TILE = 512
GLA_CHUNK = 64
GLA_SUB = 16
MOE_BM = 256
ROW_WORDS = D_MODEL // 2
RUN_ALIGN = 8
SORT_CHUNK = 256
SORT_ROWS = -(-(TOK_TILE * TOP_K + N_EXPERTS * (RUN_ALIGN - 1)) // SORT_CHUNK) * SORT_CHUNK
SAMPLE_BATCH = 8
GLA_LONG_BATCH = 2


def _cparams(sem):
    return pltpu.CompilerParams(dimension_semantics=sem, vmem_limit_bytes=VMEM_LIMIT)


def _split_bf16(v):
    hi = v.astype(BF16)
    lo = (v - hi.astype(F32)).astype(BF16)
    return hi, lo


def _dot(a, b):
    return jnp.dot(a, b, preferred_element_type=F32)


def _dot_nt(a, b):
    return lax.dot_general(a, b, (((1,), (1,)), ((), ())), preferred_element_type=F32)


def _dot_tn(a, b):
    return lax.dot_general(a, b, (((0,), (0,)), ((), ())), preferred_element_type=F32)


def _inproj_kernel(x_ref, n1_ref, w_ref, gq_ref, gk_ref, qn_ref, kn_ref, wgk_ref, bgk_ref,
                   qa_ref, ka_ref, va_ref, qb_ref, kb_ref, vb_ref, la_ref, rb_ref, ga_ref, gb_ref):
    x = x_ref[...]
    ms = jnp.mean(x * x, axis=-1, keepdims=True)
    xn = (x * lax.rsqrt(ms + NORM_EPS) * n1_ref[...]).astype(BF16)

    def proj(name):
        off, width = _OFF[name]
        return _dot(xn, w_ref[:, off:off + width])

    def head_norm(v, ones_ref, gain_ref):
        hi, lo = _split_bf16(v * v)
        ss = _dot(hi, ones_ref[...]) + _dot(lo, ones_ref[...])
        return v * lax.rsqrt(ss * (1.0 / SWA_HEAD_DIM) + QK_EPS) * gain_ref[...]

    qa_ref[...] = (head_norm(proj("qa"), gq_ref, qn_ref) * ATTN_SCALE).astype(BF16)
    ka_ref[...] = head_norm(proj("ka"), gk_ref, kn_ref)
    va_ref[...] = proj("va")
    qb_ref[...] = (proj("qb") * GLA_SCALE).astype(BF16)
    kb_ref[...] = proj("kb").astype(BF16)
    vb_ref[...] = proj("vb").astype(BF16)
    z = _dot(proj("gl").astype(BF16), wgk_ref[...]) + bgk_ref[...]
    la_ref[...] = (jnp.minimum(z, 0.0) - jnp.log1p(jnp.exp(-jnp.abs(z)))) * (1.0 / GLA_GATE_NORM)
    rb = proj("rb")
    rb_ref[...] = (rb * jax.nn.sigmoid(rb)).astype(BF16)
    ga_ref[...] = jax.nn.sigmoid(proj("ga")).astype(BF16)
    gb_ref[...] = jax.nn.sigmoid(proj("gb")).astype(BF16)


def _inproj(x2, n1, w_packed, gq, gk, qn, kn, wgk, bgk):
    T = x2.shape[0]
    tm = min(TOK_TILE, T)
    assert T % tm == 0

    def tok(width):
        return pl.BlockSpec((tm, width), lambda i: (i, 0))

    def const(shape):
        return pl.BlockSpec(shape, lambda i: (0, 0))

    outs = (("qa", A_WIDTH, BF16), ("ka", KV_WIDTH, F32), ("va", KV_WIDTH, F32), ("qb", GLA_K_WIDTH, BF16),
            ("kb", GLA_K_WIDTH, BF16), ("vb", B_WIDTH, BF16), ("la", GLA_K_WIDTH, F32), ("rb", B_WIDTH, BF16),
            ("ga", D_MODEL, BF16), ("gb", D_MODEL, BF16))
    return pl.pallas_call(
        _inproj_kernel,
        grid=(T // tm,),
        in_specs=[tok(D_MODEL), const((1, D_MODEL)), const((D_MODEL, IN_PACKED)), const((A_WIDTH, A_WIDTH)),
                  const((KV_WIDTH, KV_WIDTH)), const((1, A_WIDTH)), const((1, KV_WIDTH)),
                  const((GL_PAD, GLA_K_WIDTH)), const((1, GLA_K_WIDTH))],
        out_specs=[tok(w) for _, w, _ in outs],
        out_shape=[jax.ShapeDtypeStruct((T, w), dt) for _, w, dt in outs],
        compiler_params=_cparams(("parallel",)),
        name="inproj",
    )(x2, n1, w_packed, gq, gk, qn, kn, wgk, bgk)


def _swa_kernel(sink_ref, q_ref, kc_ref, vc_ref, kp_ref, vp_ref, o_ref, *, first_block_has_past):
    n = pl.program_id(1)
    nb, lq = q_ref.shape[0], q_ref.shape[1]
    assert lq & (lq - 1) == 0
    stack = SWA_GROUP if lq < WINDOW else 1
    rows, keys = stack * lq, WINDOW + lq
    qi = lax.broadcasted_iota(I32, (rows, keys), 0) & (lq - 1)
    ci = lax.broadcasted_iota(I32, (rows, keys), 1)
    dist = WINDOW + qi - ci
    mask = (dist >= 0) & (dist < WINDOW)
    if not first_block_has_past:
        mask = mask & ((ci >= WINDOW) | (n > 0))
    cols = lambda h0: slice(h0 // SWA_GROUP * SWA_HEAD_DIM, (h0 // SWA_GROUP + 1) * SWA_HEAD_DIM)
    qs, ks, vs = [], [], []
    for j in range(nb):
        q = q_ref[j]
        qs.append(q.astype(F32) if stack > 1 else q)
        ks.append(jnp.concatenate([kp_ref[j], kc_ref[j]], axis=0).astype(BF16))
        vs.append(jnp.concatenate([vp_ref[j], vc_ref[j]], axis=0).astype(BF16))
    groups = [(j, h0) for j in range(nb) for h0 in range(0, SWA_HEADS, stack)]
    wave = len(groups) if stack > 1 else 1
    outs = {}
    for w0 in range(0, len(groups), wave):
        scores, sinks = [], []
        for j, h0 in groups[w0:w0 + wave]:
            heads = range(h0, h0 + stack)
            qg = jnp.concatenate([qs[j][:, h * SWA_HEAD_DIM:(h + 1) * SWA_HEAD_DIM] for h in heads], axis=0)
            sinks.append(jnp.concatenate([jnp.full((lq, 1), sink_ref[h], F32) for h in heads], axis=0))
            scores.append(_dot_nt(qg.astype(BF16), ks[j][:, cols(h0)]))
        probs, denoms = [], []
        for s, sink in zip(scores, sinks):
            s = jnp.where(mask, s, NEG_INF)
            m = jnp.maximum(jnp.max(s, axis=-1, keepdims=True), sink)
            p = jnp.exp(s - m)
            denoms.append(jnp.sum(p, axis=-1, keepdims=True) + jnp.exp(sink - m))
            probs.append(p.astype(BF16))
        for (j, h0), p, denom in zip(groups[w0:w0 + wave], probs, denoms):
            o = _dot(p, vs[j][:, cols(h0)]) / denom
            outs.setdefault(j, []).extend(o[g * lq:(g + 1) * lq] for g in range(stack))
    for j in range(nb):
        o_ref[j] = jnp.concatenate(outs[j], axis=-1).astype(BF16)


def _swa(sinks, qa, ka, va, k_past, v_past, *, has_past):
    B, L, _ = qa.shape
    lq = min(L, WINDOW)
    nl = L // lq
    if has_past:
        assert nl == 1
        nb = min(B, SAMPLE_BATCH)
        past_spec = pl.BlockSpec((nb, WINDOW, KV_WIDTH), lambda b, n, s: (b, 0, 0))
        kp, vp = k_past, v_past
    else:
        assert lq == WINDOW
        nb = 1
        past_spec = pl.BlockSpec((nb, WINDOW, KV_WIDTH), lambda b, n, s: (b, jnp.maximum(n - 1, 0), 0))
        kp, vp = ka, va
    assert B % nb == 0
    cur = lambda w: pl.BlockSpec((nb, lq, w), lambda b, n, s: (b, n, 0))
    return pl.pallas_call(
        functools.partial(_swa_kernel, first_block_has_past=has_past),
        grid_spec=pltpu.PrefetchScalarGridSpec(
            num_scalar_prefetch=1,
            grid=(B // nb, nl),
            in_specs=[cur(A_WIDTH), cur(KV_WIDTH), cur(KV_WIDTH), past_spec, past_spec],
            out_specs=cur(A_WIDTH),
        ),
        out_shape=jax.ShapeDtypeStruct((B, L, A_WIDTH), BF16),
        compiler_params=_cparams(("parallel", "parallel")),
        name="swa",
    )(sinks, qa, ka, va, kp, vp)


def _gla_kernel(q_ref, k_ref, v_ref, la_ref, rb_ref, gn_ref, s0_ref, o_ref, sout_ref, s_scr, *, chunk, sub, length):
    C, SUB = chunk, sub
    S = C // SUB
    W = GLA_K_WIDTH
    nb = q_ref.shape[0]
    s_scr[...] = s0_ref[...]
    row = lax.broadcasted_iota(I32, (C, C), 0)
    col = lax.broadcasted_iota(I32, (C, C), 1)
    tri = (row >= col).astype(BF16)
    diag_mask = (row >= col) & ((row // SUB) == (col // SUB))
    eye = lax.broadcasted_iota(I32, (GLA_DK, GLA_DK), 0) == lax.broadcasted_iota(I32, (GLA_DK, GLA_DK), 1)
    krow = lax.broadcasted_iota(I32, (C, W), 0)

    def rows_of(x, r):
        return jnp.broadcast_to(x[r:r + 1, :], (SUB, W))

    ksl = lambda h: slice(h * GLA_DK, (h + 1) * GLA_DK)
    vsl = lambda h: slice(h * GLA_DV, (h + 1) * GLA_DV)

    def decays(j, r0):
        g_hi, g_lo = _split_bf16(la_ref[j, pl.ds(r0, C), :])
        return _dot(tri, g_hi) + _dot(tri, g_lo)

    def factors(j, r0, b):
        q = q_ref[j, pl.ds(r0, C), :].astype(F32)
        k = k_ref[j, pl.ds(r0, C), :].astype(F32)
        b_last = b[C - 1:C, :]
        mid = jnp.concatenate([rows_of(b, i * SUB + SUB // 2 - 1) for i in range(S)], axis=0)
        beta = jnp.concatenate([jnp.zeros((SUB, W), F32)] + [rows_of(b, i * SUB - 1) for i in range(1, S)], axis=0)
        k_off = [None]
        for i in range(1, S):
            e = jnp.exp(jnp.minimum(b[i * SUB - 1:i * SUB, :] - b, 0.0))
            k_off.append(jnp.where(krow < i * SUB, k * e, 0.0).astype(BF16))
        return dict(q_inter=(q * jnp.exp(b)).astype(BF16), k_state=(k * jnp.exp(b_last - b)).astype(BF16),
                    q_diag=(q * jnp.exp(b - mid)).astype(BF16), k_diag=(k * jnp.exp(mid - b)).astype(BF16),
                    q_off=(q * jnp.exp(b - beta)).astype(BF16), k_off=k_off, dec=jnp.exp(b_last))

    def intra(f, h):
        a_diag = _dot_nt(f["q_diag"][:, ksl(h)], f["k_diag"][:, ksl(h)])
        if S == 1:
            return jnp.where(diag_mask, a_diag, 0.0).astype(BF16)
        blocks = [jnp.zeros((SUB, C), F32)]
        for i in range(1, S):
            blocks.append(_dot_nt(f["q_off"][i * SUB:(i + 1) * SUB, ksl(h)], f["k_off"][i][:, ksl(h)]))
        return jnp.where(diag_mask, a_diag, jnp.concatenate(blocks, axis=0)).astype(BF16)

    def read_out(j, r0, f, a):
        v = v_ref[j, pl.ds(r0, C), :]
        res = []
        for h in range(GLA_HEADS):
            o_h = _dot(a[h], v[:, vsl(h)]) + _dot(f["q_inter"][:, ksl(h)], s_scr[j, h].astype(BF16))
            res.append((o_h, _dot_tn(f["k_state"][:, ksl(h)], v[:, vsl(h)])))
        return res

    def finish(j, r0, f, res):
        outs = []
        for h, (o_h, s_inc) in enumerate(res):
            dec = jnp.broadcast_to(f["dec"][:, ksl(h)], (GLA_DK, GLA_DK))
            dec_col = jnp.sum(jnp.where(eye, dec, 0.0), axis=-1, keepdims=True)
            s_scr[j, h] = dec_col * s_scr[j, h] + s_inc
            ms = jnp.mean(o_h * o_h, axis=-1, keepdims=True)
            outs.append(o_h * lax.rsqrt(ms + NORM_EPS) * gn_ref[...])
        o = jnp.concatenate(outs, axis=-1) * rb_ref[j, pl.ds(r0, C), :].astype(F32)
        o_ref[j, pl.ds(r0, C), :] = o.astype(BF16)

    def step(c):
        r0 = c * C if isinstance(c, int) else pl.multiple_of(c * C, C)
        bs = [decays(j, r0) for j in range(nb)]
        fs = [factors(j, r0, b) for j, b in enumerate(bs)]
        attn = [[intra(f, h) for h in range(GLA_HEADS)] for f in fs]
        res = [read_out(j, r0, fs[j], attn[j]) for j in range(nb)]
        for j in range(nb):
            finish(j, r0, fs[j], res[j])

    if length == C:
        step(0)
    else:
        lax.fori_loop(0, length // C, lambda c, carry: (step(c), carry)[1], 0)
    sout_ref[...] = s_scr[...]


def _gla(qb, kb, vb, la, rb, gn, s0):
    B, L, _ = qb.shape
    chunk = GLA_CHUNK if L % GLA_CHUNK == 0 else L
    sub = GLA_SUB if chunk % GLA_SUB == 0 else chunk
    nb = min(B, SAMPLE_BATCH) if L == chunk else min(B, GLA_LONG_BATCH)
    assert B % nb == 0
    seq = lambda w: pl.BlockSpec((nb, L, w), lambda b: (b, 0, 0))
    st = pl.BlockSpec((nb, GLA_HEADS, GLA_DK, GLA_DV), lambda b: (b, 0, 0, 0))
    return pl.pallas_call(
        functools.partial(_gla_kernel, chunk=chunk, sub=sub, length=L),
        grid=(B // nb,),
        in_specs=[seq(GLA_K_WIDTH), seq(GLA_K_WIDTH), seq(B_WIDTH), seq(GLA_K_WIDTH), seq(B_WIDTH),
                  pl.BlockSpec((1, GLA_DV), lambda b: (0, 0)), st],
        out_specs=[seq(B_WIDTH), st],
        out_shape=[jax.ShapeDtypeStruct((B, L, B_WIDTH), BF16),
                   jax.ShapeDtypeStruct((B, GLA_HEADS, GLA_DK, GLA_DV), F32)],
        scratch_shapes=[pltpu.VMEM((nb, GLA_HEADS, GLA_DK, GLA_DV), F32)],
        compiler_params=_cparams(("parallel",)),
        name="gla",
    )(qb, kb, vb, la, rb, gn, s0)


def _outproj_kernel(x_ref, oa_ref, ob_ref, ga_ref, gb_ref, wa_ref, wb_ref, wo_ref, n2_ref, wr_ref, br_ref, upper_ref,
                    lower_ref, h_ref, hn_ref, g_ref, lp_ref, cnt_ref):
    tm = x_ref.shape[0]
    merged = (ga_ref[...].astype(F32) * _dot(oa_ref[...], wa_ref[...])
              + gb_ref[...].astype(F32) * _dot(ob_ref[...], wb_ref[...]))
    h = x_ref[...] + _dot(merged.astype(BF16), wo_ref[...])
    h_ref[...] = h
    ms = jnp.mean(h * h, axis=-1, keepdims=True)
    hn = h * lax.rsqrt(ms + NORM_EPS) * n2_ref[...]
    hn_ref[...] = hn.astype(BF16)
    hn_hi, hn_lo = _split_bf16(hn)
    wr_hi, wr_lo = _split_bf16(wr_ref[...])
    logits = _dot(hn_hi, wr_hi) + _dot(hn_lo, wr_hi) + _dot(hn_hi, wr_lo) + br_ref[...]
    lt = logits.T[:N_EXPERTS, :]
    eid = lax.broadcasted_iota(I32, (N_EXPERTS, tm), 0)
    vals, hots = [], []
    for k in range(TOP_K):
        m = jnp.max(lt, axis=0, keepdims=True)
        idx = jnp.min(jnp.where(lt == m, eid, N_EXPERTS), axis=0, keepdims=True)
        hot = eid == idx
        lt = jnp.where(hot, -jnp.inf, lt)
        vals.append(m)
        hots.append(hot)
    ex = [jnp.exp(vk - vals[0]) for vk in vals]
    den = ex[0] + ex[1] + ex[2] + ex[3]
    for k in range(TOP_K):
        g_ref[k:k + 1, :] = ex[k] / den
    multi = (hots[0] | hots[1] | hots[2] | hots[3]).astype(BF16)
    before = _dot(multi, upper_ref[...])
    counts = jnp.sum(multi.astype(F32), axis=1, keepdims=True)
    aligned = jnp.floor((counts + (RUN_ALIGN - 1)) * (1.0 / RUN_ALIGN)) * RUN_ALIGN
    run_start = _dot(lower_ref[...], jnp.broadcast_to(aligned, (N_EXPERTS, LANE)).astype(BF16))[:, :1]
    place = before + run_start
    for k in range(TOP_K):
        lp_ref[k:k + 1, :] = jnp.sum(jnp.where(hots[k], place, 0.0), axis=0, keepdims=True).astype(I32)
    cnt_ref[...] = jnp.broadcast_to(aligned, cnt_ref.shape).astype(I32)


def _outproj(x2, oa, ob, ga, gb, wa, wb, wo, n2, wr, br, upper, lower):
    T = x2.shape[0]
    tm = TOK_TILE
    assert T % tm == 0 and upper.shape == (tm, tm)
    tok = lambda w: pl.BlockSpec((tm, w), lambda i: (i, 0))
    const = lambda shape: pl.BlockSpec(shape, lambda i: (0, 0))
    kt = pl.BlockSpec((TOP_K, tm), lambda i: (0, i))
    return pl.pallas_call(
        _outproj_kernel,
        grid=(T // tm,),
        in_specs=[tok(D_MODEL), tok(A_WIDTH), tok(B_WIDTH), tok(D_MODEL), tok(D_MODEL),
                  const((A_WIDTH, D_MODEL)), const((B_WIDTH, D_MODEL)), const((D_MODEL, D_MODEL)),
                  const((1, D_MODEL)), const((D_MODEL, LANE)), const((1, LANE)), const((tm, tm)),
                  const((N_EXPERTS, N_EXPERTS))],
        out_specs=[tok(D_MODEL), tok(D_MODEL), kt, kt, pl.BlockSpec((N_EXPERTS, LANE), lambda i: (i, 0))],
        out_shape=[jax.ShapeDtypeStruct((T, D_MODEL), F32), jax.ShapeDtypeStruct((T, D_MODEL), BF16),
                   jax.ShapeDtypeStruct((TOP_K, T), F32), jax.ShapeDtypeStruct((TOP_K, T), I32),
                   jax.ShapeDtypeStruct((T // tm * N_EXPERTS, LANE), I32)],
        compiler_params=_cparams(("parallel",)),
        name="outproj",
    )(x2, oa, ob, ga, gb, wa, wb, wo, n2, wr, br, upper, lower)


def _pack_rows(v):
    lo = pltpu.bitcast(v[:, :ROW_WORDS].astype(BF16).astype(F32), U32) >> 16
    hi = pltpu.bitcast(v[:, ROW_WORDS:].astype(BF16).astype(F32), U32) & jnp.uint32(0xFFFF0000)
    return lo | hi


def _unpack_rows(w):
    lo = pltpu.bitcast(w << 16, F32).astype(BF16)
    hi = pltpu.bitcast(w & jnp.uint32(0xFFFF0000), F32).astype(BF16)
    return lo, hi


def _run_copies(cnt_ref, off_ref, dest_ref, tot_ref, tile, tile_rows, global_rows, sem, *, to_global):
    def run(e):
        j = tile * N_EXPERTS + e
        n = pl.multiple_of(cnt_ref[j], RUN_ALIGN)
        local = tile_rows.at[pl.ds(pl.multiple_of(off_ref[j], RUN_ALIGN), n), :]
        far = global_rows.at[pl.ds(pl.multiple_of(dest_ref[j], RUN_ALIGN), n), :]
        return pltpu.make_async_copy(local, far, sem) if to_global else pltpu.make_async_copy(far, local, sem)

    for e in range(N_EXPERTS):
        pl.when(cnt_ref[tile * N_EXPERTS + e] > 0)(lambda e=e: run(e).start())
    total = pl.multiple_of(tot_ref[tile], RUN_ALIGN)
    pltpu.make_async_copy(global_rows.at[pl.ds(0, total), :], tile_rows.at[pl.ds(0, total), :], sem).wait()


def _dispatch_kernel(fill_ref, cnt_ref, off_ref, dest_ref, tot_ref, lp_ref, hn_p_ref, hn_s_ref, xs_ref,
                     rows, sem, zbuf, zsem, *, steps_p):
    i = pl.program_id(0)

    @pl.when(i == 0)
    def _():
        zbuf[...] = jnp.zeros_like(zbuf)

        def fill_copy(j):
            return pltpu.make_async_copy(zbuf, xs_ref.at[pl.ds(pl.multiple_of(fill_ref[j], MOE_BM), MOE_BM), :], zsem)

        for j in range(fill_ref.shape[0]):
            pl.when(fill_ref[j] >= 0)(lambda j=j: fill_copy(j).start())
        for j in range(fill_ref.shape[0]):
            pl.when(fill_ref[j] >= 0)(lambda j=j: fill_copy(j).wait())

    def sort_tile(hn_ref):
        hn = hn_ref[...]
        lp = lp_ref[...]
        for c in range(SORT_ROWS // SORT_CHUNK):
            r = lax.broadcasted_iota(I32, (SORT_CHUNK, TOK_TILE), 0) + c * SORT_CHUNK
            hit = (r == lp[0:1, :]) | (r == lp[1:2, :]) | (r == lp[2:3, :]) | (r == lp[3:4, :])
            perm = jnp.where(hit, 1.0, 0.0).astype(BF16)
            rows[c * SORT_CHUNK:(c + 1) * SORT_CHUNK, :] = _pack_rows(_dot(perm, hn))

    pl.when(i < steps_p)(lambda: sort_tile(hn_p_ref))
    pl.when(i >= steps_p)(lambda: sort_tile(hn_s_ref))
    _run_copies(cnt_ref, off_ref, dest_ref, tot_ref, i, rows, xs_ref, sem, to_global=True)


def _dispatch(fill, runs, lp, hn_p, hn_s, n_rows):
    tm = TOK_TILE
    steps_p, steps_s = hn_p.shape[0] // tm, hn_s.shape[0] // tm
    return pl.pallas_call(
        functools.partial(_dispatch_kernel, steps_p=steps_p),
        grid_spec=pltpu.PrefetchScalarGridSpec(
            num_scalar_prefetch=5,
            grid=(steps_p + steps_s,),
            in_specs=[pl.BlockSpec((TOP_K, tm), lambda i, *_: (0, i)),
                      pl.BlockSpec((tm, D_MODEL), lambda i, *_: (jnp.minimum(i, steps_p - 1), 0)),
                      pl.BlockSpec((tm, D_MODEL), lambda i, *_: (jnp.maximum(i - steps_p, 0), 0))],
            out_specs=pl.BlockSpec(memory_space=pl.ANY),
            scratch_shapes=[pltpu.VMEM((SORT_ROWS, ROW_WORDS), U32), pltpu.SemaphoreType.DMA,
                            pltpu.VMEM((MOE_BM, ROW_WORDS), U32), pltpu.SemaphoreType.DMA],
        ),
        out_shape=jax.ShapeDtypeStruct((n_rows, ROW_WORDS), U32),
        compiler_params=_cparams(("arbitrary",)),
        name="dispatch",
    )(fill, *runs, lp, hn_p, hn_s)


def _experts_kernel(be_ref, nb_ref, xs_ref, w1_ref, b1g_ref, b1l_ref, w2_ref, b2_ref, sel_ref, y_ref,
                    w1g_scr, w1l_scr, w2_scr):
    i = pl.program_id(0)
    live = i < nb_ref[0]

    @pl.when(jnp.logical_not(live))
    def _():
        y_ref[...] = jnp.zeros_like(y_ref)

    @pl.when(live & ((i == 0) | (be_ref[i] != be_ref[jnp.maximum(i - 1, 0)])))
    def _():
        for c in range(D_FF // LANE):
            cols = _dot(w1_ref[0, :, c * 2 * LANE:(c + 1) * 2 * LANE].astype(BF16), sel_ref[...])
            w1g_scr[:, c * LANE:(c + 1) * LANE] = cols[:, :LANE].astype(BF16)
            w1l_scr[:, c * LANE:(c + 1) * LANE] = cols[:, LANE:].astype(BF16)
        w2_scr[...] = w2_ref[0].astype(BF16)

    @pl.when(live)
    def _():
        x_lo, x_hi = _unpack_rows(xs_ref[...])

        def up(w_scr, b_ref):
            return _dot(x_lo, w_scr[:ROW_WORDS, :]) + _dot(x_hi, w_scr[ROW_WORDS:, :]) + b_ref[0]

        x_glu = jnp.minimum(up(w1g_scr, b1g_ref), SWIGLU_LIMIT)
        x_lin = jnp.clip(up(w1l_scr, b1l_ref), -SWIGLU_LIMIT, SWIGLU_LIMIT)
        act = x_glu * jax.nn.sigmoid(SWIGLU_ALPHA * x_glu) * (x_lin + 1.0)
        y_ref[...] = _pack_rows(_dot(act.astype(BF16), w2_scr[...]) + b2_ref[0])


def _experts(blk_e, n_used, xs, w1, b1g, b1l, w2, b2, sel):
    n_rows = xs.shape[0]
    nblk = n_rows // MOE_BM
    wspec = lambda r, c: pl.BlockSpec((1, r, c), lambda i, be, nb: (be[i], 0, 0))
    return pl.pallas_call(
        _experts_kernel,
        grid_spec=pltpu.PrefetchScalarGridSpec(
            num_scalar_prefetch=2,
            grid=(nblk,),
            in_specs=[pl.BlockSpec((MOE_BM, ROW_WORDS), lambda i, be, nb: (jnp.minimum(i, nb[0] - 1), 0)),
                      wspec(D_MODEL, 2 * D_FF), wspec(1, D_FF), wspec(1, D_FF),
                      wspec(D_FF, D_MODEL), wspec(1, D_MODEL),
                      pl.BlockSpec((2 * LANE, 2 * LANE), lambda i, be, nb: (0, 0))],
            out_specs=pl.BlockSpec((MOE_BM, ROW_WORDS), lambda i, be, nb: (i, 0)),
            scratch_shapes=[pltpu.VMEM((D_MODEL, D_FF), BF16), pltpu.VMEM((D_MODEL, D_FF), BF16),
                            pltpu.VMEM((D_FF, D_MODEL), BF16)],
        ),
        out_shape=jax.ShapeDtypeStruct(xs.shape, U32),
        compiler_params=_cparams(("arbitrary",)),
        name="experts",
    )(blk_e, n_used, xs, w1, b1g, b1l, w2, b2, sel)


def _combine_kernel(cnt_ref, off_ref, dest_ref, tot_ref, lp_ref, g_ref, h_ref, yb_ref, y_ref, rows, sem, *, tile0):
    @pl.when(pl.program_id(0) == 0)
    def _():
        rows[...] = jnp.zeros_like(rows)

    _run_copies(cnt_ref, off_ref, dest_ref, tot_ref, pl.program_id(0) + tile0, rows, yb_ref, sem, to_global=False)
    lp, g = lp_ref[...], g_ref[...]
    r = lax.broadcasted_iota(I32, (TOK_TILE, SORT_ROWS), 1)
    mix = jnp.zeros((TOK_TILE, SORT_ROWS), F32)
    for k in range(TOP_K):
        mix = jnp.where(r == lp[:, k:k + 1], g[:, k:k + 1], mix)
    mix = mix.astype(BF16)
    y_lo, y_hi = _unpack_rows(rows[...])
    y_ref[...] = h_ref[...] + jnp.concatenate([_dot(mix, y_lo), _dot(mix, y_hi)], axis=-1)


def _combine(runs, lp_t, gate_t, h, yb, tile0):
    T = h.shape[0]
    tm = TOK_TILE
    tok = lambda w: pl.BlockSpec((tm, w), lambda i, *_: (i, 0))
    return pl.pallas_call(
        functools.partial(_combine_kernel, tile0=tile0),
        grid_spec=pltpu.PrefetchScalarGridSpec(
            num_scalar_prefetch=4,
            grid=(T // tm,),
            in_specs=[tok(TOP_K), tok(TOP_K), tok(D_MODEL), pl.BlockSpec(memory_space=pl.ANY)],
            out_specs=tok(D_MODEL),
            scratch_shapes=[pltpu.VMEM((SORT_ROWS, ROW_WORDS), U32), pltpu.SemaphoreType.DMA],
        ),
        out_shape=jax.ShapeDtypeStruct((T, D_MODEL), F32),
        compiler_params=_cparams(("arbitrary",)),
        name="combine",
    )(*runs, lp_t, gate_t, h, yb)


def _block_ones(width, block):
    idx = np.arange(width) // block
    return jnp.asarray(idx[:, None] == idx[None, :], BF16)


def _layer(x_p, x_s, cache_k, cache_v, s_gla, norm1, w_in, q_norm, k_norm, sinks, w_gk, b_gk,
           gla_norm, w_a, w_b, w_o, norm2, w_router, b_router, w1, b1, w2, b2):
    Bp, Lp, _ = x_p.shape
    Bs, Ls, _ = x_s.shape
    Tp, Ts = Bp * Lp, Bs * Ls

    offs = np.cumsum((0, A_WIDTH, KV_WIDTH, KV_WIDTH, GLA_K_WIDTH, GLA_K_WIDTH, B_WIDTH, GLA_GATE_RANK, B_WIDTH,
                      D_MODEL, D_MODEL))
    src = dict(zip(("qa", "ka", "va", "qb", "kb", "vb", "gl", "rb", "ga", "gb"), zip(offs[:-1], offs[1:])))
    cols = []
    for name, width in _SEGS:
        a, b = src[name]
        piece = w_in[:, a:b]
        if b - a < width:
            piece = jnp.pad(piece, ((0, 0), (0, width - (b - a))))
        cols.append(piece)
    w_packed = jnp.concatenate(cols, axis=1).astype(BF16)
    wgk = jnp.pad(w_gk, ((0, GL_PAD - GLA_GATE_RANK), (0, 0))).astype(BF16)
    n1 = norm1.reshape(1, D_MODEL)
    gq, gk = _block_ones(A_WIDTH, SWA_HEAD_DIM), _block_ones(KV_WIDTH, SWA_HEAD_DIM)
    qn = jnp.tile(q_norm, SWA_HEADS).reshape(1, A_WIDTH)
    kn = jnp.tile(k_norm, SWA_KV_HEADS).reshape(1, KV_WIDTH)
    bgk = b_gk.reshape(1, GLA_K_WIDTH)
    gn = gla_norm.reshape(1, GLA_DV)
    wa, wb, wo = w_a.astype(BF16), w_b.astype(BF16), w_o.astype(BF16)
    n2 = norm2.reshape(1, D_MODEL)
    wr = jnp.pad(w_router, ((0, 0), (0, LANE - N_EXPERTS)))
    br = jnp.pad(b_router, (0, LANE - N_EXPERTS)).reshape(1, LANE)
    b1g = b1[:, 0::2].reshape(N_EXPERTS, 1, D_FF)
    b1l = b1[:, 1::2].reshape(N_EXPERTS, 1, D_FF)
    b2r = b2.reshape(N_EXPERTS, 1, D_MODEL)
    sel_np = np.zeros((2 * LANE, 2 * LANE), np.float32)
    sel_np[2 * np.arange(LANE), np.arange(LANE)] = 1.0
    sel_np[2 * np.arange(LANE) + 1, LANE + np.arange(LANE)] = 1.0
    sel = jnp.asarray(sel_np, BF16)

    def mix(x, k_past, v_past, s0):
        B, L, _ = x.shape
        T = B * L
        x2 = x.reshape(T, D_MODEL)
        qa, ka, va, qb, kb, vb, la, rb, ga, gb = _inproj(x2, n1, w_packed, gq, gk, qn, kn, wgk, bgk)
        r3 = lambda t: t.reshape(B, L, t.shape[-1])
        oa = _swa(sinks, r3(qa), r3(ka), r3(va), k_past, v_past, has_past=k_past is not None)
        ob, s_out = _gla(r3(qb), r3(kb), r3(vb), r3(la), r3(rb), gn, s0)
        h, hn, gate, lp, cnt = _outproj(x2, oa.reshape(T, A_WIDTH), ob.reshape(T, B_WIDTH), ga, gb,
                                        wa, wb, wo, n2, wr, br, upper, lower)
        return r3(ka), r3(va), s_out, h, hn, gate, lp, cnt[:, 0].reshape(T // TOK_TILE, N_EXPERTS)

    upper = jnp.asarray(np.arange(TOK_TILE)[:, None] < np.arange(TOK_TILE)[None, :], BF16)
    lower = jnp.asarray(np.arange(N_EXPERTS)[:, None] > np.arange(N_EXPERTS)[None, :], BF16)
    win = cache_k.shape[1]
    assert win == WINDOW
    ck = cache_k.reshape(Bs, win, KV_WIDTH)
    cv = cache_v.reshape(Bs, win, KV_WIDTH)
    ka_p, va_p, s_p, h_p, hn_p, g_p, lp_p, cnt_p = mix(x_p, None, None, jnp.zeros((Bp, GLA_HEADS, GLA_DK, GLA_DV), F32))
    ka_s, va_s, s_s, h_s, hn_s, g_s, lp_s, cnt_s = mix(x_s, ck, cv, s_gla)

    T = Tp + Ts
    cnt = jnp.concatenate([cnt_p, cnt_s], axis=0)
    most_rows = T * TOP_K + cnt.size * (RUN_ALIGN - 1) + N_EXPERTS * (MOE_BM - 1)
    n_rows = -(-most_rows // MOE_BM) * MOE_BM
    counts = jnp.sum(cnt, axis=0)
    padded = (counts + MOE_BM - 1) // MOE_BM * MOE_BM
    pad_end = jnp.cumsum(padded)
    start = pad_end - padded
    dest = start[None, :] + jnp.cumsum(cnt, axis=0) - cnt
    off = jnp.cumsum(cnt, axis=1) - cnt
    flat = lambda a: a.reshape(-1).astype(I32)
    runs = (flat(cnt), flat(off), flat(dest), flat(jnp.sum(cnt, axis=1)))
    blk_row = jnp.arange(n_rows // MOE_BM, dtype=I32) * MOE_BM
    blk_e = jnp.minimum(jnp.sum(pad_end[None, :] <= blk_row[:, None], axis=1), N_EXPERTS - 1).astype(I32)
    n_used = (pad_end[-1:] // MOE_BM).astype(I32)
    tails = jnp.where(padded > 0, pad_end - MOE_BM, -1)
    spare = pad_end[-1] + blk_row[:(n_rows - T * TOP_K) // MOE_BM]
    fill = jnp.concatenate([tails, jnp.where(spare < n_rows, spare, -1)]).astype(I32)

    xs = _dispatch(fill, runs, jnp.concatenate([lp_p, lp_s], axis=1), hn_p, hn_s, n_rows)
    yb = _experts(blk_e, n_used, xs, w1, b1g, b1l, w2, b2r, sel)
    y_p = _combine(runs, lp_p.T, g_p.T, h_p, yb, 0).reshape(Bp, Lp, D_MODEL)
    y_s = _combine(runs, lp_s.T, g_s.T, h_s, yb, Tp // TOK_TILE).reshape(Bs, Ls, D_MODEL)

    kv = lambda t: t.reshape(t.shape[0], t.shape[1], SWA_KV_HEADS, SWA_HEAD_DIM)
    k_p, v_p = kv(ka_p[:, -WINDOW:]), kv(va_p[:, -WINDOW:])
    k_s = jnp.concatenate([cache_k, kv(ka_s)], axis=1)[:, -win:]
    v_s = jnp.concatenate([cache_v, kv(va_s)], axis=1)[:, -win:]
    return y_p, y_s, (k_p, v_p, s_p, k_s, v_s, s_s)


def kernel(x_prompt, x_sample, cache_swa_k, cache_swa_v, state_gla, norm1, w_in, q_norm, k_norm, sinks, w_gk, b_gk,
           gla_norm, w_a, w_b, w_o, norm2, w_router, b_router, w1, b1, w2, b2):
    depth = norm1.shape[0]
    y_p, y_s = x_prompt, x_sample
    states = []
    for l in range(depth):
        y_p, y_s, st = _layer(y_p, y_s, cache_swa_k[l], cache_swa_v[l], state_gla[l], norm1[l], w_in[l], q_norm[l],
                              k_norm[l], sinks[l], w_gk[l], b_gk[l], gla_norm[l], w_a[l], w_b[l], w_o[l], norm2[l],
                              w_router[l], b_router[l], w1[l], b1[l], w2[l], b2[l])
        states.append(st)
    return (y_p, y_s) + tuple(jnp.stack([st[j] for st in states]) for j in range(6))
```

```python
import functools

import jax
import jax.numpy as jnp
import numpy as np
from jax import lax
from jax.experimental import pallas as pl
from jax.experimental.pallas import tpu as pltpu

F32 = jnp.float32
BF16 = jnp.bfloat16
I32 = jnp.int32
U32 = jnp.uint32

D_MODEL = 1024
SWA_HEADS = 8
SWA_KV_HEADS = 2
SWA_GROUP = SWA_HEADS // SWA_KV_HEADS
SWA_HEAD_DIM = 64
WINDOW = 128
ATTN_SCALE = SWA_HEAD_DIM ** -0.5
GLA_HEADS = 4
GLA_DK = 64
GLA_DV = 128
GLA_GATE_RANK = 16
GLA_GATE_NORM = 16.0
GLA_SCALE = GLA_DK ** -0.5
A_WIDTH = SWA_HEADS * SWA_HEAD_DIM
KV_WIDTH = SWA_KV_HEADS * SWA_HEAD_DIM
GLA_K_WIDTH = GLA_HEADS * GLA_DK
B_WIDTH = GLA_HEADS * GLA_DV
N_EXPERTS = 32
TOP_K = 4
D_FF = D_MODEL
SWIGLU_ALPHA = 1.702
SWIGLU_LIMIT = 7.0
NORM_EPS = 1e-5
QK_EPS = 1e-6
NEG_INF = -1e30

LANE = 128
GL_PAD = LANE
VMEM_LIMIT = 56 * 1024 * 1024

_SEGS = (("qa", A_WIDTH), ("ka", KV_WIDTH), ("va", KV_WIDTH), ("qb", GLA_K_WIDTH), ("kb", GLA_K_WIDTH),
         ("vb", B_WIDTH), ("rb", B_WIDTH), ("ga", D_MODEL), ("gb", D_MODEL), ("gl", GL_PAD))
_OFF = {}
_o = 0
for _n, _w in _SEGS:
    _OFF[_n] = (_o, _w)
    _o += _w
IN_PACKED = _o

TOK_TILE = 512
GLA_CHUNK = 64
GLA_SUB = 16
MOE_BM = 512
ROW_WORDS = D_MODEL // 2
RUN_ALIGN = 8
SORT_CHUNK = 256
SORT_ROWS = -(-(TOK_TILE * TOP_K + N_EXPERTS * (RUN_ALIGN - 1)) // SORT_CHUNK) * SORT_CHUNK
SAMPLE_BATCH = 8
GLA_LONG_BATCH = 2


def _cparams(sem):
    return pltpu.CompilerParams(dimension_semantics=sem, vmem_limit_bytes=VMEM_LIMIT)


def _split_bf16(v):
    hi = v.astype(BF16)
    lo = (v - hi.astype(F32)).astype(BF16)
    return hi, lo


def _dot(a, b):
    return jnp.dot(a, b, preferred_element_type=F32)


def _dot_nt(a, b):
    return lax.dot_general(a, b, (((1,), (1,)), ((), ())), preferred_element_type=F32)


def _dot_tn(a, b):
    return lax.dot_general(a, b, (((0,), (0,)), ((), ())), preferred_element_type=F32)


def _inproj_kernel(x_ref, n1_ref, w_ref, gq_ref, gk_ref, qn_ref, kn_ref, wgk_ref, bgk_ref,
                   qa_ref, ka_ref, va_ref, qb_ref, kb_ref, vb_ref, la_ref, rb_ref, ga_ref, gb_ref):
    x = x_ref[...]
    ms = jnp.mean(x * x, axis=-1, keepdims=True)
    xn = (x * lax.rsqrt(ms + NORM_EPS) * n1_ref[...]).astype(BF16)

    def proj(name):
        off, width = _OFF[name]
        return _dot(xn, w_ref[:, off:off + width])

    def head_norm(v, ones_ref, gain_ref):
        hi, lo = _split_bf16(v * v)
        ss = _dot(hi, ones_ref[...]) + _dot(lo, ones_ref[...])
        return v * lax.rsqrt(ss * (1.0 / SWA_HEAD_DIM) + QK_EPS) * gain_ref[...]

    qa_ref[...] = (head_norm(proj("qa"), gq_ref, qn_ref) * ATTN_SCALE).astype(BF16)
    ka_ref[...] = head_norm(proj("ka"), gk_ref, kn_ref)
    va_ref[...] = proj("va")
    qb_ref[...] = (proj("qb") * GLA_SCALE).astype(BF16)
    kb_ref[...] = proj("kb").astype(BF16)
    vb_ref[...] = proj("vb").astype(BF16)
    z = _dot(proj("gl").astype(BF16), wgk_ref[...]) + bgk_ref[...]
    la_ref[...] = (jnp.minimum(z, 0.0) - jnp.log1p(jnp.exp(-jnp.abs(z)))) * (1.0 / GLA_GATE_NORM)
    rb = proj("rb")
    rb_ref[...] = (rb * jax.nn.sigmoid(rb)).astype(BF16)
    ga_ref[...] = jax.nn.sigmoid(proj("ga")).astype(BF16)
    gb_ref[...] = jax.nn.sigmoid(proj("gb")).astype(BF16)


def _inproj(x2, n1, w_packed, gq, gk, qn, kn, wgk, bgk):
    T = x2.shape[0]
    tm = min(TOK_TILE, T)
    assert T % tm == 0

    def tok(width):
        return pl.BlockSpec((tm, width), lambda i: (i, 0))

    def const(shape):
        return pl.BlockSpec(shape, lambda i: (0, 0))

    outs = (("qa", A_WIDTH, BF16), ("ka", KV_WIDTH, F32), ("va", KV_WIDTH, F32), ("qb", GLA_K_WIDTH, BF16),
            ("kb", GLA_K_WIDTH, BF16), ("vb", B_WIDTH, BF16), ("la", GLA_K_WIDTH, F32), ("rb", B_WIDTH, BF16),
            ("ga", D_MODEL, BF16), ("gb", D_MODEL, BF16))
    return pl.pallas_call(
        _inproj_kernel,
        grid=(T // tm,),
        in_specs=[tok(D_MODEL), const((1, D_MODEL)), const((D_MODEL, IN_PACKED)), const((A_WIDTH, A_WIDTH)),
                  const((KV_WIDTH, KV_WIDTH)), const((1, A_WIDTH)), const((1, KV_WIDTH)),
                  const((GL_PAD, GLA_K_WIDTH)), const((1, GLA_K_WIDTH))],
        out_specs=[tok(w) for _, w, _ in outs],
        out_shape=[jax.ShapeDtypeStruct((T, w), dt) for _, w, dt in outs],
        compiler_params=_cparams(("parallel",)),
        name="inproj",
    )(x2, n1, w_packed, gq, gk, qn, kn, wgk, bgk)


def _swa_kernel(sink_ref, q_ref, kc_ref, vc_ref, kp_ref, vp_ref, o_ref, *, first_block_has_past):
    n = pl.program_id(1)
    nb, lq = q_ref.shape[0], q_ref.shape[1]
    assert lq & (lq - 1) == 0
    stack = SWA_GROUP if lq < WINDOW else 1
    rows, keys = stack * lq, WINDOW + lq
    qi = lax.broadcasted_iota(I32, (rows, keys), 0) & (lq - 1)
    ci = lax.broadcasted_iota(I32, (rows, keys), 1)
    dist = WINDOW + qi - ci
    mask = (dist >= 0) & (dist < WINDOW)
    if not first_block_has_past:
        mask = mask & ((ci >= WINDOW) | (n > 0))
    cols = lambda h0: slice(h0 // SWA_GROUP * SWA_HEAD_DIM, (h0 // SWA_GROUP + 1) * SWA_HEAD_DIM)
    qs, ks, vs = [], [], []
    for j in range(nb):
        q = q_ref[j]
        qs.append(q.astype(F32) if stack > 1 else q)
        ks.append(jnp.concatenate([kp_ref[j], kc_ref[j]], axis=0).astype(BF16))
        vs.append(jnp.concatenate([vp_ref[j], vc_ref[j]], axis=0).astype(BF16))
    groups = [(j, h0) for j in range(nb) for h0 in range(0, SWA_HEADS, stack)]
    wave = len(groups) if stack > 1 else 1
    outs = {}
    for w0 in range(0, len(groups), wave):
        scores, sinks = [], []
        for j, h0 in groups[w0:w0 + wave]:
            heads = range(h0, h0 + stack)
            qg = jnp.concatenate([qs[j][:, h * SWA_HEAD_DIM:(h + 1) * SWA_HEAD_DIM] for h in heads], axis=0)
            sinks.append(jnp.concatenate([jnp.full((lq, 1), sink_ref[h], F32) for h in heads], axis=0))
            scores.append(_dot_nt(qg.astype(BF16), ks[j][:, cols(h0)]))
        probs, denoms = [], []
        for s, sink in zip(scores, sinks):
            s = jnp.where(mask, s, NEG_INF)
            m = jnp.maximum(jnp.max(s, axis=-1, keepdims=True), sink)
            p = jnp.exp(s - m)
            denoms.append(jnp.sum(p, axis=-1, keepdims=True) + jnp.exp(sink - m))
            probs.append(p.astype(BF16))
        for (j, h0), p, denom in zip(groups[w0:w0 + wave], probs, denoms):
            o = _dot(p, vs[j][:, cols(h0)]) / denom
            outs.setdefault(j, []).extend(o[g * lq:(g + 1) * lq] for g in range(stack))
    for j in range(nb):
        o_ref[j] = jnp.concatenate(outs[j], axis=-1).astype(BF16)


def _swa(sinks, qa, ka, va, k_past, v_past, *, has_past):
    B, L, _ = qa.shape
    lq = min(L, WINDOW)
    nl = L // lq
    if has_past:
        assert nl == 1
        nb = min(B, SAMPLE_BATCH)
        past_spec = pl.BlockSpec((nb, WINDOW, KV_WIDTH), lambda b, n, s: (b, 0, 0))
        kp, vp = k_past, v_past
    else:
        assert lq == WINDOW
        nb = 1
        past_spec = pl.BlockSpec((nb, WINDOW, KV_WIDTH), lambda b, n, s: (b, jnp.maximum(n - 1, 0), 0))
        kp, vp = ka, va
    assert B % nb == 0
    cur = lambda w: pl.BlockSpec((nb, lq, w), lambda b, n, s: (b, n, 0))
    return pl.pallas_call(
        functools.partial(_swa_kernel, first_block_has_past=has_past),
        grid_spec=pltpu.PrefetchScalarGridSpec(
            num_scalar_prefetch=1,
            grid=(B // nb, nl),
            in_specs=[cur(A_WIDTH), cur(KV_WIDTH), cur(KV_WIDTH), past_spec, past_spec],
            out_specs=cur(A_WIDTH),
        ),
        out_shape=jax.ShapeDtypeStruct((B, L, A_WIDTH), BF16),
        compiler_params=_cparams(("parallel", "parallel")),
        name="swa",
    )(sinks, qa, ka, va, kp, vp)


def _gla_kernel(q_ref, k_ref, v_ref, la_ref, rb_ref, gn_ref, s0_ref, o_ref, sout_ref, s_scr, *, chunk, sub, length):
    C, SUB = chunk, sub
    S = C // SUB
    W = GLA_K_WIDTH
    nb = q_ref.shape[0]
    s_scr[...] = s0_ref[...]
    row = lax.broadcasted_iota(I32, (C, C), 0)
    col = lax.broadcasted_iota(I32, (C, C), 1)
    tri = (row >= col).astype(BF16)
    diag_mask = (row >= col) & ((row // SUB) == (col // SUB))
    eye = lax.broadcasted_iota(I32, (GLA_DK, GLA_DK), 0) == lax.broadcasted_iota(I32, (GLA_DK, GLA_DK), 1)
    krow = lax.broadcasted_iota(I32, (C, W), 0)

    def rows_of(x, r):
        return jnp.broadcast_to(x[r:r + 1, :], (SUB, W))

    ksl = lambda h: slice(h * GLA_DK, (h + 1) * GLA_DK)
    vsl = lambda h: slice(h * GLA_DV, (h + 1) * GLA_DV)

    def decays(j, r0):
        g_hi, g_lo = _split_bf16(la_ref[j, pl.ds(r0, C), :])
        return _dot(tri, g_hi) + _dot(tri, g_lo)

    def factors(j, r0, b):
        q = q_ref[j, pl.ds(r0, C), :].astype(F32)
        k = k_ref[j, pl.ds(r0, C), :].astype(F32)
        b_last = b[C - 1:C, :]
        mid = jnp.concatenate([rows_of(b, i * SUB + SUB // 2 - 1) for i in range(S)], axis=0)
        beta = jnp.concatenate([jnp.zeros((SUB, W), F32)] + [rows_of(b, i * SUB - 1) for i in range(1, S)], axis=0)
        k_off = [None]
        for i in range(1, S):
            e = jnp.exp(jnp.minimum(b[i * SUB - 1:i * SUB, :] - b, 0.0))
            k_off.append(jnp.where(krow < i * SUB, k * e, 0.0).astype(BF16))
        return dict(q_inter=(q * jnp.exp(b)).astype(BF16), k_state=(k * jnp.exp(b_last - b)).astype(BF16),
                    q_diag=(q * jnp.exp(b - mid)).astype(BF16), k_diag=(k * jnp.exp(mid - b)).astype(BF16),
                    q_off=(q * jnp.exp(b - beta)).astype(BF16), k_off=k_off, dec=jnp.exp(b_last))

    def intra(f, h):
        a_diag = _dot_nt(f["q_diag"][:, ksl(h)], f["k_diag"][:, ksl(h)])
        if S == 1:
            return jnp.where(diag_mask, a_diag, 0.0).astype(BF16)
        blocks = [jnp.zeros((SUB, C), F32)]
        for i in range(1, S):
            blocks.append(_dot_nt(f["q_off"][i * SUB:(i + 1) * SUB, ksl(h)], f["k_off"][i][:, ksl(h)]))
        return jnp.where(diag_mask, a_diag, jnp.concatenate(blocks, axis=0)).astype(BF16)

    def read_out(j, r0, f, a):
        v = v_ref[j, pl.ds(r0, C), :]
        res = []
        for h in range(GLA_HEADS):
            o_h = _dot(a[h], v[:, vsl(h)]) + _dot(f["q_inter"][:, ksl(h)], s_scr[j, h].astype(BF16))
            res.append((o_h, _dot_tn(f["k_state"][:, ksl(h)], v[:, vsl(h)])))
        return res

    def finish(j, r0, f, res):
        outs = []
        for h, (o_h, s_inc) in enumerate(res):
            dec = jnp.broadcast_to(f["dec"][:, ksl(h)], (GLA_DK, GLA_DK))
            dec_col = jnp.sum(jnp.where(eye, dec, 0.0), axis=-1, keepdims=True)
            s_scr[j, h] = dec_col * s_scr[j, h] + s_inc
            ms = jnp.mean(o_h * o_h, axis=-1, keepdims=True)
            outs.append(o_h * lax.rsqrt(ms + NORM_EPS) * gn_ref[...])
        o = jnp.concatenate(outs, axis=-1) * rb_ref[j, pl.ds(r0, C), :].astype(F32)
        o_ref[j, pl.ds(r0, C), :] = o.astype(BF16)

    def step(c):
        r0 = c * C if isinstance(c, int) else pl.multiple_of(c * C, C)
        bs = [decays(j, r0) for j in range(nb)]
        fs = [factors(j, r0, b) for j, b in enumerate(bs)]
        attn = [[intra(f, h) for h in range(GLA_HEADS)] for f in fs]
        res = [read_out(j, r0, fs[j], attn[j]) for j in range(nb)]
        for j in range(nb):
            finish(j, r0, fs[j], res[j])

    if length == C:
        step(0)
    else:
        lax.fori_loop(0, length // C, lambda c, carry: (step(c), carry)[1], 0)
    sout_ref[...] = s_scr[...]


def _gla(qb, kb, vb, la, rb, gn, s0):
    B, L, _ = qb.shape
    chunk = GLA_CHUNK if L % GLA_CHUNK == 0 else L
    sub = GLA_SUB if chunk % GLA_SUB == 0 else chunk
    nb = min(B, SAMPLE_BATCH) if L == chunk else min(B, GLA_LONG_BATCH)
    assert B % nb == 0
    seq = lambda w: pl.BlockSpec((nb, L, w), lambda b: (b, 0, 0))
    st = pl.BlockSpec((nb, GLA_HEADS, GLA_DK, GLA_DV), lambda b: (b, 0, 0, 0))
    return pl.pallas_call(
        functools.partial(_gla_kernel, chunk=chunk, sub=sub, length=L),
        grid=(B // nb,),
        in_specs=[seq(GLA_K_WIDTH), seq(GLA_K_WIDTH), seq(B_WIDTH), seq(GLA_K_WIDTH), seq(B_WIDTH),
                  pl.BlockSpec((1, GLA_DV), lambda b: (0, 0)), st],
        out_specs=[seq(B_WIDTH), st],
        out_shape=[jax.ShapeDtypeStruct((B, L, B_WIDTH), BF16),
                   jax.ShapeDtypeStruct((B, GLA_HEADS, GLA_DK, GLA_DV), F32)],
        scratch_shapes=[pltpu.VMEM((nb, GLA_HEADS, GLA_DK, GLA_DV), F32)],
        compiler_params=_cparams(("parallel",)),
        name="gla",
    )(qb, kb, vb, la, rb, gn, s0)


def _outproj_kernel(x_ref, oa_ref, ob_ref, ga_ref, gb_ref, wa_ref, wb_ref, wo_ref, n2_ref, wr_ref, br_ref, upper_ref,
                    lower_ref, h_ref, hn_ref, g_ref, lp_ref, cnt_ref):
    tm = x_ref.shape[0]
    merged = (ga_ref[...].astype(F32) * _dot(oa_ref[...], wa_ref[...])
              + gb_ref[...].astype(F32) * _dot(ob_ref[...], wb_ref[...]))
    h = x_ref[...] + _dot(merged.astype(BF16), wo_ref[...])
    h_ref[...] = h
    ms = jnp.mean(h * h, axis=-1, keepdims=True)
    hn = h * lax.rsqrt(ms + NORM_EPS) * n2_ref[...]
    hn_ref[...] = hn.astype(BF16)
    hn_hi, hn_lo = _split_bf16(hn)
    wr_hi, wr_lo = _split_bf16(wr_ref[...])
    logits = _dot(hn_hi, wr_hi) + _dot(hn_lo, wr_hi) + _dot(hn_hi, wr_lo) + br_ref[...]
    lt = logits.T[:N_EXPERTS, :]
    eid = lax.broadcasted_iota(I32, (N_EXPERTS, tm), 0)
    vals, hots = [], []
    for k in range(TOP_K):
        m = jnp.max(lt, axis=0, keepdims=True)
        idx = jnp.min(jnp.where(lt == m, eid, N_EXPERTS), axis=0, keepdims=True)
        hot = eid == idx
        lt = jnp.where(hot, -jnp.inf, lt)
        vals.append(m)
        hots.append(hot)
    ex = [jnp.exp(vk - vals[0]) for vk in vals]
    den = ex[0] + ex[1] + ex[2] + ex[3]
    for k in range(TOP_K):
        g_ref[k:k + 1, :] = ex[k] / den
    multi = (hots[0] | hots[1] | hots[2] | hots[3]).astype(BF16)
    before = _dot(multi, upper_ref[...])
    counts = jnp.sum(multi.astype(F32), axis=1, keepdims=True)
    aligned = jnp.floor((counts + (RUN_ALIGN - 1)) * (1.0 / RUN_ALIGN)) * RUN_ALIGN
    run_start = _dot(lower_ref[...], jnp.broadcast_to(aligned, (N_EXPERTS, LANE)).astype(BF16))[:, :1]
    place = before + run_start
    for k in range(TOP_K):
        lp_ref[k:k + 1, :] = jnp.sum(jnp.where(hots[k], place, 0.0), axis=0, keepdims=True).astype(I32)
    cnt_ref[...] = jnp.broadcast_to(aligned, cnt_ref.shape).astype(I32)


def _outproj(x2, oa, ob, ga, gb, wa, wb, wo, n2, wr, br, upper, lower):
    T = x2.shape[0]
    tm = TOK_TILE
    assert T % tm == 0 and upper.shape == (tm, tm)
    tok = lambda w: pl.BlockSpec((tm, w), lambda i: (i, 0))
    const = lambda shape: pl.BlockSpec(shape, lambda i: (0, 0))
    kt = pl.BlockSpec((TOP_K, tm), lambda i: (0, i))
    return pl.pallas_call(
        _outproj_kernel,
        grid=(T // tm,),
        in_specs=[tok(D_MODEL), tok(A_WIDTH), tok(B_WIDTH), tok(D_MODEL), tok(D_MODEL),
                  const((A_WIDTH, D_MODEL)), const((B_WIDTH, D_MODEL)), const((D_MODEL, D_MODEL)),
                  const((1, D_MODEL)), const((D_MODEL, LANE)), const((1, LANE)), const((tm, tm)),
                  const((N_EXPERTS, N_EXPERTS))],
        out_specs=[tok(D_MODEL), tok(D_MODEL), kt, kt, pl.BlockSpec((N_EXPERTS, LANE), lambda i: (i, 0))],
        out_shape=[jax.ShapeDtypeStruct((T, D_MODEL), F32), jax.ShapeDtypeStruct((T, D_MODEL), BF16),
                   jax.ShapeDtypeStruct((TOP_K, T), F32), jax.ShapeDtypeStruct((TOP_K, T), I32),
                   jax.ShapeDtypeStruct((T // tm * N_EXPERTS, LANE), I32)],
        compiler_params=_cparams(("parallel",)),
        name="outproj",
    )(x2, oa, ob, ga, gb, wa, wb, wo, n2, wr, br, upper, lower)


def _pack_rows(v):
    lo = pltpu.bitcast(v[:, :ROW_WORDS].astype(BF16).astype(F32), U32) >> 16
    hi = pltpu.bitcast(v[:, ROW_WORDS:].astype(BF16).astype(F32), U32) & jnp.uint32(0xFFFF0000)
    return lo | hi


def _unpack_rows(w):
    lo = pltpu.bitcast(w << 16, F32).astype(BF16)
    hi = pltpu.bitcast(w & jnp.uint32(0xFFFF0000), F32).astype(BF16)
    return lo, hi


def _start_runs(runs, tile, tile_rows, global_rows, sem, *, to_global):
    cnt_ref, off_ref, dest_ref, _ = runs

    def run(e):
        j = tile * N_EXPERTS + e
        n = pl.multiple_of(cnt_ref[j], RUN_ALIGN)
        local = tile_rows.at[pl.ds(pl.multiple_of(off_ref[j], RUN_ALIGN), n), :]
        far = global_rows.at[pl.ds(pl.multiple_of(dest_ref[j], RUN_ALIGN), n), :]
        return pltpu.make_async_copy(local, far, sem) if to_global else pltpu.make_async_copy(far, local, sem)

    for e in range(N_EXPERTS):
        pl.when(cnt_ref[tile * N_EXPERTS + e] > 0)(lambda e=e: run(e).start())


def _wait_runs(runs, tile, tile_rows, global_rows, sem):
    total = pl.multiple_of(runs[3][tile], RUN_ALIGN)
    pltpu.make_async_copy(global_rows.at[pl.ds(0, total), :], tile_rows.at[pl.ds(0, total), :], sem).wait()


def _dispatch_kernel(fill_ref, cnt_ref, off_ref, dest_ref, tot_ref, lp_ref, hn_p_ref, hn_s_ref, xs_ref,
                     rows2, sems, zbuf, zsem, *, steps_p):
    i = pl.program_id(0)
    runs = (cnt_ref, off_ref, dest_ref, tot_ref)
    slot = i % 2
    rows = rows2.at[slot]

    @pl.when(i == 0)
    def _():
        zbuf[...] = jnp.zeros_like(zbuf)

        def fill_copy(j):
            return pltpu.make_async_copy(zbuf, xs_ref.at[pl.ds(pl.multiple_of(fill_ref[j], MOE_BM), MOE_BM), :], zsem)

        for j in range(fill_ref.shape[0]):
            pl.when(fill_ref[j] >= 0)(lambda j=j: fill_copy(j).start())
        for j in range(fill_ref.shape[0]):
            pl.when(fill_ref[j] >= 0)(lambda j=j: fill_copy(j).wait())

    def sort_tile(hn_ref):
        hn = hn_ref[...]
        lp = lp_ref[...]
        for c in range(SORT_ROWS // SORT_CHUNK):
            r = lax.broadcasted_iota(I32, (SORT_CHUNK, TOK_TILE), 0) + c * SORT_CHUNK
            hit = (r == lp[0:1, :]) | (r == lp[1:2, :]) | (r == lp[2:3, :]) | (r == lp[3:4, :])
            perm = jnp.where(hit, 1.0, 0.0).astype(BF16)
            rows[c * SORT_CHUNK:(c + 1) * SORT_CHUNK, :] = _pack_rows(_dot(perm, hn))

    pl.when(i < steps_p)(lambda: sort_tile(hn_p_ref))
    pl.when(i >= steps_p)(lambda: sort_tile(hn_s_ref))
    _start_runs(runs, i, rows, xs_ref, sems.at[slot], to_global=True)
    pl.when(i > 0)(lambda: _wait_runs(runs, i - 1, rows2.at[1 - slot], xs_ref, sems.at[1 - slot]))
    pl.when(i == pl.num_programs(0) - 1)(lambda: _wait_runs(runs, i, rows, xs_ref, sems.at[slot]))


def _dispatch(fill, runs, lp, hn_p, hn_s, n_rows):
    tm = TOK_TILE
    steps_p, steps_s = hn_p.shape[0] // tm, hn_s.shape[0] // tm
    return pl.pallas_call(
        functools.partial(_dispatch_kernel, steps_p=steps_p),
        grid_spec=pltpu.PrefetchScalarGridSpec(
            num_scalar_prefetch=5,
            grid=(steps_p + steps_s,),
            in_specs=[pl.BlockSpec((TOP_K, tm), lambda i, *_: (0, i)),
                      pl.BlockSpec((tm, D_MODEL), lambda i, *_: (jnp.minimum(i, steps_p - 1), 0)),
                      pl.BlockSpec((tm, D_MODEL), lambda i, *_: (jnp.maximum(i - steps_p, 0), 0))],
            out_specs=pl.BlockSpec(memory_space=pl.ANY),
            scratch_shapes=[pltpu.VMEM((2, SORT_ROWS, ROW_WORDS), U32), pltpu.SemaphoreType.DMA((2,)),
                            pltpu.VMEM((MOE_BM, ROW_WORDS), U32), pltpu.SemaphoreType.DMA],
        ),
        out_shape=jax.ShapeDtypeStruct((n_rows, ROW_WORDS), U32),
        compiler_params=_cparams(("arbitrary",)),
        name="dispatch",
    )(fill, *runs, lp, hn_p, hn_s)


def _experts_kernel(be_ref, nb_ref, xs_ref, w1_ref, b1g_ref, b1l_ref, w2_ref, b2_ref, sel_ref, y_ref,
                    w1g_scr, w1l_scr, w2_scr):
    i = pl.program_id(0)
    live = i < nb_ref[0]

    @pl.when(jnp.logical_not(live))
    def _():
        y_ref[...] = jnp.zeros_like(y_ref)

    @pl.when(live & ((i == 0) | (be_ref[i] != be_ref[jnp.maximum(i - 1, 0)])))
    def _():
        for c in range(D_FF // LANE):
            cols = _dot(w1_ref[0, :, c * 2 * LANE:(c + 1) * 2 * LANE].astype(BF16), sel_ref[...])
            w1g_scr[:, c * LANE:(c + 1) * LANE] = cols[:, :LANE].astype(BF16)
            w1l_scr[:, c * LANE:(c + 1) * LANE] = cols[:, LANE:].astype(BF16)
        w2_scr[...] = w2_ref[0].astype(BF16)

    @pl.when(live)
    def _():
        x_lo, x_hi = _unpack_rows(xs_ref[...])

        def up(w_scr, b_ref):
            return _dot(x_lo, w_scr[:ROW_WORDS, :]) + _dot(x_hi, w_scr[ROW_WORDS:, :]) + b_ref[0]

        x_glu = jnp.minimum(up(w1g_scr, b1g_ref), SWIGLU_LIMIT)
        x_lin = jnp.clip(up(w1l_scr, b1l_ref), -SWIGLU_LIMIT, SWIGLU_LIMIT)
        act = x_glu * jax.nn.sigmoid(SWIGLU_ALPHA * x_glu) * (x_lin + 1.0)
        y_ref[...] = _pack_rows(_dot(act.astype(BF16), w2_scr[...]) + b2_ref[0])


def _experts(blk_e, n_used, xs, w1, b1g, b1l, w2, b2, sel):
    n_rows = xs.shape[0]
    nblk = n_rows // MOE_BM
    wspec = lambda r, c: pl.BlockSpec((1, r, c), lambda i, be, nb: (be[i], 0, 0))
    return pl.pallas_call(
        _experts_kernel,
        grid_spec=pltpu.PrefetchScalarGridSpec(
            num_scalar_prefetch=2,
            grid=(nblk,),
            in_specs=[pl.BlockSpec((MOE_BM, ROW_WORDS), lambda i, be, nb: (jnp.minimum(i, nb[0] - 1), 0)),
                      wspec(D_MODEL, 2 * D_FF), wspec(1, D_FF), wspec(1, D_FF),
                      wspec(D_FF, D_MODEL), wspec(1, D_MODEL),
                      pl.BlockSpec((2 * LANE, 2 * LANE), lambda i, be, nb: (0, 0))],
            out_specs=pl.BlockSpec((MOE_BM, ROW_WORDS), lambda i, be, nb: (i, 0)),
            scratch_shapes=[pltpu.VMEM((D_MODEL, D_FF), BF16), pltpu.VMEM((D_MODEL, D_FF), BF16),
                            pltpu.VMEM((D_FF, D_MODEL), BF16)],
        ),
        out_shape=jax.ShapeDtypeStruct(xs.shape, U32),
        compiler_params=_cparams(("arbitrary",)),
        name="experts",
    )(blk_e, n_used, xs, w1, b1g, b1l, w2, b2, sel)


def _combine_kernel(cnt_ref, off_ref, dest_ref, tot_ref, lp_ref, g_ref, h_ref, yb_ref, y_ref, rows2, sems, *, tile0):
    i = pl.program_id(0)
    runs = (cnt_ref, off_ref, dest_ref, tot_ref)
    slot = i % 2
    rows = rows2.at[slot]

    def fetch(step, into):
        _start_runs(runs, step + tile0, rows2.at[into], yb_ref, sems.at[into], to_global=False)

    @pl.when(i == 0)
    def _():
        rows2[...] = jnp.zeros_like(rows2)
        fetch(0, 0)

    pl.when(i + 1 < pl.num_programs(0))(lambda: fetch(i + 1, 1 - slot))
    _wait_runs(runs, i + tile0, rows, yb_ref, sems.at[slot])
    lp, g = lp_ref[...], g_ref[...]
    r = lax.broadcasted_iota(I32, (TOK_TILE, SORT_ROWS), 1)
    mix = jnp.zeros((TOK_TILE, SORT_ROWS), F32)
    for k in range(TOP_K):
        mix = jnp.where(r == lp[:, k:k + 1], g[:, k:k + 1], mix)
    mix = mix.astype(BF16)
    y_lo, y_hi = _unpack_rows(rows[...])
    y_ref[...] = h_ref[...] + jnp.concatenate([_dot(mix, y_lo), _dot(mix, y_hi)], axis=-1)


def _combine(runs, lp_t, gate_t, h, yb, tile0):
    T = h.shape[0]
    tm = TOK_TILE
    tok = lambda w: pl.BlockSpec((tm, w), lambda i, *_: (i, 0))
    return pl.pallas_call(
        functools.partial(_combine_kernel, tile0=tile0),
        grid_spec=pltpu.PrefetchScalarGridSpec(
            num_scalar_prefetch=4,
            grid=(T // tm,),
            in_specs=[tok(TOP_K), tok(TOP_K), tok(D_MODEL), pl.BlockSpec(memory_space=pl.ANY)],
            out_specs=tok(D_MODEL),
            scratch_shapes=[pltpu.VMEM((2, SORT_ROWS, ROW_WORDS), U32), pltpu.SemaphoreType.DMA((2,))],
        ),
        out_shape=jax.ShapeDtypeStruct((T, D_MODEL), F32),
        compiler_params=_cparams(("arbitrary",)),
        name="combine",
    )(*runs, lp_t, gate_t, h, yb)


def _block_ones(width, block):
    idx = np.arange(width) // block
    return jnp.asarray(idx[:, None] == idx[None, :], BF16)


def _layer(x_p, x_s, cache_k, cache_v, s_gla, norm1, w_in, q_norm, k_norm, sinks, w_gk, b_gk,
           gla_norm, w_a, w_b, w_o, norm2, w_router, b_router, w1, b1, w2, b2):
    Bp, Lp, _ = x_p.shape
    Bs, Ls, _ = x_s.shape
    Tp, Ts = Bp * Lp, Bs * Ls

    offs = np.cumsum((0, A_WIDTH, KV_WIDTH, KV_WIDTH, GLA_K_WIDTH, GLA_K_WIDTH, B_WIDTH, GLA_GATE_RANK, B_WIDTH,
                      D_MODEL, D_MODEL))
    src = dict(zip(("qa", "ka", "va", "qb", "kb", "vb", "gl", "rb", "ga", "gb"), zip(offs[:-1], offs[1:])))
    cols = []
    for name, width in _SEGS:
        a, b = src[name]
        piece = w_in[:, a:b]
        if b - a < width:
            piece = jnp.pad(piece, ((0, 0), (0, width - (b - a))))
        cols.append(piece)
    w_packed = jnp.concatenate(cols, axis=1).astype(BF16)
    wgk = jnp.pad(w_gk, ((0, GL_PAD - GLA_GATE_RANK), (0, 0))).astype(BF16)
    n1 = norm1.reshape(1, D_MODEL)
    gq, gk = _block_ones(A_WIDTH, SWA_HEAD_DIM), _block_ones(KV_WIDTH, SWA_HEAD_DIM)
    qn = jnp.tile(q_norm, SWA_HEADS).reshape(1, A_WIDTH)
    kn = jnp.tile(k_norm, SWA_KV_HEADS).reshape(1, KV_WIDTH)
    bgk = b_gk.reshape(1, GLA_K_WIDTH)
    gn = gla_norm.reshape(1, GLA_DV)
    wa, wb, wo = w_a.astype(BF16), w_b.astype(BF16), w_o.astype(BF16)
    n2 = norm2.reshape(1, D_MODEL)
    wr = jnp.pad(w_router, ((0, 0), (0, LANE - N_EXPERTS)))
    br = jnp.pad(b_router, (0, LANE - N_EXPERTS)).reshape(1, LANE)
    b1g = b1[:, 0::2].reshape(N_EXPERTS, 1, D_FF)
    b1l = b1[:, 1::2].reshape(N_EXPERTS, 1, D_FF)
    b2r = b2.reshape(N_EXPERTS, 1, D_MODEL)
    sel_np = np.zeros((2 * LANE, 2 * LANE), np.float32)
    sel_np[2 * np.arange(LANE), np.arange(LANE)] = 1.0
    sel_np[2 * np.arange(LANE) + 1, LANE + np.arange(LANE)] = 1.0
    sel = jnp.asarray(sel_np, BF16)

    def mix(x, k_past, v_past, s0):
        B, L, _ = x.shape
        T = B * L
        x2 = x.reshape(T, D_MODEL)
        qa, ka, va, qb, kb, vb, la, rb, ga, gb = _inproj(x2, n1, w_packed, gq, gk, qn, kn, wgk, bgk)
        r3 = lambda t: t.reshape(B, L, t.shape[-1])
        oa = _swa(sinks, r3(qa), r3(ka), r3(va), k_past, v_past, has_past=k_past is not None)
        ob, s_out = _gla(r3(qb), r3(kb), r3(vb), r3(la), r3(rb), gn, s0)
        h, hn, gate, lp, cnt = _outproj(x2, oa.reshape(T, A_WIDTH), ob.reshape(T, B_WIDTH), ga, gb,
                                        wa, wb, wo, n2, wr, br, upper, lower)
        return r3(ka), r3(va), s_out, h, hn, gate, lp, cnt[:, 0].reshape(T // TOK_TILE, N_EXPERTS)

    upper = jnp.asarray(np.arange(TOK_TILE)[:, None] < np.arange(TOK_TILE)[None, :], BF16)
    lower = jnp.asarray(np.arange(N_EXPERTS)[:, None] > np.arange(N_EXPERTS)[None, :], BF16)
    win = cache_k.shape[1]
    assert win == WINDOW
    ck = cache_k.reshape(Bs, win, KV_WIDTH)
    cv = cache_v.reshape(Bs, win, KV_WIDTH)
    ka_p, va_p, s_p, h_p, hn_p, g_p, lp_p, cnt_p = mix(x_p, None, None, jnp.zeros((Bp, GLA_HEADS, GLA_DK, GLA_DV), F32))
    ka_s, va_s, s_s, h_s, hn_s, g_s, lp_s, cnt_s = mix(x_s, ck, cv, s_gla)

    T = Tp + Ts
    cnt = jnp.concatenate([cnt_p, cnt_s], axis=0)
    most_rows = T * TOP_K + cnt.size * (RUN_ALIGN - 1) + N_EXPERTS * (MOE_BM - 1)
    n_rows = -(-most_rows // MOE_BM) * MOE_BM
    counts = jnp.sum(cnt, axis=0)
    padded = (counts + MOE_BM - 1) // MOE_BM * MOE_BM
    pad_end = jnp.cumsum(padded)
    start = pad_end - padded
    dest = start[None, :] + jnp.cumsum(cnt, axis=0) - cnt
    off = jnp.cumsum(cnt, axis=1) - cnt
    flat = lambda a: a.reshape(-1).astype(I32)
    runs = (flat(cnt), flat(off), flat(dest), flat(jnp.sum(cnt, axis=1)))
    blk_row = jnp.arange(n_rows // MOE_BM, dtype=I32) * MOE_BM
    blk_e = jnp.minimum(jnp.sum(pad_end[None, :] <= blk_row[:, None], axis=1), N_EXPERTS - 1).astype(I32)
    n_used = (pad_end[-1:] // MOE_BM).astype(I32)
    tails = jnp.where(padded > 0, pad_end - MOE_BM, -1)
    spare = pad_end[-1] + blk_row[:(n_rows - T * TOP_K) // MOE_BM]
    fill = jnp.concatenate([tails, jnp.where(spare < n_rows, spare, -1)]).astype(I32)

    xs = _dispatch(fill, runs, jnp.concatenate([lp_p, lp_s], axis=1), hn_p, hn_s, n_rows)
    yb = _experts(blk_e, n_used, xs, w1, b1g, b1l, w2, b2r, sel)
    y_p = _combine(runs, lp_p.T, g_p.T, h_p, yb, 0).reshape(Bp, Lp, D_MODEL)
    y_s = _combine(runs, lp_s.T, g_s.T, h_s, yb, Tp // TOK_TILE).reshape(Bs, Ls, D_MODEL)

    kv = lambda t: t.reshape(t.shape[0], t.shape[1], SWA_KV_HEADS, SWA_HEAD_DIM)
    k_p, v_p = kv(ka_p[:, -WINDOW:]), kv(va_p[:, -WINDOW:])
    k_s = jnp.concatenate([cache_k, kv(ka_s)], axis=1)[:, -win:]
    v_s = jnp.concatenate([cache_v, kv(va_s)], axis=1)[:, -win:]
    return y_p, y_s, (k_p, v_p, s_p, k_s, v_s, s_s)


def kernel(x_prompt, x_sample, cache_swa_k, cache_swa_v, state_gla, norm1, w_in, q_norm, k_norm, sinks, w_gk, b_gk,
           gla_norm, w_a, w_b, w_o, norm2, w_router, b_router, w1, b1, w2, b2):
    depth = norm1.shape[0]
    y_p, y_s = x_prompt, x_sample
    states = []
    for l in range(depth):
        y_p, y_s, st = _layer(y_p, y_s, cache_swa_k[l], cache_swa_v[l], state_gla[l], norm1[l], w_in[l], q_norm[l],
                              k_norm[l], sinks[l], w_gk[l], b_gk[l], gla_norm[l], w_a[l], w_b[l], w_o[l], norm2[l],
                              w_router[l], b_router[l], w1[l], b1[l], w2[l], b2[l])
        states.append(st)
    return (y_p, y_s) + tuple(jnp.stack([st[j] for st in states]) for j in range(6))
```

```python
import functools

import jax
import jax.numpy as jnp
import numpy as np
from jax import lax
from jax.experimental import pallas as pl
from jax.experimental.pallas import tpu as pltpu

F32 = jnp.float32
BF16 = jnp.bfloat16
I32 = jnp.int32
U32 = jnp.uint32

D_MODEL = 1024
SWA_HEADS = 8
SWA_KV_HEADS = 2
SWA_GROUP = SWA_HEADS // SWA_KV_HEADS
SWA_HEAD_DIM = 64
WINDOW = 128
ATTN_SCALE = SWA_HEAD_DIM ** -0.5
GLA_HEADS = 4
GLA_DK = 64
GLA_DV = 128
GLA_GATE_RANK = 16
GLA_GATE_NORM = 16.0
GLA_SCALE = GLA_DK ** -0.5
A_WIDTH = SWA_HEADS * SWA_HEAD_DIM
KV_WIDTH = SWA_KV_HEADS * SWA_HEAD_DIM
GLA_K_WIDTH = GLA_HEADS * GLA_DK
B_WIDTH = GLA_HEADS * GLA_DV
N_EXPERTS = 32
TOP_K = 4
D_FF = D_MODEL
SWIGLU_ALPHA = 1.702
SWIGLU_LIMIT = 7.0
NORM_EPS = 1e-5
QK_EPS = 1e-6
NEG_INF = -1e30

LANE = 128
GL_PAD = LANE
VMEM_LIMIT = 56 * 1024 * 1024

_SEGS = (("qa", A_WIDTH), ("ka", KV_WIDTH), ("va", KV_WIDTH), ("qb", GLA_K_WIDTH), ("kb", GLA_K_WIDTH),
         ("vb", B_WIDTH), ("rb", B_WIDTH), ("ga", D_MODEL), ("gb", D_MODEL), ("gl", GL_PAD))
_OFF = {}
_o = 0
for _n, _w in _SEGS:
    _OFF[_n] = (_o, _w)
    _o += _w
IN_PACKED = _o

TOK_TILE = 512
GLA_CHUNK = 64
GLA_SUB = 16
MOE_BM = 256
ROW_WORDS = D_MODEL // 2
RUN_ALIGN = 8
SORT_CHUNK = 256
SORT_ROWS = -(-(TOK_TILE * TOP_K + N_EXPERTS * (RUN_ALIGN - 1)) // SORT_CHUNK) * SORT_CHUNK
SAMPLE_BATCH = 8
GLA_LONG_BATCH = 2


def _cparams(sem):
    return pltpu.CompilerParams(dimension_semantics=sem, vmem_limit_bytes=VMEM_LIMIT)


def _split_bf16(v):
    hi = v.astype(BF16)
    lo = (v - hi.astype(F32)).astype(BF16)
    return hi, lo


def _dot(a, b):
    return jnp.dot(a, b, preferred_element_type=F32)


def _dot_nt(a, b):
    return lax.dot_general(a, b, (((1,), (1,)), ((), ())), preferred_element_type=F32)


def _dot_tn(a, b):
    return lax.dot_general(a, b, (((0,), (0,)), ((), ())), preferred_element_type=F32)


def _inproj_kernel(x_ref, n1_ref, w_ref, gq_ref, gk_ref, qn_ref, kn_ref, wgk_ref, bgk_ref,
                   qa_ref, ka_ref, va_ref, qb_ref, kb_ref, vb_ref, la_ref, rb_ref, ga_ref, gb_ref):
    x = x_ref[...]
    ms = jnp.mean(x * x, axis=-1, keepdims=True)
    xn = (x * lax.rsqrt(ms + NORM_EPS) * n1_ref[...]).astype(BF16)

    def proj(name):
        off, width = _OFF[name]
        return _dot(xn, w_ref[:, off:off + width])

    def head_norm(v, ones_ref, gain_ref):
        hi, lo = _split_bf16(v * v)
        ss = _dot(hi, ones_ref[...]) + _dot(lo, ones_ref[...])
        return v * lax.rsqrt(ss * (1.0 / SWA_HEAD_DIM) + QK_EPS) * gain_ref[...]

    def log_decay(gl):
        z = _dot(gl.astype(BF16), wgk_ref[...]) + bgk_ref[...]
        return (jnp.minimum(z, 0.0) - jnp.log1p(jnp.exp(-jnp.abs(z)))) * (1.0 / GLA_GATE_NORM)

    plan = (("qa", qa_ref, lambda u: head_norm(u, gq_ref, qn_ref) * ATTN_SCALE),
            ("ka", ka_ref, lambda u: head_norm(u, gk_ref, kn_ref)),
            ("va", va_ref, lambda u: u),
            ("qb", qb_ref, lambda u: u * GLA_SCALE),
            ("kb", kb_ref, lambda u: u),
            ("vb", vb_ref, lambda u: u),
            ("gl", la_ref, log_decay),
            ("rb", rb_ref, lambda u: u * jax.nn.sigmoid(u)),
            ("ga", ga_ref, jax.nn.sigmoid),
            ("gb", gb_ref, jax.nn.sigmoid))
    u = proj(plan[0][0])
    for idx, (_, out_ref, epilogue) in enumerate(plan):
        u_next = proj(plan[idx + 1][0]) if idx + 1 < len(plan) else None
        out_ref[...] = epilogue(u).astype(out_ref.dtype)
        u = u_next


def _inproj(x2, n1, w_packed, gq, gk, qn, kn, wgk, bgk):
    T = x2.shape[0]
    tm = min(TOK_TILE, T)
    assert T % tm == 0

    def tok(width):
        return pl.BlockSpec((tm, width), lambda i: (i, 0))

    def const(shape):
        return pl.BlockSpec(shape, lambda i: (0, 0))

    outs = (("qa", A_WIDTH, BF16), ("ka", KV_WIDTH, F32), ("va", KV_WIDTH, F32), ("qb", GLA_K_WIDTH, BF16),
            ("kb", GLA_K_WIDTH, BF16), ("vb", B_WIDTH, BF16), ("la", GLA_K_WIDTH, F32), ("rb", B_WIDTH, BF16),
            ("ga", D_MODEL, BF16), ("gb", D_MODEL, BF16))
    return pl.pallas_call(
        _inproj_kernel,
        grid=(T // tm,),
        in_specs=[tok(D_MODEL), const((1, D_MODEL)), const((D_MODEL, IN_PACKED)), const((A_WIDTH, A_WIDTH)),
                  const((KV_WIDTH, KV_WIDTH)), const((1, A_WIDTH)), const((1, KV_WIDTH)),
                  const((GL_PAD, GLA_K_WIDTH)), const((1, GLA_K_WIDTH))],
        out_specs=[tok(w) for _, w, _ in outs],
        out_shape=[jax.ShapeDtypeStruct((T, w), dt) for _, w, dt in outs],
        compiler_params=_cparams(("parallel",)),
        name="inproj",
    )(x2, n1, w_packed, gq, gk, qn, kn, wgk, bgk)


def _swa_kernel(sink_ref, q_ref, kc_ref, vc_ref, kp_ref, vp_ref, o_ref, *, first_block_has_past):
    n = pl.program_id(1)
    nb, lq = q_ref.shape[0], q_ref.shape[1]
    assert lq & (lq - 1) == 0
    stack = SWA_GROUP if lq < WINDOW else 1
    rows, keys = stack * lq, WINDOW + lq
    qi = lax.broadcasted_iota(I32, (rows, keys), 0) & (lq - 1)
    ci = lax.broadcasted_iota(I32, (rows, keys), 1)
    dist = WINDOW + qi - ci
    mask = (dist >= 0) & (dist < WINDOW)
    if not first_block_has_past:
        mask = mask & ((ci >= WINDOW) | (n > 0))
    cols = lambda h0: slice(h0 // SWA_GROUP * SWA_HEAD_DIM, (h0 // SWA_GROUP + 1) * SWA_HEAD_DIM)
    qs, ks, vs = [], [], []
    for j in range(nb):
        q = q_ref[j]
        qs.append(q.astype(F32) if stack > 1 else q)
        ks.append(jnp.concatenate([kp_ref[j], kc_ref[j]], axis=0).astype(BF16))
        vs.append(jnp.concatenate([vp_ref[j], vc_ref[j]], axis=0).astype(BF16))
    groups = [(j, h0) for j in range(nb) for h0 in range(0, SWA_HEADS, stack)]
    wave = len(groups) if stack > 1 else 1
    outs = {}
    for w0 in range(0, len(groups), wave):
        scores, sinks = [], []
        for j, h0 in groups[w0:w0 + wave]:
            heads = range(h0, h0 + stack)
            qg = jnp.concatenate([qs[j][:, h * SWA_HEAD_DIM:(h + 1) * SWA_HEAD_DIM] for h in heads], axis=0)
            sinks.append(jnp.concatenate([jnp.full((lq, 1), sink_ref[h], F32) for h in heads], axis=0))
            scores.append(_dot_nt(qg.astype(BF16), ks[j][:, cols(h0)]))
        probs, denoms = [], []
        for s, sink in zip(scores, sinks):
            s = jnp.where(mask, s, NEG_INF)
            m = jnp.maximum(jnp.max(s, axis=-1, keepdims=True), sink)
            p = jnp.exp(s - m)
            denoms.append(jnp.sum(p, axis=-1, keepdims=True) + jnp.exp(sink - m))
            probs.append(p.astype(BF16))
        for (j, h0), p, denom in zip(groups[w0:w0 + wave], probs, denoms):
            o = _dot(p, vs[j][:, cols(h0)]) / denom
            outs.setdefault(j, []).extend(o[g * lq:(g + 1) * lq] for g in range(stack))
    for j in range(nb):
        o_ref[j] = jnp.concatenate(outs[j], axis=-1).astype(BF16)


def _swa(sinks, qa, ka, va, k_past, v_past, *, has_past):
    B, L, _ = qa.shape
    lq = min(L, WINDOW)
    nl = L // lq
    if has_past:
        assert nl == 1
        nb = min(B, SAMPLE_BATCH)
        past_spec = pl.BlockSpec((nb, WINDOW, KV_WIDTH), lambda b, n, s: (b, 0, 0))
        kp, vp = k_past, v_past
    else:
        assert lq == WINDOW
        nb = 1
        past_spec = pl.BlockSpec((nb, WINDOW, KV_WIDTH), lambda b, n, s: (b, jnp.maximum(n - 1, 0), 0))
        kp, vp = ka, va
    assert B % nb == 0
    cur = lambda w: pl.BlockSpec((nb, lq, w), lambda b, n, s: (b, n, 0))
    return pl.pallas_call(
        functools.partial(_swa_kernel, first_block_has_past=has_past),
        grid_spec=pltpu.PrefetchScalarGridSpec(
            num_scalar_prefetch=1,
            grid=(B // nb, nl),
            in_specs=[cur(A_WIDTH), cur(KV_WIDTH), cur(KV_WIDTH), past_spec, past_spec],
            out_specs=cur(A_WIDTH),
        ),
        out_shape=jax.ShapeDtypeStruct((B, L, A_WIDTH), BF16),
        compiler_params=_cparams(("parallel", "parallel")),
        name="swa",
    )(sinks, qa, ka, va, kp, vp)


def _gla_kernel(q_ref, k_ref, v_ref, la_ref, rb_ref, gn_ref, s0_ref, o_ref, sout_ref, s_scr, *, chunk, sub, length):
    C, SUB = chunk, sub
    S = C // SUB
    W = GLA_K_WIDTH
    nb = q_ref.shape[0]
    s_scr[...] = s0_ref[...]
    row = lax.broadcasted_iota(I32, (C, C), 0)
    col = lax.broadcasted_iota(I32, (C, C), 1)
    tri = (row >= col).astype(BF16)
    diag_mask = (row >= col) & ((row // SUB) == (col // SUB))
    eye = lax.broadcasted_iota(I32, (GLA_DK, GLA_DK), 0) == lax.broadcasted_iota(I32, (GLA_DK, GLA_DK), 1)
    krow = lax.broadcasted_iota(I32, (C, W), 0)

    def rows_of(x, r):
        return jnp.broadcast_to(x[r:r + 1, :], (SUB, W))

    ksl = lambda h: slice(h * GLA_DK, (h + 1) * GLA_DK)
    vsl = lambda h: slice(h * GLA_DV, (h + 1) * GLA_DV)

    def decays(j, r0):
        g_hi, g_lo = _split_bf16(la_ref[j, pl.ds(r0, C), :])
        return _dot(tri, g_hi) + _dot(tri, g_lo)

    def factors(j, r0, b):
        q = q_ref[j, pl.ds(r0, C), :].astype(F32)
        k = k_ref[j, pl.ds(r0, C), :].astype(F32)
        b_last = b[C - 1:C, :]
        mid = jnp.concatenate([rows_of(b, i * SUB + SUB // 2 - 1) for i in range(S)], axis=0)
        beta = jnp.concatenate([jnp.zeros((SUB, W), F32)] + [rows_of(b, i * SUB - 1) for i in range(1, S)], axis=0)
        k_off = [None]
        for i in range(1, S):
            e = jnp.exp(jnp.minimum(b[i * SUB - 1:i * SUB, :] - b, 0.0))
            k_off.append(jnp.where(krow < i * SUB, k * e, 0.0).astype(BF16))
        return dict(q_inter=(q * jnp.exp(b)).astype(BF16), k_state=(k * jnp.exp(b_last - b)).astype(BF16),
                    q_diag=(q * jnp.exp(b - mid)).astype(BF16), k_diag=(k * jnp.exp(mid - b)).astype(BF16),
                    q_off=(q * jnp.exp(b - beta)).astype(BF16), k_off=k_off, dec=jnp.exp(b_last))

    def intra(f, h):
        a_diag = _dot_nt(f["q_diag"][:, ksl(h)], f["k_diag"][:, ksl(h)])
        if S == 1:
            return jnp.where(diag_mask, a_diag, 0.0).astype(BF16)
        blocks = [jnp.zeros((SUB, C), F32)]
        for i in range(1, S):
            blocks.append(_dot_nt(f["q_off"][i * SUB:(i + 1) * SUB, ksl(h)], f["k_off"][i][:, ksl(h)]))
        return jnp.where(diag_mask, a_diag, jnp.concatenate(blocks, axis=0)).astype(BF16)

    def read_out(j, r0, f, a):
        v = v_ref[j, pl.ds(r0, C), :]
        res = []
        for h in range(GLA_HEADS):
            o_h = _dot(a[h], v[:, vsl(h)]) + _dot(f["q_inter"][:, ksl(h)], s_scr[j, h].astype(BF16))
            res.append((o_h, _dot_tn(f["k_state"][:, ksl(h)], v[:, vsl(h)])))
        return res

    def finish(j, r0, f, res):
        outs = []
        for h, (o_h, s_inc) in enumerate(res):
            dec = jnp.broadcast_to(f["dec"][:, ksl(h)], (GLA_DK, GLA_DK))
            dec_col = jnp.sum(jnp.where(eye, dec, 0.0), axis=-1, keepdims=True)
            s_scr[j, h] = dec_col * s_scr[j, h] + s_inc
            ms = jnp.mean(o_h * o_h, axis=-1, keepdims=True)
            outs.append(o_h * lax.rsqrt(ms + NORM_EPS) * gn_ref[...])
        o = jnp.concatenate(outs, axis=-1) * rb_ref[j, pl.ds(r0, C), :].astype(F32)
        o_ref[j, pl.ds(r0, C), :] = o.astype(BF16)

    def step(c):
        r0 = c * C if isinstance(c, int) else pl.multiple_of(c * C, C)
        bs = [decays(j, r0) for j in range(nb)]
        fs = [factors(j, r0, b) for j, b in enumerate(bs)]
        attn = [[intra(f, h) for h in range(GLA_HEADS)] for f in fs]
        res = [read_out(j, r0, fs[j], attn[j]) for j in range(nb)]
        for j in range(nb):
            finish(j, r0, fs[j], res[j])

    if length == C:
        step(0)
    else:
        lax.fori_loop(0, length // C, lambda c, carry: (step(c), carry)[1], 0)
    sout_ref[...] = s_scr[...]


def _gla(qb, kb, vb, la, rb, gn, s0):
    B, L, _ = qb.shape
    chunk = GLA_CHUNK if L % GLA_CHUNK == 0 else L
    sub = GLA_SUB if chunk % GLA_SUB == 0 else chunk
    nb = min(B, SAMPLE_BATCH) if L == chunk else min(B, GLA_LONG_BATCH)
    assert B % nb == 0
    seq = lambda w: pl.BlockSpec((nb, L, w), lambda b: (b, 0, 0))
    st = pl.BlockSpec((nb, GLA_HEADS, GLA_DK, GLA_DV), lambda b: (b, 0, 0, 0))
    return pl.pallas_call(
        functools.partial(_gla_kernel, chunk=chunk, sub=sub, length=L),
        grid=(B // nb,),
        in_specs=[seq(GLA_K_WIDTH), seq(GLA_K_WIDTH), seq(B_WIDTH), seq(GLA_K_WIDTH), seq(B_WIDTH),
                  pl.BlockSpec((1, GLA_DV), lambda b: (0, 0)), st],
        out_specs=[seq(B_WIDTH), st],
        out_shape=[jax.ShapeDtypeStruct((B, L, B_WIDTH), BF16),
                   jax.ShapeDtypeStruct((B, GLA_HEADS, GLA_DK, GLA_DV), F32)],
        scratch_shapes=[pltpu.VMEM((nb, GLA_HEADS, GLA_DK, GLA_DV), F32)],
        compiler_params=_cparams(("parallel",)),
        name="gla",
    )(qb, kb, vb, la, rb, gn, s0)


def _outproj_kernel(x_ref, oa_ref, ob_ref, ga_ref, gb_ref, wa_ref, wb_ref, wo_ref, n2_ref, wr_ref, br_ref, upper_ref,
                    lower_ref, h_ref, hn_ref, g_ref, lp_ref, cnt_ref):
    tm = x_ref.shape[0]
    merged = (ga_ref[...].astype(F32) * _dot(oa_ref[...], wa_ref[...])
              + gb_ref[...].astype(F32) * _dot(ob_ref[...], wb_ref[...]))
    h = x_ref[...] + _dot(merged.astype(BF16), wo_ref[...])
    h_ref[...] = h
    ms = jnp.mean(h * h, axis=-1, keepdims=True)
    hn = h * lax.rsqrt(ms + NORM_EPS) * n2_ref[...]
    hn_ref[...] = hn.astype(BF16)
    hn_hi, hn_lo = _split_bf16(hn)
    wr_hi, wr_lo = _split_bf16(wr_ref[...])
    logits = _dot(hn_hi, wr_hi) + _dot(hn_lo, wr_hi) + _dot(hn_hi, wr_lo) + br_ref[...]
    lt = logits.T[:N_EXPERTS, :]
    eid = lax.broadcasted_iota(I32, (N_EXPERTS, tm), 0)
    vals, hots = [], []
    for k in range(TOP_K):
        m = jnp.max(lt, axis=0, keepdims=True)
        idx = jnp.min(jnp.where(lt == m, eid, N_EXPERTS), axis=0, keepdims=True)
        hot = eid == idx
        lt = jnp.where(hot, -jnp.inf, lt)
        vals.append(m)
        hots.append(hot)
    ex = [jnp.exp(vk - vals[0]) for vk in vals]
    den = ex[0] + ex[1] + ex[2] + ex[3]
    for k in range(TOP_K):
        g_ref[k:k + 1, :] = ex[k] / den
    multi = (hots[0] | hots[1] | hots[2] | hots[3]).astype(BF16)
    before = _dot(multi, upper_ref[...])
    counts = jnp.sum(multi.astype(F32), axis=1, keepdims=True)
    aligned = jnp.floor((counts + (RUN_ALIGN - 1)) * (1.0 / RUN_ALIGN)) * RUN_ALIGN
    run_start = _dot(lower_ref[...], jnp.broadcast_to(aligned, (N_EXPERTS, LANE)).astype(BF16))[:, :1]
    place = before + run_start
    for k in range(TOP_K):
        lp_ref[k:k + 1, :] = jnp.sum(jnp.where(hots[k], place, 0.0), axis=0, keepdims=True).astype(I32)
    cnt_ref[...] = jnp.broadcast_to(aligned, cnt_ref.shape).astype(I32)


def _outproj(x2, oa, ob, ga, gb, wa, wb, wo, n2, wr, br, upper, lower):
    T = x2.shape[0]
    tm = TOK_TILE
    assert T % tm == 0 and upper.shape == (tm, tm)
    tok = lambda w: pl.BlockSpec((tm, w), lambda i: (i, 0))
    const = lambda shape: pl.BlockSpec(shape, lambda i: (0, 0))
    kt = pl.BlockSpec((TOP_K, tm), lambda i: (0, i))
    return pl.pallas_call(
        _outproj_kernel,
        grid=(T // tm,),
        in_specs=[tok(D_MODEL), tok(A_WIDTH), tok(B_WIDTH), tok(D_MODEL), tok(D_MODEL),
                  const((A_WIDTH, D_MODEL)), const((B_WIDTH, D_MODEL)), const((D_MODEL, D_MODEL)),
                  const((1, D_MODEL)), const((D_MODEL, LANE)), const((1, LANE)), const((tm, tm)),
                  const((N_EXPERTS, N_EXPERTS))],
        out_specs=[tok(D_MODEL), tok(D_MODEL), kt, kt, pl.BlockSpec((N_EXPERTS, LANE), lambda i: (i, 0))],
        out_shape=[jax.ShapeDtypeStruct((T, D_MODEL), F32), jax.ShapeDtypeStruct((T, D_MODEL), BF16),
                   jax.ShapeDtypeStruct((TOP_K, T), F32), jax.ShapeDtypeStruct((TOP_K, T), I32),
                   jax.ShapeDtypeStruct((T // tm * N_EXPERTS, LANE), I32)],
        compiler_params=_cparams(("parallel",)),
        name="outproj",
    )(x2, oa, ob, ga, gb, wa, wb, wo, n2, wr, br, upper, lower)


def _pack_rows(v, *, is_bf16_valued=False):
    if not is_bf16_valued:
        v = v.astype(BF16).astype(F32)
    lo = pltpu.bitcast(v[:, :ROW_WORDS], U32) >> 16
    hi = pltpu.bitcast(v[:, ROW_WORDS:], U32) & jnp.uint32(0xFFFF0000)
    return lo | hi


def _unpack_rows(w):
    lo = pltpu.bitcast(w << 16, F32).astype(BF16)
    hi = pltpu.bitcast(w & jnp.uint32(0xFFFF0000), F32).astype(BF16)
    return lo, hi


def _start_runs(runs, tile, tile_rows, global_rows, sem, *, to_global):
    cnt_ref, off_ref, dest_ref, _ = runs

    def run(e):
        j = tile * N_EXPERTS + e
        n = pl.multiple_of(cnt_ref[j], RUN_ALIGN)
        local = tile_rows.at[pl.ds(pl.multiple_of(off_ref[j], RUN_ALIGN), n), :]
        far = global_rows.at[pl.ds(pl.multiple_of(dest_ref[j], RUN_ALIGN), n), :]
        return pltpu.make_async_copy(local, far, sem) if to_global else pltpu.make_async_copy(far, local, sem)

    for e in range(N_EXPERTS):
        pl.when(cnt_ref[tile * N_EXPERTS + e] > 0)(lambda e=e: run(e).start())


def _wait_runs(runs, tile, tile_rows, global_rows, sem):
    total = pl.multiple_of(runs[3][tile], RUN_ALIGN)
    pltpu.make_async_copy(global_rows.at[pl.ds(0, total), :], tile_rows.at[pl.ds(0, total), :], sem).wait()


def _dispatch_kernel(fill_ref, cnt_ref, off_ref, dest_ref, tot_ref, lp_ref, hn_p_ref, hn_s_ref, xs_ref,
                     rows2, sems, zbuf, zsem, *, steps_p):
    i = pl.program_id(0)
    runs = (cnt_ref, off_ref, dest_ref, tot_ref)
    slot = i % 2
    rows = rows2.at[slot]

    @pl.when(i == 0)
    def _():
        zbuf[...] = jnp.zeros_like(zbuf)

        def fill_copy(j):
            return pltpu.make_async_copy(zbuf, xs_ref.at[pl.ds(pl.multiple_of(fill_ref[j], MOE_BM), MOE_BM), :], zsem)

        for j in range(fill_ref.shape[0]):
            pl.when(fill_ref[j] >= 0)(lambda j=j: fill_copy(j).start())
        for j in range(fill_ref.shape[0]):
            pl.when(fill_ref[j] >= 0)(lambda j=j: fill_copy(j).wait())

    def sort_tile(hn_ref):
        hn = hn_ref[...]
        lp = lp_ref[...]
        for c in range(SORT_ROWS // SORT_CHUNK):
            r = lax.broadcasted_iota(I32, (SORT_CHUNK, TOK_TILE), 0) + c * SORT_CHUNK
            hit = (r == lp[0:1, :]) | (r == lp[1:2, :]) | (r == lp[2:3, :]) | (r == lp[3:4, :])
            perm = jnp.where(hit, 1.0, 0.0).astype(BF16)
            rows[c * SORT_CHUNK:(c + 1) * SORT_CHUNK, :] = _pack_rows(_dot(perm, hn), is_bf16_valued=True)

    pl.when(i < steps_p)(lambda: sort_tile(hn_p_ref))
    pl.when(i >= steps_p)(lambda: sort_tile(hn_s_ref))
    _start_runs(runs, i, rows, xs_ref, sems.at[slot], to_global=True)
    pl.when(i > 0)(lambda: _wait_runs(runs, i - 1, rows2.at[1 - slot], xs_ref, sems.at[1 - slot]))
    pl.when(i == pl.num_programs(0) - 1)(lambda: _wait_runs(runs, i, rows, xs_ref, sems.at[slot]))


def _dispatch(fill, runs, lp, hn_p, hn_s, n_rows):
    tm = TOK_TILE
    steps_p, steps_s = hn_p.shape[0] // tm, hn_s.shape[0] // tm
    return pl.pallas_call(
        functools.partial(_dispatch_kernel, steps_p=steps_p),
        grid_spec=pltpu.PrefetchScalarGridSpec(
            num_scalar_prefetch=5,
            grid=(steps_p + steps_s,),
            in_specs=[pl.BlockSpec((TOP_K, tm), lambda i, *_: (0, i)),
                      pl.BlockSpec((tm, D_MODEL), lambda i, *_: (jnp.minimum(i, steps_p - 1), 0)),
                      pl.BlockSpec((tm, D_MODEL), lambda i, *_: (jnp.maximum(i - steps_p, 0), 0))],
            out_specs=pl.BlockSpec(memory_space=pl.ANY),
            scratch_shapes=[pltpu.VMEM((2, SORT_ROWS, ROW_WORDS), U32), pltpu.SemaphoreType.DMA((2,)),
                            pltpu.VMEM((MOE_BM, ROW_WORDS), U32), pltpu.SemaphoreType.DMA],
        ),
        out_shape=jax.ShapeDtypeStruct((n_rows, ROW_WORDS), U32),
        compiler_params=_cparams(("arbitrary",)),
        name="dispatch",
    )(fill, *runs, lp, hn_p, hn_s)


def _experts_kernel(row0_ref, nblk_ref, spare_ref, w1_ref, b1g_ref, b1l_ref, w2_ref, b2_ref, sel_ref, xs_ref, y_ref,
                    w1g_scr, w1l_scr, w2_scr, xbuf, ybuf, in_sems, out_sems):
    e = pl.program_id(0)
    n = nblk_ref[e]
    row0 = row0_ref[e]

    def block_rows(ref, j):
        return ref.at[pl.ds(pl.multiple_of(row0 + j * MOE_BM, MOE_BM), MOE_BM), :]

    def fetch(j, slot):
        return pltpu.make_async_copy(block_rows(xs_ref, j), xbuf.at[slot], in_sems.at[slot])

    def put(j, slot):
        return pltpu.make_async_copy(ybuf.at[slot], block_rows(y_ref, j), out_sems.at[slot])

    pl.when(n > 0)(lambda: fetch(0, 0).start())

    @pl.when(n > 0)
    def _():
        for c in range(D_FF // LANE):
            cols = _dot(w1_ref[0, :, c * 2 * LANE:(c + 1) * 2 * LANE].astype(BF16), sel_ref[...])
            w1g_scr[:, c * LANE:(c + 1) * LANE] = cols[:, :LANE].astype(BF16)
            w1l_scr[:, c * LANE:(c + 1) * LANE] = cols[:, LANE:].astype(BF16)
        w2_scr[...] = w2_ref[0].astype(BF16)

    def block(j, carry):
        slot = j % 2
        fetch(j, slot).wait()
        pl.when(j + 1 < n)(lambda: fetch(j + 1, 1 - slot).start())
        pl.when(j >= 2)(lambda: put(j - 2, slot).wait())
        x_lo, x_hi = _unpack_rows(xbuf[slot])

        def up(w_scr, b_ref):
            return _dot(x_lo, w_scr[:ROW_WORDS, :]) + _dot(x_hi, w_scr[ROW_WORDS:, :]) + b_ref[0]

        x_glu = jnp.minimum(up(w1g_scr, b1g_ref), SWIGLU_LIMIT)
        x_lin = jnp.clip(up(w1l_scr, b1l_ref), -SWIGLU_LIMIT, SWIGLU_LIMIT)
        act = x_glu * jax.nn.sigmoid(SWIGLU_ALPHA * x_glu) * (x_lin + 1.0)
        ybuf[slot] = _pack_rows(_dot(act.astype(BF16), w2_scr[...]) + b2_ref[0])
        put(j, slot).start()
        return carry

    lax.fori_loop(0, n, block, 0)
    pl.when(n >= 2)(lambda: put(n - 2, n % 2).wait())
    pl.when(n >= 1)(lambda: put(n - 1, (n - 1) % 2).wait())

    @pl.when(e == pl.num_programs(0) - 1)
    def _():
        ybuf[0] = jnp.zeros((MOE_BM, ROW_WORDS), U32)

        def spare_copy(j):
            first = pl.multiple_of(spare_ref[0] + j * MOE_BM, MOE_BM)
            return pltpu.make_async_copy(ybuf.at[0], y_ref.at[pl.ds(first, MOE_BM), :], out_sems.at[0])

        lax.fori_loop(0, spare_ref[1], lambda j, c: (spare_copy(j).start(), c)[1], 0)
        lax.fori_loop(0, spare_ref[1], lambda j, c: (spare_copy(j).wait(), c)[1], 0)


def _experts(row0, nblk, spare, xs, w1, b1g, b1l, w2, b2, sel):
    wspec = lambda r, c: pl.BlockSpec((1, r, c), lambda e, *_: (e, 0, 0))
    return pl.pallas_call(
        _experts_kernel,
        grid_spec=pltpu.PrefetchScalarGridSpec(
            num_scalar_prefetch=3,
            grid=(N_EXPERTS,),
            in_specs=[wspec(D_MODEL, 2 * D_FF), wspec(1, D_FF), wspec(1, D_FF), wspec(D_FF, D_MODEL),
                      wspec(1, D_MODEL), pl.BlockSpec((2 * LANE, 2 * LANE), lambda e, *_: (0, 0)),
                      pl.BlockSpec(memory_space=pl.ANY)],
            out_specs=pl.BlockSpec(memory_space=pl.ANY),
            scratch_shapes=[pltpu.VMEM((D_MODEL, D_FF), BF16), pltpu.VMEM((D_MODEL, D_FF), BF16),
                            pltpu.VMEM((D_FF, D_MODEL), BF16), pltpu.VMEM((2, MOE_BM, ROW_WORDS), U32),
                            pltpu.VMEM((2, MOE_BM, ROW_WORDS), U32), pltpu.SemaphoreType.DMA((2,)),
                            pltpu.SemaphoreType.DMA((2,))],
        ),
        out_shape=jax.ShapeDtypeStruct(xs.shape, U32),
        compiler_params=_cparams(("arbitrary",)),
        name="experts",
    )(row0, nblk, spare, w1, b1g, b1l, w2, b2, sel, xs)


def _combine_kernel(cnt_ref, off_ref, dest_ref, tot_ref, lp_ref, g_ref, h_ref, yb_ref, y_ref, rows2, sems, *, tile0):
    i = pl.program_id(0)
    runs = (cnt_ref, off_ref, dest_ref, tot_ref)
    slot = i % 2
    rows = rows2.at[slot]

    def fetch(step, into):
        _start_runs(runs, step + tile0, rows2.at[into], yb_ref, sems.at[into], to_global=False)

    @pl.when(i == 0)
    def _():
        rows2[...] = jnp.zeros_like(rows2)
        fetch(0, 0)

    pl.when(i + 1 < pl.num_programs(0))(lambda: fetch(i + 1, 1 - slot))
    _wait_runs(runs, i + tile0, rows, yb_ref, sems.at[slot])
    lp, g = lp_ref[...], g_ref[...]
    r = lax.broadcasted_iota(I32, (TOK_TILE, SORT_ROWS), 1)
    mix = jnp.zeros((TOK_TILE, SORT_ROWS), F32)
    for k in range(TOP_K):
        mix = jnp.where(r == lp[:, k:k + 1], g[:, k:k + 1], mix)
    mix = mix.astype(BF16)
    y_lo, y_hi = _unpack_rows(rows[...])
    y_ref[...] = h_ref[...] + jnp.concatenate([_dot(mix, y_lo), _dot(mix, y_hi)], axis=-1)


def _combine(runs, lp_t, gate_t, h, yb, tile0):
    T = h.shape[0]
    tm = TOK_TILE
    tok = lambda w: pl.BlockSpec((tm, w), lambda i, *_: (i, 0))
    return pl.pallas_call(
        functools.partial(_combine_kernel, tile0=tile0),
        grid_spec=pltpu.PrefetchScalarGridSpec(
            num_scalar_prefetch=4,
            grid=(T // tm,),
            in_specs=[tok(TOP_K), tok(TOP_K), tok(D_MODEL), pl.BlockSpec(memory_space=pl.ANY)],
            out_specs=tok(D_MODEL),
            scratch_shapes=[pltpu.VMEM((2, SORT_ROWS, ROW_WORDS), U32), pltpu.SemaphoreType.DMA((2,))],
        ),
        out_shape=jax.ShapeDtypeStruct((T, D_MODEL), F32),
        compiler_params=_cparams(("arbitrary",)),
        name="combine",
    )(*runs, lp_t, gate_t, h, yb)


def _block_ones(width, block):
    idx = np.arange(width) // block
    return jnp.asarray(idx[:, None] == idx[None, :], BF16)


def _layer(x_p, x_s, cache_k, cache_v, s_gla, norm1, w_in, q_norm, k_norm, sinks, w_gk, b_gk,
           gla_norm, w_a, w_b, w_o, norm2, w_router, b_router, w1, b1, w2, b2):
    Bp, Lp, _ = x_p.shape
    Bs, Ls, _ = x_s.shape
    Tp, Ts = Bp * Lp, Bs * Ls

    offs = np.cumsum((0, A_WIDTH, KV_WIDTH, KV_WIDTH, GLA_K_WIDTH, GLA_K_WIDTH, B_WIDTH, GLA_GATE_RANK, B_WIDTH,
                      D_MODEL, D_MODEL))
    src = dict(zip(("qa", "ka", "va", "qb", "kb", "vb", "gl", "rb", "ga", "gb"), zip(offs[:-1], offs[1:])))
    cols = []
    for name, width in _SEGS:
        a, b = src[name]
        piece = w_in[:, a:b]
        if b - a < width:
            piece = jnp.pad(piece, ((0, 0), (0, width - (b - a))))
        cols.append(piece)
    w_packed = jnp.concatenate(cols, axis=1).astype(BF16)
    wgk = jnp.pad(w_gk, ((0, GL_PAD - GLA_GATE_RANK), (0, 0))).astype(BF16)
    n1 = norm1.reshape(1, D_MODEL)
    gq, gk = _block_ones(A_WIDTH, SWA_HEAD_DIM), _block_ones(KV_WIDTH, SWA_HEAD_DIM)
    qn = jnp.tile(q_norm, SWA_HEADS).reshape(1, A_WIDTH)
    kn = jnp.tile(k_norm, SWA_KV_HEADS).reshape(1, KV_WIDTH)
    bgk = b_gk.reshape(1, GLA_K_WIDTH)
    gn = gla_norm.reshape(1, GLA_DV)
    wa, wb, wo = w_a.astype(BF16), w_b.astype(BF16), w_o.astype(BF16)
    n2 = norm2.reshape(1, D_MODEL)
    wr = jnp.pad(w_router, ((0, 0), (0, LANE - N_EXPERTS)))
    br = jnp.pad(b_router, (0, LANE - N_EXPERTS)).reshape(1, LANE)
    b1g = b1[:, 0::2].reshape(N_EXPERTS, 1, D_FF)
    b1l = b1[:, 1::2].reshape(N_EXPERTS, 1, D_FF)
    b2r = b2.reshape(N_EXPERTS, 1, D_MODEL)
    sel_np = np.zeros((2 * LANE, 2 * LANE), np.float32)
    sel_np[2 * np.arange(LANE), np.arange(LANE)] = 1.0
    sel_np[2 * np.arange(LANE) + 1, LANE + np.arange(LANE)] = 1.0
    sel = jnp.asarray(sel_np, BF16)

    def mix(x, k_past, v_past, s0):
        B, L, _ = x.shape
        T = B * L
        x2 = x.reshape(T, D_MODEL)
        qa, ka, va, qb, kb, vb, la, rb, ga, gb = _inproj(x2, n1, w_packed, gq, gk, qn, kn, wgk, bgk)
        r3 = lambda t: t.reshape(B, L, t.shape[-1])
        oa = _swa(sinks, r3(qa), r3(ka), r3(va), k_past, v_past, has_past=k_past is not None)
        ob, s_out = _gla(r3(qb), r3(kb), r3(vb), r3(la), r3(rb), gn, s0)
        h, hn, gate, lp, cnt = _outproj(x2, oa.reshape(T, A_WIDTH), ob.reshape(T, B_WIDTH), ga, gb,
                                        wa, wb, wo, n2, wr, br, upper, lower)
        return r3(ka), r3(va), s_out, h, hn, gate, lp, cnt[:, 0].reshape(T // TOK_TILE, N_EXPERTS)

    upper = jnp.asarray(np.arange(TOK_TILE)[:, None] < np.arange(TOK_TILE)[None, :], BF16)
    lower = jnp.asarray(np.arange(N_EXPERTS)[:, None] > np.arange(N_EXPERTS)[None, :], BF16)
    win = cache_k.shape[1]
    assert win == WINDOW
    ck = cache_k.reshape(Bs, win, KV_WIDTH)
    cv = cache_v.reshape(Bs, win, KV_WIDTH)
    ka_p, va_p, s_p, h_p, hn_p, g_p, lp_p, cnt_p = mix(x_p, None, None, jnp.zeros((Bp, GLA_HEADS, GLA_DK, GLA_DV), F32))
    ka_s, va_s, s_s, h_s, hn_s, g_s, lp_s, cnt_s = mix(x_s, ck, cv, s_gla)

    T = Tp + Ts
    cnt = jnp.concatenate([cnt_p, cnt_s], axis=0)
    most_rows = T * TOP_K + cnt.size * (RUN_ALIGN - 1) + N_EXPERTS * (MOE_BM - 1)
    n_rows = -(-most_rows // MOE_BM) * MOE_BM
    counts = jnp.sum(cnt, axis=0)
    padded = (counts + MOE_BM - 1) // MOE_BM * MOE_BM
    pad_end = jnp.cumsum(padded)
    start = pad_end - padded
    dest = start[None, :] + jnp.cumsum(cnt, axis=0) - cnt
    off = jnp.cumsum(cnt, axis=1) - cnt
    flat = lambda a: a.reshape(-1).astype(I32)
    runs = (flat(cnt), flat(off), flat(dest), flat(jnp.sum(cnt, axis=1)))
    tails = jnp.where(padded > 0, pad_end - MOE_BM, -1)
    most_spare = (n_rows - T * TOP_K) // MOE_BM
    spare = pad_end[-1] + jnp.arange(most_spare, dtype=I32) * MOE_BM
    fill = jnp.concatenate([tails, jnp.where(spare < n_rows, spare, -1)]).astype(I32)
    spare_blocks = jnp.stack([pad_end[-1], (n_rows - pad_end[-1]) // MOE_BM]).astype(I32)

    xs = _dispatch(fill, runs, jnp.concatenate([lp_p, lp_s], axis=1), hn_p, hn_s, n_rows)
    yb = _experts(start.astype(I32), (padded // MOE_BM).astype(I32), spare_blocks, xs, w1, b1g, b1l, w2, b2r, sel)
    y_p = _combine(runs, lp_p.T, g_p.T, h_p, yb, 0).reshape(Bp, Lp, D_MODEL)
    y_s = _combine(runs, lp_s.T, g_s.T, h_s, yb, Tp // TOK_TILE).reshape(Bs, Ls, D_MODEL)

    kv = lambda t: t.reshape(t.shape[0], t.shape[1], SWA_KV_HEADS, SWA_HEAD_DIM)
    k_p, v_p = kv(ka_p[:, -WINDOW:]), kv(va_p[:, -WINDOW:])
    k_s = jnp.concatenate([cache_k, kv(ka_s)], axis=1)[:, -win:]
    v_s = jnp.concatenate([cache_v, kv(va_s)], axis=1)[:, -win:]
    return y_p, y_s, (k_p, v_p, s_p, k_s, v_s, s_s)


def kernel(x_prompt, x_sample, cache_swa_k, cache_swa_v, state_gla, norm1, w_in, q_norm, k_norm, sinks, w_gk, b_gk,
           gla_norm, w_a, w_b, w_o, norm2, w_router, b_router, w1, b1, w2, b2):
    depth = norm1.shape[0]
    y_p, y_s = x_prompt, x_sample
    states = []
    for l in range(depth):
        y_p, y_s, st = _layer(y_p, y_s, cache_swa_k[l], cache_swa_v[l], state_gla[l], norm1[l], w_in[l], q_norm[l],
                              k_norm[l], sinks[l], w_gk[l], b_gk[l], gla_norm[l], w_a[l], w_b[l], w_o[l], norm2[l],
                              w_router[l], b_router[l], w1[l], b1[l], w2[l], b2[l])
        states.append(st)
    return (y_p, y_s) + tuple(jnp.stack([st[j] for st in states]) for j in range(6))
```

```python
import functools

import jax
import jax.numpy as jnp
import numpy as np
from jax import lax
from jax.experimental import pallas as pl
from jax.experimental.pallas import tpu as pltpu

F32 = jnp.float32
BF16 = jnp.bfloat16
I32 = jnp.int32
U32 = jnp.uint32

D_MODEL = 1024
SWA_HEADS = 8
SWA_KV_HEADS = 2
SWA_GROUP = SWA_HEADS // SWA_KV_HEADS
SWA_HEAD_DIM = 64
WINDOW = 128
ATTN_SCALE = SWA_HEAD_DIM ** -0.5
GLA_HEADS = 4
GLA_DK = 64
GLA_DV = 128
GLA_GATE_RANK = 16
GLA_GATE_NORM = 16.0
GLA_SCALE = GLA_DK ** -0.5
A_WIDTH = SWA_HEADS * SWA_HEAD_DIM
KV_WIDTH = SWA_KV_HEADS * SWA_HEAD_DIM
GLA_K_WIDTH = GLA_HEADS * GLA_DK
B_WIDTH = GLA_HEADS * GLA_DV
N_EXPERTS = 32
TOP_K = 4
D_FF = D_MODEL
SWIGLU_ALPHA = 1.702
SWIGLU_LIMIT = 7.0
NORM_EPS = 1e-5
QK_EPS = 1e-6
NEG_INF = -1e30

LANE = 128
GL_PAD = LANE
VMEM_LIMIT = 56 * 1024 * 1024

_SEGS = (("qa", A_WIDTH), ("ka", KV_WIDTH), ("va", KV_WIDTH), ("qb", GLA_K_WIDTH), ("kb", GLA_K_WIDTH),
         ("vb", B_WIDTH), ("rb", B_WIDTH), ("ga", D_MODEL), ("gb", D_MODEL), ("gl", GL_PAD))
_OFF = {}
_o = 0
for _n, _w in _SEGS:
    _OFF[_n] = (_o, _w)
    _o += _w
IN_PACKED = _o

TOK_TILE = 512
GLA_CHUNK = 64
GLA_SUB = 16
MOE_BM = 256
EXPERT_BIG = 1024
EXPERT_TAILS = (512, 256)
ROW_WORDS = D_MODEL // 2
RUN_ALIGN = 8
SORT_CHUNK = 256
SORT_ROWS = -(-(TOK_TILE * TOP_K + N_EXPERTS * (RUN_ALIGN - 1)) // SORT_CHUNK) * SORT_CHUNK
SAMPLE_BATCH = 8
GLA_LONG_BATCH = 2


def _cparams(sem):
    return pltpu.CompilerParams(dimension_semantics=sem, vmem_limit_bytes=VMEM_LIMIT)


def _split_bf16(v):
    hi = v.astype(BF16)
    lo = (v - hi.astype(F32)).astype(BF16)
    return hi, lo


def _dot(a, b):
    return jnp.dot(a, b, preferred_element_type=F32)


def _dot_nt(a, b):
    return lax.dot_general(a, b, (((1,), (1,)), ((), ())), preferred_element_type=F32)


def _dot_tn(a, b):
    return lax.dot_general(a, b, (((0,), (0,)), ((), ())), preferred_element_type=F32)


def _inproj_kernel(x_ref, n1_ref, w_ref, gq_ref, gk_ref, qn_ref, kn_ref, wgk_ref, bgk_ref,
                   qa_ref, ka_ref, va_ref, qb_ref, kb_ref, vb_ref, la_ref, rb_ref, ga_ref, gb_ref):
    x = x_ref[...]
    ms = jnp.mean(x * x, axis=-1, keepdims=True)
    xn = (x * lax.rsqrt(ms + NORM_EPS) * n1_ref[...]).astype(BF16)

    def proj(name):
        off, width = _OFF[name]
        return _dot(xn, w_ref[:, off:off + width])

    def head_norm(v, ones_ref, gain_ref):
        hi, lo = _split_bf16(v * v)
        ss = _dot(hi, ones_ref[...]) + _dot(lo, ones_ref[...])
        return v * lax.rsqrt(ss * (1.0 / SWA_HEAD_DIM) + QK_EPS) * gain_ref[...]

    def log_decay(gl):
        z = _dot(gl.astype(BF16), wgk_ref[...]) + bgk_ref[...]
        return (jnp.minimum(z, 0.0) - jnp.log1p(jnp.exp(-jnp.abs(z)))) * (1.0 / GLA_GATE_NORM)

    plan = (("qa", qa_ref, lambda u: head_norm(u, gq_ref, qn_ref) * ATTN_SCALE),
            ("ka", ka_ref, lambda u: head_norm(u, gk_ref, kn_ref)),
            ("va", va_ref, lambda u: u),
            ("qb", qb_ref, lambda u: u * GLA_SCALE),
            ("kb", kb_ref, lambda u: u),
            ("vb", vb_ref, lambda u: u),
            ("gl", la_ref, log_decay),
            ("rb", rb_ref, lambda u: u * jax.nn.sigmoid(u)),
            ("ga", ga_ref, jax.nn.sigmoid),
            ("gb", gb_ref, jax.nn.sigmoid))
    u = proj(plan[0][0])
    for idx, (_, out_ref, epilogue) in enumerate(plan):
        u_next = proj(plan[idx + 1][0]) if idx + 1 < len(plan) else None
        out_ref[...] = epilogue(u).astype(out_ref.dtype)
        u = u_next


def _inproj(x2, n1, w_packed, gq, gk, qn, kn, wgk, bgk):
    T = x2.shape[0]
    tm = min(TOK_TILE, T)
    assert T % tm == 0

    def tok(width):
        return pl.BlockSpec((tm, width), lambda i: (i, 0))

    def const(shape):
        return pl.BlockSpec(shape, lambda i: (0, 0))

    outs = (("qa", A_WIDTH, BF16), ("ka", KV_WIDTH, F32), ("va", KV_WIDTH, F32), ("qb", GLA_K_WIDTH, BF16),
            ("kb", GLA_K_WIDTH, BF16), ("vb", B_WIDTH, BF16), ("la", GLA_K_WIDTH, F32), ("rb", B_WIDTH, BF16),
            ("ga", D_MODEL, BF16), ("gb", D_MODEL, BF16))
    return pl.pallas_call(
        _inproj_kernel,
        grid=(T // tm,),
        in_specs=[tok(D_MODEL), const((1, D_MODEL)), const((D_MODEL, IN_PACKED)), const((A_WIDTH, A_WIDTH)),
                  const((KV_WIDTH, KV_WIDTH)), const((1, A_WIDTH)), const((1, KV_WIDTH)),
                  const((GL_PAD, GLA_K_WIDTH)), const((1, GLA_K_WIDTH))],
        out_specs=[tok(w) for _, w, _ in outs],
        out_shape=[jax.ShapeDtypeStruct((T, w), dt) for _, w, dt in outs],
        compiler_params=_cparams(("parallel",)),
        name="inproj",
    )(x2, n1, w_packed, gq, gk, qn, kn, wgk, bgk)


def _swa_kernel(sink_ref, q_ref, kc_ref, vc_ref, kp_ref, vp_ref, o_ref, *, first_block_has_past):
    n = pl.program_id(1)
    nb, lq = q_ref.shape[0], q_ref.shape[1]
    assert lq & (lq - 1) == 0
    stack = SWA_GROUP if lq < WINDOW else 1
    rows, keys = stack * lq, WINDOW + lq
    qi = lax.broadcasted_iota(I32, (rows, keys), 0) & (lq - 1)
    ci = lax.broadcasted_iota(I32, (rows, keys), 1)
    dist = WINDOW + qi - ci
    mask = (dist >= 0) & (dist < WINDOW)
    if not first_block_has_past:
        mask = mask & ((ci >= WINDOW) | (n > 0))
    cols = lambda h0: slice(h0 // SWA_GROUP * SWA_HEAD_DIM, (h0 // SWA_GROUP + 1) * SWA_HEAD_DIM)
    qs, ks, vs = [], [], []
    for j in range(nb):
        q = q_ref[j]
        qs.append(q.astype(F32) if stack > 1 else q)
        ks.append(jnp.concatenate([kp_ref[j], kc_ref[j]], axis=0).astype(BF16))
        vs.append(jnp.concatenate([vp_ref[j], vc_ref[j]], axis=0).astype(BF16))
    groups = [(j, h0) for j in range(nb) for h0 in range(0, SWA_HEADS, stack)]
    wave = len(groups) if stack > 1 else 1
    outs = {}
    for w0 in range(0, len(groups), wave):
        scores, sinks = [], []
        for j, h0 in groups[w0:w0 + wave]:
            heads = range(h0, h0 + stack)
            qg = jnp.concatenate([qs[j][:, h * SWA_HEAD_DIM:(h + 1) * SWA_HEAD_DIM] for h in heads], axis=0)
            sinks.append(jnp.concatenate([jnp.full((lq, 1), sink_ref[h], F32) for h in heads], axis=0))
            scores.append(_dot_nt(qg.astype(BF16), ks[j][:, cols(h0)]))
        probs, denoms = [], []
        for s, sink in zip(scores, sinks):
            s = jnp.where(mask, s, NEG_INF)
            m = jnp.maximum(jnp.max(s, axis=-1, keepdims=True), sink)
            p = jnp.exp(s - m)
            denoms.append(jnp.sum(p, axis=-1, keepdims=True) + jnp.exp(sink - m))
            probs.append(p.astype(BF16))
        for (j, h0), p, denom in zip(groups[w0:w0 + wave], probs, denoms):
            o = _dot(p, vs[j][:, cols(h0)]) / denom
            outs.setdefault(j, []).extend(o[g * lq:(g + 1) * lq] for g in range(stack))
    for j in range(nb):
        o_ref[j] = jnp.concatenate(outs[j], axis=-1).astype(BF16)


def _swa(sinks, qa, ka, va, k_past, v_past, *, has_past):
    B, L, _ = qa.shape
    lq = min(L, WINDOW)
    nl = L // lq
    if has_past:
        assert nl == 1
        nb = min(B, SAMPLE_BATCH)
        past_spec = pl.BlockSpec((nb, WINDOW, KV_WIDTH), lambda b, n, s: (b, 0, 0))
        kp, vp = k_past, v_past
    else:
        assert lq == WINDOW
        nb = 1
        past_spec = pl.BlockSpec((nb, WINDOW, KV_WIDTH), lambda b, n, s: (b, jnp.maximum(n - 1, 0), 0))
        kp, vp = ka, va
    assert B % nb == 0
    cur = lambda w: pl.BlockSpec((nb, lq, w), lambda b, n, s: (b, n, 0))
    return pl.pallas_call(
        functools.partial(_swa_kernel, first_block_has_past=has_past),
        grid_spec=pltpu.PrefetchScalarGridSpec(
            num_scalar_prefetch=1,
            grid=(B // nb, nl),
            in_specs=[cur(A_WIDTH), cur(KV_WIDTH), cur(KV_WIDTH), past_spec, past_spec],
            out_specs=cur(A_WIDTH),
        ),
        out_shape=jax.ShapeDtypeStruct((B, L, A_WIDTH), BF16),
        compiler_params=_cparams(("parallel", "parallel")),
        name="swa",
    )(sinks, qa, ka, va, kp, vp)


def _gla_kernel(q_ref, k_ref, v_ref, la_ref, rb_ref, gn_ref, s0_ref, o_ref, sout_ref, s_scr, *, chunk, sub, length):
    C, SUB = chunk, sub
    S = C // SUB
    W = GLA_K_WIDTH
    nb = q_ref.shape[0]
    s_scr[...] = s0_ref[...]
    row = lax.broadcasted_iota(I32, (C, C), 0)
    col = lax.broadcasted_iota(I32, (C, C), 1)
    tri = (row >= col).astype(BF16)
    diag_mask = (row >= col) & ((row // SUB) == (col // SUB))
    eye = lax.broadcasted_iota(I32, (GLA_DK, GLA_DK), 0) == lax.broadcasted_iota(I32, (GLA_DK, GLA_DK), 1)
    krow = lax.broadcasted_iota(I32, (C, W), 0)

    def rows_of(x, r):
        return jnp.broadcast_to(x[r:r + 1, :], (SUB, W))

    ksl = lambda h: slice(h * GLA_DK, (h + 1) * GLA_DK)
    vsl = lambda h: slice(h * GLA_DV, (h + 1) * GLA_DV)

    def decays(j, r0):
        g_hi, g_lo = _split_bf16(la_ref[j, pl.ds(r0, C), :])
        return _dot(tri, g_hi) + _dot(tri, g_lo)

    def factors(j, r0, b):
        q = q_ref[j, pl.ds(r0, C), :].astype(F32)
        k = k_ref[j, pl.ds(r0, C), :].astype(F32)
        b_last = b[C - 1:C, :]
        mid = jnp.concatenate([rows_of(b, i * SUB + SUB // 2 - 1) for i in range(S)], axis=0)
        beta = jnp.concatenate([jnp.zeros((SUB, W), F32)] + [rows_of(b, i * SUB - 1) for i in range(1, S)], axis=0)
        k_off = [None]
        for i in range(1, S):
            e = jnp.exp(jnp.minimum(b[i * SUB - 1:i * SUB, :] - b, 0.0))
            k_off.append(jnp.where(krow < i * SUB, k * e, 0.0).astype(BF16))
        return dict(q_inter=(q * jnp.exp(b)).astype(BF16), k_state=(k * jnp.exp(b_last - b)).astype(BF16),
                    q_diag=(q * jnp.exp(b - mid)).astype(BF16), k_diag=(k * jnp.exp(mid - b)).astype(BF16),
                    q_off=(q * jnp.exp(b - beta)).astype(BF16), k_off=k_off, dec=jnp.exp(b_last))

    def intra(f, h):
        a_diag = _dot_nt(f["q_diag"][:, ksl(h)], f["k_diag"][:, ksl(h)])
        if S == 1:
            return jnp.where(diag_mask, a_diag, 0.0).astype(BF16)
        blocks = [jnp.zeros((SUB, C), F32)]
        for i in range(1, S):
            blocks.append(_dot_nt(f["q_off"][i * SUB:(i + 1) * SUB, ksl(h)], f["k_off"][i][:, ksl(h)]))
        return jnp.where(diag_mask, a_diag, jnp.concatenate(blocks, axis=0)).astype(BF16)

    def read_out(j, r0, f, a):
        v = v_ref[j, pl.ds(r0, C), :]
        res = []
        for h in range(GLA_HEADS):
            o_h = _dot(a[h], v[:, vsl(h)]) + _dot(f["q_inter"][:, ksl(h)], s_scr[j, h].astype(BF16))
            res.append((o_h, _dot_tn(f["k_state"][:, ksl(h)], v[:, vsl(h)])))
        return res

    def finish(j, r0, f, res):
        outs = []
        for h, (o_h, s_inc) in enumerate(res):
            dec = jnp.broadcast_to(f["dec"][:, ksl(h)], (GLA_DK, GLA_DK))
            dec_col = jnp.sum(jnp.where(eye, dec, 0.0), axis=-1, keepdims=True)
            s_scr[j, h] = dec_col * s_scr[j, h] + s_inc
            ms = jnp.mean(o_h * o_h, axis=-1, keepdims=True)
            outs.append(o_h * lax.rsqrt(ms + NORM_EPS) * gn_ref[...])
        o = jnp.concatenate(outs, axis=-1) * rb_ref[j, pl.ds(r0, C), :].astype(F32)
        o_ref[j, pl.ds(r0, C), :] = o.astype(BF16)

    def step(c):
        r0 = c * C if isinstance(c, int) else pl.multiple_of(c * C, C)
        bs = [decays(j, r0) for j in range(nb)]
        fs = [factors(j, r0, b) for j, b in enumerate(bs)]
        attn = [[intra(f, h) for h in range(GLA_HEADS)] for f in fs]
        res = [read_out(j, r0, fs[j], attn[j]) for j in range(nb)]
        for j in range(nb):
            finish(j, r0, fs[j], res[j])

    if length == C:
        step(0)
    else:
        lax.fori_loop(0, length // C, lambda c, carry: (step(c), carry)[1], 0)
    sout_ref[...] = s_scr[...]


def _gla(qb, kb, vb, la, rb, gn, s0):
    B, L, _ = qb.shape
    chunk = GLA_CHUNK if L % GLA_CHUNK == 0 else L
    sub = GLA_SUB if chunk % GLA_SUB == 0 else chunk
    nb = min(B, SAMPLE_BATCH) if L == chunk else min(B, GLA_LONG_BATCH)
    assert B % nb == 0
    seq = lambda w: pl.BlockSpec((nb, L, w), lambda b: (b, 0, 0))
    st = pl.BlockSpec((nb, GLA_HEADS, GLA_DK, GLA_DV), lambda b: (b, 0, 0, 0))
    return pl.pallas_call(
        functools.partial(_gla_kernel, chunk=chunk, sub=sub, length=L),
        grid=(B // nb,),
        in_specs=[seq(GLA_K_WIDTH), seq(GLA_K_WIDTH), seq(B_WIDTH), seq(GLA_K_WIDTH), seq(B_WIDTH),
                  pl.BlockSpec((1, GLA_DV), lambda b: (0, 0)), st],
        out_specs=[seq(B_WIDTH), st],
        out_shape=[jax.ShapeDtypeStruct((B, L, B_WIDTH), BF16),
                   jax.ShapeDtypeStruct((B, GLA_HEADS, GLA_DK, GLA_DV), F32)],
        scratch_shapes=[pltpu.VMEM((nb, GLA_HEADS, GLA_DK, GLA_DV), F32)],
        compiler_params=_cparams(("parallel",)),
        name="gla",
    )(qb, kb, vb, la, rb, gn, s0)


def _outproj_kernel(x_ref, oa_ref, ob_ref, ga_ref, gb_ref, wa_ref, wb_ref, wo_ref, n2_ref, wr_ref, br_ref, upper_ref,
                    lower_ref, h_ref, hn_ref, g_ref, lp_ref, cnt_ref):
    tm = x_ref.shape[0]
    merged = (ga_ref[...].astype(F32) * _dot(oa_ref[...], wa_ref[...])
              + gb_ref[...].astype(F32) * _dot(ob_ref[...], wb_ref[...]))
    h = x_ref[...] + _dot(merged.astype(BF16), wo_ref[...])
    h_ref[...] = h
    ms = jnp.mean(h * h, axis=-1, keepdims=True)
    hn = h * lax.rsqrt(ms + NORM_EPS) * n2_ref[...]
    hn_ref[...] = hn.astype(BF16)
    hn_hi, hn_lo = _split_bf16(hn)
    wr_hi, wr_lo = _split_bf16(wr_ref[...])
    logits = _dot(hn_hi, wr_hi) + _dot(hn_lo, wr_hi) + _dot(hn_hi, wr_lo) + br_ref[...]
    lt = logits.T[:N_EXPERTS, :]
    eid = lax.broadcasted_iota(I32, (N_EXPERTS, tm), 0)
    vals, hots = [], []
    for k in range(TOP_K):
        m = jnp.max(lt, axis=0, keepdims=True)
        idx = jnp.min(jnp.where(lt == m, eid, N_EXPERTS), axis=0, keepdims=True)
        hot = eid == idx
        lt = jnp.where(hot, -jnp.inf, lt)
        vals.append(m)
        hots.append(hot)
    ex = [jnp.exp(vk - vals[0]) for vk in vals]
    den = ex[0] + ex[1] + ex[2] + ex[3]
    for k in range(TOP_K):
        g_ref[k:k + 1, :] = ex[k] / den
    multi = (hots[0] | hots[1] | hots[2] | hots[3]).astype(BF16)
    before = _dot(multi, upper_ref[...])
    counts = jnp.sum(multi.astype(F32), axis=1, keepdims=True)
    aligned = jnp.floor((counts + (RUN_ALIGN - 1)) * (1.0 / RUN_ALIGN)) * RUN_ALIGN
    run_start = _dot(lower_ref[...], jnp.broadcast_to(aligned, (N_EXPERTS, LANE)).astype(BF16))[:, :1]
    place = before + run_start
    for k in range(TOP_K):
        lp_ref[k:k + 1, :] = jnp.sum(jnp.where(hots[k], place, 0.0), axis=0, keepdims=True).astype(I32)
    cnt_ref[...] = jnp.broadcast_to(aligned, cnt_ref.shape).astype(I32)


def _outproj(x2, oa, ob, ga, gb, wa, wb, wo, n2, wr, br, upper, lower):
    T = x2.shape[0]
    tm = TOK_TILE
    assert T % tm == 0 and upper.shape == (tm, tm)
    tok = lambda w: pl.BlockSpec((tm, w), lambda i: (i, 0))
    const = lambda shape: pl.BlockSpec(shape, lambda i: (0, 0))
    kt = pl.BlockSpec((TOP_K, tm), lambda i: (0, i))
    return pl.pallas_call(
        _outproj_kernel,
        grid=(T // tm,),
        in_specs=[tok(D_MODEL), tok(A_WIDTH), tok(B_WIDTH), tok(D_MODEL), tok(D_MODEL),
                  const((A_WIDTH, D_MODEL)), const((B_WIDTH, D_MODEL)), const((D_MODEL, D_MODEL)),
                  const((1, D_MODEL)), const((D_MODEL, LANE)), const((1, LANE)), const((tm, tm)),
                  const((N_EXPERTS, N_EXPERTS))],
        out_specs=[tok(D_MODEL), tok(D_MODEL), kt, kt, pl.BlockSpec((N_EXPERTS, LANE), lambda i: (i, 0))],
        out_shape=[jax.ShapeDtypeStruct((T, D_MODEL), F32), jax.ShapeDtypeStruct((T, D_MODEL), BF16),
                   jax.ShapeDtypeStruct((TOP_K, T), F32), jax.ShapeDtypeStruct((TOP_K, T), I32),
                   jax.ShapeDtypeStruct((T // tm * N_EXPERTS, LANE), I32)],
        compiler_params=_cparams(("parallel",)),
        name="outproj",
    )(x2, oa, ob, ga, gb, wa, wb, wo, n2, wr, br, upper, lower)


def _pack_rows(v, *, is_bf16_valued=False):
    if not is_bf16_valued:
        v = v.astype(BF16).astype(F32)
    lo = pltpu.bitcast(v[:, :ROW_WORDS], U32) >> 16
    hi = pltpu.bitcast(v[:, ROW_WORDS:], U32) & jnp.uint32(0xFFFF0000)
    return lo | hi


def _unpack_rows(w):
    lo = pltpu.bitcast(w << 16, F32).astype(BF16)
    hi = pltpu.bitcast(w & jnp.uint32(0xFFFF0000), F32).astype(BF16)
    return lo, hi


def _start_runs(runs, tile, tile_rows, global_rows, sem, *, to_global):
    cnt_ref, off_ref, dest_ref, _ = runs

    def run(e):
        j = tile * N_EXPERTS + e
        n = pl.multiple_of(cnt_ref[j], RUN_ALIGN)
        local = tile_rows.at[pl.ds(pl.multiple_of(off_ref[j], RUN_ALIGN), n), :]
        far = global_rows.at[pl.ds(pl.multiple_of(dest_ref[j], RUN_ALIGN), n), :]
        return pltpu.make_async_copy(local, far, sem) if to_global else pltpu.make_async_copy(far, local, sem)

    for e in range(N_EXPERTS):
        pl.when(cnt_ref[tile * N_EXPERTS + e] > 0)(lambda e=e: run(e).start())


def _wait_runs(runs, tile, tile_rows, global_rows, sem):
    total = pl.multiple_of(runs[3][tile], RUN_ALIGN)
    pltpu.make_async_copy(global_rows.at[pl.ds(0, total), :], tile_rows.at[pl.ds(0, total), :], sem).wait()


def _dispatch_kernel(fill_ref, cnt_ref, off_ref, dest_ref, tot_ref, lp_ref, hn_p_ref, hn_s_ref, xs_ref,
                     rows2, sems, zbuf, zsem, *, steps_p):
    i = pl.program_id(0)
    runs = (cnt_ref, off_ref, dest_ref, tot_ref)
    slot = i % 2
    rows = rows2.at[slot]

    @pl.when(i == 0)
    def _():
        zbuf[...] = jnp.zeros_like(zbuf)

        def fill_copy(j):
            return pltpu.make_async_copy(zbuf, xs_ref.at[pl.ds(pl.multiple_of(fill_ref[j], MOE_BM), MOE_BM), :], zsem)

        for j in range(fill_ref.shape[0]):
            pl.when(fill_ref[j] >= 0)(lambda j=j: fill_copy(j).start())
        for j in range(fill_ref.shape[0]):
            pl.when(fill_ref[j] >= 0)(lambda j=j: fill_copy(j).wait())

    def sort_tile(hn_ref):
        hn = hn_ref[...]
        lp = lp_ref[...]
        for c in range(SORT_ROWS // SORT_CHUNK):
            r = lax.broadcasted_iota(I32, (SORT_CHUNK, TOK_TILE), 0) + c * SORT_CHUNK
            hit = (r == lp[0:1, :]) | (r == lp[1:2, :]) | (r == lp[2:3, :]) | (r == lp[3:4, :])
            perm = jnp.where(hit, 1.0, 0.0).astype(BF16)
            rows[c * SORT_CHUNK:(c + 1) * SORT_CHUNK, :] = _pack_rows(_dot(perm, hn), is_bf16_valued=True)

    pl.when(i < steps_p)(lambda: sort_tile(hn_p_ref))
    pl.when(i >= steps_p)(lambda: sort_tile(hn_s_ref))
    _start_runs(runs, i, rows, xs_ref, sems.at[slot], to_global=True)
    pl.when(i > 0)(lambda: _wait_runs(runs, i - 1, rows2.at[1 - slot], xs_ref, sems.at[1 - slot]))
    pl.when(i == pl.num_programs(0) - 1)(lambda: _wait_runs(runs, i, rows, xs_ref, sems.at[slot]))


def _dispatch(fill, runs, lp, hn_p, hn_s, n_rows):
    tm = TOK_TILE
    steps_p, steps_s = hn_p.shape[0] // tm, hn_s.shape[0] // tm
    return pl.pallas_call(
        functools.partial(_dispatch_kernel, steps_p=steps_p),
        grid_spec=pltpu.PrefetchScalarGridSpec(
            num_scalar_prefetch=5,
            grid=(steps_p + steps_s,),
            in_specs=[pl.BlockSpec((TOP_K, tm), lambda i, *_: (0, i)),
                      pl.BlockSpec((tm, D_MODEL), lambda i, *_: (jnp.minimum(i, steps_p - 1), 0)),
                      pl.BlockSpec((tm, D_MODEL), lambda i, *_: (jnp.maximum(i - steps_p, 0), 0))],
            out_specs=pl.BlockSpec(memory_space=pl.ANY),
            scratch_shapes=[pltpu.VMEM((2, SORT_ROWS, ROW_WORDS), U32), pltpu.SemaphoreType.DMA((2,)),
                            pltpu.VMEM((MOE_BM, ROW_WORDS), U32), pltpu.SemaphoreType.DMA],
        ),
        out_shape=jax.ShapeDtypeStruct((n_rows, ROW_WORDS), U32),
        compiler_params=_cparams(("arbitrary",)),
        name="dispatch",
    )(fill, *runs, lp, hn_p, hn_s)


def _experts_kernel(row0_ref, nblk_ref, spare_ref, w1_ref, b1g_ref, b1l_ref, w2_ref, b2_ref, sel_ref, xs_ref, y_ref,
                    w1g_scr, w1l_scr, w2_scr, xbuf, ybuf, xtail, ytail, in_sems, out_sems, tail_sems):
    e = pl.program_id(0)
    n = nblk_ref[e]
    row0 = row0_ref[e]
    per_big = EXPERT_BIG // MOE_BM
    nbig = n // per_big
    left = n - nbig * per_big

    def rows_at(ref, first, size):
        return ref.at[pl.ds(pl.multiple_of(first, MOE_BM), size), :]

    def fetch(j, slot):
        return pltpu.make_async_copy(rows_at(xs_ref, row0 + j * EXPERT_BIG, EXPERT_BIG), xbuf.at[slot],
                                     in_sems.at[slot])

    def put(j, slot):
        return pltpu.make_async_copy(ybuf.at[slot], rows_at(y_ref, row0 + j * EXPERT_BIG, EXPERT_BIG),
                                     out_sems.at[slot])

    tails = []
    first, at = row0 + nbig * EXPERT_BIG, 0
    for t, size in enumerate(EXPERT_TAILS):
        present = (left & (size // MOE_BM)) != 0
        tails.append((present, first, at, size, t))
        first = first + jnp.where(present, size, 0)
        at += size

    def tail_fetch(first, at, size, t):
        return pltpu.make_async_copy(rows_at(xs_ref, first, size), xtail.at[pl.ds(at, size), :], tail_sems.at[0, t])

    def tail_put(first, at, size, t):
        return pltpu.make_async_copy(ytail.at[pl.ds(at, size), :], rows_at(y_ref, first, size), tail_sems.at[1, t])

    pl.when(nbig > 0)(lambda: fetch(0, 0).start())
    for present, *where in tails:
        pl.when(present)(lambda where=where: tail_fetch(*where).start())

    @pl.when(n > 0)
    def _():
        for c in range(D_FF // LANE):
            cols = _dot(w1_ref[0, :, c * 2 * LANE:(c + 1) * 2 * LANE].astype(BF16), sel_ref[...])
            w1g_scr[:, c * LANE:(c + 1) * LANE] = cols[:, :LANE].astype(BF16)
            w1l_scr[:, c * LANE:(c + 1) * LANE] = cols[:, LANE:].astype(BF16)
        w2_scr[...] = w2_ref[0].astype(BF16)

    def mlp(words):
        x_lo, x_hi = _unpack_rows(words)

        def up(w_scr, b_ref):
            return _dot(x_lo, w_scr[:ROW_WORDS, :]) + _dot(x_hi, w_scr[ROW_WORDS:, :]) + b_ref[0]

        x_glu = jnp.minimum(up(w1g_scr, b1g_ref), SWIGLU_LIMIT)
        x_lin = jnp.clip(up(w1l_scr, b1l_ref), -SWIGLU_LIMIT, SWIGLU_LIMIT)
        act = x_glu * jax.nn.sigmoid(SWIGLU_ALPHA * x_glu) * (x_lin + 1.0)
        return _pack_rows(_dot(act.astype(BF16), w2_scr[...]) + b2_ref[0])

    def big_block(j, carry):
        slot = j % 2
        fetch(j, slot).wait()
        pl.when(j + 1 < nbig)(lambda: fetch(j + 1, 1 - slot).start())
        pl.when(j >= 2)(lambda: put(j - 2, slot).wait())
        ybuf[slot] = mlp(xbuf[slot])
        put(j, slot).start()
        return carry

    lax.fori_loop(0, nbig, big_block, 0)

    for present, *where in tails:
        @pl.when(present)
        def _(where=where):
            _, at, size, _ = where
            tail_fetch(*where).wait()
            ytail[at:at + size, :] = mlp(xtail[at:at + size, :])
            tail_put(*where).start()

    pl.when(nbig >= 2)(lambda: put(nbig - 2, nbig % 2).wait())
    pl.when(nbig >= 1)(lambda: put(nbig - 1, (nbig - 1) % 2).wait())
    for present, *where in tails:
        pl.when(present)(lambda where=where: tail_put(*where).wait())

    @pl.when(e == pl.num_programs(0) - 1)
    def _():
        ytail[0:MOE_BM, :] = jnp.zeros((MOE_BM, ROW_WORDS), U32)

        def spare_copy(j):
            return pltpu.make_async_copy(ytail.at[pl.ds(0, MOE_BM), :],
                                         rows_at(y_ref, spare_ref[0] + j * MOE_BM, MOE_BM), tail_sems.at[1, 0])

        lax.fori_loop(0, spare_ref[1], lambda j, c: (spare_copy(j).start(), c)[1], 0)
        lax.fori_loop(0, spare_ref[1], lambda j, c: (spare_copy(j).wait(), c)[1], 0)


def _experts(row0, nblk, spare, xs, w1, b1g, b1l, w2, b2, sel):
    wspec = lambda r, c: pl.BlockSpec((1, r, c), lambda e, *_: (e, 0, 0))
    return pl.pallas_call(
        _experts_kernel,
        grid_spec=pltpu.PrefetchScalarGridSpec(
            num_scalar_prefetch=3,
            grid=(N_EXPERTS,),
            in_specs=[wspec(D_MODEL, 2 * D_FF), wspec(1, D_FF), wspec(1, D_FF), wspec(D_FF, D_MODEL),
                      wspec(1, D_MODEL), pl.BlockSpec((2 * LANE, 2 * LANE), lambda e, *_: (0, 0)),
                      pl.BlockSpec(memory_space=pl.ANY)],
            out_specs=pl.BlockSpec(memory_space=pl.ANY),
            scratch_shapes=[pltpu.VMEM((D_MODEL, D_FF), BF16), pltpu.VMEM((D_MODEL, D_FF), BF16),
                            pltpu.VMEM((D_FF, D_MODEL), BF16), pltpu.VMEM((2, EXPERT_BIG, ROW_WORDS), U32),
                            pltpu.VMEM((2, EXPERT_BIG, ROW_WORDS), U32),
                            pltpu.VMEM((sum(EXPERT_TAILS), ROW_WORDS), U32),
                            pltpu.VMEM((sum(EXPERT_TAILS), ROW_WORDS), U32), pltpu.SemaphoreType.DMA((2,)),
                            pltpu.SemaphoreType.DMA((2,)), pltpu.SemaphoreType.DMA((2, len(EXPERT_TAILS)))],
        ),
        out_shape=jax.ShapeDtypeStruct(xs.shape, U32),
        compiler_params=_cparams(("arbitrary",)),
        name="experts",
    )(row0, nblk, spare, w1, b1g, b1l, w2, b2, sel, xs)


def _combine_kernel(cnt_ref, off_ref, dest_ref, tot_ref, lp_ref, g_ref, h_ref, yb_ref, y_ref, rows2, sems, *, tile0):
    i = pl.program_id(0)
    runs = (cnt_ref, off_ref, dest_ref, tot_ref)
    slot = i % 2
    rows = rows2.at[slot]

    def fetch(step, into):
        _start_runs(runs, step + tile0, rows2.at[into], yb_ref, sems.at[into], to_global=False)

    @pl.when(i == 0)
    def _():
        rows2[...] = jnp.zeros_like(rows2)
        fetch(0, 0)

    pl.when(i + 1 < pl.num_programs(0))(lambda: fetch(i + 1, 1 - slot))
    _wait_runs(runs, i + tile0, rows, yb_ref, sems.at[slot])
    lp, g = lp_ref[...], g_ref[...]
    r = lax.broadcasted_iota(I32, (TOK_TILE, SORT_ROWS), 1)
    mix = jnp.zeros((TOK_TILE, SORT_ROWS), F32)
    for k in range(TOP_K):
        mix = jnp.where(r == lp[:, k:k + 1], g[:, k:k + 1], mix)
    mix = mix.astype(BF16)
    y_lo, y_hi = _unpack_rows(rows[...])
    y_ref[...] = h_ref[...] + jnp.concatenate([_dot(mix, y_lo), _dot(mix, y_hi)], axis=-1)


def _combine(runs, lp_t, gate_t, h, yb, tile0):
    T = h.shape[0]
    tm = TOK_TILE
    tok = lambda w: pl.BlockSpec((tm, w), lambda i, *_: (i, 0))
    return pl.pallas_call(
        functools.partial(_combine_kernel, tile0=tile0),
        grid_spec=pltpu.PrefetchScalarGridSpec(
            num_scalar_prefetch=4,
            grid=(T // tm,),
            in_specs=[tok(TOP_K), tok(TOP_K), tok(D_MODEL), pl.BlockSpec(memory_space=pl.ANY)],
            out_specs=tok(D_MODEL),
            scratch_shapes=[pltpu.VMEM((2, SORT_ROWS, ROW_WORDS), U32), pltpu.SemaphoreType.DMA((2,))],
        ),
        out_shape=jax.ShapeDtypeStruct((T, D_MODEL), F32),
        compiler_params=_cparams(("arbitrary",)),
        name="combine",
    )(*runs, lp_t, gate_t, h, yb)


def _block_ones(width, block):
    idx = np.arange(width) // block
    return jnp.asarray(idx[:, None] == idx[None, :], BF16)


def _layer(x_p, x_s, cache_k, cache_v, s_gla, norm1, w_in, q_norm, k_norm, sinks, w_gk, b_gk,
           gla_norm, w_a, w_b, w_o, norm2, w_router, b_router, w1, b1, w2, b2):
    Bp, Lp, _ = x_p.shape
    Bs, Ls, _ = x_s.shape
    Tp, Ts = Bp * Lp, Bs * Ls

    offs = np.cumsum((0, A_WIDTH, KV_WIDTH, KV_WIDTH, GLA_K_WIDTH, GLA_K_WIDTH, B_WIDTH, GLA_GATE_RANK, B_WIDTH,
                      D_MODEL, D_MODEL))
    src = dict(zip(("qa", "ka", "va", "qb", "kb", "vb", "gl", "rb", "ga", "gb"), zip(offs[:-1], offs[1:])))
    cols = []
    for name, width in _SEGS:
        a, b = src[name]
        piece = w_in[:, a:b]
        if b - a < width:
            piece = jnp.pad(piece, ((0, 0), (0, width - (b - a))))
        cols.append(piece)
    w_packed = jnp.concatenate(cols, axis=1).astype(BF16)
    wgk = jnp.pad(w_gk, ((0, GL_PAD - GLA_GATE_RANK), (0, 0))).astype(BF16)
    n1 = norm1.reshape(1, D_MODEL)
    gq, gk = _block_ones(A_WIDTH, SWA_HEAD_DIM), _block_ones(KV_WIDTH, SWA_HEAD_DIM)
    qn = jnp.tile(q_norm, SWA_HEADS).reshape(1, A_WIDTH)
    kn = jnp.tile(k_norm, SWA_KV_HEADS).reshape(1, KV_WIDTH)
    bgk = b_gk.reshape(1, GLA_K_WIDTH)
    gn = gla_norm.reshape(1, GLA_DV)
    wa, wb, wo = w_a.astype(BF16), w_b.astype(BF16), w_o.astype(BF16)
    n2 = norm2.reshape(1, D_MODEL)
    wr = jnp.pad(w_router, ((0, 0), (0, LANE - N_EXPERTS)))
    br = jnp.pad(b_router, (0, LANE - N_EXPERTS)).reshape(1, LANE)
    b1g = b1[:, 0::2].reshape(N_EXPERTS, 1, D_FF)
    b1l = b1[:, 1::2].reshape(N_EXPERTS, 1, D_FF)
    b2r = b2.reshape(N_EXPERTS, 1, D_MODEL)
    sel_np = np.zeros((2 * LANE, 2 * LANE), np.float32)
    sel_np[2 * np.arange(LANE), np.arange(LANE)] = 1.0
    sel_np[2 * np.arange(LANE) + 1, LANE + np.arange(LANE)] = 1.0
    sel = jnp.asarray(sel_np, BF16)

    def mix(x, k_past, v_past, s0):
        B, L, _ = x.shape
        T = B * L
        x2 = x.reshape(T, D_MODEL)
        qa, ka, va, qb, kb, vb, la, rb, ga, gb = _inproj(x2, n1, w_packed, gq, gk, qn, kn, wgk, bgk)
        r3 = lambda t: t.reshape(B, L, t.shape[-1])
        oa = _swa(sinks, r3(qa), r3(ka), r3(va), k_past, v_past, has_past=k_past is not None)
        ob, s_out = _gla(r3(qb), r3(kb), r3(vb), r3(la), r3(rb), gn, s0)
        h, hn, gate, lp, cnt = _outproj(x2, oa.reshape(T, A_WIDTH), ob.reshape(T, B_WIDTH), ga, gb,
                                        wa, wb, wo, n2, wr, br, upper, lower)
        return r3(ka), r3(va), s_out, h, hn, gate, lp, cnt[:, 0].reshape(T // TOK_TILE, N_EXPERTS)

    upper = jnp.asarray(np.arange(TOK_TILE)[:, None] < np.arange(TOK_TILE)[None, :], BF16)
    lower = jnp.asarray(np.arange(N_EXPERTS)[:, None] > np.arange(N_EXPERTS)[None, :], BF16)
    win = cache_k.shape[1]
    assert win == WINDOW
    ck = cache_k.reshape(Bs, win, KV_WIDTH)
    cv = cache_v.reshape(Bs, win, KV_WIDTH)
    ka_p, va_p, s_p, h_p, hn_p, g_p, lp_p, cnt_p = mix(x_p, None, None, jnp.zeros((Bp, GLA_HEADS, GLA_DK, GLA_DV), F32))
    ka_s, va_s, s_s, h_s, hn_s, g_s, lp_s, cnt_s = mix(x_s, ck, cv, s_gla)

    T = Tp + Ts
    cnt = jnp.concatenate([cnt_p, cnt_s], axis=0)
    most_rows = T * TOP_K + cnt.size * (RUN_ALIGN - 1) + N_EXPERTS * (MOE_BM - 1)
    n_rows = -(-most_rows // MOE_BM) * MOE_BM
    counts = jnp.sum(cnt, axis=0)
    padded = (counts + MOE_BM - 1) // MOE_BM * MOE_BM
    pad_end = jnp.cumsum(padded)
    start = pad_end - padded
    dest = start[None, :] + jnp.cumsum(cnt, axis=0) - cnt
    off = jnp.cumsum(cnt, axis=1) - cnt
    flat = lambda a: a.reshape(-1).astype(I32)
    runs = (flat(cnt), flat(off), flat(dest), flat(jnp.sum(cnt, axis=1)))
    tails = jnp.where(padded > 0, pad_end - MOE_BM, -1)
    most_spare = (n_rows - T * TOP_K) // MOE_BM
    spare = pad_end[-1] + jnp.arange(most_spare, dtype=I32) * MOE_BM
    fill = jnp.concatenate([tails, jnp.where(spare < n_rows, spare, -1)]).astype(I32)
    spare_blocks = jnp.stack([pad_end[-1], (n_rows - pad_end[-1]) // MOE_BM]).astype(I32)

    xs = _dispatch(fill, runs, jnp.concatenate([lp_p, lp_s], axis=1), hn_p, hn_s, n_rows)
    yb = _experts(start.astype(I32), (padded // MOE_BM).astype(I32), spare_blocks, xs, w1, b1g, b1l, w2, b2r, sel)
    y_p = _combine(runs, lp_p.T, g_p.T, h_p, yb, 0).reshape(Bp, Lp, D_MODEL)
    y_s = _combine(runs, lp_s.T, g_s.T, h_s, yb, Tp // TOK_TILE).reshape(Bs, Ls, D_MODEL)

    kv = lambda t: t.reshape(t.shape[0], t.shape[1], SWA_KV_HEADS, SWA_HEAD_DIM)
    k_p, v_p = kv(ka_p[:, -WINDOW:]), kv(va_p[:, -WINDOW:])
    k_s = jnp.concatenate([cache_k, kv(ka_s)], axis=1)[:, -win:]
    v_s = jnp.concatenate([cache_v, kv(va_s)], axis=1)[:, -win:]
    return y_p, y_s, (k_p, v_p, s_p, k_s, v_s, s_s)


def kernel(x_prompt, x_sample, cache_swa_k, cache_swa_v, state_gla, norm1, w_in, q_norm, k_norm, sinks, w_gk, b_gk,
           gla_norm, w_a, w_b, w_o, norm2, w_router, b_router, w1, b1, w2, b2):
    depth = norm1.shape[0]
    y_p, y_s = x_prompt, x_sample
    states = []
    for l in range(depth):
        y_p, y_s, st = _layer(y_p, y_s, cache_swa_k[l], cache_swa_v[l], state_gla[l], norm1[l], w_in[l], q_norm[l],
                              k_norm[l], sinks[l], w_gk[l], b_gk[l], gla_norm[l], w_a[l], w_b[l], w_o[l], norm2[l],
                              w_router[l], b_router[l], w1[l], b1[l], w2[l], b2[l])
        states.append(st)
    return (y_p, y_s) + tuple(jnp.stack([st[j] for st in states]) for j in range(6))
```

```python
import functools

import jax
import jax.numpy as jnp
import numpy as np
from jax import lax
from jax.experimental import pallas as pl
from jax.experimental.pallas import tpu as pltpu

F32 = jnp.float32
BF16 = jnp.bfloat16
I32 = jnp.int32
U32 = jnp.uint32

D_MODEL = 1024
SWA_HEADS = 8
SWA_KV_HEADS = 2
SWA_GROUP = SWA_HEADS // SWA_KV_HEADS
SWA_HEAD_DIM = 64
WINDOW = 128
ATTN_SCALE = SWA_HEAD_DIM ** -0.5
GLA_HEADS = 4
GLA_DK = 64
GLA_DV = 128
GLA_GATE_RANK = 16
GLA_GATE_NORM = 16.0
GLA_SCALE = GLA_DK ** -0.5
A_WIDTH = SWA_HEADS * SWA_HEAD_DIM
KV_WIDTH = SWA_KV_HEADS * SWA_HEAD_DIM
GLA_K_WIDTH = GLA_HEADS * GLA_DK
B_WIDTH = GLA_HEADS * GLA_DV
N_EXPERTS = 32
TOP_K = 4
D_FF = D_MODEL
SWIGLU_ALPHA = 1.702
SWIGLU_LIMIT = 7.0
NORM_EPS = 1e-5
QK_EPS = 1e-6
NEG_INF = -1e30

LANE = 128
GL_PAD = LANE
VMEM_LIMIT = 56 * 1024 * 1024

_SEGS = (("qa", A_WIDTH), ("ka", KV_WIDTH), ("va", KV_WIDTH), ("qb", GLA_K_WIDTH), ("kb", GLA_K_WIDTH),
         ("vb", B_WIDTH), ("rb", B_WIDTH), ("ga", D_MODEL), ("gb", D_MODEL), ("gl", GL_PAD))
_OFF = {}
_o = 0
for _n, _w in _SEGS:
    _OFF[_n] = (_o, _w)
    _o += _w
IN_PACKED = _o

TOK_TILE = 512
GLA_CHUNK = 64
GLA_SUB = 16
MOE_BM = 256
EXPERT_BIG = 1024
EXPERT_TAILS = (512, 256)
ROW_DMA_PRIORITY = 1
ROW_WORDS = D_MODEL // 2
RUN_ALIGN = 8
SORT_CHUNK = 256
SORT_ROWS = -(-(TOK_TILE * TOP_K + N_EXPERTS * (RUN_ALIGN - 1)) // SORT_CHUNK) * SORT_CHUNK
SAMPLE_BATCH = 8
GLA_LONG_BATCH = 2


def _cparams(sem):
    return pltpu.CompilerParams(dimension_semantics=sem, vmem_limit_bytes=VMEM_LIMIT)


def _split_bf16(v):
    hi = v.astype(BF16)
    lo = (v - hi.astype(F32)).astype(BF16)
    return hi, lo


def _dot(a, b):
    return jnp.dot(a, b, preferred_element_type=F32)


def _dot_nt(a, b):
    return lax.dot_general(a, b, (((1,), (1,)), ((), ())), preferred_element_type=F32)


def _dot_tn(a, b):
    return lax.dot_general(a, b, (((0,), (0,)), ((), ())), preferred_element_type=F32)


def _inproj_kernel(x_ref, n1_ref, w_ref, gq_ref, gk_ref, qn_ref, kn_ref, wgk_ref, bgk_ref,
                   qa_ref, ka_ref, va_ref, qb_ref, kb_ref, vb_ref, la_ref, rb_ref, ga_ref, gb_ref):
    x = x_ref[...]
    ms = jnp.mean(x * x, axis=-1, keepdims=True)
    xn = (x * lax.rsqrt(ms + NORM_EPS) * n1_ref[...]).astype(BF16)

    def proj(name):
        off, width = _OFF[name]
        return _dot(xn, w_ref[:, off:off + width])

    def head_norm(v, ones_ref, gain_ref):
        hi, lo = _split_bf16(v * v)
        ss = _dot(hi, ones_ref[...]) + _dot(lo, ones_ref[...])
        return v * lax.rsqrt(ss * (1.0 / SWA_HEAD_DIM) + QK_EPS) * gain_ref[...]

    def log_decay(gl):
        z = _dot(gl.astype(BF16), wgk_ref[...]) + bgk_ref[...]
        return (jnp.minimum(z, 0.0) - jnp.log1p(jnp.exp(-jnp.abs(z)))) * (1.0 / GLA_GATE_NORM)

    plan = (("qa", qa_ref, lambda u: head_norm(u, gq_ref, qn_ref) * ATTN_SCALE),
            ("ka", ka_ref, lambda u: head_norm(u, gk_ref, kn_ref)),
            ("va", va_ref, lambda u: u),
            ("qb", qb_ref, lambda u: u * GLA_SCALE),
            ("kb", kb_ref, lambda u: u),
            ("vb", vb_ref, lambda u: u),
            ("gl", la_ref, log_decay),
            ("rb", rb_ref, lambda u: u * jax.nn.sigmoid(u)),
            ("ga", ga_ref, jax.nn.sigmoid),
            ("gb", gb_ref, jax.nn.sigmoid))
    u = proj(plan[0][0])
    for idx, (_, out_ref, epilogue) in enumerate(plan):
        u_next = proj(plan[idx + 1][0]) if idx + 1 < len(plan) else None
        out_ref[...] = epilogue(u).astype(out_ref.dtype)
        u = u_next


def _inproj(x2, n1, w_packed, gq, gk, qn, kn, wgk, bgk):
    T = x2.shape[0]
    tm = min(TOK_TILE, T)
    assert T % tm == 0

    def tok(width):
        return pl.BlockSpec((tm, width), lambda i: (i, 0))

    def const(shape):
        return pl.BlockSpec(shape, lambda i: (0, 0))

    outs = (("qa", A_WIDTH, BF16), ("ka", KV_WIDTH, F32), ("va", KV_WIDTH, F32), ("qb", GLA_K_WIDTH, BF16),
            ("kb", GLA_K_WIDTH, BF16), ("vb", B_WIDTH, BF16), ("la", GLA_K_WIDTH, F32), ("rb", B_WIDTH, BF16),
            ("ga", D_MODEL, BF16), ("gb", D_MODEL, BF16))
    return pl.pallas_call(
        _inproj_kernel,
        grid=(T // tm,),
        in_specs=[tok(D_MODEL), const((1, D_MODEL)), const((D_MODEL, IN_PACKED)), const((A_WIDTH, A_WIDTH)),
                  const((KV_WIDTH, KV_WIDTH)), const((1, A_WIDTH)), const((1, KV_WIDTH)),
                  const((GL_PAD, GLA_K_WIDTH)), const((1, GLA_K_WIDTH))],
        out_specs=[tok(w) for _, w, _ in outs],
        out_shape=[jax.ShapeDtypeStruct((T, w), dt) for _, w, dt in outs],
        compiler_params=_cparams(("parallel",)),
        name="inproj",
    )(x2, n1, w_packed, gq, gk, qn, kn, wgk, bgk)


def _swa_kernel(sink_ref, q_ref, kc_ref, vc_ref, kp_ref, vp_ref, o_ref, *, first_block_has_past):
    n = pl.program_id(1)
    nb, lq = q_ref.shape[0], q_ref.shape[1]
    assert lq & (lq - 1) == 0
    stack = SWA_GROUP if lq < WINDOW else 1
    rows, keys = stack * lq, WINDOW + lq
    qi = lax.broadcasted_iota(I32, (rows, keys), 0) & (lq - 1)
    ci = lax.broadcasted_iota(I32, (rows, keys), 1)
    dist = WINDOW + qi - ci
    mask = (dist >= 0) & (dist < WINDOW)
    if not first_block_has_past:
        mask = mask & ((ci >= WINDOW) | (n > 0))
    cols = lambda h0: slice(h0 // SWA_GROUP * SWA_HEAD_DIM, (h0 // SWA_GROUP + 1) * SWA_HEAD_DIM)
    qs, ks, vs = [], [], []
    for j in range(nb):
        q = q_ref[j]
        qs.append(q.astype(F32) if stack > 1 else q)
        ks.append(jnp.concatenate([kp_ref[j], kc_ref[j]], axis=0).astype(BF16))
        vs.append(jnp.concatenate([vp_ref[j], vc_ref[j]], axis=0).astype(BF16))
    groups = [(j, h0) for j in range(nb) for h0 in range(0, SWA_HEADS, stack)]
    wave = len(groups) if stack > 1 else 1
    outs = {}
    for w0 in range(0, len(groups), wave):
        scores, sinks = [], []
        for j, h0 in groups[w0:w0 + wave]:
            heads = range(h0, h0 + stack)
            qg = jnp.concatenate([qs[j][:, h * SWA_HEAD_DIM:(h + 1) * SWA_HEAD_DIM] for h in heads], axis=0)
            sinks.append(jnp.concatenate([jnp.full((lq, 1), sink_ref[h], F32) for h in heads], axis=0))
            scores.append(_dot_nt(qg.astype(BF16), ks[j][:, cols(h0)]))
        probs, denoms = [], []
        for s, sink in zip(scores, sinks):
            s = jnp.where(mask, s, NEG_INF)
            m = jnp.maximum(jnp.max(s, axis=-1, keepdims=True), sink)
            p = jnp.exp(s - m)
            denoms.append(jnp.sum(p, axis=-1, keepdims=True) + jnp.exp(sink - m))
            probs.append(p.astype(BF16))
        for (j, h0), p, denom in zip(groups[w0:w0 + wave], probs, denoms):
            o = _dot(p, vs[j][:, cols(h0)]) / denom
            outs.setdefault(j, []).extend(o[g * lq:(g + 1) * lq] for g in range(stack))
    for j in range(nb):
        o_ref[j] = jnp.concatenate(outs[j], axis=-1).astype(BF16)


def _swa(sinks, qa, ka, va, k_past, v_past, *, has_past):
    B, L, _ = qa.shape
    lq = min(L, WINDOW)
    nl = L // lq
    if has_past:
        assert nl == 1
        nb = min(B, SAMPLE_BATCH)
        past_spec = pl.BlockSpec((nb, WINDOW, KV_WIDTH), lambda b, n, s: (b, 0, 0))
        kp, vp = k_past, v_past
    else:
        assert lq == WINDOW
        nb = 1
        past_spec = pl.BlockSpec((nb, WINDOW, KV_WIDTH), lambda b, n, s: (b, jnp.maximum(n - 1, 0), 0))
        kp, vp = ka, va
    assert B % nb == 0
    cur = lambda w: pl.BlockSpec((nb, lq, w), lambda b, n, s: (b, n, 0))
    return pl.pallas_call(
        functools.partial(_swa_kernel, first_block_has_past=has_past),
        grid_spec=pltpu.PrefetchScalarGridSpec(
            num_scalar_prefetch=1,
            grid=(B // nb, nl),
            in_specs=[cur(A_WIDTH), cur(KV_WIDTH), cur(KV_WIDTH), past_spec, past_spec],
            out_specs=cur(A_WIDTH),
        ),
        out_shape=jax.ShapeDtypeStruct((B, L, A_WIDTH), BF16),
        compiler_params=_cparams(("parallel", "parallel")),
        name="swa",
    )(sinks, qa, ka, va, kp, vp)


def _gla_kernel(q_ref, k_ref, v_ref, la_ref, rb_ref, gn_ref, s0_ref, o_ref, sout_ref, s_scr, *, chunk, sub, length):
    C, SUB = chunk, sub
    S = C // SUB
    W = GLA_K_WIDTH
    nb = q_ref.shape[0]
    s_scr[...] = s0_ref[...]
    row = lax.broadcasted_iota(I32, (C, C), 0)
    col = lax.broadcasted_iota(I32, (C, C), 1)
    tri = (row >= col).astype(BF16)
    diag_mask = (row >= col) & ((row // SUB) == (col // SUB))
    eye = lax.broadcasted_iota(I32, (GLA_DK, GLA_DK), 0) == lax.broadcasted_iota(I32, (GLA_DK, GLA_DK), 1)
    krow = lax.broadcasted_iota(I32, (C, W), 0)

    def rows_of(x, r):
        return jnp.broadcast_to(x[r:r + 1, :], (SUB, W))

    ksl = lambda h: slice(h * GLA_DK, (h + 1) * GLA_DK)
    vsl = lambda h: slice(h * GLA_DV, (h + 1) * GLA_DV)

    def decays(j, r0):
        g_hi, g_lo = _split_bf16(la_ref[j, pl.ds(r0, C), :])
        return _dot(tri, g_hi) + _dot(tri, g_lo)

    def factors(j, r0, b):
        q = q_ref[j, pl.ds(r0, C), :].astype(F32)
        k = k_ref[j, pl.ds(r0, C), :].astype(F32)
        b_last = b[C - 1:C, :]
        mid = jnp.concatenate([rows_of(b, i * SUB + SUB // 2 - 1) for i in range(S)], axis=0)
        beta = jnp.concatenate([jnp.zeros((SUB, W), F32)] + [rows_of(b, i * SUB - 1) for i in range(1, S)], axis=0)
        k_off = [None]
        for i in range(1, S):
            e = jnp.exp(jnp.minimum(b[i * SUB - 1:i * SUB, :] - b, 0.0))
            k_off.append(jnp.where(krow < i * SUB, k * e, 0.0).astype(BF16))
        return dict(q_inter=(q * jnp.exp(b)).astype(BF16), k_state=(k * jnp.exp(b_last - b)).astype(BF16),
                    q_diag=(q * jnp.exp(b - mid)).astype(BF16), k_diag=(k * jnp.exp(mid - b)).astype(BF16),
                    q_off=(q * jnp.exp(b - beta)).astype(BF16), k_off=k_off, dec=jnp.exp(b_last))

    def intra(f, h):
        a_diag = _dot_nt(f["q_diag"][:, ksl(h)], f["k_diag"][:, ksl(h)])
        if S == 1:
            return jnp.where(diag_mask, a_diag, 0.0).astype(BF16)
        blocks = [jnp.zeros((SUB, C), F32)]
        for i in range(1, S):
            blocks.append(_dot_nt(f["q_off"][i * SUB:(i + 1) * SUB, ksl(h)], f["k_off"][i][:, ksl(h)]))
        return jnp.where(diag_mask, a_diag, jnp.concatenate(blocks, axis=0)).astype(BF16)

    def read_out(j, r0, f, a):
        v = v_ref[j, pl.ds(r0, C), :]
        res = []
        for h in range(GLA_HEADS):
            o_h = _dot(a[h], v[:, vsl(h)]) + _dot(f["q_inter"][:, ksl(h)], s_scr[j, h].astype(BF16))
            res.append((o_h, _dot_tn(f["k_state"][:, ksl(h)], v[:, vsl(h)])))
        return res

    def finish(j, r0, f, res):
        outs = []
        for h, (o_h, s_inc) in enumerate(res):
            dec = jnp.broadcast_to(f["dec"][:, ksl(h)], (GLA_DK, GLA_DK))
            dec_col = jnp.sum(jnp.where(eye, dec, 0.0), axis=-1, keepdims=True)
            s_scr[j, h] = dec_col * s_scr[j, h] + s_inc
            ms = jnp.mean(o_h * o_h, axis=-1, keepdims=True)
            outs.append(o_h * lax.rsqrt(ms + NORM_EPS) * gn_ref[...])
        o = jnp.concatenate(outs, axis=-1) * rb_ref[j, pl.ds(r0, C), :].astype(F32)
        o_ref[j, pl.ds(r0, C), :] = o.astype(BF16)

    def step(c):
        r0 = c * C if isinstance(c, int) else pl.multiple_of(c * C, C)
        bs = [decays(j, r0) for j in range(nb)]
        fs = [factors(j, r0, b) for j, b in enumerate(bs)]
        attn = [[intra(f, h) for h in range(GLA_HEADS)] for f in fs]
        res = [read_out(j, r0, fs[j], attn[j]) for j in range(nb)]
        for j in range(nb):
            finish(j, r0, fs[j], res[j])

    if length == C:
        step(0)
    else:
        lax.fori_loop(0, length // C, lambda c, carry: (step(c), carry)[1], 0)
    sout_ref[...] = s_scr[...]


def _gla(qb, kb, vb, la, rb, gn, s0):
    B, L, _ = qb.shape
    chunk = GLA_CHUNK if L % GLA_CHUNK == 0 else L
    sub = GLA_SUB if chunk % GLA_SUB == 0 else chunk
    nb = min(B, SAMPLE_BATCH) if L == chunk else min(B, GLA_LONG_BATCH)
    assert B % nb == 0
    seq = lambda w: pl.BlockSpec((nb, L, w), lambda b: (b, 0, 0))
    st = pl.BlockSpec((nb, GLA_HEADS, GLA_DK, GLA_DV), lambda b: (b, 0, 0, 0))
    return pl.pallas_call(
        functools.partial(_gla_kernel, chunk=chunk, sub=sub, length=L),
        grid=(B // nb,),
        in_specs=[seq(GLA_K_WIDTH), seq(GLA_K_WIDTH), seq(B_WIDTH), seq(GLA_K_WIDTH), seq(B_WIDTH),
                  pl.BlockSpec((1, GLA_DV), lambda b: (0, 0)), st],
        out_specs=[seq(B_WIDTH), st],
        out_shape=[jax.ShapeDtypeStruct((B, L, B_WIDTH), BF16),
                   jax.ShapeDtypeStruct((B, GLA_HEADS, GLA_DK, GLA_DV), F32)],
        scratch_shapes=[pltpu.VMEM((nb, GLA_HEADS, GLA_DK, GLA_DV), F32)],
        compiler_params=_cparams(("parallel",)),
        name="gla",
    )(qb, kb, vb, la, rb, gn, s0)


def _outproj_kernel(x_ref, oa_ref, ob_ref, ga_ref, gb_ref, wa_ref, wb_ref, wo_ref, n2_ref, wr_ref, br_ref, upper_ref,
                    lower_ref, h_ref, hn_ref, g_ref, lp_ref, cnt_ref):
    tm = x_ref.shape[0]
    merged = (ga_ref[...].astype(F32) * _dot(oa_ref[...], wa_ref[...])
              + gb_ref[...].astype(F32) * _dot(ob_ref[...], wb_ref[...]))
    h = x_ref[...] + _dot(merged.astype(BF16), wo_ref[...])
    h_ref[...] = h
    ms = jnp.mean(h * h, axis=-1, keepdims=True)
    hn = h * lax.rsqrt(ms + NORM_EPS) * n2_ref[...]
    hn_ref[...] = hn.astype(BF16)
    hn_hi, hn_lo = _split_bf16(hn)
    wr_hi, wr_lo = _split_bf16(wr_ref[...])
    logits = _dot(hn_hi, wr_hi) + _dot(hn_lo, wr_hi) + _dot(hn_hi, wr_lo) + br_ref[...]
    lt = logits.T[:N_EXPERTS, :]
    eid = lax.broadcasted_iota(I32, (N_EXPERTS, tm), 0)
    vals, hots = [], []
    for k in range(TOP_K):
        m = jnp.max(lt, axis=0, keepdims=True)
        idx = jnp.min(jnp.where(lt == m, eid, N_EXPERTS), axis=0, keepdims=True)
        hot = eid == idx
        lt = jnp.where(hot, -jnp.inf, lt)
        vals.append(m)
        hots.append(hot)
    ex = [jnp.exp(vk - vals[0]) for vk in vals]
    den = ex[0] + ex[1] + ex[2] + ex[3]
    for k in range(TOP_K):
        g_ref[k:k + 1, :] = ex[k] / den
    multi = (hots[0] | hots[1] | hots[2] | hots[3]).astype(BF16)
    before = _dot(multi, upper_ref[...])
    counts = jnp.sum(multi.astype(F32), axis=1, keepdims=True)
    aligned = jnp.floor((counts + (RUN_ALIGN - 1)) * (1.0 / RUN_ALIGN)) * RUN_ALIGN
    run_start = _dot(lower_ref[...], jnp.broadcast_to(aligned, (N_EXPERTS, LANE)).astype(BF16))[:, :1]
    place = before + run_start
    for k in range(TOP_K):
        lp_ref[k:k + 1, :] = jnp.sum(jnp.where(hots[k], place, 0.0), axis=0, keepdims=True).astype(I32)
    cnt_ref[...] = jnp.broadcast_to(aligned, cnt_ref.shape).astype(I32)


def _outproj(x2, oa, ob, ga, gb, wa, wb, wo, n2, wr, br, upper, lower):
    T = x2.shape[0]
    tm = TOK_TILE
    assert T % tm == 0 and upper.shape == (tm, tm)
    tok = lambda w: pl.BlockSpec((tm, w), lambda i: (i, 0))
    const = lambda shape: pl.BlockSpec(shape, lambda i: (0, 0))
    kt = pl.BlockSpec((TOP_K, tm), lambda i: (0, i))
    return pl.pallas_call(
        _outproj_kernel,
        grid=(T // tm,),
        in_specs=[tok(D_MODEL), tok(A_WIDTH), tok(B_WIDTH), tok(D_MODEL), tok(D_MODEL),
                  const((A_WIDTH, D_MODEL)), const((B_WIDTH, D_MODEL)), const((D_MODEL, D_MODEL)),
                  const((1, D_MODEL)), const((D_MODEL, LANE)), const((1, LANE)), const((tm, tm)),
                  const((N_EXPERTS, N_EXPERTS))],
        out_specs=[tok(D_MODEL), tok(D_MODEL), kt, kt, pl.BlockSpec((N_EXPERTS, LANE), lambda i: (i, 0))],
        out_shape=[jax.ShapeDtypeStruct((T, D_MODEL), F32), jax.ShapeDtypeStruct((T, D_MODEL), BF16),
                   jax.ShapeDtypeStruct((TOP_K, T), F32), jax.ShapeDtypeStruct((TOP_K, T), I32),
                   jax.ShapeDtypeStruct((T // tm * N_EXPERTS, LANE), I32)],
        compiler_params=_cparams(("parallel",)),
        name="outproj",
    )(x2, oa, ob, ga, gb, wa, wb, wo, n2, wr, br, upper, lower)


def _pack_rows(v, *, is_bf16_valued=False):
    if not is_bf16_valued:
        v = v.astype(BF16).astype(F32)
    lo = pltpu.bitcast(v[:, :ROW_WORDS], U32) >> 16
    hi = pltpu.bitcast(v[:, ROW_WORDS:], U32) & jnp.uint32(0xFFFF0000)
    return lo | hi


def _unpack_rows(w):
    lo = pltpu.bitcast(w << 16, F32).astype(BF16)
    hi = pltpu.bitcast(w & jnp.uint32(0xFFFF0000), F32).astype(BF16)
    return lo, hi


def _start_runs(runs, tile, tile_rows, global_rows, sem, *, to_global):
    cnt_ref, off_ref, dest_ref, _ = runs

    def run(e):
        j = tile * N_EXPERTS + e
        n = pl.multiple_of(cnt_ref[j], RUN_ALIGN)
        local = tile_rows.at[pl.ds(pl.multiple_of(off_ref[j], RUN_ALIGN), n), :]
        far = global_rows.at[pl.ds(pl.multiple_of(dest_ref[j], RUN_ALIGN), n), :]
        return pltpu.make_async_copy(local, far, sem) if to_global else pltpu.make_async_copy(far, local, sem)

    for e in range(N_EXPERTS):
        pl.when(cnt_ref[tile * N_EXPERTS + e] > 0)(lambda e=e: run(e).start(priority=ROW_DMA_PRIORITY))


def _wait_runs(runs, tile, tile_rows, global_rows, sem):
    total = pl.multiple_of(runs[3][tile], RUN_ALIGN)
    pltpu.make_async_copy(global_rows.at[pl.ds(0, total), :], tile_rows.at[pl.ds(0, total), :], sem).wait()


def _dispatch_kernel(fill_ref, cnt_ref, off_ref, dest_ref, tot_ref, lp_ref, hn_p_ref, hn_s_ref, xs_ref,
                     rows2, sems, zbuf, zsem, *, steps_p):
    i = pl.program_id(0)
    runs = (cnt_ref, off_ref, dest_ref, tot_ref)
    slot = i % 2
    rows = rows2.at[slot]

    @pl.when(i == 0)
    def _():
        zbuf[...] = jnp.zeros_like(zbuf)

        def fill_copy(j):
            return pltpu.make_async_copy(zbuf, xs_ref.at[pl.ds(pl.multiple_of(fill_ref[j], MOE_BM), MOE_BM), :], zsem)

        for j in range(fill_ref.shape[0]):
            pl.when(fill_ref[j] >= 0)(lambda j=j: fill_copy(j).start())
        for j in range(fill_ref.shape[0]):
            pl.when(fill_ref[j] >= 0)(lambda j=j: fill_copy(j).wait())

    def sort_tile(hn_ref):
        hn = hn_ref[...]
        lp = lp_ref[...]
        for c in range(SORT_ROWS // SORT_CHUNK):
            r = lax.broadcasted_iota(I32, (SORT_CHUNK, TOK_TILE), 0) + c * SORT_CHUNK
            hit = (r == lp[0:1, :]) | (r == lp[1:2, :]) | (r == lp[2:3, :]) | (r == lp[3:4, :])
            perm = jnp.where(hit, 1.0, 0.0).astype(BF16)
            rows[c * SORT_CHUNK:(c + 1) * SORT_CHUNK, :] = _pack_rows(_dot(perm, hn), is_bf16_valued=True)

    pl.when(i < steps_p)(lambda: sort_tile(hn_p_ref))
    pl.when(i >= steps_p)(lambda: sort_tile(hn_s_ref))
    _start_runs(runs, i, rows, xs_ref, sems.at[slot], to_global=True)
    pl.when(i > 0)(lambda: _wait_runs(runs, i - 1, rows2.at[1 - slot], xs_ref, sems.at[1 - slot]))
    pl.when(i == pl.num_programs(0) - 1)(lambda: _wait_runs(runs, i, rows, xs_ref, sems.at[slot]))


def _dispatch(fill, runs, lp, hn_p, hn_s, n_rows):
    tm = TOK_TILE
    steps_p, steps_s = hn_p.shape[0] // tm, hn_s.shape[0] // tm
    return pl.pallas_call(
        functools.partial(_dispatch_kernel, steps_p=steps_p),
        grid_spec=pltpu.PrefetchScalarGridSpec(
            num_scalar_prefetch=5,
            grid=(steps_p + steps_s,),
            in_specs=[pl.BlockSpec((TOP_K, tm), lambda i, *_: (0, i)),
                      pl.BlockSpec((tm, D_MODEL), lambda i, *_: (jnp.minimum(i, steps_p - 1), 0)),
                      pl.BlockSpec((tm, D_MODEL), lambda i, *_: (jnp.maximum(i - steps_p, 0), 0))],
            out_specs=pl.BlockSpec(memory_space=pl.ANY),
            scratch_shapes=[pltpu.VMEM((2, SORT_ROWS, ROW_WORDS), U32), pltpu.SemaphoreType.DMA((2,)),
                            pltpu.VMEM((MOE_BM, ROW_WORDS), U32), pltpu.SemaphoreType.DMA],
        ),
        out_shape=jax.ShapeDtypeStruct((n_rows, ROW_WORDS), U32),
        compiler_params=_cparams(("arbitrary",)),
        name="dispatch",
    )(fill, *runs, lp, hn_p, hn_s)


def _experts_kernel(row0_ref, nblk_ref, spare_ref, w1_ref, b1g_ref, b1l_ref, w2_ref, b2_ref, sel_ref, xs_ref, y_ref,
                    w1g_scr, w1l_scr, w2_scr, xbuf, ybuf, xtail, ytail, in_sems, out_sems, tail_sems):
    e = pl.program_id(0)
    n = nblk_ref[e]
    row0 = row0_ref[e]
    per_big = EXPERT_BIG // MOE_BM
    nbig = n // per_big
    left = n - nbig * per_big

    def rows_at(ref, first, size):
        return ref.at[pl.ds(pl.multiple_of(first, MOE_BM), size), :]

    def fetch(j, slot):
        return pltpu.make_async_copy(rows_at(xs_ref, row0 + j * EXPERT_BIG, EXPERT_BIG), xbuf.at[slot],
                                     in_sems.at[slot])

    def put(j, slot):
        return pltpu.make_async_copy(ybuf.at[slot], rows_at(y_ref, row0 + j * EXPERT_BIG, EXPERT_BIG),
                                     out_sems.at[slot])

    tails = []
    first, at = row0 + nbig * EXPERT_BIG, 0
    for t, size in enumerate(EXPERT_TAILS):
        present = (left & (size // MOE_BM)) != 0
        tails.append((present, first, at, size, t))
        first = first + jnp.where(present, size, 0)
        at += size

    def tail_fetch(first, at, size, t):
        return pltpu.make_async_copy(rows_at(xs_ref, first, size), xtail.at[pl.ds(at, size), :], tail_sems.at[0, t])

    def tail_put(first, at, size, t):
        return pltpu.make_async_copy(ytail.at[pl.ds(at, size), :], rows_at(y_ref, first, size), tail_sems.at[1, t])

    pl.when(nbig > 0)(lambda: fetch(0, 0).start(priority=ROW_DMA_PRIORITY))
    for present, *where in tails:
        pl.when(present)(lambda where=where: tail_fetch(*where).start(priority=ROW_DMA_PRIORITY))

    @pl.when(n > 0)
    def _():
        for c in range(D_FF // LANE):
            cols = _dot(w1_ref[0, :, c * 2 * LANE:(c + 1) * 2 * LANE].astype(BF16), sel_ref[...])
            w1g_scr[:, c * LANE:(c + 1) * LANE] = cols[:, :LANE].astype(BF16)
            w1l_scr[:, c * LANE:(c + 1) * LANE] = cols[:, LANE:].astype(BF16)
        w2_scr[...] = w2_ref[0].astype(BF16)

    def mlp(words):
        x_lo, x_hi = _unpack_rows(words)

        def up(w_scr, b_ref):
            return _dot(x_lo, w_scr[:ROW_WORDS, :]) + _dot(x_hi, w_scr[ROW_WORDS:, :]) + b_ref[0]

        x_glu = jnp.minimum(up(w1g_scr, b1g_ref), SWIGLU_LIMIT)
        x_lin = jnp.clip(up(w1l_scr, b1l_ref), -SWIGLU_LIMIT, SWIGLU_LIMIT)
        act = x_glu * jax.nn.sigmoid(SWIGLU_ALPHA * x_glu) * (x_lin + 1.0)
        return _pack_rows(_dot(act.astype(BF16), w2_scr[...]) + b2_ref[0])

    def big_block(j, carry):
        slot = j % 2
        fetch(j, slot).wait()
        pl.when(j + 1 < nbig)(lambda: fetch(j + 1, 1 - slot).start(priority=ROW_DMA_PRIORITY))
        pl.when(j >= 2)(lambda: put(j - 2, slot).wait())
        ybuf[slot] = mlp(xbuf[slot])
        put(j, slot).start(priority=ROW_DMA_PRIORITY)
        return carry

    lax.fori_loop(0, nbig, big_block, 0)

    for present, *where in tails:
        @pl.when(present)
        def _(where=where):
            _, at, size, _ = where
            tail_fetch(*where).wait()
            ytail[at:at + size, :] = mlp(xtail[at:at + size, :])
            tail_put(*where).start(priority=ROW_DMA_PRIORITY)

    pl.when(nbig >= 2)(lambda: put(nbig - 2, nbig % 2).wait())
    pl.when(nbig >= 1)(lambda: put(nbig - 1, (nbig - 1) % 2).wait())
    for present, *where in tails:
        pl.when(present)(lambda where=where: tail_put(*where).wait())

    @pl.when(e == pl.num_programs(0) - 1)
    def _():
        ytail[0:MOE_BM, :] = jnp.zeros((MOE_BM, ROW_WORDS), U32)

        def spare_copy(j):
            return pltpu.make_async_copy(ytail.at[pl.ds(0, MOE_BM), :],
                                         rows_at(y_ref, spare_ref[0] + j * MOE_BM, MOE_BM), tail_sems.at[1, 0])

        lax.fori_loop(0, spare_ref[1], lambda j, c: (spare_copy(j).start(), c)[1], 0)
        lax.fori_loop(0, spare_ref[1], lambda j, c: (spare_copy(j).wait(), c)[1], 0)


def _experts(row0, nblk, spare, xs, w1, b1g, b1l, w2, b2, sel):
    wspec = lambda r, c: pl.BlockSpec((1, r, c), lambda e, *_: (e, 0, 0))
    return pl.pallas_call(
        _experts_kernel,
        grid_spec=pltpu.PrefetchScalarGridSpec(
            num_scalar_prefetch=3,
            grid=(N_EXPERTS,),
            in_specs=[wspec(D_MODEL, 2 * D_FF), wspec(1, D_FF), wspec(1, D_FF), wspec(D_FF, D_MODEL),
                      wspec(1, D_MODEL), pl.BlockSpec((2 * LANE, 2 * LANE), lambda e, *_: (0, 0)),
                      pl.BlockSpec(memory_space=pl.ANY)],
            out_specs=pl.BlockSpec(memory_space=pl.ANY),
            scratch_shapes=[pltpu.VMEM((D_MODEL, D_FF), BF16), pltpu.VMEM((D_MODEL, D_FF), BF16),
                            pltpu.VMEM((D_FF, D_MODEL), BF16), pltpu.VMEM((2, EXPERT_BIG, ROW_WORDS), U32),
                            pltpu.VMEM((2, EXPERT_BIG, ROW_WORDS), U32),
                            pltpu.VMEM((sum(EXPERT_TAILS), ROW_WORDS), U32),
                            pltpu.VMEM((sum(EXPERT_TAILS), ROW_WORDS), U32), pltpu.SemaphoreType.DMA((2,)),
                            pltpu.SemaphoreType.DMA((2,)), pltpu.SemaphoreType.DMA((2, len(EXPERT_TAILS)))],
        ),
        out_shape=jax.ShapeDtypeStruct(xs.shape, U32),
        compiler_params=_cparams(("arbitrary",)),
        name="experts",
    )(row0, nblk, spare, w1, b1g, b1l, w2, b2, sel, xs)


def _combine_kernel(cnt_ref, off_ref, dest_ref, tot_ref, lp_ref, g_ref, h_ref, yb_ref, y_ref, rows2, sems, *, tile0):
    i = pl.program_id(0)
    runs = (cnt_ref, off_ref, dest_ref, tot_ref)
    slot = i % 2
    rows = rows2.at[slot]

    def fetch(step, into):
        _start_runs(runs, step + tile0, rows2.at[into], yb_ref, sems.at[into], to_global=False)

    @pl.when(i == 0)
    def _():
        rows2[...] = jnp.zeros_like(rows2)
        fetch(0, 0)

    pl.when(i + 1 < pl.num_programs(0))(lambda: fetch(i + 1, 1 - slot))
    _wait_runs(runs, i + tile0, rows, yb_ref, sems.at[slot])
    lp, g = lp_ref[...], g_ref[...]
    r = lax.broadcasted_iota(I32, (TOK_TILE, SORT_ROWS), 1)
    mix = jnp.zeros((TOK_TILE, SORT_ROWS), F32)
    for k in range(TOP_K):
        mix = jnp.where(r == lp[:, k:k + 1], g[:, k:k + 1], mix)
    mix = mix.astype(BF16)
    y_lo, y_hi = _unpack_rows(rows[...])
    y_ref[...] = h_ref[...] + jnp.concatenate([_dot(mix, y_lo), _dot(mix, y_hi)], axis=-1)


def _combine(runs, lp_t, gate_t, h, yb, tile0):
    T = h.shape[0]
    tm = TOK_TILE
    tok = lambda w: pl.BlockSpec((tm, w), lambda i, *_: (i, 0))
    return pl.pallas_call(
        functools.partial(_combine_kernel, tile0=tile0),
        grid_spec=pltpu.PrefetchScalarGridSpec(
            num_scalar_prefetch=4,
            grid=(T // tm,),
            in_specs=[tok(TOP_K), tok(TOP_K), tok(D_MODEL), pl.BlockSpec(memory_space=pl.ANY)],
            out_specs=tok(D_MODEL),
            scratch_shapes=[pltpu.VMEM((2, SORT_ROWS, ROW_WORDS), U32), pltpu.SemaphoreType.DMA((2,))],
        ),
        out_shape=jax.ShapeDtypeStruct((T, D_MODEL), F32),
        compiler_params=_cparams(("arbitrary",)),
        name="combine",
    )(*runs, lp_t, gate_t, h, yb)


def _block_ones(width, block):
    idx = np.arange(width) // block
    return jnp.asarray(idx[:, None] == idx[None, :], BF16)


def _layer(x_p, x_s, cache_k, cache_v, s_gla, norm1, w_in, q_norm, k_norm, sinks, w_gk, b_gk,
           gla_norm, w_a, w_b, w_o, norm2, w_router, b_router, w1, b1, w2, b2):
    Bp, Lp, _ = x_p.shape
    Bs, Ls, _ = x_s.shape
    Tp, Ts = Bp * Lp, Bs * Ls

    offs = np.cumsum((0, A_WIDTH, KV_WIDTH, KV_WIDTH, GLA_K_WIDTH, GLA_K_WIDTH, B_WIDTH, GLA_GATE_RANK, B_WIDTH,
                      D_MODEL, D_MODEL))
    src = dict(zip(("qa", "ka", "va", "qb", "kb", "vb", "gl", "rb", "ga", "gb"), zip(offs[:-1], offs[1:])))
    cols = []
    for name, width in _SEGS:
        a, b = src[name]
        piece = w_in[:, a:b]
        if b - a < width:
            piece = jnp.pad(piece, ((0, 0), (0, width - (b - a))))
        cols.append(piece)
    w_packed = jnp.concatenate(cols, axis=1).astype(BF16)
    wgk = jnp.pad(w_gk, ((0, GL_PAD - GLA_GATE_RANK), (0, 0))).astype(BF16)
    n1 = norm1.reshape(1, D_MODEL)
    gq, gk = _block_ones(A_WIDTH, SWA_HEAD_DIM), _block_ones(KV_WIDTH, SWA_HEAD_DIM)
    qn = jnp.tile(q_norm, SWA_HEADS).reshape(1, A_WIDTH)
    kn = jnp.tile(k_norm, SWA_KV_HEADS).reshape(1, KV_WIDTH)
    bgk = b_gk.reshape(1, GLA_K_WIDTH)
    gn = gla_norm.reshape(1, GLA_DV)
    wa, wb, wo = w_a.astype(BF16), w_b.astype(BF16), w_o.astype(BF16)
    n2 = norm2.reshape(1, D_MODEL)
    wr = jnp.pad(w_router, ((0, 0), (0, LANE - N_EXPERTS)))
    br = jnp.pad(b_router, (0, LANE - N_EXPERTS)).reshape(1, LANE)
    b1g = b1[:, 0::2].reshape(N_EXPERTS, 1, D_FF)
    b1l = b1[:, 1::2].reshape(N_EXPERTS, 1, D_FF)
    b2r = b2.reshape(N_EXPERTS, 1, D_MODEL)
    sel_np = np.zeros((2 * LANE, 2 * LANE), np.float32)
    sel_np[2 * np.arange(LANE), np.arange(LANE)] = 1.0
    sel_np[2 * np.arange(LANE) + 1, LANE + np.arange(LANE)] = 1.0
    sel = jnp.asarray(sel_np, BF16)

    def mix(x, k_past, v_past, s0):
        B, L, _ = x.shape
        T = B * L
        x2 = x.reshape(T, D_MODEL)
        qa, ka, va, qb, kb, vb, la, rb, ga, gb = _inproj(x2, n1, w_packed, gq, gk, qn, kn, wgk, bgk)
        r3 = lambda t: t.reshape(B, L, t.shape[-1])
        oa = _swa(sinks, r3(qa), r3(ka), r3(va), k_past, v_past, has_past=k_past is not None)
        ob, s_out = _gla(r3(qb), r3(kb), r3(vb), r3(la), r3(rb), gn, s0)
        h, hn, gate, lp, cnt = _outproj(x2, oa.reshape(T, A_WIDTH), ob.reshape(T, B_WIDTH), ga, gb,
                                        wa, wb, wo, n2, wr, br, upper, lower)
        return r3(ka), r3(va), s_out, h, hn, gate, lp, cnt[:, 0].reshape(T // TOK_TILE, N_EXPERTS)

    upper = jnp.asarray(np.arange(TOK_TILE)[:, None] < np.arange(TOK_TILE)[None, :], BF16)
    lower = jnp.asarray(np.arange(N_EXPERTS)[:, None] > np.arange(N_EXPERTS)[None, :], BF16)
    win = cache_k.shape[1]
    assert win == WINDOW
    ck = cache_k.reshape(Bs, win, KV_WIDTH)
    cv = cache_v.reshape(Bs, win, KV_WIDTH)
    ka_p, va_p, s_p, h_p, hn_p, g_p, lp_p, cnt_p = mix(x_p, None, None, jnp.zeros((Bp, GLA_HEADS, GLA_DK, GLA_DV), F32))
    ka_s, va_s, s_s, h_s, hn_s, g_s, lp_s, cnt_s = mix(x_s, ck, cv, s_gla)

    T = Tp + Ts
    cnt = jnp.concatenate([cnt_p, cnt_s], axis=0)
    most_rows = T * TOP_K + cnt.size * (RUN_ALIGN - 1) + N_EXPERTS * (MOE_BM - 1)
    n_rows = -(-most_rows // MOE_BM) * MOE_BM
    counts = jnp.sum(cnt, axis=0)
    padded = (counts + MOE_BM - 1) // MOE_BM * MOE_BM
    pad_end = jnp.cumsum(padded)
    start = pad_end - padded
    dest = start[None, :] + jnp.cumsum(cnt, axis=0) - cnt
    off = jnp.cumsum(cnt, axis=1) - cnt
    flat = lambda a: a.reshape(-1).astype(I32)
    runs = (flat(cnt), flat(off), flat(dest), flat(jnp.sum(cnt, axis=1)))
    tails = jnp.where(padded > 0, pad_end - MOE_BM, -1)
    most_spare = (n_rows - T * TOP_K) // MOE_BM
    spare = pad_end[-1] + jnp.arange(most_spare, dtype=I32) * MOE_BM
    fill = jnp.concatenate([tails, jnp.where(spare < n_rows, spare, -1)]).astype(I32)
    spare_blocks = jnp.stack([pad_end[-1], (n_rows - pad_end[-1]) // MOE_BM]).astype(I32)

    xs = _dispatch(fill, runs, jnp.concatenate([lp_p, lp_s], axis=1), hn_p, hn_s, n_rows)
    yb = _experts(start.astype(I32), (padded // MOE_BM).astype(I32), spare_blocks, xs, w1, b1g, b1l, w2, b2r, sel)
    y_p = _combine(runs, lp_p.T, g_p.T, h_p, yb, 0).reshape(Bp, Lp, D_MODEL)
    y_s = _combine(runs, lp_s.T, g_s.T, h_s, yb, Tp // TOK_TILE).reshape(Bs, Ls, D_MODEL)

    kv = lambda t: t.reshape(t.shape[0], t.shape[1], SWA_KV_HEADS, SWA_HEAD_DIM)
    k_p, v_p = kv(ka_p[:, -WINDOW:]), kv(va_p[:, -WINDOW:])
    k_s = jnp.concatenate([cache_k, kv(ka_s)], axis=1)[:, -win:]
    v_s = jnp.concatenate([cache_v, kv(va_s)], axis=1)[:, -win:]
    return y_p, y_s, (k_p, v_p, s_p, k_s, v_s, s_s)


def kernel(x_prompt, x_sample, cache_swa_k, cache_swa_v, state_gla, norm1, w_in, q_norm, k_norm, sinks, w_gk, b_gk,
           gla_norm, w_a, w_b, w_o, norm2, w_router, b_router, w1, b1, w2, b2):
    depth = norm1.shape[0]
    y_p, y_s = x_prompt, x_sample
    states = []
    for l in range(depth):
        y_p, y_s, st = _layer(y_p, y_s, cache_swa_k[l], cache_swa_v[l], state_gla[l], norm1[l], w_in[l], q_norm[l],
                              k_norm[l], sinks[l], w_gk[l], b_gk[l], gla_norm[l], w_a[l], w_b[l], w_o[l], norm2[l],
                              w_router[l], b_router[l], w1[l], b1[l], w2[l], b2[l])
        states.append(st)
    return (y_p, y_s) + tuple(jnp.stack([st[j] for st in states]) for j in range(6))
```

```python
import functools

import jax
import jax.numpy as jnp
import numpy as np
from jax import lax
from jax.experimental import pallas as pl
from jax.experimental.pallas import tpu as pltpu

F32 = jnp.float32
BF16 = jnp.bfloat16
I32 = jnp.int32
U32 = jnp.uint32

D_MODEL = 1024
SWA_HEADS = 8
SWA_KV_HEADS = 2
SWA_GROUP = SWA_HEADS // SWA_KV_HEADS
SWA_HEAD_DIM = 64
WINDOW = 128
ATTN_SCALE = SWA_HEAD_DIM ** -0.5
GLA_HEADS = 4
GLA_DK = 64
GLA_DV = 128
GLA_GATE_RANK = 16
GLA_GATE_NORM = 16.0
GLA_SCALE = GLA_DK ** -0.5
A_WIDTH = SWA_HEADS * SWA_HEAD_DIM
KV_WIDTH = SWA_KV_HEADS * SWA_HEAD_DIM
GLA_K_WIDTH = GLA_HEADS * GLA_DK
B_WIDTH = GLA_HEADS * GLA_DV
N_EXPERTS = 32
TOP_K = 4
D_FF = D_MODEL
SWIGLU_ALPHA = 1.702
SWIGLU_LIMIT = 7.0
NORM_EPS = 1e-5
QK_EPS = 1e-6
NEG_INF = -1e30

LANE = 128
GL_PAD = LANE
VMEM_LIMIT = 56 * 1024 * 1024

_SEGS = (("qa", A_WIDTH), ("ka", KV_WIDTH), ("va", KV_WIDTH), ("qb", GLA_K_WIDTH), ("kb", GLA_K_WIDTH),
         ("vb", B_WIDTH), ("rb", B_WIDTH), ("ga", D_MODEL), ("gb", D_MODEL), ("gl", GL_PAD))
_OFF = {}
_o = 0
for _n, _w in _SEGS:
    _OFF[_n] = (_o, _w)
    _o += _w
IN_PACKED = _o

TOK_TILE = 512
GLA_CHUNK = 64
GLA_SUB = 16
MOE_BM = 512
ROW_DMA_PRIORITY = 1
ROW_WORDS = D_MODEL // 2
RUN_ALIGN = 8
SORT_CHUNK = 256
SORT_ROWS = -(-(TOK_TILE * TOP_K + N_EXPERTS * (RUN_ALIGN - 1)) // SORT_CHUNK) * SORT_CHUNK
SAMPLE_BATCH = 8
GLA_LONG_BATCH = 2


def _cparams(sem):
    return pltpu.CompilerParams(dimension_semantics=sem, vmem_limit_bytes=VMEM_LIMIT)


def _split_bf16(v):
    hi = v.astype(BF16)
    lo = (v - hi.astype(F32)).astype(BF16)
    return hi, lo


def _dot(a, b):
    return jnp.dot(a, b, preferred_element_type=F32)


def _dot_nt(a, b):
    return lax.dot_general(a, b, (((1,), (1,)), ((), ())), preferred_element_type=F32)


def _dot_tn(a, b):
    return lax.dot_general(a, b, (((0,), (0,)), ((), ())), preferred_element_type=F32)


def _inproj_kernel(x_ref, n1_ref, w_ref, gq_ref, gk_ref, qn_ref, kn_ref, wgk_ref, bgk_ref,
                   qa_ref, ka_ref, va_ref, qb_ref, kb_ref, vb_ref, la_ref, rb_ref, ga_ref, gb_ref):
    x = x_ref[...]
    ms = jnp.mean(x * x, axis=-1, keepdims=True)
    xn = (x * lax.rsqrt(ms + NORM_EPS) * n1_ref[...]).astype(BF16)

    def proj(name):
        off, width = _OFF[name]
        return _dot(xn, w_ref[:, off:off + width])

    def head_norm(v, ones_ref, gain_ref):
        hi, lo = _split_bf16(v * v)
        ss = _dot(hi, ones_ref[...]) + _dot(lo, ones_ref[...])
        return v * lax.rsqrt(ss * (1.0 / SWA_HEAD_DIM) + QK_EPS) * gain_ref[...]

    def log_decay(gl):
        z = _dot(gl.astype(BF16), wgk_ref[...]) + bgk_ref[...]
        return (jnp.minimum(z, 0.0) - jnp.log1p(jnp.exp(-jnp.abs(z)))) * (1.0 / GLA_GATE_NORM)

    plan = (("qa", qa_ref, lambda u: head_norm(u, gq_ref, qn_ref) * ATTN_SCALE),
            ("ka", ka_ref, lambda u: head_norm(u, gk_ref, kn_ref)),
            ("va", va_ref, lambda u: u),
            ("qb", qb_ref, lambda u: u * GLA_SCALE),
            ("kb", kb_ref, lambda u: u),
            ("vb", vb_ref, lambda u: u),
            ("gl", la_ref, log_decay),
            ("rb", rb_ref, lambda u: u * jax.nn.sigmoid(u)),
            ("ga", ga_ref, jax.nn.sigmoid),
            ("gb", gb_ref, jax.nn.sigmoid))
    u = proj(plan[0][0])
    for idx, (_, out_ref, epilogue) in enumerate(plan):
        u_next = proj(plan[idx + 1][0]) if idx + 1 < len(plan) else None
        out_ref[...] = epilogue(u).astype(out_ref.dtype)
        u = u_next


def _inproj(x2, n1, w_packed, gq, gk, qn, kn, wgk, bgk):
    T = x2.shape[0]
    tm = min(TOK_TILE, T)
    assert T % tm == 0

    def tok(width):
        return pl.BlockSpec((tm, width), lambda i: (i, 0))

    def const(shape):
        return pl.BlockSpec(shape, lambda i: (0, 0))

    outs = (("qa", A_WIDTH, BF16), ("ka", KV_WIDTH, F32), ("va", KV_WIDTH, F32), ("qb", GLA_K_WIDTH, BF16),
            ("kb", GLA_K_WIDTH, BF16), ("vb", B_WIDTH, BF16), ("la", GLA_K_WIDTH, F32), ("rb", B_WIDTH, BF16),
            ("ga", D_MODEL, BF16), ("gb", D_MODEL, BF16))
    return pl.pallas_call(
        _inproj_kernel,
        grid=(T // tm,),
        in_specs=[tok(D_MODEL), const((1, D_MODEL)), const((D_MODEL, IN_PACKED)), const((A_WIDTH, A_WIDTH)),
                  const((KV_WIDTH, KV_WIDTH)), const((1, A_WIDTH)), const((1, KV_WIDTH)),
                  const((GL_PAD, GLA_K_WIDTH)), const((1, GLA_K_WIDTH))],
        out_specs=[tok(w) for _, w, _ in outs],
        out_shape=[jax.ShapeDtypeStruct((T, w), dt) for _, w, dt in outs],
        compiler_params=_cparams(("parallel",)),
        name="inproj",
    )(x2, n1, w_packed, gq, gk, qn, kn, wgk, bgk)


def _swa_kernel(sink_ref, q_ref, kc_ref, vc_ref, kp_ref, vp_ref, o_ref, *, first_block_has_past):
    n = pl.program_id(1)
    nb, lq = q_ref.shape[0], q_ref.shape[1]
    assert lq & (lq - 1) == 0
    stack = SWA_GROUP if lq < WINDOW else 1
    rows, keys = stack * lq, WINDOW + lq
    qi = lax.broadcasted_iota(I32, (rows, keys), 0) & (lq - 1)
    ci = lax.broadcasted_iota(I32, (rows, keys), 1)
    dist = WINDOW + qi - ci
    mask = (dist >= 0) & (dist < WINDOW)
    if not first_block_has_past:
        mask = mask & ((ci >= WINDOW) | (n > 0))
    cols = lambda h0: slice(h0 // SWA_GROUP * SWA_HEAD_DIM, (h0 // SWA_GROUP + 1) * SWA_HEAD_DIM)
    qs, ks, vs = [], [], []
    for j in range(nb):
        q = q_ref[j]
        qs.append(q.astype(F32) if stack > 1 else q)
        ks.append(jnp.concatenate([kp_ref[j], kc_ref[j]], axis=0).astype(BF16))
        vs.append(jnp.concatenate([vp_ref[j], vc_ref[j]], axis=0).astype(BF16))
    groups = [(j, h0) for j in range(nb) for h0 in range(0, SWA_HEADS, stack)]
    wave = len(groups) if stack > 1 else 1
    outs = {}
    for w0 in range(0, len(groups), wave):
        scores, sinks = [], []
        for j, h0 in groups[w0:w0 + wave]:
            heads = range(h0, h0 + stack)
            qg = jnp.concatenate([qs[j][:, h * SWA_HEAD_DIM:(h + 1) * SWA_HEAD_DIM] for h in heads], axis=0)
            sinks.append(jnp.concatenate([jnp.full((lq, 1), sink_ref[h], F32) for h in heads], axis=0))
            scores.append(_dot_nt(qg.astype(BF16), ks[j][:, cols(h0)]))
        probs, denoms = [], []
        for s, sink in zip(scores, sinks):
            s = jnp.where(mask, s, NEG_INF)
            m = jnp.maximum(jnp.max(s, axis=-1, keepdims=True), sink)
            p = jnp.exp(s - m)
            denoms.append(jnp.sum(p, axis=-1, keepdims=True) + jnp.exp(sink - m))
            probs.append(p.astype(BF16))
        for (j, h0), p, denom in zip(groups[w0:w0 + wave], probs, denoms):
            o = _dot(p, vs[j][:, cols(h0)]) / denom
            outs.setdefault(j, []).extend(o[g * lq:(g + 1) * lq] for g in range(stack))
    for j in range(nb):
        o_ref[j] = jnp.concatenate(outs[j], axis=-1).astype(BF16)


def _swa(sinks, qa, ka, va, k_past, v_past, *, has_past):
    B, L, _ = qa.shape
    lq = min(L, WINDOW)
    nl = L // lq
    if has_past:
        assert nl == 1
        nb = min(B, SAMPLE_BATCH)
        past_spec = pl.BlockSpec((nb, WINDOW, KV_WIDTH), lambda b, n, s: (b, 0, 0))
        kp, vp = k_past, v_past
    else:
        assert lq == WINDOW
        nb = 1
        past_spec = pl.BlockSpec((nb, WINDOW, KV_WIDTH), lambda b, n, s: (b, jnp.maximum(n - 1, 0), 0))
        kp, vp = ka, va
    assert B % nb == 0
    cur = lambda w: pl.BlockSpec((nb, lq, w), lambda b, n, s: (b, n, 0))
    return pl.pallas_call(
        functools.partial(_swa_kernel, first_block_has_past=has_past),
        grid_spec=pltpu.PrefetchScalarGridSpec(
            num_scalar_prefetch=1,
            grid=(B // nb, nl),
            in_specs=[cur(A_WIDTH), cur(KV_WIDTH), cur(KV_WIDTH), past_spec, past_spec],
            out_specs=cur(A_WIDTH),
        ),
        out_shape=jax.ShapeDtypeStruct((B, L, A_WIDTH), BF16),
        compiler_params=_cparams(("parallel", "parallel")),
        name="swa",
    )(sinks, qa, ka, va, kp, vp)


def _gla_kernel(q_ref, k_ref, v_ref, la_ref, rb_ref, gn_ref, s0_ref, o_ref, sout_ref, s_scr, *, chunk, sub, length):
    C, SUB = chunk, sub
    S = C // SUB
    W = GLA_K_WIDTH
    nb = q_ref.shape[0]
    s_scr[...] = s0_ref[...]
    row = lax.broadcasted_iota(I32, (C, C), 0)
    col = lax.broadcasted_iota(I32, (C, C), 1)
    tri = (row >= col).astype(BF16)
    diag_mask = (row >= col) & ((row // SUB) == (col // SUB))
    eye = lax.broadcasted_iota(I32, (GLA_DK, GLA_DK), 0) == lax.broadcasted_iota(I32, (GLA_DK, GLA_DK), 1)
    krow = lax.broadcasted_iota(I32, (C, W), 0)

    def rows_of(x, r):
        return jnp.broadcast_to(x[r:r + 1, :], (SUB, W))

    ksl = lambda h: slice(h * GLA_DK, (h + 1) * GLA_DK)
    vsl = lambda h: slice(h * GLA_DV, (h + 1) * GLA_DV)

    def decays(j, r0):
        g_hi, g_lo = _split_bf16(la_ref[j, pl.ds(r0, C), :])
        return _dot(tri, g_hi) + _dot(tri, g_lo)

    def factors(j, r0, b):
        q = q_ref[j, pl.ds(r0, C), :].astype(F32)
        k = k_ref[j, pl.ds(r0, C), :].astype(F32)
        b_last = b[C - 1:C, :]
        mid = jnp.concatenate([rows_of(b, i * SUB + SUB // 2 - 1) for i in range(S)], axis=0)
        beta = jnp.concatenate([jnp.zeros((SUB, W), F32)] + [rows_of(b, i * SUB - 1) for i in range(1, S)], axis=0)
        k_off = [None]
        for i in range(1, S):
            e = jnp.exp(jnp.minimum(b[i * SUB - 1:i * SUB, :] - b, 0.0))
            k_off.append(jnp.where(krow < i * SUB, k * e, 0.0).astype(BF16))
        return dict(q_inter=(q * jnp.exp(b)).astype(BF16), k_state=(k * jnp.exp(b_last - b)).astype(BF16),
                    q_diag=(q * jnp.exp(b - mid)).astype(BF16), k_diag=(k * jnp.exp(mid - b)).astype(BF16),
                    q_off=(q * jnp.exp(b - beta)).astype(BF16), k_off=k_off, dec=jnp.exp(b_last))

    def intra(f, h):
        a_diag = _dot_nt(f["q_diag"][:, ksl(h)], f["k_diag"][:, ksl(h)])
        if S == 1:
            return jnp.where(diag_mask, a_diag, 0.0).astype(BF16)
        blocks = [jnp.zeros((SUB, C), F32)]
        for i in range(1, S):
            blocks.append(_dot_nt(f["q_off"][i * SUB:(i + 1) * SUB, ksl(h)], f["k_off"][i][:, ksl(h)]))
        return jnp.where(diag_mask, a_diag, jnp.concatenate(blocks, axis=0)).astype(BF16)

    def read_out(j, r0, f, a):
        v = v_ref[j, pl.ds(r0, C), :]
        res = []
        for h in range(GLA_HEADS):
            o_h = _dot(a[h], v[:, vsl(h)]) + _dot(f["q_inter"][:, ksl(h)], s_scr[j, h].astype(BF16))
            res.append((o_h, _dot_tn(f["k_state"][:, ksl(h)], v[:, vsl(h)])))
        return res

    def finish(j, r0, f, res):
        outs = []
        for h, (o_h, s_inc) in enumerate(res):
            dec = jnp.broadcast_to(f["dec"][:, ksl(h)], (GLA_DK, GLA_DK))
            dec_col = jnp.sum(jnp.where(eye, dec, 0.0), axis=-1, keepdims=True)
            s_scr[j, h] = dec_col * s_scr[j, h] + s_inc
            ms = jnp.mean(o_h * o_h, axis=-1, keepdims=True)
            outs.append(o_h * lax.rsqrt(ms + NORM_EPS) * gn_ref[...])
        o = jnp.concatenate(outs, axis=-1) * rb_ref[j, pl.ds(r0, C), :].astype(F32)
        o_ref[j, pl.ds(r0, C), :] = o.astype(BF16)

    def step(c):
        r0 = c * C if isinstance(c, int) else pl.multiple_of(c * C, C)
        bs = [decays(j, r0) for j in range(nb)]
        fs = [factors(j, r0, b) for j, b in enumerate(bs)]
        attn = [[intra(f, h) for h in range(GLA_HEADS)] for f in fs]
        res = [read_out(j, r0, fs[j], attn[j]) for j in range(nb)]
        for j in range(nb):
            finish(j, r0, fs[j], res[j])

    if length == C:
        step(0)
    else:
        lax.fori_loop(0, length // C, lambda c, carry: (step(c), carry)[1], 0)
    sout_ref[...] = s_scr[...]


def _gla(qb, kb, vb, la, rb, gn, s0):
    B, L, _ = qb.shape
    chunk = GLA_CHUNK if L % GLA_CHUNK == 0 else L
    sub = GLA_SUB if chunk % GLA_SUB == 0 else chunk
    nb = min(B, SAMPLE_BATCH) if L == chunk else min(B, GLA_LONG_BATCH)
    assert B % nb == 0
    seq = lambda w: pl.BlockSpec((nb, L, w), lambda b: (b, 0, 0))
    st = pl.BlockSpec((nb, GLA_HEADS, GLA_DK, GLA_DV), lambda b: (b, 0, 0, 0))
    return pl.pallas_call(
        functools.partial(_gla_kernel, chunk=chunk, sub=sub, length=L),
        grid=(B // nb,),
        in_specs=[seq(GLA_K_WIDTH), seq(GLA_K_WIDTH), seq(B_WIDTH), seq(GLA_K_WIDTH), seq(B_WIDTH),
                  pl.BlockSpec((1, GLA_DV), lambda b: (0, 0)), st],
        out_specs=[seq(B_WIDTH), st],
        out_shape=[jax.ShapeDtypeStruct((B, L, B_WIDTH), BF16),
                   jax.ShapeDtypeStruct((B, GLA_HEADS, GLA_DK, GLA_DV), F32)],
        scratch_shapes=[pltpu.VMEM((nb, GLA_HEADS, GLA_DK, GLA_DV), F32)],
        compiler_params=_cparams(("parallel",)),
        name="gla",
    )(qb, kb, vb, la, rb, gn, s0)


def _outproj_kernel(x_ref, oa_ref, ob_ref, ga_ref, gb_ref, wa_ref, wb_ref, wo_ref, n2_ref, wr_ref, br_ref, upper_ref,
                    lower_ref, h_ref, hn_ref, g_ref, lp_ref, cnt_ref):
    tm = x_ref.shape[0]
    merged = (ga_ref[...].astype(F32) * _dot(oa_ref[...], wa_ref[...])
              + gb_ref[...].astype(F32) * _dot(ob_ref[...], wb_ref[...]))
    h = x_ref[...] + _dot(merged.astype(BF16), wo_ref[...])
    h_ref[...] = h
    ms = jnp.mean(h * h, axis=-1, keepdims=True)
    hn = h * lax.rsqrt(ms + NORM_EPS) * n2_ref[...]
    hn_ref[...] = hn.astype(BF16)
    hn_hi, hn_lo = _split_bf16(hn)
    wr_hi, wr_lo = _split_bf16(wr_ref[...])
    logits = _dot(hn_hi, wr_hi) + _dot(hn_lo, wr_hi) + _dot(hn_hi, wr_lo) + br_ref[...]
    lt = logits.T[:N_EXPERTS, :]
    eid = lax.broadcasted_iota(I32, (N_EXPERTS, tm), 0)
    vals, hots = [], []
    for k in range(TOP_K):
        m = jnp.max(lt, axis=0, keepdims=True)
        idx = jnp.min(jnp.where(lt == m, eid, N_EXPERTS), axis=0, keepdims=True)
        hot = eid == idx
        lt = jnp.where(hot, -jnp.inf, lt)
        vals.append(m)
        hots.append(hot)
    ex = [jnp.exp(vk - vals[0]) for vk in vals]
    den = ex[0] + ex[1] + ex[2] + ex[3]
    for k in range(TOP_K):
        g_ref[k:k + 1, :] = ex[k] / den
    multi = (hots[0] | hots[1] | hots[2] | hots[3]).astype(BF16)
    before = _dot(multi, upper_ref[...])
    counts = jnp.sum(multi.astype(F32), axis=1, keepdims=True)
    aligned = jnp.floor((counts + (RUN_ALIGN - 1)) * (1.0 / RUN_ALIGN)) * RUN_ALIGN
    run_start = _dot(lower_ref[...], jnp.broadcast_to(aligned, (N_EXPERTS, LANE)).astype(BF16))[:, :1]
    place = before + run_start
    for k in range(TOP_K):
        lp_ref[k:k + 1, :] = jnp.sum(jnp.where(hots[k], place, 0.0), axis=0, keepdims=True).astype(I32)
    cnt_ref[...] = jnp.broadcast_to(aligned, cnt_ref.shape).astype(I32)


def _outproj(x2, oa, ob, ga, gb, wa, wb, wo, n2, wr, br, upper, lower):
    T = x2.shape[0]
    tm = TOK_TILE
    assert T % tm == 0 and upper.shape == (tm, tm)
    tok = lambda w: pl.BlockSpec((tm, w), lambda i: (i, 0))
    const = lambda shape: pl.BlockSpec(shape, lambda i: (0, 0))
    kt = pl.BlockSpec((TOP_K, tm), lambda i: (0, i))
    return pl.pallas_call(
        _outproj_kernel,
        grid=(T // tm,),
        in_specs=[tok(D_MODEL), tok(A_WIDTH), tok(B_WIDTH), tok(D_MODEL), tok(D_MODEL),
                  const((A_WIDTH, D_MODEL)), const((B_WIDTH, D_MODEL)), const((D_MODEL, D_MODEL)),
                  const((1, D_MODEL)), const((D_MODEL, LANE)), const((1, LANE)), const((tm, tm)),
                  const((N_EXPERTS, N_EXPERTS))],
        out_specs=[tok(D_MODEL), tok(D_MODEL), kt, kt, pl.BlockSpec((N_EXPERTS, LANE), lambda i: (i, 0))],
        out_shape=[jax.ShapeDtypeStruct((T, D_MODEL), F32), jax.ShapeDtypeStruct((T, D_MODEL), BF16),
                   jax.ShapeDtypeStruct((TOP_K, T), F32), jax.ShapeDtypeStruct((TOP_K, T), I32),
                   jax.ShapeDtypeStruct((T // tm * N_EXPERTS, LANE), I32)],
        compiler_params=_cparams(("parallel",)),
        name="outproj",
    )(x2, oa, ob, ga, gb, wa, wb, wo, n2, wr, br, upper, lower)


def _pack_rows(v, *, is_bf16_valued=False):
    if not is_bf16_valued:
        v = v.astype(BF16).astype(F32)
    lo = pltpu.bitcast(v[:, :ROW_WORDS], U32) >> 16
    hi = pltpu.bitcast(v[:, ROW_WORDS:], U32) & jnp.uint32(0xFFFF0000)
    return lo | hi


def _unpack_rows(w):
    lo = pltpu.bitcast(w << 16, F32).astype(BF16)
    hi = pltpu.bitcast(w & jnp.uint32(0xFFFF0000), F32).astype(BF16)
    return lo, hi


def _start_runs(runs, tile, tile_rows, global_rows, sem, *, to_global):
    cnt_ref, off_ref, dest_ref, _ = runs

    def run(e):
        j = tile * N_EXPERTS + e
        n = pl.multiple_of(cnt_ref[j], RUN_ALIGN)
        local = tile_rows.at[pl.ds(pl.multiple_of(off_ref[j], RUN_ALIGN), n), :]
        far = global_rows.at[pl.ds(pl.multiple_of(dest_ref[j], RUN_ALIGN), n), :]
        return pltpu.make_async_copy(local, far, sem) if to_global else pltpu.make_async_copy(far, local, sem)

    for e in range(N_EXPERTS):
        pl.when(cnt_ref[tile * N_EXPERTS + e] > 0)(lambda e=e: run(e).start(priority=ROW_DMA_PRIORITY))


def _wait_runs(runs, tile, tile_rows, global_rows, sem):
    total = pl.multiple_of(runs[3][tile], RUN_ALIGN)
    pltpu.make_async_copy(global_rows.at[pl.ds(0, total), :], tile_rows.at[pl.ds(0, total), :], sem).wait()


def _dispatch_kernel(fill_ref, cnt_ref, off_ref, dest_ref, tot_ref, lp_ref, hn_p_ref, hn_s_ref, xs_ref,
                     rows2, sems, zbuf, zsem, *, steps_p):
    i = pl.program_id(0)
    runs = (cnt_ref, off_ref, dest_ref, tot_ref)
    slot = i % 2
    rows = rows2.at[slot]

    @pl.when(i == 0)
    def _():
        zbuf[...] = jnp.zeros_like(zbuf)

        def fill_copy(j):
            return pltpu.make_async_copy(zbuf, xs_ref.at[pl.ds(pl.multiple_of(fill_ref[j], MOE_BM), MOE_BM), :], zsem)

        for j in range(fill_ref.shape[0]):
            pl.when(fill_ref[j] >= 0)(lambda j=j: fill_copy(j).start())
        for j in range(fill_ref.shape[0]):
            pl.when(fill_ref[j] >= 0)(lambda j=j: fill_copy(j).wait())

    def sort_tile(hn_ref):
        hn = hn_ref[...]
        lp = lp_ref[...]
        for c in range(SORT_ROWS // SORT_CHUNK):
            r = lax.broadcasted_iota(I32, (SORT_CHUNK, TOK_TILE), 0) + c * SORT_CHUNK
            hit = (r == lp[0:1, :]) | (r == lp[1:2, :]) | (r == lp[2:3, :]) | (r == lp[3:4, :])
            perm = jnp.where(hit, 1.0, 0.0).astype(BF16)
            rows[c * SORT_CHUNK:(c + 1) * SORT_CHUNK, :] = _pack_rows(_dot(perm, hn), is_bf16_valued=True)

    pl.when(i < steps_p)(lambda: sort_tile(hn_p_ref))
    pl.when(i >= steps_p)(lambda: sort_tile(hn_s_ref))
    _start_runs(runs, i, rows, xs_ref, sems.at[slot], to_global=True)
    pl.when(i > 0)(lambda: _wait_runs(runs, i - 1, rows2.at[1 - slot], xs_ref, sems.at[1 - slot]))
    pl.when(i == pl.num_programs(0) - 1)(lambda: _wait_runs(runs, i, rows, xs_ref, sems.at[slot]))


def _dispatch(fill, runs, lp, hn_p, hn_s, n_rows):
    tm = TOK_TILE
    steps_p, steps_s = hn_p.shape[0] // tm, hn_s.shape[0] // tm
    return pl.pallas_call(
        functools.partial(_dispatch_kernel, steps_p=steps_p),
        grid_spec=pltpu.PrefetchScalarGridSpec(
            num_scalar_prefetch=5,
            grid=(steps_p + steps_s,),
            in_specs=[pl.BlockSpec((TOP_K, tm), lambda i, *_: (0, i)),
                      pl.BlockSpec((tm, D_MODEL), lambda i, *_: (jnp.minimum(i, steps_p - 1), 0)),
                      pl.BlockSpec((tm, D_MODEL), lambda i, *_: (jnp.maximum(i - steps_p, 0), 0))],
            out_specs=pl.BlockSpec(memory_space=pl.ANY),
            scratch_shapes=[pltpu.VMEM((2, SORT_ROWS, ROW_WORDS), U32), pltpu.SemaphoreType.DMA((2,)),
                            pltpu.VMEM((MOE_BM, ROW_WORDS), U32), pltpu.SemaphoreType.DMA],
        ),
        out_shape=jax.ShapeDtypeStruct((n_rows, ROW_WORDS), U32),
        compiler_params=_cparams(("arbitrary",)),
        name="dispatch",
    )(fill, *runs, lp, hn_p, hn_s)


def _experts_kernel(be_ref, rows_ref, nu_ref, xs_ref, w1_ref, b1g_ref, b1l_ref, w2_ref, b2_ref, sel_ref, y_ref,
                    w1g_scr, w1l_scr, w2_scr):
    del nu_ref
    i = pl.program_id(0)
    used = rows_ref[i]

    @pl.when((used > 0) & ((i == 0) | (be_ref[i] != be_ref[jnp.maximum(i - 1, 0)])))
    def _():
        for c in range(D_FF // LANE):
            cols = _dot(w1_ref[0, :, c * 2 * LANE:(c + 1) * 2 * LANE].astype(BF16), sel_ref[...])
            w1g_scr[:, c * LANE:(c + 1) * LANE] = cols[:, :LANE].astype(BF16)
            w1l_scr[:, c * LANE:(c + 1) * LANE] = cols[:, LANE:].astype(BF16)
        w2_scr[...] = w2_ref[0].astype(BF16)

    def mlp(words):
        x_lo, x_hi = _unpack_rows(words)

        def up(w_scr, b_ref):
            return _dot(x_lo, w_scr[:ROW_WORDS, :]) + _dot(x_hi, w_scr[ROW_WORDS:, :]) + b_ref[0]

        x_glu = jnp.minimum(up(w1g_scr, b1g_ref), SWIGLU_LIMIT)
        x_lin = jnp.clip(up(w1l_scr, b1l_ref), -SWIGLU_LIMIT, SWIGLU_LIMIT)
        act = x_glu * jax.nn.sigmoid(SWIGLU_ALPHA * x_glu) * (x_lin + 1.0)
        return _pack_rows(_dot(act.astype(BF16), w2_scr[...]) + b2_ref[0])

    half = MOE_BM // 2

    @pl.when(used > half)
    def _():
        y_ref[...] = mlp(xs_ref[...])

    @pl.when((used > 0) & (used <= half))
    def _():
        y_ref[:half, :] = mlp(xs_ref[:half, :])
        y_ref[half:, :] = jnp.zeros((half, ROW_WORDS), U32)

    @pl.when(used == 0)
    def _():
        y_ref[...] = jnp.zeros_like(y_ref)


def _experts(blk_e, blk_rows, n_used, xs, w1, b1g, b1l, w2, b2, sel):
    nblk = xs.shape[0] // MOE_BM
    wspec = lambda r, c: pl.BlockSpec((1, r, c), lambda i, be, rows, nu: (be[i], 0, 0))
    return pl.pallas_call(
        _experts_kernel,
        grid_spec=pltpu.PrefetchScalarGridSpec(
            num_scalar_prefetch=3,
            grid=(nblk,),
            in_specs=[pl.BlockSpec((MOE_BM, ROW_WORDS), lambda i, be, rows, nu: (jnp.minimum(i, nu[0] - 1), 0)),
                      wspec(D_MODEL, 2 * D_FF), wspec(1, D_FF), wspec(1, D_FF),
                      wspec(D_FF, D_MODEL), wspec(1, D_MODEL),
                      pl.BlockSpec((2 * LANE, 2 * LANE), lambda i, be, rows, nu: (0, 0))],
            out_specs=pl.BlockSpec((MOE_BM, ROW_WORDS), lambda i, be, rows, nu: (i, 0)),
            scratch_shapes=[pltpu.VMEM((D_MODEL, D_FF), BF16), pltpu.VMEM((D_MODEL, D_FF), BF16),
                            pltpu.VMEM((D_FF, D_MODEL), BF16)],
        ),
        out_shape=jax.ShapeDtypeStruct(xs.shape, U32),
        compiler_params=_cparams(("arbitrary",)),
        name="experts",
    )(blk_e, blk_rows, n_used, xs, w1, b1g, b1l, w2, b2, sel)


def _combine_kernel(cnt_ref, off_ref, dest_ref, tot_ref, lp_ref, g_ref, h_ref, yb_ref, y_ref, rows2, sems, *, tile0):
    i = pl.program_id(0)
    runs = (cnt_ref, off_ref, dest_ref, tot_ref)
    slot = i % 2
    rows = rows2.at[slot]

    def fetch(step, into):
        _start_runs(runs, step + tile0, rows2.at[into], yb_ref, sems.at[into], to_global=False)

    @pl.when(i == 0)
    def _():
        rows2[...] = jnp.zeros_like(rows2)
        fetch(0, 0)

    pl.when(i + 1 < pl.num_programs(0))(lambda: fetch(i + 1, 1 - slot))
    _wait_runs(runs, i + tile0, rows, yb_ref, sems.at[slot])
    lp, g = lp_ref[...], g_ref[...]
    r = lax.broadcasted_iota(I32, (TOK_TILE, SORT_ROWS), 1)
    mix = jnp.zeros((TOK_TILE, SORT_ROWS), F32)
    for k in range(TOP_K):
        mix = jnp.where(r == lp[:, k:k + 1], g[:, k:k + 1], mix)
    mix = mix.astype(BF16)
    y_lo, y_hi = _unpack_rows(rows[...])
    y_ref[...] = h_ref[...] + jnp.concatenate([_dot(mix, y_lo), _dot(mix, y_hi)], axis=-1)


def _combine(runs, lp_t, gate_t, h, yb, tile0):
    T = h.shape[0]
    tm = TOK_TILE
    tok = lambda w: pl.BlockSpec((tm, w), lambda i, *_: (i, 0))
    return pl.pallas_call(
        functools.partial(_combine_kernel, tile0=tile0),
        grid_spec=pltpu.PrefetchScalarGridSpec(
            num_scalar_prefetch=4,
            grid=(T // tm,),
            in_specs=[tok(TOP_K), tok(TOP_K), tok(D_MODEL), pl.BlockSpec(memory_space=pl.ANY)],
            out_specs=tok(D_MODEL),
            scratch_shapes=[pltpu.VMEM((2, SORT_ROWS, ROW_WORDS), U32), pltpu.SemaphoreType.DMA((2,))],
        ),
        out_shape=jax.ShapeDtypeStruct((T, D_MODEL), F32),
        compiler_params=_cparams(("arbitrary",)),
        name="combine",
    )(*runs, lp_t, gate_t, h, yb)


def _block_ones(width, block):
    idx = np.arange(width) // block
    return jnp.asarray(idx[:, None] == idx[None, :], BF16)


def _layer(x_p, x_s, cache_k, cache_v, s_gla, norm1, w_in, q_norm, k_norm, sinks, w_gk, b_gk,
           gla_norm, w_a, w_b, w_o, norm2, w_router, b_router, w1, b1, w2, b2):
    Bp, Lp, _ = x_p.shape
    Bs, Ls, _ = x_s.shape
    Tp, Ts = Bp * Lp, Bs * Ls

    offs = np.cumsum((0, A_WIDTH, KV_WIDTH, KV_WIDTH, GLA_K_WIDTH, GLA_K_WIDTH, B_WIDTH, GLA_GATE_RANK, B_WIDTH,
                      D_MODEL, D_MODEL))
    src = dict(zip(("qa", "ka", "va", "qb", "kb", "vb", "gl", "rb", "ga", "gb"), zip(offs[:-1], offs[1:])))
    cols = []
    for name, width in _SEGS:
        a, b = src[name]
        piece = w_in[:, a:b]
        if b - a < width:
            piece = jnp.pad(piece, ((0, 0), (0, width - (b - a))))
        cols.append(piece)
    w_packed = jnp.concatenate(cols, axis=1).astype(BF16)
    wgk = jnp.pad(w_gk, ((0, GL_PAD - GLA_GATE_RANK), (0, 0))).astype(BF16)
    n1 = norm1.reshape(1, D_MODEL)
    gq, gk = _block_ones(A_WIDTH, SWA_HEAD_DIM), _block_ones(KV_WIDTH, SWA_HEAD_DIM)
    qn = jnp.tile(q_norm, SWA_HEADS).reshape(1, A_WIDTH)
    kn = jnp.tile(k_norm, SWA_KV_HEADS).reshape(1, KV_WIDTH)
    bgk = b_gk.reshape(1, GLA_K_WIDTH)
    gn = gla_norm.reshape(1, GLA_DV)
    wa, wb, wo = w_a.astype(BF16), w_b.astype(BF16), w_o.astype(BF16)
    n2 = norm2.reshape(1, D_MODEL)
    wr = jnp.pad(w_router, ((0, 0), (0, LANE - N_EXPERTS)))
    br = jnp.pad(b_router, (0, LANE - N_EXPERTS)).reshape(1, LANE)
    b1g = b1[:, 0::2].reshape(N_EXPERTS, 1, D_FF)
    b1l = b1[:, 1::2].reshape(N_EXPERTS, 1, D_FF)
    b2r = b2.reshape(N_EXPERTS, 1, D_MODEL)
    sel_np = np.zeros((2 * LANE, 2 * LANE), np.float32)
    sel_np[2 * np.arange(LANE), np.arange(LANE)] = 1.0
    sel_np[2 * np.arange(LANE) + 1, LANE + np.arange(LANE)] = 1.0
    sel = jnp.asarray(sel_np, BF16)

    def mix(x, k_past, v_past, s0):
        B, L, _ = x.shape
        T = B * L
        x2 = x.reshape(T, D_MODEL)
        qa, ka, va, qb, kb, vb, la, rb, ga, gb = _inproj(x2, n1, w_packed, gq, gk, qn, kn, wgk, bgk)
        r3 = lambda t: t.reshape(B, L, t.shape[-1])
        oa = _swa(sinks, r3(qa), r3(ka), r3(va), k_past, v_past, has_past=k_past is not None)
        ob, s_out = _gla(r3(qb), r3(kb), r3(vb), r3(la), r3(rb), gn, s0)
        h, hn, gate, lp, cnt = _outproj(x2, oa.reshape(T, A_WIDTH), ob.reshape(T, B_WIDTH), ga, gb,
                                        wa, wb, wo, n2, wr, br, upper, lower)
        return r3(ka), r3(va), s_out, h, hn, gate, lp, cnt[:, 0].reshape(T // TOK_TILE, N_EXPERTS)

    upper = jnp.asarray(np.arange(TOK_TILE)[:, None] < np.arange(TOK_TILE)[None, :], BF16)
    lower = jnp.asarray(np.arange(N_EXPERTS)[:, None] > np.arange(N_EXPERTS)[None, :], BF16)
    win = cache_k.shape[1]
    assert win == WINDOW
    ck = cache_k.reshape(Bs, win, KV_WIDTH)
    cv = cache_v.reshape(Bs, win, KV_WIDTH)
    ka_p, va_p, s_p, h_p, hn_p, g_p, lp_p, cnt_p = mix(x_p, None, None, jnp.zeros((Bp, GLA_HEADS, GLA_DK, GLA_DV), F32))
    ka_s, va_s, s_s, h_s, hn_s, g_s, lp_s, cnt_s = mix(x_s, ck, cv, s_gla)

    T = Tp + Ts
    cnt = jnp.concatenate([cnt_p, cnt_s], axis=0)
    most_rows = T * TOP_K + cnt.size * (RUN_ALIGN - 1) + N_EXPERTS * (MOE_BM - 1)
    n_rows = -(-most_rows // MOE_BM) * MOE_BM
    counts = jnp.sum(cnt, axis=0)
    padded = (counts + MOE_BM - 1) // MOE_BM * MOE_BM
    pad_end = jnp.cumsum(padded)
    start = pad_end - padded
    dest = start[None, :] + jnp.cumsum(cnt, axis=0) - cnt
    off = jnp.cumsum(cnt, axis=1) - cnt
    flat = lambda a: a.reshape(-1).astype(I32)
    runs = (flat(cnt), flat(off), flat(dest), flat(jnp.sum(cnt, axis=1)))
    tails = jnp.where(padded > 0, pad_end - MOE_BM, -1)
    most_spare = (n_rows - T * TOP_K) // MOE_BM
    spare = pad_end[-1] + jnp.arange(most_spare, dtype=I32) * MOE_BM
    fill = jnp.concatenate([tails, jnp.where(spare < n_rows, spare, -1)]).astype(I32)
    blk_row = jnp.arange(n_rows // MOE_BM, dtype=I32) * MOE_BM
    blk_e = jnp.minimum(jnp.sum(pad_end[None, :] <= blk_row[:, None], axis=1), N_EXPERTS - 1).astype(I32)
    used_end = start + counts
    blk_rows = jnp.clip(jnp.sum(jnp.where(blk_e[:, None] == jnp.arange(N_EXPERTS)[None, :], used_end[None, :], 0),
                                axis=1) - blk_row, 0, MOE_BM)
    blk_rows = jnp.where(blk_row < pad_end[-1], blk_rows, 0).astype(I32)
    n_used = (pad_end[-1:] // MOE_BM).astype(I32)

    xs = _dispatch(fill, runs, jnp.concatenate([lp_p, lp_s], axis=1), hn_p, hn_s, n_rows)
    yb = _experts(blk_e, blk_rows, n_used, xs, w1, b1g, b1l, w2, b2r, sel)
    y_p = _combine(runs, lp_p.T, g_p.T, h_p, yb, 0).reshape(Bp, Lp, D_MODEL)
    y_s = _combine(runs, lp_s.T, g_s.T, h_s, yb, Tp // TOK_TILE).reshape(Bs, Ls, D_MODEL)

    kv = lambda t: t.reshape(t.shape[0], t.shape[1], SWA_KV_HEADS, SWA_HEAD_DIM)
    k_p, v_p = kv(ka_p[:, -WINDOW:]), kv(va_p[:, -WINDOW:])
    k_s = jnp.concatenate([cache_k, kv(ka_s)], axis=1)[:, -win:]
    v_s = jnp.concatenate([cache_v, kv(va_s)], axis=1)[:, -win:]
    return y_p, y_s, (k_p, v_p, s_p, k_s, v_s, s_s)


def kernel(x_prompt, x_sample, cache_swa_k, cache_swa_v, state_gla, norm1, w_in, q_norm, k_norm, sinks, w_gk, b_gk,
           gla_norm, w_a, w_b, w_o, norm2, w_router, b_router, w1, b1, w2, b2):
    depth = norm1.shape[0]
    y_p, y_s = x_prompt, x_sample
    states = []
    for l in range(depth):
        y_p, y_s, st = _layer(y_p, y_s, cache_swa_k[l], cache_swa_v[l], state_gla[l], norm1[l], w_in[l], q_norm[l],
                              k_norm[l], sinks[l], w_gk[l], b_gk[l], gla_norm[l], w_a[l], w_b[l], w_o[l], norm2[l],
                              w_router[l], b_router[l], w1[l], b1[l], w2[l], b2[l])
        states.append(st)
    return (y_p, y_s) + tuple(jnp.stack([st[j] for st in states]) for j in range(6))
```

```python
import functools

import jax
import jax.numpy as jnp
import numpy as np
from jax import lax
from jax.experimental import pallas as pl
from jax.experimental.pallas import tpu as pltpu

F32 = jnp.float32
BF16 = jnp.bfloat16
I32 = jnp.int32
U32 = jnp.uint32

D_MODEL = 1024
SWA_HEADS = 8
SWA_KV_HEADS = 2
SWA_GROUP = SWA_HEADS // SWA_KV_HEADS
SWA_HEAD_DIM = 64
WINDOW = 128
ATTN_SCALE = SWA_HEAD_DIM ** -0.5
GLA_HEADS = 4
GLA_DK = 64
GLA_DV = 128
GLA_GATE_RANK = 16
GLA_GATE_NORM = 16.0
GLA_SCALE = GLA_DK ** -0.5
A_WIDTH = SWA_HEADS * SWA_HEAD_DIM
KV_WIDTH = SWA_KV_HEADS * SWA_HEAD_DIM
GLA_K_WIDTH = GLA_HEADS * GLA_DK
B_WIDTH = GLA_HEADS * GLA_DV
N_EXPERTS = 32
TOP_K = 4
D_FF = D_MODEL
SWIGLU_ALPHA = 1.702
SWIGLU_LIMIT = 7.0
NORM_EPS = 1e-5
QK_EPS = 1e-6
NEG_INF = -1e30

LANE = 128
GL_PAD = LANE
VMEM_LIMIT = 56 * 1024 * 1024

_SEGS = (("qa", A_WIDTH), ("ka", KV_WIDTH), ("va", KV_WIDTH), ("qb", GLA_K_WIDTH), ("kb", GLA_K_WIDTH),
         ("vb", B_WIDTH), ("rb", B_WIDTH), ("ga", D_MODEL), ("gb", D_MODEL), ("gl", GL_PAD))
_OFF = {}
_o = 0
for _n, _w in _SEGS:
    _OFF[_n] = (_o, _w)
    _o += _w
IN_PACKED = _o

TOK_TILE = 512
GLA_CHUNK = 64
GLA_SUB = 16
MOE_BM = 512
ROW_DMA_PRIORITY = 1
ROW_WORDS = D_MODEL // 2
RUN_ALIGN = 8
SORT_CHUNK = 256
SORT_ROWS = -(-(TOK_TILE * TOP_K + N_EXPERTS * (RUN_ALIGN - 1)) // SORT_CHUNK) * SORT_CHUNK
SAMPLE_BATCH = 8
GLA_LONG_BATCH = 2


def _cparams(sem):
    return pltpu.CompilerParams(dimension_semantics=sem, vmem_limit_bytes=VMEM_LIMIT)


def _split_bf16(v):
    hi = v.astype(BF16)
    lo = (v - hi.astype(F32)).astype(BF16)
    return hi, lo


def _dot(a, b):
    return jnp.dot(a, b, preferred_element_type=F32)


def _dot_nt(a, b):
    return lax.dot_general(a, b, (((1,), (1,)), ((), ())), preferred_element_type=F32)


def _dot_tn(a, b):
    return lax.dot_general(a, b, (((0,), (0,)), ((), ())), preferred_element_type=F32)


_SRC = {}
_o = 0
for _n, _w in (("qa", A_WIDTH), ("ka", KV_WIDTH), ("va", KV_WIDTH), ("qb", GLA_K_WIDTH), ("kb", GLA_K_WIDTH),
               ("vb", B_WIDTH), ("gl", GLA_GATE_RANK), ("rb", B_WIDTH), ("ga", D_MODEL), ("gb", D_MODEL)):
    _SRC[_n] = (_o, _w)
    _o += _w
IN_WIDTH = _o
PACK_ROWS = 128


def _pack_w_in_kernel(w_ref, o_ref):
    for name, width in _SEGS:
        src, have = _SRC[name]
        dst, _ = _OFF[name]
        o_ref[:, dst:dst + have] = w_ref[:, src:src + have].astype(BF16)
        if have < width:
            o_ref[:, dst + have:dst + width] = jnp.zeros((o_ref.shape[0], width - have), BF16)


def _pack_w_in(w_in):
    return pl.pallas_call(
        _pack_w_in_kernel,
        grid=(D_MODEL // PACK_ROWS,),
        in_specs=[pl.BlockSpec((PACK_ROWS, IN_WIDTH), lambda i: (i, 0))],
        out_specs=pl.BlockSpec((PACK_ROWS, IN_PACKED), lambda i: (i, 0)),
        out_shape=jax.ShapeDtypeStruct((D_MODEL, IN_PACKED), BF16),
        compiler_params=_cparams(("parallel",)),
        name="pack_w_in",
    )(w_in)


def _inproj_kernel(x_ref, n1_ref, w_ref, gq_ref, gk_ref, qn_ref, kn_ref, wgk_ref, bgk_ref,
                   qa_ref, ka_ref, va_ref, qb_ref, kb_ref, vb_ref, la_ref, rb_ref, ga_ref, gb_ref):
    x = x_ref[...]
    ms = jnp.mean(x * x, axis=-1, keepdims=True)
    xn = (x * lax.rsqrt(ms + NORM_EPS) * n1_ref[...]).astype(BF16)

    def proj(name):
        off, width = _OFF[name]
        return _dot(xn, w_ref[:, off:off + width])

    def head_norm(v, ones_ref, gain_ref):
        hi, lo = _split_bf16(v * v)
        ss = _dot(hi, ones_ref[...]) + _dot(lo, ones_ref[...])
        return v * lax.rsqrt(ss * (1.0 / SWA_HEAD_DIM) + QK_EPS) * gain_ref[...]

    def log_decay(gl):
        z = _dot(gl.astype(BF16), wgk_ref[...]) + bgk_ref[...]
        return (jnp.minimum(z, 0.0) - jnp.log1p(jnp.exp(-jnp.abs(z)))) * (1.0 / GLA_GATE_NORM)

    plan = (("qa", qa_ref, lambda u: head_norm(u, gq_ref, qn_ref) * ATTN_SCALE),
            ("ka", ka_ref, lambda u: head_norm(u, gk_ref, kn_ref)),
            ("va", va_ref, lambda u: u),
            ("qb", qb_ref, lambda u: u * GLA_SCALE),
            ("kb", kb_ref, lambda u: u),
            ("vb", vb_ref, lambda u: u),
            ("gl", la_ref, log_decay),
            ("rb", rb_ref, lambda u: u * jax.nn.sigmoid(u)),
            ("ga", ga_ref, jax.nn.sigmoid),
            ("gb", gb_ref, jax.nn.sigmoid))
    u = proj(plan[0][0])
    for idx, (_, out_ref, epilogue) in enumerate(plan):
        u_next = proj(plan[idx + 1][0]) if idx + 1 < len(plan) else None
        out_ref[...] = epilogue(u).astype(out_ref.dtype)
        u = u_next


def _inproj(x2, n1, w_packed, gq, gk, qn, kn, wgk, bgk):
    T = x2.shape[0]
    tm = min(TOK_TILE, T)
    assert T % tm == 0

    def tok(width):
        return pl.BlockSpec((tm, width), lambda i: (i, 0))

    def const(shape):
        return pl.BlockSpec(shape, lambda i: (0, 0))

    outs = (("qa", A_WIDTH, BF16), ("ka", KV_WIDTH, F32), ("va", KV_WIDTH, F32), ("qb", GLA_K_WIDTH, BF16),
            ("kb", GLA_K_WIDTH, BF16), ("vb", B_WIDTH, BF16), ("la", GLA_K_WIDTH, F32), ("rb", B_WIDTH, BF16),
            ("ga", D_MODEL, BF16), ("gb", D_MODEL, BF16))
    return pl.pallas_call(
        _inproj_kernel,
        grid=(T // tm,),
        in_specs=[tok(D_MODEL), const((1, D_MODEL)), const((D_MODEL, IN_PACKED)), const((A_WIDTH, A_WIDTH)),
                  const((KV_WIDTH, KV_WIDTH)), const((1, A_WIDTH)), const((1, KV_WIDTH)),
                  const((GL_PAD, GLA_K_WIDTH)), const((1, GLA_K_WIDTH))],
        out_specs=[tok(w) for _, w, _ in outs],
        out_shape=[jax.ShapeDtypeStruct((T, w), dt) for _, w, dt in outs],
        compiler_params=_cparams(("parallel",)),
        name="inproj",
    )(x2, n1, w_packed, gq, gk, qn, kn, wgk, bgk)


def _swa_kernel(sink_ref, q_ref, kc_ref, vc_ref, kp_ref, vp_ref, o_ref, *, first_block_has_past):
    n = pl.program_id(1)
    nb, lq = q_ref.shape[0], q_ref.shape[1]
    assert lq & (lq - 1) == 0
    stack = SWA_GROUP if lq < WINDOW else 1
    rows, keys = stack * lq, WINDOW + lq
    qi = lax.broadcasted_iota(I32, (rows, keys), 0) & (lq - 1)
    ci = lax.broadcasted_iota(I32, (rows, keys), 1)
    dist = WINDOW + qi - ci
    mask = (dist >= 0) & (dist < WINDOW)
    if not first_block_has_past:
        mask = mask & ((ci >= WINDOW) | (n > 0))
    cols = lambda h0: slice(h0 // SWA_GROUP * SWA_HEAD_DIM, (h0 // SWA_GROUP + 1) * SWA_HEAD_DIM)
    qs, ks, vs = [], [], []
    for j in range(nb):
        q = q_ref[j]
        qs.append(q.astype(F32) if stack > 1 else q)
        ks.append(jnp.concatenate([kp_ref[j], kc_ref[j]], axis=0).astype(BF16))
        vs.append(jnp.concatenate([vp_ref[j], vc_ref[j]], axis=0).astype(BF16))
    groups = [(j, h0) for j in range(nb) for h0 in range(0, SWA_HEADS, stack)]
    wave = len(groups) if stack > 1 else 1
    outs = {}
    for w0 in range(0, len(groups), wave):
        scores, sinks = [], []
        for j, h0 in groups[w0:w0 + wave]:
            heads = range(h0, h0 + stack)
            qg = jnp.concatenate([qs[j][:, h * SWA_HEAD_DIM:(h + 1) * SWA_HEAD_DIM] for h in heads], axis=0)
            sinks.append(jnp.concatenate([jnp.full((lq, 1), sink_ref[h], F32) for h in heads], axis=0))
            scores.append(_dot_nt(qg.astype(BF16), ks[j][:, cols(h0)]))
        probs, denoms = [], []
        for s, sink in zip(scores, sinks):
            s = jnp.where(mask, s, NEG_INF)
            m = jnp.maximum(jnp.max(s, axis=-1, keepdims=True), sink)
            p = jnp.exp(s - m)
            denoms.append(jnp.sum(p, axis=-1, keepdims=True) + jnp.exp(sink - m))
            probs.append(p.astype(BF16))
        for (j, h0), p, denom in zip(groups[w0:w0 + wave], probs, denoms):
            o = _dot(p, vs[j][:, cols(h0)]) / denom
            outs.setdefault(j, []).extend(o[g * lq:(g + 1) * lq] for g in range(stack))
    for j in range(nb):
        o_ref[j] = jnp.concatenate(outs[j], axis=-1).astype(BF16)


def _swa(sinks, qa, ka, va, k_past, v_past, *, has_past):
    B, L, _ = qa.shape
    lq = min(L, WINDOW)
    nl = L // lq
    if has_past:
        assert nl == 1
        nb = min(B, SAMPLE_BATCH)
        past_spec = pl.BlockSpec((nb, WINDOW, KV_WIDTH), lambda b, n, s: (b, 0, 0))
        kp, vp = k_past, v_past
    else:
        assert lq == WINDOW
        nb = 1
        past_spec = pl.BlockSpec((nb, WINDOW, KV_WIDTH), lambda b, n, s: (b, jnp.maximum(n - 1, 0), 0))
        kp, vp = ka, va
    assert B % nb == 0
    cur = lambda w: pl.BlockSpec((nb, lq, w), lambda b, n, s: (b, n, 0))
    return pl.pallas_call(
        functools.partial(_swa_kernel, first_block_has_past=has_past),
        grid_spec=pltpu.PrefetchScalarGridSpec(
            num_scalar_prefetch=1,
            grid=(B // nb, nl),
            in_specs=[cur(A_WIDTH), cur(KV_WIDTH), cur(KV_WIDTH), past_spec, past_spec],
            out_specs=cur(A_WIDTH),
        ),
        out_shape=jax.ShapeDtypeStruct((B, L, A_WIDTH), BF16),
        compiler_params=_cparams(("parallel", "parallel")),
        name="swa",
    )(sinks, qa, ka, va, kp, vp)


def _gla_kernel(q_ref, k_ref, v_ref, la_ref, rb_ref, gn_ref, s0_ref, o_ref, sout_ref, s_scr, *, chunk, sub, length):
    C, SUB = chunk, sub
    S = C // SUB
    W = GLA_K_WIDTH
    nb = q_ref.shape[0]
    s_scr[...] = s0_ref[...]
    row = lax.broadcasted_iota(I32, (C, C), 0)
    col = lax.broadcasted_iota(I32, (C, C), 1)
    tri = (row >= col).astype(BF16)
    diag_mask = (row >= col) & ((row // SUB) == (col // SUB))
    eye = lax.broadcasted_iota(I32, (GLA_DK, GLA_DK), 0) == lax.broadcasted_iota(I32, (GLA_DK, GLA_DK), 1)
    krow = lax.broadcasted_iota(I32, (C, W), 0)

    def rows_of(x, r):
        return jnp.broadcast_to(x[r:r + 1, :], (SUB, W))

    ksl = lambda h: slice(h * GLA_DK, (h + 1) * GLA_DK)
    vsl = lambda h: slice(h * GLA_DV, (h + 1) * GLA_DV)

    def decays(j, r0):
        g_hi, g_lo = _split_bf16(la_ref[j, pl.ds(r0, C), :])
        return _dot(tri, g_hi) + _dot(tri, g_lo)

    def factors(j, r0, b):
        q = q_ref[j, pl.ds(r0, C), :].astype(F32)
        k = k_ref[j, pl.ds(r0, C), :].astype(F32)
        b_last = b[C - 1:C, :]
        mid = jnp.concatenate([rows_of(b, i * SUB + SUB // 2 - 1) for i in range(S)], axis=0)
        beta = jnp.concatenate([jnp.zeros((SUB, W), F32)] + [rows_of(b, i * SUB - 1) for i in range(1, S)], axis=0)
        k_off = [None]
        for i in range(1, S):
            e = jnp.exp(jnp.minimum(b[i * SUB - 1:i * SUB, :] - b, 0.0))
            k_off.append(jnp.where(krow < i * SUB, k * e, 0.0).astype(BF16))
        return dict(q_inter=(q * jnp.exp(b)).astype(BF16), k_state=(k * jnp.exp(b_last - b)).astype(BF16),
                    q_diag=(q * jnp.exp(b - mid)).astype(BF16), k_diag=(k * jnp.exp(mid - b)).astype(BF16),
                    q_off=(q * jnp.exp(b - beta)).astype(BF16), k_off=k_off, dec=jnp.exp(b_last))

    def intra(f, h):
        a_diag = _dot_nt(f["q_diag"][:, ksl(h)], f["k_diag"][:, ksl(h)])
        if S == 1:
            return jnp.where(diag_mask, a_diag, 0.0).astype(BF16)
        blocks = [jnp.zeros((SUB, C), F32)]
        for i in range(1, S):
            blocks.append(_dot_nt(f["q_off"][i * SUB:(i + 1) * SUB, ksl(h)], f["k_off"][i][:, ksl(h)]))
        return jnp.where(diag_mask, a_diag, jnp.concatenate(blocks, axis=0)).astype(BF16)

    def read_out(j, r0, f, a):
        v = v_ref[j, pl.ds(r0, C), :]
        res = []
        for h in range(GLA_HEADS):
            o_h = _dot(a[h], v[:, vsl(h)]) + _dot(f["q_inter"][:, ksl(h)], s_scr[j, h].astype(BF16))
            res.append((o_h, _dot_tn(f["k_state"][:, ksl(h)], v[:, vsl(h)])))
        return res

    def finish(j, r0, f, res):
        outs = []
        for h, (o_h, s_inc) in enumerate(res):
            dec = jnp.broadcast_to(f["dec"][:, ksl(h)], (GLA_DK, GLA_DK))
            dec_col = jnp.sum(jnp.where(eye, dec, 0.0), axis=-1, keepdims=True)
            s_scr[j, h] = dec_col * s_scr[j, h] + s_inc
            ms = jnp.mean(o_h * o_h, axis=-1, keepdims=True)
            outs.append(o_h * lax.rsqrt(ms + NORM_EPS) * gn_ref[...])
        o = jnp.concatenate(outs, axis=-1) * rb_ref[j, pl.ds(r0, C), :].astype(F32)
        o_ref[j, pl.ds(r0, C), :] = o.astype(BF16)

    def step(c):
        r0 = c * C if isinstance(c, int) else pl.multiple_of(c * C, C)
        bs = [decays(j, r0) for j in range(nb)]
        fs = [factors(j, r0, b) for j, b in enumerate(bs)]
        attn = [[intra(f, h) for h in range(GLA_HEADS)] for f in fs]
        res = [read_out(j, r0, fs[j], attn[j]) for j in range(nb)]
        for j in range(nb):
            finish(j, r0, fs[j], res[j])

    if length == C:
        step(0)
    else:
        lax.fori_loop(0, length // C, lambda c, carry: (step(c), carry)[1], 0)
    sout_ref[...] = s_scr[...]


def _gla(qb, kb, vb, la, rb, gn, s0):
    B, L, _ = qb.shape
    chunk = GLA_CHUNK if L % GLA_CHUNK == 0 else L
    sub = GLA_SUB if chunk % GLA_SUB == 0 else chunk
    nb = min(B, SAMPLE_BATCH) if L == chunk else min(B, GLA_LONG_BATCH)
    assert B % nb == 0
    seq = lambda w: pl.BlockSpec((nb, L, w), lambda b: (b, 0, 0))
    st = pl.BlockSpec((nb, GLA_HEADS, GLA_DK, GLA_DV), lambda b: (b, 0, 0, 0))
    return pl.pallas_call(
        functools.partial(_gla_kernel, chunk=chunk, sub=sub, length=L),
        grid=(B // nb,),
        in_specs=[seq(GLA_K_WIDTH), seq(GLA_K_WIDTH), seq(B_WIDTH), seq(GLA_K_WIDTH), seq(B_WIDTH),
                  pl.BlockSpec((1, GLA_DV), lambda b: (0, 0)), st],
        out_specs=[seq(B_WIDTH), st],
        out_shape=[jax.ShapeDtypeStruct((B, L, B_WIDTH), BF16),
                   jax.ShapeDtypeStruct((B, GLA_HEADS, GLA_DK, GLA_DV), F32)],
        scratch_shapes=[pltpu.VMEM((nb, GLA_HEADS, GLA_DK, GLA_DV), F32)],
        compiler_params=_cparams(("parallel",)),
        name="gla",
    )(qb, kb, vb, la, rb, gn, s0)


def _outproj_kernel(x_ref, oa_ref, ob_ref, ga_ref, gb_ref, wa_ref, wb_ref, wo_ref, n2_ref, wr_ref, br_ref, upper_ref,
                    lower_ref, h_ref, hn_ref, g_ref, lp_ref, cnt_ref):
    tm = x_ref.shape[0]
    merged = (ga_ref[...].astype(F32) * _dot(oa_ref[...], wa_ref[...])
              + gb_ref[...].astype(F32) * _dot(ob_ref[...], wb_ref[...]))
    h = x_ref[...] + _dot(merged.astype(BF16), wo_ref[...])
    h_ref[...] = h
    ms = jnp.mean(h * h, axis=-1, keepdims=True)
    hn = h * lax.rsqrt(ms + NORM_EPS) * n2_ref[...]
    hn_bf = hn.astype(BF16)
    hn_ref[...] = hn_bf
    logits = _dot(hn_bf, wr_ref[...]) + br_ref[...]
    lt = logits.T[:N_EXPERTS, :]
    eid = lax.broadcasted_iota(I32, (N_EXPERTS, tm), 0)
    vals, hots = [], []
    for k in range(TOP_K):
        m = jnp.max(lt, axis=0, keepdims=True)
        idx = jnp.min(jnp.where(lt == m, eid, N_EXPERTS), axis=0, keepdims=True)
        hot = eid == idx
        lt = jnp.where(hot, -jnp.inf, lt)
        vals.append(m)
        hots.append(hot)
    ex = [jnp.exp(vk - vals[0]) for vk in vals]
    den = ex[0] + ex[1] + ex[2] + ex[3]
    for k in range(TOP_K):
        g_ref[k:k + 1, :] = ex[k] / den
    multi = (hots[0] | hots[1] | hots[2] | hots[3]).astype(BF16)
    before = _dot(multi, upper_ref[...])
    counts = jnp.sum(multi.astype(F32), axis=1, keepdims=True)
    aligned = jnp.floor((counts + (RUN_ALIGN - 1)) * (1.0 / RUN_ALIGN)) * RUN_ALIGN
    run_start = _dot(lower_ref[...], jnp.broadcast_to(aligned, (N_EXPERTS, LANE)).astype(BF16))[:, :1]
    place = before + run_start
    for k in range(TOP_K):
        lp_ref[k:k + 1, :] = jnp.sum(jnp.where(hots[k], place, 0.0), axis=0, keepdims=True).astype(I32)
    cnt_ref[...] = jnp.broadcast_to(aligned, cnt_ref.shape).astype(I32)


def _outproj(x2, oa, ob, ga, gb, wa, wb, wo, n2, wr, br, upper, lower):
    T = x2.shape[0]
    tm = TOK_TILE
    assert T % tm == 0 and upper.shape == (tm, tm)
    tok = lambda w: pl.BlockSpec((tm, w), lambda i: (i, 0))
    const = lambda shape: pl.BlockSpec(shape, lambda i: (0, 0))
    kt = pl.BlockSpec((TOP_K, tm), lambda i: (0, i))
    return pl.pallas_call(
        _outproj_kernel,
        grid=(T // tm,),
        in_specs=[tok(D_MODEL), tok(A_WIDTH), tok(B_WIDTH), tok(D_MODEL), tok(D_MODEL),
                  const((A_WIDTH, D_MODEL)), const((B_WIDTH, D_MODEL)), const((D_MODEL, D_MODEL)),
                  const((1, D_MODEL)), const((D_MODEL, LANE)), const((1, LANE)), const((tm, tm)),
                  const((N_EXPERTS, N_EXPERTS))],
        out_specs=[tok(D_MODEL), tok(D_MODEL), kt, kt, pl.BlockSpec((N_EXPERTS, LANE), lambda i: (i, 0))],
        out_shape=[jax.ShapeDtypeStruct((T, D_MODEL), F32), jax.ShapeDtypeStruct((T, D_MODEL), BF16),
                   jax.ShapeDtypeStruct((TOP_K, T), F32), jax.ShapeDtypeStruct((TOP_K, T), I32),
                   jax.ShapeDtypeStruct((T // tm * N_EXPERTS, LANE), I32)],
        compiler_params=_cparams(("parallel",)),
        name="outproj",
    )(x2, oa, ob, ga, gb, wa, wb, wo, n2, wr, br, upper, lower)


def _pack_rows(v, *, is_bf16_valued=False):
    if not is_bf16_valued:
        v = v.astype(BF16).astype(F32)
    lo = pltpu.bitcast(v[:, :ROW_WORDS], U32) >> 16
    hi = pltpu.bitcast(v[:, ROW_WORDS:], U32) & jnp.uint32(0xFFFF0000)
    return lo | hi


def _unpack_rows(w):
    lo = pltpu.bitcast(w << 16, F32).astype(BF16)
    hi = pltpu.bitcast(w & jnp.uint32(0xFFFF0000), F32).astype(BF16)
    return lo, hi


def _start_runs(runs, tile, tile_rows, global_rows, sem, *, to_global):
    cnt_ref, off_ref, dest_ref, _ = runs

    def run(e):
        j = tile * N_EXPERTS + e
        n = pl.multiple_of(cnt_ref[j], RUN_ALIGN)
        local = tile_rows.at[pl.ds(pl.multiple_of(off_ref[j], RUN_ALIGN), n), :]
        far = global_rows.at[pl.ds(pl.multiple_of(dest_ref[j], RUN_ALIGN), n), :]
        return pltpu.make_async_copy(local, far, sem) if to_global else pltpu.make_async_copy(far, local, sem)

    for e in range(N_EXPERTS):
        pl.when(cnt_ref[tile * N_EXPERTS + e] > 0)(lambda e=e: run(e).start(priority=ROW_DMA_PRIORITY))


def _wait_runs(runs, tile, tile_rows, global_rows, sem):
    total = pl.multiple_of(runs[3][tile], RUN_ALIGN)
    pltpu.make_async_copy(global_rows.at[pl.ds(0, total), :], tile_rows.at[pl.ds(0, total), :], sem).wait()


def _dispatch_kernel(fill_ref, cnt_ref, off_ref, dest_ref, tot_ref, lp_ref, hn_p_ref, hn_s_ref, xs_ref,
                     rows2, sems, zbuf, zsem, *, steps_p):
    i = pl.program_id(0)
    runs = (cnt_ref, off_ref, dest_ref, tot_ref)
    slot = i % 2
    rows = rows2.at[slot]

    @pl.when(i == 0)
    def _():
        zbuf[...] = jnp.zeros_like(zbuf)

        def fill_copy(j):
            return pltpu.make_async_copy(zbuf, xs_ref.at[pl.ds(pl.multiple_of(fill_ref[j], MOE_BM), MOE_BM), :], zsem)

        for j in range(fill_ref.shape[0]):
            pl.when(fill_ref[j] >= 0)(lambda j=j: fill_copy(j).start())
        for j in range(fill_ref.shape[0]):
            pl.when(fill_ref[j] >= 0)(lambda j=j: fill_copy(j).wait())

    def sort_tile(hn_ref):
        hn = hn_ref[...]
        lp = lp_ref[...]
        for c in range(SORT_ROWS // SORT_CHUNK):
            r = lax.broadcasted_iota(I32, (SORT_CHUNK, TOK_TILE), 0) + c * SORT_CHUNK
            hit = (r == lp[0:1, :]) | (r == lp[1:2, :]) | (r == lp[2:3, :]) | (r == lp[3:4, :])
            perm = jnp.where(hit, 1.0, 0.0).astype(BF16)
            rows[c * SORT_CHUNK:(c + 1) * SORT_CHUNK, :] = _pack_rows(_dot(perm, hn), is_bf16_valued=True)

    pl.when(i < steps_p)(lambda: sort_tile(hn_p_ref))
    pl.when(i >= steps_p)(lambda: sort_tile(hn_s_ref))
    _start_runs(runs, i, rows, xs_ref, sems.at[slot], to_global=True)
    pl.when(i > 0)(lambda: _wait_runs(runs, i - 1, rows2.at[1 - slot], xs_ref, sems.at[1 - slot]))
    pl.when(i == pl.num_programs(0) - 1)(lambda: _wait_runs(runs, i, rows, xs_ref, sems.at[slot]))


def _dispatch(fill, runs, lp, hn_p, hn_s, n_rows):
    tm = TOK_TILE
    steps_p, steps_s = hn_p.shape[0] // tm, hn_s.shape[0] // tm
    return pl.pallas_call(
        functools.partial(_dispatch_kernel, steps_p=steps_p),
        grid_spec=pltpu.PrefetchScalarGridSpec(
            num_scalar_prefetch=5,
            grid=(steps_p + steps_s,),
            in_specs=[pl.BlockSpec((TOP_K, tm), lambda i, *_: (0, i)),
                      pl.BlockSpec((tm, D_MODEL), lambda i, *_: (jnp.minimum(i, steps_p - 1), 0)),
                      pl.BlockSpec((tm, D_MODEL), lambda i, *_: (jnp.maximum(i - steps_p, 0), 0))],
            out_specs=pl.BlockSpec(memory_space=pl.ANY),
            scratch_shapes=[pltpu.VMEM((2, SORT_ROWS, ROW_WORDS), U32), pltpu.SemaphoreType.DMA((2,)),
                            pltpu.VMEM((MOE_BM, ROW_WORDS), U32), pltpu.SemaphoreType.DMA],
        ),
        out_shape=jax.ShapeDtypeStruct((n_rows, ROW_WORDS), U32),
        compiler_params=_cparams(("arbitrary",)),
        name="dispatch",
    )(fill, *runs, lp, hn_p, hn_s)


def _experts_kernel(be_ref, rows_ref, nu_ref, xs_ref, w1_ref, b1g_ref, b1l_ref, w2_ref, b2_ref, sel_ref, y_ref,
                    w1g_scr, w1l_scr, w2_scr):
    del nu_ref
    i = pl.program_id(0)
    used = rows_ref[i]

    @pl.when((used > 0) & ((i == 0) | (be_ref[i] != be_ref[jnp.maximum(i - 1, 0)])))
    def _():
        for c in range(D_FF // LANE):
            cols = _dot(w1_ref[0, :, c * 2 * LANE:(c + 1) * 2 * LANE].astype(BF16), sel_ref[...])
            w1g_scr[:, c * LANE:(c + 1) * LANE] = cols[:, :LANE].astype(BF16)
            w1l_scr[:, c * LANE:(c + 1) * LANE] = cols[:, LANE:].astype(BF16)
        w2_scr[...] = w2_ref[0].astype(BF16)

    def mlp(words):
        x_lo, x_hi = _unpack_rows(words)

        def up(w_scr, b_ref):
            return _dot(x_lo, w_scr[:ROW_WORDS, :]) + _dot(x_hi, w_scr[ROW_WORDS:, :]) + b_ref[0]

        x_glu = jnp.minimum(up(w1g_scr, b1g_ref), SWIGLU_LIMIT)
        x_lin = jnp.clip(up(w1l_scr, b1l_ref), -SWIGLU_LIMIT, SWIGLU_LIMIT)
        act = x_glu * jax.nn.sigmoid(SWIGLU_ALPHA * x_glu) * (x_lin + 1.0)
        return _pack_rows(_dot(act.astype(BF16), w2_scr[...]) + b2_ref[0])

    half = MOE_BM // 2

    @pl.when(used > half)
    def _():
        y_ref[...] = mlp(xs_ref[...])

    @pl.when((used > 0) & (used <= half))
    def _():
        y_ref[:half, :] = mlp(xs_ref[:half, :])
        y_ref[half:, :] = jnp.zeros((half, ROW_WORDS), U32)

    @pl.when(used == 0)
    def _():
        y_ref[...] = jnp.zeros_like(y_ref)


def _experts(blk_e, blk_rows, n_used, xs, w1, b1g, b1l, w2, b2, sel):
    nblk = xs.shape[0] // MOE_BM
    wspec = lambda r, c: pl.BlockSpec((1, r, c), lambda i, be, rows, nu: (be[i], 0, 0))
    return pl.pallas_call(
        _experts_kernel,
        grid_spec=pltpu.PrefetchScalarGridSpec(
            num_scalar_prefetch=3,
            grid=(nblk,),
            in_specs=[pl.BlockSpec((MOE_BM, ROW_WORDS), lambda i, be, rows, nu: (jnp.minimum(i, nu[0] - 1), 0)),
                      wspec(D_MODEL, 2 * D_FF), wspec(1, D_FF), wspec(1, D_FF),
                      wspec(D_FF, D_MODEL), wspec(1, D_MODEL),
                      pl.BlockSpec((2 * LANE, 2 * LANE), lambda i, be, rows, nu: (0, 0))],
            out_specs=pl.BlockSpec((MOE_BM, ROW_WORDS), lambda i, be, rows, nu: (i, 0)),
            scratch_shapes=[pltpu.VMEM((D_MODEL, D_FF), BF16), pltpu.VMEM((D_MODEL, D_FF), BF16),
                            pltpu.VMEM((D_FF, D_MODEL), BF16)],
        ),
        out_shape=jax.ShapeDtypeStruct(xs.shape, U32),
        compiler_params=_cparams(("arbitrary",)),
        name="experts",
    )(blk_e, blk_rows, n_used, xs, w1, b1g, b1l, w2, b2, sel)


def _combine_kernel(cnt_ref, off_ref, dest_ref, tot_ref, lp_ref, g_ref, h_ref, yb_ref, y_ref, rows2, sems, *, tile0):
    i = pl.program_id(0)
    runs = (cnt_ref, off_ref, dest_ref, tot_ref)
    slot = i % 2
    rows = rows2.at[slot]

    def fetch(step, into):
        _start_runs(runs, step + tile0, rows2.at[into], yb_ref, sems.at[into], to_global=False)

    @pl.when(i == 0)
    def _():
        rows2[...] = jnp.zeros_like(rows2)
        fetch(0, 0)

    pl.when(i + 1 < pl.num_programs(0))(lambda: fetch(i + 1, 1 - slot))
    _wait_runs(runs, i + tile0, rows, yb_ref, sems.at[slot])
    lp, g = lp_ref[...], g_ref[...]
    r = lax.broadcasted_iota(I32, (TOK_TILE, SORT_ROWS), 1)
    mix = jnp.zeros((TOK_TILE, SORT_ROWS), F32)
    for k in range(TOP_K):
        mix = jnp.where(r == lp[:, k:k + 1], g[:, k:k + 1], mix)
    mix = mix.astype(BF16)
    y_lo, y_hi = _unpack_rows(rows[...])
    y_ref[...] = h_ref[...] + jnp.concatenate([_dot(mix, y_lo), _dot(mix, y_hi)], axis=-1)


def _combine(runs, lp_t, gate_t, h, yb, tile0):
    T = h.shape[0]
    tm = TOK_TILE
    tok = lambda w: pl.BlockSpec((tm, w), lambda i, *_: (i, 0))
    return pl.pallas_call(
        functools.partial(_combine_kernel, tile0=tile0),
        grid_spec=pltpu.PrefetchScalarGridSpec(
            num_scalar_prefetch=4,
            grid=(T // tm,),
            in_specs=[tok(TOP_K), tok(TOP_K), tok(D_MODEL), pl.BlockSpec(memory_space=pl.ANY)],
            out_specs=tok(D_MODEL),
            scratch_shapes=[pltpu.VMEM((2, SORT_ROWS, ROW_WORDS), U32), pltpu.SemaphoreType.DMA((2,))],
        ),
        out_shape=jax.ShapeDtypeStruct((T, D_MODEL), F32),
        compiler_params=_cparams(("arbitrary",)),
        name="combine",
    )(*runs, lp_t, gate_t, h, yb)


def _block_ones(width, block):
    idx = np.arange(width) // block
    return jnp.asarray(idx[:, None] == idx[None, :], BF16)


def _layer(x_p, x_s, cache_k, cache_v, s_gla, norm1, w_in, q_norm, k_norm, sinks, w_gk, b_gk,
           gla_norm, w_a, w_b, w_o, norm2, w_router, b_router, w1, b1, w2, b2):
    Bp, Lp, _ = x_p.shape
    Bs, Ls, _ = x_s.shape
    Tp, Ts = Bp * Lp, Bs * Ls

    w_packed = _pack_w_in(w_in)
    wgk = jnp.pad(w_gk, ((0, GL_PAD - GLA_GATE_RANK), (0, 0))).astype(BF16)
    n1 = norm1.reshape(1, D_MODEL)
    gq, gk = _block_ones(A_WIDTH, SWA_HEAD_DIM), _block_ones(KV_WIDTH, SWA_HEAD_DIM)
    qn = jnp.tile(q_norm, SWA_HEADS).reshape(1, A_WIDTH)
    kn = jnp.tile(k_norm, SWA_KV_HEADS).reshape(1, KV_WIDTH)
    bgk = b_gk.reshape(1, GLA_K_WIDTH)
    gn = gla_norm.reshape(1, GLA_DV)
    wa, wb, wo = w_a.astype(BF16), w_b.astype(BF16), w_o.astype(BF16)
    n2 = norm2.reshape(1, D_MODEL)
    wr = jnp.pad(w_router, ((0, 0), (0, LANE - N_EXPERTS))).astype(BF16)
    br = jnp.pad(b_router, (0, LANE - N_EXPERTS)).reshape(1, LANE)
    b1g = b1[:, 0::2].reshape(N_EXPERTS, 1, D_FF)
    b1l = b1[:, 1::2].reshape(N_EXPERTS, 1, D_FF)
    b2r = b2.reshape(N_EXPERTS, 1, D_MODEL)
    sel_np = np.zeros((2 * LANE, 2 * LANE), np.float32)
    sel_np[2 * np.arange(LANE), np.arange(LANE)] = 1.0
    sel_np[2 * np.arange(LANE) + 1, LANE + np.arange(LANE)] = 1.0
    sel = jnp.asarray(sel_np, BF16)

    def mix(x, k_past, v_past, s0):
        B, L, _ = x.shape
        T = B * L
        x2 = x.reshape(T, D_MODEL)
        qa, ka, va, qb, kb, vb, la, rb, ga, gb = _inproj(x2, n1, w_packed, gq, gk, qn, kn, wgk, bgk)
        r3 = lambda t: t.reshape(B, L, t.shape[-1])
        oa = _swa(sinks, r3(qa), r3(ka), r3(va), k_past, v_past, has_past=k_past is not None)
        ob, s_out = _gla(r3(qb), r3(kb), r3(vb), r3(la), r3(rb), gn, s0)
        h, hn, gate, lp, cnt = _outproj(x2, oa.reshape(T, A_WIDTH), ob.reshape(T, B_WIDTH), ga, gb,
                                        wa, wb, wo, n2, wr, br, upper, lower)
        return r3(ka), r3(va), s_out, h, hn, gate, lp, cnt[:, 0].reshape(T // TOK_TILE, N_EXPERTS)

    upper = jnp.asarray(np.arange(TOK_TILE)[:, None] < np.arange(TOK_TILE)[None, :], BF16)
    lower = jnp.asarray(np.arange(N_EXPERTS)[:, None] > np.arange(N_EXPERTS)[None, :], BF16)
    win = cache_k.shape[1]
    assert win == WINDOW
    ck = cache_k.reshape(Bs, win, KV_WIDTH)
    cv = cache_v.reshape(Bs, win, KV_WIDTH)
    ka_p, va_p, s_p, h_p, hn_p, g_p, lp_p, cnt_p = mix(x_p, None, None, jnp.zeros((Bp, GLA_HEADS, GLA_DK, GLA_DV), F32))
    ka_s, va_s, s_s, h_s, hn_s, g_s, lp_s, cnt_s = mix(x_s, ck, cv, s_gla)

    T = Tp + Ts
    cnt = jnp.concatenate([cnt_p, cnt_s], axis=0)
    most_rows = T * TOP_K + cnt.size * (RUN_ALIGN - 1) + N_EXPERTS * (MOE_BM - 1)
    n_rows = -(-most_rows // MOE_BM) * MOE_BM
    counts = jnp.sum(cnt, axis=0)
    padded = (counts + MOE_BM - 1) // MOE_BM * MOE_BM
    pad_end = jnp.cumsum(padded)
    start = pad_end - padded
    dest = start[None, :] + jnp.cumsum(cnt, axis=0) - cnt
    off = jnp.cumsum(cnt, axis=1) - cnt
    flat = lambda a: a.reshape(-1).astype(I32)
    runs = (flat(cnt), flat(off), flat(dest), flat(jnp.sum(cnt, axis=1)))
    tails = jnp.where(padded > 0, pad_end - MOE_BM, -1)
    most_spare = (n_rows - T * TOP_K) // MOE_BM
    spare = pad_end[-1] + jnp.arange(most_spare, dtype=I32) * MOE_BM
    fill = jnp.concatenate([tails, jnp.where(spare < n_rows, spare, -1)]).astype(I32)
    blk_row = jnp.arange(n_rows // MOE_BM, dtype=I32) * MOE_BM
    blk_e = jnp.minimum(jnp.sum(pad_end[None, :] <= blk_row[:, None], axis=1), N_EXPERTS - 1).astype(I32)
    used_end = start + counts
    blk_rows = jnp.clip(jnp.sum(jnp.where(blk_e[:, None] == jnp.arange(N_EXPERTS)[None, :], used_end[None, :], 0),
                                axis=1) - blk_row, 0, MOE_BM)
    blk_rows = jnp.where(blk_row < pad_end[-1], blk_rows, 0).astype(I32)
    n_used = (pad_end[-1:] // MOE_BM).astype(I32)

    xs = _dispatch(fill, runs, jnp.concatenate([lp_p, lp_s], axis=1), hn_p, hn_s, n_rows)
    yb = _experts(blk_e, blk_rows, n_used, xs, w1, b1g, b1l, w2, b2r, sel)
    y_p = _combine(runs, lp_p.T, g_p.T, h_p, yb, 0).reshape(Bp, Lp, D_MODEL)
    y_s = _combine(runs, lp_s.T, g_s.T, h_s, yb, Tp // TOK_TILE).reshape(Bs, Ls, D_MODEL)

    kv = lambda t: t.reshape(t.shape[0], t.shape[1], SWA_KV_HEADS, SWA_HEAD_DIM)
    k_p, v_p = kv(ka_p[:, -WINDOW:]), kv(va_p[:, -WINDOW:])
    k_s = jnp.concatenate([cache_k, kv(ka_s)], axis=1)[:, -win:]
    v_s = jnp.concatenate([cache_v, kv(va_s)], axis=1)[:, -win:]
    return y_p, y_s, (k_p, v_p, s_p, k_s, v_s, s_s)


def kernel(x_prompt, x_sample, cache_swa_k, cache_swa_v, state_gla, norm1, w_in, q_norm, k_norm, sinks, w_gk, b_gk,
           gla_norm, w_a, w_b, w_o, norm2, w_router, b_router, w1, b1, w2, b2):
    depth = norm1.shape[0]
    y_p, y_s = x_prompt, x_sample
    states = []
    for l in range(depth):
        y_p, y_s, st = _layer(y_p, y_s, cache_swa_k[l], cache_swa_v[l], state_gla[l], norm1[l], w_in[l], q_norm[l],
                              k_norm[l], sinks[l], w_gk[l], b_gk[l], gla_norm[l], w_a[l], w_b[l], w_o[l], norm2[l],
                              w_router[l], b_router[l], w1[l], b1[l], w2[l], b2[l])
        states.append(st)
    return (y_p, y_s) + tuple(jnp.stack([st[j] for st in states]) for j in range(6))
```

```python
import functools

import jax
import jax.numpy as jnp
import numpy as np
from jax import lax
from jax.experimental import pallas as pl
from jax.experimental.pallas import tpu as pltpu

F32 = jnp.float32
BF16 = jnp.bfloat16
I32 = jnp.int32
U32 = jnp.uint32
I16 = jnp.int16

D_MODEL = 1024
SWA_HEADS = 8
SWA_KV_HEADS = 2
SWA_GROUP = SWA_HEADS // SWA_KV_HEADS
SWA_HEAD_DIM = 64
WINDOW = 128
ATTN_SCALE = SWA_HEAD_DIM ** -0.5
GLA_HEADS = 4
GLA_DK = 64
GLA_DV = 128
GLA_GATE_RANK = 16
GLA_GATE_NORM = 16.0
GLA_SCALE = GLA_DK ** -0.5
A_WIDTH = SWA_HEADS * SWA_HEAD_DIM
KV_WIDTH = SWA_KV_HEADS * SWA_HEAD_DIM
GLA_K_WIDTH = GLA_HEADS * GLA_DK
B_WIDTH = GLA_HEADS * GLA_DV
N_EXPERTS = 32
TOP_K = 4
D_FF = D_MODEL
SWIGLU_ALPHA = 1.702
SWIGLU_LIMIT = 7.0
NORM_EPS = 1e-5
QK_EPS = 1e-6
NEG_INF = -1e30

LANE = 128
GL_PAD = LANE
VMEM_LIMIT = 56 * 1024 * 1024

_SEGS = (("qa", A_WIDTH), ("ka", KV_WIDTH), ("va", KV_WIDTH), ("qb", GLA_K_WIDTH), ("kb", GLA_K_WIDTH),
         ("vb", B_WIDTH), ("rb", B_WIDTH), ("ga", D_MODEL), ("gb", D_MODEL), ("gl", GL_PAD))
_OFF = {}
_o = 0
for _n, _w in _SEGS:
    _OFF[_n] = (_o, _w)
    _o += _w
IN_PACKED = _o

TOK_TILE = 512
GLA_CHUNK = 64
GLA_SUB = 16
MOE_BM = 512
ROW_DMA_PRIORITY = 1
ROW_WORDS = D_MODEL // 2
RUN_ALIGN = 8
SORT_CHUNK = 256
SORT_ROWS = -(-(TOK_TILE * TOP_K + N_EXPERTS * (RUN_ALIGN - 1)) // SORT_CHUNK) * SORT_CHUNK
SAMPLE_BATCH = 8
GLA_LONG_BATCH = 2


def _cparams(sem):
    return pltpu.CompilerParams(dimension_semantics=sem, vmem_limit_bytes=VMEM_LIMIT)


def _split_bf16(v):
    hi = v.astype(BF16)
    lo = (v - hi.astype(F32)).astype(BF16)
    return hi, lo


def _dot(a, b):
    return jnp.dot(a, b, preferred_element_type=F32)


def _dot_nt(a, b):
    return lax.dot_general(a, b, (((1,), (1,)), ((), ())), preferred_element_type=F32)


def _dot_tn(a, b):
    return lax.dot_general(a, b, (((0,), (0,)), ((), ())), preferred_element_type=F32)


_SRC = {}
_o = 0
for _n, _w in (("qa", A_WIDTH), ("ka", KV_WIDTH), ("va", KV_WIDTH), ("qb", GLA_K_WIDTH), ("kb", GLA_K_WIDTH),
               ("vb", B_WIDTH), ("gl", GLA_GATE_RANK), ("rb", B_WIDTH), ("ga", D_MODEL), ("gb", D_MODEL)):
    _SRC[_n] = (_o, _w)
    _o += _w
IN_WIDTH = _o
PACK_ROWS = 128


def _pack_w_in_kernel(w_ref, o_ref):
    for name, width in _SEGS:
        src, have = _SRC[name]
        dst, _ = _OFF[name]
        o_ref[:, dst:dst + have] = w_ref[:, src:src + have].astype(BF16)
        if have < width:
            o_ref[:, dst + have:dst + width] = jnp.zeros((o_ref.shape[0], width - have), BF16)


def _pack_w_in(w_in):
    return pl.pallas_call(
        _pack_w_in_kernel,
        grid=(D_MODEL // PACK_ROWS,),
        in_specs=[pl.BlockSpec((PACK_ROWS, IN_WIDTH), lambda i: (i, 0))],
        out_specs=pl.BlockSpec((PACK_ROWS, IN_PACKED), lambda i: (i, 0)),
        out_shape=jax.ShapeDtypeStruct((D_MODEL, IN_PACKED), BF16),
        compiler_params=_cparams(("parallel",)),
        name="pack_w_in",
    )(w_in)


def _inproj_kernel(x_ref, n1_ref, w_ref, gq_ref, gk_ref, qn_ref, kn_ref, wgk_ref, bgk_ref,
                   qa_ref, ka_ref, va_ref, qb_ref, kb_ref, vb_ref, la_ref, rb_ref, ga_ref, gb_ref):
    x = x_ref[...]
    ms = jnp.mean(x * x, axis=-1, keepdims=True)
    xn = (x * lax.rsqrt(ms + NORM_EPS) * n1_ref[...]).astype(BF16)

    def proj(name):
        off, width = _OFF[name]
        return _dot(xn, w_ref[:, off:off + width])

    def head_norm(v, ones_ref, gain_ref):
        hi, lo = _split_bf16(v * v)
        ss = _dot(hi, ones_ref[...]) + _dot(lo, ones_ref[...])
        return v * lax.rsqrt(ss * (1.0 / SWA_HEAD_DIM) + QK_EPS) * gain_ref[...]

    def log_decay(gl):
        z = _dot(gl.astype(BF16), wgk_ref[...]) + bgk_ref[...]
        return (jnp.minimum(z, 0.0) - jnp.log1p(jnp.exp(-jnp.abs(z)))) * (1.0 / GLA_GATE_NORM)

    plan = (("qa", qa_ref, lambda u: head_norm(u, gq_ref, qn_ref) * ATTN_SCALE),
            ("ka", ka_ref, lambda u: head_norm(u, gk_ref, kn_ref)),
            ("va", va_ref, lambda u: u),
            ("qb", qb_ref, lambda u: u * GLA_SCALE),
            ("kb", kb_ref, lambda u: u),
            ("vb", vb_ref, lambda u: u),
            ("gl", la_ref, log_decay),
            ("rb", rb_ref, lambda u: u * jax.nn.sigmoid(u)),
            ("ga", ga_ref, jax.nn.sigmoid),
            ("gb", gb_ref, jax.nn.sigmoid))
    u = proj(plan[0][0])
    for idx, (_, out_ref, epilogue) in enumerate(plan):
        u_next = proj(plan[idx + 1][0]) if idx + 1 < len(plan) else None
        out_ref[...] = epilogue(u).astype(out_ref.dtype)
        u = u_next


def _inproj(x2, n1, w_packed, gq, gk, qn, kn, wgk, bgk):
    T = x2.shape[0]
    tm = min(TOK_TILE, T)
    assert T % tm == 0

    def tok(width):
        return pl.BlockSpec((tm, width), lambda i: (i, 0))

    def const(shape):
        return pl.BlockSpec(shape, lambda i: (0, 0))

    outs = (("qa", A_WIDTH, BF16), ("ka", KV_WIDTH, F32), ("va", KV_WIDTH, F32), ("qb", GLA_K_WIDTH, BF16),
            ("kb", GLA_K_WIDTH, BF16), ("vb", B_WIDTH, BF16), ("la", GLA_K_WIDTH, F32), ("rb", B_WIDTH, BF16),
            ("ga", D_MODEL, BF16), ("gb", D_MODEL, BF16))
    return pl.pallas_call(
        _inproj_kernel,
        grid=(T // tm,),
        in_specs=[tok(D_MODEL), const((1, D_MODEL)), const((D_MODEL, IN_PACKED)), const((A_WIDTH, A_WIDTH)),
                  const((KV_WIDTH, KV_WIDTH)), const((1, A_WIDTH)), const((1, KV_WIDTH)),
                  const((GL_PAD, GLA_K_WIDTH)), const((1, GLA_K_WIDTH))],
        out_specs=[tok(w) for _, w, _ in outs],
        out_shape=[jax.ShapeDtypeStruct((T, w), dt) for _, w, dt in outs],
        compiler_params=_cparams(("parallel",)),
        name="inproj",
    )(x2, n1, w_packed, gq, gk, qn, kn, wgk, bgk)


def _swa_kernel(sink_ref, q_ref, kc_ref, vc_ref, kp_ref, vp_ref, o_ref, *, first_block_has_past):
    n = pl.program_id(1)
    nb, lq = q_ref.shape[0], q_ref.shape[1]
    assert lq & (lq - 1) == 0
    stack = SWA_GROUP if lq < WINDOW else 1
    rows, keys = stack * lq, WINDOW + lq
    qi = lax.broadcasted_iota(I32, (rows, keys), 0) & (lq - 1)
    ci = lax.broadcasted_iota(I32, (rows, keys), 1)
    dist = WINDOW + qi - ci
    mask = (dist >= 0) & (dist < WINDOW)
    if not first_block_has_past:
        mask = mask & ((ci >= WINDOW) | (n > 0))
    cols = lambda h0: slice(h0 // SWA_GROUP * SWA_HEAD_DIM, (h0 // SWA_GROUP + 1) * SWA_HEAD_DIM)
    qs, ks, vs = [], [], []
    for j in range(nb):
        q = q_ref[j]
        qs.append(q.astype(F32) if stack > 1 else q)
        ks.append(jnp.concatenate([kp_ref[j], kc_ref[j]], axis=0).astype(BF16))
        vs.append(jnp.concatenate([vp_ref[j], vc_ref[j]], axis=0).astype(BF16))
    groups = [(j, h0) for j in range(nb) for h0 in range(0, SWA_HEADS, stack)]
    wave = len(groups) if stack > 1 else 1
    outs = {}
    for w0 in range(0, len(groups), wave):
        scores, sinks = [], []
        for j, h0 in groups[w0:w0 + wave]:
            heads = range(h0, h0 + stack)
            qg = jnp.concatenate([qs[j][:, h * SWA_HEAD_DIM:(h + 1) * SWA_HEAD_DIM] for h in heads], axis=0)
            sinks.append(jnp.concatenate([jnp.full((lq, 1), sink_ref[h], F32) for h in heads], axis=0))
            scores.append(_dot_nt(qg.astype(BF16), ks[j][:, cols(h0)]))
        probs, denoms = [], []
        for s, sink in zip(scores, sinks):
            s = jnp.where(mask, s, NEG_INF)
            m = jnp.maximum(jnp.max(s, axis=-1, keepdims=True), sink)
            p = jnp.exp(s - m)
            denoms.append(jnp.sum(p, axis=-1, keepdims=True) + jnp.exp(sink - m))
            probs.append(p.astype(BF16))
        for (j, h0), p, denom in zip(groups[w0:w0 + wave], probs, denoms):
            o = _dot(p, vs[j][:, cols(h0)]) / denom
            outs.setdefault(j, []).extend(o[g * lq:(g + 1) * lq] for g in range(stack))
    for j in range(nb):
        o_ref[j] = jnp.concatenate(outs[j], axis=-1).astype(BF16)


def _swa(sinks, qa, ka, va, k_past, v_past, *, has_past):
    B, L, _ = qa.shape
    lq = min(L, WINDOW)
    nl = L // lq
    if has_past:
        assert nl == 1
        nb = min(B, SAMPLE_BATCH)
        past_spec = pl.BlockSpec((nb, WINDOW, KV_WIDTH), lambda b, n, s: (b, 0, 0))
        kp, vp = k_past, v_past
    else:
        assert lq == WINDOW
        nb = 1
        past_spec = pl.BlockSpec((nb, WINDOW, KV_WIDTH), lambda b, n, s: (b, jnp.maximum(n - 1, 0), 0))
        kp, vp = ka, va
    assert B % nb == 0
    cur = lambda w: pl.BlockSpec((nb, lq, w), lambda b, n, s: (b, n, 0))
    return pl.pallas_call(
        functools.partial(_swa_kernel, first_block_has_past=has_past),
        grid_spec=pltpu.PrefetchScalarGridSpec(
            num_scalar_prefetch=1,
            grid=(B // nb, nl),
            in_specs=[cur(A_WIDTH), cur(KV_WIDTH), cur(KV_WIDTH), past_spec, past_spec],
            out_specs=cur(A_WIDTH),
        ),
        out_shape=jax.ShapeDtypeStruct((B, L, A_WIDTH), BF16),
        compiler_params=_cparams(("parallel", "parallel")),
        name="swa",
    )(sinks, qa, ka, va, kp, vp)


def _gla_kernel(q_ref, k_ref, v_ref, la_ref, rb_ref, gn_ref, s0_ref, o_ref, sout_ref, s_scr, *, chunk, sub, length):
    C, SUB = chunk, sub
    S = C // SUB
    W = GLA_K_WIDTH
    nb = q_ref.shape[0]
    s_scr[...] = s0_ref[...]
    row = lax.broadcasted_iota(I32, (C, C), 0)
    col = lax.broadcasted_iota(I32, (C, C), 1)
    tri = (row >= col).astype(BF16)
    diag_mask = (row >= col) & ((row // SUB) == (col // SUB))
    eye = lax.broadcasted_iota(I32, (GLA_DK, GLA_DK), 0) == lax.broadcasted_iota(I32, (GLA_DK, GLA_DK), 1)
    krow = lax.broadcasted_iota(I32, (C, W), 0)

    def rows_of(x, r):
        return jnp.broadcast_to(x[r:r + 1, :], (SUB, W))

    ksl = lambda h: slice(h * GLA_DK, (h + 1) * GLA_DK)
    vsl = lambda h: slice(h * GLA_DV, (h + 1) * GLA_DV)

    def decays(j, r0):
        g_hi, g_lo = _split_bf16(la_ref[j, pl.ds(r0, C), :])
        return _dot(tri, g_hi) + _dot(tri, g_lo)

    def factors(j, r0, b):
        q = q_ref[j, pl.ds(r0, C), :].astype(F32)
        k = k_ref[j, pl.ds(r0, C), :].astype(F32)
        b_last = b[C - 1:C, :]
        mid = jnp.concatenate([rows_of(b, i * SUB + SUB // 2 - 1) for i in range(S)], axis=0)
        beta = jnp.concatenate([jnp.zeros((SUB, W), F32)] + [rows_of(b, i * SUB - 1) for i in range(1, S)], axis=0)
        k_off = [None]
        for i in range(1, S):
            e = jnp.exp(jnp.minimum(b[i * SUB - 1:i * SUB, :] - b, 0.0))
            k_off.append(jnp.where(krow < i * SUB, k * e, 0.0).astype(BF16))
        return dict(q_inter=(q * jnp.exp(b)).astype(BF16), k_state=(k * jnp.exp(b_last - b)).astype(BF16),
                    q_diag=(q * jnp.exp(b - mid)).astype(BF16), k_diag=(k * jnp.exp(mid - b)).astype(BF16),
                    q_off=(q * jnp.exp(b - beta)).astype(BF16), k_off=k_off, dec=jnp.exp(b_last))

    def intra(f, h):
        a_diag = _dot_nt(f["q_diag"][:, ksl(h)], f["k_diag"][:, ksl(h)])
        if S == 1:
            return jnp.where(diag_mask, a_diag, 0.0).astype(BF16)
        blocks = [jnp.zeros((SUB, C), F32)]
        for i in range(1, S):
            blocks.append(_dot_nt(f["q_off"][i * SUB:(i + 1) * SUB, ksl(h)], f["k_off"][i][:, ksl(h)]))
        return jnp.where(diag_mask, a_diag, jnp.concatenate(blocks, axis=0)).astype(BF16)

    def read_out(j, r0, f, a):
        v = v_ref[j, pl.ds(r0, C), :]
        res = []
        for h in range(GLA_HEADS):
            o_h = _dot(a[h], v[:, vsl(h)]) + _dot(f["q_inter"][:, ksl(h)], s_scr[j, h].astype(BF16))
            res.append((o_h, _dot_tn(f["k_state"][:, ksl(h)], v[:, vsl(h)])))
        return res

    def finish(j, r0, f, res):
        outs = []
        for h, (o_h, s_inc) in enumerate(res):
            dec = jnp.broadcast_to(f["dec"][:, ksl(h)], (GLA_DK, GLA_DK))
            dec_col = jnp.sum(jnp.where(eye, dec, 0.0), axis=-1, keepdims=True)
            s_scr[j, h] = dec_col * s_scr[j, h] + s_inc
            ms = jnp.mean(o_h * o_h, axis=-1, keepdims=True)
            outs.append(o_h * lax.rsqrt(ms + NORM_EPS) * gn_ref[...])
        o = jnp.concatenate(outs, axis=-1) * rb_ref[j, pl.ds(r0, C), :].astype(F32)
        o_ref[j, pl.ds(r0, C), :] = o.astype(BF16)

    def step(c):
        r0 = c * C if isinstance(c, int) else pl.multiple_of(c * C, C)
        bs = [decays(j, r0) for j in range(nb)]
        fs = [factors(j, r0, b) for j, b in enumerate(bs)]
        attn = [[intra(f, h) for h in range(GLA_HEADS)] for f in fs]
        res = [read_out(j, r0, fs[j], attn[j]) for j in range(nb)]
        for j in range(nb):
            finish(j, r0, fs[j], res[j])

    if length == C:
        step(0)
    else:
        lax.fori_loop(0, length // C, lambda c, carry: (step(c), carry)[1], 0)
    sout_ref[...] = s_scr[...]


def _gla(qb, kb, vb, la, rb, gn, s0):
    B, L, _ = qb.shape
    chunk = GLA_CHUNK if L % GLA_CHUNK == 0 else L
    sub = GLA_SUB if chunk % GLA_SUB == 0 else chunk
    nb = min(B, SAMPLE_BATCH) if L == chunk else min(B, GLA_LONG_BATCH)
    assert B % nb == 0
    seq = lambda w: pl.BlockSpec((nb, L, w), lambda b: (b, 0, 0))
    st = pl.BlockSpec((nb, GLA_HEADS, GLA_DK, GLA_DV), lambda b: (b, 0, 0, 0))
    return pl.pallas_call(
        functools.partial(_gla_kernel, chunk=chunk, sub=sub, length=L),
        grid=(B // nb,),
        in_specs=[seq(GLA_K_WIDTH), seq(GLA_K_WIDTH), seq(B_WIDTH), seq(GLA_K_WIDTH), seq(B_WIDTH),
                  pl.BlockSpec((1, GLA_DV), lambda b: (0, 0)), st],
        out_specs=[seq(B_WIDTH), st],
        out_shape=[jax.ShapeDtypeStruct((B, L, B_WIDTH), BF16),
                   jax.ShapeDtypeStruct((B, GLA_HEADS, GLA_DK, GLA_DV), F32)],
        scratch_shapes=[pltpu.VMEM((nb, GLA_HEADS, GLA_DK, GLA_DV), F32)],
        compiler_params=_cparams(("parallel",)),
        name="gla",
    )(qb, kb, vb, la, rb, gn, s0)


def _outproj_kernel(x_ref, oa_ref, ob_ref, ga_ref, gb_ref, wa_ref, wb_ref, wo_ref, n2_ref, wr_ref, br_ref, upper_ref,
                    lower_ref, h_ref, hn_ref, g_ref, lp_ref, cnt_ref):
    tm = x_ref.shape[0]
    merged = (ga_ref[...].astype(F32) * _dot(oa_ref[...], wa_ref[...])
              + gb_ref[...].astype(F32) * _dot(ob_ref[...], wb_ref[...]))
    h = x_ref[...] + _dot(merged.astype(BF16), wo_ref[...])
    h_ref[...] = h
    ms = jnp.mean(h * h, axis=-1, keepdims=True)
    hn = h * lax.rsqrt(ms + NORM_EPS) * n2_ref[...]
    hn_bf = hn.astype(BF16)
    hn_ref[...] = hn_bf
    logits = _dot(hn_bf, wr_ref[...]) + br_ref[...]
    lt = logits.T[:N_EXPERTS, :]
    eid = lax.broadcasted_iota(I32, (N_EXPERTS, tm), 0)
    vals, hots = [], []
    for k in range(TOP_K):
        m = jnp.max(lt, axis=0, keepdims=True)
        idx = jnp.min(jnp.where(lt == m, eid, N_EXPERTS), axis=0, keepdims=True)
        hot = eid == idx
        lt = jnp.where(hot, -jnp.inf, lt)
        vals.append(m)
        hots.append(hot)
    ex = [jnp.exp(vk - vals[0]) for vk in vals]
    den = ex[0] + ex[1] + ex[2] + ex[3]
    for k in range(TOP_K):
        g_ref[k:k + 1, :] = ex[k] / den
    multi = (hots[0] | hots[1] | hots[2] | hots[3]).astype(BF16)
    before = _dot(multi, upper_ref[...])
    counts = jnp.sum(multi.astype(F32), axis=1, keepdims=True)
    aligned = jnp.floor((counts + (RUN_ALIGN - 1)) * (1.0 / RUN_ALIGN)) * RUN_ALIGN
    run_start = _dot(lower_ref[...], jnp.broadcast_to(aligned, (N_EXPERTS, LANE)).astype(BF16))[:, :1]
    place = before + run_start
    for k in range(TOP_K):
        lp_ref[k:k + 1, :] = jnp.sum(jnp.where(hots[k], place, 0.0), axis=0, keepdims=True).astype(I32)
    cnt_ref[...] = jnp.broadcast_to(aligned, cnt_ref.shape).astype(I32)


def _outproj(x2, oa, ob, ga, gb, wa, wb, wo, n2, wr, br, upper, lower):
    T = x2.shape[0]
    tm = TOK_TILE
    assert T % tm == 0 and upper.shape == (tm, tm)
    tok = lambda w: pl.BlockSpec((tm, w), lambda i: (i, 0))
    const = lambda shape: pl.BlockSpec(shape, lambda i: (0, 0))
    kt = pl.BlockSpec((TOP_K, tm), lambda i: (0, i))
    return pl.pallas_call(
        _outproj_kernel,
        grid=(T // tm,),
        in_specs=[tok(D_MODEL), tok(A_WIDTH), tok(B_WIDTH), tok(D_MODEL), tok(D_MODEL),
                  const((A_WIDTH, D_MODEL)), const((B_WIDTH, D_MODEL)), const((D_MODEL, D_MODEL)),
                  const((1, D_MODEL)), const((D_MODEL, LANE)), const((1, LANE)), const((tm, tm)),
                  const((N_EXPERTS, N_EXPERTS))],
        out_specs=[tok(D_MODEL), tok(D_MODEL), kt, kt, pl.BlockSpec((N_EXPERTS, LANE), lambda i: (i, 0))],
        out_shape=[jax.ShapeDtypeStruct((T, D_MODEL), F32), jax.ShapeDtypeStruct((T, D_MODEL), BF16),
                   jax.ShapeDtypeStruct((TOP_K, T), F32), jax.ShapeDtypeStruct((TOP_K, T), I32),
                   jax.ShapeDtypeStruct((T // tm * N_EXPERTS, LANE), I32)],
        compiler_params=_cparams(("parallel",)),
        name="outproj",
    )(x2, oa, ob, ga, gb, wa, wb, wo, n2, wr, br, upper, lower)


def _pack_rows(v, *, is_bf16_valued=False):
    if not is_bf16_valued:
        v = v.astype(BF16).astype(F32)
    lo = pltpu.bitcast(v[:, :ROW_WORDS], U32) >> 16
    hi = pltpu.bitcast(v[:, ROW_WORDS:], U32) & jnp.uint32(0xFFFF0000)
    return lo | hi


def _unpack_rows(w):
    lo = pltpu.bitcast(w << 16, F32).astype(BF16)
    hi = pltpu.bitcast(w & jnp.uint32(0xFFFF0000), F32).astype(BF16)
    return lo, hi


def _start_runs(runs, tile, tile_rows, global_rows, sem, *, to_global):
    cnt_ref, off_ref, dest_ref, _ = runs

    def run(e):
        j = tile * N_EXPERTS + e
        n = pl.multiple_of(cnt_ref[j], RUN_ALIGN)
        local = tile_rows.at[pl.ds(pl.multiple_of(off_ref[j], RUN_ALIGN), n), :]
        far = global_rows.at[pl.ds(pl.multiple_of(dest_ref[j], RUN_ALIGN), n), :]
        return pltpu.make_async_copy(local, far, sem) if to_global else pltpu.make_async_copy(far, local, sem)

    for e in range(N_EXPERTS):
        pl.when(cnt_ref[tile * N_EXPERTS + e] > 0)(lambda e=e: run(e).start(priority=ROW_DMA_PRIORITY))


def _wait_runs(runs, tile, tile_rows, global_rows, sem):
    total = pl.multiple_of(runs[3][tile], RUN_ALIGN)
    pltpu.make_async_copy(global_rows.at[pl.ds(0, total), :], tile_rows.at[pl.ds(0, total), :], sem).wait()


def _dispatch_kernel(fill_ref, cnt_ref, off_ref, dest_ref, tot_ref, lp_ref, hn_p_ref, hn_s_ref, xs_ref,
                     rows2, sems, zbuf, zsem, *, steps_p):
    i = pl.program_id(0)
    runs = (cnt_ref, off_ref, dest_ref, tot_ref)
    slot = i % 2
    rows = rows2.at[slot]

    @pl.when(i == 0)
    def _():
        zbuf[...] = jnp.zeros_like(zbuf)

        def fill_copy(j):
            return pltpu.make_async_copy(zbuf, xs_ref.at[pl.ds(pl.multiple_of(fill_ref[j], MOE_BM), MOE_BM), :], zsem)

        for j in range(fill_ref.shape[0]):
            pl.when(fill_ref[j] >= 0)(lambda j=j: fill_copy(j).start())
        for j in range(fill_ref.shape[0]):
            pl.when(fill_ref[j] >= 0)(lambda j=j: fill_copy(j).wait())

    def sort_tile(hn_ref):
        hn = hn_ref[...]
        r = lax.broadcasted_iota(I32, (SORT_CHUNK, TOK_TILE), 0).astype(I16)
        for c in range(SORT_ROWS // SORT_CHUNK):
            lp = (lp_ref[...] - c * SORT_CHUNK).astype(I16)
            hit = (r == lp[0:1, :]) | (r == lp[1:2, :]) | (r == lp[2:3, :]) | (r == lp[3:4, :])
            perm = jnp.where(hit, jnp.ones((), BF16), jnp.zeros((), BF16))
            rows[c * SORT_CHUNK:(c + 1) * SORT_CHUNK, :] = _pack_rows(_dot(perm, hn), is_bf16_valued=True)

    pl.when(i < steps_p)(lambda: sort_tile(hn_p_ref))
    pl.when(i >= steps_p)(lambda: sort_tile(hn_s_ref))
    _start_runs(runs, i, rows, xs_ref, sems.at[slot], to_global=True)
    pl.when(i > 0)(lambda: _wait_runs(runs, i - 1, rows2.at[1 - slot], xs_ref, sems.at[1 - slot]))
    pl.when(i == pl.num_programs(0) - 1)(lambda: _wait_runs(runs, i, rows, xs_ref, sems.at[slot]))


def _dispatch(fill, runs, lp, hn_p, hn_s, n_rows):
    tm = TOK_TILE
    steps_p, steps_s = hn_p.shape[0] // tm, hn_s.shape[0] // tm
    return pl.pallas_call(
        functools.partial(_dispatch_kernel, steps_p=steps_p),
        grid_spec=pltpu.PrefetchScalarGridSpec(
            num_scalar_prefetch=5,
            grid=(steps_p + steps_s,),
            in_specs=[pl.BlockSpec((TOP_K, tm), lambda i, *_: (0, i)),
                      pl.BlockSpec((tm, D_MODEL), lambda i, *_: (jnp.minimum(i, steps_p - 1), 0)),
                      pl.BlockSpec((tm, D_MODEL), lambda i, *_: (jnp.maximum(i - steps_p, 0), 0))],
            out_specs=pl.BlockSpec(memory_space=pl.ANY),
            scratch_shapes=[pltpu.VMEM((2, SORT_ROWS, ROW_WORDS), U32), pltpu.SemaphoreType.DMA((2,)),
                            pltpu.VMEM((MOE_BM, ROW_WORDS), U32), pltpu.SemaphoreType.DMA],
        ),
        out_shape=jax.ShapeDtypeStruct((n_rows, ROW_WORDS), U32),
        compiler_params=_cparams(("arbitrary",)),
        name="dispatch",
    )(fill, *runs, lp, hn_p, hn_s)


def _experts_kernel(be_ref, rows_ref, nu_ref, xs_ref, w1_ref, b1g_ref, b1l_ref, w2_ref, b2_ref, sel_ref, y_ref,
                    w1g_scr, w1l_scr, w2_scr):
    del nu_ref
    i = pl.program_id(0)
    used = rows_ref[i]

    @pl.when((used > 0) & ((i == 0) | (be_ref[i] != be_ref[jnp.maximum(i - 1, 0)])))
    def _():
        for c in range(D_FF // LANE):
            cols = _dot(w1_ref[0, :, c * 2 * LANE:(c + 1) * 2 * LANE].astype(BF16), sel_ref[...])
            w1g_scr[:, c * LANE:(c + 1) * LANE] = cols[:, :LANE].astype(BF16)
            w1l_scr[:, c * LANE:(c + 1) * LANE] = cols[:, LANE:].astype(BF16)
        w2_scr[...] = w2_ref[0].astype(BF16)

    def mlp(words):
        x_lo, x_hi = _unpack_rows(words)

        def up(w_scr, b_ref):
            return _dot(x_lo, w_scr[:ROW_WORDS, :]) + _dot(x_hi, w_scr[ROW_WORDS:, :]) + b_ref[0]

        x_glu = jnp.minimum(up(w1g_scr, b1g_ref), SWIGLU_LIMIT)
        x_lin = jnp.clip(up(w1l_scr, b1l_ref), -SWIGLU_LIMIT, SWIGLU_LIMIT)
        act = x_glu * jax.nn.sigmoid(SWIGLU_ALPHA * x_glu) * (x_lin + 1.0)
        return _pack_rows(_dot(act.astype(BF16), w2_scr[...]) + b2_ref[0])

    half = MOE_BM // 2

    @pl.when(used > half)
    def _():
        y_ref[...] = mlp(xs_ref[...])

    @pl.when((used > 0) & (used <= half))
    def _():
        y_ref[:half, :] = mlp(xs_ref[:half, :])
        y_ref[half:, :] = jnp.zeros((half, ROW_WORDS), U32)

    @pl.when(used == 0)
    def _():
        y_ref[...] = jnp.zeros_like(y_ref)


def _experts(blk_e, blk_rows, n_used, xs, w1, b1g, b1l, w2, b2, sel):
    nblk = xs.shape[0] // MOE_BM
    wspec = lambda r, c: pl.BlockSpec((1, r, c), lambda i, be, rows, nu: (be[i], 0, 0))
    return pl.pallas_call(
        _experts_kernel,
        grid_spec=pltpu.PrefetchScalarGridSpec(
            num_scalar_prefetch=3,
            grid=(nblk,),
            in_specs=[pl.BlockSpec((MOE_BM, ROW_WORDS), lambda i, be, rows, nu: (jnp.minimum(i, nu[0] - 1), 0)),
                      wspec(D_MODEL, 2 * D_FF), wspec(1, D_FF), wspec(1, D_FF),
                      wspec(D_FF, D_MODEL), wspec(1, D_MODEL),
                      pl.BlockSpec((2 * LANE, 2 * LANE), lambda i, be, rows, nu: (0, 0))],
            out_specs=pl.BlockSpec((MOE_BM, ROW_WORDS), lambda i, be, rows, nu: (i, 0)),
            scratch_shapes=[pltpu.VMEM((D_MODEL, D_FF), BF16), pltpu.VMEM((D_MODEL, D_FF), BF16),
                            pltpu.VMEM((D_FF, D_MODEL), BF16)],
        ),
        out_shape=jax.ShapeDtypeStruct(xs.shape, U32),
        compiler_params=_cparams(("arbitrary",)),
        name="experts",
    )(blk_e, blk_rows, n_used, xs, w1, b1g, b1l, w2, b2, sel)


def _combine_kernel(cnt_ref, off_ref, dest_ref, tot_ref, lp_ref, g_ref, h_ref, yb_ref, y_ref, rows2, sems, *, tile0):
    i = pl.program_id(0)
    runs = (cnt_ref, off_ref, dest_ref, tot_ref)
    slot = i % 2
    rows = rows2.at[slot]

    def fetch(step, into):
        _start_runs(runs, step + tile0, rows2.at[into], yb_ref, sems.at[into], to_global=False)

    @pl.when(i == 0)
    def _():
        rows2[...] = jnp.zeros_like(rows2)
        fetch(0, 0)

    pl.when(i + 1 < pl.num_programs(0))(lambda: fetch(i + 1, 1 - slot))
    _wait_runs(runs, i + tile0, rows, yb_ref, sems.at[slot])
    lp, g = lp_ref[...].astype(I16), g_ref[...].astype(BF16)
    r = lax.broadcasted_iota(I32, (TOK_TILE, SORT_ROWS), 1).astype(I16)
    mix = jnp.zeros((TOK_TILE, SORT_ROWS), BF16)
    for k in range(TOP_K):
        mix = jnp.where(r == lp[:, k:k + 1], g[:, k:k + 1], mix)
    y_lo, y_hi = _unpack_rows(rows[...])
    y_ref[...] = h_ref[...] + jnp.concatenate([_dot(mix, y_lo), _dot(mix, y_hi)], axis=-1)


def _combine(runs, lp_t, gate_t, h, yb, tile0):
    T = h.shape[0]
    tm = TOK_TILE
    tok = lambda w: pl.BlockSpec((tm, w), lambda i, *_: (i, 0))
    return pl.pallas_call(
        functools.partial(_combine_kernel, tile0=tile0),
        grid_spec=pltpu.PrefetchScalarGridSpec(
            num_scalar_prefetch=4,
            grid=(T // tm,),
            in_specs=[tok(TOP_K), tok(TOP_K), tok(D_MODEL), pl.BlockSpec(memory_space=pl.ANY)],
            out_specs=tok(D_MODEL),
            scratch_shapes=[pltpu.VMEM((2, SORT_ROWS, ROW_WORDS), U32), pltpu.SemaphoreType.DMA((2,))],
        ),
        out_shape=jax.ShapeDtypeStruct((T, D_MODEL), F32),
        compiler_params=_cparams(("arbitrary",)),
        name="combine",
    )(*runs, lp_t, gate_t, h, yb)


def _block_ones(width, block):
    idx = np.arange(width) // block
    return jnp.asarray(idx[:, None] == idx[None, :], BF16)


def _layer(x_p, x_s, cache_k, cache_v, s_gla, norm1, w_in, q_norm, k_norm, sinks, w_gk, b_gk,
           gla_norm, w_a, w_b, w_o, norm2, w_router, b_router, w1, b1, w2, b2):
    Bp, Lp, _ = x_p.shape
    Bs, Ls, _ = x_s.shape
    Tp, Ts = Bp * Lp, Bs * Ls

    w_packed = _pack_w_in(w_in)
    wgk = jnp.pad(w_gk, ((0, GL_PAD - GLA_GATE_RANK), (0, 0))).astype(BF16)
    n1 = norm1.reshape(1, D_MODEL)
    gq, gk = _block_ones(A_WIDTH, SWA_HEAD_DIM), _block_ones(KV_WIDTH, SWA_HEAD_DIM)
    qn = jnp.tile(q_norm, SWA_HEADS).reshape(1, A_WIDTH)
    kn = jnp.tile(k_norm, SWA_KV_HEADS).reshape(1, KV_WIDTH)
    bgk = b_gk.reshape(1, GLA_K_WIDTH)
    gn = gla_norm.reshape(1, GLA_DV)
    wa, wb, wo = w_a.astype(BF16), w_b.astype(BF16), w_o.astype(BF16)
    n2 = norm2.reshape(1, D_MODEL)
    wr = jnp.pad(w_router, ((0, 0), (0, LANE - N_EXPERTS))).astype(BF16)
    br = jnp.pad(b_router, (0, LANE - N_EXPERTS)).reshape(1, LANE)
    b1g = b1[:, 0::2].reshape(N_EXPERTS, 1, D_FF)
    b1l = b1[:, 1::2].reshape(N_EXPERTS, 1, D_FF)
    b2r = b2.reshape(N_EXPERTS, 1, D_MODEL)
    sel_np = np.zeros((2 * LANE, 2 * LANE), np.float32)
    sel_np[2 * np.arange(LANE), np.arange(LANE)] = 1.0
    sel_np[2 * np.arange(LANE) + 1, LANE + np.arange(LANE)] = 1.0
    sel = jnp.asarray(sel_np, BF16)

    def mix(x, k_past, v_past, s0):
        B, L, _ = x.shape
        T = B * L
        x2 = x.reshape(T, D_MODEL)
        qa, ka, va, qb, kb, vb, la, rb, ga, gb = _inproj(x2, n1, w_packed, gq, gk, qn, kn, wgk, bgk)
        r3 = lambda t: t.reshape(B, L, t.shape[-1])
        oa = _swa(sinks, r3(qa), r3(ka), r3(va), k_past, v_past, has_past=k_past is not None)
        ob, s_out = _gla(r3(qb), r3(kb), r3(vb), r3(la), r3(rb), gn, s0)
        h, hn, gate, lp, cnt = _outproj(x2, oa.reshape(T, A_WIDTH), ob.reshape(T, B_WIDTH), ga, gb,
                                        wa, wb, wo, n2, wr, br, upper, lower)
        return r3(ka), r3(va), s_out, h, hn, gate, lp, cnt[:, 0].reshape(T // TOK_TILE, N_EXPERTS)

    upper = jnp.asarray(np.arange(TOK_TILE)[:, None] < np.arange(TOK_TILE)[None, :], BF16)
    lower = jnp.asarray(np.arange(N_EXPERTS)[:, None] > np.arange(N_EXPERTS)[None, :], BF16)
    win = cache_k.shape[1]
    assert win == WINDOW
    ck = cache_k.reshape(Bs, win, KV_WIDTH)
    cv = cache_v.reshape(Bs, win, KV_WIDTH)
    ka_p, va_p, s_p, h_p, hn_p, g_p, lp_p, cnt_p = mix(x_p, None, None, jnp.zeros((Bp, GLA_HEADS, GLA_DK, GLA_DV), F32))
    ka_s, va_s, s_s, h_s, hn_s, g_s, lp_s, cnt_s = mix(x_s, ck, cv, s_gla)

    T = Tp + Ts
    cnt = jnp.concatenate([cnt_p, cnt_s], axis=0)
    most_rows = T * TOP_K + cnt.size * (RUN_ALIGN - 1) + N_EXPERTS * (MOE_BM - 1)
    n_rows = -(-most_rows // MOE_BM) * MOE_BM
    counts = jnp.sum(cnt, axis=0)
    padded = (counts + MOE_BM - 1) // MOE_BM * MOE_BM
    pad_end = jnp.cumsum(padded)
    start = pad_end - padded
    dest = start[None, :] + jnp.cumsum(cnt, axis=0) - cnt
    off = jnp.cumsum(cnt, axis=1) - cnt
    flat = lambda a: a.reshape(-1).astype(I32)
    runs = (flat(cnt), flat(off), flat(dest), flat(jnp.sum(cnt, axis=1)))
    tails = jnp.where(padded > 0, pad_end - MOE_BM, -1)
    most_spare = (n_rows - T * TOP_K) // MOE_BM
    spare = pad_end[-1] + jnp.arange(most_spare, dtype=I32) * MOE_BM
    fill = jnp.concatenate([tails, jnp.where(spare < n_rows, spare, -1)]).astype(I32)
    blk_row = jnp.arange(n_rows // MOE_BM, dtype=I32) * MOE_BM
    blk_e = jnp.minimum(jnp.sum(pad_end[None, :] <= blk_row[:, None], axis=1), N_EXPERTS - 1).astype(I32)
    used_end = start + counts
    blk_rows = jnp.clip(jnp.sum(jnp.where(blk_e[:, None] == jnp.arange(N_EXPERTS)[None, :], used_end[None, :], 0),
                                axis=1) - blk_row, 0, MOE_BM)
    blk_rows = jnp.where(blk_row < pad_end[-1], blk_rows, 0).astype(I32)
    n_used = (pad_end[-1:] // MOE_BM).astype(I32)

    xs = _dispatch(fill, runs, jnp.concatenate([lp_p, lp_s], axis=1), hn_p, hn_s, n_rows)
    yb = _experts(blk_e, blk_rows, n_used, xs, w1, b1g, b1l, w2, b2r, sel)
    y_p = _combine(runs, lp_p.T, g_p.T, h_p, yb, 0).reshape(Bp, Lp, D_MODEL)
    y_s = _combine(runs, lp_s.T, g_s.T, h_s, yb, Tp // TOK_TILE).reshape(Bs, Ls, D_MODEL)

    kv = lambda t: t.reshape(t.shape[0], t.shape[1], SWA_KV_HEADS, SWA_HEAD_DIM)
    k_p, v_p = kv(ka_p[:, -WINDOW:]), kv(va_p[:, -WINDOW:])
    k_s = jnp.concatenate([cache_k, kv(ka_s)], axis=1)[:, -win:]
    v_s = jnp.concatenate([cache_v, kv(va_s)], axis=1)[:, -win:]
    return y_p, y_s, (k_p, v_p, s_p, k_s, v_s, s_s)


def kernel(x_prompt, x_sample, cache_swa_k, cache_swa_v, state_gla, norm1, w_in, q_norm, k_norm, sinks, w_gk, b_gk,
           gla_norm, w_a, w_b, w_o, norm2, w_router, b_router, w1, b1, w2, b2):
    depth = norm1.shape[0]
    y_p, y_s = x_prompt, x_sample
    states = []
    for l in range(depth):
        y_p, y_s, st = _layer(y_p, y_s, cache_swa_k[l], cache_swa_v[l], state_gla[l], norm1[l], w_in[l], q_norm[l],
                              k_norm[l], sinks[l], w_gk[l], b_gk[l], gla_norm[l], w_a[l], w_b[l], w_o[l], norm2[l],
                              w_router[l], b_router[l], w1[l], b1[l], w2[l], b2[l])
        states.append(st)
    return (y_p, y_s) + tuple(jnp.stack([st[j] for st in states]) for j in range(6))
```

```python
import functools

import jax
import jax.numpy as jnp
import numpy as np
from jax import lax
from jax.experimental import pallas as pl
from jax.experimental.pallas import tpu as pltpu

F32 = jnp.float32
BF16 = jnp.bfloat16
I32 = jnp.int32
U32 = jnp.uint32
I16 = jnp.int16

D_MODEL = 1024
SWA_HEADS = 8
SWA_KV_HEADS = 2
SWA_GROUP = SWA_HEADS // SWA_KV_HEADS
SWA_HEAD_DIM = 64
WINDOW = 128
ATTN_SCALE = SWA_HEAD_DIM ** -0.5
GLA_HEADS = 4
GLA_DK = 64
GLA_DV = 128
GLA_GATE_RANK = 16
GLA_GATE_NORM = 16.0
GLA_SCALE = GLA_DK ** -0.5
A_WIDTH = SWA_HEADS * SWA_HEAD_DIM
KV_WIDTH = SWA_KV_HEADS * SWA_HEAD_DIM
GLA_K_WIDTH = GLA_HEADS * GLA_DK
B_WIDTH = GLA_HEADS * GLA_DV
N_EXPERTS = 32
TOP_K = 4
D_FF = D_MODEL
SWIGLU_ALPHA = 1.702
SWIGLU_LIMIT = 7.0
NORM_EPS = 1e-5
QK_EPS = 1e-6
NEG_INF = -1e30

LANE = 128
GL_PAD = LANE
VMEM_LIMIT = 56 * 1024 * 1024

_SEGS = (("qa", A_WIDTH), ("ka", KV_WIDTH), ("va", KV_WIDTH), ("qb", GLA_K_WIDTH), ("kb", GLA_K_WIDTH),
         ("vb", B_WIDTH), ("rb", B_WIDTH), ("ga", D_MODEL), ("gb", D_MODEL), ("gl", GL_PAD))
_OFF = {}
_o = 0
for _n, _w in _SEGS:
    _OFF[_n] = (_o, _w)
    _o += _w
IN_PACKED = _o

TOK_TILE = 512
GLA_CHUNK = 64
GLA_SUB = 16
MOE_BM = 512
ROW_DMA_PRIORITY = 1
ROW_WORDS = D_MODEL // 2
RUN_ALIGN = 8
SORT_CHUNK = 256
SORT_ROWS = -(-(TOK_TILE * TOP_K + N_EXPERTS * (RUN_ALIGN - 1)) // SORT_CHUNK) * SORT_CHUNK
SAMPLE_BATCH = 8
GLA_LONG_BATCH = 8
GLA_SEQ_TILE = 512


def _cparams(sem):
    return pltpu.CompilerParams(dimension_semantics=sem, vmem_limit_bytes=VMEM_LIMIT)


def _split_bf16(v):
    hi = v.astype(BF16)
    lo = (v - hi.astype(F32)).astype(BF16)
    return hi, lo


def _dot(a, b):
    return jnp.dot(a, b, preferred_element_type=F32)


def _dot_nt(a, b):
    return lax.dot_general(a, b, (((1,), (1,)), ((), ())), preferred_element_type=F32)


def _dot_tn(a, b):
    return lax.dot_general(a, b, (((0,), (0,)), ((), ())), preferred_element_type=F32)


_SRC = {}
_o = 0
for _n, _w in (("qa", A_WIDTH), ("ka", KV_WIDTH), ("va", KV_WIDTH), ("qb", GLA_K_WIDTH), ("kb", GLA_K_WIDTH),
               ("vb", B_WIDTH), ("gl", GLA_GATE_RANK), ("rb", B_WIDTH), ("ga", D_MODEL), ("gb", D_MODEL)):
    _SRC[_n] = (_o, _w)
    _o += _w
IN_WIDTH = _o
PACK_ROWS = 128


def _pack_w_in_kernel(w_ref, o_ref):
    for name, width in _SEGS:
        src, have = _SRC[name]
        dst, _ = _OFF[name]
        o_ref[:, dst:dst + have] = w_ref[:, src:src + have].astype(BF16)
        if have < width:
            o_ref[:, dst + have:dst + width] = jnp.zeros((o_ref.shape[0], width - have), BF16)


def _pack_w_in(w_in):
    return pl.pallas_call(
        _pack_w_in_kernel,
        grid=(D_MODEL // PACK_ROWS,),
        in_specs=[pl.BlockSpec((PACK_ROWS, IN_WIDTH), lambda i: (i, 0))],
        out_specs=pl.BlockSpec((PACK_ROWS, IN_PACKED), lambda i: (i, 0)),
        out_shape=jax.ShapeDtypeStruct((D_MODEL, IN_PACKED), BF16),
        compiler_params=_cparams(("parallel",)),
        name="pack_w_in",
    )(w_in)


def _inproj_kernel(x_ref, n1_ref, w_ref, gq_ref, gk_ref, qn_ref, kn_ref, wgk_ref, bgk_ref,
                   qa_ref, ka_ref, va_ref, qb_ref, kb_ref, vb_ref, la_ref, rb_ref, ga_ref, gb_ref):
    x = x_ref[...]
    ms = jnp.mean(x * x, axis=-1, keepdims=True)
    xn = (x * lax.rsqrt(ms + NORM_EPS) * n1_ref[...]).astype(BF16)

    def proj(name):
        off, width = _OFF[name]
        return _dot(xn, w_ref[:, off:off + width])

    def head_norm(v, ones_ref, gain_ref):
        hi, lo = _split_bf16(v * v)
        ss = _dot(hi, ones_ref[...]) + _dot(lo, ones_ref[...])
        return v * lax.rsqrt(ss * (1.0 / SWA_HEAD_DIM) + QK_EPS) * gain_ref[...]

    def log_decay(gl):
        z = _dot(gl.astype(BF16), wgk_ref[...]) + bgk_ref[...]
        return (jnp.minimum(z, 0.0) - jnp.log1p(jnp.exp(-jnp.abs(z)))) * (1.0 / GLA_GATE_NORM)

    plan = (("qa", qa_ref, lambda u: head_norm(u, gq_ref, qn_ref) * ATTN_SCALE),
            ("ka", ka_ref, lambda u: head_norm(u, gk_ref, kn_ref)),
            ("va", va_ref, lambda u: u),
            ("qb", qb_ref, lambda u: u * GLA_SCALE),
            ("kb", kb_ref, lambda u: u),
            ("vb", vb_ref, lambda u: u),
            ("gl", la_ref, log_decay),
            ("rb", rb_ref, lambda u: u * jax.nn.sigmoid(u)),
            ("ga", ga_ref, jax.nn.sigmoid),
            ("gb", gb_ref, jax.nn.sigmoid))
    u = proj(plan[0][0])
    for idx, (_, out_ref, epilogue) in enumerate(plan):
        u_next = proj(plan[idx + 1][0]) if idx + 1 < len(plan) else None
        out_ref[...] = epilogue(u).astype(out_ref.dtype)
        u = u_next


def _inproj(x2, n1, w_packed, gq, gk, qn, kn, wgk, bgk):
    T = x2.shape[0]
    tm = min(TOK_TILE, T)
    assert T % tm == 0

    def tok(width):
        return pl.BlockSpec((tm, width), lambda i: (i, 0))

    def const(shape):
        return pl.BlockSpec(shape, lambda i: (0, 0))

    outs = (("qa", A_WIDTH, BF16), ("ka", KV_WIDTH, F32), ("va", KV_WIDTH, F32), ("qb", GLA_K_WIDTH, BF16),
            ("kb", GLA_K_WIDTH, BF16), ("vb", B_WIDTH, BF16), ("la", GLA_K_WIDTH, F32), ("rb", B_WIDTH, BF16),
            ("ga", D_MODEL, BF16), ("gb", D_MODEL, BF16))
    return pl.pallas_call(
        _inproj_kernel,
        grid=(T // tm,),
        in_specs=[tok(D_MODEL), const((1, D_MODEL)), const((D_MODEL, IN_PACKED)), const((A_WIDTH, A_WIDTH)),
                  const((KV_WIDTH, KV_WIDTH)), const((1, A_WIDTH)), const((1, KV_WIDTH)),
                  const((GL_PAD, GLA_K_WIDTH)), const((1, GLA_K_WIDTH))],
        out_specs=[tok(w) for _, w, _ in outs],
        out_shape=[jax.ShapeDtypeStruct((T, w), dt) for _, w, dt in outs],
        compiler_params=_cparams(("parallel",)),
        name="inproj",
    )(x2, n1, w_packed, gq, gk, qn, kn, wgk, bgk)


def _swa_kernel(sink_ref, q_ref, kc_ref, vc_ref, kp_ref, vp_ref, o_ref, *, first_block_has_past):
    n = pl.program_id(1)
    nb, lq = q_ref.shape[0], q_ref.shape[1]
    assert lq & (lq - 1) == 0
    stack = SWA_GROUP if lq < WINDOW else 1
    rows, keys = stack * lq, WINDOW + lq
    qi = lax.broadcasted_iota(I32, (rows, keys), 0) & (lq - 1)
    ci = lax.broadcasted_iota(I32, (rows, keys), 1)
    dist = WINDOW + qi - ci
    mask = (dist >= 0) & (dist < WINDOW)
    if not first_block_has_past:
        mask = mask & ((ci >= WINDOW) | (n > 0))
    cols = lambda h0: slice(h0 // SWA_GROUP * SWA_HEAD_DIM, (h0 // SWA_GROUP + 1) * SWA_HEAD_DIM)
    qs, ks, vs = [], [], []
    for j in range(nb):
        q = q_ref[j]
        qs.append(q.astype(F32) if stack > 1 else q)
        ks.append(jnp.concatenate([kp_ref[j], kc_ref[j]], axis=0).astype(BF16))
        vs.append(jnp.concatenate([vp_ref[j], vc_ref[j]], axis=0).astype(BF16))
    groups = [(j, h0) for j in range(nb) for h0 in range(0, SWA_HEADS, stack)]
    wave = len(groups) if stack > 1 else 1
    outs = {}
    for w0 in range(0, len(groups), wave):
        scores, sinks = [], []
        for j, h0 in groups[w0:w0 + wave]:
            heads = range(h0, h0 + stack)
            qg = jnp.concatenate([qs[j][:, h * SWA_HEAD_DIM:(h + 1) * SWA_HEAD_DIM] for h in heads], axis=0)
            sinks.append(jnp.concatenate([jnp.full((lq, 1), sink_ref[h], F32) for h in heads], axis=0))
            scores.append(_dot_nt(qg.astype(BF16), ks[j][:, cols(h0)]))
        probs, denoms = [], []
        for s, sink in zip(scores, sinks):
            s = jnp.where(mask, s, NEG_INF)
            m = jnp.maximum(jnp.max(s, axis=-1, keepdims=True), sink)
            p = jnp.exp(s - m)
            denoms.append(jnp.sum(p, axis=-1, keepdims=True) + jnp.exp(sink - m))
            probs.append(p.astype(BF16))
        for (j, h0), p, denom in zip(groups[w0:w0 + wave], probs, denoms):
            o = _dot(p, vs[j][:, cols(h0)]) / denom
            outs.setdefault(j, []).extend(o[g * lq:(g + 1) * lq] for g in range(stack))
    for j in range(nb):
        o_ref[j] = jnp.concatenate(outs[j], axis=-1).astype(BF16)


def _swa(sinks, qa, ka, va, k_past, v_past, *, has_past):
    B, L, _ = qa.shape
    lq = min(L, WINDOW)
    nl = L // lq
    if has_past:
        assert nl == 1
        nb = min(B, SAMPLE_BATCH)
        past_spec = pl.BlockSpec((nb, WINDOW, KV_WIDTH), lambda b, n, s: (b, 0, 0))
        kp, vp = k_past, v_past
    else:
        assert lq == WINDOW
        nb = 1
        past_spec = pl.BlockSpec((nb, WINDOW, KV_WIDTH), lambda b, n, s: (b, jnp.maximum(n - 1, 0), 0))
        kp, vp = ka, va
    assert B % nb == 0
    cur = lambda w: pl.BlockSpec((nb, lq, w), lambda b, n, s: (b, n, 0))
    return pl.pallas_call(
        functools.partial(_swa_kernel, first_block_has_past=has_past),
        grid_spec=pltpu.PrefetchScalarGridSpec(
            num_scalar_prefetch=1,
            grid=(B // nb, nl),
            in_specs=[cur(A_WIDTH), cur(KV_WIDTH), cur(KV_WIDTH), past_spec, past_spec],
            out_specs=cur(A_WIDTH),
        ),
        out_shape=jax.ShapeDtypeStruct((B, L, A_WIDTH), BF16),
        compiler_params=_cparams(("parallel", "parallel")),
        name="swa",
    )(sinks, qa, ka, va, kp, vp)


def _gla_kernel(q_ref, k_ref, v_ref, la_ref, rb_ref, gn_ref, s0_ref, o_ref, sout_ref, s_scr, *, chunk, sub, length):
    C, SUB = chunk, sub
    S = C // SUB
    W = GLA_K_WIDTH
    nb = q_ref.shape[0]

    @pl.when(pl.program_id(1) == 0)
    def _():
        s_scr[...] = s0_ref[...]

    row = lax.broadcasted_iota(I32, (C, C), 0)
    col = lax.broadcasted_iota(I32, (C, C), 1)
    tri = (row >= col).astype(BF16)
    diag_mask = (row >= col) & ((row // SUB) == (col // SUB))
    eye = lax.broadcasted_iota(I32, (GLA_DK, GLA_DK), 0) == lax.broadcasted_iota(I32, (GLA_DK, GLA_DK), 1)
    krow = lax.broadcasted_iota(I32, (C, W), 0)

    def rows_of(x, r):
        return jnp.broadcast_to(x[r:r + 1, :], (SUB, W))

    ksl = lambda h: slice(h * GLA_DK, (h + 1) * GLA_DK)
    vsl = lambda h: slice(h * GLA_DV, (h + 1) * GLA_DV)

    def decays(j, r0):
        g_hi, g_lo = _split_bf16(la_ref[j, pl.ds(r0, C), :])
        return _dot(tri, g_hi) + _dot(tri, g_lo)

    def factors(j, r0, b):
        q = q_ref[j, pl.ds(r0, C), :].astype(F32)
        k = k_ref[j, pl.ds(r0, C), :].astype(F32)
        b_last = b[C - 1:C, :]
        mid = jnp.concatenate([rows_of(b, i * SUB + SUB // 2 - 1) for i in range(S)], axis=0)
        beta = jnp.concatenate([jnp.zeros((SUB, W), F32)] + [rows_of(b, i * SUB - 1) for i in range(1, S)], axis=0)
        k_off = [None]
        for i in range(1, S):
            e = jnp.exp(jnp.minimum(b[i * SUB - 1:i * SUB, :] - b, 0.0))
            k_off.append(jnp.where(krow < i * SUB, k * e, 0.0).astype(BF16))
        return dict(q_inter=(q * jnp.exp(b)).astype(BF16), k_state=(k * jnp.exp(b_last - b)).astype(BF16),
                    q_diag=(q * jnp.exp(b - mid)).astype(BF16), k_diag=(k * jnp.exp(mid - b)).astype(BF16),
                    q_off=(q * jnp.exp(b - beta)).astype(BF16), k_off=k_off, dec=jnp.exp(b_last))

    def intra(f, h):
        a_diag = _dot_nt(f["q_diag"][:, ksl(h)], f["k_diag"][:, ksl(h)])
        if S == 1:
            return jnp.where(diag_mask, a_diag, 0.0).astype(BF16)
        blocks = [jnp.zeros((SUB, C), F32)]
        for i in range(1, S):
            blocks.append(_dot_nt(f["q_off"][i * SUB:(i + 1) * SUB, ksl(h)], f["k_off"][i][:, ksl(h)]))
        return jnp.where(diag_mask, a_diag, jnp.concatenate(blocks, axis=0)).astype(BF16)

    def read_out(j, r0, f, a):
        v = v_ref[j, pl.ds(r0, C), :]
        res = []
        for h in range(GLA_HEADS):
            o_h = _dot(a[h], v[:, vsl(h)]) + _dot(f["q_inter"][:, ksl(h)], s_scr[j, h].astype(BF16))
            res.append((o_h, _dot_tn(f["k_state"][:, ksl(h)], v[:, vsl(h)])))
        return res

    def finish(j, r0, f, res):
        outs = []
        for h, (o_h, s_inc) in enumerate(res):
            dec = jnp.broadcast_to(f["dec"][:, ksl(h)], (GLA_DK, GLA_DK))
            dec_col = jnp.sum(jnp.where(eye, dec, 0.0), axis=-1, keepdims=True)
            s_scr[j, h] = dec_col * s_scr[j, h] + s_inc
            ms = jnp.mean(o_h * o_h, axis=-1, keepdims=True)
            outs.append(o_h * lax.rsqrt(ms + NORM_EPS) * gn_ref[...])
        o = jnp.concatenate(outs, axis=-1) * rb_ref[j, pl.ds(r0, C), :].astype(F32)
        o_ref[j, pl.ds(r0, C), :] = o.astype(BF16)

    def step(c):
        r0 = c * C if isinstance(c, int) else pl.multiple_of(c * C, C)
        bs = [decays(j, r0) for j in range(nb)]
        fs = [factors(j, r0, b) for j, b in enumerate(bs)]
        attn = [[intra(f, h) for h in range(GLA_HEADS)] for f in fs]
        res = [read_out(j, r0, fs[j], attn[j]) for j in range(nb)]
        for j in range(nb):
            finish(j, r0, fs[j], res[j])

    if length == C:
        step(0)
    else:
        lax.fori_loop(0, length // C, lambda c, carry: (step(c), carry)[1], 0)

    @pl.when(pl.program_id(1) == pl.num_programs(1) - 1)
    def _():
        sout_ref[...] = s_scr[...]


def _gla(qb, kb, vb, la, rb, gn, s0):
    B, L, _ = qb.shape
    chunk = GLA_CHUNK if L % GLA_CHUNK == 0 else L
    sub = GLA_SUB if chunk % GLA_SUB == 0 else chunk
    nb = min(B, SAMPLE_BATCH) if L == chunk else min(B, GLA_LONG_BATCH)
    lt = min(L, GLA_SEQ_TILE)
    assert B % nb == 0 and L % lt == 0 and lt % chunk == 0
    seq = lambda w: pl.BlockSpec((nb, lt, w), lambda b, l: (b, l, 0))
    st = pl.BlockSpec((nb, GLA_HEADS, GLA_DK, GLA_DV), lambda b, l: (b, 0, 0, 0))
    return pl.pallas_call(
        functools.partial(_gla_kernel, chunk=chunk, sub=sub, length=lt),
        grid=(B // nb, L // lt),
        in_specs=[seq(GLA_K_WIDTH), seq(GLA_K_WIDTH), seq(B_WIDTH), seq(GLA_K_WIDTH), seq(B_WIDTH),
                  pl.BlockSpec((1, GLA_DV), lambda b, l: (0, 0)), st],
        out_specs=[seq(B_WIDTH), st],
        out_shape=[jax.ShapeDtypeStruct((B, L, B_WIDTH), BF16),
                   jax.ShapeDtypeStruct((B, GLA_HEADS, GLA_DK, GLA_DV), F32)],
        scratch_shapes=[pltpu.VMEM((nb, GLA_HEADS, GLA_DK, GLA_DV), F32)],
        compiler_params=_cparams(("parallel", "arbitrary")),
        name="gla",
    )(qb, kb, vb, la, rb, gn, s0)


def _outproj_kernel(x_ref, oa_ref, ob_ref, ga_ref, gb_ref, wa_ref, wb_ref, wo_ref, n2_ref, wr_ref, br_ref, upper_ref,
                    lower_ref, h_ref, hn_ref, g_ref, lp_ref, cnt_ref):
    tm = x_ref.shape[0]
    merged = (ga_ref[...].astype(F32) * _dot(oa_ref[...], wa_ref[...])
              + gb_ref[...].astype(F32) * _dot(ob_ref[...], wb_ref[...]))
    h = x_ref[...] + _dot(merged.astype(BF16), wo_ref[...])
    h_ref[...] = h
    ms = jnp.mean(h * h, axis=-1, keepdims=True)
    hn = h * lax.rsqrt(ms + NORM_EPS) * n2_ref[...]
    hn_bf = hn.astype(BF16)
    hn_ref[...] = hn_bf
    logits = _dot(hn_bf, wr_ref[...]) + br_ref[...]
    lt = logits.T[:N_EXPERTS, :]
    eid = lax.broadcasted_iota(I32, (N_EXPERTS, tm), 0)
    vals, hots = [], []
    for k in range(TOP_K):
        m = jnp.max(lt, axis=0, keepdims=True)
        idx = jnp.min(jnp.where(lt == m, eid, N_EXPERTS), axis=0, keepdims=True)
        hot = eid == idx
        lt = jnp.where(hot, -jnp.inf, lt)
        vals.append(m)
        hots.append(hot)
    ex = [jnp.exp(vk - vals[0]) for vk in vals]
    den = ex[0] + ex[1] + ex[2] + ex[3]
    for k in range(TOP_K):
        g_ref[k:k + 1, :] = ex[k] / den
    multi = (hots[0] | hots[1] | hots[2] | hots[3]).astype(BF16)
    before = _dot(multi, upper_ref[...])
    counts = jnp.sum(multi.astype(F32), axis=1, keepdims=True)
    aligned = jnp.floor((counts + (RUN_ALIGN - 1)) * (1.0 / RUN_ALIGN)) * RUN_ALIGN
    run_start = _dot(lower_ref[...], jnp.broadcast_to(aligned, (N_EXPERTS, LANE)).astype(BF16))[:, :1]
    place = before + run_start
    for k in range(TOP_K):
        lp_ref[k:k + 1, :] = jnp.sum(jnp.where(hots[k], place, 0.0), axis=0, keepdims=True).astype(I32)
    cnt_ref[...] = jnp.broadcast_to(aligned, cnt_ref.shape).astype(I32)


def _outproj(x2, oa, ob, ga, gb, wa, wb, wo, n2, wr, br, upper, lower):
    T = x2.shape[0]
    tm = TOK_TILE
    assert T % tm == 0 and upper.shape == (tm, tm)
    tok = lambda w: pl.BlockSpec((tm, w), lambda i: (i, 0))
    const = lambda shape: pl.BlockSpec(shape, lambda i: (0, 0))
    kt = pl.BlockSpec((TOP_K, tm), lambda i: (0, i))
    return pl.pallas_call(
        _outproj_kernel,
        grid=(T // tm,),
        in_specs=[tok(D_MODEL), tok(A_WIDTH), tok(B_WIDTH), tok(D_MODEL), tok(D_MODEL),
                  const((A_WIDTH, D_MODEL)), const((B_WIDTH, D_MODEL)), const((D_MODEL, D_MODEL)),
                  const((1, D_MODEL)), const((D_MODEL, LANE)), const((1, LANE)), const((tm, tm)),
                  const((N_EXPERTS, N_EXPERTS))],
        out_specs=[tok(D_MODEL), tok(D_MODEL), kt, kt, pl.BlockSpec((N_EXPERTS, LANE), lambda i: (i, 0))],
        out_shape=[jax.ShapeDtypeStruct((T, D_MODEL), F32), jax.ShapeDtypeStruct((T, D_MODEL), BF16),
                   jax.ShapeDtypeStruct((TOP_K, T), F32), jax.ShapeDtypeStruct((TOP_K, T), I32),
                   jax.ShapeDtypeStruct((T // tm * N_EXPERTS, LANE), I32)],
        compiler_params=_cparams(("parallel",)),
        name="outproj",
    )(x2, oa, ob, ga, gb, wa, wb, wo, n2, wr, br, upper, lower)


def _pack_rows(v, *, is_bf16_valued=False):
    if not is_bf16_valued:
        v = v.astype(BF16).astype(F32)
    lo = pltpu.bitcast(v[:, :ROW_WORDS], U32) >> 16
    hi = pltpu.bitcast(v[:, ROW_WORDS:], U32) & jnp.uint32(0xFFFF0000)
    return lo | hi


def _unpack_rows(w):
    lo = pltpu.bitcast(w << 16, F32).astype(BF16)
    hi = pltpu.bitcast(w & jnp.uint32(0xFFFF0000), F32).astype(BF16)
    return lo, hi


def _start_runs(runs, tile, tile_rows, global_rows, sem, *, to_global):
    cnt_ref, off_ref, dest_ref, _ = runs

    def run(e):
        j = tile * N_EXPERTS + e
        n = pl.multiple_of(cnt_ref[j], RUN_ALIGN)
        local = tile_rows.at[pl.ds(pl.multiple_of(off_ref[j], RUN_ALIGN), n), :]
        far = global_rows.at[pl.ds(pl.multiple_of(dest_ref[j], RUN_ALIGN), n), :]
        return pltpu.make_async_copy(local, far, sem) if to_global else pltpu.make_async_copy(far, local, sem)

    for e in range(N_EXPERTS):
        pl.when(cnt_ref[tile * N_EXPERTS + e] > 0)(lambda e=e: run(e).start(priority=ROW_DMA_PRIORITY))


def _wait_runs(runs, tile, tile_rows, global_rows, sem):
    total = pl.multiple_of(runs[3][tile], RUN_ALIGN)
    pltpu.make_async_copy(global_rows.at[pl.ds(0, total), :], tile_rows.at[pl.ds(0, total), :], sem).wait()


def _dispatch_kernel(fill_ref, cnt_ref, off_ref, dest_ref, tot_ref, lp_ref, hn_p_ref, hn_s_ref, xs_ref,
                     rows2, sems, zbuf, zsem, *, steps_p):
    i = pl.program_id(0)
    runs = (cnt_ref, off_ref, dest_ref, tot_ref)
    slot = i % 2
    rows = rows2.at[slot]

    @pl.when(i == 0)
    def _():
        zbuf[...] = jnp.zeros_like(zbuf)

        def fill_copy(j):
            return pltpu.make_async_copy(zbuf, xs_ref.at[pl.ds(pl.multiple_of(fill_ref[j], MOE_BM), MOE_BM), :], zsem)

        for j in range(fill_ref.shape[0]):
            pl.when(fill_ref[j] >= 0)(lambda j=j: fill_copy(j).start())
        for j in range(fill_ref.shape[0]):
            pl.when(fill_ref[j] >= 0)(lambda j=j: fill_copy(j).wait())

    def sort_tile(hn_ref):
        hn = hn_ref[...]
        r = lax.broadcasted_iota(I32, (SORT_CHUNK, TOK_TILE), 0).astype(I16)
        for c in range(SORT_ROWS // SORT_CHUNK):
            lp = (lp_ref[...] - c * SORT_CHUNK).astype(I16)
            hit = (r == lp[0:1, :]) | (r == lp[1:2, :]) | (r == lp[2:3, :]) | (r == lp[3:4, :])
            perm = jnp.where(hit, jnp.ones((), BF16), jnp.zeros((), BF16))
            rows[c * SORT_CHUNK:(c + 1) * SORT_CHUNK, :] = _pack_rows(_dot(perm, hn), is_bf16_valued=True)

    pl.when(i < steps_p)(lambda: sort_tile(hn_p_ref))
    pl.when(i >= steps_p)(lambda: sort_tile(hn_s_ref))
    _start_runs(runs, i, rows, xs_ref, sems.at[slot], to_global=True)
    pl.when(i > 0)(lambda: _wait_runs(runs, i - 1, rows2.at[1 - slot], xs_ref, sems.at[1 - slot]))
    pl.when(i == pl.num_programs(0) - 1)(lambda: _wait_runs(runs, i, rows, xs_ref, sems.at[slot]))


def _dispatch(fill, runs, lp, hn_p, hn_s, n_rows):
    tm = TOK_TILE
    steps_p, steps_s = hn_p.shape[0] // tm, hn_s.shape[0] // tm
    return pl.pallas_call(
        functools.partial(_dispatch_kernel, steps_p=steps_p),
        grid_spec=pltpu.PrefetchScalarGridSpec(
            num_scalar_prefetch=5,
            grid=(steps_p + steps_s,),
            in_specs=[pl.BlockSpec((TOP_K, tm), lambda i, *_: (0, i)),
                      pl.BlockSpec((tm, D_MODEL), lambda i, *_: (jnp.minimum(i, steps_p - 1), 0)),
                      pl.BlockSpec((tm, D_MODEL), lambda i, *_: (jnp.maximum(i - steps_p, 0), 0))],
            out_specs=pl.BlockSpec(memory_space=pl.ANY),
            scratch_shapes=[pltpu.VMEM((2, SORT_ROWS, ROW_WORDS), U32), pltpu.SemaphoreType.DMA((2,)),
                            pltpu.VMEM((MOE_BM, ROW_WORDS), U32), pltpu.SemaphoreType.DMA],
        ),
        out_shape=jax.ShapeDtypeStruct((n_rows, ROW_WORDS), U32),
        compiler_params=_cparams(("arbitrary",)),
        name="dispatch",
    )(fill, *runs, lp, hn_p, hn_s)


def _experts_kernel(be_ref, rows_ref, nu_ref, xs_ref, w1_ref, b1g_ref, b1l_ref, w2_ref, b2_ref, sel_ref, y_ref,
                    w1g_scr, w1l_scr, w2_scr):
    del nu_ref
    i = pl.program_id(0)
    used = rows_ref[i]

    @pl.when((used > 0) & ((i == 0) | (be_ref[i] != be_ref[jnp.maximum(i - 1, 0)])))
    def _():
        for c in range(D_FF // LANE):
            cols = _dot(w1_ref[0, :, c * 2 * LANE:(c + 1) * 2 * LANE].astype(BF16), sel_ref[...])
            w1g_scr[:, c * LANE:(c + 1) * LANE] = cols[:, :LANE].astype(BF16)
            w1l_scr[:, c * LANE:(c + 1) * LANE] = cols[:, LANE:].astype(BF16)
        w2_scr[...] = w2_ref[0].astype(BF16)

    def mlp(words):
        x_lo, x_hi = _unpack_rows(words)

        def up(w_scr, b_ref):
            return _dot(x_lo, w_scr[:ROW_WORDS, :]) + _dot(x_hi, w_scr[ROW_WORDS:, :]) + b_ref[0]

        x_glu = jnp.minimum(up(w1g_scr, b1g_ref), SWIGLU_LIMIT)
        x_lin = jnp.clip(up(w1l_scr, b1l_ref), -SWIGLU_LIMIT, SWIGLU_LIMIT)
        act = x_glu * jax.nn.sigmoid(SWIGLU_ALPHA * x_glu) * (x_lin + 1.0)
        return _pack_rows(_dot(act.astype(BF16), w2_scr[...]) + b2_ref[0])

    half = MOE_BM // 2

    @pl.when(used > half)
    def _():
        y_ref[...] = mlp(xs_ref[...])

    @pl.when((used > 0) & (used <= half))
    def _():
        y_ref[:half, :] = mlp(xs_ref[:half, :])
        y_ref[half:, :] = jnp.zeros((half, ROW_WORDS), U32)

    @pl.when(used == 0)
    def _():
        y_ref[...] = jnp.zeros_like(y_ref)


def _experts(blk_e, blk_rows, n_used, xs, w1, b1g, b1l, w2, b2, sel):
    nblk = xs.shape[0] // MOE_BM
    wspec = lambda r, c: pl.BlockSpec((1, r, c), lambda i, be, rows, nu: (be[i], 0, 0))
    return pl.pallas_call(
        _experts_kernel,
        grid_spec=pltpu.PrefetchScalarGridSpec(
            num_scalar_prefetch=3,
            grid=(nblk,),
            in_specs=[pl.BlockSpec((MOE_BM, ROW_WORDS), lambda i, be, rows, nu: (jnp.minimum(i, nu[0] - 1), 0)),
                      wspec(D_MODEL, 2 * D_FF), wspec(1, D_FF), wspec(1, D_FF),
                      wspec(D_FF, D_MODEL), wspec(1, D_MODEL),
                      pl.BlockSpec((2 * LANE, 2 * LANE), lambda i, be, rows, nu: (0, 0))],
            out_specs=pl.BlockSpec((MOE_BM, ROW_WORDS), lambda i, be, rows, nu: (i, 0)),
            scratch_shapes=[pltpu.VMEM((D_MODEL, D_FF), BF16), pltpu.VMEM((D_MODEL, D_FF), BF16),
                            pltpu.VMEM((D_FF, D_MODEL), BF16)],
        ),
        out_shape=jax.ShapeDtypeStruct(xs.shape, U32),
        compiler_params=_cparams(("arbitrary",)),
        name="experts",
    )(blk_e, blk_rows, n_used, xs, w1, b1g, b1l, w2, b2, sel)


def _combine_kernel(cnt_ref, off_ref, dest_ref, tot_ref, lp_ref, g_ref, h_ref, yb_ref, y_ref, rows2, sems, *, tile0):
    i = pl.program_id(0)
    runs = (cnt_ref, off_ref, dest_ref, tot_ref)
    slot = i % 2
    rows = rows2.at[slot]

    def fetch(step, into):
        _start_runs(runs, step + tile0, rows2.at[into], yb_ref, sems.at[into], to_global=False)

    @pl.when(i == 0)
    def _():
        rows2[...] = jnp.zeros_like(rows2)
        fetch(0, 0)

    pl.when(i + 1 < pl.num_programs(0))(lambda: fetch(i + 1, 1 - slot))
    _wait_runs(runs, i + tile0, rows, yb_ref, sems.at[slot])
    lp, g = lp_ref[...].astype(I16), g_ref[...].astype(BF16)
    r = lax.broadcasted_iota(I32, (TOK_TILE, SORT_ROWS), 1).astype(I16)
    mix = jnp.zeros((TOK_TILE, SORT_ROWS), BF16)
    for k in range(TOP_K):
        mix = jnp.where(r == lp[:, k:k + 1], g[:, k:k + 1], mix)
    y_lo, y_hi = _unpack_rows(rows[...])
    y_ref[...] = h_ref[...] + jnp.concatenate([_dot(mix, y_lo), _dot(mix, y_hi)], axis=-1)


def _combine(runs, lp_t, gate_t, h, yb, tile0):
    T = h.shape[0]
    tm = TOK_TILE
    tok = lambda w: pl.BlockSpec((tm, w), lambda i, *_: (i, 0))
    return pl.pallas_call(
        functools.partial(_combine_kernel, tile0=tile0),
        grid_spec=pltpu.PrefetchScalarGridSpec(
            num_scalar_prefetch=4,
            grid=(T // tm,),
            in_specs=[tok(TOP_K), tok(TOP_K), tok(D_MODEL), pl.BlockSpec(memory_space=pl.ANY)],
            out_specs=tok(D_MODEL),
            scratch_shapes=[pltpu.VMEM((2, SORT_ROWS, ROW_WORDS), U32), pltpu.SemaphoreType.DMA((2,))],
        ),
        out_shape=jax.ShapeDtypeStruct((T, D_MODEL), F32),
        compiler_params=_cparams(("arbitrary",)),
        name="combine",
    )(*runs, lp_t, gate_t, h, yb)


def _block_ones(width, block):
    idx = np.arange(width) // block
    return jnp.asarray(idx[:, None] == idx[None, :], BF16)


def _layer(x_p, x_s, cache_k, cache_v, s_gla, norm1, w_in, q_norm, k_norm, sinks, w_gk, b_gk,
           gla_norm, w_a, w_b, w_o, norm2, w_router, b_router, w1, b1, w2, b2):
    Bp, Lp, _ = x_p.shape
    Bs, Ls, _ = x_s.shape
    Tp, Ts = Bp * Lp, Bs * Ls

    w_packed = _pack_w_in(w_in)
    wgk = jnp.pad(w_gk, ((0, GL_PAD - GLA_GATE_RANK), (0, 0))).astype(BF16)
    n1 = norm1.reshape(1, D_MODEL)
    gq, gk = _block_ones(A_WIDTH, SWA_HEAD_DIM), _block_ones(KV_WIDTH, SWA_HEAD_DIM)
    qn = jnp.tile(q_norm, SWA_HEADS).reshape(1, A_WIDTH)
    kn = jnp.tile(k_norm, SWA_KV_HEADS).reshape(1, KV_WIDTH)
    bgk = b_gk.reshape(1, GLA_K_WIDTH)
    gn = gla_norm.reshape(1, GLA_DV)
    wa, wb, wo = w_a.astype(BF16), w_b.astype(BF16), w_o.astype(BF16)
    n2 = norm2.reshape(1, D_MODEL)
    wr = jnp.pad(w_router, ((0, 0), (0, LANE - N_EXPERTS))).astype(BF16)
    br = jnp.pad(b_router, (0, LANE - N_EXPERTS)).reshape(1, LANE)
    b1g = b1[:, 0::2].reshape(N_EXPERTS, 1, D_FF)
    b1l = b1[:, 1::2].reshape(N_EXPERTS, 1, D_FF)
    b2r = b2.reshape(N_EXPERTS, 1, D_MODEL)
    sel_np = np.zeros((2 * LANE, 2 * LANE), np.float32)
    sel_np[2 * np.arange(LANE), np.arange(LANE)] = 1.0
    sel_np[2 * np.arange(LANE) + 1, LANE + np.arange(LANE)] = 1.0
    sel = jnp.asarray(sel_np, BF16)

    def mix(x, k_past, v_past, s0):
        B, L, _ = x.shape
        T = B * L
        x2 = x.reshape(T, D_MODEL)
        qa, ka, va, qb, kb, vb, la, rb, ga, gb = _inproj(x2, n1, w_packed, gq, gk, qn, kn, wgk, bgk)
        r3 = lambda t: t.reshape(B, L, t.shape[-1])
        oa = _swa(sinks, r3(qa), r3(ka), r3(va), k_past, v_past, has_past=k_past is not None)
        ob, s_out = _gla(r3(qb), r3(kb), r3(vb), r3(la), r3(rb), gn, s0)
        h, hn, gate, lp, cnt = _outproj(x2, oa.reshape(T, A_WIDTH), ob.reshape(T, B_WIDTH), ga, gb,
                                        wa, wb, wo, n2, wr, br, upper, lower)
        return r3(ka), r3(va), s_out, h, hn, gate, lp, cnt[:, 0].reshape(T // TOK_TILE, N_EXPERTS)

    upper = jnp.asarray(np.arange(TOK_TILE)[:, None] < np.arange(TOK_TILE)[None, :], BF16)
    lower = jnp.asarray(np.arange(N_EXPERTS)[:, None] > np.arange(N_EXPERTS)[None, :], BF16)
    win = cache_k.shape[1]
    assert win == WINDOW
    ck = cache_k.reshape(Bs, win, KV_WIDTH)
    cv = cache_v.reshape(Bs, win, KV_WIDTH)
    ka_p, va_p, s_p, h_p, hn_p, g_p, lp_p, cnt_p = mix(x_p, None, None, jnp.zeros((Bp, GLA_HEADS, GLA_DK, GLA_DV), F32))
    ka_s, va_s, s_s, h_s, hn_s, g_s, lp_s, cnt_s = mix(x_s, ck, cv, s_gla)

    T = Tp + Ts
    cnt = jnp.concatenate([cnt_p, cnt_s], axis=0)
    most_rows = T * TOP_K + cnt.size * (RUN_ALIGN - 1) + N_EXPERTS * (MOE_BM - 1)
    n_rows = -(-most_rows // MOE_BM) * MOE_BM
    counts = jnp.sum(cnt, axis=0)
    padded = (counts + MOE_BM - 1) // MOE_BM * MOE_BM
    pad_end = jnp.cumsum(padded)
    start = pad_end - padded
    dest = start[None, :] + jnp.cumsum(cnt, axis=0) - cnt
    off = jnp.cumsum(cnt, axis=1) - cnt
    flat = lambda a: a.reshape(-1).astype(I32)
    runs = (flat(cnt), flat(off), flat(dest), flat(jnp.sum(cnt, axis=1)))
    tails = jnp.where(padded > 0, pad_end - MOE_BM, -1)
    most_spare = (n_rows - T * TOP_K) // MOE_BM
    spare = pad_end[-1] + jnp.arange(most_spare, dtype=I32) * MOE_BM
    fill = jnp.concatenate([tails, jnp.where(spare < n_rows, spare, -1)]).astype(I32)
    blk_row = jnp.arange(n_rows // MOE_BM, dtype=I32) * MOE_BM
    blk_e = jnp.minimum(jnp.sum(pad_end[None, :] <= blk_row[:, None], axis=1), N_EXPERTS - 1).astype(I32)
    used_end = start + counts
    blk_rows = jnp.clip(jnp.sum(jnp.where(blk_e[:, None] == jnp.arange(N_EXPERTS)[None, :], used_end[None, :], 0),
                                axis=1) - blk_row, 0, MOE_BM)
    blk_rows = jnp.where(blk_row < pad_end[-1], blk_rows, 0).astype(I32)
    n_used = (pad_end[-1:] // MOE_BM).astype(I32)

    xs = _dispatch(fill, runs, jnp.concatenate([lp_p, lp_s], axis=1), hn_p, hn_s, n_rows)
    yb = _experts(blk_e, blk_rows, n_used, xs, w1, b1g, b1l, w2, b2r, sel)
    y_p = _combine(runs, lp_p.T, g_p.T, h_p, yb, 0).reshape(Bp, Lp, D_MODEL)
    y_s = _combine(runs, lp_s.T, g_s.T, h_s, yb, Tp // TOK_TILE).reshape(Bs, Ls, D_MODEL)

    kv = lambda t: t.reshape(t.shape[0], t.shape[1], SWA_KV_HEADS, SWA_HEAD_DIM)
    k_p, v_p = kv(ka_p[:, -WINDOW:]), kv(va_p[:, -WINDOW:])
    k_s = jnp.concatenate([cache_k, kv(ka_s)], axis=1)[:, -win:]
    v_s = jnp.concatenate([cache_v, kv(va_s)], axis=1)[:, -win:]
    return y_p, y_s, (k_p, v_p, s_p, k_s, v_s, s_s)


def kernel(x_prompt, x_sample, cache_swa_k, cache_swa_v, state_gla, norm1, w_in, q_norm, k_norm, sinks, w_gk, b_gk,
           gla_norm, w_a, w_b, w_o, norm2, w_router, b_router, w1, b1, w2, b2):
    depth = norm1.shape[0]
    y_p, y_s = x_prompt, x_sample
    states = []
    for l in range(depth):
        y_p, y_s, st = _layer(y_p, y_s, cache_swa_k[l], cache_swa_v[l], state_gla[l], norm1[l], w_in[l], q_norm[l],
                              k_norm[l], sinks[l], w_gk[l], b_gk[l], gla_norm[l], w_a[l], w_b[l], w_o[l], norm2[l],
                              w_router[l], b_router[l], w1[l], b1[l], w2[l], b2[l])
        states.append(st)
    return (y_p, y_s) + tuple(jnp.stack([st[j] for st in states]) for j in range(6))
```

```python
import functools

import jax
import jax.numpy as jnp
import numpy as np
from jax import lax
from jax.experimental import pallas as pl
from jax.experimental.pallas import tpu as pltpu

F32 = jnp.float32
BF16 = jnp.bfloat16
I32 = jnp.int32
U32 = jnp.uint32
I16 = jnp.int16

D_MODEL = 1024
SWA_HEADS = 8
SWA_KV_HEADS = 2
SWA_GROUP = SWA_HEADS // SWA_KV_HEADS
SWA_HEAD_DIM = 64
WINDOW = 128
ATTN_SCALE = SWA_HEAD_DIM ** -0.5
GLA_HEADS = 4
GLA_DK = 64
GLA_DV = 128
GLA_GATE_RANK = 16
GLA_GATE_NORM = 16.0
GLA_SCALE = GLA_DK ** -0.5
A_WIDTH = SWA_HEADS * SWA_HEAD_DIM
KV_WIDTH = SWA_KV_HEADS * SWA_HEAD_DIM
GLA_K_WIDTH = GLA_HEADS * GLA_DK
B_WIDTH = GLA_HEADS * GLA_DV
N_EXPERTS = 32
TOP_K = 4
D_FF = D_MODEL
SWIGLU_ALPHA = 1.702
SWIGLU_LIMIT = 7.0
NORM_EPS = 1e-5
QK_EPS = 1e-6
NEG_INF = -1e30

LANE = 128
GL_PAD = LANE
VMEM_LIMIT = 56 * 1024 * 1024

_SEGS = (("qa", A_WIDTH), ("ka", KV_WIDTH), ("va", KV_WIDTH), ("qb", GLA_K_WIDTH), ("kb", GLA_K_WIDTH),
         ("vb", B_WIDTH), ("rb", B_WIDTH), ("ga", D_MODEL), ("gb", D_MODEL), ("gl", GL_PAD))
_OFF = {}
_o = 0
for _n, _w in _SEGS:
    _OFF[_n] = (_o, _w)
    _o += _w
IN_PACKED = _o

TOK_TILE = 512
GLA_CHUNK = 64
GLA_SUB = 16
MOE_BM = 512
EXPERT_PIECE_LEAD = (3, 2, 1)
ROW_DMA_PRIORITY = 1
ROW_WORDS = D_MODEL // 2
RUN_ALIGN = 8
SORT_CHUNK = 256
SORT_ROWS = -(-(TOK_TILE * TOP_K + N_EXPERTS * (RUN_ALIGN - 1)) // SORT_CHUNK) * SORT_CHUNK
SAMPLE_BATCH = 8
GLA_LONG_BATCH = 8
GLA_SEQ_TILE = 512


def _cparams(sem):
    return pltpu.CompilerParams(dimension_semantics=sem, vmem_limit_bytes=VMEM_LIMIT)


def _split_bf16(v):
    hi = v.astype(BF16)
    lo = (v - hi.astype(F32)).astype(BF16)
    return hi, lo


def _dot(a, b):
    return jnp.dot(a, b, preferred_element_type=F32)


def _dot_nt(a, b):
    return lax.dot_general(a, b, (((1,), (1,)), ((), ())), preferred_element_type=F32)


def _dot_tn(a, b):
    return lax.dot_general(a, b, (((0,), (0,)), ((), ())), preferred_element_type=F32)


_SRC = {}
_o = 0
for _n, _w in (("qa", A_WIDTH), ("ka", KV_WIDTH), ("va", KV_WIDTH), ("qb", GLA_K_WIDTH), ("kb", GLA_K_WIDTH),
               ("vb", B_WIDTH), ("gl", GLA_GATE_RANK), ("rb", B_WIDTH), ("ga", D_MODEL), ("gb", D_MODEL)):
    _SRC[_n] = (_o, _w)
    _o += _w
IN_WIDTH = _o
PACK_ROWS = 128


def _pack_w_in_kernel(w_ref, o_ref):
    for name, width in _SEGS:
        src, have = _SRC[name]
        dst, _ = _OFF[name]
        o_ref[:, dst:dst + have] = w_ref[:, src:src + have].astype(BF16)
        if have < width:
            o_ref[:, dst + have:dst + width] = jnp.zeros((o_ref.shape[0], width - have), BF16)


def _pack_w_in(w_in):
    return pl.pallas_call(
        _pack_w_in_kernel,
        grid=(D_MODEL // PACK_ROWS,),
        in_specs=[pl.BlockSpec((PACK_ROWS, IN_WIDTH), lambda i: (i, 0))],
        out_specs=pl.BlockSpec((PACK_ROWS, IN_PACKED), lambda i: (i, 0)),
        out_shape=jax.ShapeDtypeStruct((D_MODEL, IN_PACKED), BF16),
        compiler_params=_cparams(("parallel",)),
        name="pack_w_in",
    )(w_in)


def _inproj_kernel(x_ref, n1_ref, w_ref, gq_ref, gk_ref, qn_ref, kn_ref, wgk_ref, bgk_ref,
                   qa_ref, ka_ref, va_ref, qb_ref, kb_ref, vb_ref, la_ref, rb_ref, ga_ref, gb_ref):
    x = x_ref[...]
    ms = jnp.mean(x * x, axis=-1, keepdims=True)
    xn = (x * lax.rsqrt(ms + NORM_EPS) * n1_ref[...]).astype(BF16)

    def proj(name):
        off, width = _OFF[name]
        return _dot(xn, w_ref[:, off:off + width])

    def head_norm(v, ones_ref, gain_ref):
        hi, lo = _split_bf16(v * v)
        ss = _dot(hi, ones_ref[...]) + _dot(lo, ones_ref[...])
        return v * lax.rsqrt(ss * (1.0 / SWA_HEAD_DIM) + QK_EPS) * gain_ref[...]

    def log_decay(gl):
        z = _dot(gl.astype(BF16), wgk_ref[...]) + bgk_ref[...]
        return (jnp.minimum(z, 0.0) - jnp.log1p(jnp.exp(-jnp.abs(z)))) * (1.0 / GLA_GATE_NORM)

    plan = (("qa", qa_ref, lambda u: head_norm(u, gq_ref, qn_ref) * ATTN_SCALE),
            ("ka", ka_ref, lambda u: head_norm(u, gk_ref, kn_ref)),
            ("va", va_ref, lambda u: u),
            ("qb", qb_ref, lambda u: u * GLA_SCALE),
            ("kb", kb_ref, lambda u: u),
            ("vb", vb_ref, lambda u: u),
            ("gl", la_ref, log_decay),
            ("rb", rb_ref, lambda u: u * jax.nn.sigmoid(u)),
            ("ga", ga_ref, jax.nn.sigmoid),
            ("gb", gb_ref, jax.nn.sigmoid))
    u = proj(plan[0][0])
    for idx, (_, out_ref, epilogue) in enumerate(plan):
        u_next = proj(plan[idx + 1][0]) if idx + 1 < len(plan) else None
        out_ref[...] = epilogue(u).astype(out_ref.dtype)
        u = u_next


def _inproj(x2, n1, w_packed, gq, gk, qn, kn, wgk, bgk):
    T = x2.shape[0]
    tm = min(TOK_TILE, T)
    assert T % tm == 0

    def tok(width):
        return pl.BlockSpec((tm, width), lambda i: (i, 0))

    def const(shape):
        return pl.BlockSpec(shape, lambda i: (0, 0))

    outs = (("qa", A_WIDTH, BF16), ("ka", KV_WIDTH, F32), ("va", KV_WIDTH, F32), ("qb", GLA_K_WIDTH, BF16),
            ("kb", GLA_K_WIDTH, BF16), ("vb", B_WIDTH, BF16), ("la", GLA_K_WIDTH, F32), ("rb", B_WIDTH, BF16),
            ("ga", D_MODEL, BF16), ("gb", D_MODEL, BF16))
    return pl.pallas_call(
        _inproj_kernel,
        grid=(T // tm,),
        in_specs=[tok(D_MODEL), const((1, D_MODEL)), const((D_MODEL, IN_PACKED)), const((A_WIDTH, A_WIDTH)),
                  const((KV_WIDTH, KV_WIDTH)), const((1, A_WIDTH)), const((1, KV_WIDTH)),
                  const((GL_PAD, GLA_K_WIDTH)), const((1, GLA_K_WIDTH))],
        out_specs=[tok(w) for _, w, _ in outs],
        out_shape=[jax.ShapeDtypeStruct((T, w), dt) for _, w, dt in outs],
        compiler_params=_cparams(("parallel",)),
        name="inproj",
    )(x2, n1, w_packed, gq, gk, qn, kn, wgk, bgk)


def _swa_kernel(sink_ref, q_ref, kc_ref, vc_ref, kp_ref, vp_ref, o_ref, *, first_block_has_past):
    n = pl.program_id(1)
    nb, lq = q_ref.shape[0], q_ref.shape[1]
    assert lq & (lq - 1) == 0
    stack = SWA_GROUP if lq < WINDOW else 1
    rows, keys = stack * lq, WINDOW + lq
    qi = lax.broadcasted_iota(I32, (rows, keys), 0) & (lq - 1)
    ci = lax.broadcasted_iota(I32, (rows, keys), 1)
    dist = WINDOW + qi - ci
    mask = (dist >= 0) & (dist < WINDOW)
    if not first_block_has_past:
        mask = mask & ((ci >= WINDOW) | (n > 0))
    cols = lambda h0: slice(h0 // SWA_GROUP * SWA_HEAD_DIM, (h0 // SWA_GROUP + 1) * SWA_HEAD_DIM)
    qs, ks, vs = [], [], []
    for j in range(nb):
        q = q_ref[j]
        qs.append(q.astype(F32) if stack > 1 else q)
        ks.append(jnp.concatenate([kp_ref[j], kc_ref[j]], axis=0).astype(BF16))
        vs.append(jnp.concatenate([vp_ref[j], vc_ref[j]], axis=0).astype(BF16))
    groups = [(j, h0) for j in range(nb) for h0 in range(0, SWA_HEADS, stack)]
    wave = len(groups) if stack > 1 else 1
    outs = {}
    for w0 in range(0, len(groups), wave):
        scores, sinks = [], []
        for j, h0 in groups[w0:w0 + wave]:
            heads = range(h0, h0 + stack)
            qg = jnp.concatenate([qs[j][:, h * SWA_HEAD_DIM:(h + 1) * SWA_HEAD_DIM] for h in heads], axis=0)
            sinks.append(jnp.concatenate([jnp.full((lq, 1), sink_ref[h], F32) for h in heads], axis=0))
            scores.append(_dot_nt(qg.astype(BF16), ks[j][:, cols(h0)]))
        probs, denoms = [], []
        for s, sink in zip(scores, sinks):
            s = jnp.where(mask, s, NEG_INF)
            m = jnp.maximum(jnp.max(s, axis=-1, keepdims=True), sink)
            p = jnp.exp(s - m)
            denoms.append(jnp.sum(p, axis=-1, keepdims=True) + jnp.exp(sink - m))
            probs.append(p.astype(BF16))
        for (j, h0), p, denom in zip(groups[w0:w0 + wave], probs, denoms):
            o = _dot(p, vs[j][:, cols(h0)]) / denom
            outs.setdefault(j, []).extend(o[g * lq:(g + 1) * lq] for g in range(stack))
    for j in range(nb):
        o_ref[j] = jnp.concatenate(outs[j], axis=-1).astype(BF16)


def _swa(sinks, qa, ka, va, k_past, v_past, *, has_past):
    B, L, _ = qa.shape
    lq = min(L, WINDOW)
    nl = L // lq
    if has_past:
        assert nl == 1
        nb = min(B, SAMPLE_BATCH)
        past_spec = pl.BlockSpec((nb, WINDOW, KV_WIDTH), lambda b, n, s: (b, 0, 0))
        kp, vp = k_past, v_past
    else:
        assert lq == WINDOW
        nb = 1
        past_spec = pl.BlockSpec((nb, WINDOW, KV_WIDTH), lambda b, n, s: (b, jnp.maximum(n - 1, 0), 0))
        kp, vp = ka, va
    assert B % nb == 0
    cur = lambda w: pl.BlockSpec((nb, lq, w), lambda b, n, s: (b, n, 0))
    return pl.pallas_call(
        functools.partial(_swa_kernel, first_block_has_past=has_past),
        grid_spec=pltpu.PrefetchScalarGridSpec(
            num_scalar_prefetch=1,
            grid=(B // nb, nl),
            in_specs=[cur(A_WIDTH), cur(KV_WIDTH), cur(KV_WIDTH), past_spec, past_spec],
            out_specs=cur(A_WIDTH),
        ),
        out_shape=jax.ShapeDtypeStruct((B, L, A_WIDTH), BF16),
        compiler_params=_cparams(("parallel", "parallel")),
        name="swa",
    )(sinks, qa, ka, va, kp, vp)


def _gla_kernel(q_ref, k_ref, v_ref, la_ref, rb_ref, gn_ref, s0_ref, o_ref, sout_ref, s_scr, *, chunk, sub, length):
    C, SUB = chunk, sub
    S = C // SUB
    W = GLA_K_WIDTH
    nb = q_ref.shape[0]

    @pl.when(pl.program_id(1) == 0)
    def _():
        s_scr[...] = s0_ref[...]

    row = lax.broadcasted_iota(I32, (C, C), 0)
    col = lax.broadcasted_iota(I32, (C, C), 1)
    tri = (row >= col).astype(BF16)
    diag_mask = (row >= col) & ((row // SUB) == (col // SUB))
    eye = lax.broadcasted_iota(I32, (GLA_DK, GLA_DK), 0) == lax.broadcasted_iota(I32, (GLA_DK, GLA_DK), 1)
    krow = lax.broadcasted_iota(I32, (C, W), 0)

    def rows_of(x, r):
        return jnp.broadcast_to(x[r:r + 1, :], (SUB, W))

    ksl = lambda h: slice(h * GLA_DK, (h + 1) * GLA_DK)
    vsl = lambda h: slice(h * GLA_DV, (h + 1) * GLA_DV)

    def decays(j, r0):
        g_hi, g_lo = _split_bf16(la_ref[j, pl.ds(r0, C), :])
        return _dot(tri, g_hi) + _dot(tri, g_lo)

    def factors(j, r0, b):
        q = q_ref[j, pl.ds(r0, C), :].astype(F32)
        k = k_ref[j, pl.ds(r0, C), :].astype(F32)
        b_last = b[C - 1:C, :]
        mid = jnp.concatenate([rows_of(b, i * SUB + SUB // 2 - 1) for i in range(S)], axis=0)
        beta = jnp.concatenate([jnp.zeros((SUB, W), F32)] + [rows_of(b, i * SUB - 1) for i in range(1, S)], axis=0)
        k_off = [None]
        for i in range(1, S):
            e = jnp.exp(jnp.minimum(b[i * SUB - 1:i * SUB, :] - b, 0.0))
            k_off.append(jnp.where(krow < i * SUB, k * e, 0.0).astype(BF16))
        return dict(q_inter=(q * jnp.exp(b)).astype(BF16), k_state=(k * jnp.exp(b_last - b)).astype(BF16),
                    q_diag=(q * jnp.exp(b - mid)).astype(BF16), k_diag=(k * jnp.exp(mid - b)).astype(BF16),
                    q_off=(q * jnp.exp(b - beta)).astype(BF16), k_off=k_off, dec=jnp.exp(b_last))

    def intra(f, h):
        a_diag = _dot_nt(f["q_diag"][:, ksl(h)], f["k_diag"][:, ksl(h)])
        if S == 1:
            return jnp.where(diag_mask, a_diag, 0.0).astype(BF16)
        blocks = [jnp.zeros((SUB, C), F32)]
        for i in range(1, S):
            blocks.append(_dot_nt(f["q_off"][i * SUB:(i + 1) * SUB, ksl(h)], f["k_off"][i][:, ksl(h)]))
        return jnp.where(diag_mask, a_diag, jnp.concatenate(blocks, axis=0)).astype(BF16)

    def read_out(j, r0, f, a):
        v = v_ref[j, pl.ds(r0, C), :]
        res = []
        for h in range(GLA_HEADS):
            o_h = _dot(a[h], v[:, vsl(h)]) + _dot(f["q_inter"][:, ksl(h)], s_scr[j, h].astype(BF16))
            res.append((o_h, _dot_tn(f["k_state"][:, ksl(h)], v[:, vsl(h)])))
        return res

    def finish(j, r0, f, res):
        outs = []
        for h, (o_h, s_inc) in enumerate(res):
            dec = jnp.broadcast_to(f["dec"][:, ksl(h)], (GLA_DK, GLA_DK))
            dec_col = jnp.sum(jnp.where(eye, dec, 0.0), axis=-1, keepdims=True)
            s_scr[j, h] = dec_col * s_scr[j, h] + s_inc
            ms = jnp.mean(o_h * o_h, axis=-1, keepdims=True)
            outs.append(o_h * lax.rsqrt(ms + NORM_EPS) * gn_ref[...])
        o = jnp.concatenate(outs, axis=-1) * rb_ref[j, pl.ds(r0, C), :].astype(F32)
        o_ref[j, pl.ds(r0, C), :] = o.astype(BF16)

    def step(c):
        r0 = c * C if isinstance(c, int) else pl.multiple_of(c * C, C)
        bs = [decays(j, r0) for j in range(nb)]
        fs = [factors(j, r0, b) for j, b in enumerate(bs)]
        attn = [[intra(f, h) for h in range(GLA_HEADS)] for f in fs]
        res = [read_out(j, r0, fs[j], attn[j]) for j in range(nb)]
        for j in range(nb):
            finish(j, r0, fs[j], res[j])

    if length == C:
        step(0)
    else:
        lax.fori_loop(0, length // C, lambda c, carry: (step(c), carry)[1], 0)

    @pl.when(pl.program_id(1) == pl.num_programs(1) - 1)
    def _():
        sout_ref[...] = s_scr[...]


def _gla(qb, kb, vb, la, rb, gn, s0):
    B, L, _ = qb.shape
    chunk = GLA_CHUNK if L % GLA_CHUNK == 0 else L
    sub = GLA_SUB if chunk % GLA_SUB == 0 else chunk
    nb = min(B, SAMPLE_BATCH) if L == chunk else min(B, GLA_LONG_BATCH)
    lt = min(L, GLA_SEQ_TILE)
    assert B % nb == 0 and L % lt == 0 and lt % chunk == 0
    seq = lambda w: pl.BlockSpec((nb, lt, w), lambda b, l: (b, l, 0))
    st = pl.BlockSpec((nb, GLA_HEADS, GLA_DK, GLA_DV), lambda b, l: (b, 0, 0, 0))
    return pl.pallas_call(
        functools.partial(_gla_kernel, chunk=chunk, sub=sub, length=lt),
        grid=(B // nb, L // lt),
        in_specs=[seq(GLA_K_WIDTH), seq(GLA_K_WIDTH), seq(B_WIDTH), seq(GLA_K_WIDTH), seq(B_WIDTH),
                  pl.BlockSpec((1, GLA_DV), lambda b, l: (0, 0)), st],
        out_specs=[seq(B_WIDTH), st],
        out_shape=[jax.ShapeDtypeStruct((B, L, B_WIDTH), BF16),
                   jax.ShapeDtypeStruct((B, GLA_HEADS, GLA_DK, GLA_DV), F32)],
        scratch_shapes=[pltpu.VMEM((nb, GLA_HEADS, GLA_DK, GLA_DV), F32)],
        compiler_params=_cparams(("parallel", "arbitrary")),
        name="gla",
    )(qb, kb, vb, la, rb, gn, s0)


def _outproj_kernel(x_ref, oa_ref, ob_ref, ga_ref, gb_ref, wa_ref, wb_ref, wo_ref, n2_ref, wr_ref, br_ref, upper_ref,
                    lower_ref, h_ref, hn_ref, g_ref, lp_ref, cnt_ref):
    tm = x_ref.shape[0]
    merged = (ga_ref[...].astype(F32) * _dot(oa_ref[...], wa_ref[...])
              + gb_ref[...].astype(F32) * _dot(ob_ref[...], wb_ref[...]))
    h = x_ref[...] + _dot(merged.astype(BF16), wo_ref[...])
    h_ref[...] = h
    ms = jnp.mean(h * h, axis=-1, keepdims=True)
    hn = h * lax.rsqrt(ms + NORM_EPS) * n2_ref[...]
    hn_bf = hn.astype(BF16)
    hn_ref[...] = hn_bf
    logits = _dot(hn_bf, wr_ref[...]) + br_ref[...]
    lt = logits.T[:N_EXPERTS, :]
    eid = lax.broadcasted_iota(I32, (N_EXPERTS, tm), 0)
    vals, hots = [], []
    for k in range(TOP_K):
        m = jnp.max(lt, axis=0, keepdims=True)
        idx = jnp.min(jnp.where(lt == m, eid, N_EXPERTS), axis=0, keepdims=True)
        hot = eid == idx
        lt = jnp.where(hot, -jnp.inf, lt)
        vals.append(m)
        hots.append(hot)
    ex = [jnp.exp(vk - vals[0]) for vk in vals]
    den = ex[0] + ex[1] + ex[2] + ex[3]
    for k in range(TOP_K):
        g_ref[k:k + 1, :] = ex[k] / den
    multi = (hots[0] | hots[1] | hots[2] | hots[3]).astype(BF16)
    before = _dot(multi, upper_ref[...])
    counts = jnp.sum(multi.astype(F32), axis=1, keepdims=True)
    aligned = jnp.floor((counts + (RUN_ALIGN - 1)) * (1.0 / RUN_ALIGN)) * RUN_ALIGN
    run_start = _dot(lower_ref[...], jnp.broadcast_to(aligned, (N_EXPERTS, LANE)).astype(BF16))[:, :1]
    place = before + run_start
    for k in range(TOP_K):
        lp_ref[k:k + 1, :] = jnp.sum(jnp.where(hots[k], place, 0.0), axis=0, keepdims=True).astype(I32)
    cnt_ref[...] = jnp.broadcast_to(aligned, cnt_ref.shape).astype(I32)


def _outproj(x2, oa, ob, ga, gb, wa, wb, wo, n2, wr, br, upper, lower):
    T = x2.shape[0]
    tm = TOK_TILE
    assert T % tm == 0 and upper.shape == (tm, tm)
    tok = lambda w: pl.BlockSpec((tm, w), lambda i: (i, 0))
    const = lambda shape: pl.BlockSpec(shape, lambda i: (0, 0))
    kt = pl.BlockSpec((TOP_K, tm), lambda i: (0, i))
    return pl.pallas_call(
        _outproj_kernel,
        grid=(T // tm,),
        in_specs=[tok(D_MODEL), tok(A_WIDTH), tok(B_WIDTH), tok(D_MODEL), tok(D_MODEL),
                  const((A_WIDTH, D_MODEL)), const((B_WIDTH, D_MODEL)), const((D_MODEL, D_MODEL)),
                  const((1, D_MODEL)), const((D_MODEL, LANE)), const((1, LANE)), const((tm, tm)),
                  const((N_EXPERTS, N_EXPERTS))],
        out_specs=[tok(D_MODEL), tok(D_MODEL), kt, kt, pl.BlockSpec((N_EXPERTS, LANE), lambda i: (i, 0))],
        out_shape=[jax.ShapeDtypeStruct((T, D_MODEL), F32), jax.ShapeDtypeStruct((T, D_MODEL), BF16),
                   jax.ShapeDtypeStruct((TOP_K, T), F32), jax.ShapeDtypeStruct((TOP_K, T), I32),
                   jax.ShapeDtypeStruct((T // tm * N_EXPERTS, LANE), I32)],
        compiler_params=_cparams(("parallel",)),
        name="outproj",
    )(x2, oa, ob, ga, gb, wa, wb, wo, n2, wr, br, upper, lower)


def _pack_rows(v, *, is_bf16_valued=False):
    if not is_bf16_valued:
        v = v.astype(BF16).astype(F32)
    lo = pltpu.bitcast(v[:, :ROW_WORDS], U32) >> 16
    hi = pltpu.bitcast(v[:, ROW_WORDS:], U32) & jnp.uint32(0xFFFF0000)
    return lo | hi


def _unpack_rows(w):
    lo = pltpu.bitcast(w << 16, F32).astype(BF16)
    hi = pltpu.bitcast(w & jnp.uint32(0xFFFF0000), F32).astype(BF16)
    return lo, hi


def _start_runs(runs, tile, tile_rows, global_rows, sem, *, to_global):
    cnt_ref, off_ref, dest_ref, _ = runs

    def run(e):
        j = tile * N_EXPERTS + e
        n = pl.multiple_of(cnt_ref[j], RUN_ALIGN)
        local = tile_rows.at[pl.ds(pl.multiple_of(off_ref[j], RUN_ALIGN), n), :]
        far = global_rows.at[pl.ds(pl.multiple_of(dest_ref[j], RUN_ALIGN), n), :]
        return pltpu.make_async_copy(local, far, sem) if to_global else pltpu.make_async_copy(far, local, sem)

    for e in range(N_EXPERTS):
        pl.when(cnt_ref[tile * N_EXPERTS + e] > 0)(lambda e=e: run(e).start(priority=ROW_DMA_PRIORITY))


def _wait_runs(runs, tile, tile_rows, global_rows, sem):
    total = pl.multiple_of(runs[3][tile], RUN_ALIGN)
    pltpu.make_async_copy(global_rows.at[pl.ds(0, total), :], tile_rows.at[pl.ds(0, total), :], sem).wait()


def _dispatch_kernel(fill_ref, cnt_ref, off_ref, dest_ref, tot_ref, lp_ref, hn_p_ref, hn_s_ref, xs_ref,
                     rows2, sems, zbuf, zsem, *, steps_p):
    i = pl.program_id(0)
    runs = (cnt_ref, off_ref, dest_ref, tot_ref)
    slot = i % 2
    rows = rows2.at[slot]

    @pl.when(i == 0)
    def _():
        zbuf[...] = jnp.zeros_like(zbuf)

        def fill_copy(j):
            return pltpu.make_async_copy(zbuf, xs_ref.at[pl.ds(pl.multiple_of(fill_ref[j], MOE_BM), MOE_BM), :], zsem)

        for j in range(fill_ref.shape[0]):
            pl.when(fill_ref[j] >= 0)(lambda j=j: fill_copy(j).start())
        for j in range(fill_ref.shape[0]):
            pl.when(fill_ref[j] >= 0)(lambda j=j: fill_copy(j).wait())

    def sort_tile(hn_ref):
        hn = hn_ref[...]
        r = lax.broadcasted_iota(I32, (SORT_CHUNK, TOK_TILE), 0).astype(I16)
        for c in range(SORT_ROWS // SORT_CHUNK):
            lp = (lp_ref[...] - c * SORT_CHUNK).astype(I16)
            hit = (r == lp[0:1, :]) | (r == lp[1:2, :]) | (r == lp[2:3, :]) | (r == lp[3:4, :])
            perm = jnp.where(hit, jnp.ones((), BF16), jnp.zeros((), BF16))
            rows[c * SORT_CHUNK:(c + 1) * SORT_CHUNK, :] = _pack_rows(_dot(perm, hn), is_bf16_valued=True)

    pl.when(i < steps_p)(lambda: sort_tile(hn_p_ref))
    pl.when(i >= steps_p)(lambda: sort_tile(hn_s_ref))
    _start_runs(runs, i, rows, xs_ref, sems.at[slot], to_global=True)
    pl.when(i > 0)(lambda: _wait_runs(runs, i - 1, rows2.at[1 - slot], xs_ref, sems.at[1 - slot]))
    pl.when(i == pl.num_programs(0) - 1)(lambda: _wait_runs(runs, i, rows, xs_ref, sems.at[slot]))


def _dispatch(fill, runs, lp, hn_p, hn_s, n_rows):
    tm = TOK_TILE
    steps_p, steps_s = hn_p.shape[0] // tm, hn_s.shape[0] // tm
    return pl.pallas_call(
        functools.partial(_dispatch_kernel, steps_p=steps_p),
        grid_spec=pltpu.PrefetchScalarGridSpec(
            num_scalar_prefetch=5,
            grid=(steps_p + steps_s,),
            in_specs=[pl.BlockSpec((TOP_K, tm), lambda i, *_: (0, i)),
                      pl.BlockSpec((tm, D_MODEL), lambda i, *_: (jnp.minimum(i, steps_p - 1), 0)),
                      pl.BlockSpec((tm, D_MODEL), lambda i, *_: (jnp.maximum(i - steps_p, 0), 0))],
            out_specs=pl.BlockSpec(memory_space=pl.ANY),
            scratch_shapes=[pltpu.VMEM((2, SORT_ROWS, ROW_WORDS), U32), pltpu.SemaphoreType.DMA((2,)),
                            pltpu.VMEM((MOE_BM, ROW_WORDS), U32), pltpu.SemaphoreType.DMA],
        ),
        out_shape=jax.ShapeDtypeStruct((n_rows, ROW_WORDS), U32),
        compiler_params=_cparams(("arbitrary",)),
        name="dispatch",
    )(fill, *runs, lp, hn_p, hn_s)


def _experts_kernel(be_ref, rows_ref, nu_ref, slot_ref, pa_ref, pb_ref, pc_ref, xs_ref, w1a_ref, w1b_ref, w2_ref,
                    b1g_ref, b1l_ref, b2_ref, sel_ref, y_ref, w1g_scr, w1l_scr, w2_scr):
    del nu_ref
    i = pl.program_id(0)
    used = rows_ref[i]
    half_k = D_MODEL // 2

    def arrives(piece_ref):
        return (i == 0) | (piece_ref[i] != piece_ref[jnp.maximum(i - 1, 0)])

    def take_w1_half(w_ref, piece_ref, r0):
        slot = slot_ref[piece_ref[i]]
        for c in range(D_FF // LANE):
            cols = _dot(w_ref[0, :, c * 2 * LANE:(c + 1) * 2 * LANE].astype(BF16), sel_ref[...])
            w1g_scr[slot, r0:r0 + half_k, c * LANE:(c + 1) * LANE] = cols[:, :LANE].astype(BF16)
            w1l_scr[slot, r0:r0 + half_k, c * LANE:(c + 1) * LANE] = cols[:, LANE:].astype(BF16)

    pl.when(arrives(pa_ref))(lambda: take_w1_half(w1a_ref, pa_ref, 0))
    pl.when(arrives(pb_ref))(lambda: take_w1_half(w1b_ref, pb_ref, half_k))

    @pl.when(arrives(pc_ref))
    def _():
        w2_scr[slot_ref[pc_ref[i]]] = w2_ref[0].astype(BF16)

    cur = slot_ref[be_ref[i]]

    def mlp(words):
        x_lo, x_hi = _unpack_rows(words)

        def up(w_scr, b_ref):
            return _dot(x_lo, w_scr[cur, :ROW_WORDS, :]) + _dot(x_hi, w_scr[cur, ROW_WORDS:, :]) + b_ref[0]

        x_glu = jnp.minimum(up(w1g_scr, b1g_ref), SWIGLU_LIMIT)
        x_lin = jnp.clip(up(w1l_scr, b1l_ref), -SWIGLU_LIMIT, SWIGLU_LIMIT)
        act = x_glu * jax.nn.sigmoid(SWIGLU_ALPHA * x_glu) * (x_lin + 1.0)
        return _pack_rows(_dot(act.astype(BF16), w2_scr[cur]) + b2_ref[0])

    half = MOE_BM // 2

    @pl.when(used > half)
    def _():
        y_ref[...] = mlp(xs_ref[...])

    @pl.when((used > 0) & (used <= half))
    def _():
        y_ref[:half, :] = mlp(xs_ref[:half, :])
        y_ref[half:, :] = jnp.zeros((half, ROW_WORDS), U32)

    @pl.when(used == 0)
    def _():
        y_ref[...] = jnp.zeros_like(y_ref)


def _experts(blk_e, blk_rows, n_used, slot_e, pieces, xs, w1, b1g, b1l, w2, b2, sel):
    nblk = xs.shape[0] // MOE_BM
    by_expert = lambda r, c: pl.BlockSpec((1, r, c), lambda i, be, *_: (be[i], 0, 0))
    piece = lambda p, r, c, blk: pl.BlockSpec((1, r, c), lambda i, be, rows, nu, sl, *pc: (pc[p][i], blk, 0))
    return pl.pallas_call(
        _experts_kernel,
        grid_spec=pltpu.PrefetchScalarGridSpec(
            num_scalar_prefetch=4 + len(pieces),
            grid=(nblk,),
            in_specs=[pl.BlockSpec((MOE_BM, ROW_WORDS), lambda i, be, rows, nu, *_: (jnp.minimum(i, nu[0] - 1), 0)),
                      piece(0, D_MODEL // 2, 2 * D_FF, 0), piece(1, D_MODEL // 2, 2 * D_FF, 1),
                      piece(2, D_FF, D_MODEL, 0),
                      by_expert(1, D_FF), by_expert(1, D_FF), by_expert(1, D_MODEL),
                      pl.BlockSpec((2 * LANE, 2 * LANE), lambda i, *_: (0, 0))],
            out_specs=pl.BlockSpec((MOE_BM, ROW_WORDS), lambda i, *_: (i, 0)),
            scratch_shapes=[pltpu.VMEM((2, D_MODEL, D_FF), BF16), pltpu.VMEM((2, D_MODEL, D_FF), BF16),
                            pltpu.VMEM((2, D_FF, D_MODEL), BF16)],
        ),
        out_shape=jax.ShapeDtypeStruct(xs.shape, U32),
        compiler_params=_cparams(("arbitrary",)),
        name="experts",
    )(blk_e, blk_rows, n_used, slot_e, *pieces, xs, w1, w1, w2, b1g, b1l, b2, sel)


def _combine_kernel(cnt_ref, off_ref, dest_ref, tot_ref, lp_ref, g_ref, h_ref, yb_ref, y_ref, rows2, sems, *, tile0):
    i = pl.program_id(0)
    runs = (cnt_ref, off_ref, dest_ref, tot_ref)
    slot = i % 2
    rows = rows2.at[slot]

    def fetch(step, into):
        _start_runs(runs, step + tile0, rows2.at[into], yb_ref, sems.at[into], to_global=False)

    @pl.when(i == 0)
    def _():
        rows2[...] = jnp.zeros_like(rows2)
        fetch(0, 0)

    pl.when(i + 1 < pl.num_programs(0))(lambda: fetch(i + 1, 1 - slot))
    _wait_runs(runs, i + tile0, rows, yb_ref, sems.at[slot])
    lp, g = lp_ref[...].astype(I16), g_ref[...].astype(BF16)
    r = lax.broadcasted_iota(I32, (TOK_TILE, SORT_ROWS), 1).astype(I16)
    mix = jnp.zeros((TOK_TILE, SORT_ROWS), BF16)
    for k in range(TOP_K):
        mix = jnp.where(r == lp[:, k:k + 1], g[:, k:k + 1], mix)
    y_lo, y_hi = _unpack_rows(rows[...])
    y_ref[...] = h_ref[...] + jnp.concatenate([_dot(mix, y_lo), _dot(mix, y_hi)], axis=-1)


def _combine(runs, lp_t, gate_t, h, yb, tile0):
    T = h.shape[0]
    tm = TOK_TILE
    tok = lambda w: pl.BlockSpec((tm, w), lambda i, *_: (i, 0))
    return pl.pallas_call(
        functools.partial(_combine_kernel, tile0=tile0),
        grid_spec=pltpu.PrefetchScalarGridSpec(
            num_scalar_prefetch=4,
            grid=(T // tm,),
            in_specs=[tok(TOP_K), tok(TOP_K), tok(D_MODEL), pl.BlockSpec(memory_space=pl.ANY)],
            out_specs=tok(D_MODEL),
            scratch_shapes=[pltpu.VMEM((2, SORT_ROWS, ROW_WORDS), U32), pltpu.SemaphoreType.DMA((2,))],
        ),
        out_shape=jax.ShapeDtypeStruct((T, D_MODEL), F32),
        compiler_params=_cparams(("arbitrary",)),
        name="combine",
    )(*runs, lp_t, gate_t, h, yb)


def _block_ones(width, block):
    idx = np.arange(width) // block
    return jnp.asarray(idx[:, None] == idx[None, :], BF16)


def _layer(x_p, x_s, cache_k, cache_v, s_gla, norm1, w_in, q_norm, k_norm, sinks, w_gk, b_gk,
           gla_norm, w_a, w_b, w_o, norm2, w_router, b_router, w1, b1, w2, b2):
    Bp, Lp, _ = x_p.shape
    Bs, Ls, _ = x_s.shape
    Tp, Ts = Bp * Lp, Bs * Ls

    w_packed = _pack_w_in(w_in)
    wgk = jnp.pad(w_gk, ((0, GL_PAD - GLA_GATE_RANK), (0, 0))).astype(BF16)
    n1 = norm1.reshape(1, D_MODEL)
    gq, gk = _block_ones(A_WIDTH, SWA_HEAD_DIM), _block_ones(KV_WIDTH, SWA_HEAD_DIM)
    qn = jnp.tile(q_norm, SWA_HEADS).reshape(1, A_WIDTH)
    kn = jnp.tile(k_norm, SWA_KV_HEADS).reshape(1, KV_WIDTH)
    bgk = b_gk.reshape(1, GLA_K_WIDTH)
    gn = gla_norm.reshape(1, GLA_DV)
    wa, wb, wo = w_a.astype(BF16), w_b.astype(BF16), w_o.astype(BF16)
    n2 = norm2.reshape(1, D_MODEL)
    wr = jnp.pad(w_router, ((0, 0), (0, LANE - N_EXPERTS))).astype(BF16)
    br = jnp.pad(b_router, (0, LANE - N_EXPERTS)).reshape(1, LANE)
    b1g = b1[:, 0::2].reshape(N_EXPERTS, 1, D_FF)
    b1l = b1[:, 1::2].reshape(N_EXPERTS, 1, D_FF)
    b2r = b2.reshape(N_EXPERTS, 1, D_MODEL)
    sel_np = np.zeros((2 * LANE, 2 * LANE), np.float32)
    sel_np[2 * np.arange(LANE), np.arange(LANE)] = 1.0
    sel_np[2 * np.arange(LANE) + 1, LANE + np.arange(LANE)] = 1.0
    sel = jnp.asarray(sel_np, BF16)

    def mix(x, k_past, v_past, s0):
        B, L, _ = x.shape
        T = B * L
        x2 = x.reshape(T, D_MODEL)
        qa, ka, va, qb, kb, vb, la, rb, ga, gb = _inproj(x2, n1, w_packed, gq, gk, qn, kn, wgk, bgk)
        r3 = lambda t: t.reshape(B, L, t.shape[-1])
        oa = _swa(sinks, r3(qa), r3(ka), r3(va), k_past, v_past, has_past=k_past is not None)
        ob, s_out = _gla(r3(qb), r3(kb), r3(vb), r3(la), r3(rb), gn, s0)
        h, hn, gate, lp, cnt = _outproj(x2, oa.reshape(T, A_WIDTH), ob.reshape(T, B_WIDTH), ga, gb,
                                        wa, wb, wo, n2, wr, br, upper, lower)
        return r3(ka), r3(va), s_out, h, hn, gate, lp, cnt[:, 0].reshape(T // TOK_TILE, N_EXPERTS)

    upper = jnp.asarray(np.arange(TOK_TILE)[:, None] < np.arange(TOK_TILE)[None, :], BF16)
    lower = jnp.asarray(np.arange(N_EXPERTS)[:, None] > np.arange(N_EXPERTS)[None, :], BF16)
    win = cache_k.shape[1]
    assert win == WINDOW
    ck = cache_k.reshape(Bs, win, KV_WIDTH)
    cv = cache_v.reshape(Bs, win, KV_WIDTH)
    ka_p, va_p, s_p, h_p, hn_p, g_p, lp_p, cnt_p = mix(x_p, None, None, jnp.zeros((Bp, GLA_HEADS, GLA_DK, GLA_DV), F32))
    ka_s, va_s, s_s, h_s, hn_s, g_s, lp_s, cnt_s = mix(x_s, ck, cv, s_gla)

    T = Tp + Ts
    cnt = jnp.concatenate([cnt_p, cnt_s], axis=0)
    most_rows = T * TOP_K + cnt.size * (RUN_ALIGN - 1) + N_EXPERTS * (MOE_BM - 1)
    n_rows = -(-most_rows // MOE_BM) * MOE_BM
    counts = jnp.sum(cnt, axis=0)
    padded = (counts + MOE_BM - 1) // MOE_BM * MOE_BM
    pad_end = jnp.cumsum(padded)
    start = pad_end - padded
    dest = start[None, :] + jnp.cumsum(cnt, axis=0) - cnt
    off = jnp.cumsum(cnt, axis=1) - cnt
    flat = lambda a: a.reshape(-1).astype(I32)
    runs = (flat(cnt), flat(off), flat(dest), flat(jnp.sum(cnt, axis=1)))
    tails = jnp.where(padded > 0, pad_end - MOE_BM, -1)
    most_spare = (n_rows - T * TOP_K) // MOE_BM
    spare = pad_end[-1] + jnp.arange(most_spare, dtype=I32) * MOE_BM
    fill = jnp.concatenate([tails, jnp.where(spare < n_rows, spare, -1)]).astype(I32)
    blk_row = jnp.arange(n_rows // MOE_BM, dtype=I32) * MOE_BM
    blk_e = jnp.minimum(jnp.sum(pad_end[None, :] <= blk_row[:, None], axis=1), N_EXPERTS - 1).astype(I32)
    used_end = start + counts
    blk_rows = jnp.clip(jnp.sum(jnp.where(blk_e[:, None] == jnp.arange(N_EXPERTS)[None, :], used_end[None, :], 0),
                                axis=1) - blk_row, 0, MOE_BM)
    blk_rows = jnp.where(blk_row < pad_end[-1], blk_rows, 0).astype(I32)
    n_used = (pad_end[-1:] // MOE_BM).astype(I32)
    e_ids = jnp.arange(N_EXPERTS, dtype=I32)
    live = padded > 0
    blk_e = jnp.where(blk_row < pad_end[-1], blk_e, jnp.max(jnp.where(live, e_ids, 0)))
    later = jnp.where(live[None, :] & (e_ids[None, :] > e_ids[:, None]), e_ids[None, :], N_EXPERTS)
    nxt = jnp.min(later, axis=1)
    nxt = jnp.where(nxt < N_EXPERTS, nxt, e_ids)
    pick = lambda table: jnp.sum(jnp.where(blk_e[:, None] == e_ids[None, :], table[None, :], 0), axis=1)
    first_blk, next_first = pick(start // MOE_BM), pick(pad_end // MOE_BM)
    blk_i = jnp.arange(n_rows // MOE_BM, dtype=I32)
    pieces = tuple(jnp.where(blk_i >= jnp.maximum(next_first - lead, first_blk + 1), pick(nxt), blk_e).astype(I32)
                   for lead in EXPERT_PIECE_LEAD)
    slot_e = ((jnp.cumsum(live.astype(I32)) - 1) & 1).astype(I32)

    xs = _dispatch(fill, runs, jnp.concatenate([lp_p, lp_s], axis=1), hn_p, hn_s, n_rows)
    yb = _experts(blk_e, blk_rows, n_used, slot_e, pieces, xs, w1, b1g, b1l, w2, b2r, sel)
    y_p = _combine(runs, lp_p.T, g_p.T, h_p, yb, 0).reshape(Bp, Lp, D_MODEL)
    y_s = _combine(runs, lp_s.T, g_s.T, h_s, yb, Tp // TOK_TILE).reshape(Bs, Ls, D_MODEL)

    kv = lambda t: t.reshape(t.shape[0], t.shape[1], SWA_KV_HEADS, SWA_HEAD_DIM)
    k_p, v_p = kv(ka_p[:, -WINDOW:]), kv(va_p[:, -WINDOW:])
    k_s = jnp.concatenate([cache_k, kv(ka_s)], axis=1)[:, -win:]
    v_s = jnp.concatenate([cache_v, kv(va_s)], axis=1)[:, -win:]
    return y_p, y_s, (k_p, v_p, s_p, k_s, v_s, s_s)


def kernel(x_prompt, x_sample, cache_swa_k, cache_swa_v, state_gla, norm1, w_in, q_norm, k_norm, sinks, w_gk, b_gk,
           gla_norm, w_a, w_b, w_o, norm2, w_router, b_router, w1, b1, w2, b2):
    depth = norm1.shape[0]
    y_p, y_s = x_prompt, x_sample
    states = []
    for l in range(depth):
        y_p, y_s, st = _layer(y_p, y_s, cache_swa_k[l], cache_swa_v[l], state_gla[l], norm1[l], w_in[l], q_norm[l],
                              k_norm[l], sinks[l], w_gk[l], b_gk[l], gla_norm[l], w_a[l], w_b[l], w_o[l], norm2[l],
                              w_router[l], b_router[l], w1[l], b1[l], w2[l], b2[l])
        states.append(st)
    return (y_p, y_s) + tuple(jnp.stack([st[j] for st in states]) for j in range(6))
```

```python
import functools

import jax
import jax.numpy as jnp
import numpy as np
from jax import lax
from jax.experimental import pallas as pl
from jax.experimental.pallas import tpu as pltpu

F32 = jnp.float32
BF16 = jnp.bfloat16
I32 = jnp.int32
U32 = jnp.uint32
I16 = jnp.int16

D_MODEL = 1024
SWA_HEADS = 8
SWA_KV_HEADS = 2
SWA_GROUP = SWA_HEADS // SWA_KV_HEADS
SWA_HEAD_DIM = 64
WINDOW = 128
ATTN_SCALE = SWA_HEAD_DIM ** -0.5
GLA_HEADS = 4
GLA_DK = 64
GLA_DV = 128
GLA_GATE_RANK = 16
GLA_GATE_NORM = 16.0
GLA_SCALE = GLA_DK ** -0.5
A_WIDTH = SWA_HEADS * SWA_HEAD_DIM
KV_WIDTH = SWA_KV_HEADS * SWA_HEAD_DIM
GLA_K_WIDTH = GLA_HEADS * GLA_DK
B_WIDTH = GLA_HEADS * GLA_DV
N_EXPERTS = 32
TOP_K = 4
D_FF = D_MODEL
SWIGLU_ALPHA = 1.702
SWIGLU_LIMIT = 7.0
NORM_EPS = 1e-5
QK_EPS = 1e-6
NEG_INF = -1e30

LANE = 128
GL_PAD = LANE
VMEM_LIMIT = 56 * 1024 * 1024

_SEGS = (("qa", A_WIDTH), ("ka", KV_WIDTH), ("va", KV_WIDTH), ("qb", GLA_K_WIDTH), ("kb", GLA_K_WIDTH),
         ("vb", B_WIDTH), ("rb", B_WIDTH), ("ga", D_MODEL), ("gb", D_MODEL), ("gl", GL_PAD))
_OFF = {}
_o = 0
for _n, _w in _SEGS:
    _OFF[_n] = (_o, _w)
    _o += _w
IN_PACKED = _o

TOK_TILE = 512
GLA_CHUNK = 64
GLA_SUB = 16
MOE_BM = 512
EXPERT_PIECE_LEAD = (3, 2, 1)
ROW_DMA_PRIORITY = 1
ROW_WORDS = D_MODEL // 2
RUN_ALIGN = 8
SORT_CHUNK = 256
SORT_ROWS = -(-(TOK_TILE * TOP_K + N_EXPERTS * (RUN_ALIGN - 1)) // SORT_CHUNK) * SORT_CHUNK
SAMPLE_BATCH = 8
GLA_LONG_BATCH = 8
GLA_SEQ_TILE = 512


def _cparams(sem):
    return pltpu.CompilerParams(dimension_semantics=sem, vmem_limit_bytes=VMEM_LIMIT)


def _split_bf16(v):
    hi = v.astype(BF16)
    lo = (v - hi.astype(F32)).astype(BF16)
    return hi, lo


def _dot(a, b):
    return jnp.dot(a, b, preferred_element_type=F32)


def _dot_nt(a, b):
    return lax.dot_general(a, b, (((1,), (1,)), ((), ())), preferred_element_type=F32)


def _dot_tn(a, b):
    return lax.dot_general(a, b, (((0,), (0,)), ((), ())), preferred_element_type=F32)


_SRC = {}
_o = 0
for _n, _w in (("qa", A_WIDTH), ("ka", KV_WIDTH), ("va", KV_WIDTH), ("qb", GLA_K_WIDTH), ("kb", GLA_K_WIDTH),
               ("vb", B_WIDTH), ("gl", GLA_GATE_RANK), ("rb", B_WIDTH), ("ga", D_MODEL), ("gb", D_MODEL)):
    _SRC[_n] = (_o, _w)
    _o += _w
IN_WIDTH = _o
PACK_ROWS = 128


def _pack_w_in_kernel(w_ref, o_ref):
    for name, width in _SEGS:
        src, have = _SRC[name]
        dst, _ = _OFF[name]
        o_ref[:, dst:dst + have] = w_ref[:, src:src + have].astype(BF16)
        if have < width:
            o_ref[:, dst + have:dst + width] = jnp.zeros((o_ref.shape[0], width - have), BF16)


def _pack_w_in(w_in):
    return pl.pallas_call(
        _pack_w_in_kernel,
        grid=(D_MODEL // PACK_ROWS,),
        in_specs=[pl.BlockSpec((PACK_ROWS, IN_WIDTH), lambda i: (i, 0))],
        out_specs=pl.BlockSpec((PACK_ROWS, IN_PACKED), lambda i: (i, 0)),
        out_shape=jax.ShapeDtypeStruct((D_MODEL, IN_PACKED), BF16),
        compiler_params=_cparams(("parallel",)),
        name="pack_w_in",
    )(w_in)


def _projections(x_ref, n1_ref, w_ref, gq_ref, gk_ref, qn_ref, kn_ref, wgk_ref, bgk_ref):
    x = x_ref[...]
    ms = jnp.mean(x * x, axis=-1, keepdims=True)
    xn = (x * lax.rsqrt(ms + NORM_EPS) * n1_ref[...]).astype(BF16)

    def proj(name):
        off, width = _OFF[name]
        return _dot(xn, w_ref[:, off:off + width])

    def head_norm(v, ones_ref, gain_ref):
        hi, lo = _split_bf16(v * v)
        ss = _dot(hi, ones_ref[...]) + _dot(lo, ones_ref[...])
        return v * lax.rsqrt(ss * (1.0 / SWA_HEAD_DIM) + QK_EPS) * gain_ref[...]

    def log_decay(gl):
        z = _dot(gl.astype(BF16), wgk_ref[...]) + bgk_ref[...]
        return (jnp.minimum(z, 0.0) - jnp.log1p(jnp.exp(-jnp.abs(z)))) * (1.0 / GLA_GATE_NORM)

    epilogue = {"qa": lambda u: head_norm(u, gq_ref, qn_ref) * ATTN_SCALE,
                "ka": lambda u: head_norm(u, gk_ref, kn_ref),
                "va": lambda u: u,
                "qb": lambda u: u * GLA_SCALE,
                "kb": lambda u: u,
                "vb": lambda u: u,
                "gl": log_decay,
                "rb": lambda u: u * jax.nn.sigmoid(u),
                "ga": jax.nn.sigmoid,
                "gb": jax.nn.sigmoid}
    return proj, epilogue


def _run_plan(proj, epilogue, plan, between=None):
    u = proj(plan[0][0])
    for idx, (name, out_ref) in enumerate(plan):
        u_next = proj(plan[idx + 1][0]) if idx + 1 < len(plan) else None
        out_ref[...] = epilogue[name](u).astype(out_ref.dtype)
        if between is not None:
            between(idx, len(plan))
        u = u_next


def _inproj_kernel(x_ref, n1_ref, w_ref, gq_ref, gk_ref, qn_ref, kn_ref, wgk_ref, bgk_ref,
                   qa_ref, ka_ref, va_ref, qb_ref, kb_ref, vb_ref, la_ref, rb_ref, ga_ref, gb_ref):
    proj, epilogue = _projections(x_ref, n1_ref, w_ref, gq_ref, gk_ref, qn_ref, kn_ref, wgk_ref, bgk_ref)
    _run_plan(proj, epilogue, (("qa", qa_ref), ("ka", ka_ref), ("va", va_ref), ("qb", qb_ref), ("kb", kb_ref),
                               ("vb", vb_ref), ("gl", la_ref), ("rb", rb_ref), ("ga", ga_ref), ("gb", gb_ref)))


def _inproj_swa_kernel(sink_ref, x_ref, n1_ref, w_ref, gq_ref, gk_ref, qn_ref, kn_ref, wgk_ref, bgk_ref,
                       oa_ref, ka_ref, va_ref, qb_ref, kb_ref, vb_ref, la_ref, rb_ref, ga_ref, gb_ref,
                       k_prev, v_prev, *, tiles_per_seq):
    t = pl.program_id(0)
    has_past = (t % tiles_per_seq) != 0

    @pl.when(t == 0)
    def _():
        k_prev[...] = jnp.zeros_like(k_prev)
        v_prev[...] = jnp.zeros_like(v_prev)

    proj, epilogue = _projections(x_ref, n1_ref, w_ref, gq_ref, gk_ref, qn_ref, kn_ref, wgk_ref, bgk_ref)
    tm = x_ref.shape[0]
    q = epilogue["qa"](proj("qa")).astype(BF16)
    k_new = epilogue["ka"](proj("ka"))
    v_new = proj("va")
    ka_ref[...] = k_new
    va_ref[...] = v_new
    k_all = jnp.concatenate([k_prev[...], k_new.astype(BF16)], axis=0)
    v_all = jnp.concatenate([v_prev[...], v_new.astype(BF16)], axis=0)
    k_prev[...] = k_all[tm:, :]
    v_prev[...] = v_all[tm:, :]

    qi = lax.broadcasted_iota(I32, (WINDOW, 2 * WINDOW), 0)
    ci = lax.broadcasted_iota(I32, (WINDOW, 2 * WINDOW), 1)
    dist = WINDOW + qi - ci
    band = (dist >= 0) & (dist < WINDOW)
    band_first = band & ((ci >= WINDOW) | has_past)
    outs = {}

    def attend(blk, h):
        kv = slice(h // SWA_GROUP * SWA_HEAD_DIM, (h // SWA_GROUP + 1) * SWA_HEAD_DIM)
        rows = slice(blk * WINDOW, (blk + 2) * WINDOW)
        s = _dot_nt(q[blk * WINDOW:(blk + 1) * WINDOW, h * SWA_HEAD_DIM:(h + 1) * SWA_HEAD_DIM], k_all[rows, kv])
        s = jnp.where(band_first if blk == 0 else band, s, NEG_INF)
        sink = sink_ref[h]
        m = jnp.maximum(jnp.max(s, axis=-1, keepdims=True), sink)
        p = jnp.exp(s - m)
        denom = jnp.sum(p, axis=-1, keepdims=True) + jnp.exp(sink - m)
        outs[blk, h] = _dot(p.astype(BF16), v_all[rows, kv]) / denom

    units = [(blk, h) for blk in range(tm // WINDOW) for h in range(SWA_HEADS)]
    plan = (("qb", qb_ref), ("kb", kb_ref), ("vb", vb_ref), ("gl", la_ref), ("rb", rb_ref), ("ga", ga_ref),
            ("gb", gb_ref))

    ahead = [_OFF[name][1] for name, _ in plan[1:]] + [0]
    cuts = [len(units) * sum(ahead[:i]) // sum(ahead) for i in range(len(plan) + 1)]

    def between(idx, _):
        for blk, h in units[cuts[idx]:cuts[idx + 1]]:
            attend(blk, h)

    _run_plan(proj, epilogue, plan, between)
    for blk in range(tm // WINDOW):
        oa_ref[blk * WINDOW:(blk + 1) * WINDOW, :] = jnp.concatenate(
            [outs[blk, h] for h in range(SWA_HEADS)], axis=-1).astype(BF16)


def _inproj(x2, n1, w_packed, gq, gk, qn, kn, wgk, bgk):
    T = x2.shape[0]
    tm = min(TOK_TILE, T)
    assert T % tm == 0

    def tok(width):
        return pl.BlockSpec((tm, width), lambda i: (i, 0))

    def const(shape):
        return pl.BlockSpec(shape, lambda i: (0, 0))

    outs = (("qa", A_WIDTH, BF16), ("ka", KV_WIDTH, F32), ("va", KV_WIDTH, F32), ("qb", GLA_K_WIDTH, BF16),
            ("kb", GLA_K_WIDTH, BF16), ("vb", B_WIDTH, BF16), ("la", GLA_K_WIDTH, F32), ("rb", B_WIDTH, BF16),
            ("ga", D_MODEL, BF16), ("gb", D_MODEL, BF16))
    return pl.pallas_call(
        _inproj_kernel,
        grid=(T // tm,),
        in_specs=[tok(D_MODEL), const((1, D_MODEL)), const((D_MODEL, IN_PACKED)), const((A_WIDTH, A_WIDTH)),
                  const((KV_WIDTH, KV_WIDTH)), const((1, A_WIDTH)), const((1, KV_WIDTH)),
                  const((GL_PAD, GLA_K_WIDTH)), const((1, GLA_K_WIDTH))],
        out_specs=[tok(w) for _, w, _ in outs],
        out_shape=[jax.ShapeDtypeStruct((T, w), dt) for _, w, dt in outs],
        compiler_params=_cparams(("parallel",)),
        name="inproj",
    )(x2, n1, w_packed, gq, gk, qn, kn, wgk, bgk)


def _inproj_swa(sinks, x2, seq_len, n1, w_packed, gq, gk, qn, kn, wgk, bgk):
    T = x2.shape[0]
    tm = TOK_TILE
    assert seq_len % tm == 0 and T % seq_len == 0
    tok = lambda width: pl.BlockSpec((tm, width), lambda i, s: (i, 0))
    const = lambda shape: pl.BlockSpec(shape, lambda i, s: (0, 0))
    outs = (("oa", A_WIDTH, BF16), ("ka", KV_WIDTH, F32), ("va", KV_WIDTH, F32), ("qb", GLA_K_WIDTH, BF16),
            ("kb", GLA_K_WIDTH, BF16), ("vb", B_WIDTH, BF16), ("la", GLA_K_WIDTH, F32), ("rb", B_WIDTH, BF16),
            ("ga", D_MODEL, BF16), ("gb", D_MODEL, BF16))
    return pl.pallas_call(
        functools.partial(_inproj_swa_kernel, tiles_per_seq=seq_len // tm),
        grid_spec=pltpu.PrefetchScalarGridSpec(
            num_scalar_prefetch=1,
            grid=(T // tm,),
            in_specs=[tok(D_MODEL), const((1, D_MODEL)), const((D_MODEL, IN_PACKED)), const((A_WIDTH, A_WIDTH)),
                      const((KV_WIDTH, KV_WIDTH)), const((1, A_WIDTH)), const((1, KV_WIDTH)),
                      const((GL_PAD, GLA_K_WIDTH)), const((1, GLA_K_WIDTH))],
            out_specs=[tok(w) for _, w, _ in outs],
            scratch_shapes=[pltpu.VMEM((WINDOW, KV_WIDTH), BF16), pltpu.VMEM((WINDOW, KV_WIDTH), BF16)],
        ),
        out_shape=[jax.ShapeDtypeStruct((T, w), dt) for _, w, dt in outs],
        compiler_params=_cparams(("arbitrary",)),
        name="inproj_swa",
    )(sinks, x2, n1, w_packed, gq, gk, qn, kn, wgk, bgk)


def _swa_kernel(sink_ref, q_ref, kc_ref, vc_ref, kp_ref, vp_ref, o_ref, *, first_block_has_past):
    n = pl.program_id(1)
    nb, lq = q_ref.shape[0], q_ref.shape[1]
    assert lq & (lq - 1) == 0
    stack = SWA_GROUP if lq < WINDOW else 1
    rows, keys = stack * lq, WINDOW + lq
    qi = lax.broadcasted_iota(I32, (rows, keys), 0) & (lq - 1)
    ci = lax.broadcasted_iota(I32, (rows, keys), 1)
    dist = WINDOW + qi - ci
    mask = (dist >= 0) & (dist < WINDOW)
    if not first_block_has_past:
        mask = mask & ((ci >= WINDOW) | (n > 0))
    cols = lambda h0: slice(h0 // SWA_GROUP * SWA_HEAD_DIM, (h0 // SWA_GROUP + 1) * SWA_HEAD_DIM)
    qs, ks, vs = [], [], []
    for j in range(nb):
        q = q_ref[j]
        qs.append(q.astype(F32) if stack > 1 else q)
        ks.append(jnp.concatenate([kp_ref[j], kc_ref[j]], axis=0).astype(BF16))
        vs.append(jnp.concatenate([vp_ref[j], vc_ref[j]], axis=0).astype(BF16))
    groups = [(j, h0) for j in range(nb) for h0 in range(0, SWA_HEADS, stack)]
    wave = len(groups) if stack > 1 else 1
    outs = {}
    for w0 in range(0, len(groups), wave):
        scores, sinks = [], []
        for j, h0 in groups[w0:w0 + wave]:
            heads = range(h0, h0 + stack)
            qg = jnp.concatenate([qs[j][:, h * SWA_HEAD_DIM:(h + 1) * SWA_HEAD_DIM] for h in heads], axis=0)
            sinks.append(jnp.concatenate([jnp.full((lq, 1), sink_ref[h], F32) for h in heads], axis=0))
            scores.append(_dot_nt(qg.astype(BF16), ks[j][:, cols(h0)]))
        probs, denoms = [], []
        for s, sink in zip(scores, sinks):
            s = jnp.where(mask, s, NEG_INF)
            m = jnp.maximum(jnp.max(s, axis=-1, keepdims=True), sink)
            p = jnp.exp(s - m)
            denoms.append(jnp.sum(p, axis=-1, keepdims=True) + jnp.exp(sink - m))
            probs.append(p.astype(BF16))
        for (j, h0), p, denom in zip(groups[w0:w0 + wave], probs, denoms):
            o = _dot(p, vs[j][:, cols(h0)]) / denom
            outs.setdefault(j, []).extend(o[g * lq:(g + 1) * lq] for g in range(stack))
    for j in range(nb):
        o_ref[j] = jnp.concatenate(outs[j], axis=-1).astype(BF16)


def _swa(sinks, qa, ka, va, k_past, v_past, *, has_past):
    B, L, _ = qa.shape
    lq = min(L, WINDOW)
    nl = L // lq
    if has_past:
        assert nl == 1
        nb = min(B, SAMPLE_BATCH)
        past_spec = pl.BlockSpec((nb, WINDOW, KV_WIDTH), lambda b, n, s: (b, 0, 0))
        kp, vp = k_past, v_past
    else:
        assert lq == WINDOW
        nb = 1
        past_spec = pl.BlockSpec((nb, WINDOW, KV_WIDTH), lambda b, n, s: (b, jnp.maximum(n - 1, 0), 0))
        kp, vp = ka, va
    assert B % nb == 0
    cur = lambda w: pl.BlockSpec((nb, lq, w), lambda b, n, s: (b, n, 0))
    return pl.pallas_call(
        functools.partial(_swa_kernel, first_block_has_past=has_past),
        grid_spec=pltpu.PrefetchScalarGridSpec(
            num_scalar_prefetch=1,
            grid=(B // nb, nl),
            in_specs=[cur(A_WIDTH), cur(KV_WIDTH), cur(KV_WIDTH), past_spec, past_spec],
            out_specs=cur(A_WIDTH),
        ),
        out_shape=jax.ShapeDtypeStruct((B, L, A_WIDTH), BF16),
        compiler_params=_cparams(("parallel", "parallel")),
        name="swa",
    )(sinks, qa, ka, va, kp, vp)


def _gla_kernel(q_ref, k_ref, v_ref, la_ref, rb_ref, gn_ref, s0_ref, o_ref, sout_ref, s_scr, *, chunk, sub, length):
    C, SUB = chunk, sub
    S = C // SUB
    W = GLA_K_WIDTH
    nb = q_ref.shape[0]

    @pl.when(pl.program_id(1) == 0)
    def _():
        s_scr[...] = s0_ref[...]

    row = lax.broadcasted_iota(I32, (C, C), 0)
    col = lax.broadcasted_iota(I32, (C, C), 1)
    tri = (row >= col).astype(BF16)
    diag_mask = (row >= col) & ((row // SUB) == (col // SUB))
    eye = lax.broadcasted_iota(I32, (GLA_DK, GLA_DK), 0) == lax.broadcasted_iota(I32, (GLA_DK, GLA_DK), 1)
    krow = lax.broadcasted_iota(I32, (C, W), 0)

    def rows_of(x, r):
        return jnp.broadcast_to(x[r:r + 1, :], (SUB, W))

    ksl = lambda h: slice(h * GLA_DK, (h + 1) * GLA_DK)
    vsl = lambda h: slice(h * GLA_DV, (h + 1) * GLA_DV)

    def decays(j, r0):
        g_hi, g_lo = _split_bf16(la_ref[j, pl.ds(r0, C), :])
        return _dot(tri, g_hi) + _dot(tri, g_lo)

    def factors(j, r0, b):
        q = q_ref[j, pl.ds(r0, C), :].astype(F32)
        k = k_ref[j, pl.ds(r0, C), :].astype(F32)
        b_last = b[C - 1:C, :]
        mid = jnp.concatenate([rows_of(b, i * SUB + SUB // 2 - 1) for i in range(S)], axis=0)
        beta = jnp.concatenate([jnp.zeros((SUB, W), F32)] + [rows_of(b, i * SUB - 1) for i in range(1, S)], axis=0)
        k_off = [None]
        for i in range(1, S):
            e = jnp.exp(jnp.minimum(b[i * SUB - 1:i * SUB, :] - b, 0.0))
            k_off.append(jnp.where(krow < i * SUB, k * e, 0.0).astype(BF16))
        return dict(q_inter=(q * jnp.exp(b)).astype(BF16), k_state=(k * jnp.exp(b_last - b)).astype(BF16),
                    q_diag=(q * jnp.exp(b - mid)).astype(BF16), k_diag=(k * jnp.exp(mid - b)).astype(BF16),
                    q_off=(q * jnp.exp(b - beta)).astype(BF16), k_off=k_off, dec=jnp.exp(b_last))

    def intra(f, h):
        a_diag = _dot_nt(f["q_diag"][:, ksl(h)], f["k_diag"][:, ksl(h)])
        if S == 1:
            return jnp.where(diag_mask, a_diag, 0.0).astype(BF16)
        blocks = [jnp.zeros((SUB, C), F32)]
        for i in range(1, S):
            blocks.append(_dot_nt(f["q_off"][i * SUB:(i + 1) * SUB, ksl(h)], f["k_off"][i][:, ksl(h)]))
        return jnp.where(diag_mask, a_diag, jnp.concatenate(blocks, axis=0)).astype(BF16)

    def read_out(j, r0, f, a):
        v = v_ref[j, pl.ds(r0, C), :]
        res = []
        for h in range(GLA_HEADS):
            o_h = _dot(a[h], v[:, vsl(h)]) + _dot(f["q_inter"][:, ksl(h)], s_scr[j, h].astype(BF16))
            res.append((o_h, _dot_tn(f["k_state"][:, ksl(h)], v[:, vsl(h)])))
        return res

    def finish(j, r0, f, res):
        outs = []
        for h, (o_h, s_inc) in enumerate(res):
            dec = jnp.broadcast_to(f["dec"][:, ksl(h)], (GLA_DK, GLA_DK))
            dec_col = jnp.sum(jnp.where(eye, dec, 0.0), axis=-1, keepdims=True)
            s_scr[j, h] = dec_col * s_scr[j, h] + s_inc
            ms = jnp.mean(o_h * o_h, axis=-1, keepdims=True)
            outs.append(o_h * lax.rsqrt(ms + NORM_EPS) * gn_ref[...])
        o = jnp.concatenate(outs, axis=-1) * rb_ref[j, pl.ds(r0, C), :].astype(F32)
        o_ref[j, pl.ds(r0, C), :] = o.astype(BF16)

    def step(c):
        r0 = c * C if isinstance(c, int) else pl.multiple_of(c * C, C)
        bs = [decays(j, r0) for j in range(nb)]
        fs = [factors(j, r0, b) for j, b in enumerate(bs)]
        attn = [[intra(f, h) for h in range(GLA_HEADS)] for f in fs]
        res = [read_out(j, r0, fs[j], attn[j]) for j in range(nb)]
        for j in range(nb):
            finish(j, r0, fs[j], res[j])

    if length == C:
        step(0)
    else:
        lax.fori_loop(0, length // C, lambda c, carry: (step(c), carry)[1], 0)

    @pl.when(pl.program_id(1) == pl.num_programs(1) - 1)
    def _():
        sout_ref[...] = s_scr[...]


def _gla(qb, kb, vb, la, rb, gn, s0):
    B, L, _ = qb.shape
    chunk = GLA_CHUNK if L % GLA_CHUNK == 0 else L
    sub = GLA_SUB if chunk % GLA_SUB == 0 else chunk
    nb = min(B, SAMPLE_BATCH) if L == chunk else min(B, GLA_LONG_BATCH)
    lt = min(L, GLA_SEQ_TILE)
    assert B % nb == 0 and L % lt == 0 and lt % chunk == 0
    seq = lambda w: pl.BlockSpec((nb, lt, w), lambda b, l: (b, l, 0))
    st = pl.BlockSpec((nb, GLA_HEADS, GLA_DK, GLA_DV), lambda b, l: (b, 0, 0, 0))
    return pl.pallas_call(
        functools.partial(_gla_kernel, chunk=chunk, sub=sub, length=lt),
        grid=(B // nb, L // lt),
        in_specs=[seq(GLA_K_WIDTH), seq(GLA_K_WIDTH), seq(B_WIDTH), seq(GLA_K_WIDTH), seq(B_WIDTH),
                  pl.BlockSpec((1, GLA_DV), lambda b, l: (0, 0)), st],
        out_specs=[seq(B_WIDTH), st],
        out_shape=[jax.ShapeDtypeStruct((B, L, B_WIDTH), BF16),
                   jax.ShapeDtypeStruct((B, GLA_HEADS, GLA_DK, GLA_DV), F32)],
        scratch_shapes=[pltpu.VMEM((nb, GLA_HEADS, GLA_DK, GLA_DV), F32)],
        compiler_params=_cparams(("parallel", "arbitrary")),
        name="gla",
    )(qb, kb, vb, la, rb, gn, s0)


def _outproj_kernel(x_ref, oa_ref, ob_ref, ga_ref, gb_ref, wa_ref, wb_ref, wo_ref, n2_ref, wr_ref, br_ref, upper_ref,
                    lower_ref, h_ref, hn_ref, g_ref, lp_ref, cnt_ref):
    tm = x_ref.shape[0]
    merged = (ga_ref[...].astype(F32) * _dot(oa_ref[...], wa_ref[...])
              + gb_ref[...].astype(F32) * _dot(ob_ref[...], wb_ref[...]))
    h = x_ref[...] + _dot(merged.astype(BF16), wo_ref[...])
    h_ref[...] = h
    ms = jnp.mean(h * h, axis=-1, keepdims=True)
    hn = h * lax.rsqrt(ms + NORM_EPS) * n2_ref[...]
    hn_bf = hn.astype(BF16)
    hn_ref[...] = hn_bf
    logits = _dot(hn_bf, wr_ref[...]) + br_ref[...]
    lt = logits.T[:N_EXPERTS, :]
    eid = lax.broadcasted_iota(I32, (N_EXPERTS, tm), 0)
    vals, hots = [], []
    for k in range(TOP_K):
        m = jnp.max(lt, axis=0, keepdims=True)
        idx = jnp.min(jnp.where(lt == m, eid, N_EXPERTS), axis=0, keepdims=True)
        hot = eid == idx
        lt = jnp.where(hot, -jnp.inf, lt)
        vals.append(m)
        hots.append(hot)
    ex = [jnp.exp(vk - vals[0]) for vk in vals]
    den = ex[0] + ex[1] + ex[2] + ex[3]
    for k in range(TOP_K):
        g_ref[k:k + 1, :] = ex[k] / den
    multi = (hots[0] | hots[1] | hots[2] | hots[3]).astype(BF16)
    before = _dot(multi, upper_ref[...])
    counts = jnp.sum(multi.astype(F32), axis=1, keepdims=True)
    aligned = jnp.floor((counts + (RUN_ALIGN - 1)) * (1.0 / RUN_ALIGN)) * RUN_ALIGN
    run_start = _dot(lower_ref[...], jnp.broadcast_to(aligned, (N_EXPERTS, LANE)).astype(BF16))[:, :1]
    place = before + run_start
    for k in range(TOP_K):
        lp_ref[k:k + 1, :] = jnp.sum(jnp.where(hots[k], place, 0.0), axis=0, keepdims=True).astype(I32)
    cnt_ref[...] = jnp.broadcast_to(aligned, cnt_ref.shape).astype(I32)


def _outproj(x2, oa, ob, ga, gb, wa, wb, wo, n2, wr, br, upper, lower):
    T = x2.shape[0]
    tm = TOK_TILE
    assert T % tm == 0 and upper.shape == (tm, tm)
    tok = lambda w: pl.BlockSpec((tm, w), lambda i: (i, 0))
    const = lambda shape: pl.BlockSpec(shape, lambda i: (0, 0))
    kt = pl.BlockSpec((TOP_K, tm), lambda i: (0, i))
    return pl.pallas_call(
        _outproj_kernel,
        grid=(T // tm,),
        in_specs=[tok(D_MODEL), tok(A_WIDTH), tok(B_WIDTH), tok(D_MODEL), tok(D_MODEL),
                  const((A_WIDTH, D_MODEL)), const((B_WIDTH, D_MODEL)), const((D_MODEL, D_MODEL)),
                  const((1, D_MODEL)), const((D_MODEL, LANE)), const((1, LANE)), const((tm, tm)),
                  const((N_EXPERTS, N_EXPERTS))],
        out_specs=[tok(D_MODEL), tok(D_MODEL), kt, kt, pl.BlockSpec((N_EXPERTS, LANE), lambda i: (i, 0))],
        out_shape=[jax.ShapeDtypeStruct((T, D_MODEL), F32), jax.ShapeDtypeStruct((T, D_MODEL), BF16),
                   jax.ShapeDtypeStruct((TOP_K, T), F32), jax.ShapeDtypeStruct((TOP_K, T), I32),
                   jax.ShapeDtypeStruct((T // tm * N_EXPERTS, LANE), I32)],
        compiler_params=_cparams(("parallel",)),
        name="outproj",
    )(x2, oa, ob, ga, gb, wa, wb, wo, n2, wr, br, upper, lower)


def _pack_rows(v, *, is_bf16_valued=False):
    if not is_bf16_valued:
        v = v.astype(BF16).astype(F32)
    lo = pltpu.bitcast(v[:, :ROW_WORDS], U32) >> 16
    hi = pltpu.bitcast(v[:, ROW_WORDS:], U32) & jnp.uint32(0xFFFF0000)
    return lo | hi


def _unpack_rows(w):
    lo = pltpu.bitcast(w << 16, F32).astype(BF16)
    hi = pltpu.bitcast(w & jnp.uint32(0xFFFF0000), F32).astype(BF16)
    return lo, hi


def _start_runs(runs, tile, tile_rows, global_rows, sem, *, to_global):
    cnt_ref, off_ref, dest_ref, _ = runs

    def run(e):
        j = tile * N_EXPERTS + e
        n = pl.multiple_of(cnt_ref[j], RUN_ALIGN)
        local = tile_rows.at[pl.ds(pl.multiple_of(off_ref[j], RUN_ALIGN), n), :]
        far = global_rows.at[pl.ds(pl.multiple_of(dest_ref[j], RUN_ALIGN), n), :]
        return pltpu.make_async_copy(local, far, sem) if to_global else pltpu.make_async_copy(far, local, sem)

    for e in range(N_EXPERTS):
        pl.when(cnt_ref[tile * N_EXPERTS + e] > 0)(lambda e=e: run(e).start(priority=ROW_DMA_PRIORITY))


def _wait_runs(runs, tile, tile_rows, global_rows, sem):
    total = pl.multiple_of(runs[3][tile], RUN_ALIGN)
    pltpu.make_async_copy(global_rows.at[pl.ds(0, total), :], tile_rows.at[pl.ds(0, total), :], sem).wait()


def _dispatch_kernel(fill_ref, cnt_ref, off_ref, dest_ref, tot_ref, lp_ref, hn_p_ref, hn_s_ref, xs_ref,
                     rows2, sems, zbuf, zsem, *, steps_p):
    i = pl.program_id(0)
    runs = (cnt_ref, off_ref, dest_ref, tot_ref)
    slot = i % 2
    rows = rows2.at[slot]

    @pl.when(i == 0)
    def _():
        zbuf[...] = jnp.zeros_like(zbuf)

        def fill_copy(j):
            return pltpu.make_async_copy(zbuf, xs_ref.at[pl.ds(pl.multiple_of(fill_ref[j], MOE_BM), MOE_BM), :], zsem)

        for j in range(fill_ref.shape[0]):
            pl.when(fill_ref[j] >= 0)(lambda j=j: fill_copy(j).start())
        for j in range(fill_ref.shape[0]):
            pl.when(fill_ref[j] >= 0)(lambda j=j: fill_copy(j).wait())

    def sort_tile(hn_ref):
        hn = hn_ref[...]
        r = lax.broadcasted_iota(I32, (SORT_CHUNK, TOK_TILE), 0).astype(I16)
        for c in range(SORT_ROWS // SORT_CHUNK):
            lp = (lp_ref[...] - c * SORT_CHUNK).astype(I16)
            hit = (r == lp[0:1, :]) | (r == lp[1:2, :]) | (r == lp[2:3, :]) | (r == lp[3:4, :])
            perm = jnp.where(hit, jnp.ones((), BF16), jnp.zeros((), BF16))
            rows[c * SORT_CHUNK:(c + 1) * SORT_CHUNK, :] = _pack_rows(_dot(perm, hn), is_bf16_valued=True)

    pl.when(i < steps_p)(lambda: sort_tile(hn_p_ref))
    pl.when(i >= steps_p)(lambda: sort_tile(hn_s_ref))
    _start_runs(runs, i, rows, xs_ref, sems.at[slot], to_global=True)
    pl.when(i > 0)(lambda: _wait_runs(runs, i - 1, rows2.at[1 - slot], xs_ref, sems.at[1 - slot]))
    pl.when(i == pl.num_programs(0) - 1)(lambda: _wait_runs(runs, i, rows, xs_ref, sems.at[slot]))


def _dispatch(fill, runs, lp, hn_p, hn_s, n_rows):
    tm = TOK_TILE
    steps_p, steps_s = hn_p.shape[0] // tm, hn_s.shape[0] // tm
    return pl.pallas_call(
        functools.partial(_dispatch_kernel, steps_p=steps_p),
        grid_spec=pltpu.PrefetchScalarGridSpec(
            num_scalar_prefetch=5,
            grid=(steps_p + steps_s,),
            in_specs=[pl.BlockSpec((TOP_K, tm), lambda i, *_: (0, i)),
                      pl.BlockSpec((tm, D_MODEL), lambda i, *_: (jnp.minimum(i, steps_p - 1), 0)),
                      pl.BlockSpec((tm, D_MODEL), lambda i, *_: (jnp.maximum(i - steps_p, 0), 0))],
            out_specs=pl.BlockSpec(memory_space=pl.ANY),
            scratch_shapes=[pltpu.VMEM((2, SORT_ROWS, ROW_WORDS), U32), pltpu.SemaphoreType.DMA((2,)),
                            pltpu.VMEM((MOE_BM, ROW_WORDS), U32), pltpu.SemaphoreType.DMA],
        ),
        out_shape=jax.ShapeDtypeStruct((n_rows, ROW_WORDS), U32),
        compiler_params=_cparams(("arbitrary",)),
        name="dispatch",
    )(fill, *runs, lp, hn_p, hn_s)


def _experts_kernel(be_ref, rows_ref, nu_ref, slot_ref, pa_ref, pb_ref, pc_ref, xs_ref, w1a_ref, w1b_ref, w2_ref,
                    b1g_ref, b1l_ref, b2_ref, sel_ref, y_ref, w1g_scr, w1l_scr, w2_scr):
    del nu_ref
    i = pl.program_id(0)
    used = rows_ref[i]
    half_k = D_MODEL // 2

    def arrives(piece_ref):
        return (i == 0) | (piece_ref[i] != piece_ref[jnp.maximum(i - 1, 0)])

    def take_w1_half(w_ref, piece_ref, r0):
        slot = slot_ref[piece_ref[i]]
        for c in range(D_FF // LANE):
            cols = _dot(w_ref[0, :, c * 2 * LANE:(c + 1) * 2 * LANE].astype(BF16), sel_ref[...])
            w1g_scr[slot, r0:r0 + half_k, c * LANE:(c + 1) * LANE] = cols[:, :LANE].astype(BF16)
            w1l_scr[slot, r0:r0 + half_k, c * LANE:(c + 1) * LANE] = cols[:, LANE:].astype(BF16)

    pl.when(arrives(pa_ref))(lambda: take_w1_half(w1a_ref, pa_ref, 0))
    pl.when(arrives(pb_ref))(lambda: take_w1_half(w1b_ref, pb_ref, half_k))

    @pl.when(arrives(pc_ref))
    def _():
        w2_scr[slot_ref[pc_ref[i]]] = w2_ref[0].astype(BF16)

    cur = slot_ref[be_ref[i]]

    def mlp(words):
        x_lo, x_hi = _unpack_rows(words)

        def up(w_scr, b_ref):
            return _dot(x_lo, w_scr[cur, :ROW_WORDS, :]) + _dot(x_hi, w_scr[cur, ROW_WORDS:, :]) + b_ref[0]

        x_glu = jnp.minimum(up(w1g_scr, b1g_ref), SWIGLU_LIMIT)
        x_lin = jnp.clip(up(w1l_scr, b1l_ref), -SWIGLU_LIMIT, SWIGLU_LIMIT)
        act = x_glu * jax.nn.sigmoid(SWIGLU_ALPHA * x_glu) * (x_lin + 1.0)
        return _pack_rows(_dot(act.astype(BF16), w2_scr[cur]) + b2_ref[0])

    half = MOE_BM // 2

    @pl.when(used > half)
    def _():
        y_ref[...] = mlp(xs_ref[...])

    @pl.when((used > 0) & (used <= half))
    def _():
        y_ref[:half, :] = mlp(xs_ref[:half, :])
        y_ref[half:, :] = jnp.zeros((half, ROW_WORDS), U32)

    @pl.when(used == 0)
    def _():
        y_ref[...] = jnp.zeros_like(y_ref)


def _experts(blk_e, blk_rows, n_used, slot_e, pieces, xs, w1, b1g, b1l, w2, b2, sel):
    nblk = xs.shape[0] // MOE_BM
    by_expert = lambda r, c: pl.BlockSpec((1, r, c), lambda i, be, *_: (be[i], 0, 0))
    piece = lambda p, r, c, blk: pl.BlockSpec((1, r, c), lambda i, be, rows, nu, sl, *pc: (pc[p][i], blk, 0))
    return pl.pallas_call(
        _experts_kernel,
        grid_spec=pltpu.PrefetchScalarGridSpec(
            num_scalar_prefetch=4 + len(pieces),
            grid=(nblk,),
            in_specs=[pl.BlockSpec((MOE_BM, ROW_WORDS), lambda i, be, rows, nu, *_: (jnp.minimum(i, nu[0] - 1), 0)),
                      piece(0, D_MODEL // 2, 2 * D_FF, 0), piece(1, D_MODEL // 2, 2 * D_FF, 1),
                      piece(2, D_FF, D_MODEL, 0),
                      by_expert(1, D_FF), by_expert(1, D_FF), by_expert(1, D_MODEL),
                      pl.BlockSpec((2 * LANE, 2 * LANE), lambda i, *_: (0, 0))],
            out_specs=pl.BlockSpec((MOE_BM, ROW_WORDS), lambda i, *_: (i, 0)),
            scratch_shapes=[pltpu.VMEM((2, D_MODEL, D_FF), BF16), pltpu.VMEM((2, D_MODEL, D_FF), BF16),
                            pltpu.VMEM((2, D_FF, D_MODEL), BF16)],
        ),
        out_shape=jax.ShapeDtypeStruct(xs.shape, U32),
        compiler_params=_cparams(("arbitrary",)),
        name="experts",
    )(blk_e, blk_rows, n_used, slot_e, *pieces, xs, w1, w1, w2, b1g, b1l, b2, sel)


def _combine_kernel(cnt_ref, off_ref, dest_ref, tot_ref, lp_ref, g_ref, h_ref, yb_ref, y_ref, rows2, sems, *, tile0):
    i = pl.program_id(0)
    runs = (cnt_ref, off_ref, dest_ref, tot_ref)
    slot = i % 2
    rows = rows2.at[slot]

    def fetch(step, into):
        _start_runs(runs, step + tile0, rows2.at[into], yb_ref, sems.at[into], to_global=False)

    @pl.when(i == 0)
    def _():
        rows2[...] = jnp.zeros_like(rows2)
        fetch(0, 0)

    pl.when(i + 1 < pl.num_programs(0))(lambda: fetch(i + 1, 1 - slot))
    _wait_runs(runs, i + tile0, rows, yb_ref, sems.at[slot])
    lp, g = lp_ref[...].astype(I16), g_ref[...].astype(BF16)
    r = lax.broadcasted_iota(I32, (TOK_TILE, SORT_ROWS), 1).astype(I16)
    mix = jnp.zeros((TOK_TILE, SORT_ROWS), BF16)
    for k in range(TOP_K):
        mix = jnp.where(r == lp[:, k:k + 1], g[:, k:k + 1], mix)
    y_lo, y_hi = _unpack_rows(rows[...])
    y_ref[...] = h_ref[...] + jnp.concatenate([_dot(mix, y_lo), _dot(mix, y_hi)], axis=-1)


def _combine(runs, lp_t, gate_t, h, yb, tile0):
    T = h.shape[0]
    tm = TOK_TILE
    tok = lambda w: pl.BlockSpec((tm, w), lambda i, *_: (i, 0))
    return pl.pallas_call(
        functools.partial(_combine_kernel, tile0=tile0),
        grid_spec=pltpu.PrefetchScalarGridSpec(
            num_scalar_prefetch=4,
            grid=(T // tm,),
            in_specs=[tok(TOP_K), tok(TOP_K), tok(D_MODEL), pl.BlockSpec(memory_space=pl.ANY)],
            out_specs=tok(D_MODEL),
            scratch_shapes=[pltpu.VMEM((2, SORT_ROWS, ROW_WORDS), U32), pltpu.SemaphoreType.DMA((2,))],
        ),
        out_shape=jax.ShapeDtypeStruct((T, D_MODEL), F32),
        compiler_params=_cparams(("arbitrary",)),
        name="combine",
    )(*runs, lp_t, gate_t, h, yb)


def _block_ones(width, block):
    idx = np.arange(width) // block
    return jnp.asarray(idx[:, None] == idx[None, :], BF16)


def _layer(x_p, x_s, cache_k, cache_v, s_gla, norm1, w_in, q_norm, k_norm, sinks, w_gk, b_gk,
           gla_norm, w_a, w_b, w_o, norm2, w_router, b_router, w1, b1, w2, b2):
    Bp, Lp, _ = x_p.shape
    Bs, Ls, _ = x_s.shape
    Tp, Ts = Bp * Lp, Bs * Ls

    w_packed = _pack_w_in(w_in)
    wgk = jnp.pad(w_gk, ((0, GL_PAD - GLA_GATE_RANK), (0, 0))).astype(BF16)
    n1 = norm1.reshape(1, D_MODEL)
    gq, gk = _block_ones(A_WIDTH, SWA_HEAD_DIM), _block_ones(KV_WIDTH, SWA_HEAD_DIM)
    qn = jnp.tile(q_norm, SWA_HEADS).reshape(1, A_WIDTH)
    kn = jnp.tile(k_norm, SWA_KV_HEADS).reshape(1, KV_WIDTH)
    bgk = b_gk.reshape(1, GLA_K_WIDTH)
    gn = gla_norm.reshape(1, GLA_DV)
    wa, wb, wo = w_a.astype(BF16), w_b.astype(BF16), w_o.astype(BF16)
    n2 = norm2.reshape(1, D_MODEL)
    wr = jnp.pad(w_router, ((0, 0), (0, LANE - N_EXPERTS))).astype(BF16)
    br = jnp.pad(b_router, (0, LANE - N_EXPERTS)).reshape(1, LANE)
    b1g = b1[:, 0::2].reshape(N_EXPERTS, 1, D_FF)
    b1l = b1[:, 1::2].reshape(N_EXPERTS, 1, D_FF)
    b2r = b2.reshape(N_EXPERTS, 1, D_MODEL)
    sel_np = np.zeros((2 * LANE, 2 * LANE), np.float32)
    sel_np[2 * np.arange(LANE), np.arange(LANE)] = 1.0
    sel_np[2 * np.arange(LANE) + 1, LANE + np.arange(LANE)] = 1.0
    sel = jnp.asarray(sel_np, BF16)

    def mix(x, k_past, v_past, s0):
        B, L, _ = x.shape
        T = B * L
        x2 = x.reshape(T, D_MODEL)
        r3 = lambda t: t.reshape(B, L, t.shape[-1])
        if k_past is None and L % TOK_TILE == 0:
            oa, ka, va, qb, kb, vb, la, rb, ga, gb = _inproj_swa(sinks, x2, L, n1, w_packed, gq, gk, qn, kn, wgk, bgk)
        else:
            qa, ka, va, qb, kb, vb, la, rb, ga, gb = _inproj(x2, n1, w_packed, gq, gk, qn, kn, wgk, bgk)
            oa = _swa(sinks, r3(qa), r3(ka), r3(va), k_past, v_past, has_past=k_past is not None)
        ob, s_out = _gla(r3(qb), r3(kb), r3(vb), r3(la), r3(rb), gn, s0)
        h, hn, gate, lp, cnt = _outproj(x2, oa.reshape(T, A_WIDTH), ob.reshape(T, B_WIDTH), ga, gb,
                                        wa, wb, wo, n2, wr, br, upper, lower)
        return r3(ka), r3(va), s_out, h, hn, gate, lp, cnt[:, 0].reshape(T // TOK_TILE, N_EXPERTS)

    upper = jnp.asarray(np.arange(TOK_TILE)[:, None] < np.arange(TOK_TILE)[None, :], BF16)
    lower = jnp.asarray(np.arange(N_EXPERTS)[:, None] > np.arange(N_EXPERTS)[None, :], BF16)
    win = cache_k.shape[1]
    assert win == WINDOW
    ck = cache_k.reshape(Bs, win, KV_WIDTH)
    cv = cache_v.reshape(Bs, win, KV_WIDTH)
    ka_p, va_p, s_p, h_p, hn_p, g_p, lp_p, cnt_p = mix(x_p, None, None, jnp.zeros((Bp, GLA_HEADS, GLA_DK, GLA_DV), F32))
    ka_s, va_s, s_s, h_s, hn_s, g_s, lp_s, cnt_s = mix(x_s, ck, cv, s_gla)

    T = Tp + Ts
    cnt = jnp.concatenate([cnt_p, cnt_s], axis=0)
    most_rows = T * TOP_K + cnt.size * (RUN_ALIGN - 1) + N_EXPERTS * (MOE_BM - 1)
    n_rows = -(-most_rows // MOE_BM) * MOE_BM
    counts = jnp.sum(cnt, axis=0)
    padded = (counts + MOE_BM - 1) // MOE_BM * MOE_BM
    pad_end = jnp.cumsum(padded)
    start = pad_end - padded
    dest = start[None, :] + jnp.cumsum(cnt, axis=0) - cnt
    off = jnp.cumsum(cnt, axis=1) - cnt
    flat = lambda a: a.reshape(-1).astype(I32)
    runs = (flat(cnt), flat(off), flat(dest), flat(jnp.sum(cnt, axis=1)))
    tails = jnp.where(padded > 0, pad_end - MOE_BM, -1)
    most_spare = (n_rows - T * TOP_K) // MOE_BM
    spare = pad_end[-1] + jnp.arange(most_spare, dtype=I32) * MOE_BM
    fill = jnp.concatenate([tails, jnp.where(spare < n_rows, spare, -1)]).astype(I32)
    blk_row = jnp.arange(n_rows // MOE_BM, dtype=I32) * MOE_BM
    blk_e = jnp.minimum(jnp.sum(pad_end[None, :] <= blk_row[:, None], axis=1), N_EXPERTS - 1).astype(I32)
    used_end = start + counts
    blk_rows = jnp.clip(jnp.sum(jnp.where(blk_e[:, None] == jnp.arange(N_EXPERTS)[None, :], used_end[None, :], 0),
                                axis=1) - blk_row, 0, MOE_BM)
    blk_rows = jnp.where(blk_row < pad_end[-1], blk_rows, 0).astype(I32)
    n_used = (pad_end[-1:] // MOE_BM).astype(I32)
    e_ids = jnp.arange(N_EXPERTS, dtype=I32)
    live = padded > 0
    blk_e = jnp.where(blk_row < pad_end[-1], blk_e, jnp.max(jnp.where(live, e_ids, 0)))
    later = jnp.where(live[None, :] & (e_ids[None, :] > e_ids[:, None]), e_ids[None, :], N_EXPERTS)
    nxt = jnp.min(later, axis=1)
    nxt = jnp.where(nxt < N_EXPERTS, nxt, e_ids)
    pick = lambda table: jnp.sum(jnp.where(blk_e[:, None] == e_ids[None, :], table[None, :], 0), axis=1)
    first_blk, next_first = pick(start // MOE_BM), pick(pad_end // MOE_BM)
    blk_i = jnp.arange(n_rows // MOE_BM, dtype=I32)
    pieces = tuple(jnp.where(blk_i >= jnp.maximum(next_first - lead, first_blk + 1), pick(nxt), blk_e).astype(I32)
                   for lead in EXPERT_PIECE_LEAD)
    slot_e = ((jnp.cumsum(live.astype(I32)) - 1) & 1).astype(I32)

    xs = _dispatch(fill, runs, jnp.concatenate([lp_p, lp_s], axis=1), hn_p, hn_s, n_rows)
    yb = _experts(blk_e, blk_rows, n_used, slot_e, pieces, xs, w1, b1g, b1l, w2, b2r, sel)
    y_p = _combine(runs, lp_p.T, g_p.T, h_p, yb, 0).reshape(Bp, Lp, D_MODEL)
    y_s = _combine(runs, lp_s.T, g_s.T, h_s, yb, Tp // TOK_TILE).reshape(Bs, Ls, D_MODEL)

    kv = lambda t: t.reshape(t.shape[0], t.shape[1], SWA_KV_HEADS, SWA_HEAD_DIM)
    k_p, v_p = kv(ka_p[:, -WINDOW:]), kv(va_p[:, -WINDOW:])
    k_s = jnp.concatenate([cache_k, kv(ka_s)], axis=1)[:, -win:]
    v_s = jnp.concatenate([cache_v, kv(va_s)], axis=1)[:, -win:]
    return y_p, y_s, (k_p, v_p, s_p, k_s, v_s, s_s)


def kernel(x_prompt, x_sample, cache_swa_k, cache_swa_v, state_gla, norm1, w_in, q_norm, k_norm, sinks, w_gk, b_gk,
           gla_norm, w_a, w_b, w_o, norm2, w_router, b_router, w1, b1, w2, b2):
    depth = norm1.shape[0]
    y_p, y_s = x_prompt, x_sample
    states = []
    for l in range(depth):
        y_p, y_s, st = _layer(y_p, y_s, cache_swa_k[l], cache_swa_v[l], state_gla[l], norm1[l], w_in[l], q_norm[l],
                              k_norm[l], sinks[l], w_gk[l], b_gk[l], gla_norm[l], w_a[l], w_b[l], w_o[l], norm2[l],
                              w_router[l], b_router[l], w1[l], b1[l], w2[l], b2[l])
        states.append(st)
    return (y_p, y_s) + tuple(jnp.stack([st[j] for st in states]) for j in range(6))
```

```python
import functools

import jax
import jax.numpy as jnp
import numpy as np
from jax import lax
from jax.experimental import pallas as pl
from jax.experimental.pallas import tpu as pltpu

F32 = jnp.float32
BF16 = jnp.bfloat16
I32 = jnp.int32
U32 = jnp.uint32
I16 = jnp.int16

D_MODEL = 1024
SWA_HEADS = 8
SWA_KV_HEADS = 2
SWA_GROUP = SWA_HEADS // SWA_KV_HEADS
SWA_HEAD_DIM = 64
WINDOW = 128
ATTN_SCALE = SWA_HEAD_DIM ** -0.5
GLA_HEADS = 4
GLA_DK = 64
GLA_DV = 128
GLA_GATE_RANK = 16
GLA_GATE_NORM = 16.0
GLA_SCALE = GLA_DK ** -0.5
A_WIDTH = SWA_HEADS * SWA_HEAD_DIM
KV_WIDTH = SWA_KV_HEADS * SWA_HEAD_DIM
GLA_K_WIDTH = GLA_HEADS * GLA_DK
B_WIDTH = GLA_HEADS * GLA_DV
N_EXPERTS = 32
TOP_K = 4
D_FF = D_MODEL
SWIGLU_ALPHA = 1.702
SWIGLU_LIMIT = 7.0
NORM_EPS = 1e-5
QK_EPS = 1e-6
NEG_INF = -1e30

LANE = 128
GL_PAD = LANE
VMEM_LIMIT = 56 * 1024 * 1024

_SEGS = (("qa", A_WIDTH), ("ka", KV_WIDTH), ("va", KV_WIDTH), ("qb", GLA_K_WIDTH), ("kb", GLA_K_WIDTH),
         ("vb", B_WIDTH), ("rb", B_WIDTH), ("ga", D_MODEL), ("gb", D_MODEL), ("gl", GL_PAD))
_OFF = {}
_o = 0
for _n, _w in _SEGS:
    _OFF[_n] = (_o, _w)
    _o += _w
IN_PACKED = _o

TOK_TILE = 512
GLA_CHUNK = 128
GLA_SUB = 16
MOE_BM = 512
EXPERT_PIECE_LEAD = (3, 2, 1)
ROW_WORDS = D_MODEL // 2
RUN_ALIGN = 8
SORT_CHUNK = 256
SORT_ROWS = -(-(TOK_TILE * TOP_K + N_EXPERTS * (RUN_ALIGN - 1)) // SORT_CHUNK) * SORT_CHUNK
SAMPLE_BATCH = 8
GLA_LONG_BATCH = 8
GLA_SEQ_TILE = 512


def _cparams(sem):
    return pltpu.CompilerParams(dimension_semantics=sem, vmem_limit_bytes=VMEM_LIMIT)


def _split_bf16(v):
    hi = v.astype(BF16)
    lo = (v - hi.astype(F32)).astype(BF16)
    return hi, lo


def _dot(a, b):
    return jnp.dot(a, b, preferred_element_type=F32)


def _dot_nt(a, b):
    return lax.dot_general(a, b, (((1,), (1,)), ((), ())), preferred_element_type=F32)


def _dot_tn(a, b):
    return lax.dot_general(a, b, (((0,), (0,)), ((), ())), preferred_element_type=F32)


_SRC = {}
_o = 0
for _n, _w in (("qa", A_WIDTH), ("ka", KV_WIDTH), ("va", KV_WIDTH), ("qb", GLA_K_WIDTH), ("kb", GLA_K_WIDTH),
               ("vb", B_WIDTH), ("gl", GLA_GATE_RANK), ("rb", B_WIDTH), ("ga", D_MODEL), ("gb", D_MODEL)):
    _SRC[_n] = (_o, _w)
    _o += _w
IN_WIDTH = _o
PACK_ROWS = 128


def _pack_w_in_kernel(w_ref, o_ref):
    for name, width in _SEGS:
        src, have = _SRC[name]
        dst, _ = _OFF[name]
        o_ref[:, dst:dst + have] = w_ref[:, src:src + have].astype(BF16)
        if have < width:
            o_ref[:, dst + have:dst + width] = jnp.zeros((o_ref.shape[0], width - have), BF16)


def _pack_w_in(w_in):
    return pl.pallas_call(
        _pack_w_in_kernel,
        grid=(D_MODEL // PACK_ROWS,),
        in_specs=[pl.BlockSpec((PACK_ROWS, IN_WIDTH), lambda i: (i, 0))],
        out_specs=pl.BlockSpec((PACK_ROWS, IN_PACKED), lambda i: (i, 0)),
        out_shape=jax.ShapeDtypeStruct((D_MODEL, IN_PACKED), BF16),
        compiler_params=_cparams(("parallel",)),
        name="pack_w_in",
    )(w_in)


def _projections(x_ref, n1_ref, w_ref, gq_ref, gk_ref, qn_ref, kn_ref, wgk_ref, bgk_ref):
    x = x_ref[...]
    ms = jnp.mean(x * x, axis=-1, keepdims=True)
    xn = (x * lax.rsqrt(ms + NORM_EPS) * n1_ref[...]).astype(BF16)

    def proj(name):
        off, width = _OFF[name]
        return _dot(xn, w_ref[:, off:off + width])

    def head_norm(v, ones_ref, gain_ref):
        hi, lo = _split_bf16(v * v)
        ss = _dot(hi, ones_ref[...]) + _dot(lo, ones_ref[...])
        return v * lax.rsqrt(ss * (1.0 / SWA_HEAD_DIM) + QK_EPS) * gain_ref[...]

    def log_decay(gl):
        z = _dot(gl.astype(BF16), wgk_ref[...]) + bgk_ref[...]
        return (jnp.minimum(z, 0.0) - jnp.log1p(jnp.exp(-jnp.abs(z)))) * (1.0 / GLA_GATE_NORM)

    epilogue = {"qa": lambda u: head_norm(u, gq_ref, qn_ref) * ATTN_SCALE,
                "ka": lambda u: head_norm(u, gk_ref, kn_ref),
                "va": lambda u: u,
                "qb": lambda u: u * GLA_SCALE,
                "kb": lambda u: u,
                "vb": lambda u: u,
                "gl": log_decay,
                "rb": lambda u: u * jax.nn.sigmoid(u),
                "ga": jax.nn.sigmoid,
                "gb": jax.nn.sigmoid}
    return proj, epilogue


def _run_plan(proj, epilogue, plan, before=None, between=None):
    u = proj(plan[0][0])
    for idx, (name, out_ref) in enumerate(plan):
        if before is not None:
            before(idx)
        u_next = proj(plan[idx + 1][0]) if idx + 1 < len(plan) else None
        out_ref[...] = epilogue[name](u).astype(out_ref.dtype)
        if between is not None:
            between(idx)
        u = u_next


def _inproj_kernel(x_ref, n1_ref, w_ref, gq_ref, gk_ref, qn_ref, kn_ref, wgk_ref, bgk_ref,
                   qa_ref, ka_ref, va_ref, qb_ref, kb_ref, vb_ref, la_ref, rb_ref, ga_ref, gb_ref):
    proj, epilogue = _projections(x_ref, n1_ref, w_ref, gq_ref, gk_ref, qn_ref, kn_ref, wgk_ref, bgk_ref)
    _run_plan(proj, epilogue, (("qa", qa_ref), ("ka", ka_ref), ("va", va_ref), ("qb", qb_ref), ("kb", kb_ref),
                               ("vb", vb_ref), ("gl", la_ref), ("rb", rb_ref), ("ga", ga_ref), ("gb", gb_ref)))


def _inproj_swa_kernel(sink_ref, x_ref, n1_ref, w_ref, gq_ref, gk_ref, qn_ref, kn_ref, wgk_ref, bgk_ref,
                       oa_ref, ka_ref, va_ref, qb_ref, kb_ref, vb_ref, la_ref, rb_ref, ga_ref, gb_ref,
                       k_prev, v_prev, *, tiles_per_seq):
    t = pl.program_id(0)
    has_past = (t % tiles_per_seq) != 0

    @pl.when(t == 0)
    def _():
        k_prev[...] = jnp.zeros_like(k_prev)
        v_prev[...] = jnp.zeros_like(v_prev)

    proj, epilogue = _projections(x_ref, n1_ref, w_ref, gq_ref, gk_ref, qn_ref, kn_ref, wgk_ref, bgk_ref)
    tm = x_ref.shape[0]
    q = epilogue["qa"](proj("qa")).astype(BF16)
    k_new = epilogue["ka"](proj("ka"))
    v_new = proj("va")
    ka_ref[...] = k_new
    va_ref[...] = v_new
    k_all = jnp.concatenate([k_prev[...], k_new.astype(BF16)], axis=0)
    v_all = jnp.concatenate([v_prev[...], v_new.astype(BF16)], axis=0)
    k_prev[...] = k_all[tm:, :]
    v_prev[...] = v_all[tm:, :]

    qi = lax.broadcasted_iota(I32, (WINDOW, 2 * WINDOW), 0)
    ci = lax.broadcasted_iota(I32, (WINDOW, 2 * WINDOW), 1)
    dist = WINDOW + qi - ci
    band = (dist >= 0) & (dist < WINDOW)
    band_first = band & ((ci >= WINDOW) | has_past)
    outs = {}

    def keys_of(blk, h):
        return (slice(blk * WINDOW, (blk + 2) * WINDOW),
                slice(h // SWA_GROUP * SWA_HEAD_DIM, (h // SWA_GROUP + 1) * SWA_HEAD_DIM))

    def scores(blk, h):
        return _dot_nt(q[blk * WINDOW:(blk + 1) * WINDOW, h * SWA_HEAD_DIM:(h + 1) * SWA_HEAD_DIM],
                       k_all[keys_of(blk, h)])

    def finish(blk, h, s):
        s = jnp.where(band_first if blk == 0 else band, s, NEG_INF)
        sink = sink_ref[h]
        m = jnp.maximum(jnp.max(s, axis=-1, keepdims=True), sink)
        p = jnp.exp(s - m)
        denom = jnp.sum(p, axis=-1, keepdims=True) + jnp.exp(sink - m)
        outs[blk, h] = _dot(p.astype(BF16), v_all[keys_of(blk, h)]) / denom

    units = [(blk, h) for blk in range(tm // WINDOW) for h in range(SWA_HEADS)]
    plan = (("qb", qb_ref), ("kb", kb_ref), ("vb", vb_ref), ("gl", la_ref), ("rb", rb_ref), ("ga", ga_ref),
            ("gb", gb_ref))

    ahead = [_OFF[name][1] for name, _ in plan[1:]] + [0]
    cuts = [len(units) * sum(ahead[:i]) // sum(ahead) for i in range(len(plan) + 1)]
    pending = {}

    def before(idx):
        for unit in units[cuts[idx]:cuts[idx + 1]]:
            pending[unit] = scores(*unit)

    def between(idx):
        for unit in units[cuts[idx]:cuts[idx + 1]]:
            finish(*unit, pending.pop(unit))

    _run_plan(proj, epilogue, plan, before, between)
    for blk in range(tm // WINDOW):
        oa_ref[blk * WINDOW:(blk + 1) * WINDOW, :] = jnp.concatenate(
            [outs[blk, h] for h in range(SWA_HEADS)], axis=-1).astype(BF16)


def _inproj(x2, n1, w_packed, gq, gk, qn, kn, wgk, bgk):
    T = x2.shape[0]
    tm = min(TOK_TILE, T)
    assert T % tm == 0

    def tok(width):
        return pl.BlockSpec((tm, width), lambda i: (i, 0))

    def const(shape):
        return pl.BlockSpec(shape, lambda i: (0, 0))

    outs = (("qa", A_WIDTH, BF16), ("ka", KV_WIDTH, F32), ("va", KV_WIDTH, F32), ("qb", GLA_K_WIDTH, BF16),
            ("kb", GLA_K_WIDTH, BF16), ("vb", B_WIDTH, BF16), ("la", GLA_K_WIDTH, F32), ("rb", B_WIDTH, BF16),
            ("ga", D_MODEL, BF16), ("gb", D_MODEL, BF16))
    return pl.pallas_call(
        _inproj_kernel,
        grid=(T // tm,),
        in_specs=[tok(D_MODEL), const((1, D_MODEL)), const((D_MODEL, IN_PACKED)), const((A_WIDTH, A_WIDTH)),
                  const((KV_WIDTH, KV_WIDTH)), const((1, A_WIDTH)), const((1, KV_WIDTH)),
                  const((GL_PAD, GLA_K_WIDTH)), const((1, GLA_K_WIDTH))],
        out_specs=[tok(w) for _, w, _ in outs],
        out_shape=[jax.ShapeDtypeStruct((T, w), dt) for _, w, dt in outs],
        compiler_params=_cparams(("parallel",)),
        name="inproj",
    )(x2, n1, w_packed, gq, gk, qn, kn, wgk, bgk)


def _inproj_swa(sinks, x2, seq_len, n1, w_packed, gq, gk, qn, kn, wgk, bgk):
    T = x2.shape[0]
    tm = TOK_TILE
    assert seq_len % tm == 0 and T % seq_len == 0
    tok = lambda width: pl.BlockSpec((tm, width), lambda i, s: (i, 0))
    const = lambda shape: pl.BlockSpec(shape, lambda i, s: (0, 0))
    outs = (("oa", A_WIDTH, BF16), ("ka", KV_WIDTH, F32), ("va", KV_WIDTH, F32), ("qb", GLA_K_WIDTH, BF16),
            ("kb", GLA_K_WIDTH, BF16), ("vb", B_WIDTH, BF16), ("la", GLA_K_WIDTH, F32), ("rb", B_WIDTH, BF16),
            ("ga", D_MODEL, BF16), ("gb", D_MODEL, BF16))
    return pl.pallas_call(
        functools.partial(_inproj_swa_kernel, tiles_per_seq=seq_len // tm),
        grid_spec=pltpu.PrefetchScalarGridSpec(
            num_scalar_prefetch=1,
            grid=(T // tm,),
            in_specs=[tok(D_MODEL), const((1, D_MODEL)), const((D_MODEL, IN_PACKED)), const((A_WIDTH, A_WIDTH)),
                      const((KV_WIDTH, KV_WIDTH)), const((1, A_WIDTH)), const((1, KV_WIDTH)),
                      const((GL_PAD, GLA_K_WIDTH)), const((1, GLA_K_WIDTH))],
            out_specs=[tok(w) for _, w, _ in outs],
            scratch_shapes=[pltpu.VMEM((WINDOW, KV_WIDTH), BF16), pltpu.VMEM((WINDOW, KV_WIDTH), BF16)],
        ),
        out_shape=[jax.ShapeDtypeStruct((T, w), dt) for _, w, dt in outs],
        compiler_params=_cparams(("arbitrary",)),
        name="inproj_swa",
    )(sinks, x2, n1, w_packed, gq, gk, qn, kn, wgk, bgk)


def _swa_kernel(sink_ref, q_ref, kc_ref, vc_ref, kp_ref, vp_ref, o_ref, *, first_block_has_past):
    n = pl.program_id(1)
    nb, lq = q_ref.shape[0], q_ref.shape[1]
    assert lq & (lq - 1) == 0
    stack = SWA_GROUP if lq < WINDOW else 1
    rows, keys = stack * lq, WINDOW + lq
    qi = lax.broadcasted_iota(I32, (rows, keys), 0) & (lq - 1)
    ci = lax.broadcasted_iota(I32, (rows, keys), 1)
    dist = WINDOW + qi - ci
    mask = (dist >= 0) & (dist < WINDOW)
    if not first_block_has_past:
        mask = mask & ((ci >= WINDOW) | (n > 0))
    cols = lambda h0: slice(h0 // SWA_GROUP * SWA_HEAD_DIM, (h0 // SWA_GROUP + 1) * SWA_HEAD_DIM)
    qs, ks, vs = [], [], []
    for j in range(nb):
        q = q_ref[j]
        qs.append(q.astype(F32) if stack > 1 else q)
        ks.append(jnp.concatenate([kp_ref[j], kc_ref[j]], axis=0).astype(BF16))
        vs.append(jnp.concatenate([vp_ref[j], vc_ref[j]], axis=0).astype(BF16))
    groups = [(j, h0) for j in range(nb) for h0 in range(0, SWA_HEADS, stack)]
    wave = len(groups) if stack > 1 else 1
    outs = {}
    for w0 in range(0, len(groups), wave):
        scores, sinks = [], []
        for j, h0 in groups[w0:w0 + wave]:
            heads = range(h0, h0 + stack)
            qg = jnp.concatenate([qs[j][:, h * SWA_HEAD_DIM:(h + 1) * SWA_HEAD_DIM] for h in heads], axis=0)
            sinks.append(jnp.concatenate([jnp.full((lq, 1), sink_ref[h], F32) for h in heads], axis=0))
            scores.append(_dot_nt(qg.astype(BF16), ks[j][:, cols(h0)]))
        probs, denoms = [], []
        for s, sink in zip(scores, sinks):
            s = jnp.where(mask, s, NEG_INF)
            m = jnp.maximum(jnp.max(s, axis=-1, keepdims=True), sink)
            p = jnp.exp(s - m)
            denoms.append(jnp.sum(p, axis=-1, keepdims=True) + jnp.exp(sink - m))
            probs.append(p.astype(BF16))
        for (j, h0), p, denom in zip(groups[w0:w0 + wave], probs, denoms):
            o = _dot(p, vs[j][:, cols(h0)]) / denom
            outs.setdefault(j, []).extend(o[g * lq:(g + 1) * lq] for g in range(stack))
    for j in range(nb):
        o_ref[j] = jnp.concatenate(outs[j], axis=-1).astype(BF16)


def _swa(sinks, qa, ka, va, k_past, v_past, *, has_past):
    B, L, _ = qa.shape
    lq = min(L, WINDOW)
    nl = L // lq
    if has_past:
        assert nl == 1
        nb = min(B, SAMPLE_BATCH)
        past_spec = pl.BlockSpec((nb, WINDOW, KV_WIDTH), lambda b, n, s: (b, 0, 0))
        kp, vp = k_past, v_past
    else:
        assert lq == WINDOW
        nb = 1
        past_spec = pl.BlockSpec((nb, WINDOW, KV_WIDTH), lambda b, n, s: (b, jnp.maximum(n - 1, 0), 0))
        kp, vp = ka, va
    assert B % nb == 0
    cur = lambda w: pl.BlockSpec((nb, lq, w), lambda b, n, s: (b, n, 0))
    return pl.pallas_call(
        functools.partial(_swa_kernel, first_block_has_past=has_past),
        grid_spec=pltpu.PrefetchScalarGridSpec(
            num_scalar_prefetch=1,
            grid=(B // nb, nl),
            in_specs=[cur(A_WIDTH), cur(KV_WIDTH), cur(KV_WIDTH), past_spec, past_spec],
            out_specs=cur(A_WIDTH),
        ),
        out_shape=jax.ShapeDtypeStruct((B, L, A_WIDTH), BF16),
        compiler_params=_cparams(("parallel", "parallel")),
        name="swa",
    )(sinks, qa, ka, va, kp, vp)


def _gla_kernel(q_ref, k_ref, v_ref, la_ref, rb_ref, gn_ref, s0_ref, o_ref, sout_ref, s_scr, *, chunk, sub, length):
    C, SUB = chunk, sub
    S = C // SUB
    W = GLA_K_WIDTH
    nb = q_ref.shape[0]

    @pl.when(pl.program_id(1) == 0)
    def _():
        s_scr[...] = s0_ref[...]

    row = lax.broadcasted_iota(I32, (C, C), 0)
    col = lax.broadcasted_iota(I32, (C, C), 1)
    tri = (row >= col).astype(BF16)
    diag_mask = (row >= col) & ((row // SUB) == (col // SUB))
    eye = lax.broadcasted_iota(I32, (GLA_DK, GLA_DK), 0) == lax.broadcasted_iota(I32, (GLA_DK, GLA_DK), 1)
    krow = lax.broadcasted_iota(I32, (C, W), 0)

    def rows_of(x, r):
        return jnp.broadcast_to(x[r:r + 1, :], (SUB, W))

    ksl = lambda h: slice(h * GLA_DK, (h + 1) * GLA_DK)
    vsl = lambda h: slice(h * GLA_DV, (h + 1) * GLA_DV)

    def decays(j, r0):
        g_hi, g_lo = _split_bf16(la_ref[j, pl.ds(r0, C), :])
        return _dot(tri, g_hi) + _dot(tri, g_lo)

    def factors(j, r0, b):
        q = q_ref[j, pl.ds(r0, C), :].astype(F32)
        k = k_ref[j, pl.ds(r0, C), :].astype(F32)
        b_last = b[C - 1:C, :]
        mid = jnp.concatenate([rows_of(b, i * SUB + SUB // 2 - 1) for i in range(S)], axis=0)
        beta = jnp.concatenate([jnp.zeros((SUB, W), F32)] + [rows_of(b, i * SUB - 1) for i in range(1, S)], axis=0)
        k_off = [None]
        for i in range(1, S):
            e = jnp.exp(jnp.minimum(b[i * SUB - 1:i * SUB, :] - b, 0.0))
            k_off.append(jnp.where(krow < i * SUB, k * e, 0.0).astype(BF16))
        return dict(q_inter=(q * jnp.exp(b)).astype(BF16), k_state=(k * jnp.exp(b_last - b)).astype(BF16),
                    q_diag=(q * jnp.exp(b - mid)).astype(BF16), k_diag=(k * jnp.exp(mid - b)).astype(BF16),
                    q_off=(q * jnp.exp(b - beta)).astype(BF16), k_off=k_off, dec=jnp.exp(b_last))

    def intra(f, h):
        a_diag = _dot_nt(f["q_diag"][:, ksl(h)], f["k_diag"][:, ksl(h)])
        if S == 1:
            return jnp.where(diag_mask, a_diag, 0.0).astype(BF16)
        blocks = [jnp.zeros((SUB, C), F32)]
        for i in range(1, S):
            blocks.append(_dot_nt(f["q_off"][i * SUB:(i + 1) * SUB, ksl(h)], f["k_off"][i][:, ksl(h)]))
        return jnp.where(diag_mask, a_diag, jnp.concatenate(blocks, axis=0)).astype(BF16)

    def read_out(j, r0, f, a):
        v = v_ref[j, pl.ds(r0, C), :]
        res = []
        for h in range(GLA_HEADS):
            o_h = _dot(a[h], v[:, vsl(h)]) + _dot(f["q_inter"][:, ksl(h)], s_scr[j, h].astype(BF16))
            res.append((o_h, _dot_tn(f["k_state"][:, ksl(h)], v[:, vsl(h)])))
        return res

    def finish(j, r0, f, res):
        outs = []
        for h, (o_h, s_inc) in enumerate(res):
            dec = jnp.broadcast_to(f["dec"][:, ksl(h)], (GLA_DK, GLA_DK))
            dec_col = jnp.sum(jnp.where(eye, dec, 0.0), axis=-1, keepdims=True)
            s_scr[j, h] = dec_col * s_scr[j, h] + s_inc
            ms = jnp.mean(o_h * o_h, axis=-1, keepdims=True)
            outs.append(o_h * lax.rsqrt(ms + NORM_EPS) * gn_ref[...])
        o = jnp.concatenate(outs, axis=-1) * rb_ref[j, pl.ds(r0, C), :].astype(F32)
        o_ref[j, pl.ds(r0, C), :] = o.astype(BF16)

    def step(c):
        r0 = c * C if isinstance(c, int) else pl.multiple_of(c * C, C)
        bs = [decays(j, r0) for j in range(nb)]
        fs = [factors(j, r0, b) for j, b in enumerate(bs)]
        attn = [[intra(f, h) for h in range(GLA_HEADS)] for f in fs]
        res = [read_out(j, r0, fs[j], attn[j]) for j in range(nb)]
        for j in range(nb):
            finish(j, r0, fs[j], res[j])

    if length == C:
        step(0)
    else:
        lax.fori_loop(0, length // C, lambda c, carry: (step(c), carry)[1], 0)

    @pl.when(pl.program_id(1) == pl.num_programs(1) - 1)
    def _():
        sout_ref[...] = s_scr[...]


def _gla(qb, kb, vb, la, rb, gn, s0):
    B, L, _ = qb.shape
    chunk = GLA_CHUNK if L % GLA_CHUNK == 0 else L
    sub = GLA_SUB if chunk % GLA_SUB == 0 else chunk
    nb = min(B, SAMPLE_BATCH) if L == chunk else min(B, GLA_LONG_BATCH)
    lt = min(L, GLA_SEQ_TILE)
    assert B % nb == 0 and L % lt == 0 and lt % chunk == 0
    seq = lambda w: pl.BlockSpec((nb, lt, w), lambda b, l: (b, l, 0))
    st = pl.BlockSpec((nb, GLA_HEADS, GLA_DK, GLA_DV), lambda b, l: (b, 0, 0, 0))
    return pl.pallas_call(
        functools.partial(_gla_kernel, chunk=chunk, sub=sub, length=lt),
        grid=(B // nb, L // lt),
        in_specs=[seq(GLA_K_WIDTH), seq(GLA_K_WIDTH), seq(B_WIDTH), seq(GLA_K_WIDTH), seq(B_WIDTH),
                  pl.BlockSpec((1, GLA_DV), lambda b, l: (0, 0)), st],
        out_specs=[seq(B_WIDTH), st],
        out_shape=[jax.ShapeDtypeStruct((B, L, B_WIDTH), BF16),
                   jax.ShapeDtypeStruct((B, GLA_HEADS, GLA_DK, GLA_DV), F32)],
        scratch_shapes=[pltpu.VMEM((nb, GLA_HEADS, GLA_DK, GLA_DV), F32)],
        compiler_params=_cparams(("parallel", "arbitrary")),
        name="gla",
    )(qb, kb, vb, la, rb, gn, s0)


def _outproj_kernel(x_ref, oa_ref, ob_ref, ga_ref, gb_ref, wa_ref, wb_ref, wo_ref, n2_ref, wr_ref, br_ref, upper_ref,
                    lower_ref, h_ref, hn_ref, g_ref, lp_ref, cnt_ref):
    tm = x_ref.shape[0]
    merged = (ga_ref[...].astype(F32) * _dot(oa_ref[...], wa_ref[...])
              + gb_ref[...].astype(F32) * _dot(ob_ref[...], wb_ref[...]))
    h = x_ref[...] + _dot(merged.astype(BF16), wo_ref[...])
    h_ref[...] = h
    ms = jnp.mean(h * h, axis=-1, keepdims=True)
    hn = h * lax.rsqrt(ms + NORM_EPS) * n2_ref[...]
    hn_bf = hn.astype(BF16)
    hn_ref[...] = hn_bf
    logits = _dot(hn_bf, wr_ref[...]) + br_ref[...]
    lt = logits.T[:N_EXPERTS, :]
    eid = lax.broadcasted_iota(I32, (N_EXPERTS, tm), 0)
    vals, hots = [], []
    for k in range(TOP_K):
        m = jnp.max(lt, axis=0, keepdims=True)
        idx = jnp.min(jnp.where(lt == m, eid, N_EXPERTS), axis=0, keepdims=True)
        hot = eid == idx
        lt = jnp.where(hot, -jnp.inf, lt)
        vals.append(m)
        hots.append(hot)
    ex = [jnp.exp(vk - vals[0]) for vk in vals]
    den = ex[0] + ex[1] + ex[2] + ex[3]
    for k in range(TOP_K):
        g_ref[k:k + 1, :] = ex[k] / den
    multi = (hots[0] | hots[1] | hots[2] | hots[3]).astype(BF16)
    before = _dot(multi, upper_ref[...])
    counts = jnp.sum(multi.astype(F32), axis=1, keepdims=True)
    aligned = jnp.floor((counts + (RUN_ALIGN - 1)) * (1.0 / RUN_ALIGN)) * RUN_ALIGN
    run_start = _dot(lower_ref[...], jnp.broadcast_to(aligned, (N_EXPERTS, LANE)).astype(BF16))[:, :1]
    place = before + run_start
    for k in range(TOP_K):
        lp_ref[k:k + 1, :] = jnp.sum(jnp.where(hots[k], place, 0.0), axis=0, keepdims=True).astype(I32)
    cnt_ref[...] = jnp.broadcast_to(aligned, cnt_ref.shape).astype(I32)


def _outproj(x2, oa, ob, ga, gb, wa, wb, wo, n2, wr, br, upper, lower):
    T = x2.shape[0]
    tm = TOK_TILE
    assert T % tm == 0 and upper.shape == (tm, tm)
    tok = lambda w: pl.BlockSpec((tm, w), lambda i: (i, 0))
    const = lambda shape: pl.BlockSpec(shape, lambda i: (0, 0))
    kt = pl.BlockSpec((TOP_K, tm), lambda i: (0, i))
    return pl.pallas_call(
        _outproj_kernel,
        grid=(T // tm,),
        in_specs=[tok(D_MODEL), tok(A_WIDTH), tok(B_WIDTH), tok(D_MODEL), tok(D_MODEL),
                  const((A_WIDTH, D_MODEL)), const((B_WIDTH, D_MODEL)), const((D_MODEL, D_MODEL)),
                  const((1, D_MODEL)), const((D_MODEL, LANE)), const((1, LANE)), const((tm, tm)),
                  const((N_EXPERTS, N_EXPERTS))],
        out_specs=[tok(D_MODEL), tok(D_MODEL), kt, kt, pl.BlockSpec((N_EXPERTS, LANE), lambda i: (i, 0))],
        out_shape=[jax.ShapeDtypeStruct((T, D_MODEL), F32), jax.ShapeDtypeStruct((T, D_MODEL), BF16),
                   jax.ShapeDtypeStruct((TOP_K, T), F32), jax.ShapeDtypeStruct((TOP_K, T), I32),
                   jax.ShapeDtypeStruct((T // tm * N_EXPERTS, LANE), I32)],
        compiler_params=_cparams(("parallel",)),
        name="outproj",
    )(x2, oa, ob, ga, gb, wa, wb, wo, n2, wr, br, upper, lower)


def _pack_rows(v, *, is_bf16_valued=False):
    if not is_bf16_valued:
        v = v.astype(BF16).astype(F32)
    lo = pltpu.bitcast(v[:, :ROW_WORDS], U32) >> 16
    hi = pltpu.bitcast(v[:, ROW_WORDS:], U32) & jnp.uint32(0xFFFF0000)
    return lo | hi


def _unpack_rows(w):
    lo = pltpu.bitcast(w << 16, F32).astype(BF16)
    hi = pltpu.bitcast(w & jnp.uint32(0xFFFF0000), F32).astype(BF16)
    return lo, hi


def _start_runs(runs, tile, tile_rows, global_rows, sem, *, to_global):
    cnt_ref, off_ref, dest_ref, _ = runs

    def run(e):
        j = tile * N_EXPERTS + e
        n = pl.multiple_of(cnt_ref[j], RUN_ALIGN)
        local = tile_rows.at[pl.ds(pl.multiple_of(off_ref[j], RUN_ALIGN), n), :]
        far = global_rows.at[pl.ds(pl.multiple_of(dest_ref[j], RUN_ALIGN), n), :]
        return pltpu.make_async_copy(local, far, sem) if to_global else pltpu.make_async_copy(far, local, sem)

    for e in range(N_EXPERTS):
        pl.when(cnt_ref[tile * N_EXPERTS + e] > 0)(lambda e=e: run(e).start())


def _wait_runs(runs, tile, tile_rows, global_rows, sem):
    total = pl.multiple_of(runs[3][tile], RUN_ALIGN)
    pltpu.make_async_copy(global_rows.at[pl.ds(0, total), :], tile_rows.at[pl.ds(0, total), :], sem).wait()


def _dispatch_kernel(fill_ref, cnt_ref, off_ref, dest_ref, tot_ref, lp_ref, hn_p_ref, hn_s_ref, xs_ref,
                     rows2, sems, zbuf, zsem, *, steps_p):
    i = pl.program_id(0)
    runs = (cnt_ref, off_ref, dest_ref, tot_ref)
    slot = i % 2
    rows = rows2.at[slot]

    @pl.when(i == 0)
    def _():
        zbuf[...] = jnp.zeros_like(zbuf)

        def fill_copy(j):
            return pltpu.make_async_copy(zbuf, xs_ref.at[pl.ds(pl.multiple_of(fill_ref[j], MOE_BM), MOE_BM), :], zsem)

        for j in range(fill_ref.shape[0]):
            pl.when(fill_ref[j] >= 0)(lambda j=j: fill_copy(j).start())
        for j in range(fill_ref.shape[0]):
            pl.when(fill_ref[j] >= 0)(lambda j=j: fill_copy(j).wait())

    def sort_tile(hn_ref):
        hn = hn_ref[...]
        r = lax.broadcasted_iota(I32, (SORT_CHUNK, TOK_TILE), 0).astype(I16)
        for c in range(SORT_ROWS // SORT_CHUNK):
            lp = (lp_ref[...] - c * SORT_CHUNK).astype(I16)
            hit = (r == lp[0:1, :]) | (r == lp[1:2, :]) | (r == lp[2:3, :]) | (r == lp[3:4, :])
            perm = jnp.where(hit, jnp.ones((), BF16), jnp.zeros((), BF16))
            rows[c * SORT_CHUNK:(c + 1) * SORT_CHUNK, :] = _pack_rows(_dot(perm, hn), is_bf16_valued=True)

    pl.when(i < steps_p)(lambda: sort_tile(hn_p_ref))
    pl.when(i >= steps_p)(lambda: sort_tile(hn_s_ref))
    _start_runs(runs, i, rows, xs_ref, sems.at[slot], to_global=True)
    pl.when(i > 0)(lambda: _wait_runs(runs, i - 1, rows2.at[1 - slot], xs_ref, sems.at[1 - slot]))
    pl.when(i == pl.num_programs(0) - 1)(lambda: _wait_runs(runs, i, rows, xs_ref, sems.at[slot]))


def _dispatch(fill, runs, lp, hn_p, hn_s, n_rows):
    tm = TOK_TILE
    steps_p, steps_s = hn_p.shape[0] // tm, hn_s.shape[0] // tm
    return pl.pallas_call(
        functools.partial(_dispatch_kernel, steps_p=steps_p),
        grid_spec=pltpu.PrefetchScalarGridSpec(
            num_scalar_prefetch=5,
            grid=(steps_p + steps_s,),
            in_specs=[pl.BlockSpec((TOP_K, tm), lambda i, *_: (0, i)),
                      pl.BlockSpec((tm, D_MODEL), lambda i, *_: (jnp.minimum(i, steps_p - 1), 0)),
                      pl.BlockSpec((tm, D_MODEL), lambda i, *_: (jnp.maximum(i - steps_p, 0), 0))],
            out_specs=pl.BlockSpec(memory_space=pl.ANY),
            scratch_shapes=[pltpu.VMEM((2, SORT_ROWS, ROW_WORDS), U32), pltpu.SemaphoreType.DMA((2,)),
                            pltpu.VMEM((MOE_BM, ROW_WORDS), U32), pltpu.SemaphoreType.DMA],
        ),
        out_shape=jax.ShapeDtypeStruct((n_rows, ROW_WORDS), U32),
        compiler_params=_cparams(("arbitrary",)),
        name="dispatch",
    )(fill, *runs, lp, hn_p, hn_s)


def _experts_kernel(be_ref, rows_ref, nu_ref, slot_ref, pa_ref, pb_ref, pc_ref, xs_ref, w1a_ref, w1b_ref, w2_ref,
                    b1g_ref, b1l_ref, b2_ref, sel_ref, y_ref, w1g_scr, w1l_scr, w2_scr):
    del nu_ref
    i = pl.program_id(0)
    used = rows_ref[i]
    half_k = D_MODEL // 2

    def arrives(piece_ref):
        return (i == 0) | (piece_ref[i] != piece_ref[jnp.maximum(i - 1, 0)])

    def take_w1_half(w_ref, piece_ref, r0):
        slot = slot_ref[piece_ref[i]]
        for c in range(D_FF // LANE):
            cols = _dot(w_ref[0, :, c * 2 * LANE:(c + 1) * 2 * LANE].astype(BF16), sel_ref[...])
            w1g_scr[slot, r0:r0 + half_k, c * LANE:(c + 1) * LANE] = cols[:, :LANE].astype(BF16)
            w1l_scr[slot, r0:r0 + half_k, c * LANE:(c + 1) * LANE] = cols[:, LANE:].astype(BF16)

    pl.when(arrives(pa_ref))(lambda: take_w1_half(w1a_ref, pa_ref, 0))
    pl.when(arrives(pb_ref))(lambda: take_w1_half(w1b_ref, pb_ref, half_k))

    @pl.when(arrives(pc_ref))
    def _():
        w2_scr[slot_ref[pc_ref[i]]] = w2_ref[0].astype(BF16)

    cur = slot_ref[be_ref[i]]

    def mlp(words):
        x_lo, x_hi = _unpack_rows(words)

        def up(w_scr, b_ref):
            return _dot(x_lo, w_scr[cur, :ROW_WORDS, :]) + _dot(x_hi, w_scr[cur, ROW_WORDS:, :]) + b_ref[0]

        x_glu = jnp.minimum(up(w1g_scr, b1g_ref), SWIGLU_LIMIT)
        x_lin = jnp.clip(up(w1l_scr, b1l_ref), -SWIGLU_LIMIT, SWIGLU_LIMIT)
        act = x_glu * jax.nn.sigmoid(SWIGLU_ALPHA * x_glu) * (x_lin + 1.0)
        return _pack_rows(_dot(act.astype(BF16), w2_scr[cur]) + b2_ref[0])

    half = MOE_BM // 2

    @pl.when(used > half)
    def _():
        y_ref[...] = mlp(xs_ref[...])

    @pl.when((used > 0) & (used <= half))
    def _():
        y_ref[:half, :] = mlp(xs_ref[:half, :])
        y_ref[half:, :] = jnp.zeros((half, ROW_WORDS), U32)

    @pl.when(used == 0)
    def _():
        y_ref[...] = jnp.zeros_like(y_ref)


def _experts(blk_e, blk_rows, n_used, slot_e, pieces, xs, w1, b1g, b1l, w2, b2, sel):
    nblk = xs.shape[0] // MOE_BM
    by_expert = lambda r, c: pl.BlockSpec((1, r, c), lambda i, be, *_: (be[i], 0, 0))
    piece = lambda p, r, c, blk: pl.BlockSpec((1, r, c), lambda i, be, rows, nu, sl, *pc: (pc[p][i], blk, 0))
    return pl.pallas_call(
        _experts_kernel,
        grid_spec=pltpu.PrefetchScalarGridSpec(
            num_scalar_prefetch=4 + len(pieces),
            grid=(nblk,),
            in_specs=[pl.BlockSpec((MOE_BM, ROW_WORDS), lambda i, be, rows, nu, *_: (jnp.minimum(i, nu[0] - 1), 0)),
                      piece(0, D_MODEL // 2, 2 * D_FF, 0), piece(1, D_MODEL // 2, 2 * D_FF, 1),
                      piece(2, D_FF, D_MODEL, 0),
                      by_expert(1, D_FF), by_expert(1, D_FF), by_expert(1, D_MODEL),
                      pl.BlockSpec((2 * LANE, 2 * LANE), lambda i, *_: (0, 0))],
            out_specs=pl.BlockSpec((MOE_BM, ROW_WORDS), lambda i, *_: (i, 0)),
            scratch_shapes=[pltpu.VMEM((2, D_MODEL, D_FF), BF16), pltpu.VMEM((2, D_MODEL, D_FF), BF16),
                            pltpu.VMEM((2, D_FF, D_MODEL), BF16)],
        ),
        out_shape=jax.ShapeDtypeStruct(xs.shape, U32),
        compiler_params=_cparams(("arbitrary",)),
        name="experts",
    )(blk_e, blk_rows, n_used, slot_e, *pieces, xs, w1, w1, w2, b1g, b1l, b2, sel)


def _combine_kernel(cnt_ref, off_ref, dest_ref, tot_ref, lp_ref, g_ref, h_ref, yb_ref, y_ref, rows2, sems, *, tile0):
    i = pl.program_id(0)
    runs = (cnt_ref, off_ref, dest_ref, tot_ref)
    slot = i % 2
    rows = rows2.at[slot]

    def fetch(step, into):
        _start_runs(runs, step + tile0, rows2.at[into], yb_ref, sems.at[into], to_global=False)

    @pl.when(i == 0)
    def _():
        rows2[...] = jnp.zeros_like(rows2)
        fetch(0, 0)

    pl.when(i + 1 < pl.num_programs(0))(lambda: fetch(i + 1, 1 - slot))
    _wait_runs(runs, i + tile0, rows, yb_ref, sems.at[slot])
    lp, g = lp_ref[...].astype(I16), g_ref[...].astype(BF16)
    r = lax.broadcasted_iota(I32, (TOK_TILE, SORT_ROWS), 1).astype(I16)
    mix = jnp.zeros((TOK_TILE, SORT_ROWS), BF16)
    for k in range(TOP_K):
        mix = jnp.where(r == lp[:, k:k + 1], g[:, k:k + 1], mix)
    y_lo, y_hi = _unpack_rows(rows[...])
    y_ref[...] = h_ref[...] + jnp.concatenate([_dot(mix, y_lo), _dot(mix, y_hi)], axis=-1)


def _combine(runs, lp_t, gate_t, h, yb, tile0):
    T = h.shape[0]
    tm = TOK_TILE
    tok = lambda w: pl.BlockSpec((tm, w), lambda i, *_: (i, 0))
    return pl.pallas_call(
        functools.partial(_combine_kernel, tile0=tile0),
        grid_spec=pltpu.PrefetchScalarGridSpec(
            num_scalar_prefetch=4,
            grid=(T // tm,),
            in_specs=[tok(TOP_K), tok(TOP_K), tok(D_MODEL), pl.BlockSpec(memory_space=pl.ANY)],
            out_specs=tok(D_MODEL),
            scratch_shapes=[pltpu.VMEM((2, SORT_ROWS, ROW_WORDS), U32), pltpu.SemaphoreType.DMA((2,))],
        ),
        out_shape=jax.ShapeDtypeStruct((T, D_MODEL), F32),
        compiler_params=_cparams(("arbitrary",)),
        name="combine",
    )(*runs, lp_t, gate_t, h, yb)


def _block_ones(width, block):
    idx = np.arange(width) // block
    return jnp.asarray(idx[:, None] == idx[None, :], BF16)


def _layer(x_p, x_s, cache_k, cache_v, s_gla, norm1, w_in, q_norm, k_norm, sinks, w_gk, b_gk,
           gla_norm, w_a, w_b, w_o, norm2, w_router, b_router, w1, b1, w2, b2):
    Bp, Lp, _ = x_p.shape
    Bs, Ls, _ = x_s.shape
    Tp, Ts = Bp * Lp, Bs * Ls

    w_packed = _pack_w_in(w_in)
    wgk = jnp.pad(w_gk, ((0, GL_PAD - GLA_GATE_RANK), (0, 0))).astype(BF16)
    n1 = norm1.reshape(1, D_MODEL)
    gq, gk = _block_ones(A_WIDTH, SWA_HEAD_DIM), _block_ones(KV_WIDTH, SWA_HEAD_DIM)
    qn = jnp.tile(q_norm, SWA_HEADS).reshape(1, A_WIDTH)
    kn = jnp.tile(k_norm, SWA_KV_HEADS).reshape(1, KV_WIDTH)
    bgk = b_gk.reshape(1, GLA_K_WIDTH)
    gn = gla_norm.reshape(1, GLA_DV)
    wa, wb, wo = w_a.astype(BF16), w_b.astype(BF16), w_o.astype(BF16)
    n2 = norm2.reshape(1, D_MODEL)
    wr = jnp.pad(w_router, ((0, 0), (0, LANE - N_EXPERTS))).astype(BF16)
    br = jnp.pad(b_router, (0, LANE - N_EXPERTS)).reshape(1, LANE)
    b1g = b1[:, 0::2].reshape(N_EXPERTS, 1, D_FF)
    b1l = b1[:, 1::2].reshape(N_EXPERTS, 1, D_FF)
    b2r = b2.reshape(N_EXPERTS, 1, D_MODEL)
    sel_np = np.zeros((2 * LANE, 2 * LANE), np.float32)
    sel_np[2 * np.arange(LANE), np.arange(LANE)] = 1.0
    sel_np[2 * np.arange(LANE) + 1, LANE + np.arange(LANE)] = 1.0
    sel = jnp.asarray(sel_np, BF16)

    def mix(x, k_past, v_past, s0):
        B, L, _ = x.shape
        T = B * L
        x2 = x.reshape(T, D_MODEL)
        r3 = lambda t: t.reshape(B, L, t.shape[-1])
        if k_past is None and L % TOK_TILE == 0:
            oa, ka, va, qb, kb, vb, la, rb, ga, gb = _inproj_swa(sinks, x2, L, n1, w_packed, gq, gk, qn, kn, wgk, bgk)
        else:
            qa, ka, va, qb, kb, vb, la, rb, ga, gb = _inproj(x2, n1, w_packed, gq, gk, qn, kn, wgk, bgk)
            oa = _swa(sinks, r3(qa), r3(ka), r3(va), k_past, v_past, has_past=k_past is not None)
        ob, s_out = _gla(r3(qb), r3(kb), r3(vb), r3(la), r3(rb), gn, s0)
        h, hn, gate, lp, cnt = _outproj(x2, oa.reshape(T, A_WIDTH), ob.reshape(T, B_WIDTH), ga, gb,
                                        wa, wb, wo, n2, wr, br, upper, lower)
        return r3(ka), r3(va), s_out, h, hn, gate, lp, cnt[:, 0].reshape(T // TOK_TILE, N_EXPERTS)

    upper = jnp.asarray(np.arange(TOK_TILE)[:, None] < np.arange(TOK_TILE)[None, :], BF16)
    lower = jnp.asarray(np.arange(N_EXPERTS)[:, None] > np.arange(N_EXPERTS)[None, :], BF16)
    win = cache_k.shape[1]
    assert win == WINDOW
    ck = cache_k.reshape(Bs, win, KV_WIDTH)
    cv = cache_v.reshape(Bs, win, KV_WIDTH)
    ka_p, va_p, s_p, h_p, hn_p, g_p, lp_p, cnt_p = mix(x_p, None, None, jnp.zeros((Bp, GLA_HEADS, GLA_DK, GLA_DV), F32))
    ka_s, va_s, s_s, h_s, hn_s, g_s, lp_s, cnt_s = mix(x_s, ck, cv, s_gla)

    T = Tp + Ts
    cnt = jnp.concatenate([cnt_p, cnt_s], axis=0)
    most_rows = T * TOP_K + cnt.size * (RUN_ALIGN - 1) + N_EXPERTS * (MOE_BM - 1)
    n_rows = -(-most_rows // MOE_BM) * MOE_BM
    counts = jnp.sum(cnt, axis=0)
    padded = (counts + MOE_BM - 1) // MOE_BM * MOE_BM
    pad_end = jnp.cumsum(padded)
    start = pad_end - padded
    dest = start[None, :] + jnp.cumsum(cnt, axis=0) - cnt
    off = jnp.cumsum(cnt, axis=1) - cnt
    flat = lambda a: a.reshape(-1).astype(I32)
    runs = (flat(cnt), flat(off), flat(dest), flat(jnp.sum(cnt, axis=1)))
    tails = jnp.where(padded > 0, pad_end - MOE_BM, -1)
    most_spare = (n_rows - T * TOP_K) // MOE_BM
    spare = pad_end[-1] + jnp.arange(most_spare, dtype=I32) * MOE_BM
    fill = jnp.concatenate([tails, jnp.where(spare < n_rows, spare, -1)]).astype(I32)
    blk_row = jnp.arange(n_rows // MOE_BM, dtype=I32) * MOE_BM
    blk_e = jnp.minimum(jnp.sum(pad_end[None, :] <= blk_row[:, None], axis=1), N_EXPERTS - 1).astype(I32)
    used_end = start + counts
    blk_rows = jnp.clip(jnp.sum(jnp.where(blk_e[:, None] == jnp.arange(N_EXPERTS)[None, :], used_end[None, :], 0),
                                axis=1) - blk_row, 0, MOE_BM)
    blk_rows = jnp.where(blk_row < pad_end[-1], blk_rows, 0).astype(I32)
    n_used = (pad_end[-1:] // MOE_BM).astype(I32)
    e_ids = jnp.arange(N_EXPERTS, dtype=I32)
    live = padded > 0
    blk_e = jnp.where(blk_row < pad_end[-1], blk_e, jnp.max(jnp.where(live, e_ids, 0)))
    later = jnp.where(live[None, :] & (e_ids[None, :] > e_ids[:, None]), e_ids[None, :], N_EXPERTS)
    nxt = jnp.min(later, axis=1)
    nxt = jnp.where(nxt < N_EXPERTS, nxt, e_ids)
    pick = lambda table: jnp.sum(jnp.where(blk_e[:, None] == e_ids[None, :], table[None, :], 0), axis=1)
    first_blk, next_first = pick(start // MOE_BM), pick(pad_end // MOE_BM)
    blk_i = jnp.arange(n_rows // MOE_BM, dtype=I32)
    pieces = tuple(jnp.where(blk_i >= jnp.maximum(next_first - lead, first_blk + 1), pick(nxt), blk_e).astype(I32)
                   for lead in EXPERT_PIECE_LEAD)
    slot_e = ((jnp.cumsum(live.astype(I32)) - 1) & 1).astype(I32)

    xs = _dispatch(fill, runs, jnp.concatenate([lp_p, lp_s], axis=1), hn_p, hn_s, n_rows)
    yb = _experts(blk_e, blk_rows, n_used, slot_e, pieces, xs, w1, b1g, b1l, w2, b2r, sel)
    y_p = _combine(runs, lp_p.T, g_p.T, h_p, yb, 0).reshape(Bp, Lp, D_MODEL)
    y_s = _combine(runs, lp_s.T, g_s.T, h_s, yb, Tp // TOK_TILE).reshape(Bs, Ls, D_MODEL)

    kv = lambda t: t.reshape(t.shape[0], t.shape[1], SWA_KV_HEADS, SWA_HEAD_DIM)
    k_p, v_p = kv(ka_p[:, -WINDOW:]), kv(va_p[:, -WINDOW:])
    k_s = jnp.concatenate([cache_k, kv(ka_s)], axis=1)[:, -win:]
    v_s = jnp.concatenate([cache_v, kv(va_s)], axis=1)[:, -win:]
    return y_p, y_s, (k_p, v_p, s_p, k_s, v_s, s_s)


def kernel(x_prompt, x_sample, cache_swa_k, cache_swa_v, state_gla, norm1, w_in, q_norm, k_norm, sinks, w_gk, b_gk,
           gla_norm, w_a, w_b, w_o, norm2, w_router, b_router, w1, b1, w2, b2):
    depth = norm1.shape[0]
    y_p, y_s = x_prompt, x_sample
    states = []
    for l in range(depth):
        y_p, y_s, st = _layer(y_p, y_s, cache_swa_k[l], cache_swa_v[l], state_gla[l], norm1[l], w_in[l], q_norm[l],
                              k_norm[l], sinks[l], w_gk[l], b_gk[l], gla_norm[l], w_a[l], w_b[l], w_o[l], norm2[l],
                              w_router[l], b_router[l], w1[l], b1[l], w2[l], b2[l])
        states.append(st)
    return (y_p, y_s) + tuple(jnp.stack([st[j] for st in states]) for j in range(6))
```

```python
import functools

import jax
import jax.numpy as jnp
import numpy as np
from jax import lax
from jax.experimental import pallas as pl
from jax.experimental.pallas import tpu as pltpu

F32 = jnp.float32
BF16 = jnp.bfloat16
I32 = jnp.int32
U32 = jnp.uint32
I16 = jnp.int16

D_MODEL = 1024
SWA_HEADS = 8
SWA_KV_HEADS = 2
SWA_GROUP = SWA_HEADS // SWA_KV_HEADS
SWA_HEAD_DIM = 64
WINDOW = 128
ATTN_SCALE = SWA_HEAD_DIM ** -0.5
GLA_HEADS = 4
GLA_DK = 64
GLA_DV = 128
GLA_GATE_RANK = 16
GLA_GATE_NORM = 16.0
GLA_SCALE = GLA_DK ** -0.5
A_WIDTH = SWA_HEADS * SWA_HEAD_DIM
KV_WIDTH = SWA_KV_HEADS * SWA_HEAD_DIM
GLA_K_WIDTH = GLA_HEADS * GLA_DK
B_WIDTH = GLA_HEADS * GLA_DV
N_EXPERTS = 32
TOP_K = 4
D_FF = D_MODEL
SWIGLU_ALPHA = 1.702
SWIGLU_LIMIT = 7.0
NORM_EPS = 1e-5
QK_EPS = 1e-6
NEG_INF = -1e30

LANE = 128
GL_PAD = LANE
VMEM_LIMIT = 56 * 1024 * 1024

_SEGS = (("qa", A_WIDTH), ("ka", KV_WIDTH), ("va", KV_WIDTH), ("qb", GLA_K_WIDTH), ("kb", GLA_K_WIDTH),
         ("vb", B_WIDTH), ("rb", B_WIDTH), ("ga", D_MODEL), ("gb", D_MODEL), ("gl", GL_PAD))
_OFF = {}
_o = 0
for _n, _w in _SEGS:
    _OFF[_n] = (_o, _w)
    _o += _w
IN_PACKED = _o

TOK_TILE = 512
GLA_CHUNK = 128
GLA_SUB = 16
MOE_BM = 512
EXPERT_ROW_STEP = 128
EXPERT_PIECE_LEAD = (3, 2, 1)
ROW_WORDS = D_MODEL // 2
RUN_ALIGN = 8
SORT_CHUNK = 256
SORT_ROWS = -(-(TOK_TILE * TOP_K + N_EXPERTS * (RUN_ALIGN - 1)) // SORT_CHUNK) * SORT_CHUNK
SAMPLE_BATCH = 8
GLA_LONG_BATCH = 8
GLA_SEQ_TILE = 512


def _cparams(sem):
    return pltpu.CompilerParams(dimension_semantics=sem, vmem_limit_bytes=VMEM_LIMIT)


def _split_bf16(v):
    hi = v.astype(BF16)
    lo = (v - hi.astype(F32)).astype(BF16)
    return hi, lo


def _dot(a, b):
    return jnp.dot(a, b, preferred_element_type=F32)


def _dot_nt(a, b):
    return lax.dot_general(a, b, (((1,), (1,)), ((), ())), preferred_element_type=F32)


def _dot_tn(a, b):
    return lax.dot_general(a, b, (((0,), (0,)), ((), ())), preferred_element_type=F32)


_SRC = {}
_o = 0
for _n, _w in (("qa", A_WIDTH), ("ka", KV_WIDTH), ("va", KV_WIDTH), ("qb", GLA_K_WIDTH), ("kb", GLA_K_WIDTH),
               ("vb", B_WIDTH), ("gl", GLA_GATE_RANK), ("rb", B_WIDTH), ("ga", D_MODEL), ("gb", D_MODEL)):
    _SRC[_n] = (_o, _w)
    _o += _w
IN_WIDTH = _o
PACK_ROWS = 128


def _pack_w_in_kernel(w_ref, o_ref):
    for name, width in _SEGS:
        src, have = _SRC[name]
        dst, _ = _OFF[name]
        o_ref[:, dst:dst + have] = w_ref[:, src:src + have].astype(BF16)
        if have < width:
            o_ref[:, dst + have:dst + width] = jnp.zeros((o_ref.shape[0], width - have), BF16)


def _pack_w_in(w_in):
    return pl.pallas_call(
        _pack_w_in_kernel,
        grid=(D_MODEL // PACK_ROWS,),
        in_specs=[pl.BlockSpec((PACK_ROWS, IN_WIDTH), lambda i: (i, 0))],
        out_specs=pl.BlockSpec((PACK_ROWS, IN_PACKED), lambda i: (i, 0)),
        out_shape=jax.ShapeDtypeStruct((D_MODEL, IN_PACKED), BF16),
        compiler_params=_cparams(("parallel",)),
        name="pack_w_in",
    )(w_in)


def _projections(x_ref, n1_ref, w_ref, gq_ref, gk_ref, qn_ref, kn_ref, wgk_ref, bgk_ref):
    x = x_ref[...]
    ms = jnp.mean(x * x, axis=-1, keepdims=True)
    xn = (x * lax.rsqrt(ms + NORM_EPS) * n1_ref[...]).astype(BF16)

    def proj(name):
        off, width = _OFF[name]
        return _dot(xn, w_ref[:, off:off + width])

    def head_norm(v, ones_ref, gain_ref):
        hi, lo = _split_bf16(v * v)
        ss = _dot(hi, ones_ref[...]) + _dot(lo, ones_ref[...])
        return v * lax.rsqrt(ss * (1.0 / SWA_HEAD_DIM) + QK_EPS) * gain_ref[...]

    def log_decay(gl):
        z = _dot(gl.astype(BF16), wgk_ref[...]) + bgk_ref[...]
        return (jnp.minimum(z, 0.0) - jnp.log1p(jnp.exp(-jnp.abs(z)))) * (1.0 / GLA_GATE_NORM)

    epilogue = {"qa": lambda u: head_norm(u, gq_ref, qn_ref) * ATTN_SCALE,
                "ka": lambda u: head_norm(u, gk_ref, kn_ref),
                "va": lambda u: u,
                "qb": lambda u: u * GLA_SCALE,
                "kb": lambda u: u,
                "vb": lambda u: u,
                "gl": log_decay,
                "rb": lambda u: u * jax.nn.sigmoid(u),
                "ga": jax.nn.sigmoid,
                "gb": jax.nn.sigmoid}
    return proj, epilogue


def _run_plan(proj, epilogue, plan, before=None, between=None):
    u = proj(plan[0][0])
    for idx, (name, out_ref) in enumerate(plan):
        if before is not None:
            before(idx)
        u_next = proj(plan[idx + 1][0]) if idx + 1 < len(plan) else None
        out_ref[...] = epilogue[name](u).astype(out_ref.dtype)
        if between is not None:
            between(idx)
        u = u_next


def _inproj_kernel(x_ref, n1_ref, w_ref, gq_ref, gk_ref, qn_ref, kn_ref, wgk_ref, bgk_ref,
                   qa_ref, ka_ref, va_ref, qb_ref, kb_ref, vb_ref, la_ref, rb_ref, ga_ref, gb_ref):
    proj, epilogue = _projections(x_ref, n1_ref, w_ref, gq_ref, gk_ref, qn_ref, kn_ref, wgk_ref, bgk_ref)
    _run_plan(proj, epilogue, (("qa", qa_ref), ("ka", ka_ref), ("va", va_ref), ("qb", qb_ref), ("kb", kb_ref),
                               ("vb", vb_ref), ("gl", la_ref), ("rb", rb_ref), ("ga", ga_ref), ("gb", gb_ref)))


def _inproj_swa_kernel(sink_ref, x_ref, n1_ref, w_ref, gq_ref, gk_ref, qn_ref, kn_ref, wgk_ref, bgk_ref,
                       oa_ref, ka_ref, va_ref, qb_ref, kb_ref, vb_ref, la_ref, rb_ref, ga_ref, gb_ref,
                       k_prev, v_prev, *, tiles_per_seq):
    t = pl.program_id(0)
    has_past = (t % tiles_per_seq) != 0

    @pl.when(t == 0)
    def _():
        k_prev[...] = jnp.zeros_like(k_prev)
        v_prev[...] = jnp.zeros_like(v_prev)

    proj, epilogue = _projections(x_ref, n1_ref, w_ref, gq_ref, gk_ref, qn_ref, kn_ref, wgk_ref, bgk_ref)
    tm = x_ref.shape[0]
    q = epilogue["qa"](proj("qa")).astype(BF16)
    k_new = epilogue["ka"](proj("ka"))
    v_new = proj("va")
    ka_ref[...] = k_new
    va_ref[...] = v_new
    k_all = jnp.concatenate([k_prev[...], k_new.astype(BF16)], axis=0)
    v_all = jnp.concatenate([v_prev[...], v_new.astype(BF16)], axis=0)
    k_prev[...] = k_all[tm:, :]
    v_prev[...] = v_all[tm:, :]

    qi = lax.broadcasted_iota(I32, (WINDOW, 2 * WINDOW), 0)
    ci = lax.broadcasted_iota(I32, (WINDOW, 2 * WINDOW), 1)
    dist = WINDOW + qi - ci
    band = (dist >= 0) & (dist < WINDOW)
    band_first = band & ((ci >= WINDOW) | has_past)
    outs = {}

    def keys_of(blk, h):
        return (slice(blk * WINDOW, (blk + 2) * WINDOW),
                slice(h // SWA_GROUP * SWA_HEAD_DIM, (h // SWA_GROUP + 1) * SWA_HEAD_DIM))

    def scores(blk, h):
        return _dot_nt(q[blk * WINDOW:(blk + 1) * WINDOW, h * SWA_HEAD_DIM:(h + 1) * SWA_HEAD_DIM],
                       k_all[keys_of(blk, h)])

    def finish(blk, h, s):
        s = jnp.where(band_first if blk == 0 else band, s, NEG_INF)
        sink = sink_ref[h]
        m = jnp.maximum(jnp.max(s, axis=-1, keepdims=True), sink)
        p = jnp.exp(s - m)
        denom = jnp.sum(p, axis=-1, keepdims=True) + jnp.exp(sink - m)
        outs[blk, h] = _dot(p.astype(BF16), v_all[keys_of(blk, h)]) / denom

    units = [(blk, h) for blk in range(tm // WINDOW) for h in range(SWA_HEADS)]
    plan = (("qb", qb_ref), ("kb", kb_ref), ("vb", vb_ref), ("gl", la_ref), ("rb", rb_ref), ("ga", ga_ref),
            ("gb", gb_ref))

    ahead = [_OFF[name][1] for name, _ in plan[1:]] + [0]
    cuts = [len(units) * sum(ahead[:i]) // sum(ahead) for i in range(len(plan) + 1)]
    pending = {}

    def before(idx):
        for unit in units[cuts[idx]:cuts[idx + 1]]:
            pending[unit] = scores(*unit)

    def between(idx):
        for unit in units[cuts[idx]:cuts[idx + 1]]:
            finish(*unit, pending.pop(unit))

    _run_plan(proj, epilogue, plan, before, between)
    for blk in range(tm // WINDOW):
        oa_ref[blk * WINDOW:(blk + 1) * WINDOW, :] = jnp.concatenate(
            [outs[blk, h] for h in range(SWA_HEADS)], axis=-1).astype(BF16)


def _inproj(x2, n1, w_packed, gq, gk, qn, kn, wgk, bgk):
    T = x2.shape[0]
    tm = min(TOK_TILE, T)
    assert T % tm == 0

    def tok(width):
        return pl.BlockSpec((tm, width), lambda i: (i, 0))

    def const(shape):
        return pl.BlockSpec(shape, lambda i: (0, 0))

    outs = (("qa", A_WIDTH, BF16), ("ka", KV_WIDTH, F32), ("va", KV_WIDTH, F32), ("qb", GLA_K_WIDTH, BF16),
            ("kb", GLA_K_WIDTH, BF16), ("vb", B_WIDTH, BF16), ("la", GLA_K_WIDTH, F32), ("rb", B_WIDTH, BF16),
            ("ga", D_MODEL, BF16), ("gb", D_MODEL, BF16))
    return pl.pallas_call(
        _inproj_kernel,
        grid=(T // tm,),
        in_specs=[tok(D_MODEL), const((1, D_MODEL)), const((D_MODEL, IN_PACKED)), const((A_WIDTH, A_WIDTH)),
                  const((KV_WIDTH, KV_WIDTH)), const((1, A_WIDTH)), const((1, KV_WIDTH)),
                  const((GL_PAD, GLA_K_WIDTH)), const((1, GLA_K_WIDTH))],
        out_specs=[tok(w) for _, w, _ in outs],
        out_shape=[jax.ShapeDtypeStruct((T, w), dt) for _, w, dt in outs],
        compiler_params=_cparams(("parallel",)),
        name="inproj",
    )(x2, n1, w_packed, gq, gk, qn, kn, wgk, bgk)


def _inproj_swa(sinks, x2, seq_len, n1, w_packed, gq, gk, qn, kn, wgk, bgk):
    T = x2.shape[0]
    tm = TOK_TILE
    assert seq_len % tm == 0 and T % seq_len == 0
    tok = lambda width: pl.BlockSpec((tm, width), lambda i, s: (i, 0))
    const = lambda shape: pl.BlockSpec(shape, lambda i, s: (0, 0))
    outs = (("oa", A_WIDTH, BF16), ("ka", KV_WIDTH, F32), ("va", KV_WIDTH, F32), ("qb", GLA_K_WIDTH, BF16),
            ("kb", GLA_K_WIDTH, BF16), ("vb", B_WIDTH, BF16), ("la", GLA_K_WIDTH, F32), ("rb", B_WIDTH, BF16),
            ("ga", D_MODEL, BF16), ("gb", D_MODEL, BF16))
    return pl.pallas_call(
        functools.partial(_inproj_swa_kernel, tiles_per_seq=seq_len // tm),
        grid_spec=pltpu.PrefetchScalarGridSpec(
            num_scalar_prefetch=1,
            grid=(T // tm,),
            in_specs=[tok(D_MODEL), const((1, D_MODEL)), const((D_MODEL, IN_PACKED)), const((A_WIDTH, A_WIDTH)),
                      const((KV_WIDTH, KV_WIDTH)), const((1, A_WIDTH)), const((1, KV_WIDTH)),
                      const((GL_PAD, GLA_K_WIDTH)), const((1, GLA_K_WIDTH))],
            out_specs=[tok(w) for _, w, _ in outs],
            scratch_shapes=[pltpu.VMEM((WINDOW, KV_WIDTH), BF16), pltpu.VMEM((WINDOW, KV_WIDTH), BF16)],
        ),
        out_shape=[jax.ShapeDtypeStruct((T, w), dt) for _, w, dt in outs],
        compiler_params=_cparams(("arbitrary",)),
        name="inproj_swa",
    )(sinks, x2, n1, w_packed, gq, gk, qn, kn, wgk, bgk)


def _swa_kernel(sink_ref, q_ref, kc_ref, vc_ref, kp_ref, vp_ref, o_ref, *, first_block_has_past):
    n = pl.program_id(1)
    nb, lq = q_ref.shape[0], q_ref.shape[1]
    assert lq & (lq - 1) == 0
    stack = SWA_GROUP if lq < WINDOW else 1
    rows, keys = stack * lq, WINDOW + lq
    qi = lax.broadcasted_iota(I32, (rows, keys), 0) & (lq - 1)
    ci = lax.broadcasted_iota(I32, (rows, keys), 1)
    dist = WINDOW + qi - ci
    mask = (dist >= 0) & (dist < WINDOW)
    if not first_block_has_past:
        mask = mask & ((ci >= WINDOW) | (n > 0))
    cols = lambda h0: slice(h0 // SWA_GROUP * SWA_HEAD_DIM, (h0 // SWA_GROUP + 1) * SWA_HEAD_DIM)
    qs, ks, vs = [], [], []
    for j in range(nb):
        q = q_ref[j]
        qs.append(q.astype(F32) if stack > 1 else q)
        ks.append(jnp.concatenate([kp_ref[j], kc_ref[j]], axis=0).astype(BF16))
        vs.append(jnp.concatenate([vp_ref[j], vc_ref[j]], axis=0).astype(BF16))
    groups = [(j, h0) for j in range(nb) for h0 in range(0, SWA_HEADS, stack)]
    wave = len(groups) if stack > 1 else 1
    outs = {}
    for w0 in range(0, len(groups), wave):
        scores, sinks = [], []
        for j, h0 in groups[w0:w0 + wave]:
            heads = range(h0, h0 + stack)
            qg = jnp.concatenate([qs[j][:, h * SWA_HEAD_DIM:(h + 1) * SWA_HEAD_DIM] for h in heads], axis=0)
            sinks.append(jnp.concatenate([jnp.full((lq, 1), sink_ref[h], F32) for h in heads], axis=0))
            scores.append(_dot_nt(qg.astype(BF16), ks[j][:, cols(h0)]))
        probs, denoms = [], []
        for s, sink in zip(scores, sinks):
            s = jnp.where(mask, s, NEG_INF)
            m = jnp.maximum(jnp.max(s, axis=-1, keepdims=True), sink)
            p = jnp.exp(s - m)
            denoms.append(jnp.sum(p, axis=-1, keepdims=True) + jnp.exp(sink - m))
            probs.append(p.astype(BF16))
        for (j, h0), p, denom in zip(groups[w0:w0 + wave], probs, denoms):
            o = _dot(p, vs[j][:, cols(h0)]) / denom
            outs.setdefault(j, []).extend(o[g * lq:(g + 1) * lq] for g in range(stack))
    for j in range(nb):
        o_ref[j] = jnp.concatenate(outs[j], axis=-1).astype(BF16)


def _swa(sinks, qa, ka, va, k_past, v_past, *, has_past):
    B, L, _ = qa.shape
    lq = min(L, WINDOW)
    nl = L // lq
    if has_past:
        assert nl == 1
        nb = min(B, SAMPLE_BATCH)
        past_spec = pl.BlockSpec((nb, WINDOW, KV_WIDTH), lambda b, n, s: (b, 0, 0))
        kp, vp = k_past, v_past
    else:
        assert lq == WINDOW
        nb = 1
        past_spec = pl.BlockSpec((nb, WINDOW, KV_WIDTH), lambda b, n, s: (b, jnp.maximum(n - 1, 0), 0))
        kp, vp = ka, va
    assert B % nb == 0
    cur = lambda w: pl.BlockSpec((nb, lq, w), lambda b, n, s: (b, n, 0))
    return pl.pallas_call(
        functools.partial(_swa_kernel, first_block_has_past=has_past),
        grid_spec=pltpu.PrefetchScalarGridSpec(
            num_scalar_prefetch=1,
            grid=(B // nb, nl),
            in_specs=[cur(A_WIDTH), cur(KV_WIDTH), cur(KV_WIDTH), past_spec, past_spec],
            out_specs=cur(A_WIDTH),
        ),
        out_shape=jax.ShapeDtypeStruct((B, L, A_WIDTH), BF16),
        compiler_params=_cparams(("parallel", "parallel")),
        name="swa",
    )(sinks, qa, ka, va, kp, vp)


def _gla_kernel(q_ref, k_ref, v_ref, la_ref, rb_ref, gn_ref, s0_ref, o_ref, sout_ref, s_scr, *, chunk, sub, length):
    C, SUB = chunk, sub
    S = C // SUB
    W = GLA_K_WIDTH
    nb = q_ref.shape[0]

    @pl.when(pl.program_id(1) == 0)
    def _():
        s_scr[...] = s0_ref[...]

    row = lax.broadcasted_iota(I32, (C, C), 0)
    col = lax.broadcasted_iota(I32, (C, C), 1)
    tri = (row >= col).astype(BF16)
    diag_mask = (row >= col) & ((row // SUB) == (col // SUB))
    eye = lax.broadcasted_iota(I32, (GLA_DK, GLA_DK), 0) == lax.broadcasted_iota(I32, (GLA_DK, GLA_DK), 1)
    krow = lax.broadcasted_iota(I32, (C, W), 0)

    def rows_of(x, r):
        return jnp.broadcast_to(x[r:r + 1, :], (SUB, W))

    ksl = lambda h: slice(h * GLA_DK, (h + 1) * GLA_DK)
    vsl = lambda h: slice(h * GLA_DV, (h + 1) * GLA_DV)

    def decays(j, r0):
        g_hi, g_lo = _split_bf16(la_ref[j, pl.ds(r0, C), :])
        return _dot(tri, g_hi) + _dot(tri, g_lo)

    def factors(j, r0, b):
        q = q_ref[j, pl.ds(r0, C), :].astype(F32)
        k = k_ref[j, pl.ds(r0, C), :].astype(F32)
        b_last = b[C - 1:C, :]
        mid = jnp.concatenate([rows_of(b, i * SUB + SUB // 2 - 1) for i in range(S)], axis=0)
        beta = jnp.concatenate([jnp.zeros((SUB, W), F32)] + [rows_of(b, i * SUB - 1) for i in range(1, S)], axis=0)
        k_off = [None]
        for i in range(1, S):
            e = jnp.exp(jnp.minimum(b[i * SUB - 1:i * SUB, :] - b, 0.0))
            k_off.append(jnp.where(krow < i * SUB, k * e, 0.0).astype(BF16))
        return dict(q_inter=(q * jnp.exp(b)).astype(BF16), k_state=(k * jnp.exp(b_last - b)).astype(BF16),
                    q_diag=(q * jnp.exp(b - mid)).astype(BF16), k_diag=(k * jnp.exp(mid - b)).astype(BF16),
                    q_off=(q * jnp.exp(b - beta)).astype(BF16), k_off=k_off, dec=jnp.exp(b_last))

    def intra(f, h):
        a_diag = _dot_nt(f["q_diag"][:, ksl(h)], f["k_diag"][:, ksl(h)])
        if S == 1:
            return jnp.where(diag_mask, a_diag, 0.0).astype(BF16)
        blocks = [jnp.zeros((SUB, C), F32)]
        for i in range(1, S):
            blocks.append(_dot_nt(f["q_off"][i * SUB:(i + 1) * SUB, ksl(h)], f["k_off"][i][:, ksl(h)]))
        return jnp.where(diag_mask, a_diag, jnp.concatenate(blocks, axis=0)).astype(BF16)

    def read_out(j, r0, f, a):
        v = v_ref[j, pl.ds(r0, C), :]
        res = []
        for h in range(GLA_HEADS):
            o_h = _dot(a[h], v[:, vsl(h)]) + _dot(f["q_inter"][:, ksl(h)], s_scr[j, h].astype(BF16))
            res.append((o_h, _dot_tn(f["k_state"][:, ksl(h)], v[:, vsl(h)])))
        return res

    def finish(j, r0, f, res):
        outs = []
        for h, (o_h, s_inc) in enumerate(res):
            dec = jnp.broadcast_to(f["dec"][:, ksl(h)], (GLA_DK, GLA_DK))
            dec_col = jnp.sum(jnp.where(eye, dec, 0.0), axis=-1, keepdims=True)
            s_scr[j, h] = dec_col * s_scr[j, h] + s_inc
            ms = jnp.mean(o_h * o_h, axis=-1, keepdims=True)
            outs.append(o_h * lax.rsqrt(ms + NORM_EPS) * gn_ref[...])
        o = jnp.concatenate(outs, axis=-1) * rb_ref[j, pl.ds(r0, C), :].astype(F32)
        o_ref[j, pl.ds(r0, C), :] = o.astype(BF16)

    def step(c):
        r0 = c * C if isinstance(c, int) else pl.multiple_of(c * C, C)
        bs = [decays(j, r0) for j in range(nb)]
        fs = [factors(j, r0, b) for j, b in enumerate(bs)]
        attn = [[intra(f, h) for h in range(GLA_HEADS)] for f in fs]
        res = [read_out(j, r0, fs[j], attn[j]) for j in range(nb)]
        for j in range(nb):
            finish(j, r0, fs[j], res[j])

    if length == C:
        step(0)
    else:
        lax.fori_loop(0, length // C, lambda c, carry: (step(c), carry)[1], 0)

    @pl.when(pl.program_id(1) == pl.num_programs(1) - 1)
    def _():
        sout_ref[...] = s_scr[...]


def _gla(qb, kb, vb, la, rb, gn, s0):
    B, L, _ = qb.shape
    chunk = GLA_CHUNK if L % GLA_CHUNK == 0 else L
    sub = GLA_SUB if chunk % GLA_SUB == 0 else chunk
    nb = min(B, SAMPLE_BATCH) if L == chunk else min(B, GLA_LONG_BATCH)
    lt = min(L, GLA_SEQ_TILE)
    assert B % nb == 0 and L % lt == 0 and lt % chunk == 0
    seq = lambda w: pl.BlockSpec((nb, lt, w), lambda b, l: (b, l, 0))
    st = pl.BlockSpec((nb, GLA_HEADS, GLA_DK, GLA_DV), lambda b, l: (b, 0, 0, 0))
    return pl.pallas_call(
        functools.partial(_gla_kernel, chunk=chunk, sub=sub, length=lt),
        grid=(B // nb, L // lt),
        in_specs=[seq(GLA_K_WIDTH), seq(GLA_K_WIDTH), seq(B_WIDTH), seq(GLA_K_WIDTH), seq(B_WIDTH),
                  pl.BlockSpec((1, GLA_DV), lambda b, l: (0, 0)), st],
        out_specs=[seq(B_WIDTH), st],
        out_shape=[jax.ShapeDtypeStruct((B, L, B_WIDTH), BF16),
                   jax.ShapeDtypeStruct((B, GLA_HEADS, GLA_DK, GLA_DV), F32)],
        scratch_shapes=[pltpu.VMEM((nb, GLA_HEADS, GLA_DK, GLA_DV), F32)],
        compiler_params=_cparams(("parallel", "arbitrary")),
        name="gla",
    )(qb, kb, vb, la, rb, gn, s0)


def _outproj_kernel(x_ref, oa_ref, ob_ref, ga_ref, gb_ref, wa_ref, wb_ref, wo_ref, n2_ref, wr_ref, br_ref, upper_ref,
                    lower_ref, h_ref, hn_ref, g_ref, lp_ref, cnt_ref):
    tm = x_ref.shape[0]
    merged = (ga_ref[...].astype(F32) * _dot(oa_ref[...], wa_ref[...])
              + gb_ref[...].astype(F32) * _dot(ob_ref[...], wb_ref[...]))
    h = x_ref[...] + _dot(merged.astype(BF16), wo_ref[...])
    h_ref[...] = h
    ms = jnp.mean(h * h, axis=-1, keepdims=True)
    hn = h * lax.rsqrt(ms + NORM_EPS) * n2_ref[...]
    hn_bf = hn.astype(BF16)
    hn_ref[...] = hn_bf
    logits = _dot(hn_bf, wr_ref[...]) + br_ref[...]
    lt = logits.T[:N_EXPERTS, :]
    eid = lax.broadcasted_iota(I32, (N_EXPERTS, tm), 0)
    vals, hots = [], []
    for k in range(TOP_K):
        m = jnp.max(lt, axis=0, keepdims=True)
        idx = jnp.min(jnp.where(lt == m, eid, N_EXPERTS), axis=0, keepdims=True)
        hot = eid == idx
        lt = jnp.where(hot, -jnp.inf, lt)
        vals.append(m)
        hots.append(hot)
    ex = [jnp.exp(vk - vals[0]) for vk in vals]
    den = ex[0] + ex[1] + ex[2] + ex[3]
    for k in range(TOP_K):
        g_ref[k:k + 1, :] = ex[k] / den
    multi = (hots[0] | hots[1] | hots[2] | hots[3]).astype(BF16)
    before = _dot(multi, upper_ref[...])
    counts = jnp.sum(multi.astype(F32), axis=1, keepdims=True)
    aligned = jnp.floor((counts + (RUN_ALIGN - 1)) * (1.0 / RUN_ALIGN)) * RUN_ALIGN
    run_start = _dot(lower_ref[...], jnp.broadcast_to(aligned, (N_EXPERTS, LANE)).astype(BF16))[:, :1]
    place = before + run_start
    for k in range(TOP_K):
        lp_ref[k:k + 1, :] = jnp.sum(jnp.where(hots[k], place, 0.0), axis=0, keepdims=True).astype(I32)
    cnt_ref[...] = jnp.broadcast_to(aligned, cnt_ref.shape).astype(I32)


def _outproj(x2, oa, ob, ga, gb, wa, wb, wo, n2, wr, br, upper, lower):
    T = x2.shape[0]
    tm = TOK_TILE
    assert T % tm == 0 and upper.shape == (tm, tm)
    tok = lambda w: pl.BlockSpec((tm, w), lambda i: (i, 0))
    const = lambda shape: pl.BlockSpec(shape, lambda i: (0, 0))
    kt = pl.BlockSpec((TOP_K, tm), lambda i: (0, i))
    return pl.pallas_call(
        _outproj_kernel,
        grid=(T // tm,),
        in_specs=[tok(D_MODEL), tok(A_WIDTH), tok(B_WIDTH), tok(D_MODEL), tok(D_MODEL),
                  const((A_WIDTH, D_MODEL)), const((B_WIDTH, D_MODEL)), const((D_MODEL, D_MODEL)),
                  const((1, D_MODEL)), const((D_MODEL, LANE)), const((1, LANE)), const((tm, tm)),
                  const((N_EXPERTS, N_EXPERTS))],
        out_specs=[tok(D_MODEL), tok(D_MODEL), kt, kt, pl.BlockSpec((N_EXPERTS, LANE), lambda i: (i, 0))],
        out_shape=[jax.ShapeDtypeStruct((T, D_MODEL), F32), jax.ShapeDtypeStruct((T, D_MODEL), BF16),
                   jax.ShapeDtypeStruct((TOP_K, T), F32), jax.ShapeDtypeStruct((TOP_K, T), I32),
                   jax.ShapeDtypeStruct((T // tm * N_EXPERTS, LANE), I32)],
        compiler_params=_cparams(("parallel",)),
        name="outproj",
    )(x2, oa, ob, ga, gb, wa, wb, wo, n2, wr, br, upper, lower)


def _pack_rows(v, *, is_bf16_valued=False):
    if not is_bf16_valued:
        v = v.astype(BF16).astype(F32)
    lo = pltpu.bitcast(v[:, :ROW_WORDS], U32) >> 16
    hi = pltpu.bitcast(v[:, ROW_WORDS:], U32) & jnp.uint32(0xFFFF0000)
    return lo | hi


def _unpack_rows(w):
    lo = pltpu.bitcast(w << 16, F32).astype(BF16)
    hi = pltpu.bitcast(w & jnp.uint32(0xFFFF0000), F32).astype(BF16)
    return lo, hi


def _start_runs(runs, tile, tile_rows, global_rows, sem, *, to_global):
    cnt_ref, off_ref, dest_ref, _ = runs

    def run(e):
        j = tile * N_EXPERTS + e
        n = pl.multiple_of(cnt_ref[j], RUN_ALIGN)
        local = tile_rows.at[pl.ds(pl.multiple_of(off_ref[j], RUN_ALIGN), n), :]
        far = global_rows.at[pl.ds(pl.multiple_of(dest_ref[j], RUN_ALIGN), n), :]
        return pltpu.make_async_copy(local, far, sem) if to_global else pltpu.make_async_copy(far, local, sem)

    for e in range(N_EXPERTS):
        pl.when(cnt_ref[tile * N_EXPERTS + e] > 0)(lambda e=e: run(e).start())


def _wait_runs(runs, tile, tile_rows, global_rows, sem):
    total = pl.multiple_of(runs[3][tile], RUN_ALIGN)
    pltpu.make_async_copy(global_rows.at[pl.ds(0, total), :], tile_rows.at[pl.ds(0, total), :], sem).wait()


def _dispatch_kernel(fill_ref, cnt_ref, off_ref, dest_ref, tot_ref, lp_ref, hn_p_ref, hn_s_ref, xs_ref,
                     rows2, sems, zbuf, zsem, *, steps_p):
    i = pl.program_id(0)
    runs = (cnt_ref, off_ref, dest_ref, tot_ref)
    slot = i % 2
    rows = rows2.at[slot]

    @pl.when(i == 0)
    def _():
        zbuf[...] = jnp.zeros_like(zbuf)

        def fill_copy(j):
            return pltpu.make_async_copy(zbuf, xs_ref.at[pl.ds(pl.multiple_of(fill_ref[j], MOE_BM), MOE_BM), :], zsem)

        for j in range(fill_ref.shape[0]):
            pl.when(fill_ref[j] >= 0)(lambda j=j: fill_copy(j).start())
        for j in range(fill_ref.shape[0]):
            pl.when(fill_ref[j] >= 0)(lambda j=j: fill_copy(j).wait())

    def sort_tile(hn_ref):
        hn = hn_ref[...]
        r = lax.broadcasted_iota(I32, (SORT_CHUNK, TOK_TILE), 0).astype(I16)
        for c in range(SORT_ROWS // SORT_CHUNK):
            lp = (lp_ref[...] - c * SORT_CHUNK).astype(I16)
            hit = (r == lp[0:1, :]) | (r == lp[1:2, :]) | (r == lp[2:3, :]) | (r == lp[3:4, :])
            perm = jnp.where(hit, jnp.ones((), BF16), jnp.zeros((), BF16))
            rows[c * SORT_CHUNK:(c + 1) * SORT_CHUNK, :] = _pack_rows(_dot(perm, hn), is_bf16_valued=True)

    pl.when(i < steps_p)(lambda: sort_tile(hn_p_ref))
    pl.when(i >= steps_p)(lambda: sort_tile(hn_s_ref))
    _start_runs(runs, i, rows, xs_ref, sems.at[slot], to_global=True)
    pl.when(i > 0)(lambda: _wait_runs(runs, i - 1, rows2.at[1 - slot], xs_ref, sems.at[1 - slot]))
    pl.when(i == pl.num_programs(0) - 1)(lambda: _wait_runs(runs, i, rows, xs_ref, sems.at[slot]))


def _dispatch(fill, runs, lp, hn_p, hn_s, n_rows):
    tm = TOK_TILE
    steps_p, steps_s = hn_p.shape[0] // tm, hn_s.shape[0] // tm
    return pl.pallas_call(
        functools.partial(_dispatch_kernel, steps_p=steps_p),
        grid_spec=pltpu.PrefetchScalarGridSpec(
            num_scalar_prefetch=5,
            grid=(steps_p + steps_s,),
            in_specs=[pl.BlockSpec((TOP_K, tm), lambda i, *_: (0, i)),
                      pl.BlockSpec((tm, D_MODEL), lambda i, *_: (jnp.minimum(i, steps_p - 1), 0)),
                      pl.BlockSpec((tm, D_MODEL), lambda i, *_: (jnp.maximum(i - steps_p, 0), 0))],
            out_specs=pl.BlockSpec(memory_space=pl.ANY),
            scratch_shapes=[pltpu.VMEM((2, SORT_ROWS, ROW_WORDS), U32), pltpu.SemaphoreType.DMA((2,)),
                            pltpu.VMEM((MOE_BM, ROW_WORDS), U32), pltpu.SemaphoreType.DMA],
        ),
        out_shape=jax.ShapeDtypeStruct((n_rows, ROW_WORDS), U32),
        compiler_params=_cparams(("arbitrary",)),
        name="dispatch",
    )(fill, *runs, lp, hn_p, hn_s)


def _experts_kernel(be_ref, rows_ref, nu_ref, slot_ref, pa_ref, pb_ref, pc_ref, xs_ref, w1a_ref, w1b_ref, w2_ref,
                    b1g_ref, b1l_ref, b2_ref, sel_ref, y_ref, w1g_scr, w1l_scr, w2_scr):
    del nu_ref
    i = pl.program_id(0)
    used = rows_ref[i]
    half_k = D_MODEL // 2

    def arrives(piece_ref):
        return (i == 0) | (piece_ref[i] != piece_ref[jnp.maximum(i - 1, 0)])

    def take_w1_half(w_ref, piece_ref, r0):
        slot = slot_ref[piece_ref[i]]
        for c in range(D_FF // LANE):
            cols = _dot(w_ref[0, :, c * 2 * LANE:(c + 1) * 2 * LANE].astype(BF16), sel_ref[...])
            w1g_scr[slot, r0:r0 + half_k, c * LANE:(c + 1) * LANE] = cols[:, :LANE].astype(BF16)
            w1l_scr[slot, r0:r0 + half_k, c * LANE:(c + 1) * LANE] = cols[:, LANE:].astype(BF16)

    pl.when(arrives(pa_ref))(lambda: take_w1_half(w1a_ref, pa_ref, 0))
    pl.when(arrives(pb_ref))(lambda: take_w1_half(w1b_ref, pb_ref, half_k))

    @pl.when(arrives(pc_ref))
    def _():
        w2_scr[slot_ref[pc_ref[i]]] = w2_ref[0].astype(BF16)

    cur = slot_ref[be_ref[i]]

    def mlp(words):
        x_lo, x_hi = _unpack_rows(words)

        def up(w_scr, b_ref):
            return _dot(x_lo, w_scr[cur, :ROW_WORDS, :]) + _dot(x_hi, w_scr[cur, ROW_WORDS:, :]) + b_ref[0]

        x_glu = jnp.minimum(up(w1g_scr, b1g_ref), SWIGLU_LIMIT)
        x_lin = jnp.clip(up(w1l_scr, b1l_ref), -SWIGLU_LIMIT, SWIGLU_LIMIT)
        act = x_glu * jax.nn.sigmoid(SWIGLU_ALPHA * x_glu) * (x_lin + 1.0)
        return _pack_rows(_dot(act.astype(BF16), w2_scr[cur]) + b2_ref[0])

    for rows in range(EXPERT_ROW_STEP, MOE_BM + 1, EXPERT_ROW_STEP):
        @pl.when((used > rows - EXPERT_ROW_STEP) & (used <= rows))
        def _(rows=rows):
            y_ref[:rows, :] = mlp(xs_ref[:rows, :])
            if rows < MOE_BM:
                y_ref[rows:, :] = jnp.zeros((MOE_BM - rows, ROW_WORDS), U32)

    @pl.when(used == 0)
    def _():
        y_ref[...] = jnp.zeros_like(y_ref)


def _experts(blk_e, blk_rows, n_used, slot_e, pieces, xs, w1, b1g, b1l, w2, b2, sel):
    nblk = xs.shape[0] // MOE_BM
    by_expert = lambda r, c: pl.BlockSpec((1, r, c), lambda i, be, *_: (be[i], 0, 0))
    piece = lambda p, r, c, blk: pl.BlockSpec((1, r, c), lambda i, be, rows, nu, sl, *pc: (pc[p][i], blk, 0))
    return pl.pallas_call(
        _experts_kernel,
        grid_spec=pltpu.PrefetchScalarGridSpec(
            num_scalar_prefetch=4 + len(pieces),
            grid=(nblk,),
            in_specs=[pl.BlockSpec((MOE_BM, ROW_WORDS), lambda i, be, rows, nu, *_: (jnp.minimum(i, nu[0] - 1), 0)),
                      piece(0, D_MODEL // 2, 2 * D_FF, 0), piece(1, D_MODEL // 2, 2 * D_FF, 1),
                      piece(2, D_FF, D_MODEL, 0),
                      by_expert(1, D_FF), by_expert(1, D_FF), by_expert(1, D_MODEL),
                      pl.BlockSpec((2 * LANE, 2 * LANE), lambda i, *_: (0, 0))],
            out_specs=pl.BlockSpec((MOE_BM, ROW_WORDS), lambda i, *_: (i, 0)),
            scratch_shapes=[pltpu.VMEM((2, D_MODEL, D_FF), BF16), pltpu.VMEM((2, D_MODEL, D_FF), BF16),
                            pltpu.VMEM((2, D_FF, D_MODEL), BF16)],
        ),
        out_shape=jax.ShapeDtypeStruct(xs.shape, U32),
        compiler_params=_cparams(("arbitrary",)),
        name="experts",
    )(blk_e, blk_rows, n_used, slot_e, *pieces, xs, w1, w1, w2, b1g, b1l, b2, sel)


def _combine_kernel(cnt_ref, off_ref, dest_ref, tot_ref, lp_ref, g_ref, h_ref, yb_ref, y_ref, rows2, sems, *, tile0):
    i = pl.program_id(0)
    runs = (cnt_ref, off_ref, dest_ref, tot_ref)
    slot = i % 2
    rows = rows2.at[slot]

    def fetch(step, into):
        _start_runs(runs, step + tile0, rows2.at[into], yb_ref, sems.at[into], to_global=False)

    @pl.when(i == 0)
    def _():
        rows2[...] = jnp.zeros_like(rows2)
        fetch(0, 0)

    pl.when(i + 1 < pl.num_programs(0))(lambda: fetch(i + 1, 1 - slot))
    _wait_runs(runs, i + tile0, rows, yb_ref, sems.at[slot])
    lp, g = lp_ref[...].astype(I16), g_ref[...].astype(BF16)
    r = lax.broadcasted_iota(I32, (TOK_TILE, SORT_ROWS), 1).astype(I16)
    mix = jnp.zeros((TOK_TILE, SORT_ROWS), BF16)
    for k in range(TOP_K):
        mix = jnp.where(r == lp[:, k:k + 1], g[:, k:k + 1], mix)
    y_lo, y_hi = _unpack_rows(rows[...])
    y_ref[...] = h_ref[...] + jnp.concatenate([_dot(mix, y_lo), _dot(mix, y_hi)], axis=-1)


def _combine(runs, lp_t, gate_t, h, yb, tile0):
    T = h.shape[0]
    tm = TOK_TILE
    tok = lambda w: pl.BlockSpec((tm, w), lambda i, *_: (i, 0))
    return pl.pallas_call(
        functools.partial(_combine_kernel, tile0=tile0),
        grid_spec=pltpu.PrefetchScalarGridSpec(
            num_scalar_prefetch=4,
            grid=(T // tm,),
            in_specs=[tok(TOP_K), tok(TOP_K), tok(D_MODEL), pl.BlockSpec(memory_space=pl.ANY)],
            out_specs=tok(D_MODEL),
            scratch_shapes=[pltpu.VMEM((2, SORT_ROWS, ROW_WORDS), U32), pltpu.SemaphoreType.DMA((2,))],
        ),
        out_shape=jax.ShapeDtypeStruct((T, D_MODEL), F32),
        compiler_params=_cparams(("arbitrary",)),
        name="combine",
    )(*runs, lp_t, gate_t, h, yb)


def _block_ones(width, block):
    idx = np.arange(width) // block
    return jnp.asarray(idx[:, None] == idx[None, :], BF16)


def _layer(x_p, x_s, cache_k, cache_v, s_gla, norm1, w_in, q_norm, k_norm, sinks, w_gk, b_gk,
           gla_norm, w_a, w_b, w_o, norm2, w_router, b_router, w1, b1, w2, b2):
    Bp, Lp, _ = x_p.shape
    Bs, Ls, _ = x_s.shape
    Tp, Ts = Bp * Lp, Bs * Ls

    w_packed = _pack_w_in(w_in)
    wgk = jnp.pad(w_gk, ((0, GL_PAD - GLA_GATE_RANK), (0, 0))).astype(BF16)
    n1 = norm1.reshape(1, D_MODEL)
    gq, gk = _block_ones(A_WIDTH, SWA_HEAD_DIM), _block_ones(KV_WIDTH, SWA_HEAD_DIM)
    qn = jnp.tile(q_norm, SWA_HEADS).reshape(1, A_WIDTH)
    kn = jnp.tile(k_norm, SWA_KV_HEADS).reshape(1, KV_WIDTH)
    bgk = b_gk.reshape(1, GLA_K_WIDTH)
    gn = gla_norm.reshape(1, GLA_DV)
    wa, wb, wo = w_a.astype(BF16), w_b.astype(BF16), w_o.astype(BF16)
    n2 = norm2.reshape(1, D_MODEL)
    wr = jnp.pad(w_router, ((0, 0), (0, LANE - N_EXPERTS))).astype(BF16)
    br = jnp.pad(b_router, (0, LANE - N_EXPERTS)).reshape(1, LANE)
    b1g = b1[:, 0::2].reshape(N_EXPERTS, 1, D_FF)
    b1l = b1[:, 1::2].reshape(N_EXPERTS, 1, D_FF)
    b2r = b2.reshape(N_EXPERTS, 1, D_MODEL)
    sel_np = np.zeros((2 * LANE, 2 * LANE), np.float32)
    sel_np[2 * np.arange(LANE), np.arange(LANE)] = 1.0
    sel_np[2 * np.arange(LANE) + 1, LANE + np.arange(LANE)] = 1.0
    sel = jnp.asarray(sel_np, BF16)

    def mix(x, k_past, v_past, s0):
        B, L, _ = x.shape
        T = B * L
        x2 = x.reshape(T, D_MODEL)
        r3 = lambda t: t.reshape(B, L, t.shape[-1])
        if k_past is None and L % TOK_TILE == 0:
            oa, ka, va, qb, kb, vb, la, rb, ga, gb = _inproj_swa(sinks, x2, L, n1, w_packed, gq, gk, qn, kn, wgk, bgk)
        else:
            qa, ka, va, qb, kb, vb, la, rb, ga, gb = _inproj(x2, n1, w_packed, gq, gk, qn, kn, wgk, bgk)
            oa = _swa(sinks, r3(qa), r3(ka), r3(va), k_past, v_past, has_past=k_past is not None)
        ob, s_out = _gla(r3(qb), r3(kb), r3(vb), r3(la), r3(rb), gn, s0)
        h, hn, gate, lp, cnt = _outproj(x2, oa.reshape(T, A_WIDTH), ob.reshape(T, B_WIDTH), ga, gb,
                                        wa, wb, wo, n2, wr, br, upper, lower)
        return r3(ka), r3(va), s_out, h, hn, gate, lp, cnt[:, 0].reshape(T // TOK_TILE, N_EXPERTS)

    upper = jnp.asarray(np.arange(TOK_TILE)[:, None] < np.arange(TOK_TILE)[None, :], BF16)
    lower = jnp.asarray(np.arange(N_EXPERTS)[:, None] > np.arange(N_EXPERTS)[None, :], BF16)
    win = cache_k.shape[1]
    assert win == WINDOW
    ck = cache_k.reshape(Bs, win, KV_WIDTH)
    cv = cache_v.reshape(Bs, win, KV_WIDTH)
    ka_p, va_p, s_p, h_p, hn_p, g_p, lp_p, cnt_p = mix(x_p, None, None, jnp.zeros((Bp, GLA_HEADS, GLA_DK, GLA_DV), F32))
    ka_s, va_s, s_s, h_s, hn_s, g_s, lp_s, cnt_s = mix(x_s, ck, cv, s_gla)

    T = Tp + Ts
    cnt = jnp.concatenate([cnt_p, cnt_s], axis=0)
    most_rows = T * TOP_K + cnt.size * (RUN_ALIGN - 1) + N_EXPERTS * (MOE_BM - 1)
    n_rows = -(-most_rows // MOE_BM) * MOE_BM
    counts = jnp.sum(cnt, axis=0)
    padded = (counts + MOE_BM - 1) // MOE_BM * MOE_BM
    pad_end = jnp.cumsum(padded)
    start = pad_end - padded
    dest = start[None, :] + jnp.cumsum(cnt, axis=0) - cnt
    off = jnp.cumsum(cnt, axis=1) - cnt
    flat = lambda a: a.reshape(-1).astype(I32)
    runs = (flat(cnt), flat(off), flat(dest), flat(jnp.sum(cnt, axis=1)))
    tails = jnp.where(padded > 0, pad_end - MOE_BM, -1)
    most_spare = (n_rows - T * TOP_K) // MOE_BM
    spare = pad_end[-1] + jnp.arange(most_spare, dtype=I32) * MOE_BM
    fill = jnp.concatenate([tails, jnp.where(spare < n_rows, spare, -1)]).astype(I32)
    blk_row = jnp.arange(n_rows // MOE_BM, dtype=I32) * MOE_BM
    blk_e = jnp.minimum(jnp.sum(pad_end[None, :] <= blk_row[:, None], axis=1), N_EXPERTS - 1).astype(I32)
    used_end = start + counts
    blk_rows = jnp.clip(jnp.sum(jnp.where(blk_e[:, None] == jnp.arange(N_EXPERTS)[None, :], used_end[None, :], 0),
                                axis=1) - blk_row, 0, MOE_BM)
    blk_rows = jnp.where(blk_row < pad_end[-1], blk_rows, 0).astype(I32)
    n_used = (pad_end[-1:] // MOE_BM).astype(I32)
    e_ids = jnp.arange(N_EXPERTS, dtype=I32)
    live = padded > 0
    blk_e = jnp.where(blk_row < pad_end[-1], blk_e, jnp.max(jnp.where(live, e_ids, 0)))
    later = jnp.where(live[None, :] & (e_ids[None, :] > e_ids[:, None]), e_ids[None, :], N_EXPERTS)
    nxt = jnp.min(later, axis=1)
    nxt = jnp.where(nxt < N_EXPERTS, nxt, e_ids)
    pick = lambda table: jnp.sum(jnp.where(blk_e[:, None] == e_ids[None, :], table[None, :], 0), axis=1)
    first_blk, next_first = pick(start // MOE_BM), pick(pad_end // MOE_BM)
    blk_i = jnp.arange(n_rows // MOE_BM, dtype=I32)
    pieces = tuple(jnp.where(blk_i >= jnp.maximum(next_first - lead, first_blk + 1), pick(nxt), blk_e).astype(I32)
                   for lead in EXPERT_PIECE_LEAD)
    slot_e = ((jnp.cumsum(live.astype(I32)) - 1) & 1).astype(I32)

    xs = _dispatch(fill, runs, jnp.concatenate([lp_p, lp_s], axis=1), hn_p, hn_s, n_rows)
    yb = _experts(blk_e, blk_rows, n_used, slot_e, pieces, xs, w1, b1g, b1l, w2, b2r, sel)
    y_p = _combine(runs, lp_p.T, g_p.T, h_p, yb, 0).reshape(Bp, Lp, D_MODEL)
    y_s = _combine(runs, lp_s.T, g_s.T, h_s, yb, Tp // TOK_TILE).reshape(Bs, Ls, D_MODEL)

    kv = lambda t: t.reshape(t.shape[0], t.shape[1], SWA_KV_HEADS, SWA_HEAD_DIM)
    k_p, v_p = kv(ka_p[:, -WINDOW:]), kv(va_p[:, -WINDOW:])
    k_s = jnp.concatenate([cache_k, kv(ka_s)], axis=1)[:, -win:]
    v_s = jnp.concatenate([cache_v, kv(va_s)], axis=1)[:, -win:]
    return y_p, y_s, (k_p, v_p, s_p, k_s, v_s, s_s)


def kernel(x_prompt, x_sample, cache_swa_k, cache_swa_v, state_gla, norm1, w_in, q_norm, k_norm, sinks, w_gk, b_gk,
           gla_norm, w_a, w_b, w_o, norm2, w_router, b_router, w1, b1, w2, b2):
    depth = norm1.shape[0]
    y_p, y_s = x_prompt, x_sample
    states = []
    for l in range(depth):
        y_p, y_s, st = _layer(y_p, y_s, cache_swa_k[l], cache_swa_v[l], state_gla[l], norm1[l], w_in[l], q_norm[l],
                              k_norm[l], sinks[l], w_gk[l], b_gk[l], gla_norm[l], w_a[l], w_b[l], w_o[l], norm2[l],
                              w_router[l], b_router[l], w1[l], b1[l], w2[l], b2[l])
        states.append(st)
    return (y_p, y_s) + tuple(jnp.stack([st[j] for st in states]) for j in range(6))
```

```python
import functools

import jax
import jax.numpy as jnp
import numpy as np
from jax import lax
from jax.experimental import pallas as pl
from jax.experimental.pallas import tpu as pltpu

F32 = jnp.float32
BF16 = jnp.bfloat16
I32 = jnp.int32
U32 = jnp.uint32
I16 = jnp.int16

D_MODEL = 1024
SWA_HEADS = 8
SWA_KV_HEADS = 2
SWA_GROUP = SWA_HEADS // SWA_KV_HEADS
SWA_HEAD_DIM = 64
WINDOW = 128
ATTN_SCALE = SWA_HEAD_DIM ** -0.5
GLA_HEADS = 4
GLA_DK = 64
GLA_DV = 128
GLA_GATE_RANK = 16
GLA_GATE_NORM = 16.0
GLA_SCALE = GLA_DK ** -0.5
A_WIDTH = SWA_HEADS * SWA_HEAD_DIM
KV_WIDTH = SWA_KV_HEADS * SWA_HEAD_DIM
GLA_K_WIDTH = GLA_HEADS * GLA_DK
B_WIDTH = GLA_HEADS * GLA_DV
N_EXPERTS = 32
TOP_K = 4
D_FF = D_MODEL
SWIGLU_ALPHA = 1.702
SWIGLU_LIMIT = 7.0
NORM_EPS = 1e-5
QK_EPS = 1e-6
NEG_INF = -1e30

LANE = 128
GL_PAD = LANE
VMEM_LIMIT = 56 * 1024 * 1024

_SEGS = (("qa", A_WIDTH), ("ka", KV_WIDTH), ("va", KV_WIDTH), ("qb", GLA_K_WIDTH), ("kb", GLA_K_WIDTH),
         ("vb", B_WIDTH), ("rb", B_WIDTH), ("ga", D_MODEL), ("gb", D_MODEL), ("gl", GL_PAD))
_OFF = {}
_o = 0
for _n, _w in _SEGS:
    _OFF[_n] = (_o, _w)
    _o += _w
IN_PACKED = _o

TOK_TILE = 512
GLA_CHUNK = 128
GLA_SUB = 16
MOE_BM = 512
EXPERT_ROW_STEP = 128
EXPERT_PIECE_LEAD = (3, 2, 1)
ROW_WORDS = D_MODEL // 2
RUN_ALIGN = 8
SORT_CHUNK = 256
SORT_ROWS = -(-(TOK_TILE * TOP_K + N_EXPERTS * (RUN_ALIGN - 1)) // SORT_CHUNK) * SORT_CHUNK
SAMPLE_BATCH = 8
GLA_LONG_BATCH = 8
GLA_SEQ_TILE = 512


def _cparams(sem):
    return pltpu.CompilerParams(dimension_semantics=sem, vmem_limit_bytes=VMEM_LIMIT)


def _split_bf16(v):
    hi = v.astype(BF16)
    lo = (v - hi.astype(F32)).astype(BF16)
    return hi, lo


def _dot(a, b):
    return jnp.dot(a, b, preferred_element_type=F32)


def _dot_nt(a, b):
    return lax.dot_general(a, b, (((1,), (1,)), ((), ())), preferred_element_type=F32)


def _dot_tn(a, b):
    return lax.dot_general(a, b, (((0,), (0,)), ((), ())), preferred_element_type=F32)


_SRC = {}
_o = 0
for _n, _w in (("qa", A_WIDTH), ("ka", KV_WIDTH), ("va", KV_WIDTH), ("qb", GLA_K_WIDTH), ("kb", GLA_K_WIDTH),
               ("vb", B_WIDTH), ("gl", GLA_GATE_RANK), ("rb", B_WIDTH), ("ga", D_MODEL), ("gb", D_MODEL)):
    _SRC[_n] = (_o, _w)
    _o += _w
IN_WIDTH = _o
PACK_ROWS = 128


def _pack_w_in_kernel(w_ref, o_ref):
    for name, width in _SEGS:
        src, have = _SRC[name]
        dst, _ = _OFF[name]
        o_ref[:, dst:dst + have] = w_ref[:, src:src + have].astype(BF16)
        if have < width:
            o_ref[:, dst + have:dst + width] = jnp.zeros((o_ref.shape[0], width - have), BF16)


def _pack_w_in(w_in):
    return pl.pallas_call(
        _pack_w_in_kernel,
        grid=(D_MODEL // PACK_ROWS,),
        in_specs=[pl.BlockSpec((PACK_ROWS, IN_WIDTH), lambda i: (i, 0))],
        out_specs=pl.BlockSpec((PACK_ROWS, IN_PACKED), lambda i: (i, 0)),
        out_shape=jax.ShapeDtypeStruct((D_MODEL, IN_PACKED), BF16),
        compiler_params=_cparams(("parallel",)),
        name="pack_w_in",
    )(w_in)


def _projections(x_ref, n1_ref, w_ref, gq_ref, gk_ref, qn_ref, kn_ref, wgk_ref, bgk_ref):
    x = x_ref[...]
    ms = jnp.mean(x * x, axis=-1, keepdims=True)
    xn = (x * lax.rsqrt(ms + NORM_EPS) * n1_ref[...]).astype(BF16)

    def proj(name):
        off, width = _OFF[name]
        return _dot(xn, w_ref[:, off:off + width])

    def head_norm(v, ones_ref, gain_ref):
        hi, lo = _split_bf16(v * v)
        ss = _dot(hi, ones_ref[...]) + _dot(lo, ones_ref[...])
        return v * lax.rsqrt(ss * (1.0 / SWA_HEAD_DIM) + QK_EPS) * gain_ref[...]

    def log_decay(gl):
        z = _dot(gl.astype(BF16), wgk_ref[...]) + bgk_ref[...]
        return (jnp.minimum(z, 0.0) - jnp.log1p(jnp.exp(-jnp.abs(z)))) * (1.0 / GLA_GATE_NORM)

    epilogue = {"qa": lambda u: head_norm(u, gq_ref, qn_ref) * ATTN_SCALE,
                "ka": lambda u: head_norm(u, gk_ref, kn_ref),
                "va": lambda u: u,
                "qb": lambda u: u * GLA_SCALE,
                "kb": lambda u: u,
                "vb": lambda u: u,
                "gl": log_decay,
                "rb": lambda u: u * jax.nn.sigmoid(u),
                "ga": jax.nn.sigmoid,
                "gb": jax.nn.sigmoid}
    return proj, epilogue


def _run_plan(proj, epilogue, plan, before=None, between=None):
    u = proj(plan[0][0])
    for idx, (name, out_ref) in enumerate(plan):
        if before is not None:
            before(idx)
        u_next = proj(plan[idx + 1][0]) if idx + 1 < len(plan) else None
        out_ref[...] = epilogue[name](u).astype(out_ref.dtype)
        if between is not None:
            between(idx)
        u = u_next


def _inproj_kernel(x_ref, n1_ref, w_ref, gq_ref, gk_ref, qn_ref, kn_ref, wgk_ref, bgk_ref,
                   qa_ref, ka_ref, va_ref, qb_ref, kb_ref, vb_ref, la_ref, rb_ref, ga_ref, gb_ref):
    proj, epilogue = _projections(x_ref, n1_ref, w_ref, gq_ref, gk_ref, qn_ref, kn_ref, wgk_ref, bgk_ref)
    _run_plan(proj, epilogue, (("qa", qa_ref), ("ka", ka_ref), ("va", va_ref), ("qb", qb_ref), ("kb", kb_ref),
                               ("vb", vb_ref), ("gl", la_ref), ("rb", rb_ref), ("ga", ga_ref), ("gb", gb_ref)))


def _inproj_swa_kernel(sink_ref, x_ref, n1_ref, w_ref, gq_ref, gk_ref, qn_ref, kn_ref, wgk_ref, bgk_ref,
                       oa_ref, ka_ref, va_ref, qb_ref, kb_ref, vb_ref, la_ref, rb_ref, ga_ref, gb_ref,
                       k_prev, v_prev, *, tiles_per_seq):
    t = pl.program_id(0)
    has_past = (t % tiles_per_seq) != 0

    @pl.when(t == 0)
    def _():
        k_prev[...] = jnp.zeros_like(k_prev)
        v_prev[...] = jnp.zeros_like(v_prev)

    proj, epilogue = _projections(x_ref, n1_ref, w_ref, gq_ref, gk_ref, qn_ref, kn_ref, wgk_ref, bgk_ref)
    tm = x_ref.shape[0]
    q = epilogue["qa"](proj("qa")).astype(BF16)
    k_new = epilogue["ka"](proj("ka"))
    v_new = proj("va")
    ka_ref[...] = k_new
    va_ref[...] = v_new
    k_all = jnp.concatenate([k_prev[...], k_new.astype(BF16)], axis=0)
    v_all = jnp.concatenate([v_prev[...], v_new.astype(BF16)], axis=0)
    k_prev[...] = k_all[tm:, :]
    v_prev[...] = v_all[tm:, :]

    qi = lax.broadcasted_iota(I32, (WINDOW, 2 * WINDOW), 0)
    ci = lax.broadcasted_iota(I32, (WINDOW, 2 * WINDOW), 1)
    dist = WINDOW + qi - ci
    band = (dist >= 0) & (dist < WINDOW)
    band_first = band & ((ci >= WINDOW) | has_past)
    outs = {}

    def keys_of(blk, h):
        return (slice(blk * WINDOW, (blk + 2) * WINDOW),
                slice(h // SWA_GROUP * SWA_HEAD_DIM, (h // SWA_GROUP + 1) * SWA_HEAD_DIM))

    def scores(blk, h):
        return _dot_nt(q[blk * WINDOW:(blk + 1) * WINDOW, h * SWA_HEAD_DIM:(h + 1) * SWA_HEAD_DIM],
                       k_all[keys_of(blk, h)])

    def finish(blk, h, s):
        s = jnp.where(band_first if blk == 0 else band, s, NEG_INF)
        sink = sink_ref[h]
        m = jnp.maximum(jnp.max(s, axis=-1, keepdims=True), sink)
        p = jnp.exp(s - m)
        denom = jnp.sum(p, axis=-1, keepdims=True) + jnp.exp(sink - m)
        outs[blk, h] = _dot(p.astype(BF16), v_all[keys_of(blk, h)]) / denom

    units = [(blk, h) for blk in range(tm // WINDOW) for h in range(SWA_HEADS)]
    plan = (("qb", qb_ref), ("kb", kb_ref), ("vb", vb_ref), ("gl", la_ref), ("rb", rb_ref), ("ga", ga_ref),
            ("gb", gb_ref))

    ahead = [_OFF[name][1] for name, _ in plan[1:]] + [0]
    cuts = [len(units) * sum(ahead[:i]) // sum(ahead) for i in range(len(plan) + 1)]
    pending = {}

    def before(idx):
        for unit in units[cuts[idx]:cuts[idx + 1]]:
            pending[unit] = scores(*unit)

    def between(idx):
        for unit in units[cuts[idx]:cuts[idx + 1]]:
            finish(*unit, pending.pop(unit))

    _run_plan(proj, epilogue, plan, before, between)
    for blk in range(tm // WINDOW):
        oa_ref[blk * WINDOW:(blk + 1) * WINDOW, :] = jnp.concatenate(
            [outs[blk, h] for h in range(SWA_HEADS)], axis=-1).astype(BF16)


def _inproj(x2, n1, w_packed, gq, gk, qn, kn, wgk, bgk):
    T = x2.shape[0]
    tm = min(TOK_TILE, T)
    assert T % tm == 0

    def tok(width):
        return pl.BlockSpec((tm, width), lambda i: (i, 0))

    def const(shape):
        return pl.BlockSpec(shape, lambda i: (0, 0))

    outs = (("qa", A_WIDTH, BF16), ("ka", KV_WIDTH, F32), ("va", KV_WIDTH, F32), ("qb", GLA_K_WIDTH, BF16),
            ("kb", GLA_K_WIDTH, BF16), ("vb", B_WIDTH, BF16), ("la", GLA_K_WIDTH, F32), ("rb", B_WIDTH, BF16),
            ("ga", D_MODEL, BF16), ("gb", D_MODEL, BF16))
    return pl.pallas_call(
        _inproj_kernel,
        grid=(T // tm,),
        in_specs=[tok(D_MODEL), const((1, D_MODEL)), const((D_MODEL, IN_PACKED)), const((A_WIDTH, A_WIDTH)),
                  const((KV_WIDTH, KV_WIDTH)), const((1, A_WIDTH)), const((1, KV_WIDTH)),
                  const((GL_PAD, GLA_K_WIDTH)), const((1, GLA_K_WIDTH))],
        out_specs=[tok(w) for _, w, _ in outs],
        out_shape=[jax.ShapeDtypeStruct((T, w), dt) for _, w, dt in outs],
        compiler_params=_cparams(("parallel",)),
        name="inproj",
    )(x2, n1, w_packed, gq, gk, qn, kn, wgk, bgk)


def _inproj_swa(sinks, x2, seq_len, n1, w_packed, gq, gk, qn, kn, wgk, bgk):
    T = x2.shape[0]
    tm = TOK_TILE
    assert seq_len % tm == 0 and T % seq_len == 0
    tok = lambda width: pl.BlockSpec((tm, width), lambda i, s: (i, 0))
    const = lambda shape: pl.BlockSpec(shape, lambda i, s: (0, 0))
    outs = (("oa", A_WIDTH, BF16), ("ka", KV_WIDTH, F32), ("va", KV_WIDTH, F32), ("qb", GLA_K_WIDTH, BF16),
            ("kb", GLA_K_WIDTH, BF16), ("vb", B_WIDTH, BF16), ("la", GLA_K_WIDTH, F32), ("rb", B_WIDTH, BF16),
            ("ga", D_MODEL, BF16), ("gb", D_MODEL, BF16))
    return pl.pallas_call(
        functools.partial(_inproj_swa_kernel, tiles_per_seq=seq_len // tm),
        grid_spec=pltpu.PrefetchScalarGridSpec(
            num_scalar_prefetch=1,
            grid=(T // tm,),
            in_specs=[tok(D_MODEL), const((1, D_MODEL)), const((D_MODEL, IN_PACKED)), const((A_WIDTH, A_WIDTH)),
                      const((KV_WIDTH, KV_WIDTH)), const((1, A_WIDTH)), const((1, KV_WIDTH)),
                      const((GL_PAD, GLA_K_WIDTH)), const((1, GLA_K_WIDTH))],
            out_specs=[tok(w) for _, w, _ in outs],
            scratch_shapes=[pltpu.VMEM((WINDOW, KV_WIDTH), BF16), pltpu.VMEM((WINDOW, KV_WIDTH), BF16)],
        ),
        out_shape=[jax.ShapeDtypeStruct((T, w), dt) for _, w, dt in outs],
        compiler_params=_cparams(("arbitrary",)),
        name="inproj_swa",
    )(sinks, x2, n1, w_packed, gq, gk, qn, kn, wgk, bgk)


def _run(phases):
    for _ in phases:
        pass


def _swa_kernel(*refs, **static):
    _run(_swa_phases(*refs, **static))


def _swa_phases(sink_ref, q_ref, kc_ref, vc_ref, kp_ref, vp_ref, o_ref, *, first_block_has_past):
    n = pl.program_id(1)
    nb, lq = q_ref.shape[0], q_ref.shape[1]
    assert lq & (lq - 1) == 0
    stack = SWA_GROUP if lq < WINDOW else 1
    rows, keys = stack * lq, WINDOW + lq
    qi = lax.broadcasted_iota(I32, (rows, keys), 0) & (lq - 1)
    ci = lax.broadcasted_iota(I32, (rows, keys), 1)
    dist = WINDOW + qi - ci
    mask = (dist >= 0) & (dist < WINDOW)
    if not first_block_has_past:
        mask = mask & ((ci >= WINDOW) | (n > 0))
    cols = lambda h0: slice(h0 // SWA_GROUP * SWA_HEAD_DIM, (h0 // SWA_GROUP + 1) * SWA_HEAD_DIM)
    qs, ks, vs = [], [], []
    for j in range(nb):
        q = q_ref[j]
        qs.append(q.astype(F32) if stack > 1 else q)
        ks.append(jnp.concatenate([kp_ref[j], kc_ref[j]], axis=0).astype(BF16))
        vs.append(jnp.concatenate([vp_ref[j], vc_ref[j]], axis=0).astype(BF16))
    groups = [(j, h0) for j in range(nb) for h0 in range(0, SWA_HEADS, stack)]
    wave = len(groups) if stack > 1 else 1
    outs = {}
    for w0 in range(0, len(groups), wave):
        scores, sinks = [], []
        for j, h0 in groups[w0:w0 + wave]:
            heads = range(h0, h0 + stack)
            qg = jnp.concatenate([qs[j][:, h * SWA_HEAD_DIM:(h + 1) * SWA_HEAD_DIM] for h in heads], axis=0)
            sinks.append(jnp.concatenate([jnp.full((lq, 1), sink_ref[h], F32) for h in heads], axis=0))
            scores.append(_dot_nt(qg.astype(BF16), ks[j][:, cols(h0)]))
        yield
        probs, denoms = [], []
        for s, sink in zip(scores, sinks):
            s = jnp.where(mask, s, NEG_INF)
            m = jnp.maximum(jnp.max(s, axis=-1, keepdims=True), sink)
            p = jnp.exp(s - m)
            denoms.append(jnp.sum(p, axis=-1, keepdims=True) + jnp.exp(sink - m))
            probs.append(p.astype(BF16))
        yield
        for (j, h0), p, denom in zip(groups[w0:w0 + wave], probs, denoms):
            o = _dot(p, vs[j][:, cols(h0)]) / denom
            outs.setdefault(j, []).extend(o[g * lq:(g + 1) * lq] for g in range(stack))
    for j in range(nb):
        o_ref[j] = jnp.concatenate(outs[j], axis=-1).astype(BF16)


def _swa(sinks, qa, ka, va, k_past, v_past, *, has_past):
    B, L, _ = qa.shape
    lq = min(L, WINDOW)
    nl = L // lq
    if has_past:
        assert nl == 1
        nb = min(B, SAMPLE_BATCH)
        past_spec = pl.BlockSpec((nb, WINDOW, KV_WIDTH), lambda b, n, s: (b, 0, 0))
        kp, vp = k_past, v_past
    else:
        assert lq == WINDOW
        nb = 1
        past_spec = pl.BlockSpec((nb, WINDOW, KV_WIDTH), lambda b, n, s: (b, jnp.maximum(n - 1, 0), 0))
        kp, vp = ka, va
    assert B % nb == 0
    cur = lambda w: pl.BlockSpec((nb, lq, w), lambda b, n, s: (b, n, 0))
    return pl.pallas_call(
        functools.partial(_swa_kernel, first_block_has_past=has_past),
        grid_spec=pltpu.PrefetchScalarGridSpec(
            num_scalar_prefetch=1,
            grid=(B // nb, nl),
            in_specs=[cur(A_WIDTH), cur(KV_WIDTH), cur(KV_WIDTH), past_spec, past_spec],
            out_specs=cur(A_WIDTH),
        ),
        out_shape=jax.ShapeDtypeStruct((B, L, A_WIDTH), BF16),
        compiler_params=_cparams(("parallel", "parallel")),
        name="swa",
    )(sinks, qa, ka, va, kp, vp)


def _gla_kernel(*refs, **static):
    _run(_gla_phases(*refs, **static))


def _gla_phases(q_ref, k_ref, v_ref, la_ref, rb_ref, gn_ref, s0_ref, o_ref, sout_ref, s_scr, *, chunk, sub, length,
                whole_sequence=False):
    C, SUB = chunk, sub
    S = C // SUB
    W = GLA_K_WIDTH
    nb = q_ref.shape[0]

    if whole_sequence:
        s_scr[...] = s0_ref[...]
    else:
        @pl.when(pl.program_id(1) == 0)
        def _():
            s_scr[...] = s0_ref[...]

    row = lax.broadcasted_iota(I32, (C, C), 0)
    col = lax.broadcasted_iota(I32, (C, C), 1)
    tri = (row >= col).astype(BF16)
    diag_mask = (row >= col) & ((row // SUB) == (col // SUB))
    eye = lax.broadcasted_iota(I32, (GLA_DK, GLA_DK), 0) == lax.broadcasted_iota(I32, (GLA_DK, GLA_DK), 1)
    krow = lax.broadcasted_iota(I32, (C, W), 0)

    def rows_of(x, r):
        return jnp.broadcast_to(x[r:r + 1, :], (SUB, W))

    ksl = lambda h: slice(h * GLA_DK, (h + 1) * GLA_DK)
    vsl = lambda h: slice(h * GLA_DV, (h + 1) * GLA_DV)

    def decays(j, r0):
        g_hi, g_lo = _split_bf16(la_ref[j, pl.ds(r0, C), :])
        return _dot(tri, g_hi) + _dot(tri, g_lo)

    def factors(j, r0, b):
        q = q_ref[j, pl.ds(r0, C), :].astype(F32)
        k = k_ref[j, pl.ds(r0, C), :].astype(F32)
        b_last = b[C - 1:C, :]
        mid = jnp.concatenate([rows_of(b, i * SUB + SUB // 2 - 1) for i in range(S)], axis=0)
        beta = jnp.concatenate([jnp.zeros((SUB, W), F32)] + [rows_of(b, i * SUB - 1) for i in range(1, S)], axis=0)
        k_off = [None]
        for i in range(1, S):
            e = jnp.exp(jnp.minimum(b[i * SUB - 1:i * SUB, :] - b, 0.0))
            k_off.append(jnp.where(krow < i * SUB, k * e, 0.0).astype(BF16))
        return dict(q_inter=(q * jnp.exp(b)).astype(BF16), k_state=(k * jnp.exp(b_last - b)).astype(BF16),
                    q_diag=(q * jnp.exp(b - mid)).astype(BF16), k_diag=(k * jnp.exp(mid - b)).astype(BF16),
                    q_off=(q * jnp.exp(b - beta)).astype(BF16), k_off=k_off, dec=jnp.exp(b_last))

    def intra(f, h):
        a_diag = _dot_nt(f["q_diag"][:, ksl(h)], f["k_diag"][:, ksl(h)])
        if S == 1:
            return jnp.where(diag_mask, a_diag, 0.0).astype(BF16)
        blocks = [jnp.zeros((SUB, C), F32)]
        for i in range(1, S):
            blocks.append(_dot_nt(f["q_off"][i * SUB:(i + 1) * SUB, ksl(h)], f["k_off"][i][:, ksl(h)]))
        return jnp.where(diag_mask, a_diag, jnp.concatenate(blocks, axis=0)).astype(BF16)

    def read_out(j, r0, f, a):
        v = v_ref[j, pl.ds(r0, C), :]
        res = []
        for h in range(GLA_HEADS):
            o_h = _dot(a[h], v[:, vsl(h)]) + _dot(f["q_inter"][:, ksl(h)], s_scr[j, h].astype(BF16))
            res.append((o_h, _dot_tn(f["k_state"][:, ksl(h)], v[:, vsl(h)])))
        return res

    def finish(j, r0, f, res):
        outs = []
        for h, (o_h, s_inc) in enumerate(res):
            dec = jnp.broadcast_to(f["dec"][:, ksl(h)], (GLA_DK, GLA_DK))
            dec_col = jnp.sum(jnp.where(eye, dec, 0.0), axis=-1, keepdims=True)
            s_scr[j, h] = dec_col * s_scr[j, h] + s_inc
            ms = jnp.mean(o_h * o_h, axis=-1, keepdims=True)
            outs.append(o_h * lax.rsqrt(ms + NORM_EPS) * gn_ref[...])
        o = jnp.concatenate(outs, axis=-1) * rb_ref[j, pl.ds(r0, C), :].astype(F32)
        o_ref[j, pl.ds(r0, C), :] = o.astype(BF16)

    def step(c):
        r0 = c * C if isinstance(c, int) else pl.multiple_of(c * C, C)
        bs = [decays(j, r0) for j in range(nb)]
        yield
        fs = [factors(j, r0, b) for j, b in enumerate(bs)]
        yield
        attn = [[intra(f, h) for h in range(GLA_HEADS)] for f in fs]
        yield
        res = [read_out(j, r0, fs[j], attn[j]) for j in range(nb)]
        yield
        for j in range(nb):
            finish(j, r0, fs[j], res[j])

    if length == C:
        yield from step(0)
    else:
        lax.fori_loop(0, length // C, lambda c, carry: (_run(step(c)), carry)[1], 0)

    if whole_sequence:
        sout_ref[...] = s_scr[...]
    else:
        @pl.when(pl.program_id(1) == pl.num_programs(1) - 1)
        def _():
            sout_ref[...] = s_scr[...]


def _gla(qb, kb, vb, la, rb, gn, s0):
    B, L, _ = qb.shape
    chunk = GLA_CHUNK if L % GLA_CHUNK == 0 else L
    sub = GLA_SUB if chunk % GLA_SUB == 0 else chunk
    nb = min(B, SAMPLE_BATCH) if L == chunk else min(B, GLA_LONG_BATCH)
    lt = min(L, GLA_SEQ_TILE)
    assert B % nb == 0 and L % lt == 0 and lt % chunk == 0
    seq = lambda w: pl.BlockSpec((nb, lt, w), lambda b, l: (b, l, 0))
    st = pl.BlockSpec((nb, GLA_HEADS, GLA_DK, GLA_DV), lambda b, l: (b, 0, 0, 0))
    return pl.pallas_call(
        functools.partial(_gla_kernel, chunk=chunk, sub=sub, length=lt),
        grid=(B // nb, L // lt),
        in_specs=[seq(GLA_K_WIDTH), seq(GLA_K_WIDTH), seq(B_WIDTH), seq(GLA_K_WIDTH), seq(B_WIDTH),
                  pl.BlockSpec((1, GLA_DV), lambda b, l: (0, 0)), st],
        out_specs=[seq(B_WIDTH), st],
        out_shape=[jax.ShapeDtypeStruct((B, L, B_WIDTH), BF16),
                   jax.ShapeDtypeStruct((B, GLA_HEADS, GLA_DK, GLA_DV), F32)],
        scratch_shapes=[pltpu.VMEM((nb, GLA_HEADS, GLA_DK, GLA_DV), F32)],
        compiler_params=_cparams(("parallel", "arbitrary")),
        name="gla",
    )(qb, kb, vb, la, rb, gn, s0)


def _short_mixers_kernel(sink_ref, q_ref, kc_ref, vc_ref, kp_ref, vp_ref, qb_ref, kb_ref, vb_ref, la_ref, rb_ref, gn_ref,
                         s0_ref, oa_ref, ob_ref, sout_ref, s_scr, *, length):
    bodies = [_gla_phases(qb_ref, kb_ref, vb_ref, la_ref, rb_ref, gn_ref, s0_ref, ob_ref, sout_ref, s_scr,
                          chunk=length, sub=length, length=length, whole_sequence=True),
              _swa_phases(sink_ref, q_ref, kc_ref, vc_ref, kp_ref, vp_ref, oa_ref, first_block_has_past=True)]
    while bodies:
        for body in list(bodies):
            if next(body, StopIteration) is StopIteration:
                bodies.remove(body)


def _short_mixers(sinks, qa, ka, va, k_past, v_past, qb, kb, vb, la, rb, gn, s0):
    B, L, _ = qa.shape
    assert L < WINDOW and L < GLA_CHUNK
    nb = min(B, SAMPLE_BATCH)
    assert B % nb == 0
    seq = lambda w: pl.BlockSpec((nb, L, w), lambda b, n, s: (b, 0, 0))
    past = pl.BlockSpec((nb, WINDOW, KV_WIDTH), lambda b, n, s: (b, 0, 0))
    st = pl.BlockSpec((nb, GLA_HEADS, GLA_DK, GLA_DV), lambda b, n, s: (b, 0, 0, 0))
    return pl.pallas_call(
        functools.partial(_short_mixers_kernel, length=L),
        grid_spec=pltpu.PrefetchScalarGridSpec(
            num_scalar_prefetch=1,
            grid=(B // nb, 1),
            in_specs=[seq(A_WIDTH), seq(KV_WIDTH), seq(KV_WIDTH), past, past,
                      seq(GLA_K_WIDTH), seq(GLA_K_WIDTH), seq(B_WIDTH), seq(GLA_K_WIDTH), seq(B_WIDTH),
                      pl.BlockSpec((1, GLA_DV), lambda b, n, s: (0, 0)), st],
            out_specs=[seq(A_WIDTH), seq(B_WIDTH), st],
            scratch_shapes=[pltpu.VMEM((nb, GLA_HEADS, GLA_DK, GLA_DV), F32)],
        ),
        out_shape=[jax.ShapeDtypeStruct((B, L, A_WIDTH), BF16), jax.ShapeDtypeStruct((B, L, B_WIDTH), BF16),
                   jax.ShapeDtypeStruct((B, GLA_HEADS, GLA_DK, GLA_DV), F32)],
        compiler_params=_cparams(("parallel", "arbitrary")),
        name="short_mixers",
    )(sinks, qa, ka, va, k_past, v_past, qb, kb, vb, la, rb, gn, s0)


def _outproj_kernel(x_ref, oa_ref, ob_ref, ga_ref, gb_ref, wa_ref, wb_ref, wo_ref, n2_ref, wr_ref, br_ref, upper_ref,
                    lower_ref, h_ref, hn_ref, g_ref, lp_ref, cnt_ref):
    tm = x_ref.shape[0]
    merged = (ga_ref[...].astype(F32) * _dot(oa_ref[...], wa_ref[...])
              + gb_ref[...].astype(F32) * _dot(ob_ref[...], wb_ref[...]))
    h = x_ref[...] + _dot(merged.astype(BF16), wo_ref[...])
    h_ref[...] = h
    ms = jnp.mean(h * h, axis=-1, keepdims=True)
    hn = h * lax.rsqrt(ms + NORM_EPS) * n2_ref[...]
    hn_bf = hn.astype(BF16)
    hn_ref[...] = hn_bf
    logits = _dot(hn_bf, wr_ref[...]) + br_ref[...]
    lt = logits.T[:N_EXPERTS, :]
    eid = lax.broadcasted_iota(I32, (N_EXPERTS, tm), 0)
    vals, hots = [], []
    for k in range(TOP_K):
        m = jnp.max(lt, axis=0, keepdims=True)
        idx = jnp.min(jnp.where(lt == m, eid, N_EXPERTS), axis=0, keepdims=True)
        hot = eid == idx
        lt = jnp.where(hot, -jnp.inf, lt)
        vals.append(m)
        hots.append(hot)
    ex = [jnp.exp(vk - vals[0]) for vk in vals]
    den = ex[0] + ex[1] + ex[2] + ex[3]
    for k in range(TOP_K):
        g_ref[k:k + 1, :] = ex[k] / den
    multi = (hots[0] | hots[1] | hots[2] | hots[3]).astype(BF16)
    before = _dot(multi, upper_ref[...])
    counts = jnp.sum(multi.astype(F32), axis=1, keepdims=True)
    aligned = jnp.floor((counts + (RUN_ALIGN - 1)) * (1.0 / RUN_ALIGN)) * RUN_ALIGN
    run_start = _dot(lower_ref[...], jnp.broadcast_to(aligned, (N_EXPERTS, LANE)).astype(BF16))[:, :1]
    place = before + run_start
    for k in range(TOP_K):
        lp_ref[k:k + 1, :] = jnp.sum(jnp.where(hots[k], place, 0.0), axis=0, keepdims=True).astype(I32)
    cnt_ref[...] = jnp.broadcast_to(aligned, cnt_ref.shape).astype(I32)


def _outproj(x2, oa, ob, ga, gb, wa, wb, wo, n2, wr, br, upper, lower):
    T = x2.shape[0]
    tm = TOK_TILE
    assert T % tm == 0 and upper.shape == (tm, tm)
    tok = lambda w: pl.BlockSpec((tm, w), lambda i: (i, 0))
    const = lambda shape: pl.BlockSpec(shape, lambda i: (0, 0))
    kt = pl.BlockSpec((TOP_K, tm), lambda i: (0, i))
    return pl.pallas_call(
        _outproj_kernel,
        grid=(T // tm,),
        in_specs=[tok(D_MODEL), tok(A_WIDTH), tok(B_WIDTH), tok(D_MODEL), tok(D_MODEL),
                  const((A_WIDTH, D_MODEL)), const((B_WIDTH, D_MODEL)), const((D_MODEL, D_MODEL)),
                  const((1, D_MODEL)), const((D_MODEL, LANE)), const((1, LANE)), const((tm, tm)),
                  const((N_EXPERTS, N_EXPERTS))],
        out_specs=[tok(D_MODEL), tok(D_MODEL), kt, kt, pl.BlockSpec((N_EXPERTS, LANE), lambda i: (i, 0))],
        out_shape=[jax.ShapeDtypeStruct((T, D_MODEL), F32), jax.ShapeDtypeStruct((T, D_MODEL), BF16),
                   jax.ShapeDtypeStruct((TOP_K, T), F32), jax.ShapeDtypeStruct((TOP_K, T), I32),
                   jax.ShapeDtypeStruct((T // tm * N_EXPERTS, LANE), I32)],
        compiler_params=_cparams(("parallel",)),
        name="outproj",
    )(x2, oa, ob, ga, gb, wa, wb, wo, n2, wr, br, upper, lower)


def _pack_rows(v, *, is_bf16_valued=False):
    if not is_bf16_valued:
        v = v.astype(BF16).astype(F32)
    lo = pltpu.bitcast(v[:, :ROW_WORDS], U32) >> 16
    hi = pltpu.bitcast(v[:, ROW_WORDS:], U32) & jnp.uint32(0xFFFF0000)
    return lo | hi


def _unpack_rows(w):
    lo = pltpu.bitcast(w << 16, F32).astype(BF16)
    hi = pltpu.bitcast(w & jnp.uint32(0xFFFF0000), F32).astype(BF16)
    return lo, hi


def _start_runs(runs, tile, tile_rows, global_rows, sem, *, to_global):
    cnt_ref, off_ref, dest_ref, _ = runs

    def run(e):
        j = tile * N_EXPERTS + e
        n = pl.multiple_of(cnt_ref[j], RUN_ALIGN)
        local = tile_rows.at[pl.ds(pl.multiple_of(off_ref[j], RUN_ALIGN), n), :]
        far = global_rows.at[pl.ds(pl.multiple_of(dest_ref[j], RUN_ALIGN), n), :]
        return pltpu.make_async_copy(local, far, sem) if to_global else pltpu.make_async_copy(far, local, sem)

    for e in range(N_EXPERTS):
        pl.when(cnt_ref[tile * N_EXPERTS + e] > 0)(lambda e=e: run(e).start())


def _wait_runs(runs, tile, tile_rows, global_rows, sem):
    total = pl.multiple_of(runs[3][tile], RUN_ALIGN)
    pltpu.make_async_copy(global_rows.at[pl.ds(0, total), :], tile_rows.at[pl.ds(0, total), :], sem).wait()


def _dispatch_kernel(fill_ref, cnt_ref, off_ref, dest_ref, tot_ref, lp_ref, hn_p_ref, hn_s_ref, xs_ref,
                     rows2, sems, zbuf, zsem, *, steps_p):
    i = pl.program_id(0)
    runs = (cnt_ref, off_ref, dest_ref, tot_ref)
    slot = i % 2
    rows = rows2.at[slot]

    @pl.when(i == 0)
    def _():
        zbuf[...] = jnp.zeros_like(zbuf)

        def fill_copy(j):
            return pltpu.make_async_copy(zbuf, xs_ref.at[pl.ds(pl.multiple_of(fill_ref[j], MOE_BM), MOE_BM), :], zsem)

        for j in range(fill_ref.shape[0]):
            pl.when(fill_ref[j] >= 0)(lambda j=j: fill_copy(j).start())
        for j in range(fill_ref.shape[0]):
            pl.when(fill_ref[j] >= 0)(lambda j=j: fill_copy(j).wait())

    def sort_tile(hn_ref):
        hn = hn_ref[...]
        r = lax.broadcasted_iota(I32, (SORT_CHUNK, TOK_TILE), 0).astype(I16)
        for c in range(SORT_ROWS // SORT_CHUNK):
            lp = (lp_ref[...] - c * SORT_CHUNK).astype(I16)
            hit = (r == lp[0:1, :]) | (r == lp[1:2, :]) | (r == lp[2:3, :]) | (r == lp[3:4, :])
            perm = jnp.where(hit, jnp.ones((), BF16), jnp.zeros((), BF16))
            rows[c * SORT_CHUNK:(c + 1) * SORT_CHUNK, :] = _pack_rows(_dot(perm, hn), is_bf16_valued=True)

    pl.when(i < steps_p)(lambda: sort_tile(hn_p_ref))
    pl.when(i >= steps_p)(lambda: sort_tile(hn_s_ref))
    _start_runs(runs, i, rows, xs_ref, sems.at[slot], to_global=True)
    pl.when(i > 0)(lambda: _wait_runs(runs, i - 1, rows2.at[1 - slot], xs_ref, sems.at[1 - slot]))
    pl.when(i == pl.num_programs(0) - 1)(lambda: _wait_runs(runs, i, rows, xs_ref, sems.at[slot]))


def _dispatch(fill, runs, lp, hn_p, hn_s, n_rows):
    tm = TOK_TILE
    steps_p, steps_s = hn_p.shape[0] // tm, hn_s.shape[0] // tm
    return pl.pallas_call(
        functools.partial(_dispatch_kernel, steps_p=steps_p),
        grid_spec=pltpu.PrefetchScalarGridSpec(
            num_scalar_prefetch=5,
            grid=(steps_p + steps_s,),
            in_specs=[pl.BlockSpec((TOP_K, tm), lambda i, *_: (0, i)),
                      pl.BlockSpec((tm, D_MODEL), lambda i, *_: (jnp.minimum(i, steps_p - 1), 0)),
                      pl.BlockSpec((tm, D_MODEL), lambda i, *_: (jnp.maximum(i - steps_p, 0), 0))],
            out_specs=pl.BlockSpec(memory_space=pl.ANY),
            scratch_shapes=[pltpu.VMEM((2, SORT_ROWS, ROW_WORDS), U32), pltpu.SemaphoreType.DMA((2,)),
                            pltpu.VMEM((MOE_BM, ROW_WORDS), U32), pltpu.SemaphoreType.DMA],
        ),
        out_shape=jax.ShapeDtypeStruct((n_rows, ROW_WORDS), U32),
        compiler_params=_cparams(("arbitrary",)),
        name="dispatch",
    )(fill, *runs, lp, hn_p, hn_s)


def _experts_kernel(be_ref, rows_ref, nu_ref, slot_ref, pa_ref, pb_ref, pc_ref, xs_ref, w1a_ref, w1b_ref, w2_ref,
                    b1g_ref, b1l_ref, b2_ref, sel_ref, y_ref, w1g_scr, w1l_scr, w2_scr):
    del nu_ref
    i = pl.program_id(0)
    used = rows_ref[i]
    half_k = D_MODEL // 2

    def arrives(piece_ref):
        return (i == 0) | (piece_ref[i] != piece_ref[jnp.maximum(i - 1, 0)])

    def take_w1_half(w_ref, piece_ref, r0):
        slot = slot_ref[piece_ref[i]]
        for c in range(D_FF // LANE):
            cols = _dot(w_ref[0, :, c * 2 * LANE:(c + 1) * 2 * LANE].astype(BF16), sel_ref[...])
            w1g_scr[slot, r0:r0 + half_k, c * LANE:(c + 1) * LANE] = cols[:, :LANE].astype(BF16)
            w1l_scr[slot, r0:r0 + half_k, c * LANE:(c + 1) * LANE] = cols[:, LANE:].astype(BF16)

    pl.when(arrives(pa_ref))(lambda: take_w1_half(w1a_ref, pa_ref, 0))
    pl.when(arrives(pb_ref))(lambda: take_w1_half(w1b_ref, pb_ref, half_k))

    @pl.when(arrives(pc_ref))
    def _():
        w2_scr[slot_ref[pc_ref[i]]] = w2_ref[0].astype(BF16)

    cur = slot_ref[be_ref[i]]

    def mlp(words):
        x_lo, x_hi = _unpack_rows(words)

        def up(w_scr, b_ref):
            return _dot(x_lo, w_scr[cur, :ROW_WORDS, :]) + _dot(x_hi, w_scr[cur, ROW_WORDS:, :]) + b_ref[0]

        x_glu = jnp.minimum(up(w1g_scr, b1g_ref), SWIGLU_LIMIT)
        x_lin = jnp.clip(up(w1l_scr, b1l_ref), -SWIGLU_LIMIT, SWIGLU_LIMIT)
        act = x_glu * jax.nn.sigmoid(SWIGLU_ALPHA * x_glu) * (x_lin + 1.0)
        return _pack_rows(_dot(act.astype(BF16), w2_scr[cur]) + b2_ref[0])

    for rows in range(EXPERT_ROW_STEP, MOE_BM + 1, EXPERT_ROW_STEP):
        @pl.when((used > rows - EXPERT_ROW_STEP) & (used <= rows))
        def _(rows=rows):
            y_ref[:rows, :] = mlp(xs_ref[:rows, :])
            if rows < MOE_BM:
                y_ref[rows:, :] = jnp.zeros((MOE_BM - rows, ROW_WORDS), U32)

    @pl.when(used == 0)
    def _():
        y_ref[...] = jnp.zeros_like(y_ref)


def _experts(blk_e, blk_rows, n_used, slot_e, pieces, xs, w1, b1g, b1l, w2, b2, sel):
    nblk = xs.shape[0] // MOE_BM
    by_expert = lambda r, c: pl.BlockSpec((1, r, c), lambda i, be, *_: (be[i], 0, 0))
    piece = lambda p, r, c, blk: pl.BlockSpec((1, r, c), lambda i, be, rows, nu, sl, *pc: (pc[p][i], blk, 0))
    return pl.pallas_call(
        _experts_kernel,
        grid_spec=pltpu.PrefetchScalarGridSpec(
            num_scalar_prefetch=4 + len(pieces),
            grid=(nblk,),
            in_specs=[pl.BlockSpec((MOE_BM, ROW_WORDS), lambda i, be, rows, nu, *_: (jnp.minimum(i, nu[0] - 1), 0)),
                      piece(0, D_MODEL // 2, 2 * D_FF, 0), piece(1, D_MODEL // 2, 2 * D_FF, 1),
                      piece(2, D_FF, D_MODEL, 0),
                      by_expert(1, D_FF), by_expert(1, D_FF), by_expert(1, D_MODEL),
                      pl.BlockSpec((2 * LANE, 2 * LANE), lambda i, *_: (0, 0))],
            out_specs=pl.BlockSpec((MOE_BM, ROW_WORDS), lambda i, *_: (i, 0)),
            scratch_shapes=[pltpu.VMEM((2, D_MODEL, D_FF), BF16), pltpu.VMEM((2, D_MODEL, D_FF), BF16),
                            pltpu.VMEM((2, D_FF, D_MODEL), BF16)],
        ),
        out_shape=jax.ShapeDtypeStruct(xs.shape, U32),
        compiler_params=_cparams(("arbitrary",)),
        name="experts",
    )(blk_e, blk_rows, n_used, slot_e, *pieces, xs, w1, w1, w2, b1g, b1l, b2, sel)


def _combine_kernel(cnt_ref, off_ref, dest_ref, tot_ref, lp_ref, g_ref, h_ref, yb_ref, y_ref, rows2, sems, *, tile0):
    i = pl.program_id(0)
    runs = (cnt_ref, off_ref, dest_ref, tot_ref)
    slot = i % 2
    rows = rows2.at[slot]

    def fetch(step, into):
        _start_runs(runs, step + tile0, rows2.at[into], yb_ref, sems.at[into], to_global=False)

    @pl.when(i == 0)
    def _():
        rows2[...] = jnp.zeros_like(rows2)
        fetch(0, 0)

    pl.when(i + 1 < pl.num_programs(0))(lambda: fetch(i + 1, 1 - slot))
    _wait_runs(runs, i + tile0, rows, yb_ref, sems.at[slot])
    lp, g = lp_ref[...].astype(I16), g_ref[...].astype(BF16)
    r = lax.broadcasted_iota(I32, (TOK_TILE, SORT_ROWS), 1).astype(I16)
    mix = jnp.zeros((TOK_TILE, SORT_ROWS), BF16)
    for k in range(TOP_K):
        mix = jnp.where(r == lp[:, k:k + 1], g[:, k:k + 1], mix)
    y_lo, y_hi = _unpack_rows(rows[...])
    y_ref[...] = h_ref[...] + jnp.concatenate([_dot(mix, y_lo), _dot(mix, y_hi)], axis=-1)


def _combine(runs, lp_t, gate_t, h, yb, tile0):
    T = h.shape[0]
    tm = TOK_TILE
    tok = lambda w: pl.BlockSpec((tm, w), lambda i, *_: (i, 0))
    return pl.pallas_call(
        functools.partial(_combine_kernel, tile0=tile0),
        grid_spec=pltpu.PrefetchScalarGridSpec(
            num_scalar_prefetch=4,
            grid=(T // tm,),
            in_specs=[tok(TOP_K), tok(TOP_K), tok(D_MODEL), pl.BlockSpec(memory_space=pl.ANY)],
            out_specs=tok(D_MODEL),
            scratch_shapes=[pltpu.VMEM((2, SORT_ROWS, ROW_WORDS), U32), pltpu.SemaphoreType.DMA((2,))],
        ),
        out_shape=jax.ShapeDtypeStruct((T, D_MODEL), F32),
        compiler_params=_cparams(("arbitrary",)),
        name="combine",
    )(*runs, lp_t, gate_t, h, yb)


def _block_ones(width, block):
    idx = np.arange(width) // block
    return jnp.asarray(idx[:, None] == idx[None, :], BF16)


def _layer(x_p, x_s, cache_k, cache_v, s_gla, norm1, w_in, q_norm, k_norm, sinks, w_gk, b_gk,
           gla_norm, w_a, w_b, w_o, norm2, w_router, b_router, w1, b1, w2, b2):
    Bp, Lp, _ = x_p.shape
    Bs, Ls, _ = x_s.shape
    Tp, Ts = Bp * Lp, Bs * Ls

    w_packed = _pack_w_in(w_in)
    wgk = jnp.pad(w_gk, ((0, GL_PAD - GLA_GATE_RANK), (0, 0))).astype(BF16)
    n1 = norm1.reshape(1, D_MODEL)
    gq, gk = _block_ones(A_WIDTH, SWA_HEAD_DIM), _block_ones(KV_WIDTH, SWA_HEAD_DIM)
    qn = jnp.tile(q_norm, SWA_HEADS).reshape(1, A_WIDTH)
    kn = jnp.tile(k_norm, SWA_KV_HEADS).reshape(1, KV_WIDTH)
    bgk = b_gk.reshape(1, GLA_K_WIDTH)
    gn = gla_norm.reshape(1, GLA_DV)
    wa, wb, wo = w_a.astype(BF16), w_b.astype(BF16), w_o.astype(BF16)
    n2 = norm2.reshape(1, D_MODEL)
    wr = jnp.pad(w_router, ((0, 0), (0, LANE - N_EXPERTS))).astype(BF16)
    br = jnp.pad(b_router, (0, LANE - N_EXPERTS)).reshape(1, LANE)
    b1g = b1[:, 0::2].reshape(N_EXPERTS, 1, D_FF)
    b1l = b1[:, 1::2].reshape(N_EXPERTS, 1, D_FF)
    b2r = b2.reshape(N_EXPERTS, 1, D_MODEL)
    sel_np = np.zeros((2 * LANE, 2 * LANE), np.float32)
    sel_np[2 * np.arange(LANE), np.arange(LANE)] = 1.0
    sel_np[2 * np.arange(LANE) + 1, LANE + np.arange(LANE)] = 1.0
    sel = jnp.asarray(sel_np, BF16)

    def mix(x, k_past, v_past, s0):
        B, L, _ = x.shape
        T = B * L
        x2 = x.reshape(T, D_MODEL)
        r3 = lambda t: t.reshape(B, L, t.shape[-1])
        if k_past is None and L % TOK_TILE == 0:
            oa, ka, va, qb, kb, vb, la, rb, ga, gb = _inproj_swa(sinks, x2, L, n1, w_packed, gq, gk, qn, kn, wgk, bgk)
            ob, s_out = _gla(r3(qb), r3(kb), r3(vb), r3(la), r3(rb), gn, s0)
        else:
            qa, ka, va, qb, kb, vb, la, rb, ga, gb = _inproj(x2, n1, w_packed, gq, gk, qn, kn, wgk, bgk)
            if k_past is not None and L < min(WINDOW, GLA_CHUNK):
                oa, ob, s_out = _short_mixers(sinks, r3(qa), r3(ka), r3(va), k_past, v_past,
                                              r3(qb), r3(kb), r3(vb), r3(la), r3(rb), gn, s0)
            else:
                oa = _swa(sinks, r3(qa), r3(ka), r3(va), k_past, v_past, has_past=k_past is not None)
                ob, s_out = _gla(r3(qb), r3(kb), r3(vb), r3(la), r3(rb), gn, s0)
        h, hn, gate, lp, cnt = _outproj(x2, oa.reshape(T, A_WIDTH), ob.reshape(T, B_WIDTH), ga, gb,
                                        wa, wb, wo, n2, wr, br, upper, lower)
        return r3(ka), r3(va), s_out, h, hn, gate, lp, cnt[:, 0].reshape(T // TOK_TILE, N_EXPERTS)

    upper = jnp.asarray(np.arange(TOK_TILE)[:, None] < np.arange(TOK_TILE)[None, :], BF16)
    lower = jnp.asarray(np.arange(N_EXPERTS)[:, None] > np.arange(N_EXPERTS)[None, :], BF16)
    win = cache_k.shape[1]
    assert win == WINDOW
    ck = cache_k.reshape(Bs, win, KV_WIDTH)
    cv = cache_v.reshape(Bs, win, KV_WIDTH)
    ka_p, va_p, s_p, h_p, hn_p, g_p, lp_p, cnt_p = mix(x_p, None, None, jnp.zeros((Bp, GLA_HEADS, GLA_DK, GLA_DV), F32))
    ka_s, va_s, s_s, h_s, hn_s, g_s, lp_s, cnt_s = mix(x_s, ck, cv, s_gla)

    T = Tp + Ts
    cnt = jnp.concatenate([cnt_p, cnt_s], axis=0)
    most_rows = T * TOP_K + cnt.size * (RUN_ALIGN - 1) + N_EXPERTS * (MOE_BM - 1)
    n_rows = -(-most_rows // MOE_BM) * MOE_BM
    counts = jnp.sum(cnt, axis=0)
    padded = (counts + MOE_BM - 1) // MOE_BM * MOE_BM
    pad_end = jnp.cumsum(padded)
    start = pad_end - padded
    dest = start[None, :] + jnp.cumsum(cnt, axis=0) - cnt
    off = jnp.cumsum(cnt, axis=1) - cnt
    flat = lambda a: a.reshape(-1).astype(I32)
    runs = (flat(cnt), flat(off), flat(dest), flat(jnp.sum(cnt, axis=1)))
    tails = jnp.where(padded > 0, pad_end - MOE_BM, -1)
    most_spare = (n_rows - T * TOP_K) // MOE_BM
    spare = pad_end[-1] + jnp.arange(most_spare, dtype=I32) * MOE_BM
    fill = jnp.concatenate([tails, jnp.where(spare < n_rows, spare, -1)]).astype(I32)
    blk_row = jnp.arange(n_rows // MOE_BM, dtype=I32) * MOE_BM
    blk_e = jnp.minimum(jnp.sum(pad_end[None, :] <= blk_row[:, None], axis=1), N_EXPERTS - 1).astype(I32)
    used_end = start + counts
    blk_rows = jnp.clip(jnp.sum(jnp.where(blk_e[:, None] == jnp.arange(N_EXPERTS)[None, :], used_end[None, :], 0),
                                axis=1) - blk_row, 0, MOE_BM)
    blk_rows = jnp.where(blk_row < pad_end[-1], blk_rows, 0).astype(I32)
    n_used = (pad_end[-1:] // MOE_BM).astype(I32)
    e_ids = jnp.arange(N_EXPERTS, dtype=I32)
    live = padded > 0
    blk_e = jnp.where(blk_row < pad_end[-1], blk_e, jnp.max(jnp.where(live, e_ids, 0)))
    later = jnp.where(live[None, :] & (e_ids[None, :] > e_ids[:, None]), e_ids[None, :], N_EXPERTS)
    nxt = jnp.min(later, axis=1)
    nxt = jnp.where(nxt < N_EXPERTS, nxt, e_ids)
    pick = lambda table: jnp.sum(jnp.where(blk_e[:, None] == e_ids[None, :], table[None, :], 0), axis=1)
    first_blk, next_first = pick(start // MOE_BM), pick(pad_end // MOE_BM)
    blk_i = jnp.arange(n_rows // MOE_BM, dtype=I32)
    pieces = tuple(jnp.where(blk_i >= jnp.maximum(next_first - lead, first_blk + 1), pick(nxt), blk_e).astype(I32)
                   for lead in EXPERT_PIECE_LEAD)
    slot_e = ((jnp.cumsum(live.astype(I32)) - 1) & 1).astype(I32)

    xs = _dispatch(fill, runs, jnp.concatenate([lp_p, lp_s], axis=1), hn_p, hn_s, n_rows)
    yb = _experts(blk_e, blk_rows, n_used, slot_e, pieces, xs, w1, b1g, b1l, w2, b2r, sel)
    y_p = _combine(runs, lp_p.T, g_p.T, h_p, yb, 0).reshape(Bp, Lp, D_MODEL)
    y_s = _combine(runs, lp_s.T, g_s.T, h_s, yb, Tp // TOK_TILE).reshape(Bs, Ls, D_MODEL)

    kv = lambda t: t.reshape(t.shape[0], t.shape[1], SWA_KV_HEADS, SWA_HEAD_DIM)
    k_p, v_p = kv(ka_p[:, -WINDOW:]), kv(va_p[:, -WINDOW:])
    k_s = jnp.concatenate([cache_k, kv(ka_s)], axis=1)[:, -win:]
    v_s = jnp.concatenate([cache_v, kv(va_s)], axis=1)[:, -win:]
    return y_p, y_s, (k_p, v_p, s_p, k_s, v_s, s_s)


def kernel(x_prompt, x_sample, cache_swa_k, cache_swa_v, state_gla, norm1, w_in, q_norm, k_norm, sinks, w_gk, b_gk,
           gla_norm, w_a, w_b, w_o, norm2, w_router, b_router, w1, b1, w2, b2):
    depth = norm1.shape[0]
    y_p, y_s = x_prompt, x_sample
    states = []
    for l in range(depth):
        y_p, y_s, st = _layer(y_p, y_s, cache_swa_k[l], cache_swa_v[l], state_gla[l], norm1[l], w_in[l], q_norm[l],
                              k_norm[l], sinks[l], w_gk[l], b_gk[l], gla_norm[l], w_a[l], w_b[l], w_o[l], norm2[l],
                              w_router[l], b_router[l], w1[l], b1[l], w2[l], b2[l])
        states.append(st)
    return (y_p, y_s) + tuple(jnp.stack([st[j] for st in states]) for j in range(6))
```

```python
import functools

import jax
import jax.numpy as jnp
import numpy as np
from jax import lax
from jax.experimental import pallas as pl
from jax.experimental.pallas import tpu as pltpu

F32 = jnp.float32
BF16 = jnp.bfloat16
I32 = jnp.int32
U32 = jnp.uint32
I16 = jnp.int16

D_MODEL = 1024
SWA_HEADS = 8
SWA_KV_HEADS = 2
SWA_GROUP = SWA_HEADS // SWA_KV_HEADS
SWA_HEAD_DIM = 64
WINDOW = 128
ATTN_SCALE = SWA_HEAD_DIM ** -0.5
GLA_HEADS = 4
GLA_DK = 64
GLA_DV = 128
GLA_GATE_RANK = 16
GLA_GATE_NORM = 16.0
GLA_SCALE = GLA_DK ** -0.5
A_WIDTH = SWA_HEADS * SWA_HEAD_DIM
KV_WIDTH = SWA_KV_HEADS * SWA_HEAD_DIM
GLA_K_WIDTH = GLA_HEADS * GLA_DK
B_WIDTH = GLA_HEADS * GLA_DV
N_EXPERTS = 32
TOP_K = 4
D_FF = D_MODEL
SWIGLU_ALPHA = 1.702
SWIGLU_LIMIT = 7.0
NORM_EPS = 1e-5
QK_EPS = 1e-6
NEG_INF = -1e30

LANE = 128
GL_PAD = LANE
VMEM_LIMIT = 56 * 1024 * 1024

_SEGS = (("qa", A_WIDTH), ("ka", KV_WIDTH), ("va", KV_WIDTH), ("qb", GLA_K_WIDTH), ("kb", GLA_K_WIDTH),
         ("vb", B_WIDTH), ("rb", B_WIDTH), ("ga", D_MODEL), ("gb", D_MODEL), ("gl", GL_PAD))
_OFF = {}
_o = 0
for _n, _w in _SEGS:
    _OFF[_n] = (_o, _w)
    _o += _w
IN_PACKED = _o

TOK_TILE = 512
GLA_CHUNK = 128
GLA_SUB = 16
MOE_BM = 512
EXPERT_ROW_STEP = 128
EXPERT_PIECE_LEAD = (3, 2, 1)
ROW_WORDS = D_MODEL // 2
RUN_ALIGN = 8
SORT_CHUNK = 256
SORT_ROWS = -(-(TOK_TILE * TOP_K + N_EXPERTS * (RUN_ALIGN - 1)) // SORT_CHUNK) * SORT_CHUNK
SAMPLE_BATCH = 8
GLA_LONG_BATCH = 8
GLA_SEQ_TILE = 512


def _cparams(sem):
    return pltpu.CompilerParams(dimension_semantics=sem, vmem_limit_bytes=VMEM_LIMIT)


def _split_bf16(v):
    hi = v.astype(BF16)
    lo = (v - hi.astype(F32)).astype(BF16)
    return hi, lo


def _dot(a, b):
    return jnp.dot(a, b, preferred_element_type=F32)


def _dot_nt(a, b):
    return lax.dot_general(a, b, (((1,), (1,)), ((), ())), preferred_element_type=F32)


def _dot_tn(a, b):
    return lax.dot_general(a, b, (((0,), (0,)), ((), ())), preferred_element_type=F32)


_SRC = {}
_o = 0
for _n, _w in (("qa", A_WIDTH), ("ka", KV_WIDTH), ("va", KV_WIDTH), ("qb", GLA_K_WIDTH), ("kb", GLA_K_WIDTH),
               ("vb", B_WIDTH), ("gl", GLA_GATE_RANK), ("rb", B_WIDTH), ("ga", D_MODEL), ("gb", D_MODEL)):
    _SRC[_n] = (_o, _w)
    _o += _w
IN_WIDTH = _o
PACK_ROWS = 128


def _pack_w_in_kernel(w_ref, o_ref):
    lane = lax.broadcasted_iota(jnp.int32, (PACK_ROWS, LANE), 1)
    for name, width in _SEGS:
        src, have = _SRC[name]
        dst, _ = _OFF[name]
        for c in range(0, width, LANE):
            t = w_ref[src + c:src + c + LANE, :].T
            if have - c < LANE:
                t = jnp.where(lane < have - c, t, 0.0)
            o_ref[:, dst + c:dst + c + LANE] = t.astype(BF16)


def _pack_w_in(w_in):
    return pl.pallas_call(
        _pack_w_in_kernel,
        grid=(D_MODEL // PACK_ROWS,),
        in_specs=[pl.BlockSpec((IN_WIDTH, PACK_ROWS), lambda i: (0, i))],
        out_specs=pl.BlockSpec((PACK_ROWS, IN_PACKED), lambda i: (i, 0)),
        out_shape=jax.ShapeDtypeStruct((D_MODEL, IN_PACKED), BF16),
        compiler_params=_cparams(("parallel",)),
        name="pack_w_in",
    )(w_in.T)


def _projections(x_ref, n1_ref, w_ref, gq_ref, gk_ref, qn_ref, kn_ref, wgk_ref, bgk_ref):
    x = x_ref[...]
    ms = jnp.mean(x * x, axis=-1, keepdims=True)
    xn = (x * lax.rsqrt(ms + NORM_EPS) * n1_ref[...]).astype(BF16)

    def proj(name):
        off, width = _OFF[name]
        return _dot(xn, w_ref[:, off:off + width])

    def head_norm(v, ones_ref, gain_ref):
        hi, lo = _split_bf16(v * v)
        ss = _dot(hi, ones_ref[...]) + _dot(lo, ones_ref[...])
        return v * lax.rsqrt(ss * (1.0 / SWA_HEAD_DIM) + QK_EPS) * gain_ref[...]

    def log_decay(gl):
        z = _dot(gl.astype(BF16), wgk_ref[...]) + bgk_ref[...]
        return (jnp.minimum(z, 0.0) - jnp.log1p(jnp.exp(-jnp.abs(z)))) * (1.0 / GLA_GATE_NORM)

    epilogue = {"qa": lambda u: head_norm(u, gq_ref, qn_ref) * ATTN_SCALE,
                "ka": lambda u: head_norm(u, gk_ref, kn_ref),
                "va": lambda u: u,
                "qb": lambda u: u * GLA_SCALE,
                "kb": lambda u: u,
                "vb": lambda u: u,
                "gl": log_decay,
                "rb": lambda u: u * jax.nn.sigmoid(u),
                "ga": jax.nn.sigmoid,
                "gb": jax.nn.sigmoid}
    return proj, epilogue


def _run_plan(proj, epilogue, plan, before=None, between=None):
    u = proj(plan[0][0])
    for idx, (name, out_ref) in enumerate(plan):
        if before is not None:
            before(idx)
        u_next = proj(plan[idx + 1][0]) if idx + 1 < len(plan) else None
        out_ref[...] = epilogue[name](u).astype(out_ref.dtype)
        if between is not None:
            between(idx)
        u = u_next


def _inproj_kernel(x_ref, n1_ref, w_ref, gq_ref, gk_ref, qn_ref, kn_ref, wgk_ref, bgk_ref,
                   qa_ref, ka_ref, va_ref, qb_ref, kb_ref, vb_ref, la_ref, rb_ref, ga_ref, gb_ref):
    proj, epilogue = _projections(x_ref, n1_ref, w_ref, gq_ref, gk_ref, qn_ref, kn_ref, wgk_ref, bgk_ref)
    _run_plan(proj, epilogue, (("qa", qa_ref), ("ka", ka_ref), ("va", va_ref), ("qb", qb_ref), ("kb", kb_ref),
                               ("vb", vb_ref), ("gl", la_ref), ("rb", rb_ref), ("ga", ga_ref), ("gb", gb_ref)))


def _inproj_swa_kernel(sink_ref, x_ref, n1_ref, w_ref, gq_ref, gk_ref, qn_ref, kn_ref, wgk_ref, bgk_ref,
                       oa_ref, ka_ref, va_ref, qb_ref, kb_ref, vb_ref, la_ref, rb_ref, ga_ref, gb_ref,
                       k_prev, v_prev, *, tiles_per_seq):
    t = pl.program_id(0)
    has_past = (t % tiles_per_seq) != 0

    @pl.when(t == 0)
    def _():
        k_prev[...] = jnp.zeros_like(k_prev)
        v_prev[...] = jnp.zeros_like(v_prev)

    proj, epilogue = _projections(x_ref, n1_ref, w_ref, gq_ref, gk_ref, qn_ref, kn_ref, wgk_ref, bgk_ref)
    tm = x_ref.shape[0]
    q = epilogue["qa"](proj("qa")).astype(BF16)
    k_new = epilogue["ka"](proj("ka"))
    v_new = proj("va")
    ka_ref[...] = k_new
    va_ref[...] = v_new
    k_all = jnp.concatenate([k_prev[...], k_new.astype(BF16)], axis=0)
    v_all = jnp.concatenate([v_prev[...], v_new.astype(BF16)], axis=0)
    k_prev[...] = k_all[tm:, :]
    v_prev[...] = v_all[tm:, :]

    qi = lax.broadcasted_iota(I32, (WINDOW, 2 * WINDOW), 0)
    ci = lax.broadcasted_iota(I32, (WINDOW, 2 * WINDOW), 1)
    dist = WINDOW + qi - ci
    band = (dist >= 0) & (dist < WINDOW)
    band_first = band & ((ci >= WINDOW) | has_past)
    outs = {}

    def keys_of(blk, h):
        return (slice(blk * WINDOW, (blk + 2) * WINDOW),
                slice(h // SWA_GROUP * SWA_HEAD_DIM, (h // SWA_GROUP + 1) * SWA_HEAD_DIM))

    def scores(blk, h):
        return _dot_nt(q[blk * WINDOW:(blk + 1) * WINDOW, h * SWA_HEAD_DIM:(h + 1) * SWA_HEAD_DIM],
                       k_all[keys_of(blk, h)])

    def finish(blk, h, s):
        s = jnp.where(band_first if blk == 0 else band, s, NEG_INF)
        sink = sink_ref[h]
        m = jnp.maximum(jnp.max(s, axis=-1, keepdims=True), sink)
        p = jnp.exp(s - m)
        denom = jnp.sum(p, axis=-1, keepdims=True) + jnp.exp(sink - m)
        outs[blk, h] = _dot(p.astype(BF16), v_all[keys_of(blk, h)]) / denom

    units = [(blk, h) for blk in range(tm // WINDOW) for h in range(SWA_HEADS)]
    plan = (("qb", qb_ref), ("kb", kb_ref), ("vb", vb_ref), ("gl", la_ref), ("rb", rb_ref), ("ga", ga_ref),
            ("gb", gb_ref))

    ahead = [_OFF[name][1] for name, _ in plan[1:]] + [0]
    cuts = [len(units) * sum(ahead[:i]) // sum(ahead) for i in range(len(plan) + 1)]
    pending = {}

    def before(idx):
        for unit in units[cuts[idx]:cuts[idx + 1]]:
            pending[unit] = scores(*unit)

    def between(idx):
        for unit in units[cuts[idx]:cuts[idx + 1]]:
            finish(*unit, pending.pop(unit))

    _run_plan(proj, epilogue, plan, before, between)
    for blk in range(tm // WINDOW):
        oa_ref[blk * WINDOW:(blk + 1) * WINDOW, :] = jnp.concatenate(
            [outs[blk, h] for h in range(SWA_HEADS)], axis=-1).astype(BF16)


def _inproj(x2, n1, w_packed, gq, gk, qn, kn, wgk, bgk):
    T = x2.shape[0]
    tm = min(TOK_TILE, T)
    assert T % tm == 0

    def tok(width):
        return pl.BlockSpec((tm, width), lambda i: (i, 0))

    def const(shape):
        return pl.BlockSpec(shape, lambda i: (0, 0))

    outs = (("qa", A_WIDTH, BF16), ("ka", KV_WIDTH, F32), ("va", KV_WIDTH, F32), ("qb", GLA_K_WIDTH, BF16),
            ("kb", GLA_K_WIDTH, BF16), ("vb", B_WIDTH, BF16), ("la", GLA_K_WIDTH, F32), ("rb", B_WIDTH, BF16),
            ("ga", D_MODEL, BF16), ("gb", D_MODEL, BF16))
    return pl.pallas_call(
        _inproj_kernel,
        grid=(T // tm,),
        in_specs=[tok(D_MODEL), const((1, D_MODEL)), const((D_MODEL, IN_PACKED)), const((A_WIDTH, A_WIDTH)),
                  const((KV_WIDTH, KV_WIDTH)), const((1, A_WIDTH)), const((1, KV_WIDTH)),
                  const((GL_PAD, GLA_K_WIDTH)), const((1, GLA_K_WIDTH))],
        out_specs=[tok(w) for _, w, _ in outs],
        out_shape=[jax.ShapeDtypeStruct((T, w), dt) for _, w, dt in outs],
        compiler_params=_cparams(("parallel",)),
        name="inproj",
    )(x2, n1, w_packed, gq, gk, qn, kn, wgk, bgk)


def _inproj_swa(sinks, x2, seq_len, n1, w_packed, gq, gk, qn, kn, wgk, bgk):
    T = x2.shape[0]
    tm = TOK_TILE
    assert seq_len % tm == 0 and T % seq_len == 0
    tok = lambda width: pl.BlockSpec((tm, width), lambda i, s: (i, 0))
    const = lambda shape: pl.BlockSpec(shape, lambda i, s: (0, 0))
    outs = (("oa", A_WIDTH, BF16), ("ka", KV_WIDTH, F32), ("va", KV_WIDTH, F32), ("qb", GLA_K_WIDTH, BF16),
            ("kb", GLA_K_WIDTH, BF16), ("vb", B_WIDTH, BF16), ("la", GLA_K_WIDTH, F32), ("rb", B_WIDTH, BF16),
            ("ga", D_MODEL, BF16), ("gb", D_MODEL, BF16))
    return pl.pallas_call(
        functools.partial(_inproj_swa_kernel, tiles_per_seq=seq_len // tm),
        grid_spec=pltpu.PrefetchScalarGridSpec(
            num_scalar_prefetch=1,
            grid=(T // tm,),
            in_specs=[tok(D_MODEL), const((1, D_MODEL)), const((D_MODEL, IN_PACKED)), const((A_WIDTH, A_WIDTH)),
                      const((KV_WIDTH, KV_WIDTH)), const((1, A_WIDTH)), const((1, KV_WIDTH)),
                      const((GL_PAD, GLA_K_WIDTH)), const((1, GLA_K_WIDTH))],
            out_specs=[tok(w) for _, w, _ in outs],
            scratch_shapes=[pltpu.VMEM((WINDOW, KV_WIDTH), BF16), pltpu.VMEM((WINDOW, KV_WIDTH), BF16)],
        ),
        out_shape=[jax.ShapeDtypeStruct((T, w), dt) for _, w, dt in outs],
        compiler_params=_cparams(("arbitrary",)),
        name="inproj_swa",
    )(sinks, x2, n1, w_packed, gq, gk, qn, kn, wgk, bgk)


def _run(phases):
    for _ in phases:
        pass


def _swa_kernel(*refs, **static):
    _run(_swa_phases(*refs, **static))


def _swa_phases(sink_ref, q_ref, kc_ref, vc_ref, kp_ref, vp_ref, o_ref, *, first_block_has_past):
    n = pl.program_id(1)
    nb, lq = q_ref.shape[0], q_ref.shape[1]
    assert lq & (lq - 1) == 0
    stack = SWA_GROUP if lq < WINDOW else 1
    rows, keys = stack * lq, WINDOW + lq
    qi = lax.broadcasted_iota(I32, (rows, keys), 0) & (lq - 1)
    ci = lax.broadcasted_iota(I32, (rows, keys), 1)
    dist = WINDOW + qi - ci
    mask = (dist >= 0) & (dist < WINDOW)
    if not first_block_has_past:
        mask = mask & ((ci >= WINDOW) | (n > 0))
    cols = lambda h0: slice(h0 // SWA_GROUP * SWA_HEAD_DIM, (h0 // SWA_GROUP + 1) * SWA_HEAD_DIM)
    qs, ks, vs = [], [], []
    for j in range(nb):
        q = q_ref[j]
        qs.append(q.astype(F32) if stack > 1 else q)
        ks.append(jnp.concatenate([kp_ref[j], kc_ref[j]], axis=0).astype(BF16))
        vs.append(jnp.concatenate([vp_ref[j], vc_ref[j]], axis=0).astype(BF16))
    groups = [(j, h0) for j in range(nb) for h0 in range(0, SWA_HEADS, stack)]
    wave = len(groups) if stack > 1 else 1
    outs = {}
    for w0 in range(0, len(groups), wave):
        scores, sinks = [], []
        for j, h0 in groups[w0:w0 + wave]:
            heads = range(h0, h0 + stack)
            qg = jnp.concatenate([qs[j][:, h * SWA_HEAD_DIM:(h + 1) * SWA_HEAD_DIM] for h in heads], axis=0)
            sinks.append(jnp.concatenate([jnp.full((lq, 1), sink_ref[h], F32) for h in heads], axis=0))
            scores.append(_dot_nt(qg.astype(BF16), ks[j][:, cols(h0)]))
        yield
        probs, denoms = [], []
        for s, sink in zip(scores, sinks):
            s = jnp.where(mask, s, NEG_INF)
            m = jnp.maximum(jnp.max(s, axis=-1, keepdims=True), sink)
            p = jnp.exp(s - m)
            denoms.append(jnp.sum(p, axis=-1, keepdims=True) + jnp.exp(sink - m))
            probs.append(p.astype(BF16))
        yield
        for (j, h0), p, denom in zip(groups[w0:w0 + wave], probs, denoms):
            o = _dot(p, vs[j][:, cols(h0)]) / denom
            outs.setdefault(j, []).extend(o[g * lq:(g + 1) * lq] for g in range(stack))
    for j in range(nb):
        o_ref[j] = jnp.concatenate(outs[j], axis=-1).astype(BF16)


def _swa(sinks, qa, ka, va, k_past, v_past, *, has_past):
    B, L, _ = qa.shape
    lq = min(L, WINDOW)
    nl = L // lq
    if has_past:
        assert nl == 1
        nb = min(B, SAMPLE_BATCH)
        past_spec = pl.BlockSpec((nb, WINDOW, KV_WIDTH), lambda b, n, s: (b, 0, 0))
        kp, vp = k_past, v_past
    else:
        assert lq == WINDOW
        nb = 1
        past_spec = pl.BlockSpec((nb, WINDOW, KV_WIDTH), lambda b, n, s: (b, jnp.maximum(n - 1, 0), 0))
        kp, vp = ka, va
    assert B % nb == 0
    cur = lambda w: pl.BlockSpec((nb, lq, w), lambda b, n, s: (b, n, 0))
    return pl.pallas_call(
        functools.partial(_swa_kernel, first_block_has_past=has_past),
        grid_spec=pltpu.PrefetchScalarGridSpec(
            num_scalar_prefetch=1,
            grid=(B // nb, nl),
            in_specs=[cur(A_WIDTH), cur(KV_WIDTH), cur(KV_WIDTH), past_spec, past_spec],
            out_specs=cur(A_WIDTH),
        ),
        out_shape=jax.ShapeDtypeStruct((B, L, A_WIDTH), BF16),
        compiler_params=_cparams(("parallel", "parallel")),
        name="swa",
    )(sinks, qa, ka, va, kp, vp)


def _gla_kernel(*refs, **static):
    _run(_gla_phases(*refs, **static))


def _gla_phases(q_ref, k_ref, v_ref, la_ref, rb_ref, gn_ref, s0_ref, o_ref, sout_ref, s_scr, *, chunk, sub, length,
                whole_sequence=False):
    C, SUB = chunk, sub
    S = C // SUB
    W = GLA_K_WIDTH
    nb = q_ref.shape[0]

    if whole_sequence:
        s_scr[...] = s0_ref[...]
    else:
        @pl.when(pl.program_id(1) == 0)
        def _():
            s_scr[...] = s0_ref[...]

    row = lax.broadcasted_iota(I32, (C, C), 0)
    col = lax.broadcasted_iota(I32, (C, C), 1)
    tri = (row >= col).astype(BF16)
    diag_mask = (row >= col) & ((row // SUB) == (col // SUB))
    eye = lax.broadcasted_iota(I32, (GLA_DK, GLA_DK), 0) == lax.broadcasted_iota(I32, (GLA_DK, GLA_DK), 1)
    krow = lax.broadcasted_iota(I32, (C, W), 0)

    def rows_of(x, r):
        return jnp.broadcast_to(x[r:r + 1, :], (SUB, W))

    ksl = lambda h: slice(h * GLA_DK, (h + 1) * GLA_DK)
    vsl = lambda h: slice(h * GLA_DV, (h + 1) * GLA_DV)

    def decays(j, r0):
        g_hi, g_lo = _split_bf16(la_ref[j, pl.ds(r0, C), :])
        return _dot(tri, g_hi) + _dot(tri, g_lo)

    def factors(j, r0, b):
        q = q_ref[j, pl.ds(r0, C), :].astype(F32)
        k = k_ref[j, pl.ds(r0, C), :].astype(F32)
        b_last = b[C - 1:C, :]
        mid = jnp.concatenate([rows_of(b, i * SUB + SUB // 2 - 1) for i in range(S)], axis=0)
        beta = jnp.concatenate([jnp.zeros((SUB, W), F32)] + [rows_of(b, i * SUB - 1) for i in range(1, S)], axis=0)
        k_off = [None]
        for i in range(1, S):
            e = jnp.exp(jnp.minimum(b[i * SUB - 1:i * SUB, :] - b, 0.0))
            k_off.append(jnp.where(krow < i * SUB, k * e, 0.0).astype(BF16))
        return dict(q_inter=(q * jnp.exp(b)).astype(BF16), k_state=(k * jnp.exp(b_last - b)).astype(BF16),
                    q_diag=(q * jnp.exp(b - mid)).astype(BF16), k_diag=(k * jnp.exp(mid - b)).astype(BF16),
                    q_off=(q * jnp.exp(b - beta)).astype(BF16), k_off=k_off, dec=jnp.exp(b_last))

    def intra(f, h):
        a_diag = _dot_nt(f["q_diag"][:, ksl(h)], f["k_diag"][:, ksl(h)])
        if S == 1:
            return jnp.where(diag_mask, a_diag, 0.0).astype(BF16)
        blocks = [jnp.zeros((SUB, C), F32)]
        for i in range(1, S):
            blocks.append(_dot_nt(f["q_off"][i * SUB:(i + 1) * SUB, ksl(h)], f["k_off"][i][:, ksl(h)]))
        return jnp.where(diag_mask, a_diag, jnp.concatenate(blocks, axis=0)).astype(BF16)

    def read_out(j, r0, f, a):
        v = v_ref[j, pl.ds(r0, C), :]
        res = []
        for h in range(GLA_HEADS):
            o_h = _dot(a[h], v[:, vsl(h)]) + _dot(f["q_inter"][:, ksl(h)], s_scr[j, h].astype(BF16))
            res.append((o_h, _dot_tn(f["k_state"][:, ksl(h)], v[:, vsl(h)])))
        return res

    def finish(j, r0, f, res):
        outs = []
        for h, (o_h, s_inc) in enumerate(res):
            dec = jnp.broadcast_to(f["dec"][:, ksl(h)], (GLA_DK, GLA_DK))
            dec_col = jnp.sum(jnp.where(eye, dec, 0.0), axis=-1, keepdims=True)
            s_scr[j, h] = dec_col * s_scr[j, h] + s_inc
            ms = jnp.mean(o_h * o_h, axis=-1, keepdims=True)
            outs.append(o_h * lax.rsqrt(ms + NORM_EPS) * gn_ref[...])
        o = jnp.concatenate(outs, axis=-1) * rb_ref[j, pl.ds(r0, C), :].astype(F32)
        o_ref[j, pl.ds(r0, C), :] = o.astype(BF16)

    def step(c):
        r0 = c * C if isinstance(c, int) else pl.multiple_of(c * C, C)
        bs = [decays(j, r0) for j in range(nb)]
        yield
        fs = [factors(j, r0, b) for j, b in enumerate(bs)]
        yield
        attn = [[intra(f, h) for h in range(GLA_HEADS)] for f in fs]
        yield
        res = [read_out(j, r0, fs[j], attn[j]) for j in range(nb)]
        yield
        for j in range(nb):
            finish(j, r0, fs[j], res[j])

    if length == C:
        yield from step(0)
    else:
        lax.fori_loop(0, length // C, lambda c, carry: (_run(step(c)), carry)[1], 0)

    if whole_sequence:
        sout_ref[...] = s_scr[...]
    else:
        @pl.when(pl.program_id(1) == pl.num_programs(1) - 1)
        def _():
            sout_ref[...] = s_scr[...]


def _gla(qb, kb, vb, la, rb, gn, s0):
    B, L, _ = qb.shape
    chunk = GLA_CHUNK if L % GLA_CHUNK == 0 else L
    sub = GLA_SUB if chunk % GLA_SUB == 0 else chunk
    nb = min(B, SAMPLE_BATCH) if L == chunk else min(B, GLA_LONG_BATCH)
    lt = min(L, GLA_SEQ_TILE)
    assert B % nb == 0 and L % lt == 0 and lt % chunk == 0
    seq = lambda w: pl.BlockSpec((nb, lt, w), lambda b, l: (b, l, 0))
    st = pl.BlockSpec((nb, GLA_HEADS, GLA_DK, GLA_DV), lambda b, l: (b, 0, 0, 0))
    return pl.pallas_call(
        functools.partial(_gla_kernel, chunk=chunk, sub=sub, length=lt),
        grid=(B // nb, L // lt),
        in_specs=[seq(GLA_K_WIDTH), seq(GLA_K_WIDTH), seq(B_WIDTH), seq(GLA_K_WIDTH), seq(B_WIDTH),
                  pl.BlockSpec((1, GLA_DV), lambda b, l: (0, 0)), st],
        out_specs=[seq(B_WIDTH), st],
        out_shape=[jax.ShapeDtypeStruct((B, L, B_WIDTH), BF16),
                   jax.ShapeDtypeStruct((B, GLA_HEADS, GLA_DK, GLA_DV), F32)],
        scratch_shapes=[pltpu.VMEM((nb, GLA_HEADS, GLA_DK, GLA_DV), F32)],
        compiler_params=_cparams(("parallel", "arbitrary")),
        name="gla",
    )(qb, kb, vb, la, rb, gn, s0)


def _short_mixers_kernel(sink_ref, q_ref, kc_ref, vc_ref, kp_ref, vp_ref, qb_ref, kb_ref, vb_ref, la_ref, rb_ref, gn_ref,
                         s0_ref, oa_ref, ob_ref, sout_ref, s_scr, *, length):
    bodies = [_gla_phases(qb_ref, kb_ref, vb_ref, la_ref, rb_ref, gn_ref, s0_ref, ob_ref, sout_ref, s_scr,
                          chunk=length, sub=length, length=length, whole_sequence=True),
              _swa_phases(sink_ref, q_ref, kc_ref, vc_ref, kp_ref, vp_ref, oa_ref, first_block_has_past=True)]
    while bodies:
        for body in list(bodies):
            if next(body, StopIteration) is StopIteration:
                bodies.remove(body)


def _short_mixers(sinks, qa, ka, va, k_past, v_past, qb, kb, vb, la, rb, gn, s0):
    B, L, _ = qa.shape
    assert L < WINDOW and L < GLA_CHUNK
    nb = min(B, SAMPLE_BATCH)
    assert B % nb == 0
    seq = lambda w: pl.BlockSpec((nb, L, w), lambda b, n, s: (b, 0, 0))
    past = pl.BlockSpec((nb, WINDOW, KV_WIDTH), lambda b, n, s: (b, 0, 0))
    st = pl.BlockSpec((nb, GLA_HEADS, GLA_DK, GLA_DV), lambda b, n, s: (b, 0, 0, 0))
    return pl.pallas_call(
        functools.partial(_short_mixers_kernel, length=L),
        grid_spec=pltpu.PrefetchScalarGridSpec(
            num_scalar_prefetch=1,
            grid=(B // nb, 1),
            in_specs=[seq(A_WIDTH), seq(KV_WIDTH), seq(KV_WIDTH), past, past,
                      seq(GLA_K_WIDTH), seq(GLA_K_WIDTH), seq(B_WIDTH), seq(GLA_K_WIDTH), seq(B_WIDTH),
                      pl.BlockSpec((1, GLA_DV), lambda b, n, s: (0, 0)), st],
            out_specs=[seq(A_WIDTH), seq(B_WIDTH), st],
            scratch_shapes=[pltpu.VMEM((nb, GLA_HEADS, GLA_DK, GLA_DV), F32)],
        ),
        out_shape=[jax.ShapeDtypeStruct((B, L, A_WIDTH), BF16), jax.ShapeDtypeStruct((B, L, B_WIDTH), BF16),
                   jax.ShapeDtypeStruct((B, GLA_HEADS, GLA_DK, GLA_DV), F32)],
        compiler_params=_cparams(("parallel", "arbitrary")),
        name="short_mixers",
    )(sinks, qa, ka, va, k_past, v_past, qb, kb, vb, la, rb, gn, s0)


def _outproj_kernel(x_ref, oa_ref, ob_ref, ga_ref, gb_ref, wa_ref, wb_ref, wo_ref, n2_ref, wr_ref, br_ref, upper_ref,
                    lower_ref, h_ref, hn_ref, g_ref, lp_ref, cnt_ref):
    tm = x_ref.shape[0]
    merged = (ga_ref[...].astype(F32) * _dot(oa_ref[...], wa_ref[...])
              + gb_ref[...].astype(F32) * _dot(ob_ref[...], wb_ref[...]))
    h = x_ref[...] + _dot(merged.astype(BF16), wo_ref[...])
    h_ref[...] = h
    ms = jnp.mean(h * h, axis=-1, keepdims=True)
    hn = h * lax.rsqrt(ms + NORM_EPS) * n2_ref[...]
    hn_bf = hn.astype(BF16)
    hn_ref[...] = hn_bf
    logits = _dot(hn_bf, wr_ref[...]) + br_ref[...]
    lt = logits.T[:N_EXPERTS, :]
    eid = lax.broadcasted_iota(I32, (N_EXPERTS, tm), 0)
    vals, hots = [], []
    for k in range(TOP_K):
        m = jnp.max(lt, axis=0, keepdims=True)
        idx = jnp.min(jnp.where(lt == m, eid, N_EXPERTS), axis=0, keepdims=True)
        hot = eid == idx
        lt = jnp.where(hot, -jnp.inf, lt)
        vals.append(m)
        hots.append(hot)
    ex = [jnp.exp(vk - vals[0]) for vk in vals]
    den = ex[0] + ex[1] + ex[2] + ex[3]
    for k in range(TOP_K):
        g_ref[k:k + 1, :] = ex[k] / den
    multi = (hots[0] | hots[1] | hots[2] | hots[3]).astype(BF16)
    before = _dot(multi, upper_ref[...])
    counts = jnp.sum(multi.astype(F32), axis=1, keepdims=True)
    aligned = jnp.floor((counts + (RUN_ALIGN - 1)) * (1.0 / RUN_ALIGN)) * RUN_ALIGN
    run_start = _dot(lower_ref[...], jnp.broadcast_to(aligned, (N_EXPERTS, LANE)).astype(BF16))[:, :1]
    place = before + run_start
    for k in range(TOP_K):
        lp_ref[k:k + 1, :] = jnp.sum(jnp.where(hots[k], place, 0.0), axis=0, keepdims=True).astype(I32)
    cnt_ref[...] = jnp.broadcast_to(aligned, cnt_ref.shape).astype(I32)


def _outproj(x2, oa, ob, ga, gb, wa, wb, wo, n2, wr, br, upper, lower):
    T = x2.shape[0]
    tm = TOK_TILE
    assert T % tm == 0 and upper.shape == (tm, tm)
    tok = lambda w: pl.BlockSpec((tm, w), lambda i: (i, 0))
    const = lambda shape: pl.BlockSpec(shape, lambda i: (0, 0))
    kt = pl.BlockSpec((TOP_K, tm), lambda i: (0, i))
    return pl.pallas_call(
        _outproj_kernel,
        grid=(T // tm,),
        in_specs=[tok(D_MODEL), tok(A_WIDTH), tok(B_WIDTH), tok(D_MODEL), tok(D_MODEL),
                  const((A_WIDTH, D_MODEL)), const((B_WIDTH, D_MODEL)), const((D_MODEL, D_MODEL)),
                  const((1, D_MODEL)), const((D_MODEL, LANE)), const((1, LANE)), const((tm, tm)),
                  const((N_EXPERTS, N_EXPERTS))],
        out_specs=[tok(D_MODEL), tok(D_MODEL), kt, kt, pl.BlockSpec((N_EXPERTS, LANE), lambda i: (i, 0))],
        out_shape=[jax.ShapeDtypeStruct((T, D_MODEL), F32), jax.ShapeDtypeStruct((T, D_MODEL), BF16),
                   jax.ShapeDtypeStruct((TOP_K, T), F32), jax.ShapeDtypeStruct((TOP_K, T), I32),
                   jax.ShapeDtypeStruct((T // tm * N_EXPERTS, LANE), I32)],
        compiler_params=_cparams(("parallel",)),
        name="outproj",
    )(x2, oa, ob, ga, gb, wa, wb, wo, n2, wr, br, upper, lower)


def _pack_rows(v, *, is_bf16_valued=False):
    if not is_bf16_valued:
        v = v.astype(BF16).astype(F32)
    lo = pltpu.bitcast(v[:, :ROW_WORDS], U32) >> 16
    hi = pltpu.bitcast(v[:, ROW_WORDS:], U32) & jnp.uint32(0xFFFF0000)
    return lo | hi


def _unpack_rows(w):
    lo = pltpu.bitcast(w << 16, F32).astype(BF16)
    hi = pltpu.bitcast(w & jnp.uint32(0xFFFF0000), F32).astype(BF16)
    return lo, hi


def _start_runs(runs, tile, tile_rows, global_rows, sem, *, to_global):
    cnt_ref, off_ref, dest_ref, _ = runs

    def run(e):
        j = tile * N_EXPERTS + e
        n = pl.multiple_of(cnt_ref[j], RUN_ALIGN)
        local = tile_rows.at[pl.ds(pl.multiple_of(off_ref[j], RUN_ALIGN), n), :]
        far = global_rows.at[pl.ds(pl.multiple_of(dest_ref[j], RUN_ALIGN), n), :]
        return pltpu.make_async_copy(local, far, sem) if to_global else pltpu.make_async_copy(far, local, sem)

    for e in range(N_EXPERTS):
        pl.when(cnt_ref[tile * N_EXPERTS + e] > 0)(lambda e=e: run(e).start())


def _wait_runs(runs, tile, tile_rows, global_rows, sem):
    total = pl.multiple_of(runs[3][tile], RUN_ALIGN)
    pltpu.make_async_copy(global_rows.at[pl.ds(0, total), :], tile_rows.at[pl.ds(0, total), :], sem).wait()


def _dispatch_kernel(fill_ref, cnt_ref, off_ref, dest_ref, tot_ref, lp_ref, hn_p_ref, hn_s_ref, xs_ref,
                     rows2, sems, zbuf, zsem, *, steps_p):
    i = pl.program_id(0)
    runs = (cnt_ref, off_ref, dest_ref, tot_ref)
    slot = i % 2
    rows = rows2.at[slot]

    @pl.when(i == 0)
    def _():
        zbuf[...] = jnp.zeros_like(zbuf)

        def fill_copy(j):
            return pltpu.make_async_copy(zbuf, xs_ref.at[pl.ds(pl.multiple_of(fill_ref[j], MOE_BM), MOE_BM), :], zsem)

        for j in range(fill_ref.shape[0]):
            pl.when(fill_ref[j] >= 0)(lambda j=j: fill_copy(j).start())
        for j in range(fill_ref.shape[0]):
            pl.when(fill_ref[j] >= 0)(lambda j=j: fill_copy(j).wait())

    def sort_tile(hn_ref):
        hn = hn_ref[...]
        r = lax.broadcasted_iota(I32, (SORT_CHUNK, TOK_TILE), 0).astype(I16)
        for c in range(SORT_ROWS // SORT_CHUNK):
            lp = (lp_ref[...] - c * SORT_CHUNK).astype(I16)
            hit = (r == lp[0:1, :]) | (r == lp[1:2, :]) | (r == lp[2:3, :]) | (r == lp[3:4, :])
            perm = jnp.where(hit, jnp.ones((), BF16), jnp.zeros((), BF16))
            rows[c * SORT_CHUNK:(c + 1) * SORT_CHUNK, :] = _pack_rows(_dot(perm, hn), is_bf16_valued=True)

    pl.when(i < steps_p)(lambda: sort_tile(hn_p_ref))
    pl.when(i >= steps_p)(lambda: sort_tile(hn_s_ref))
    _start_runs(runs, i, rows, xs_ref, sems.at[slot], to_global=True)
    pl.when(i > 0)(lambda: _wait_runs(runs, i - 1, rows2.at[1 - slot], xs_ref, sems.at[1 - slot]))
    pl.when(i == pl.num_programs(0) - 1)(lambda: _wait_runs(runs, i, rows, xs_ref, sems.at[slot]))


def _dispatch(fill, runs, lp, hn_p, hn_s, n_rows):
    tm = TOK_TILE
    steps_p, steps_s = hn_p.shape[0] // tm, hn_s.shape[0] // tm
    return pl.pallas_call(
        functools.partial(_dispatch_kernel, steps_p=steps_p),
        grid_spec=pltpu.PrefetchScalarGridSpec(
            num_scalar_prefetch=5,
            grid=(steps_p + steps_s,),
            in_specs=[pl.BlockSpec((TOP_K, tm), lambda i, *_: (0, i)),
                      pl.BlockSpec((tm, D_MODEL), lambda i, *_: (jnp.minimum(i, steps_p - 1), 0)),
                      pl.BlockSpec((tm, D_MODEL), lambda i, *_: (jnp.maximum(i - steps_p, 0), 0))],
            out_specs=pl.BlockSpec(memory_space=pl.ANY),
            scratch_shapes=[pltpu.VMEM((2, SORT_ROWS, ROW_WORDS), U32), pltpu.SemaphoreType.DMA((2,)),
                            pltpu.VMEM((MOE_BM, ROW_WORDS), U32), pltpu.SemaphoreType.DMA],
        ),
        out_shape=jax.ShapeDtypeStruct((n_rows, ROW_WORDS), U32),
        compiler_params=_cparams(("arbitrary",)),
        name="dispatch",
    )(fill, *runs, lp, hn_p, hn_s)


def _experts_kernel(be_ref, rows_ref, nu_ref, slot_ref, pa_ref, pb_ref, pc_ref, xs_ref, w1a_ref, w1b_ref, w2_ref,
                    b1g_ref, b1l_ref, b2_ref, sel_ref, y_ref, w1g_scr, w1l_scr, w2_scr):
    del nu_ref
    i = pl.program_id(0)
    used = rows_ref[i]
    half_k = D_MODEL // 2

    def arrives(piece_ref):
        return (i == 0) | (piece_ref[i] != piece_ref[jnp.maximum(i - 1, 0)])

    def take_w1_half(w_ref, piece_ref, r0):
        slot = slot_ref[piece_ref[i]]
        for c in range(D_FF // LANE):
            cols = _dot(w_ref[0, :, c * 2 * LANE:(c + 1) * 2 * LANE].astype(BF16), sel_ref[...])
            w1g_scr[slot, r0:r0 + half_k, c * LANE:(c + 1) * LANE] = cols[:, :LANE].astype(BF16)
            w1l_scr[slot, r0:r0 + half_k, c * LANE:(c + 1) * LANE] = cols[:, LANE:].astype(BF16)

    pl.when(arrives(pa_ref))(lambda: take_w1_half(w1a_ref, pa_ref, 0))
    pl.when(arrives(pb_ref))(lambda: take_w1_half(w1b_ref, pb_ref, half_k))

    @pl.when(arrives(pc_ref))
    def _():
        w2_scr[slot_ref[pc_ref[i]]] = w2_ref[0].astype(BF16)

    cur = slot_ref[be_ref[i]]

    def mlp(words):
        x_lo, x_hi = _unpack_rows(words)

        def up(w_scr, b_ref):
            return _dot(x_lo, w_scr[cur, :ROW_WORDS, :]) + _dot(x_hi, w_scr[cur, ROW_WORDS:, :]) + b_ref[0]

        x_glu = jnp.minimum(up(w1g_scr, b1g_ref), SWIGLU_LIMIT)
        x_lin = jnp.clip(up(w1l_scr, b1l_ref), -SWIGLU_LIMIT, SWIGLU_LIMIT)
        act = x_glu * jax.nn.sigmoid(SWIGLU_ALPHA * x_glu) * (x_lin + 1.0)
        return _pack_rows(_dot(act.astype(BF16), w2_scr[cur]) + b2_ref[0])

    for rows in range(EXPERT_ROW_STEP, MOE_BM + 1, EXPERT_ROW_STEP):
        @pl.when((used > rows - EXPERT_ROW_STEP) & (used <= rows))
        def _(rows=rows):
            y_ref[:rows, :] = mlp(xs_ref[:rows, :])
            if rows < MOE_BM:
                y_ref[rows:, :] = jnp.zeros((MOE_BM - rows, ROW_WORDS), U32)

    @pl.when(used == 0)
    def _():
        y_ref[...] = jnp.zeros_like(y_ref)


def _experts(blk_e, blk_rows, n_used, slot_e, pieces, xs, w1, b1g, b1l, w2, b2, sel):
    nblk = xs.shape[0] // MOE_BM
    by_expert = lambda r, c: pl.BlockSpec((1, r, c), lambda i, be, *_: (be[i], 0, 0))
    piece = lambda p, r, c, blk: pl.BlockSpec((1, r, c), lambda i, be, rows, nu, sl, *pc: (pc[p][i], blk, 0))
    return pl.pallas_call(
        _experts_kernel,
        grid_spec=pltpu.PrefetchScalarGridSpec(
            num_scalar_prefetch=4 + len(pieces),
            grid=(nblk,),
            in_specs=[pl.BlockSpec((MOE_BM, ROW_WORDS), lambda i, be, rows, nu, *_: (jnp.minimum(i, nu[0] - 1), 0)),
                      piece(0, D_MODEL // 2, 2 * D_FF, 0), piece(1, D_MODEL // 2, 2 * D_FF, 1),
                      piece(2, D_FF, D_MODEL, 0),
                      by_expert(1, D_FF), by_expert(1, D_FF), by_expert(1, D_MODEL),
                      pl.BlockSpec((2 * LANE, 2 * LANE), lambda i, *_: (0, 0))],
            out_specs=pl.BlockSpec((MOE_BM, ROW_WORDS), lambda i, *_: (i, 0)),
            scratch_shapes=[pltpu.VMEM((2, D_MODEL, D_FF), BF16), pltpu.VMEM((2, D_MODEL, D_FF), BF16),
                            pltpu.VMEM((2, D_FF, D_MODEL), BF16)],
        ),
        out_shape=jax.ShapeDtypeStruct(xs.shape, U32),
        compiler_params=_cparams(("arbitrary",)),
        name="experts",
    )(blk_e, blk_rows, n_used, slot_e, *pieces, xs, w1, w1, w2, b1g, b1l, b2, sel)


def _combine_kernel(cnt_ref, off_ref, dest_ref, tot_ref, lp_ref, g_ref, h_ref, yb_ref, y_ref, rows2, sems, *, tile0):
    i = pl.program_id(0)
    runs = (cnt_ref, off_ref, dest_ref, tot_ref)
    slot = i % 2
    rows = rows2.at[slot]

    def fetch(step, into):
        _start_runs(runs, step + tile0, rows2.at[into], yb_ref, sems.at[into], to_global=False)

    @pl.when(i == 0)
    def _():
        rows2[...] = jnp.zeros_like(rows2)
        fetch(0, 0)

    pl.when(i + 1 < pl.num_programs(0))(lambda: fetch(i + 1, 1 - slot))
    _wait_runs(runs, i + tile0, rows, yb_ref, sems.at[slot])
    lp, g = lp_ref[...].astype(I16), g_ref[...].astype(BF16)
    r = lax.broadcasted_iota(I32, (TOK_TILE, SORT_ROWS), 1).astype(I16)
    mix = jnp.zeros((TOK_TILE, SORT_ROWS), BF16)
    for k in range(TOP_K):
        mix = jnp.where(r == lp[:, k:k + 1], g[:, k:k + 1], mix)
    y_lo, y_hi = _unpack_rows(rows[...])
    y_ref[...] = h_ref[...] + jnp.concatenate([_dot(mix, y_lo), _dot(mix, y_hi)], axis=-1)


def _combine(runs, lp_t, gate_t, h, yb, tile0):
    T = h.shape[0]
    tm = TOK_TILE
    tok = lambda w: pl.BlockSpec((tm, w), lambda i, *_: (i, 0))
    return pl.pallas_call(
        functools.partial(_combine_kernel, tile0=tile0),
        grid_spec=pltpu.PrefetchScalarGridSpec(
            num_scalar_prefetch=4,
            grid=(T // tm,),
            in_specs=[tok(TOP_K), tok(TOP_K), tok(D_MODEL), pl.BlockSpec(memory_space=pl.ANY)],
            out_specs=tok(D_MODEL),
            scratch_shapes=[pltpu.VMEM((2, SORT_ROWS, ROW_WORDS), U32), pltpu.SemaphoreType.DMA((2,))],
        ),
        out_shape=jax.ShapeDtypeStruct((T, D_MODEL), F32),
        compiler_params=_cparams(("arbitrary",)),
        name="combine",
    )(*runs, lp_t, gate_t, h, yb)


def _block_ones(width, block):
    idx = np.arange(width) // block
    return jnp.asarray(idx[:, None] == idx[None, :], BF16)


def _layer(x_p, x_s, cache_k, cache_v, s_gla, norm1, w_in, q_norm, k_norm, sinks, w_gk, b_gk,
           gla_norm, w_a, w_b, w_o, norm2, w_router, b_router, w1, b1, w2, b2):
    Bp, Lp, _ = x_p.shape
    Bs, Ls, _ = x_s.shape
    Tp, Ts = Bp * Lp, Bs * Ls

    w_packed = _pack_w_in(w_in)
    wgk = jnp.pad(w_gk, ((0, GL_PAD - GLA_GATE_RANK), (0, 0))).astype(BF16)
    n1 = norm1.reshape(1, D_MODEL)
    gq, gk = _block_ones(A_WIDTH, SWA_HEAD_DIM), _block_ones(KV_WIDTH, SWA_HEAD_DIM)
    qn = jnp.tile(q_norm, SWA_HEADS).reshape(1, A_WIDTH)
    kn = jnp.tile(k_norm, SWA_KV_HEADS).reshape(1, KV_WIDTH)
    bgk = b_gk.reshape(1, GLA_K_WIDTH)
    gn = gla_norm.reshape(1, GLA_DV)
    wa, wb, wo = w_a.astype(BF16), w_b.astype(BF16), w_o.astype(BF16)
    n2 = norm2.reshape(1, D_MODEL)
    wr = jnp.pad(w_router, ((0, 0), (0, LANE - N_EXPERTS))).astype(BF16)
    br = jnp.pad(b_router, (0, LANE - N_EXPERTS)).reshape(1, LANE)
    b1g = b1[:, 0::2].reshape(N_EXPERTS, 1, D_FF)
    b1l = b1[:, 1::2].reshape(N_EXPERTS, 1, D_FF)
    b2r = b2.reshape(N_EXPERTS, 1, D_MODEL)
    sel_np = np.zeros((2 * LANE, 2 * LANE), np.float32)
    sel_np[2 * np.arange(LANE), np.arange(LANE)] = 1.0
    sel_np[2 * np.arange(LANE) + 1, LANE + np.arange(LANE)] = 1.0
    sel = jnp.asarray(sel_np, BF16)

    def mix(x, k_past, v_past, s0):
        B, L, _ = x.shape
        T = B * L
        x2 = x.reshape(T, D_MODEL)
        r3 = lambda t: t.reshape(B, L, t.shape[-1])
        if k_past is None and L % TOK_TILE == 0:
            oa, ka, va, qb, kb, vb, la, rb, ga, gb = _inproj_swa(sinks, x2, L, n1, w_packed, gq, gk, qn, kn, wgk, bgk)
            ob, s_out = _gla(r3(qb), r3(kb), r3(vb), r3(la), r3(rb), gn, s0)
        else:
            qa, ka, va, qb, kb, vb, la, rb, ga, gb = _inproj(x2, n1, w_packed, gq, gk, qn, kn, wgk, bgk)
            if k_past is not None and L < min(WINDOW, GLA_CHUNK):
                oa, ob, s_out = _short_mixers(sinks, r3(qa), r3(ka), r3(va), k_past, v_past,
                                              r3(qb), r3(kb), r3(vb), r3(la), r3(rb), gn, s0)
            else:
                oa = _swa(sinks, r3(qa), r3(ka), r3(va), k_past, v_past, has_past=k_past is not None)
                ob, s_out = _gla(r3(qb), r3(kb), r3(vb), r3(la), r3(rb), gn, s0)
        h, hn, gate, lp, cnt = _outproj(x2, oa.reshape(T, A_WIDTH), ob.reshape(T, B_WIDTH), ga, gb,
                                        wa, wb, wo, n2, wr, br, upper, lower)
        return r3(ka), r3(va), s_out, h, hn, gate, lp, cnt[:, 0].reshape(T // TOK_TILE, N_EXPERTS)

    upper = jnp.asarray(np.arange(TOK_TILE)[:, None] < np.arange(TOK_TILE)[None, :], BF16)
    lower = jnp.asarray(np.arange(N_EXPERTS)[:, None] > np.arange(N_EXPERTS)[None, :], BF16)
    win = cache_k.shape[1]
    assert win == WINDOW
    ck = cache_k.reshape(Bs, win, KV_WIDTH)
    cv = cache_v.reshape(Bs, win, KV_WIDTH)
    ka_p, va_p, s_p, h_p, hn_p, g_p, lp_p, cnt_p = mix(x_p, None, None, jnp.zeros((Bp, GLA_HEADS, GLA_DK, GLA_DV), F32))
    ka_s, va_s, s_s, h_s, hn_s, g_s, lp_s, cnt_s = mix(x_s, ck, cv, s_gla)

    T = Tp + Ts
    cnt = jnp.concatenate([cnt_p, cnt_s], axis=0)
    most_rows = T * TOP_K + cnt.size * (RUN_ALIGN - 1) + N_EXPERTS * (MOE_BM - 1)
    n_rows = -(-most_rows // MOE_BM) * MOE_BM
    counts = jnp.sum(cnt, axis=0)
    padded = (counts + MOE_BM - 1) // MOE_BM * MOE_BM
    pad_end = jnp.cumsum(padded)
    start = pad_end - padded
    dest = start[None, :] + jnp.cumsum(cnt, axis=0) - cnt
    off = jnp.cumsum(cnt, axis=1) - cnt
    flat = lambda a: a.reshape(-1).astype(I32)
    runs = (flat(cnt), flat(off), flat(dest), flat(jnp.sum(cnt, axis=1)))
    tails = jnp.where(padded > 0, pad_end - MOE_BM, -1)
    most_spare = (n_rows - T * TOP_K) // MOE_BM
    spare = pad_end[-1] + jnp.arange(most_spare, dtype=I32) * MOE_BM
    fill = jnp.concatenate([tails, jnp.where(spare < n_rows, spare, -1)]).astype(I32)
    blk_row = jnp.arange(n_rows // MOE_BM, dtype=I32) * MOE_BM
    blk_e = jnp.minimum(jnp.sum(pad_end[None, :] <= blk_row[:, None], axis=1), N_EXPERTS - 1).astype(I32)
    used_end = start + counts
    blk_rows = jnp.clip(jnp.sum(jnp.where(blk_e[:, None] == jnp.arange(N_EXPERTS)[None, :], used_end[None, :], 0),
                                axis=1) - blk_row, 0, MOE_BM)
    blk_rows = jnp.where(blk_row < pad_end[-1], blk_rows, 0).astype(I32)
    n_used = (pad_end[-1:] // MOE_BM).astype(I32)
    e_ids = jnp.arange(N_EXPERTS, dtype=I32)
    live = padded > 0
    blk_e = jnp.where(blk_row < pad_end[-1], blk_e, jnp.max(jnp.where(live, e_ids, 0)))
    later = jnp.where(live[None, :] & (e_ids[None, :] > e_ids[:, None]), e_ids[None, :], N_EXPERTS)
    nxt = jnp.min(later, axis=1)
    nxt = jnp.where(nxt < N_EXPERTS, nxt, e_ids)
    pick = lambda table: jnp.sum(jnp.where(blk_e[:, None] == e_ids[None, :], table[None, :], 0), axis=1)
    first_blk, next_first = pick(start // MOE_BM), pick(pad_end // MOE_BM)
    blk_i = jnp.arange(n_rows // MOE_BM, dtype=I32)
    pieces = tuple(jnp.where(blk_i >= jnp.maximum(next_first - lead, first_blk + 1), pick(nxt), blk_e).astype(I32)
                   for lead in EXPERT_PIECE_LEAD)
    slot_e = ((jnp.cumsum(live.astype(I32)) - 1) & 1).astype(I32)

    xs = _dispatch(fill, runs, jnp.concatenate([lp_p, lp_s], axis=1), hn_p, hn_s, n_rows)
    yb = _experts(blk_e, blk_rows, n_used, slot_e, pieces, xs, w1, b1g, b1l, w2, b2r, sel)
    y_p = _combine(runs, lp_p.T, g_p.T, h_p, yb, 0).reshape(Bp, Lp, D_MODEL)
    y_s = _combine(runs, lp_s.T, g_s.T, h_s, yb, Tp // TOK_TILE).reshape(Bs, Ls, D_MODEL)

    kv = lambda t: t.reshape(t.shape[0], t.shape[1], SWA_KV_HEADS, SWA_HEAD_DIM)
    k_p, v_p = kv(ka_p[:, -WINDOW:]), kv(va_p[:, -WINDOW:])
    k_s = jnp.concatenate([cache_k, kv(ka_s)], axis=1)[:, -win:]
    v_s = jnp.concatenate([cache_v, kv(va_s)], axis=1)[:, -win:]
    return y_p, y_s, (k_p, v_p, s_p, k_s, v_s, s_s)


def kernel(x_prompt, x_sample, cache_swa_k, cache_swa_v, state_gla, norm1, w_in, q_norm, k_norm, sinks, w_gk, b_gk,
           gla_norm, w_a, w_b, w_o, norm2, w_router, b_router, w1, b1, w2, b2):
    depth = norm1.shape[0]
    y_p, y_s = x_prompt, x_sample
    states = []
    for l in range(depth):
        y_p, y_s, st = _layer(y_p, y_s, cache_swa_k[l], cache_swa_v[l], state_gla[l], norm1[l], w_in[l], q_norm[l],
                              k_norm[l], sinks[l], w_gk[l], b_gk[l], gla_norm[l], w_a[l], w_b[l], w_o[l], norm2[l],
                              w_router[l], b_router[l], w1[l], b1[l], w2[l], b2[l])
        states.append(st)
    return (y_p, y_s) + tuple(jnp.stack([st[j] for st in states]) for j in range(6))
```

```python
import functools

import jax
import jax.numpy as jnp
import numpy as np
from jax import lax
from jax.experimental import pallas as pl
from jax.experimental.pallas import tpu as pltpu

F32 = jnp.float32
BF16 = jnp.bfloat16
I32 = jnp.int32
U32 = jnp.uint32
I16 = jnp.int16

D_MODEL = 1024
SWA_HEADS = 8
SWA_KV_HEADS = 2
SWA_GROUP = SWA_HEADS // SWA_KV_HEADS
SWA_HEAD_DIM = 64
WINDOW = 128
ATTN_SCALE = SWA_HEAD_DIM ** -0.5
GLA_HEADS = 4
GLA_DK = 64
GLA_DV = 128
GLA_GATE_RANK = 16
GLA_GATE_NORM = 16.0
GLA_SCALE = GLA_DK ** -0.5
A_WIDTH = SWA_HEADS * SWA_HEAD_DIM
KV_WIDTH = SWA_KV_HEADS * SWA_HEAD_DIM
GLA_K_WIDTH = GLA_HEADS * GLA_DK
B_WIDTH = GLA_HEADS * GLA_DV
N_EXPERTS = 32
TOP_K = 4
D_FF = D_MODEL
SWIGLU_ALPHA = 1.702
SWIGLU_LIMIT = 7.0
NORM_EPS = 1e-5
QK_EPS = 1e-6
NEG_INF = -1e30

LANE = 128
GL_PAD = LANE
VMEM_LIMIT = 56 * 1024 * 1024

_SEGS = (("qa", A_WIDTH), ("ka", KV_WIDTH), ("va", KV_WIDTH), ("qb", GLA_K_WIDTH), ("kb", GLA_K_WIDTH),
         ("vb", B_WIDTH), ("rb", B_WIDTH), ("ga", D_MODEL), ("gb", D_MODEL), ("gl", GL_PAD))
_OFF = {}
_o = 0
for _n, _w in _SEGS:
    _OFF[_n] = (_o, _w)
    _o += _w
IN_PACKED = _o

TOK_TILE = 512
GLA_CHUNK = 128
GLA_SUB = 16
MOE_BM = 512
EXPERT_ROW_STEP = 128
EXPERT_PIECE_LEAD = (3, 2, 1)
ROW_WORDS = D_MODEL // 2
RUN_ALIGN = 8
SORT_CHUNK = 256
SORT_ROWS = -(-(TOK_TILE * TOP_K + N_EXPERTS * (RUN_ALIGN - 1)) // SORT_CHUNK) * SORT_CHUNK
SAMPLE_BATCH = 8
GLA_LONG_BATCH = 8
GLA_SEQ_TILE = 512


def _cparams(sem):
    return pltpu.CompilerParams(dimension_semantics=sem, vmem_limit_bytes=VMEM_LIMIT)


def _split_bf16(v):
    hi = v.astype(BF16)
    lo = (v - hi.astype(F32)).astype(BF16)
    return hi, lo


def _dot(a, b):
    return jnp.dot(a, b, preferred_element_type=F32)


def _dot_nt(a, b):
    return lax.dot_general(a, b, (((1,), (1,)), ((), ())), preferred_element_type=F32)


def _dot_tn(a, b):
    return lax.dot_general(a, b, (((0,), (0,)), ((), ())), preferred_element_type=F32)


_SRC = {}
_o = 0
for _n, _w in (("qa", A_WIDTH), ("ka", KV_WIDTH), ("va", KV_WIDTH), ("qb", GLA_K_WIDTH), ("kb", GLA_K_WIDTH),
               ("vb", B_WIDTH), ("gl", GLA_GATE_RANK), ("rb", B_WIDTH), ("ga", D_MODEL), ("gb", D_MODEL)):
    _SRC[_n] = (_o, _w)
    _o += _w
IN_WIDTH = _o
PACK_ROWS = 128


def _pack_w_in_kernel(w_ref, o_ref):
    lane = lax.broadcasted_iota(jnp.int32, (PACK_ROWS, LANE), 1)
    for name, width in _SEGS:
        src, have = _SRC[name]
        dst, _ = _OFF[name]
        for c in range(0, width, LANE):
            t = w_ref[src + c:src + c + LANE, :].T
            if have - c < LANE:
                t = jnp.where(lane < have - c, t, 0.0)
            o_ref[:, dst + c:dst + c + LANE] = t.astype(BF16)


def _pack_w_in(w_in):
    return pl.pallas_call(
        _pack_w_in_kernel,
        grid=(D_MODEL // PACK_ROWS,),
        in_specs=[pl.BlockSpec((IN_WIDTH, PACK_ROWS), lambda i: (0, i))],
        out_specs=pl.BlockSpec((PACK_ROWS, IN_PACKED), lambda i: (i, 0)),
        out_shape=jax.ShapeDtypeStruct((D_MODEL, IN_PACKED), BF16),
        compiler_params=_cparams(("parallel",)),
        name="pack_w_in",
    )(w_in.T)


def _projections(x_ref, n1_ref, w_ref, gq_ref, gk_ref, qn_ref, kn_ref, wgk_ref, bgk_ref):
    x = x_ref[...]
    ms = jnp.mean(x * x, axis=-1, keepdims=True)
    xn = (x * lax.rsqrt(ms + NORM_EPS) * n1_ref[...]).astype(BF16)

    def proj(name):
        off, width = _OFF[name]
        return _dot(xn, w_ref[:, off:off + width])

    def head_norm(v, ones_ref, gain_ref):
        hi, lo = _split_bf16(v * v)
        ss = _dot(hi, ones_ref[...]) + _dot(lo, ones_ref[...])
        return v * lax.rsqrt(ss * (1.0 / SWA_HEAD_DIM) + QK_EPS) * gain_ref[...]

    def log_decay(gl):
        z = _dot(gl.astype(BF16), wgk_ref[...]) + bgk_ref[...]
        return (jnp.minimum(z, 0.0) - jnp.log1p(jnp.exp(-jnp.abs(z)))) * (1.0 / GLA_GATE_NORM)

    epilogue = {"qa": lambda u: head_norm(u, gq_ref, qn_ref) * ATTN_SCALE,
                "ka": lambda u: head_norm(u, gk_ref, kn_ref),
                "va": lambda u: u,
                "qb": lambda u: u * GLA_SCALE,
                "kb": lambda u: u,
                "vb": lambda u: u,
                "gl": log_decay,
                "rb": lambda u: u * jax.nn.sigmoid(u),
                "ga": jax.nn.sigmoid,
                "gb": jax.nn.sigmoid}
    return proj, epilogue


def _run_plan(proj, epilogue, plan, before=None, between=None):
    u = proj(plan[0][0])
    for idx, (name, out_ref) in enumerate(plan):
        if before is not None:
            before(idx)
        u_next = proj(plan[idx + 1][0]) if idx + 1 < len(plan) else None
        out_ref[...] = epilogue[name](u).astype(out_ref.dtype)
        if between is not None:
            between(idx)
        u = u_next


def _inproj_kernel(x_ref, n1_ref, w_ref, gq_ref, gk_ref, qn_ref, kn_ref, wgk_ref, bgk_ref,
                   qa_ref, ka_ref, va_ref, qb_ref, kb_ref, vb_ref, la_ref, rb_ref, ga_ref, gb_ref):
    proj, epilogue = _projections(x_ref, n1_ref, w_ref, gq_ref, gk_ref, qn_ref, kn_ref, wgk_ref, bgk_ref)
    _run_plan(proj, epilogue, (("qa", qa_ref), ("ka", ka_ref), ("va", va_ref), ("qb", qb_ref), ("kb", kb_ref),
                               ("vb", vb_ref), ("gl", la_ref), ("rb", rb_ref), ("ga", ga_ref), ("gb", gb_ref)))


def _inproj_swa_kernel(sink_ref, x_ref, n1_ref, w_ref, gq_ref, gk_ref, qn_ref, kn_ref, wgk_ref, bgk_ref,
                       oa_ref, ka_ref, va_ref, qb_ref, kb_ref, vb_ref, la_ref, rb_ref, ga_ref, gb_ref,
                       k_prev, v_prev, *, tiles_per_seq):
    t = pl.program_id(0)
    has_past = (t % tiles_per_seq) != 0

    @pl.when(t == 0)
    def _():
        k_prev[...] = jnp.zeros_like(k_prev)
        v_prev[...] = jnp.zeros_like(v_prev)

    proj, epilogue = _projections(x_ref, n1_ref, w_ref, gq_ref, gk_ref, qn_ref, kn_ref, wgk_ref, bgk_ref)
    tm = x_ref.shape[0]
    q = epilogue["qa"](proj("qa")).astype(BF16)
    k_new = epilogue["ka"](proj("ka"))
    v_new = proj("va")
    ka_ref[...] = k_new
    va_ref[...] = v_new
    k_all = jnp.concatenate([k_prev[...], k_new.astype(BF16)], axis=0)
    v_all = jnp.concatenate([v_prev[...], v_new.astype(BF16)], axis=0)
    k_prev[...] = k_all[tm:, :]
    v_prev[...] = v_all[tm:, :]

    qi = lax.broadcasted_iota(I32, (WINDOW, 2 * WINDOW), 0)
    ci = lax.broadcasted_iota(I32, (WINDOW, 2 * WINDOW), 1)
    dist = WINDOW + qi - ci
    band = (dist >= 0) & (dist < WINDOW)
    band_first = band & ((ci >= WINDOW) | has_past)
    outs = {}

    def keys_of(blk, h):
        return (slice(blk * WINDOW, (blk + 2) * WINDOW),
                slice(h // SWA_GROUP * SWA_HEAD_DIM, (h // SWA_GROUP + 1) * SWA_HEAD_DIM))

    def scores(blk, h):
        return _dot_nt(q[blk * WINDOW:(blk + 1) * WINDOW, h * SWA_HEAD_DIM:(h + 1) * SWA_HEAD_DIM],
                       k_all[keys_of(blk, h)])

    def finish(blk, h, s):
        s = jnp.where(band_first if blk == 0 else band, s, NEG_INF)
        sink = sink_ref[h]
        m = jnp.maximum(jnp.max(s, axis=-1, keepdims=True), sink)
        p = jnp.exp(s - m)
        denom = jnp.sum(p, axis=-1, keepdims=True) + jnp.exp(sink - m)
        outs[blk, h] = _dot(p.astype(BF16), v_all[keys_of(blk, h)]) / denom

    units = [(blk, h) for blk in range(tm // WINDOW) for h in range(SWA_HEADS)]
    plan = (("qb", qb_ref), ("kb", kb_ref), ("vb", vb_ref), ("gl", la_ref), ("rb", rb_ref), ("ga", ga_ref),
            ("gb", gb_ref))

    ahead = [_OFF[name][1] for name, _ in plan[1:]] + [0]
    cuts = [len(units) * sum(ahead[:i]) // sum(ahead) for i in range(len(plan) + 1)]
    pending = {}

    def before(idx):
        for unit in units[cuts[idx]:cuts[idx + 1]]:
            pending[unit] = scores(*unit)

    def between(idx):
        for unit in units[cuts[idx]:cuts[idx + 1]]:
            finish(*unit, pending.pop(unit))

    _run_plan(proj, epilogue, plan, before, between)
    for blk in range(tm // WINDOW):
        oa_ref[blk * WINDOW:(blk + 1) * WINDOW, :] = jnp.concatenate(
            [outs[blk, h] for h in range(SWA_HEADS)], axis=-1).astype(BF16)


def _inproj(x2, n1, w_packed, gq, gk, qn, kn, wgk, bgk):
    T = x2.shape[0]
    tm = min(TOK_TILE, T)
    assert T % tm == 0

    def tok(width):
        return pl.BlockSpec((tm, width), lambda i: (i, 0))

    def const(shape):
        return pl.BlockSpec(shape, lambda i: (0, 0))

    outs = (("qa", A_WIDTH, BF16), ("ka", KV_WIDTH, F32), ("va", KV_WIDTH, F32), ("qb", GLA_K_WIDTH, BF16),
            ("kb", GLA_K_WIDTH, BF16), ("vb", B_WIDTH, BF16), ("la", GLA_K_WIDTH, F32), ("rb", B_WIDTH, BF16),
            ("ga", D_MODEL, BF16), ("gb", D_MODEL, BF16))
    return pl.pallas_call(
        _inproj_kernel,
        grid=(T // tm,),
        in_specs=[tok(D_MODEL), const((1, D_MODEL)), const((D_MODEL, IN_PACKED)), const((A_WIDTH, A_WIDTH)),
                  const((KV_WIDTH, KV_WIDTH)), const((1, A_WIDTH)), const((1, KV_WIDTH)),
                  const((GL_PAD, GLA_K_WIDTH)), const((1, GLA_K_WIDTH))],
        out_specs=[tok(w) for _, w, _ in outs],
        out_shape=[jax.ShapeDtypeStruct((T, w), dt) for _, w, dt in outs],
        compiler_params=_cparams(("parallel",)),
        name="inproj",
    )(x2, n1, w_packed, gq, gk, qn, kn, wgk, bgk)


def _inproj_swa(sinks, x2, seq_len, n1, w_packed, gq, gk, qn, kn, wgk, bgk):
    T = x2.shape[0]
    tm = TOK_TILE
    assert seq_len % tm == 0 and T % seq_len == 0
    tok = lambda width: pl.BlockSpec((tm, width), lambda i, s: (i, 0))
    const = lambda shape: pl.BlockSpec(shape, lambda i, s: (0, 0))
    outs = (("oa", A_WIDTH, BF16), ("ka", KV_WIDTH, F32), ("va", KV_WIDTH, F32), ("qb", GLA_K_WIDTH, BF16),
            ("kb", GLA_K_WIDTH, BF16), ("vb", B_WIDTH, BF16), ("la", GLA_K_WIDTH, F32), ("rb", B_WIDTH, BF16),
            ("ga", D_MODEL, BF16), ("gb", D_MODEL, BF16))
    return pl.pallas_call(
        functools.partial(_inproj_swa_kernel, tiles_per_seq=seq_len // tm),
        grid_spec=pltpu.PrefetchScalarGridSpec(
            num_scalar_prefetch=1,
            grid=(T // tm,),
            in_specs=[tok(D_MODEL), const((1, D_MODEL)), const((D_MODEL, IN_PACKED)), const((A_WIDTH, A_WIDTH)),
                      const((KV_WIDTH, KV_WIDTH)), const((1, A_WIDTH)), const((1, KV_WIDTH)),
                      const((GL_PAD, GLA_K_WIDTH)), const((1, GLA_K_WIDTH))],
            out_specs=[tok(w) for _, w, _ in outs],
            scratch_shapes=[pltpu.VMEM((WINDOW, KV_WIDTH), BF16), pltpu.VMEM((WINDOW, KV_WIDTH), BF16)],
        ),
        out_shape=[jax.ShapeDtypeStruct((T, w), dt) for _, w, dt in outs],
        compiler_params=_cparams(("arbitrary",)),
        name="inproj_swa",
    )(sinks, x2, n1, w_packed, gq, gk, qn, kn, wgk, bgk)


def _run(phases):
    for _ in phases:
        pass


def _swa_kernel(*refs, **static):
    _run(_swa_phases(*refs, **static))


def _swa_phases(sink_ref, q_ref, kc_ref, vc_ref, kp_ref, vp_ref, o_ref, *, first_block_has_past,
                past_transposed=False):
    n = pl.program_id(1)
    nb, lq = q_ref.shape[0], q_ref.shape[1]
    assert lq & (lq - 1) == 0
    stack = SWA_GROUP if lq < WINDOW else 1
    rows, keys = stack * lq, WINDOW + lq
    qi = lax.broadcasted_iota(I32, (rows, keys), 0) & (lq - 1)
    ci = lax.broadcasted_iota(I32, (rows, keys), 1)
    dist = WINDOW + qi - ci
    mask = (dist >= 0) & (dist < WINDOW)
    if not first_block_has_past:
        mask = mask & ((ci >= WINDOW) | (n > 0))
    cols = lambda h0: slice(h0 // SWA_GROUP * SWA_HEAD_DIM, (h0 // SWA_GROUP + 1) * SWA_HEAD_DIM)
    qs, ks, vs = [], [], []
    for j in range(nb):
        q = q_ref[j]
        qs.append(q.astype(F32) if stack > 1 else q)
        if past_transposed:
            ks.append((kp_ref[j].astype(BF16), kc_ref[j].astype(BF16)))
            vp = vp_ref[j].T
        else:
            ks.append(jnp.concatenate([kp_ref[j], kc_ref[j]], axis=0).astype(BF16))
            vp = vp_ref[j]
        vs.append(jnp.concatenate([vp, vc_ref[j]], axis=0).astype(BF16))
    groups = [(j, h0) for j in range(nb) for h0 in range(0, SWA_HEADS, stack)]
    wave = len(groups) if stack > 1 else 1
    outs = {}
    for w0 in range(0, len(groups), wave):
        scores, sinks = [], []
        for j, h0 in groups[w0:w0 + wave]:
            heads = range(h0, h0 + stack)
            qg = jnp.concatenate([qs[j][:, h * SWA_HEAD_DIM:(h + 1) * SWA_HEAD_DIM] for h in heads], axis=0)
            sinks.append(jnp.concatenate([jnp.full((lq, 1), sink_ref[h], F32) for h in heads], axis=0))
            qg = qg.astype(BF16)
            if past_transposed:
                scores.append(jnp.concatenate([_dot(qg, ks[j][0][cols(h0), :]), _dot_nt(qg, ks[j][1][:, cols(h0)])],
                                              axis=1))
            else:
                scores.append(_dot_nt(qg, ks[j][:, cols(h0)]))
        yield
        probs, denoms = [], []
        for s, sink in zip(scores, sinks):
            s = jnp.where(mask, s, NEG_INF)
            m = jnp.maximum(jnp.max(s, axis=-1, keepdims=True), sink)
            p = jnp.exp(s - m)
            denoms.append(jnp.sum(p, axis=-1, keepdims=True) + jnp.exp(sink - m))
            probs.append(p.astype(BF16))
        yield
        for (j, h0), p, denom in zip(groups[w0:w0 + wave], probs, denoms):
            o = _dot(p, vs[j][:, cols(h0)]) / denom
            outs.setdefault(j, []).extend(o[g * lq:(g + 1) * lq] for g in range(stack))
    for j in range(nb):
        o_ref[j] = jnp.concatenate(outs[j], axis=-1).astype(BF16)


def _swa(sinks, qa, ka, va, k_past, v_past, *, has_past):
    B, L, _ = qa.shape
    lq = min(L, WINDOW)
    nl = L // lq
    if has_past:
        assert nl == 1
        nb = min(B, SAMPLE_BATCH)
        past_spec = pl.BlockSpec((nb, WINDOW, KV_WIDTH), lambda b, n, s: (b, 0, 0))
        kp, vp = k_past, v_past
    else:
        assert lq == WINDOW
        nb = 1
        past_spec = pl.BlockSpec((nb, WINDOW, KV_WIDTH), lambda b, n, s: (b, jnp.maximum(n - 1, 0), 0))
        kp, vp = ka, va
    assert B % nb == 0
    cur = lambda w: pl.BlockSpec((nb, lq, w), lambda b, n, s: (b, n, 0))
    return pl.pallas_call(
        functools.partial(_swa_kernel, first_block_has_past=has_past),
        grid_spec=pltpu.PrefetchScalarGridSpec(
            num_scalar_prefetch=1,
            grid=(B // nb, nl),
            in_specs=[cur(A_WIDTH), cur(KV_WIDTH), cur(KV_WIDTH), past_spec, past_spec],
            out_specs=cur(A_WIDTH),
        ),
        out_shape=jax.ShapeDtypeStruct((B, L, A_WIDTH), BF16),
        compiler_params=_cparams(("parallel", "parallel")),
        name="swa",
    )(sinks, qa, ka, va, kp, vp)


def _gla_kernel(*refs, **static):
    _run(_gla_phases(*refs, **static))


def _gla_phases(q_ref, k_ref, v_ref, la_ref, rb_ref, gn_ref, s0_ref, o_ref, sout_ref, s_scr, *, chunk, sub, length,
                whole_sequence=False):
    C, SUB = chunk, sub
    S = C // SUB
    W = GLA_K_WIDTH
    nb = q_ref.shape[0]

    if whole_sequence:
        s_scr[...] = s0_ref[...]
    else:
        @pl.when(pl.program_id(1) == 0)
        def _():
            s_scr[...] = s0_ref[...]

    row = lax.broadcasted_iota(I32, (C, C), 0)
    col = lax.broadcasted_iota(I32, (C, C), 1)
    tri = (row >= col).astype(BF16)
    diag_mask = (row >= col) & ((row // SUB) == (col // SUB))
    eye = lax.broadcasted_iota(I32, (GLA_DK, GLA_DK), 0) == lax.broadcasted_iota(I32, (GLA_DK, GLA_DK), 1)
    krow = lax.broadcasted_iota(I32, (C, W), 0)

    def rows_of(x, r):
        return jnp.broadcast_to(x[r:r + 1, :], (SUB, W))

    ksl = lambda h: slice(h * GLA_DK, (h + 1) * GLA_DK)
    vsl = lambda h: slice(h * GLA_DV, (h + 1) * GLA_DV)

    def decays(j, r0):
        g_hi, g_lo = _split_bf16(la_ref[j, pl.ds(r0, C), :])
        return _dot(tri, g_hi) + _dot(tri, g_lo)

    def factors(j, r0, b):
        q = q_ref[j, pl.ds(r0, C), :].astype(F32)
        k = k_ref[j, pl.ds(r0, C), :].astype(F32)
        b_last = b[C - 1:C, :]
        mid = jnp.concatenate([rows_of(b, i * SUB + SUB // 2 - 1) for i in range(S)], axis=0)
        beta = jnp.concatenate([jnp.zeros((SUB, W), F32)] + [rows_of(b, i * SUB - 1) for i in range(1, S)], axis=0)
        k_off = [None]
        for i in range(1, S):
            e = jnp.exp(jnp.minimum(b[i * SUB - 1:i * SUB, :] - b, 0.0))
            k_off.append(jnp.where(krow < i * SUB, k * e, 0.0).astype(BF16))
        return dict(q_inter=(q * jnp.exp(b)).astype(BF16), k_state=(k * jnp.exp(b_last - b)).astype(BF16),
                    q_diag=(q * jnp.exp(b - mid)).astype(BF16), k_diag=(k * jnp.exp(mid - b)).astype(BF16),
                    q_off=(q * jnp.exp(b - beta)).astype(BF16), k_off=k_off, dec=jnp.exp(b_last))

    def intra(f, h):
        a_diag = _dot_nt(f["q_diag"][:, ksl(h)], f["k_diag"][:, ksl(h)])
        if S == 1:
            return jnp.where(diag_mask, a_diag, 0.0).astype(BF16)
        blocks = [jnp.zeros((SUB, C), F32)]
        for i in range(1, S):
            blocks.append(_dot_nt(f["q_off"][i * SUB:(i + 1) * SUB, ksl(h)], f["k_off"][i][:, ksl(h)]))
        return jnp.where(diag_mask, a_diag, jnp.concatenate(blocks, axis=0)).astype(BF16)

    def read_out(j, r0, f, a):
        v = v_ref[j, pl.ds(r0, C), :]
        res = []
        for h in range(GLA_HEADS):
            o_h = _dot(a[h], v[:, vsl(h)]) + _dot(f["q_inter"][:, ksl(h)], s_scr[j, h].astype(BF16))
            res.append((o_h, _dot_tn(f["k_state"][:, ksl(h)], v[:, vsl(h)])))
        return res

    def finish(j, r0, f, res):
        outs = []
        for h, (o_h, s_inc) in enumerate(res):
            dec = jnp.broadcast_to(f["dec"][:, ksl(h)], (GLA_DK, GLA_DK))
            dec_col = jnp.sum(jnp.where(eye, dec, 0.0), axis=-1, keepdims=True)
            s_scr[j, h] = dec_col * s_scr[j, h] + s_inc
            ms = jnp.mean(o_h * o_h, axis=-1, keepdims=True)
            outs.append(o_h * lax.rsqrt(ms + NORM_EPS) * gn_ref[...])
        o = jnp.concatenate(outs, axis=-1) * rb_ref[j, pl.ds(r0, C), :].astype(F32)
        o_ref[j, pl.ds(r0, C), :] = o.astype(BF16)

    def step(c):
        r0 = c * C if isinstance(c, int) else pl.multiple_of(c * C, C)
        bs = [decays(j, r0) for j in range(nb)]
        yield
        fs = [factors(j, r0, b) for j, b in enumerate(bs)]
        yield
        attn = [[intra(f, h) for h in range(GLA_HEADS)] for f in fs]
        yield
        res = [read_out(j, r0, fs[j], attn[j]) for j in range(nb)]
        yield
        for j in range(nb):
            finish(j, r0, fs[j], res[j])

    if length == C:
        yield from step(0)
    else:
        lax.fori_loop(0, length // C, lambda c, carry: (_run(step(c)), carry)[1], 0)

    if whole_sequence:
        sout_ref[...] = s_scr[...]
    else:
        @pl.when(pl.program_id(1) == pl.num_programs(1) - 1)
        def _():
            sout_ref[...] = s_scr[...]


def _gla(qb, kb, vb, la, rb, gn, s0):
    B, L, _ = qb.shape
    chunk = GLA_CHUNK if L % GLA_CHUNK == 0 else L
    sub = GLA_SUB if chunk % GLA_SUB == 0 else chunk
    nb = min(B, SAMPLE_BATCH) if L == chunk else min(B, GLA_LONG_BATCH)
    lt = min(L, GLA_SEQ_TILE)
    assert B % nb == 0 and L % lt == 0 and lt % chunk == 0
    seq = lambda w: pl.BlockSpec((nb, lt, w), lambda b, l: (b, l, 0))
    st = pl.BlockSpec((nb, GLA_HEADS, GLA_DK, GLA_DV), lambda b, l: (b, 0, 0, 0))
    return pl.pallas_call(
        functools.partial(_gla_kernel, chunk=chunk, sub=sub, length=lt),
        grid=(B // nb, L // lt),
        in_specs=[seq(GLA_K_WIDTH), seq(GLA_K_WIDTH), seq(B_WIDTH), seq(GLA_K_WIDTH), seq(B_WIDTH),
                  pl.BlockSpec((1, GLA_DV), lambda b, l: (0, 0)), st],
        out_specs=[seq(B_WIDTH), st],
        out_shape=[jax.ShapeDtypeStruct((B, L, B_WIDTH), BF16),
                   jax.ShapeDtypeStruct((B, GLA_HEADS, GLA_DK, GLA_DV), F32)],
        scratch_shapes=[pltpu.VMEM((nb, GLA_HEADS, GLA_DK, GLA_DV), F32)],
        compiler_params=_cparams(("parallel", "arbitrary")),
        name="gla",
    )(qb, kb, vb, la, rb, gn, s0)


def _short_mixers_kernel(sink_ref, q_ref, kc_ref, vc_ref, kp_ref, vp_ref, qb_ref, kb_ref, vb_ref, la_ref, rb_ref, gn_ref,
                         s0_ref, oa_ref, ob_ref, sout_ref, s_scr, *, length):
    bodies = [_gla_phases(qb_ref, kb_ref, vb_ref, la_ref, rb_ref, gn_ref, s0_ref, ob_ref, sout_ref, s_scr,
                          chunk=length, sub=length, length=length, whole_sequence=True),
              _swa_phases(sink_ref, q_ref, kc_ref, vc_ref, kp_ref, vp_ref, oa_ref, first_block_has_past=True,
                          past_transposed=True)]
    while bodies:
        for body in list(bodies):
            if next(body, StopIteration) is StopIteration:
                bodies.remove(body)


def _short_mixers(sinks, qa, ka, va, k_past, v_past, qb, kb, vb, la, rb, gn, s0):
    B, L, _ = qa.shape
    assert L < WINDOW and L < GLA_CHUNK
    nb = min(B, SAMPLE_BATCH)
    assert B % nb == 0
    seq = lambda w: pl.BlockSpec((nb, L, w), lambda b, n, s: (b, 0, 0))
    past = pl.BlockSpec((nb, KV_WIDTH, WINDOW), lambda b, n, s: (b, 0, 0))
    st = pl.BlockSpec((nb, GLA_HEADS, GLA_DK, GLA_DV), lambda b, n, s: (b, 0, 0, 0))
    return pl.pallas_call(
        functools.partial(_short_mixers_kernel, length=L),
        grid_spec=pltpu.PrefetchScalarGridSpec(
            num_scalar_prefetch=1,
            grid=(B // nb, 1),
            in_specs=[seq(A_WIDTH), seq(KV_WIDTH), seq(KV_WIDTH), past, past,
                      seq(GLA_K_WIDTH), seq(GLA_K_WIDTH), seq(B_WIDTH), seq(GLA_K_WIDTH), seq(B_WIDTH),
                      pl.BlockSpec((1, GLA_DV), lambda b, n, s: (0, 0)), st],
            out_specs=[seq(A_WIDTH), seq(B_WIDTH), st],
            scratch_shapes=[pltpu.VMEM((nb, GLA_HEADS, GLA_DK, GLA_DV), F32)],
        ),
        out_shape=[jax.ShapeDtypeStruct((B, L, A_WIDTH), BF16), jax.ShapeDtypeStruct((B, L, B_WIDTH), BF16),
                   jax.ShapeDtypeStruct((B, GLA_HEADS, GLA_DK, GLA_DV), F32)],
        compiler_params=_cparams(("parallel", "arbitrary")),
        name="short_mixers",
    )(sinks, qa, ka, va, k_past, v_past, qb, kb, vb, la, rb, gn, s0)


def _outproj_kernel(x_ref, oa_ref, ob_ref, ga_ref, gb_ref, wa_ref, wb_ref, wo_ref, n2_ref, wr_ref, br_ref, upper_ref,
                    lower_ref, h_ref, hn_ref, g_ref, lp_ref, cnt_ref):
    tm = x_ref.shape[0]
    merged = (ga_ref[...].astype(F32) * _dot(oa_ref[...], wa_ref[...])
              + gb_ref[...].astype(F32) * _dot(ob_ref[...], wb_ref[...]))
    h = x_ref[...] + _dot(merged.astype(BF16), wo_ref[...])
    h_ref[...] = h
    ms = jnp.mean(h * h, axis=-1, keepdims=True)
    hn = h * lax.rsqrt(ms + NORM_EPS) * n2_ref[...]
    hn_bf = hn.astype(BF16)
    hn_ref[...] = hn_bf
    logits = _dot(hn_bf, wr_ref[...]) + br_ref[...]
    lt = logits.T[:N_EXPERTS, :]
    eid = lax.broadcasted_iota(I32, (N_EXPERTS, tm), 0)
    vals, hots = [], []
    for k in range(TOP_K):
        m = jnp.max(lt, axis=0, keepdims=True)
        idx = jnp.min(jnp.where(lt == m, eid, N_EXPERTS), axis=0, keepdims=True)
        hot = eid == idx
        lt = jnp.where(hot, -jnp.inf, lt)
        vals.append(m)
        hots.append(hot)
    ex = [jnp.exp(vk - vals[0]) for vk in vals]
    den = ex[0] + ex[1] + ex[2] + ex[3]
    for k in range(TOP_K):
        g_ref[k:k + 1, :] = ex[k] / den
    multi = (hots[0] | hots[1] | hots[2] | hots[3]).astype(BF16)
    before = _dot(multi, upper_ref[...])
    counts = jnp.sum(multi.astype(F32), axis=1, keepdims=True)
    aligned = jnp.floor((counts + (RUN_ALIGN - 1)) * (1.0 / RUN_ALIGN)) * RUN_ALIGN
    run_start = _dot(lower_ref[...], jnp.broadcast_to(aligned, (N_EXPERTS, LANE)).astype(BF16))[:, :1]
    place = before + run_start
    for k in range(TOP_K):
        lp_ref[k:k + 1, :] = jnp.sum(jnp.where(hots[k], place, 0.0), axis=0, keepdims=True).astype(I32)
    cnt_ref[...] = jnp.broadcast_to(aligned, cnt_ref.shape).astype(I32)


def _outproj(x2, oa, ob, ga, gb, wa, wb, wo, n2, wr, br, upper, lower):
    T = x2.shape[0]
    tm = TOK_TILE
    assert T % tm == 0 and upper.shape == (tm, tm)
    tok = lambda w: pl.BlockSpec((tm, w), lambda i: (i, 0))
    const = lambda shape: pl.BlockSpec(shape, lambda i: (0, 0))
    kt = pl.BlockSpec((TOP_K, tm), lambda i: (0, i))
    return pl.pallas_call(
        _outproj_kernel,
        grid=(T // tm,),
        in_specs=[tok(D_MODEL), tok(A_WIDTH), tok(B_WIDTH), tok(D_MODEL), tok(D_MODEL),
                  const((A_WIDTH, D_MODEL)), const((B_WIDTH, D_MODEL)), const((D_MODEL, D_MODEL)),
                  const((1, D_MODEL)), const((D_MODEL, LANE)), const((1, LANE)), const((tm, tm)),
                  const((N_EXPERTS, N_EXPERTS))],
        out_specs=[tok(D_MODEL), tok(D_MODEL), kt, kt, pl.BlockSpec((N_EXPERTS, LANE), lambda i: (i, 0))],
        out_shape=[jax.ShapeDtypeStruct((T, D_MODEL), F32), jax.ShapeDtypeStruct((T, D_MODEL), BF16),
                   jax.ShapeDtypeStruct((TOP_K, T), F32), jax.ShapeDtypeStruct((TOP_K, T), I32),
                   jax.ShapeDtypeStruct((T // tm * N_EXPERTS, LANE), I32)],
        compiler_params=_cparams(("parallel",)),
        name="outproj",
    )(x2, oa, ob, ga, gb, wa, wb, wo, n2, wr, br, upper, lower)


def _pack_rows(v, *, is_bf16_valued=False):
    if not is_bf16_valued:
        v = v.astype(BF16).astype(F32)
    lo = pltpu.bitcast(v[:, :ROW_WORDS], U32) >> 16
    hi = pltpu.bitcast(v[:, ROW_WORDS:], U32) & jnp.uint32(0xFFFF0000)
    return lo | hi


def _unpack_rows(w):
    lo = pltpu.bitcast(w << 16, F32).astype(BF16)
    hi = pltpu.bitcast(w & jnp.uint32(0xFFFF0000), F32).astype(BF16)
    return lo, hi


def _start_runs(runs, tile, tile_rows, global_rows, sem, *, to_global):
    cnt_ref, off_ref, dest_ref, _ = runs

    def run(e):
        j = tile * N_EXPERTS + e
        n = pl.multiple_of(cnt_ref[j], RUN_ALIGN)
        local = tile_rows.at[pl.ds(pl.multiple_of(off_ref[j], RUN_ALIGN), n), :]
        far = global_rows.at[pl.ds(pl.multiple_of(dest_ref[j], RUN_ALIGN), n), :]
        return pltpu.make_async_copy(local, far, sem) if to_global else pltpu.make_async_copy(far, local, sem)

    for e in range(N_EXPERTS):
        pl.when(cnt_ref[tile * N_EXPERTS + e] > 0)(lambda e=e: run(e).start())


def _wait_runs(runs, tile, tile_rows, global_rows, sem):
    total = pl.multiple_of(runs[3][tile], RUN_ALIGN)
    pltpu.make_async_copy(global_rows.at[pl.ds(0, total), :], tile_rows.at[pl.ds(0, total), :], sem).wait()


def _dispatch_kernel(fill_ref, cnt_ref, off_ref, dest_ref, tot_ref, lp_ref, hn_p_ref, hn_s_ref, xs_ref,
                     rows2, sems, zbuf, zsem, *, steps_p):
    i = pl.program_id(0)
    runs = (cnt_ref, off_ref, dest_ref, tot_ref)
    slot = i % 2
    rows = rows2.at[slot]

    @pl.when(i == 0)
    def _():
        zbuf[...] = jnp.zeros_like(zbuf)

        def fill_copy(j):
            return pltpu.make_async_copy(zbuf, xs_ref.at[pl.ds(pl.multiple_of(fill_ref[j], MOE_BM), MOE_BM), :], zsem)

        for j in range(fill_ref.shape[0]):
            pl.when(fill_ref[j] >= 0)(lambda j=j: fill_copy(j).start())
        for j in range(fill_ref.shape[0]):
            pl.when(fill_ref[j] >= 0)(lambda j=j: fill_copy(j).wait())

    def sort_tile(hn_ref):
        hn = hn_ref[...]
        r = lax.broadcasted_iota(I32, (SORT_CHUNK, TOK_TILE), 0).astype(I16)
        for c in range(SORT_ROWS // SORT_CHUNK):
            lp = (lp_ref[...] - c * SORT_CHUNK).astype(I16)
            hit = (r == lp[0:1, :]) | (r == lp[1:2, :]) | (r == lp[2:3, :]) | (r == lp[3:4, :])
            perm = jnp.where(hit, jnp.ones((), BF16), jnp.zeros((), BF16))
            rows[c * SORT_CHUNK:(c + 1) * SORT_CHUNK, :] = _pack_rows(_dot(perm, hn), is_bf16_valued=True)

    pl.when(i < steps_p)(lambda: sort_tile(hn_p_ref))
    pl.when(i >= steps_p)(lambda: sort_tile(hn_s_ref))
    _start_runs(runs, i, rows, xs_ref, sems.at[slot], to_global=True)
    pl.when(i > 0)(lambda: _wait_runs(runs, i - 1, rows2.at[1 - slot], xs_ref, sems.at[1 - slot]))
    pl.when(i == pl.num_programs(0) - 1)(lambda: _wait_runs(runs, i, rows, xs_ref, sems.at[slot]))


def _dispatch(fill, runs, lp, hn_p, hn_s, n_rows):
    tm = TOK_TILE
    steps_p, steps_s = hn_p.shape[0] // tm, hn_s.shape[0] // tm
    return pl.pallas_call(
        functools.partial(_dispatch_kernel, steps_p=steps_p),
        grid_spec=pltpu.PrefetchScalarGridSpec(
            num_scalar_prefetch=5,
            grid=(steps_p + steps_s,),
            in_specs=[pl.BlockSpec((TOP_K, tm), lambda i, *_: (0, i)),
                      pl.BlockSpec((tm, D_MODEL), lambda i, *_: (jnp.minimum(i, steps_p - 1), 0)),
                      pl.BlockSpec((tm, D_MODEL), lambda i, *_: (jnp.maximum(i - steps_p, 0), 0))],
            out_specs=pl.BlockSpec(memory_space=pl.ANY),
            scratch_shapes=[pltpu.VMEM((2, SORT_ROWS, ROW_WORDS), U32), pltpu.SemaphoreType.DMA((2,)),
                            pltpu.VMEM((MOE_BM, ROW_WORDS), U32), pltpu.SemaphoreType.DMA],
        ),
        out_shape=jax.ShapeDtypeStruct((n_rows, ROW_WORDS), U32),
        compiler_params=_cparams(("arbitrary",)),
        name="dispatch",
    )(fill, *runs, lp, hn_p, hn_s)


def _experts_kernel(be_ref, rows_ref, nu_ref, slot_ref, pa_ref, pb_ref, pc_ref, xs_ref, w1a_ref, w1b_ref, w2_ref,
                    b1g_ref, b1l_ref, b2_ref, sel_ref, y_ref, w1g_scr, w1l_scr, w2_scr):
    del nu_ref
    i = pl.program_id(0)
    used = rows_ref[i]
    half_k = D_MODEL // 2

    def arrives(piece_ref):
        return (i == 0) | (piece_ref[i] != piece_ref[jnp.maximum(i - 1, 0)])

    def take_w1_half(w_ref, piece_ref, r0):
        slot = slot_ref[piece_ref[i]]
        for c in range(D_FF // LANE):
            cols = _dot(w_ref[0, :, c * 2 * LANE:(c + 1) * 2 * LANE].astype(BF16), sel_ref[...])
            w1g_scr[slot, r0:r0 + half_k, c * LANE:(c + 1) * LANE] = cols[:, :LANE].astype(BF16)
            w1l_scr[slot, r0:r0 + half_k, c * LANE:(c + 1) * LANE] = cols[:, LANE:].astype(BF16)

    pl.when(arrives(pa_ref))(lambda: take_w1_half(w1a_ref, pa_ref, 0))
    pl.when(arrives(pb_ref))(lambda: take_w1_half(w1b_ref, pb_ref, half_k))

    @pl.when(arrives(pc_ref))
    def _():
        w2_scr[slot_ref[pc_ref[i]]] = w2_ref[0].astype(BF16)

    cur = slot_ref[be_ref[i]]

    def mlp(words):
        x_lo, x_hi = _unpack_rows(words)

        def up(w_scr, b_ref):
            return _dot(x_lo, w_scr[cur, :ROW_WORDS, :]) + _dot(x_hi, w_scr[cur, ROW_WORDS:, :]) + b_ref[0]

        x_glu = jnp.minimum(up(w1g_scr, b1g_ref), SWIGLU_LIMIT)
        x_lin = jnp.clip(up(w1l_scr, b1l_ref), -SWIGLU_LIMIT, SWIGLU_LIMIT)
        act = x_glu * jax.nn.sigmoid(SWIGLU_ALPHA * x_glu) * (x_lin + 1.0)
        return _pack_rows(_dot(act.astype(BF16), w2_scr[cur]) + b2_ref[0])

    for rows in range(EXPERT_ROW_STEP, MOE_BM + 1, EXPERT_ROW_STEP):
        @pl.when((used > rows - EXPERT_ROW_STEP) & (used <= rows))
        def _(rows=rows):
            y_ref[:rows, :] = mlp(xs_ref[:rows, :])
            if rows < MOE_BM:
                y_ref[rows:, :] = jnp.zeros((MOE_BM - rows, ROW_WORDS), U32)

    @pl.when(used == 0)
    def _():
        y_ref[...] = jnp.zeros_like(y_ref)


def _experts(blk_e, blk_rows, n_used, slot_e, pieces, xs, w1, b1g, b1l, w2, b2, sel):
    nblk = xs.shape[0] // MOE_BM
    by_expert = lambda r, c: pl.BlockSpec((1, r, c), lambda i, be, *_: (be[i], 0, 0))
    piece = lambda p, r, c, blk: pl.BlockSpec((1, r, c), lambda i, be, rows, nu, sl, *pc: (pc[p][i], blk, 0))
    return pl.pallas_call(
        _experts_kernel,
        grid_spec=pltpu.PrefetchScalarGridSpec(
            num_scalar_prefetch=4 + len(pieces),
            grid=(nblk,),
            in_specs=[pl.BlockSpec((MOE_BM, ROW_WORDS), lambda i, be, rows, nu, *_: (jnp.minimum(i, nu[0] - 1), 0)),
                      piece(0, D_MODEL // 2, 2 * D_FF, 0), piece(1, D_MODEL // 2, 2 * D_FF, 1),
                      piece(2, D_FF, D_MODEL, 0),
                      by_expert(1, D_FF), by_expert(1, D_FF), by_expert(1, D_MODEL),
                      pl.BlockSpec((2 * LANE, 2 * LANE), lambda i, *_: (0, 0))],
            out_specs=pl.BlockSpec((MOE_BM, ROW_WORDS), lambda i, *_: (i, 0)),
            scratch_shapes=[pltpu.VMEM((2, D_MODEL, D_FF), BF16), pltpu.VMEM((2, D_MODEL, D_FF), BF16),
                            pltpu.VMEM((2, D_FF, D_MODEL), BF16)],
        ),
        out_shape=jax.ShapeDtypeStruct(xs.shape, U32),
        compiler_params=_cparams(("arbitrary",)),
        name="experts",
    )(blk_e, blk_rows, n_used, slot_e, *pieces, xs, w1, w1, w2, b1g, b1l, b2, sel)


def _combine_kernel(cnt_ref, off_ref, dest_ref, tot_ref, lp_ref, g_ref, h_ref, yb_ref, y_ref, rows2, sems, *, tile0):
    i = pl.program_id(0)
    runs = (cnt_ref, off_ref, dest_ref, tot_ref)
    slot = i % 2
    rows = rows2.at[slot]

    def fetch(step, into):
        _start_runs(runs, step + tile0, rows2.at[into], yb_ref, sems.at[into], to_global=False)

    @pl.when(i == 0)
    def _():
        rows2[...] = jnp.zeros_like(rows2)
        fetch(0, 0)

    pl.when(i + 1 < pl.num_programs(0))(lambda: fetch(i + 1, 1 - slot))
    _wait_runs(runs, i + tile0, rows, yb_ref, sems.at[slot])
    lp, g = lp_ref[...].astype(I16), g_ref[...].astype(BF16)
    r = lax.broadcasted_iota(I32, (TOK_TILE, SORT_ROWS), 1).astype(I16)
    mix = jnp.zeros((TOK_TILE, SORT_ROWS), BF16)
    for k in range(TOP_K):
        mix = jnp.where(r == lp[:, k:k + 1], g[:, k:k + 1], mix)
    y_lo, y_hi = _unpack_rows(rows[...])
    y_ref[...] = h_ref[...] + jnp.concatenate([_dot(mix, y_lo), _dot(mix, y_hi)], axis=-1)


def _combine(runs, lp_t, gate_t, h, yb, tile0):
    T = h.shape[0]
    tm = TOK_TILE
    tok = lambda w: pl.BlockSpec((tm, w), lambda i, *_: (i, 0))
    return pl.pallas_call(
        functools.partial(_combine_kernel, tile0=tile0),
        grid_spec=pltpu.PrefetchScalarGridSpec(
            num_scalar_prefetch=4,
            grid=(T // tm,),
            in_specs=[tok(TOP_K), tok(TOP_K), tok(D_MODEL), pl.BlockSpec(memory_space=pl.ANY)],
            out_specs=tok(D_MODEL),
            scratch_shapes=[pltpu.VMEM((2, SORT_ROWS, ROW_WORDS), U32), pltpu.SemaphoreType.DMA((2,))],
        ),
        out_shape=jax.ShapeDtypeStruct((T, D_MODEL), F32),
        compiler_params=_cparams(("arbitrary",)),
        name="combine",
    )(*runs, lp_t, gate_t, h, yb)


def _block_ones(width, block):
    idx = np.arange(width) // block
    return jnp.asarray(idx[:, None] == idx[None, :], BF16)


def _layer(x_p, x_s, cache_k, cache_v, s_gla, norm1, w_in, q_norm, k_norm, sinks, w_gk, b_gk,
           gla_norm, w_a, w_b, w_o, norm2, w_router, b_router, w1, b1, w2, b2):
    Bp, Lp, _ = x_p.shape
    Bs, Ls, _ = x_s.shape
    Tp, Ts = Bp * Lp, Bs * Ls

    w_packed = _pack_w_in(w_in)
    wgk = jnp.pad(w_gk, ((0, GL_PAD - GLA_GATE_RANK), (0, 0))).astype(BF16)
    n1 = norm1.reshape(1, D_MODEL)
    gq, gk = _block_ones(A_WIDTH, SWA_HEAD_DIM), _block_ones(KV_WIDTH, SWA_HEAD_DIM)
    qn = jnp.tile(q_norm, SWA_HEADS).reshape(1, A_WIDTH)
    kn = jnp.tile(k_norm, SWA_KV_HEADS).reshape(1, KV_WIDTH)
    bgk = b_gk.reshape(1, GLA_K_WIDTH)
    gn = gla_norm.reshape(1, GLA_DV)
    wa, wb, wo = w_a.astype(BF16), w_b.astype(BF16), w_o.astype(BF16)
    n2 = norm2.reshape(1, D_MODEL)
    wr = jnp.pad(w_router, ((0, 0), (0, LANE - N_EXPERTS))).astype(BF16)
    br = jnp.pad(b_router, (0, LANE - N_EXPERTS)).reshape(1, LANE)
    b1g = b1[:, 0::2].reshape(N_EXPERTS, 1, D_FF)
    b1l = b1[:, 1::2].reshape(N_EXPERTS, 1, D_FF)
    b2r = b2.reshape(N_EXPERTS, 1, D_MODEL)
    sel_np = np.zeros((2 * LANE, 2 * LANE), np.float32)
    sel_np[2 * np.arange(LANE), np.arange(LANE)] = 1.0
    sel_np[2 * np.arange(LANE) + 1, LANE + np.arange(LANE)] = 1.0
    sel = jnp.asarray(sel_np, BF16)

    def mix(x, k_past, v_past, s0):
        B, L, _ = x.shape
        T = B * L
        x2 = x.reshape(T, D_MODEL)
        r3 = lambda t: t.reshape(B, L, t.shape[-1])
        if k_past is None and L % TOK_TILE == 0:
            oa, ka, va, qb, kb, vb, la, rb, ga, gb = _inproj_swa(sinks, x2, L, n1, w_packed, gq, gk, qn, kn, wgk, bgk)
            ob, s_out = _gla(r3(qb), r3(kb), r3(vb), r3(la), r3(rb), gn, s0)
        else:
            qa, ka, va, qb, kb, vb, la, rb, ga, gb = _inproj(x2, n1, w_packed, gq, gk, qn, kn, wgk, bgk)
            if k_past is not None and L < min(WINDOW, GLA_CHUNK):
                fm = lambda c: jnp.transpose(c, (0, 2, 3, 1)).reshape(B, KV_WIDTH, win)
                oa, ob, s_out = _short_mixers(sinks, r3(qa), r3(ka), r3(va), fm(k_past), fm(v_past),
                                              r3(qb), r3(kb), r3(vb), r3(la), r3(rb), gn, s0)
            else:
                if k_past is not None:
                    k_past, v_past = k_past.reshape(B, win, KV_WIDTH), v_past.reshape(B, win, KV_WIDTH)
                oa = _swa(sinks, r3(qa), r3(ka), r3(va), k_past, v_past, has_past=k_past is not None)
                ob, s_out = _gla(r3(qb), r3(kb), r3(vb), r3(la), r3(rb), gn, s0)
        h, hn, gate, lp, cnt = _outproj(x2, oa.reshape(T, A_WIDTH), ob.reshape(T, B_WIDTH), ga, gb,
                                        wa, wb, wo, n2, wr, br, upper, lower)
        return r3(ka), r3(va), s_out, h, hn, gate, lp, cnt[:, 0].reshape(T // TOK_TILE, N_EXPERTS)

    upper = jnp.asarray(np.arange(TOK_TILE)[:, None] < np.arange(TOK_TILE)[None, :], BF16)
    lower = jnp.asarray(np.arange(N_EXPERTS)[:, None] > np.arange(N_EXPERTS)[None, :], BF16)
    win = cache_k.shape[1]
    assert win == WINDOW
    ka_p, va_p, s_p, h_p, hn_p, g_p, lp_p, cnt_p = mix(x_p, None, None, jnp.zeros((Bp, GLA_HEADS, GLA_DK, GLA_DV), F32))
    ka_s, va_s, s_s, h_s, hn_s, g_s, lp_s, cnt_s = mix(x_s, cache_k, cache_v, s_gla)

    T = Tp + Ts
    cnt = jnp.concatenate([cnt_p, cnt_s], axis=0)
    most_rows = T * TOP_K + cnt.size * (RUN_ALIGN - 1) + N_EXPERTS * (MOE_BM - 1)
    n_rows = -(-most_rows // MOE_BM) * MOE_BM
    counts = jnp.sum(cnt, axis=0)
    padded = (counts + MOE_BM - 1) // MOE_BM * MOE_BM
    pad_end = jnp.cumsum(padded)
    start = pad_end - padded
    dest = start[None, :] + jnp.cumsum(cnt, axis=0) - cnt
    off = jnp.cumsum(cnt, axis=1) - cnt
    flat = lambda a: a.reshape(-1).astype(I32)
    runs = (flat(cnt), flat(off), flat(dest), flat(jnp.sum(cnt, axis=1)))
    tails = jnp.where(padded > 0, pad_end - MOE_BM, -1)
    most_spare = (n_rows - T * TOP_K) // MOE_BM
    spare = pad_end[-1] + jnp.arange(most_spare, dtype=I32) * MOE_BM
    fill = jnp.concatenate([tails, jnp.where(spare < n_rows, spare, -1)]).astype(I32)
    blk_row = jnp.arange(n_rows // MOE_BM, dtype=I32) * MOE_BM
    blk_e = jnp.minimum(jnp.sum(pad_end[None, :] <= blk_row[:, None], axis=1), N_EXPERTS - 1).astype(I32)
    used_end = start + counts
    blk_rows = jnp.clip(jnp.sum(jnp.where(blk_e[:, None] == jnp.arange(N_EXPERTS)[None, :], used_end[None, :], 0),
                                axis=1) - blk_row, 0, MOE_BM)
    blk_rows = jnp.where(blk_row < pad_end[-1], blk_rows, 0).astype(I32)
    n_used = (pad_end[-1:] // MOE_BM).astype(I32)
    e_ids = jnp.arange(N_EXPERTS, dtype=I32)
    live = padded > 0
    blk_e = jnp.where(blk_row < pad_end[-1], blk_e, jnp.max(jnp.where(live, e_ids, 0)))
    later = jnp.where(live[None, :] & (e_ids[None, :] > e_ids[:, None]), e_ids[None, :], N_EXPERTS)
    nxt = jnp.min(later, axis=1)
    nxt = jnp.where(nxt < N_EXPERTS, nxt, e_ids)
    pick = lambda table: jnp.sum(jnp.where(blk_e[:, None] == e_ids[None, :], table[None, :], 0), axis=1)
    first_blk, next_first = pick(start // MOE_BM), pick(pad_end // MOE_BM)
    blk_i = jnp.arange(n_rows // MOE_BM, dtype=I32)
    pieces = tuple(jnp.where(blk_i >= jnp.maximum(next_first - lead, first_blk + 1), pick(nxt), blk_e).astype(I32)
                   for lead in EXPERT_PIECE_LEAD)
    slot_e = ((jnp.cumsum(live.astype(I32)) - 1) & 1).astype(I32)

    xs = _dispatch(fill, runs, jnp.concatenate([lp_p, lp_s], axis=1), hn_p, hn_s, n_rows)
    yb = _experts(blk_e, blk_rows, n_used, slot_e, pieces, xs, w1, b1g, b1l, w2, b2r, sel)
    y_p = _combine(runs, lp_p.T, g_p.T, h_p, yb, 0).reshape(Bp, Lp, D_MODEL)
    y_s = _combine(runs, lp_s.T, g_s.T, h_s, yb, Tp // TOK_TILE).reshape(Bs, Ls, D_MODEL)

    kv = lambda t: t.reshape(t.shape[0], t.shape[1], SWA_KV_HEADS, SWA_HEAD_DIM)
    k_p, v_p = kv(ka_p[:, -WINDOW:]), kv(va_p[:, -WINDOW:])
    k_s = jnp.concatenate([cache_k, kv(ka_s)], axis=1)[:, -win:]
    v_s = jnp.concatenate([cache_v, kv(va_s)], axis=1)[:, -win:]
    return y_p, y_s, (k_p, v_p, s_p, k_s, v_s, s_s)


def kernel(x_prompt, x_sample, cache_swa_k, cache_swa_v, state_gla, norm1, w_in, q_norm, k_norm, sinks, w_gk, b_gk,
           gla_norm, w_a, w_b, w_o, norm2, w_router, b_router, w1, b1, w2, b2):
    depth = norm1.shape[0]
    y_p, y_s = x_prompt, x_sample
    states = []
    for l in range(depth):
        y_p, y_s, st = _layer(y_p, y_s, cache_swa_k[l], cache_swa_v[l], state_gla[l], norm1[l], w_in[l], q_norm[l],
                              k_norm[l], sinks[l], w_gk[l], b_gk[l], gla_norm[l], w_a[l], w_b[l], w_o[l], norm2[l],
                              w_router[l], b_router[l], w1[l], b1[l], w2[l], b2[l])
        states.append(st)
    return (y_p, y_s) + tuple(jnp.stack([st[j] for st in states]) for j in range(6))
```

```python
import functools

import jax
import jax.numpy as jnp
import numpy as np
from jax import lax
from jax.experimental import pallas as pl
from jax.experimental.pallas import tpu as pltpu

F32 = jnp.float32
BF16 = jnp.bfloat16
I32 = jnp.int32
U32 = jnp.uint32
I16 = jnp.int16

D_MODEL = 1024
SWA_HEADS = 8
SWA_KV_HEADS = 2
SWA_GROUP = SWA_HEADS // SWA_KV_HEADS
SWA_HEAD_DIM = 64
WINDOW = 128
ATTN_SCALE = SWA_HEAD_DIM ** -0.5
GLA_HEADS = 4
GLA_DK = 64
GLA_DV = 128
GLA_GATE_RANK = 16
GLA_GATE_NORM = 16.0
GLA_SCALE = GLA_DK ** -0.5
A_WIDTH = SWA_HEADS * SWA_HEAD_DIM
KV_WIDTH = SWA_KV_HEADS * SWA_HEAD_DIM
GLA_K_WIDTH = GLA_HEADS * GLA_DK
B_WIDTH = GLA_HEADS * GLA_DV
N_EXPERTS = 32
TOP_K = 4
D_FF = D_MODEL
SWIGLU_ALPHA = 1.702
SWIGLU_LIMIT = 7.0
NORM_EPS = 1e-5
QK_EPS = 1e-6
NEG_INF = -1e30

LANE = 128
GL_PAD = LANE
VMEM_LIMIT = 56 * 1024 * 1024

_SEGS = (("qa", A_WIDTH), ("ka", KV_WIDTH), ("va", KV_WIDTH), ("qb", GLA_K_WIDTH), ("kb", GLA_K_WIDTH),
         ("vb", B_WIDTH), ("rb", B_WIDTH), ("ga", D_MODEL), ("gb", D_MODEL), ("gl", GL_PAD))
_OFF = {}
_o = 0
for _n, _w in _SEGS:
    _OFF[_n] = (_o, _w)
    _o += _w
IN_PACKED = _o

TOK_TILE = 512
GLA_CHUNK = 128
GLA_SUB = 16
MOE_BM = 512
EXPERT_ROW_STEP = 128
EXPERT_PIECE_LEAD = (3, 2, 1)
ROW_WORDS = D_MODEL // 2
RUN_ALIGN = 8
SORT_CHUNK = 256
SORT_ROWS = -(-(TOK_TILE * TOP_K + N_EXPERTS * (RUN_ALIGN - 1)) // SORT_CHUNK) * SORT_CHUNK
SAMPLE_BATCH = 8
GLA_LONG_BATCH = 8
GLA_SEQ_TILE = 512


def _cparams(sem):
    return pltpu.CompilerParams(dimension_semantics=sem, vmem_limit_bytes=VMEM_LIMIT)


def _split_bf16(v):
    hi = v.astype(BF16)
    lo = (v - hi.astype(F32)).astype(BF16)
    return hi, lo


def _dot(a, b):
    return jnp.dot(a, b, preferred_element_type=F32)


def _dot_nt(a, b):
    return lax.dot_general(a, b, (((1,), (1,)), ((), ())), preferred_element_type=F32)


def _dot_tn(a, b):
    return lax.dot_general(a, b, (((0,), (0,)), ((), ())), preferred_element_type=F32)


_SRC = {}
_o = 0
for _n, _w in (("qa", A_WIDTH), ("ka", KV_WIDTH), ("va", KV_WIDTH), ("qb", GLA_K_WIDTH), ("kb", GLA_K_WIDTH),
               ("vb", B_WIDTH), ("gl", GLA_GATE_RANK), ("rb", B_WIDTH), ("ga", D_MODEL), ("gb", D_MODEL)):
    _SRC[_n] = (_o, _w)
    _o += _w
IN_WIDTH = _o
PACK_ROWS = 128


def _pack_w_in_kernel(w_ref, o_ref):
    lane = lax.broadcasted_iota(jnp.int32, (PACK_ROWS, LANE), 1)
    for name, width in _SEGS:
        src, have = _SRC[name]
        dst, _ = _OFF[name]
        for c in range(0, width, LANE):
            t = w_ref[src + c:src + c + LANE, :].T
            if have - c < LANE:
                t = jnp.where(lane < have - c, t, 0.0)
            o_ref[:, dst + c:dst + c + LANE] = t.astype(BF16)


def _pack_w_in(w_in):
    return pl.pallas_call(
        _pack_w_in_kernel,
        grid=(D_MODEL // PACK_ROWS,),
        in_specs=[pl.BlockSpec((IN_WIDTH, PACK_ROWS), lambda i: (0, i))],
        out_specs=pl.BlockSpec((PACK_ROWS, IN_PACKED), lambda i: (i, 0)),
        out_shape=jax.ShapeDtypeStruct((D_MODEL, IN_PACKED), BF16),
        compiler_params=_cparams(("parallel",)),
        name="pack_w_in",
    )(w_in.T)


def _projections(x_ref, n1_ref, w_ref, gq_ref, gk_ref, qn_ref, kn_ref, wgk_ref, bgk_ref):
    x = x_ref[...]
    ms = jnp.mean(x * x, axis=-1, keepdims=True)
    xn = (x * lax.rsqrt(ms + NORM_EPS) * n1_ref[...]).astype(BF16)

    def proj(name):
        off, width = _OFF[name]
        return _dot(xn, w_ref[:, off:off + width])

    def head_norm(v, ones_ref, gain_ref):
        hi, lo = _split_bf16(v * v)
        ss = _dot(hi, ones_ref[...]) + _dot(lo, ones_ref[...])
        return v * lax.rsqrt(ss * (1.0 / SWA_HEAD_DIM) + QK_EPS) * gain_ref[...]

    def log_decay(gl):
        z = _dot(gl.astype(BF16), wgk_ref[...]) + bgk_ref[...]
        return (jnp.minimum(z, 0.0) - jnp.log1p(jnp.exp(-jnp.abs(z)))) * (1.0 / GLA_GATE_NORM)

    epilogue = {"qa": lambda u: head_norm(u, gq_ref, qn_ref) * ATTN_SCALE,
                "ka": lambda u: head_norm(u, gk_ref, kn_ref),
                "va": lambda u: u,
                "qb": lambda u: u * GLA_SCALE,
                "kb": lambda u: u,
                "vb": lambda u: u,
                "gl": log_decay,
                "rb": lambda u: u * jax.nn.sigmoid(u),
                "ga": jax.nn.sigmoid,
                "gb": jax.nn.sigmoid}
    return proj, epilogue


def _run_plan(proj, epilogue, plan, before=None, between=None):
    u = proj(plan[0][0])
    for idx, (name, out_ref) in enumerate(plan):
        if before is not None:
            before(idx)
        u_next = proj(plan[idx + 1][0]) if idx + 1 < len(plan) else None
        out_ref[...] = epilogue[name](u).astype(out_ref.dtype)
        if between is not None:
            between(idx)
        u = u_next


def _inproj_kernel(x_ref, n1_ref, w_ref, gq_ref, gk_ref, qn_ref, kn_ref, wgk_ref, bgk_ref,
                   qa_ref, ka_ref, va_ref, qb_ref, kb_ref, vb_ref, la_ref, rb_ref, ga_ref, gb_ref):
    proj, epilogue = _projections(x_ref, n1_ref, w_ref, gq_ref, gk_ref, qn_ref, kn_ref, wgk_ref, bgk_ref)
    _run_plan(proj, epilogue, (("qa", qa_ref), ("ka", ka_ref), ("va", va_ref), ("qb", qb_ref), ("kb", kb_ref),
                               ("vb", vb_ref), ("gl", la_ref), ("rb", rb_ref), ("ga", ga_ref), ("gb", gb_ref)))


def _inproj_swa_kernel(sink_ref, x_ref, n1_ref, w_ref, gq_ref, gk_ref, qn_ref, kn_ref, wgk_ref, bgk_ref,
                       oa_ref, ka_ref, va_ref, qb_ref, kb_ref, vb_ref, la_ref, rb_ref, ga_ref, gb_ref,
                       k_prev, v_prev, *, tiles_per_seq):
    t = pl.program_id(0)
    has_past = (t % tiles_per_seq) != 0

    @pl.when(t == 0)
    def _():
        k_prev[...] = jnp.zeros_like(k_prev)
        v_prev[...] = jnp.zeros_like(v_prev)

    proj, epilogue = _projections(x_ref, n1_ref, w_ref, gq_ref, gk_ref, qn_ref, kn_ref, wgk_ref, bgk_ref)
    tm = x_ref.shape[0]
    q = epilogue["qa"](proj("qa")).astype(BF16)
    k_new = epilogue["ka"](proj("ka"))
    v_new = proj("va")
    ka_ref[...] = k_new
    va_ref[...] = v_new
    k_all = jnp.concatenate([k_prev[...], k_new.astype(BF16)], axis=0)
    v_all = jnp.concatenate([v_prev[...], v_new.astype(BF16)], axis=0)
    k_prev[...] = k_all[tm:, :]
    v_prev[...] = v_all[tm:, :]

    qi = lax.broadcasted_iota(I32, (WINDOW, 2 * WINDOW), 0)
    ci = lax.broadcasted_iota(I32, (WINDOW, 2 * WINDOW), 1)
    dist = WINDOW + qi - ci
    band = (dist >= 0) & (dist < WINDOW)
    band_first = band & ((ci >= WINDOW) | has_past)
    outs = {}

    def keys_of(blk, h):
        return (slice(blk * WINDOW, (blk + 2) * WINDOW),
                slice(h // SWA_GROUP * SWA_HEAD_DIM, (h // SWA_GROUP + 1) * SWA_HEAD_DIM))

    def scores(blk, h):
        return _dot_nt(q[blk * WINDOW:(blk + 1) * WINDOW, h * SWA_HEAD_DIM:(h + 1) * SWA_HEAD_DIM],
                       k_all[keys_of(blk, h)])

    def finish(blk, h, s):
        s = jnp.where(band_first if blk == 0 else band, s, NEG_INF)
        sink = sink_ref[h]
        m = jnp.maximum(jnp.max(s, axis=-1, keepdims=True), sink)
        p = jnp.exp(s - m)
        denom = jnp.sum(p, axis=-1, keepdims=True) + jnp.exp(sink - m)
        outs[blk, h] = _dot(p.astype(BF16), v_all[keys_of(blk, h)]) / denom

    units = [(blk, h) for blk in range(tm // WINDOW) for h in range(SWA_HEADS)]
    plan = (("qb", qb_ref), ("kb", kb_ref), ("vb", vb_ref), ("gl", la_ref), ("rb", rb_ref), ("ga", ga_ref),
            ("gb", gb_ref))

    ahead = [_OFF[name][1] for name, _ in plan[1:]] + [0]
    cuts = [len(units) * sum(ahead[:i]) // sum(ahead) for i in range(len(plan) + 1)]
    pending = {}

    def before(idx):
        for unit in units[cuts[idx]:cuts[idx + 1]]:
            pending[unit] = scores(*unit)

    def between(idx):
        for unit in units[cuts[idx]:cuts[idx + 1]]:
            finish(*unit, pending.pop(unit))

    _run_plan(proj, epilogue, plan, before, between)
    for blk in range(tm // WINDOW):
        oa_ref[blk * WINDOW:(blk + 1) * WINDOW, :] = jnp.concatenate(
            [outs[blk, h] for h in range(SWA_HEADS)], axis=-1).astype(BF16)


def _inproj(x2, n1, w_packed, gq, gk, qn, kn, wgk, bgk):
    T = x2.shape[0]
    tm = min(TOK_TILE, T)
    assert T % tm == 0

    def tok(width):
        return pl.BlockSpec((tm, width), lambda i: (i, 0))

    def const(shape):
        return pl.BlockSpec(shape, lambda i: (0, 0))

    outs = (("qa", A_WIDTH, BF16), ("ka", KV_WIDTH, F32), ("va", KV_WIDTH, F32), ("qb", GLA_K_WIDTH, BF16),
            ("kb", GLA_K_WIDTH, BF16), ("vb", B_WIDTH, BF16), ("la", GLA_K_WIDTH, F32), ("rb", B_WIDTH, BF16),
            ("ga", D_MODEL, BF16), ("gb", D_MODEL, BF16))
    return pl.pallas_call(
        _inproj_kernel,
        grid=(T // tm,),
        in_specs=[tok(D_MODEL), const((1, D_MODEL)), const((D_MODEL, IN_PACKED)), const((A_WIDTH, A_WIDTH)),
                  const((KV_WIDTH, KV_WIDTH)), const((1, A_WIDTH)), const((1, KV_WIDTH)),
                  const((GL_PAD, GLA_K_WIDTH)), const((1, GLA_K_WIDTH))],
        out_specs=[tok(w) for _, w, _ in outs],
        out_shape=[jax.ShapeDtypeStruct((T, w), dt) for _, w, dt in outs],
        compiler_params=_cparams(("parallel",)),
        name="inproj",
    )(x2, n1, w_packed, gq, gk, qn, kn, wgk, bgk)


def _inproj_swa(sinks, x2, seq_len, n1, w_packed, gq, gk, qn, kn, wgk, bgk):
    T = x2.shape[0]
    tm = TOK_TILE
    assert seq_len % tm == 0 and T % seq_len == 0
    tok = lambda width: pl.BlockSpec((tm, width), lambda i, s: (i, 0))
    const = lambda shape: pl.BlockSpec(shape, lambda i, s: (0, 0))
    outs = (("oa", A_WIDTH, BF16), ("ka", KV_WIDTH, F32), ("va", KV_WIDTH, F32), ("qb", GLA_K_WIDTH, BF16),
            ("kb", GLA_K_WIDTH, BF16), ("vb", B_WIDTH, BF16), ("la", GLA_K_WIDTH, F32), ("rb", B_WIDTH, BF16),
            ("ga", D_MODEL, BF16), ("gb", D_MODEL, BF16))
    return pl.pallas_call(
        functools.partial(_inproj_swa_kernel, tiles_per_seq=seq_len // tm),
        grid_spec=pltpu.PrefetchScalarGridSpec(
            num_scalar_prefetch=1,
            grid=(T // tm,),
            in_specs=[tok(D_MODEL), const((1, D_MODEL)), const((D_MODEL, IN_PACKED)), const((A_WIDTH, A_WIDTH)),
                      const((KV_WIDTH, KV_WIDTH)), const((1, A_WIDTH)), const((1, KV_WIDTH)),
                      const((GL_PAD, GLA_K_WIDTH)), const((1, GLA_K_WIDTH))],
            out_specs=[tok(w) for _, w, _ in outs],
            scratch_shapes=[pltpu.VMEM((WINDOW, KV_WIDTH), BF16), pltpu.VMEM((WINDOW, KV_WIDTH), BF16)],
        ),
        out_shape=[jax.ShapeDtypeStruct((T, w), dt) for _, w, dt in outs],
        compiler_params=_cparams(("arbitrary",)),
        name="inproj_swa",
    )(sinks, x2, n1, w_packed, gq, gk, qn, kn, wgk, bgk)


def _run(phases):
    for _ in phases:
        pass


def _swa_kernel(*refs, **static):
    _run(_swa_phases(*refs, **static))


def _swa_phases(sink_ref, q_ref, kc_ref, vc_ref, kp_ref, vp_ref, o_ref, *, first_block_has_past,
                past_transposed=False, cache_out=None):
    n = pl.program_id(1)
    nb, lq = q_ref.shape[0], q_ref.shape[1]
    assert lq & (lq - 1) == 0
    stack = SWA_GROUP if lq < WINDOW else 1
    rows, keys = stack * lq, WINDOW + lq
    qi = lax.broadcasted_iota(I32, (rows, keys), 0) & (lq - 1)
    ci = lax.broadcasted_iota(I32, (rows, keys), 1)
    dist = WINDOW + qi - ci
    mask = (dist >= 0) & (dist < WINDOW)
    if not first_block_has_past:
        mask = mask & ((ci >= WINDOW) | (n > 0))
    cols = lambda h0: slice(h0 // SWA_GROUP * SWA_HEAD_DIM, (h0 // SWA_GROUP + 1) * SWA_HEAD_DIM)
    qs, ks, vs = [], [], []
    for j in range(nb):
        q = q_ref[j]
        qs.append(q.astype(F32) if stack > 1 else q)
        if past_transposed:
            ks.append((kp_ref[j].astype(BF16), kc_ref[j].astype(BF16)))
            vp = vp_ref[j].T
        else:
            ks.append(jnp.concatenate([kp_ref[j], kc_ref[j]], axis=0).astype(BF16))
            vp = vp_ref[j]
        vs.append(jnp.concatenate([vp, vc_ref[j]], axis=0).astype(BF16))
    if cache_out is not None:
        assert past_transposed and lq < WINDOW
        keep = lax.broadcasted_iota(I32, (KV_WIDTH, WINDOW), 1) < WINDOW - lq
        ri = lax.broadcasted_iota(I32, (4 * lq, WINDOW), 0) & (lq - 1)
        place = jnp.where(lax.broadcasted_iota(I32, (4 * lq, WINDOW), 1) - (WINDOW - lq) == ri, 1.0, 0.0).astype(BF16)
        for past_ref, cur_ref, out_ref in ((kp_ref, kc_ref, cache_out[0]), (vp_ref, vc_ref, cache_out[1])):
            for j in range(nb):
                cur = cur_ref[j]
                hi = cur.astype(BF16).astype(F32)
                mid = (cur - hi).astype(BF16).astype(F32)
                lo = cur - hi - mid
                stack3 = jnp.concatenate([hi, mid, lo, jnp.zeros_like(cur)], axis=0).astype(BF16)
                out_ref[j] = jnp.where(keep, pltpu.roll(past_ref[j], WINDOW - lq, 1), _dot_tn(stack3, place))
    groups = [(j, h0) for j in range(nb) for h0 in range(0, SWA_HEADS, stack)]
    wave = len(groups) if stack > 1 else 1
    outs = {}
    for w0 in range(0, len(groups), wave):
        scores, sinks = [], []
        for j, h0 in groups[w0:w0 + wave]:
            heads = range(h0, h0 + stack)
            qg = jnp.concatenate([qs[j][:, h * SWA_HEAD_DIM:(h + 1) * SWA_HEAD_DIM] for h in heads], axis=0)
            sinks.append(jnp.concatenate([jnp.full((lq, 1), sink_ref[h], F32) for h in heads], axis=0))
            qg = qg.astype(BF16)
            if past_transposed:
                scores.append(jnp.concatenate([_dot(qg, ks[j][0][cols(h0), :]), _dot_nt(qg, ks[j][1][:, cols(h0)])],
                                              axis=1))
            else:
                scores.append(_dot_nt(qg, ks[j][:, cols(h0)]))
        yield
        probs, denoms = [], []
        for s, sink in zip(scores, sinks):
            s = jnp.where(mask, s, NEG_INF)
            m = jnp.maximum(jnp.max(s, axis=-1, keepdims=True), sink)
            p = jnp.exp(s - m)
            denoms.append(jnp.sum(p, axis=-1, keepdims=True) + jnp.exp(sink - m))
            probs.append(p.astype(BF16))
        yield
        for (j, h0), p, denom in zip(groups[w0:w0 + wave], probs, denoms):
            o = _dot(p, vs[j][:, cols(h0)]) / denom
            outs.setdefault(j, []).extend(o[g * lq:(g + 1) * lq] for g in range(stack))
    for j in range(nb):
        o_ref[j] = jnp.concatenate(outs[j], axis=-1).astype(BF16)


def _swa(sinks, qa, ka, va, k_past, v_past, *, has_past):
    B, L, _ = qa.shape
    lq = min(L, WINDOW)
    nl = L // lq
    if has_past:
        assert nl == 1
        nb = min(B, SAMPLE_BATCH)
        past_spec = pl.BlockSpec((nb, WINDOW, KV_WIDTH), lambda b, n, s: (b, 0, 0))
        kp, vp = k_past, v_past
    else:
        assert lq == WINDOW
        nb = 1
        past_spec = pl.BlockSpec((nb, WINDOW, KV_WIDTH), lambda b, n, s: (b, jnp.maximum(n - 1, 0), 0))
        kp, vp = ka, va
    assert B % nb == 0
    cur = lambda w: pl.BlockSpec((nb, lq, w), lambda b, n, s: (b, n, 0))
    return pl.pallas_call(
        functools.partial(_swa_kernel, first_block_has_past=has_past),
        grid_spec=pltpu.PrefetchScalarGridSpec(
            num_scalar_prefetch=1,
            grid=(B // nb, nl),
            in_specs=[cur(A_WIDTH), cur(KV_WIDTH), cur(KV_WIDTH), past_spec, past_spec],
            out_specs=cur(A_WIDTH),
        ),
        out_shape=jax.ShapeDtypeStruct((B, L, A_WIDTH), BF16),
        compiler_params=_cparams(("parallel", "parallel")),
        name="swa",
    )(sinks, qa, ka, va, kp, vp)


def _gla_kernel(*refs, **static):
    _run(_gla_phases(*refs, **static))


def _gla_phases(q_ref, k_ref, v_ref, la_ref, rb_ref, gn_ref, s0_ref, o_ref, sout_ref, s_scr, *, chunk, sub, length,
                whole_sequence=False):
    C, SUB = chunk, sub
    S = C // SUB
    W = GLA_K_WIDTH
    nb = q_ref.shape[0]

    if whole_sequence:
        s_scr[...] = s0_ref[...]
    else:
        @pl.when(pl.program_id(1) == 0)
        def _():
            s_scr[...] = s0_ref[...]

    row = lax.broadcasted_iota(I32, (C, C), 0)
    col = lax.broadcasted_iota(I32, (C, C), 1)
    tri = (row >= col).astype(BF16)
    diag_mask = (row >= col) & ((row // SUB) == (col // SUB))
    eye = lax.broadcasted_iota(I32, (GLA_DK, GLA_DK), 0) == lax.broadcasted_iota(I32, (GLA_DK, GLA_DK), 1)
    krow = lax.broadcasted_iota(I32, (C, W), 0)

    def rows_of(x, r):
        return jnp.broadcast_to(x[r:r + 1, :], (SUB, W))

    ksl = lambda h: slice(h * GLA_DK, (h + 1) * GLA_DK)
    vsl = lambda h: slice(h * GLA_DV, (h + 1) * GLA_DV)

    def decays(j, r0):
        g_hi, g_lo = _split_bf16(la_ref[j, pl.ds(r0, C), :])
        return _dot(tri, g_hi) + _dot(tri, g_lo)

    def factors(j, r0, b):
        q = q_ref[j, pl.ds(r0, C), :].astype(F32)
        k = k_ref[j, pl.ds(r0, C), :].astype(F32)
        b_last = b[C - 1:C, :]
        mid = jnp.concatenate([rows_of(b, i * SUB + SUB // 2 - 1) for i in range(S)], axis=0)
        beta = jnp.concatenate([jnp.zeros((SUB, W), F32)] + [rows_of(b, i * SUB - 1) for i in range(1, S)], axis=0)
        k_off = [None]
        for i in range(1, S):
            e = jnp.exp(jnp.minimum(b[i * SUB - 1:i * SUB, :] - b, 0.0))
            k_off.append(jnp.where(krow < i * SUB, k * e, 0.0).astype(BF16))
        return dict(q_inter=(q * jnp.exp(b)).astype(BF16), k_state=(k * jnp.exp(b_last - b)).astype(BF16),
                    q_diag=(q * jnp.exp(b - mid)).astype(BF16), k_diag=(k * jnp.exp(mid - b)).astype(BF16),
                    q_off=(q * jnp.exp(b - beta)).astype(BF16), k_off=k_off, dec=jnp.exp(b_last))

    def intra(f, h):
        a_diag = _dot_nt(f["q_diag"][:, ksl(h)], f["k_diag"][:, ksl(h)])
        if S == 1:
            return jnp.where(diag_mask, a_diag, 0.0).astype(BF16)
        blocks = [jnp.zeros((SUB, C), F32)]
        for i in range(1, S):
            blocks.append(_dot_nt(f["q_off"][i * SUB:(i + 1) * SUB, ksl(h)], f["k_off"][i][:, ksl(h)]))
        return jnp.where(diag_mask, a_diag, jnp.concatenate(blocks, axis=0)).astype(BF16)

    def read_out(j, r0, f, a):
        v = v_ref[j, pl.ds(r0, C), :]
        res = []
        for h in range(GLA_HEADS):
            o_h = _dot(a[h], v[:, vsl(h)]) + _dot(f["q_inter"][:, ksl(h)], s_scr[j, h].astype(BF16))
            res.append((o_h, _dot_tn(f["k_state"][:, ksl(h)], v[:, vsl(h)])))
        return res

    def finish(j, r0, f, res):
        outs = []
        for h, (o_h, s_inc) in enumerate(res):
            dec = jnp.broadcast_to(f["dec"][:, ksl(h)], (GLA_DK, GLA_DK))
            dec_col = jnp.sum(jnp.where(eye, dec, 0.0), axis=-1, keepdims=True)
            s_scr[j, h] = dec_col * s_scr[j, h] + s_inc
            ms = jnp.mean(o_h * o_h, axis=-1, keepdims=True)
            outs.append(o_h * lax.rsqrt(ms + NORM_EPS) * gn_ref[...])
        o = jnp.concatenate(outs, axis=-1) * rb_ref[j, pl.ds(r0, C), :].astype(F32)
        o_ref[j, pl.ds(r0, C), :] = o.astype(BF16)

    def step(c):
        r0 = c * C if isinstance(c, int) else pl.multiple_of(c * C, C)
        bs = [decays(j, r0) for j in range(nb)]
        yield
        fs = [factors(j, r0, b) for j, b in enumerate(bs)]
        yield
        attn = [[intra(f, h) for h in range(GLA_HEADS)] for f in fs]
        yield
        res = [read_out(j, r0, fs[j], attn[j]) for j in range(nb)]
        yield
        for j in range(nb):
            finish(j, r0, fs[j], res[j])

    if length == C:
        yield from step(0)
    else:
        lax.fori_loop(0, length // C, lambda c, carry: (_run(step(c)), carry)[1], 0)

    if whole_sequence:
        sout_ref[...] = s_scr[...]
    else:
        @pl.when(pl.program_id(1) == pl.num_programs(1) - 1)
        def _():
            sout_ref[...] = s_scr[...]


def _gla(qb, kb, vb, la, rb, gn, s0):
    B, L, _ = qb.shape
    chunk = GLA_CHUNK if L % GLA_CHUNK == 0 else L
    sub = GLA_SUB if chunk % GLA_SUB == 0 else chunk
    nb = min(B, SAMPLE_BATCH) if L == chunk else min(B, GLA_LONG_BATCH)
    lt = min(L, GLA_SEQ_TILE)
    assert B % nb == 0 and L % lt == 0 and lt % chunk == 0
    seq = lambda w: pl.BlockSpec((nb, lt, w), lambda b, l: (b, l, 0))
    st = pl.BlockSpec((nb, GLA_HEADS, GLA_DK, GLA_DV), lambda b, l: (b, 0, 0, 0))
    return pl.pallas_call(
        functools.partial(_gla_kernel, chunk=chunk, sub=sub, length=lt),
        grid=(B // nb, L // lt),
        in_specs=[seq(GLA_K_WIDTH), seq(GLA_K_WIDTH), seq(B_WIDTH), seq(GLA_K_WIDTH), seq(B_WIDTH),
                  pl.BlockSpec((1, GLA_DV), lambda b, l: (0, 0)), st],
        out_specs=[seq(B_WIDTH), st],
        out_shape=[jax.ShapeDtypeStruct((B, L, B_WIDTH), BF16),
                   jax.ShapeDtypeStruct((B, GLA_HEADS, GLA_DK, GLA_DV), F32)],
        scratch_shapes=[pltpu.VMEM((nb, GLA_HEADS, GLA_DK, GLA_DV), F32)],
        compiler_params=_cparams(("parallel", "arbitrary")),
        name="gla",
    )(qb, kb, vb, la, rb, gn, s0)


def _short_mixers_kernel(sink_ref, q_ref, kc_ref, vc_ref, kp_ref, vp_ref, qb_ref, kb_ref, vb_ref, la_ref, rb_ref, gn_ref,
                         s0_ref, oa_ref, ob_ref, sout_ref, ko_ref, vo_ref, s_scr, *, length):
    bodies = [_gla_phases(qb_ref, kb_ref, vb_ref, la_ref, rb_ref, gn_ref, s0_ref, ob_ref, sout_ref, s_scr,
                          chunk=length, sub=length, length=length, whole_sequence=True),
              _swa_phases(sink_ref, q_ref, kc_ref, vc_ref, kp_ref, vp_ref, oa_ref, first_block_has_past=True,
                          past_transposed=True, cache_out=(ko_ref, vo_ref))]
    while bodies:
        for body in list(bodies):
            if next(body, StopIteration) is StopIteration:
                bodies.remove(body)


def _short_mixers(sinks, qa, ka, va, k_past, v_past, qb, kb, vb, la, rb, gn, s0):
    B, L, _ = qa.shape
    assert L < WINDOW and L < GLA_CHUNK
    nb = min(B, SAMPLE_BATCH)
    assert B % nb == 0
    seq = lambda w: pl.BlockSpec((nb, L, w), lambda b, n, s: (b, 0, 0))
    past = pl.BlockSpec((nb, KV_WIDTH, WINDOW), lambda b, n, s: (b, 0, 0))
    st = pl.BlockSpec((nb, GLA_HEADS, GLA_DK, GLA_DV), lambda b, n, s: (b, 0, 0, 0))
    return pl.pallas_call(
        functools.partial(_short_mixers_kernel, length=L),
        grid_spec=pltpu.PrefetchScalarGridSpec(
            num_scalar_prefetch=1,
            grid=(B // nb, 1),
            in_specs=[seq(A_WIDTH), seq(KV_WIDTH), seq(KV_WIDTH), past, past,
                      seq(GLA_K_WIDTH), seq(GLA_K_WIDTH), seq(B_WIDTH), seq(GLA_K_WIDTH), seq(B_WIDTH),
                      pl.BlockSpec((1, GLA_DV), lambda b, n, s: (0, 0)), st],
            out_specs=[seq(A_WIDTH), seq(B_WIDTH), st, past, past],
            scratch_shapes=[pltpu.VMEM((nb, GLA_HEADS, GLA_DK, GLA_DV), F32)],
        ),
        out_shape=[jax.ShapeDtypeStruct((B, L, A_WIDTH), BF16), jax.ShapeDtypeStruct((B, L, B_WIDTH), BF16),
                   jax.ShapeDtypeStruct((B, GLA_HEADS, GLA_DK, GLA_DV), F32),
                   jax.ShapeDtypeStruct((B, KV_WIDTH, WINDOW), F32), jax.ShapeDtypeStruct((B, KV_WIDTH, WINDOW), F32)],
        compiler_params=_cparams(("parallel", "arbitrary")),
        name="short_mixers",
    )(sinks, qa, ka, va, k_past, v_past, qb, kb, vb, la, rb, gn, s0)


def _outproj_kernel(x_ref, oa_ref, ob_ref, ga_ref, gb_ref, wa_ref, wb_ref, wo_ref, n2_ref, wr_ref, br_ref, upper_ref,
                    lower_ref, h_ref, hn_ref, g_ref, lp_ref, cnt_ref):
    tm = x_ref.shape[0]
    merged = (ga_ref[...].astype(F32) * _dot(oa_ref[...], wa_ref[...])
              + gb_ref[...].astype(F32) * _dot(ob_ref[...], wb_ref[...]))
    h = x_ref[...] + _dot(merged.astype(BF16), wo_ref[...])
    h_ref[...] = h
    ms = jnp.mean(h * h, axis=-1, keepdims=True)
    hn = h * lax.rsqrt(ms + NORM_EPS) * n2_ref[...]
    hn_bf = hn.astype(BF16)
    hn_ref[...] = hn_bf
    logits = _dot(hn_bf, wr_ref[...]) + br_ref[...]
    lt = logits.T[:N_EXPERTS, :]
    eid = lax.broadcasted_iota(I32, (N_EXPERTS, tm), 0)
    vals, hots = [], []
    for k in range(TOP_K):
        m = jnp.max(lt, axis=0, keepdims=True)
        idx = jnp.min(jnp.where(lt == m, eid, N_EXPERTS), axis=0, keepdims=True)
        hot = eid == idx
        lt = jnp.where(hot, -jnp.inf, lt)
        vals.append(m)
        hots.append(hot)
    ex = [jnp.exp(vk - vals[0]) for vk in vals]
    den = ex[0] + ex[1] + ex[2] + ex[3]
    for k in range(TOP_K):
        g_ref[k:k + 1, :] = ex[k] / den
    multi = (hots[0] | hots[1] | hots[2] | hots[3]).astype(BF16)
    before = _dot(multi, upper_ref[...])
    counts = jnp.sum(multi.astype(F32), axis=1, keepdims=True)
    aligned = jnp.floor((counts + (RUN_ALIGN - 1)) * (1.0 / RUN_ALIGN)) * RUN_ALIGN
    run_start = _dot(lower_ref[...], jnp.broadcast_to(aligned, (N_EXPERTS, LANE)).astype(BF16))[:, :1]
    place = before + run_start
    for k in range(TOP_K):
        lp_ref[k:k + 1, :] = jnp.sum(jnp.where(hots[k], place, 0.0), axis=0, keepdims=True).astype(I32)
    cnt_ref[...] = jnp.broadcast_to(aligned, cnt_ref.shape).astype(I32)


def _outproj(x2, oa, ob, ga, gb, wa, wb, wo, n2, wr, br, upper, lower):
    T = x2.shape[0]
    tm = TOK_TILE
    assert T % tm == 0 and upper.shape == (tm, tm)
    tok = lambda w: pl.BlockSpec((tm, w), lambda i: (i, 0))
    const = lambda shape: pl.BlockSpec(shape, lambda i: (0, 0))
    kt = pl.BlockSpec((TOP_K, tm), lambda i: (0, i))
    return pl.pallas_call(
        _outproj_kernel,
        grid=(T // tm,),
        in_specs=[tok(D_MODEL), tok(A_WIDTH), tok(B_WIDTH), tok(D_MODEL), tok(D_MODEL),
                  const((A_WIDTH, D_MODEL)), const((B_WIDTH, D_MODEL)), const((D_MODEL, D_MODEL)),
                  const((1, D_MODEL)), const((D_MODEL, LANE)), const((1, LANE)), const((tm, tm)),
                  const((N_EXPERTS, N_EXPERTS))],
        out_specs=[tok(D_MODEL), tok(D_MODEL), kt, kt, pl.BlockSpec((N_EXPERTS, LANE), lambda i: (i, 0))],
        out_shape=[jax.ShapeDtypeStruct((T, D_MODEL), F32), jax.ShapeDtypeStruct((T, D_MODEL), BF16),
                   jax.ShapeDtypeStruct((TOP_K, T), F32), jax.ShapeDtypeStruct((TOP_K, T), I32),
                   jax.ShapeDtypeStruct((T // tm * N_EXPERTS, LANE), I32)],
        compiler_params=_cparams(("parallel",)),
        name="outproj",
    )(x2, oa, ob, ga, gb, wa, wb, wo, n2, wr, br, upper, lower)


def _pack_rows(v, *, is_bf16_valued=False):
    if not is_bf16_valued:
        v = v.astype(BF16).astype(F32)
    lo = pltpu.bitcast(v[:, :ROW_WORDS], U32) >> 16
    hi = pltpu.bitcast(v[:, ROW_WORDS:], U32) & jnp.uint32(0xFFFF0000)
    return lo | hi


def _unpack_rows(w):
    lo = pltpu.bitcast(w << 16, F32).astype(BF16)
    hi = pltpu.bitcast(w & jnp.uint32(0xFFFF0000), F32).astype(BF16)
    return lo, hi


def _start_runs(runs, tile, tile_rows, global_rows, sem, *, to_global):
    cnt_ref, off_ref, dest_ref, _ = runs

    def run(e):
        j = tile * N_EXPERTS + e
        n = pl.multiple_of(cnt_ref[j], RUN_ALIGN)
        local = tile_rows.at[pl.ds(pl.multiple_of(off_ref[j], RUN_ALIGN), n), :]
        far = global_rows.at[pl.ds(pl.multiple_of(dest_ref[j], RUN_ALIGN), n), :]
        return pltpu.make_async_copy(local, far, sem) if to_global else pltpu.make_async_copy(far, local, sem)

    for e in range(N_EXPERTS):
        pl.when(cnt_ref[tile * N_EXPERTS + e] > 0)(lambda e=e: run(e).start())


def _wait_runs(runs, tile, tile_rows, global_rows, sem):
    total = pl.multiple_of(runs[3][tile], RUN_ALIGN)
    pltpu.make_async_copy(global_rows.at[pl.ds(0, total), :], tile_rows.at[pl.ds(0, total), :], sem).wait()


def _dispatch_kernel(fill_ref, cnt_ref, off_ref, dest_ref, tot_ref, lp_ref, hn_p_ref, hn_s_ref, xs_ref,
                     rows2, sems, zbuf, zsem, *, steps_p):
    i = pl.program_id(0)
    runs = (cnt_ref, off_ref, dest_ref, tot_ref)
    slot = i % 2
    rows = rows2.at[slot]

    @pl.when(i == 0)
    def _():
        zbuf[...] = jnp.zeros_like(zbuf)

        def fill_copy(j):
            return pltpu.make_async_copy(zbuf, xs_ref.at[pl.ds(pl.multiple_of(fill_ref[j], MOE_BM), MOE_BM), :], zsem)

        for j in range(fill_ref.shape[0]):
            pl.when(fill_ref[j] >= 0)(lambda j=j: fill_copy(j).start())
        for j in range(fill_ref.shape[0]):
            pl.when(fill_ref[j] >= 0)(lambda j=j: fill_copy(j).wait())

    def sort_tile(hn_ref):
        hn = hn_ref[...]
        r = lax.broadcasted_iota(I32, (SORT_CHUNK, TOK_TILE), 0).astype(I16)
        for c in range(SORT_ROWS // SORT_CHUNK):
            lp = (lp_ref[...] - c * SORT_CHUNK).astype(I16)
            hit = (r == lp[0:1, :]) | (r == lp[1:2, :]) | (r == lp[2:3, :]) | (r == lp[3:4, :])
            perm = jnp.where(hit, jnp.ones((), BF16), jnp.zeros((), BF16))
            rows[c * SORT_CHUNK:(c + 1) * SORT_CHUNK, :] = _pack_rows(_dot(perm, hn), is_bf16_valued=True)

    pl.when(i < steps_p)(lambda: sort_tile(hn_p_ref))
    pl.when(i >= steps_p)(lambda: sort_tile(hn_s_ref))
    _start_runs(runs, i, rows, xs_ref, sems.at[slot], to_global=True)
    pl.when(i > 0)(lambda: _wait_runs(runs, i - 1, rows2.at[1 - slot], xs_ref, sems.at[1 - slot]))
    pl.when(i == pl.num_programs(0) - 1)(lambda: _wait_runs(runs, i, rows, xs_ref, sems.at[slot]))


def _dispatch(fill, runs, lp, hn_p, hn_s, n_rows):
    tm = TOK_TILE
    steps_p, steps_s = hn_p.shape[0] // tm, hn_s.shape[0] // tm
    return pl.pallas_call(
        functools.partial(_dispatch_kernel, steps_p=steps_p),
        grid_spec=pltpu.PrefetchScalarGridSpec(
            num_scalar_prefetch=5,
            grid=(steps_p + steps_s,),
            in_specs=[pl.BlockSpec((TOP_K, tm), lambda i, *_: (0, i)),
                      pl.BlockSpec((tm, D_MODEL), lambda i, *_: (jnp.minimum(i, steps_p - 1), 0)),
                      pl.BlockSpec((tm, D_MODEL), lambda i, *_: (jnp.maximum(i - steps_p, 0), 0))],
            out_specs=pl.BlockSpec(memory_space=pl.ANY),
            scratch_shapes=[pltpu.VMEM((2, SORT_ROWS, ROW_WORDS), U32), pltpu.SemaphoreType.DMA((2,)),
                            pltpu.VMEM((MOE_BM, ROW_WORDS), U32), pltpu.SemaphoreType.DMA],
        ),
        out_shape=jax.ShapeDtypeStruct((n_rows, ROW_WORDS), U32),
        compiler_params=_cparams(("arbitrary",)),
        name="dispatch",
    )(fill, *runs, lp, hn_p, hn_s)


def _experts_kernel(be_ref, rows_ref, nu_ref, slot_ref, pa_ref, pb_ref, pc_ref, xs_ref, w1a_ref, w1b_ref, w2_ref,
                    b1g_ref, b1l_ref, b2_ref, sel_ref, y_ref, w1g_scr, w1l_scr, w2_scr):
    del nu_ref
    i = pl.program_id(0)
    used = rows_ref[i]
    half_k = D_MODEL // 2

    def arrives(piece_ref):
        return (i == 0) | (piece_ref[i] != piece_ref[jnp.maximum(i - 1, 0)])

    def take_w1_half(w_ref, piece_ref, r0):
        slot = slot_ref[piece_ref[i]]
        for c in range(D_FF // LANE):
            cols = _dot(w_ref[0, :, c * 2 * LANE:(c + 1) * 2 * LANE].astype(BF16), sel_ref[...])
            w1g_scr[slot, r0:r0 + half_k, c * LANE:(c + 1) * LANE] = cols[:, :LANE].astype(BF16)
            w1l_scr[slot, r0:r0 + half_k, c * LANE:(c + 1) * LANE] = cols[:, LANE:].astype(BF16)

    pl.when(arrives(pa_ref))(lambda: take_w1_half(w1a_ref, pa_ref, 0))
    pl.when(arrives(pb_ref))(lambda: take_w1_half(w1b_ref, pb_ref, half_k))

    @pl.when(arrives(pc_ref))
    def _():
        w2_scr[slot_ref[pc_ref[i]]] = w2_ref[0].astype(BF16)

    cur = slot_ref[be_ref[i]]

    def mlp(words):
        x_lo, x_hi = _unpack_rows(words)

        def up(w_scr, b_ref):
            return _dot(x_lo, w_scr[cur, :ROW_WORDS, :]) + _dot(x_hi, w_scr[cur, ROW_WORDS:, :]) + b_ref[0]

        x_glu = jnp.minimum(up(w1g_scr, b1g_ref), SWIGLU_LIMIT)
        x_lin = jnp.clip(up(w1l_scr, b1l_ref), -SWIGLU_LIMIT, SWIGLU_LIMIT)
        act = x_glu * jax.nn.sigmoid(SWIGLU_ALPHA * x_glu) * (x_lin + 1.0)
        return _pack_rows(_dot(act.astype(BF16), w2_scr[cur]) + b2_ref[0])

    for rows in range(EXPERT_ROW_STEP, MOE_BM + 1, EXPERT_ROW_STEP):
        @pl.when((used > rows - EXPERT_ROW_STEP) & (used <= rows))
        def _(rows=rows):
            y_ref[:rows, :] = mlp(xs_ref[:rows, :])
            if rows < MOE_BM:
                y_ref[rows:, :] = jnp.zeros((MOE_BM - rows, ROW_WORDS), U32)

    @pl.when(used == 0)
    def _():
        y_ref[...] = jnp.zeros_like(y_ref)


def _experts(blk_e, blk_rows, n_used, slot_e, pieces, xs, w1, b1g, b1l, w2, b2, sel):
    nblk = xs.shape[0] // MOE_BM
    by_expert = lambda r, c: pl.BlockSpec((1, r, c), lambda i, be, *_: (be[i], 0, 0))
    piece = lambda p, r, c, blk: pl.BlockSpec((1, r, c), lambda i, be, rows, nu, sl, *pc: (pc[p][i], blk, 0))
    return pl.pallas_call(
        _experts_kernel,
        grid_spec=pltpu.PrefetchScalarGridSpec(
            num_scalar_prefetch=4 + len(pieces),
            grid=(nblk,),
            in_specs=[pl.BlockSpec((MOE_BM, ROW_WORDS), lambda i, be, rows, nu, *_: (jnp.minimum(i, nu[0] - 1), 0)),
                      piece(0, D_MODEL // 2, 2 * D_FF, 0), piece(1, D_MODEL // 2, 2 * D_FF, 1),
                      piece(2, D_FF, D_MODEL, 0),
                      by_expert(1, D_FF), by_expert(1, D_FF), by_expert(1, D_MODEL),
                      pl.BlockSpec((2 * LANE, 2 * LANE), lambda i, *_: (0, 0))],
            out_specs=pl.BlockSpec((MOE_BM, ROW_WORDS), lambda i, *_: (i, 0)),
            scratch_shapes=[pltpu.VMEM((2, D_MODEL, D_FF), BF16), pltpu.VMEM((2, D_MODEL, D_FF), BF16),
                            pltpu.VMEM((2, D_FF, D_MODEL), BF16)],
        ),
        out_shape=jax.ShapeDtypeStruct(xs.shape, U32),
        compiler_params=_cparams(("arbitrary",)),
        name="experts",
    )(blk_e, blk_rows, n_used, slot_e, *pieces, xs, w1, w1, w2, b1g, b1l, b2, sel)


def _combine_kernel(cnt_ref, off_ref, dest_ref, tot_ref, lp_ref, g_ref, h_ref, yb_ref, y_ref, rows2, sems, *, tile0):
    i = pl.program_id(0)
    runs = (cnt_ref, off_ref, dest_ref, tot_ref)
    slot = i % 2
    rows = rows2.at[slot]

    def fetch(step, into):
        _start_runs(runs, step + tile0, rows2.at[into], yb_ref, sems.at[into], to_global=False)

    @pl.when(i == 0)
    def _():
        rows2[...] = jnp.zeros_like(rows2)
        fetch(0, 0)

    pl.when(i + 1 < pl.num_programs(0))(lambda: fetch(i + 1, 1 - slot))
    _wait_runs(runs, i + tile0, rows, yb_ref, sems.at[slot])
    lp, g = lp_ref[...].astype(I16), g_ref[...].astype(BF16)
    r = lax.broadcasted_iota(I32, (TOK_TILE, SORT_ROWS), 1).astype(I16)
    mix = jnp.zeros((TOK_TILE, SORT_ROWS), BF16)
    for k in range(TOP_K):
        mix = jnp.where(r == lp[:, k:k + 1], g[:, k:k + 1], mix)
    y_lo, y_hi = _unpack_rows(rows[...])
    y_ref[...] = h_ref[...] + jnp.concatenate([_dot(mix, y_lo), _dot(mix, y_hi)], axis=-1)


def _combine(runs, lp_t, gate_t, h, yb, tile0):
    T = h.shape[0]
    tm = TOK_TILE
    tok = lambda w: pl.BlockSpec((tm, w), lambda i, *_: (i, 0))
    return pl.pallas_call(
        functools.partial(_combine_kernel, tile0=tile0),
        grid_spec=pltpu.PrefetchScalarGridSpec(
            num_scalar_prefetch=4,
            grid=(T // tm,),
            in_specs=[tok(TOP_K), tok(TOP_K), tok(D_MODEL), pl.BlockSpec(memory_space=pl.ANY)],
            out_specs=tok(D_MODEL),
            scratch_shapes=[pltpu.VMEM((2, SORT_ROWS, ROW_WORDS), U32), pltpu.SemaphoreType.DMA((2,))],
        ),
        out_shape=jax.ShapeDtypeStruct((T, D_MODEL), F32),
        compiler_params=_cparams(("arbitrary",)),
        name="combine",
    )(*runs, lp_t, gate_t, h, yb)


def _block_ones(width, block):
    idx = np.arange(width) // block
    return jnp.asarray(idx[:, None] == idx[None, :], BF16)


def _layer(x_p, x_s, cache_k, cache_v, s_gla, norm1, w_in, q_norm, k_norm, sinks, w_gk, b_gk,
           gla_norm, w_a, w_b, w_o, norm2, w_router, b_router, w1, b1, w2, b2):
    Bp, Lp, _ = x_p.shape
    Bs, Ls, _ = x_s.shape
    Tp, Ts = Bp * Lp, Bs * Ls

    w_packed = _pack_w_in(w_in)
    wgk = jnp.pad(w_gk, ((0, GL_PAD - GLA_GATE_RANK), (0, 0))).astype(BF16)
    n1 = norm1.reshape(1, D_MODEL)
    gq, gk = _block_ones(A_WIDTH, SWA_HEAD_DIM), _block_ones(KV_WIDTH, SWA_HEAD_DIM)
    qn = jnp.tile(q_norm, SWA_HEADS).reshape(1, A_WIDTH)
    kn = jnp.tile(k_norm, SWA_KV_HEADS).reshape(1, KV_WIDTH)
    bgk = b_gk.reshape(1, GLA_K_WIDTH)
    gn = gla_norm.reshape(1, GLA_DV)
    wa, wb, wo = w_a.astype(BF16), w_b.astype(BF16), w_o.astype(BF16)
    n2 = norm2.reshape(1, D_MODEL)
    wr = jnp.pad(w_router, ((0, 0), (0, LANE - N_EXPERTS))).astype(BF16)
    br = jnp.pad(b_router, (0, LANE - N_EXPERTS)).reshape(1, LANE)
    b1g = b1[:, 0::2].reshape(N_EXPERTS, 1, D_FF)
    b1l = b1[:, 1::2].reshape(N_EXPERTS, 1, D_FF)
    b2r = b2.reshape(N_EXPERTS, 1, D_MODEL)
    sel_np = np.zeros((2 * LANE, 2 * LANE), np.float32)
    sel_np[2 * np.arange(LANE), np.arange(LANE)] = 1.0
    sel_np[2 * np.arange(LANE) + 1, LANE + np.arange(LANE)] = 1.0
    sel = jnp.asarray(sel_np, BF16)

    def mix(x, k_past, v_past, s0):
        B, L, _ = x.shape
        T = B * L
        x2 = x.reshape(T, D_MODEL)
        r3 = lambda t: t.reshape(B, L, t.shape[-1])
        new_cache = None
        if k_past is None and L % TOK_TILE == 0:
            oa, ka, va, qb, kb, vb, la, rb, ga, gb = _inproj_swa(sinks, x2, L, n1, w_packed, gq, gk, qn, kn, wgk, bgk)
            ob, s_out = _gla(r3(qb), r3(kb), r3(vb), r3(la), r3(rb), gn, s0)
        else:
            qa, ka, va, qb, kb, vb, la, rb, ga, gb = _inproj(x2, n1, w_packed, gq, gk, qn, kn, wgk, bgk)
            if k_past is not None and L < min(WINDOW, GLA_CHUNK):
                fm = lambda c: jnp.transpose(c, (0, 2, 3, 1)).reshape(B, KV_WIDTH, win)
                oa, ob, s_out, k_fm, v_fm = _short_mixers(sinks, r3(qa), r3(ka), r3(va), fm(k_past), fm(v_past),
                                                          r3(qb), r3(kb), r3(vb), r3(la), r3(rb), gn, s0)
                unfm = lambda c: jnp.transpose(c.reshape(B, SWA_KV_HEADS, SWA_HEAD_DIM, win), (0, 3, 1, 2))
                new_cache = (unfm(k_fm), unfm(v_fm))
            else:
                if k_past is not None:
                    k_past, v_past = k_past.reshape(B, win, KV_WIDTH), v_past.reshape(B, win, KV_WIDTH)
                oa = _swa(sinks, r3(qa), r3(ka), r3(va), k_past, v_past, has_past=k_past is not None)
                ob, s_out = _gla(r3(qb), r3(kb), r3(vb), r3(la), r3(rb), gn, s0)
        h, hn, gate, lp, cnt = _outproj(x2, oa.reshape(T, A_WIDTH), ob.reshape(T, B_WIDTH), ga, gb,
                                        wa, wb, wo, n2, wr, br, upper, lower)
        return (r3(ka), r3(va), new_cache), s_out, h, hn, gate, lp, cnt[:, 0].reshape(T // TOK_TILE, N_EXPERTS)

    upper = jnp.asarray(np.arange(TOK_TILE)[:, None] < np.arange(TOK_TILE)[None, :], BF16)
    lower = jnp.asarray(np.arange(N_EXPERTS)[:, None] > np.arange(N_EXPERTS)[None, :], BF16)
    win = cache_k.shape[1]
    assert win == WINDOW
    (ka_p, va_p, _), s_p, h_p, hn_p, g_p, lp_p, cnt_p = mix(x_p, None, None, jnp.zeros((Bp, GLA_HEADS, GLA_DK, GLA_DV), F32))
    (ka_s, va_s, cache_s), s_s, h_s, hn_s, g_s, lp_s, cnt_s = mix(x_s, cache_k, cache_v, s_gla)

    T = Tp + Ts
    cnt = jnp.concatenate([cnt_p, cnt_s], axis=0)
    most_rows = T * TOP_K + cnt.size * (RUN_ALIGN - 1) + N_EXPERTS * (MOE_BM - 1)
    n_rows = -(-most_rows // MOE_BM) * MOE_BM
    counts = jnp.sum(cnt, axis=0)
    padded = (counts + MOE_BM - 1) // MOE_BM * MOE_BM
    pad_end = jnp.cumsum(padded)
    start = pad_end - padded
    dest = start[None, :] + jnp.cumsum(cnt, axis=0) - cnt
    off = jnp.cumsum(cnt, axis=1) - cnt
    flat = lambda a: a.reshape(-1).astype(I32)
    runs = (flat(cnt), flat(off), flat(dest), flat(jnp.sum(cnt, axis=1)))
    tails = jnp.where(padded > 0, pad_end - MOE_BM, -1)
    most_spare = (n_rows - T * TOP_K) // MOE_BM
    spare = pad_end[-1] + jnp.arange(most_spare, dtype=I32) * MOE_BM
    fill = jnp.concatenate([tails, jnp.where(spare < n_rows, spare, -1)]).astype(I32)
    blk_row = jnp.arange(n_rows // MOE_BM, dtype=I32) * MOE_BM
    blk_e = jnp.minimum(jnp.sum(pad_end[None, :] <= blk_row[:, None], axis=1), N_EXPERTS - 1).astype(I32)
    used_end = start + counts
    blk_rows = jnp.clip(jnp.sum(jnp.where(blk_e[:, None] == jnp.arange(N_EXPERTS)[None, :], used_end[None, :], 0),
                                axis=1) - blk_row, 0, MOE_BM)
    blk_rows = jnp.where(blk_row < pad_end[-1], blk_rows, 0).astype(I32)
    n_used = (pad_end[-1:] // MOE_BM).astype(I32)
    e_ids = jnp.arange(N_EXPERTS, dtype=I32)
    live = padded > 0
    blk_e = jnp.where(blk_row < pad_end[-1], blk_e, jnp.max(jnp.where(live, e_ids, 0)))
    later = jnp.where(live[None, :] & (e_ids[None, :] > e_ids[:, None]), e_ids[None, :], N_EXPERTS)
    nxt = jnp.min(later, axis=1)
    nxt = jnp.where(nxt < N_EXPERTS, nxt, e_ids)
    pick = lambda table: jnp.sum(jnp.where(blk_e[:, None] == e_ids[None, :], table[None, :], 0), axis=1)
    first_blk, next_first = pick(start // MOE_BM), pick(pad_end // MOE_BM)
    blk_i = jnp.arange(n_rows // MOE_BM, dtype=I32)
    pieces = tuple(jnp.where(blk_i >= jnp.maximum(next_first - lead, first_blk + 1), pick(nxt), blk_e).astype(I32)
                   for lead in EXPERT_PIECE_LEAD)
    slot_e = ((jnp.cumsum(live.astype(I32)) - 1) & 1).astype(I32)

    xs = _dispatch(fill, runs, jnp.concatenate([lp_p, lp_s], axis=1), hn_p, hn_s, n_rows)
    yb = _experts(blk_e, blk_rows, n_used, slot_e, pieces, xs, w1, b1g, b1l, w2, b2r, sel)
    y_p = _combine(runs, lp_p.T, g_p.T, h_p, yb, 0).reshape(Bp, Lp, D_MODEL)
    y_s = _combine(runs, lp_s.T, g_s.T, h_s, yb, Tp // TOK_TILE).reshape(Bs, Ls, D_MODEL)

    kv = lambda t: t.reshape(t.shape[0], t.shape[1], SWA_KV_HEADS, SWA_HEAD_DIM)
    k_p, v_p = kv(ka_p[:, -WINDOW:]), kv(va_p[:, -WINDOW:])
    if cache_s is not None:
        k_s, v_s = cache_s
    else:
        k_s = jnp.concatenate([cache_k, kv(ka_s)], axis=1)[:, -win:]
        v_s = jnp.concatenate([cache_v, kv(va_s)], axis=1)[:, -win:]
    return y_p, y_s, (k_p, v_p, s_p, k_s, v_s, s_s)


def kernel(x_prompt, x_sample, cache_swa_k, cache_swa_v, state_gla, norm1, w_in, q_norm, k_norm, sinks, w_gk, b_gk,
           gla_norm, w_a, w_b, w_o, norm2, w_router, b_router, w1, b1, w2, b2):
    depth = norm1.shape[0]
    y_p, y_s = x_prompt, x_sample
    states = []
    for l in range(depth):
        y_p, y_s, st = _layer(y_p, y_s, cache_swa_k[l], cache_swa_v[l], state_gla[l], norm1[l], w_in[l], q_norm[l],
                              k_norm[l], sinks[l], w_gk[l], b_gk[l], gla_norm[l], w_a[l], w_b[l], w_o[l], norm2[l],
                              w_router[l], b_router[l], w1[l], b1[l], w2[l], b2[l])
        states.append(st)
    return (y_p, y_s) + tuple(jnp.stack([st[j] for st in states]) for j in range(6))
```
